```python
import math
import jax, jax.numpy as jnp
from jax import lax
import numpy as np

D_MODEL = 1024
BATCH = 32
SEQ = 2048
DEPTH = 2

GRID_W = 64
CTX_LEN = 256
EPS = 1e-6

D_MIX = D_MODEL
A_WIDTH = D_MIX // 4
A_HEADS = 4
A_HEAD_DIM = A_WIDTH // A_HEADS
CHUNK = 128
B_WIDTH = D_MIX // 2
SSM_GROUP = 16
SSM_GROUPS = B_WIDTH // SSM_GROUP
SSM_STATE = 64
C_WIDTH = D_MIX - A_WIDTH - B_WIDTH
POOL_WINDOWS = (2, 4, 8, 16)
POOL_GROUP = C_WIDTH // len(POOL_WINDOWS)
D_IN = 2 * A_WIDTH + B_WIDTH + C_WIDTH
D_FF = ((-(-8 * D_MODEL // 3) + 255) // 256) * 256

kernel_name = "hybrid_gmlp_s5_pool_dit_prefix"


def rms_norm(x, g):
    xf = x.astype(jnp.float32)
    y = xf * lax.rsqrt(jnp.mean(xf * xf, axis=-1, keepdims=True) + EPS)
    return (y * g.astype(jnp.float32)).astype(x.dtype)


def layer_norm(x):
    xf = x.astype(jnp.float32)
    mu = jnp.mean(xf, axis=-1, keepdims=True)
    var = jnp.mean(jnp.square(xf - mu), axis=-1, keepdims=True)
    return ((xf - mu) * lax.rsqrt(var + EPS)).astype(x.dtype)


def modulate(h, shift, scale):
    return h * (1 + scale) + shift


def sincos_2d(rows, cols, dim):
    quarter = dim // 4
    omega = 1.0 / (10000.0 ** (jnp.arange(quarter, dtype=jnp.float32) / quarter))
    r = jnp.arange(rows, dtype=jnp.float32)[:, None] * omega
    cc = jnp.arange(cols, dtype=jnp.float32)[:, None] * omega
    er = jnp.concatenate([jnp.sin(r), jnp.cos(r)], axis=-1)
    ec = jnp.concatenate([jnp.sin(cc), jnp.cos(cc)], axis=-1)
    pe = jnp.concatenate([jnp.broadcast_to(er[:, None, :], (rows, cols, dim // 2)),
                          jnp.broadcast_to(ec[None, :, :], (rows, cols, dim // 2))], axis=-1)
    return pe.reshape(rows * cols, dim)


def spatial_gating(z, w_s, b_s):
    bsz, n, _ = z.shape
    z = jax.nn.gelu(z)
    u, v = jnp.split(z, 2, axis=-1)
    v = layer_norm(v.reshape(bsz, n // CHUNK, CHUNK, A_HEADS, A_HEAD_DIM))
    s = jnp.einsum('hpq,bkqhd->bkphd', w_s, v) + b_s.T[None, None, :, :, None]
    return u * s.reshape(bsz, n, A_WIDTH)


def ssm_discretize(lam_re, lam_im, log_dt, b_re, b_im):
    lam = lax.complex(lam_re.astype(jnp.float32), lam_im.astype(jnp.float32))
    dt = jnp.exp(log_dt.astype(jnp.float32))[:, None]
    lam_bar = jnp.exp(lam * dt)
    b = lax.complex(b_re.astype(jnp.float32), b_im.astype(jnp.float32))
    b_bar = ((lam_bar - 1.0) / lam)[..., None] * b
    return lam_bar, b_bar


def diag_scan(lam_bar, bu, h0, reverse):
    if h0 is not None:
        edge = bu.shape[1] - 1 if reverse else 0
        bu = bu.at[:, edge].add(lam_bar * h0)
    a = jnp.broadcast_to(lam_bar, bu.shape)

    def combine(e1, e2):
        a1, b1 = e1
        a2, b2 = e2
        return a1 * a2, a2 * b1 + b2

    _, h = lax.associative_scan(combine, (a, bu), reverse=reverse, axis=1)
    return h


def ssm_mixer(u_lat, u_ctx, lam_re, lam_im, log_dt, b_re, b_im, c_re, c_im, d, glu_w, glu_b, need_ctx):
    def groups(u):
        return u.reshape(u.shape[0], u.shape[1], SSM_GROUPS, SSM_GROUP).astype(jnp.float32)

    g_lat, g_ctx = groups(u_lat), groups(u_ctx)
    df = d.astype(jnp.float32)
    y_lat = df * g_lat
    y_ctx = df * g_ctx if need_ctx else None
    for k, reverse in enumerate((False, True)):
        lam_bar, b_bar = ssm_discretize(lam_re[k], lam_im[k], log_dt[k], b_re[k], b_im[k])
        cm = lax.complex(c_re[k].astype(jnp.float32), c_im[k].astype(jnp.float32))
        bu_ctx = jnp.einsum('blgh,gph->blgp', g_ctx.astype(jnp.complex64), b_bar)
        h_ctx = diag_scan(lam_bar, bu_ctx, None, reverse)
        h_end = h_ctx[:, 0] if reverse else h_ctx[:, -1]
        bu_lat = jnp.einsum('blgh,gph->blgp', g_lat.astype(jnp.complex64), b_bar)
        h_lat = diag_scan(lam_bar, bu_lat, h_end, reverse)
        y_lat = y_lat + jnp.einsum('ghp,blgp->blgh', cm, h_lat).real
        if need_ctx:
            y_ctx = y_ctx + jnp.einsum('ghp,blgp->blgh', cm, h_ctx).real

    def glu(y, dtype):
        g = jax.nn.gelu(y.reshape(y.shape[0], y.shape[1], B_WIDTH)).astype(dtype)
        return g * jax.nn.sigmoid(g @ glu_w + glu_b)

    out_lat = glu(y_lat, u_lat.dtype)
    out_ctx = glu(y_ctx, u_ctx.dtype) if need_ctx else None
    return out_lat, out_ctx


def window_mean(x, w):
    n = x.shape[-2]
    cs = jnp.cumsum(x.astype(jnp.float32), axis=-2)
    cs = jnp.concatenate([jnp.zeros_like(cs[..., :1, :]), cs], axis=-2)
    t = np.arange(n)
    lo = np.clip(t - w // 2, 0, n)
    hi = np.clip(t - w // 2 + w, 0, n)
    cnt = (hi - lo).astype(np.float32)[:, None]
    return ((jnp.take(cs, hi, axis=-2) - jnp.take(cs, lo, axis=-2)) / cnt).astype(x.dtype)


def pool_mixer(p, pool_w, pool_scale, rows):
    bsz, n, _ = p.shape
    outs = []
    for i, w in enumerate(POOL_WINDOWS):
        pg = p[..., i * POOL_GROUP:(i + 1) * POOL_GROUP]
        if rows is None:
            m = window_mean(pg, w)
        else:
            m = window_mean(pg.reshape(bsz, rows, GRID_W, POOL_GROUP), w).reshape(bsz, n, POOL_GROUP)
        outs.append((m - pg) @ pool_w[i])
    return jnp.concatenate(outs, axis=-1) * pool_scale


def mixing_sublayer(h_lat, h_ctx, rows, need_ctx, w_in, w_out, sgu_w, sgu_b,
                    lam_re, lam_im, log_dt, b_re, b_im, c_re, c_im, d, glu_w, glu_b,
                    pool_w, pool_scale):
    b_lo, b_hi = 2 * A_WIDTH, 2 * A_WIDTH + B_WIDTH
    z_lat = h_lat @ w_in
    if need_ctx:
        z_ctx = h_ctx @ w_in
        u_ctx = z_ctx[..., b_lo:b_hi]
    else:
        u_ctx = h_ctx @ w_in[:, b_lo:b_hi]
    a_lat = spatial_gating(z_lat[..., :b_lo], sgu_w, sgu_b)
    s_lat, s_ctx = ssm_mixer(z_lat[..., b_lo:b_hi], u_ctx, lam_re, lam_im, log_dt, b_re, b_im,
                             c_re, c_im, d, glu_w, glu_b, need_ctx)
    p_lat = pool_mixer(z_lat[..., b_hi:], pool_w, pool_scale, rows)
    m_lat = jnp.concatenate([a_lat, s_lat, p_lat], axis=-1) @ w_out
    m_ctx = None
    if need_ctx:
        a_ctx = spatial_gating(z_ctx[..., :b_lo], sgu_w, sgu_b)
        p_ctx = pool_mixer(z_ctx[..., b_hi:], pool_w, pool_scale, None)
        m_ctx = jnp.concatenate([a_ctx, s_ctx, p_ctx], axis=-1) @ w_out
    return m_lat, m_ctx


def swiglu(h, w_gate, w_up, w_down):
    return (jax.nn.silu(h @ w_gate) * (h @ w_up)) @ w_down


def _fwd_setup_inputs(seed: int = 0) -> dict:
    key = jax.random.key(seed)
    ks = jax.random.split(key, 32)
    f32 = jnp.float32

    def nrm(k, shape, scale):
        return jax.random.normal(k, shape, f32) * scale

    lam_im0 = math.pi * jnp.arange(SSM_STATE, dtype=f32)
    return {
        "x": nrm(ks[0], (BATCH, SEQ, D_MODEL), 1.0),
        "c": nrm(ks[1], (BATCH, D_MODEL), 1.0),
        "ctx": nrm(ks[2], (BATCH, CTX_LEN, D_MODEL), 1.0),
        "c_ctx": nrm(ks[3], (D_MODEL,), 1.0),
        "w_mod": nrm(ks[4], (DEPTH, D_MODEL, 6 * D_MODEL), 0.5 * D_MODEL ** -0.5),
        "b_mod": nrm(ks[5], (DEPTH, 6 * D_MODEL), 0.02),
        "norm_mix_pre": 1.0 + nrm(ks[6], (DEPTH, D_MODEL), 0.1),
        "norm_mix_post": 1.0 + nrm(ks[7], (DEPTH, D_MODEL), 0.1),
        "norm_ffn_pre": 1.0 + nrm(ks[8], (DEPTH, D_MODEL), 0.1),
        "norm_ffn_post": 1.0 + nrm(ks[9], (DEPTH, D_MODEL), 0.1),
        "w_in": nrm(ks[10], (DEPTH, D_MODEL, D_IN), D_MODEL ** -0.5),
        "w_out": nrm(ks[11], (DEPTH, D_MIX, D_MODEL), D_MIX ** -0.5),
        "sgu_w": nrm(ks[12], (DEPTH, A_HEADS, CHUNK, CHUNK), CHUNK ** -0.5),
        "sgu_b": 1.0 + nrm(ks[13], (DEPTH, A_HEADS, CHUNK), 0.1),
        "ssm_lam_re": -0.5 + nrm(ks[14], (DEPTH, 2, SSM_GROUPS, SSM_STATE), 0.01),
        "ssm_lam_im": lam_im0 + nrm(ks[15], (DEPTH, 2, SSM_GROUPS, SSM_STATE), 0.01),
        "ssm_log_dt": jax.random.uniform(ks[16], (DEPTH, 2, SSM_GROUPS), f32,
                                         minval=math.log(1e-3), maxval=math.log(1e-1)),
        "ssm_b_re": nrm(ks[17], (DEPTH, 2, SSM_GROUPS, SSM_STATE, SSM_GROUP), (2 * SSM_GROUP) ** -0.5),
        "ssm_b_im": nrm(ks[18], (DEPTH, 2, SSM_GROUPS, SSM_STATE, SSM_GROUP), (2 * SSM_GROUP) ** -0.5),
        "ssm_c_re": nrm(ks[19], (DEPTH, 2, SSM_GROUPS, SSM_GROUP, SSM_STATE), SSM_STATE ** -0.5),
        "ssm_c_im": nrm(ks[20], (DEPTH, 2, SSM_GROUPS, SSM_GROUP, SSM_STATE), SSM_STATE ** -0.5),
        "ssm_d": nrm(ks[21], (DEPTH, SSM_GROUPS, SSM_GROUP), 1.0),
        "glu_w": nrm(ks[22], (DEPTH, B_WIDTH, B_WIDTH), B_WIDTH ** -0.5),
        "glu_b": nrm(ks[23], (DEPTH, B_WIDTH), 0.02),
        "pool_w": nrm(ks[24], (DEPTH, len(POOL_WINDOWS), POOL_GROUP, POOL_GROUP), POOL_GROUP ** -0.5),
        "pool_scale": 1.0 + nrm(ks[25], (DEPTH, C_WIDTH), 0.1),
        "ffn_w_gate": nrm(ks[26], (DEPTH, D_MODEL, D_FF), D_MODEL ** -0.5),
        "ffn_w_up": nrm(ks[27], (DEPTH, D_MODEL, D_FF), D_MODEL ** -0.5),
        "ffn_w_down": nrm(ks[28], (DEPTH, D_FF, D_MODEL), D_FF ** -0.5),
    }


def _fwd_reference(x, c, ctx, c_ctx, w_mod, b_mod, norm_mix_pre, norm_mix_post, norm_ffn_pre, norm_ffn_post,
              w_in, w_out, sgu_w, sgu_b, ssm_lam_re, ssm_lam_im, ssm_log_dt, ssm_b_re, ssm_b_im,
              ssm_c_re, ssm_c_im, ssm_d, glu_w, glu_b, pool_w, pool_scale,
              ffn_w_gate, ffn_w_up, ffn_w_down):
    n_lat = x.shape[1]
    ROWS = n_lat // GRID_W
    x_lat = x + sincos_2d(ROWS, GRID_W, x.shape[-1]).astype(x.dtype)[None]
    x_ctx = ctx
    for i in range(DEPTH):
        need_ctx = i < DEPTH - 1
        mod_lat = jax.nn.silu(c) @ w_mod[i] + b_mod[i]
        mod_ctx = jax.nn.silu(c_ctx) @ w_mod[i] + b_mod[i]
        sh1, sc1, g1, sh2, sc2, g2 = [m[:, None, :] for m in jnp.split(mod_lat, 6, axis=-1)]
        csh1, csc1, cg1, csh2, csc2, cg2 = jnp.split(mod_ctx, 6, axis=-1)

        h_lat = modulate(rms_norm(x_lat, norm_mix_pre[i]), sh1, sc1)
        h_ctx = modulate(rms_norm(x_ctx, norm_mix_pre[i]), csh1, csc1)
        m_lat, m_ctx = mixing_sublayer(h_lat, h_ctx, ROWS, need_ctx, w_in[i], w_out[i], sgu_w[i], sgu_b[i],
                                       ssm_lam_re[i], ssm_lam_im[i], ssm_log_dt[i], ssm_b_re[i], ssm_b_im[i],
                                       ssm_c_re[i], ssm_c_im[i], ssm_d[i], glu_w[i], glu_b[i],
                                       pool_w[i], pool_scale[i])
        x_lat = x_lat + g1 * rms_norm(m_lat, norm_mix_post[i])
        f_lat = swiglu(modulate(rms_norm(x_lat, norm_ffn_pre[i]), sh2, sc2),
                       ffn_w_gate[i], ffn_w_up[i], ffn_w_down[i])
        x_lat = x_lat + g2 * rms_norm(f_lat, norm_ffn_post[i])
        if need_ctx:
            x_ctx = x_ctx + cg1 * rms_norm(m_ctx, norm_mix_post[i])
            f_ctx = swiglu(modulate(rms_norm(x_ctx, norm_ffn_pre[i]), csh2, csc2),
                           ffn_w_gate[i], ffn_w_up[i], ffn_w_down[i])
            x_ctx = x_ctx + cg2 * rms_norm(f_ctx, norm_ffn_post[i])
    return x_lat


import jax as _jax
import jax.numpy as _jnp

TWIN_FORMAT = 'train_step'
FWD_PARAMS = ['x', 'c', 'ctx', 'c_ctx', 'w_mod', 'b_mod', 'norm_mix_pre', 'norm_mix_post', 'norm_ffn_pre', 'norm_ffn_post', 'w_in', 'w_out', 'sgu_w', 'sgu_b', 'ssm_lam_re', 'ssm_lam_im', 'ssm_log_dt', 'ssm_b_re', 'ssm_b_im', 'ssm_c_re', 'ssm_c_im', 'ssm_d', 'glu_w', 'glu_b', 'pool_w', 'pool_scale', 'ffn_w_gate', 'ffn_w_up', 'ffn_w_down']
TWIN_WEIGHTS = ['c_ctx', 'w_mod', 'b_mod', 'norm_mix_pre', 'norm_mix_post', 'norm_ffn_pre', 'norm_ffn_post', 'w_in', 'w_out', 'sgu_w', 'sgu_b', 'ssm_lam_re', 'ssm_lam_im', 'ssm_log_dt', 'ssm_b_re', 'ssm_b_im', 'ssm_c_re', 'ssm_c_im', 'ssm_d', 'glu_w', 'glu_b', 'pool_w', 'pool_scale', 'ffn_w_gate', 'ffn_w_up', 'ffn_w_down']
TWIN_DIFF_INPUT = 'x'
TWIN_INPUTS = ['x', 'c', 'ctx', 'c_ctx', 'w_mod', 'b_mod', 'norm_mix_pre', 'norm_mix_post', 'norm_ffn_pre', 'norm_ffn_post', 'w_in', 'w_out', 'sgu_w', 'sgu_b', 'ssm_lam_re', 'ssm_lam_im', 'ssm_log_dt', 'ssm_b_re', 'ssm_b_im', 'ssm_c_re', 'ssm_c_im', 'ssm_d', 'glu_w', 'glu_b', 'pool_w', 'pool_scale', 'ffn_w_gate', 'ffn_w_up', 'ffn_w_down', 'loss_target', 'm_c_ctx', 'm_w_mod', 'm_b_mod', 'm_norm_mix_pre', 'm_norm_mix_post', 'm_norm_ffn_pre', 'm_norm_ffn_post', 'm_w_in', 'm_w_out', 'm_sgu_w', 'm_sgu_b', 'm_ssm_lam_re', 'm_ssm_lam_im', 'm_ssm_log_dt', 'm_ssm_b_re', 'm_ssm_b_im', 'm_ssm_c_re', 'm_ssm_c_im', 'm_ssm_d', 'm_glu_w', 'm_glu_b', 'm_pool_w', 'm_pool_scale', 'm_ffn_w_gate', 'm_ffn_w_up', 'm_ffn_w_down', 'v_c_ctx', 'v_w_mod', 'v_b_mod', 'v_norm_mix_pre', 'v_norm_mix_post', 'v_norm_ffn_pre', 'v_norm_ffn_post', 'v_w_in', 'v_w_out', 'v_sgu_w', 'v_sgu_b', 'v_ssm_lam_re', 'v_ssm_lam_im', 'v_ssm_log_dt', 'v_ssm_b_re', 'v_ssm_b_im', 'v_ssm_c_re', 'v_ssm_c_im', 'v_ssm_d', 'v_glu_w', 'v_glu_b', 'v_pool_w', 'v_pool_scale', 'v_ffn_w_gate', 'v_ffn_w_up', 'v_ffn_w_down']
TWIN_OUTPUTS = ['loss', 'grad_x', 'grad_c_ctx', 'grad_w_mod', 'grad_b_mod', 'grad_norm_mix_pre', 'grad_norm_mix_post', 'grad_norm_ffn_pre', 'grad_norm_ffn_post', 'grad_w_in', 'grad_w_out', 'grad_sgu_w', 'grad_sgu_b', 'grad_ssm_lam_re', 'grad_ssm_lam_im', 'grad_ssm_log_dt', 'grad_ssm_b_re', 'grad_ssm_b_im', 'grad_ssm_c_re', 'grad_ssm_c_im', 'grad_ssm_d', 'grad_glu_w', 'grad_glu_b', 'grad_pool_w', 'grad_pool_scale', 'grad_ffn_w_gate', 'grad_ffn_w_up', 'grad_ffn_w_down', 'delta_c_ctx', 'delta_w_mod', 'delta_b_mod', 'delta_norm_mix_pre', 'delta_norm_mix_post', 'delta_norm_ffn_pre', 'delta_norm_ffn_post', 'delta_w_in', 'delta_w_out', 'delta_sgu_w', 'delta_sgu_b', 'delta_ssm_lam_re', 'delta_ssm_lam_im', 'delta_ssm_log_dt', 'delta_ssm_b_re', 'delta_ssm_b_im', 'delta_ssm_c_re', 'delta_ssm_c_im', 'delta_ssm_d', 'delta_glu_w', 'delta_glu_b', 'delta_pool_w', 'delta_pool_scale', 'delta_ffn_w_gate', 'delta_ffn_w_up', 'delta_ffn_w_down', 'new_m_c_ctx', 'new_m_w_mod', 'new_m_b_mod', 'new_m_norm_mix_pre', 'new_m_norm_mix_post', 'new_m_norm_ffn_pre', 'new_m_norm_ffn_post', 'new_m_w_in', 'new_m_w_out', 'new_m_sgu_w', 'new_m_sgu_b', 'new_m_ssm_lam_re', 'new_m_ssm_lam_im', 'new_m_ssm_log_dt', 'new_m_ssm_b_re', 'new_m_ssm_b_im', 'new_m_ssm_c_re', 'new_m_ssm_c_im', 'new_m_ssm_d', 'new_m_glu_w', 'new_m_glu_b', 'new_m_pool_w', 'new_m_pool_scale', 'new_m_ffn_w_gate', 'new_m_ffn_w_up', 'new_m_ffn_w_down', 'new_v_c_ctx', 'new_v_w_mod', 'new_v_b_mod', 'new_v_norm_mix_pre', 'new_v_norm_mix_post', 'new_v_norm_ffn_pre', 'new_v_norm_ffn_post', 'new_v_w_in', 'new_v_w_out', 'new_v_sgu_w', 'new_v_sgu_b', 'new_v_ssm_lam_re', 'new_v_ssm_lam_im', 'new_v_ssm_log_dt', 'new_v_ssm_b_re', 'new_v_ssm_b_im', 'new_v_ssm_c_re', 'new_v_ssm_c_im', 'new_v_ssm_d', 'new_v_glu_w', 'new_v_glu_b', 'new_v_pool_w', 'new_v_pool_scale', 'new_v_ffn_w_gate', 'new_v_ffn_w_up', 'new_v_ffn_w_down']
TWIN_LEAF_KINDS = {'loss': 'loss', 'grad_x': 'grad_x', 'grad_c_ctx': 'grad_w', 'grad_w_mod': 'grad_w', 'grad_b_mod': 'grad_w', 'grad_norm_mix_pre': 'grad_w', 'grad_norm_mix_post': 'grad_w', 'grad_norm_ffn_pre': 'grad_w', 'grad_norm_ffn_post': 'grad_w', 'grad_w_in': 'grad_w', 'grad_w_out': 'grad_w', 'grad_sgu_w': 'grad_w', 'grad_sgu_b': 'grad_w', 'grad_ssm_lam_re': 'grad_w', 'grad_ssm_lam_im': 'grad_w', 'grad_ssm_log_dt': 'grad_w', 'grad_ssm_b_re': 'grad_w', 'grad_ssm_b_im': 'grad_w', 'grad_ssm_c_re': 'grad_w', 'grad_ssm_c_im': 'grad_w', 'grad_ssm_d': 'grad_w', 'grad_glu_w': 'grad_w', 'grad_glu_b': 'grad_w', 'grad_pool_w': 'grad_w', 'grad_pool_scale': 'grad_w', 'grad_ffn_w_gate': 'grad_w', 'grad_ffn_w_up': 'grad_w', 'grad_ffn_w_down': 'grad_w', 'delta_c_ctx': 'delta_w', 'delta_w_mod': 'delta_w', 'delta_b_mod': 'delta_w', 'delta_norm_mix_pre': 'delta_w', 'delta_norm_mix_post': 'delta_w', 'delta_norm_ffn_pre': 'delta_w', 'delta_norm_ffn_post': 'delta_w', 'delta_w_in': 'delta_w', 'delta_w_out': 'delta_w', 'delta_sgu_w': 'delta_w', 'delta_sgu_b': 'delta_w', 'delta_ssm_lam_re': 'delta_w', 'delta_ssm_lam_im': 'delta_w', 'delta_ssm_log_dt': 'delta_w', 'delta_ssm_b_re': 'delta_w', 'delta_ssm_b_im': 'delta_w', 'delta_ssm_c_re': 'delta_w', 'delta_ssm_c_im': 'delta_w', 'delta_ssm_d': 'delta_w', 'delta_glu_w': 'delta_w', 'delta_glu_b': 'delta_w', 'delta_pool_w': 'delta_w', 'delta_pool_scale': 'delta_w', 'delta_ffn_w_gate': 'delta_w', 'delta_ffn_w_up': 'delta_w', 'delta_ffn_w_down': 'delta_w', 'new_m_c_ctx': 'new_m', 'new_m_w_mod': 'new_m', 'new_m_b_mod': 'new_m', 'new_m_norm_mix_pre': 'new_m', 'new_m_norm_mix_post': 'new_m', 'new_m_norm_ffn_pre': 'new_m', 'new_m_norm_ffn_post': 'new_m', 'new_m_w_in': 'new_m', 'new_m_w_out': 'new_m', 'new_m_sgu_w': 'new_m', 'new_m_sgu_b': 'new_m', 'new_m_ssm_lam_re': 'new_m', 'new_m_ssm_lam_im': 'new_m', 'new_m_ssm_log_dt': 'new_m', 'new_m_ssm_b_re': 'new_m', 'new_m_ssm_b_im': 'new_m', 'new_m_ssm_c_re': 'new_m', 'new_m_ssm_c_im': 'new_m', 'new_m_ssm_d': 'new_m', 'new_m_glu_w': 'new_m', 'new_m_glu_b': 'new_m', 'new_m_pool_w': 'new_m', 'new_m_pool_scale': 'new_m', 'new_m_ffn_w_gate': 'new_m', 'new_m_ffn_w_up': 'new_m', 'new_m_ffn_w_down': 'new_m', 'new_v_c_ctx': 'new_v', 'new_v_w_mod': 'new_v', 'new_v_b_mod': 'new_v', 'new_v_norm_mix_pre': 'new_v', 'new_v_norm_mix_post': 'new_v', 'new_v_norm_ffn_pre': 'new_v', 'new_v_norm_ffn_post': 'new_v', 'new_v_w_in': 'new_v', 'new_v_w_out': 'new_v', 'new_v_sgu_w': 'new_v', 'new_v_sgu_b': 'new_v', 'new_v_ssm_lam_re': 'new_v', 'new_v_ssm_lam_im': 'new_v', 'new_v_ssm_log_dt': 'new_v', 'new_v_ssm_b_re': 'new_v', 'new_v_ssm_b_im': 'new_v', 'new_v_ssm_c_re': 'new_v', 'new_v_ssm_c_im': 'new_v', 'new_v_ssm_d': 'new_v', 'new_v_glu_w': 'new_v', 'new_v_glu_b': 'new_v', 'new_v_pool_w': 'new_v', 'new_v_pool_scale': 'new_v', 'new_v_ffn_w_gate': 'new_v', 'new_v_ffn_w_up': 'new_v', 'new_v_ffn_w_down': 'new_v'}


def _forward(args):
    return _fwd_reference(*[args[k] for k in FWD_PARAMS])


def _output_shape():
    out = _jax.eval_shape(lambda: _forward(_fwd_setup_inputs(0)))
    return out.shape, out.dtype

N_MICROBATCH = 1
ADAM_LR = 0.001
ADAM_B1 = 0.9
ADAM_B2 = 0.999
ADAM_EPS = 1e-08
ADAM_WD = 0.01
ADAM_STEP = 10
PER_EXAMPLE_BATCH_AXIS = {'x': 0, 'c': 0, 'ctx': 0, 'loss_target': 0}
SHARED_INPUTS = []
_WEIGHT_DTYPES = {'c_ctx': _jnp.float32, 'w_mod': _jnp.float32, 'b_mod': _jnp.float32, 'norm_mix_pre': _jnp.float32, 'norm_mix_post': _jnp.float32, 'norm_ffn_pre': _jnp.float32, 'norm_ffn_post': _jnp.float32, 'w_in': _jnp.float32, 'w_out': _jnp.float32, 'sgu_w': _jnp.float32, 'sgu_b': _jnp.float32, 'ssm_lam_re': _jnp.float32, 'ssm_lam_im': _jnp.float32, 'ssm_log_dt': _jnp.float32, 'ssm_b_re': _jnp.float32, 'ssm_b_im': _jnp.float32, 'ssm_c_re': _jnp.float32, 'ssm_c_im': _jnp.float32, 'ssm_d': _jnp.float32, 'glu_w': _jnp.float32, 'glu_b': _jnp.float32, 'pool_w': _jnp.float32, 'pool_scale': _jnp.float32, 'ffn_w_gate': _jnp.float32, 'ffn_w_up': _jnp.float32, 'ffn_w_down': _jnp.float32}
MOMENT_SCALE = {'c_ctx': 1.236926e-02, 'w_mod': 4.755088e+00, 'b_mod': 9.977062e+00, 'norm_mix_pre': 1.417603e+00, 'norm_mix_post': 8.463277e+00, 'norm_ffn_pre': 1.854811e+00, 'norm_ffn_post': 8.099633e+00, 'w_in': 1.405328e+00, 'w_out': 2.701162e+00, 'sgu_w': 1.409296e-01, 'sgu_b': 2.511962e-01, 'ssm_lam_re': 2.537750e-01, 'ssm_lam_im': 2.906269e-01, 'ssm_log_dt': 2.376927e+00, 'ssm_b_re': 1.558232e-01, 'ssm_b_im': 1.707043e-01, 'ssm_c_re': 2.377697e-01, 'ssm_c_im': 2.572838e-01, 'ssm_d': 2.593394e+00, 'glu_w': 5.994630e-01, 'glu_b': 1.109246e+00, 'pool_w': 2.749924e-01, 'pool_scale': 3.050516e-01, 'ffn_w_gate': 8.654140e-01, 'ffn_w_up': 1.105990e+00, 'ffn_w_down': 1.819403e+00}


def _to_microbatches(a, axis):
    t = _jnp.moveaxis(a, axis, 0)
    t = t.reshape((N_MICROBATCH, t.shape[0] // N_MICROBATCH) + t.shape[1:])
    return _jnp.moveaxis(t, 1, axis + 1)


def setup_inputs(seed: int = 0) -> dict:
    inp = _fwd_setup_inputs(seed)
    key = _jax.random.fold_in(_jax.random.key(seed), 7919)
    shape, _ = _output_shape()
    out = dict(inp)
    out["loss_target"] = _jax.random.normal(_jax.random.fold_in(key, 0), shape, _jnp.float32)
    for i, name in enumerate(TWIN_WEIGHTS):
        w = inp[name].astype(_jnp.float32)
        if MOMENT_SCALE is None:
            s = _jnp.sqrt(_jnp.mean(_jnp.square(w)) + 1e-30)
        else:
            s = MOMENT_SCALE[name]
        km, kv = _jax.random.split(_jax.random.fold_in(key, i + 1))
        out[name] = w
        out["m_" + name] = s * _jax.random.normal(km, w.shape, _jnp.float32)
        out["v_" + name] = (s * s) * _jax.random.uniform(kv, w.shape, _jnp.float32, 0.5, 1.5)
    if N_MICROBATCH > 1:
        for name, axis in PER_EXAMPLE_BATCH_AXIS.items():
            out[name] = _to_microbatches(out[name], axis)
    return {'x': out['x'], 'c': out['c'], 'ctx': out['ctx'], 'c_ctx': out['c_ctx'], 'w_mod': out['w_mod'], 'b_mod': out['b_mod'], 'norm_mix_pre': out['norm_mix_pre'], 'norm_mix_post': out['norm_mix_post'], 'norm_ffn_pre': out['norm_ffn_pre'], 'norm_ffn_post': out['norm_ffn_post'], 'w_in': out['w_in'], 'w_out': out['w_out'], 'sgu_w': out['sgu_w'], 'sgu_b': out['sgu_b'], 'ssm_lam_re': out['ssm_lam_re'], 'ssm_lam_im': out['ssm_lam_im'], 'ssm_log_dt': out['ssm_log_dt'], 'ssm_b_re': out['ssm_b_re'], 'ssm_b_im': out['ssm_b_im'], 'ssm_c_re': out['ssm_c_re'], 'ssm_c_im': out['ssm_c_im'], 'ssm_d': out['ssm_d'], 'glu_w': out['glu_w'], 'glu_b': out['glu_b'], 'pool_w': out['pool_w'], 'pool_scale': out['pool_scale'], 'ffn_w_gate': out['ffn_w_gate'], 'ffn_w_up': out['ffn_w_up'], 'ffn_w_down': out['ffn_w_down'], 'loss_target': out['loss_target'], 'm_c_ctx': out['m_c_ctx'], 'm_w_mod': out['m_w_mod'], 'm_b_mod': out['m_b_mod'], 'm_norm_mix_pre': out['m_norm_mix_pre'], 'm_norm_mix_post': out['m_norm_mix_post'], 'm_norm_ffn_pre': out['m_norm_ffn_pre'], 'm_norm_ffn_post': out['m_norm_ffn_post'], 'm_w_in': out['m_w_in'], 'm_w_out': out['m_w_out'], 'm_sgu_w': out['m_sgu_w'], 'm_sgu_b': out['m_sgu_b'], 'm_ssm_lam_re': out['m_ssm_lam_re'], 'm_ssm_lam_im': out['m_ssm_lam_im'], 'm_ssm_log_dt': out['m_ssm_log_dt'], 'm_ssm_b_re': out['m_ssm_b_re'], 'm_ssm_b_im': out['m_ssm_b_im'], 'm_ssm_c_re': out['m_ssm_c_re'], 'm_ssm_c_im': out['m_ssm_c_im'], 'm_ssm_d': out['m_ssm_d'], 'm_glu_w': out['m_glu_w'], 'm_glu_b': out['m_glu_b'], 'm_pool_w': out['m_pool_w'], 'm_pool_scale': out['m_pool_scale'], 'm_ffn_w_gate': out['m_ffn_w_gate'], 'm_ffn_w_up': out['m_ffn_w_up'], 'm_ffn_w_down': out['m_ffn_w_down'], 'v_c_ctx': out['v_c_ctx'], 'v_w_mod': out['v_w_mod'], 'v_b_mod': out['v_b_mod'], 'v_norm_mix_pre': out['v_norm_mix_pre'], 'v_norm_mix_post': out['v_norm_mix_post'], 'v_norm_ffn_pre': out['v_norm_ffn_pre'], 'v_norm_ffn_post': out['v_norm_ffn_post'], 'v_w_in': out['v_w_in'], 'v_w_out': out['v_w_out'], 'v_sgu_w': out['v_sgu_w'], 'v_sgu_b': out['v_sgu_b'], 'v_ssm_lam_re': out['v_ssm_lam_re'], 'v_ssm_lam_im': out['v_ssm_lam_im'], 'v_ssm_log_dt': out['v_ssm_log_dt'], 'v_ssm_b_re': out['v_ssm_b_re'], 'v_ssm_b_im': out['v_ssm_b_im'], 'v_ssm_c_re': out['v_ssm_c_re'], 'v_ssm_c_im': out['v_ssm_c_im'], 'v_ssm_d': out['v_ssm_d'], 'v_glu_w': out['v_glu_w'], 'v_glu_b': out['v_glu_b'], 'v_pool_w': out['v_pool_w'], 'v_pool_scale': out['v_pool_scale'], 'v_ffn_w_gate': out['v_ffn_w_gate'], 'v_ffn_w_up': out['v_ffn_w_up'], 'v_ffn_w_down': out['v_ffn_w_down']}


def _loss(weights, diff, rest, loss_target):
    with _jax.named_scope("forward"):
        args = {**rest, TWIN_DIFF_INPUT: diff, **{k: w.astype(_WEIGHT_DTYPES[k]) for k, w in weights.items()}}
        y = _forward(args)
    with _jax.named_scope("loss_head"):
        err = _jnp.square(y.astype(_jnp.float32) - loss_target)
        return 0.5 * _jnp.sum(_jnp.mean(err, axis=-1)) if err.ndim else 0.5 * err


def _adamw(w, g, m, v):
    m = ADAM_B1 * m + (1.0 - ADAM_B1) * g
    v = ADAM_B2 * v + (1.0 - ADAM_B2) * _jnp.square(g)
    m_hat = m / (1.0 - ADAM_B1 ** ADAM_STEP)
    v_hat = v / (1.0 - ADAM_B2 ** ADAM_STEP)
    delta = -ADAM_LR * (m_hat / (_jnp.sqrt(v_hat) + ADAM_EPS) + ADAM_WD * w)
    return delta, m, v


def reference(x, c, ctx, c_ctx, w_mod, b_mod, norm_mix_pre, norm_mix_post, norm_ffn_pre, norm_ffn_post, w_in, w_out, sgu_w, sgu_b, ssm_lam_re, ssm_lam_im, ssm_log_dt, ssm_b_re, ssm_b_im, ssm_c_re, ssm_c_im, ssm_d, glu_w, glu_b, pool_w, pool_scale, ffn_w_gate, ffn_w_up, ffn_w_down, loss_target, m_c_ctx, m_w_mod, m_b_mod, m_norm_mix_pre, m_norm_mix_post, m_norm_ffn_pre, m_norm_ffn_post, m_w_in, m_w_out, m_sgu_w, m_sgu_b, m_ssm_lam_re, m_ssm_lam_im, m_ssm_log_dt, m_ssm_b_re, m_ssm_b_im, m_ssm_c_re, m_ssm_c_im, m_ssm_d, m_glu_w, m_glu_b, m_pool_w, m_pool_scale, m_ffn_w_gate, m_ffn_w_up, m_ffn_w_down, v_c_ctx, v_w_mod, v_b_mod, v_norm_mix_pre, v_norm_mix_post, v_norm_ffn_pre, v_norm_ffn_post, v_w_in, v_w_out, v_sgu_w, v_sgu_b, v_ssm_lam_re, v_ssm_lam_im, v_ssm_log_dt, v_ssm_b_re, v_ssm_b_im, v_ssm_c_re, v_ssm_c_im, v_ssm_d, v_glu_w, v_glu_b, v_pool_w, v_pool_scale, v_ffn_w_gate, v_ffn_w_up, v_ffn_w_down):
    given = dict(x=x, c=c, ctx=ctx, c_ctx=c_ctx, w_mod=w_mod, b_mod=b_mod, norm_mix_pre=norm_mix_pre, norm_mix_post=norm_mix_post, norm_ffn_pre=norm_ffn_pre, norm_ffn_post=norm_ffn_post, w_in=w_in, w_out=w_out, sgu_w=sgu_w, sgu_b=sgu_b, ssm_lam_re=ssm_lam_re, ssm_lam_im=ssm_lam_im, ssm_log_dt=ssm_log_dt, ssm_b_re=ssm_b_re, ssm_b_im=ssm_b_im, ssm_c_re=ssm_c_re, ssm_c_im=ssm_c_im, ssm_d=ssm_d, glu_w=glu_w, glu_b=glu_b, pool_w=pool_w, pool_scale=pool_scale, ffn_w_gate=ffn_w_gate, ffn_w_up=ffn_w_up, ffn_w_down=ffn_w_down, loss_target=loss_target, m_c_ctx=m_c_ctx, m_w_mod=m_w_mod, m_b_mod=m_b_mod, m_norm_mix_pre=m_norm_mix_pre, m_norm_mix_post=m_norm_mix_post, m_norm_ffn_pre=m_norm_ffn_pre, m_norm_ffn_post=m_norm_ffn_post, m_w_in=m_w_in, m_w_out=m_w_out, m_sgu_w=m_sgu_w, m_sgu_b=m_sgu_b, m_ssm_lam_re=m_ssm_lam_re, m_ssm_lam_im=m_ssm_lam_im, m_ssm_log_dt=m_ssm_log_dt, m_ssm_b_re=m_ssm_b_re, m_ssm_b_im=m_ssm_b_im, m_ssm_c_re=m_ssm_c_re, m_ssm_c_im=m_ssm_c_im, m_ssm_d=m_ssm_d, m_glu_w=m_glu_w, m_glu_b=m_glu_b, m_pool_w=m_pool_w, m_pool_scale=m_pool_scale, m_ffn_w_gate=m_ffn_w_gate, m_ffn_w_up=m_ffn_w_up, m_ffn_w_down=m_ffn_w_down, v_c_ctx=v_c_ctx, v_w_mod=v_w_mod, v_b_mod=v_b_mod, v_norm_mix_pre=v_norm_mix_pre, v_norm_mix_post=v_norm_mix_post, v_norm_ffn_pre=v_norm_ffn_pre, v_norm_ffn_post=v_norm_ffn_post, v_w_in=v_w_in, v_w_out=v_w_out, v_sgu_w=v_sgu_w, v_sgu_b=v_sgu_b, v_ssm_lam_re=v_ssm_lam_re, v_ssm_lam_im=v_ssm_lam_im, v_ssm_log_dt=v_ssm_log_dt, v_ssm_b_re=v_ssm_b_re, v_ssm_b_im=v_ssm_b_im, v_ssm_c_re=v_ssm_c_re, v_ssm_c_im=v_ssm_c_im, v_ssm_d=v_ssm_d, v_glu_w=v_glu_w, v_glu_b=v_glu_b, v_pool_w=v_pool_w, v_pool_scale=v_pool_scale, v_ffn_w_gate=v_ffn_w_gate, v_ffn_w_up=v_ffn_w_up, v_ffn_w_down=v_ffn_w_down)
    weights = {n: given[n] for n in TWIN_WEIGHTS}
    shared = {n: given[n] for n in SHARED_INPUTS}
    per_example = {n: given[n] for n in ['x', 'c', 'ctx']}
    grad_fn = _jax.value_and_grad(_loss, argnums=(0, 1))

    def one_microbatch(ex, loss_target):
        ex = dict(ex)
        diff = ex.pop(TWIN_DIFF_INPUT)
        return grad_fn(weights, diff, {**shared, **ex}, loss_target)

    if N_MICROBATCH == 1:
        loss, (grad_w, grad_x) = one_microbatch(per_example, given["loss_target"])
    else:
        def body(carry, xs):
            loss_sum, grad_sum = carry
            l_k, (gw_k, gx_k) = one_microbatch(xs[0], xs[1])
            with _jax.named_scope("update"):
                return (loss_sum + l_k, _jax.tree.map(_jnp.add, grad_sum, gw_k)), gx_k

        init = (_jnp.zeros((), _jnp.float32), _jax.tree.map(_jnp.zeros_like, weights))
        (loss, grad_w), grad_x = _jax.lax.scan(body, init, (per_example, given["loss_target"]))
    with _jax.named_scope("update"):
        delta_w, new_m, new_v = {}, {}, {}
        for n in TWIN_WEIGHTS:
            delta_w[n], new_m[n], new_v[n] = _adamw(weights[n], grad_w[n], given["m_" + n], given["v_" + n])
    return (loss, grad_x, *[grad_w[n] for n in TWIN_WEIGHTS], *[delta_w[n] for n in TWIN_WEIGHTS],
            *[new_m[n] for n in TWIN_WEIGHTS], *[new_v[n] for n in TWIN_WEIGHTS])
```

```python
import functools
import math

import numpy as np
import jax
import jax.numpy as jnp
from jax import lax
from jax.experimental import pallas as pl
from jax.experimental.pallas import tpu as pltpu

F32 = jnp.float32
BF16 = jnp.bfloat16
HI = lax.Precision.HIGHEST
MESH = pl.DeviceIdType.MESH

D = 1024
D_IN = 1280
D_FF = 2816
A_W = 256
B_W = 512
C_W = 256
A_HEADS = 4
CHUNK = 128
SSM_G = 32
SSM_H = 16
SSM_P = 64
GRID_W = 64
POOL_WINDOWS = (2, 4, 8, 16)
EPS = 1e-6
N_DEV = 8

TT = 256
TC = 8
ROW_W = TC * B_W
FF_CHUNK = 256
VMEM_LIMIT = 56 * 1024 * 1024

ADAM_LR = 0.001
ADAM_B1 = 0.9
ADAM_B2 = 0.999
ADAM_EPS = 1e-08
ADAM_WD = 0.01
ADAM_STEP = 10


def _cp(sem):
    return pltpu.CompilerParams(dimension_semantics=sem, vmem_limit_bytes=VMEM_LIMIT)


def dot_nn(a, b):
    return jnp.dot(a, b, preferred_element_type=F32)


def dot_nt(a, b):
    return lax.dot_general(a, b, (((1,), (1,)), ((), ())), preferred_element_type=F32)


def dot_tn(a, b):
    return lax.dot_general(a, b, (((0,), (0,)), ((), ())), preferred_element_type=F32)


def split_bf16(x):
    hi = x.astype(BF16)
    lo = (x - hi.astype(F32)).astype(BF16)
    return hi, lo


def gelu(x):
    return jax.nn.gelu(x)


def gelu_grad(x):
    c = math.sqrt(2.0 / math.pi)
    t = jnp.tanh(c * (x + 0.044715 * x * x * x))
    return 0.5 * (1.0 + t) + 0.5 * x * (1.0 - t * t) * c * (1.0 + 3.0 * 0.044715 * x * x)


def rms_stats(x):
    r = lax.rsqrt(jnp.mean(x * x, axis=-1, keepdims=True) + EPS)
    return r, x * r


def rms_bwd(r, xn, dxn):
    return r * (dxn - xn * jnp.mean(dxn * xn, axis=-1, keepdims=True))


def colsum(x):
    return jnp.sum(x, axis=0, keepdims=True)


def lane_group(width, group):
    return lax.broadcasted_iota(jnp.int32, (1, width), 1) // group


def _tile_spec(width):
    return pl.BlockSpec((None, TT, width), lambda b, j: (b, j, 0))


def _mod_spec(nc):
    return pl.BlockSpec((None, None, 8, D), lambda b, j: (b, jnp.where(j >= nc, 1, 0), 0, 0))


def _full_spec(shape):
    zeros = (0,) * len(shape)
    return pl.BlockSpec(shape, lambda b, j: zeros)


def _kind_spec(shape, nc):
    zeros = (0,) * len(shape)
    return pl.BlockSpec((None,) + shape, lambda b, j: (jnp.where(j >= nc, 1, 0),) + zeros)


def _stat_spec():
    return pl.BlockSpec((None, None, 8, D), lambda b, j: (b, j, 0, 0))


def embed_tokens(x, ctx, pe):
    bl, seq, _ = x.shape
    nc = ctx.shape[1] // TT
    nt = nc + seq // TT

    def body(ctx_ref, x_ref, pe_ref, o_ref):
        j = pl.program_id(1)

        @pl.when(j < nc)
        def _():
            o_ref[...] = ctx_ref[...]

        @pl.when(j >= nc)
        def _():
            o_ref[...] = x_ref[...] + pe_ref[...]

    return pl.pallas_call(
        body, name="embed_tokens", grid=(bl, nt),
        in_specs=[pl.BlockSpec((None, TT, D), lambda b, j: (b, jnp.minimum(j, nc - 1), 0)),
                  pl.BlockSpec((None, TT, D), lambda b, j: (b, jnp.maximum(j - nc, 0), 0)),
                  pl.BlockSpec((TT, D), lambda b, j: (jnp.maximum(j - nc, 0), 0))],
        out_specs=_tile_spec(D),
        out_shape=jax.ShapeDtypeStruct((bl, nt * TT, D), F32),
        compiler_params=_cp(("arbitrary", "arbitrary")),
    )(ctx, x, pe)


def pre_mix(xs, mod, n1, w_int, nc):
    bl, s, _ = xs.shape

    def body(x_ref, mod_ref, n_ref, w_ref, za_ref, zu_ref, zp_ref):
        r, xn = rms_stats(x_ref[...])
        h = xn * n_ref[...] * (1.0 + mod_ref[1:2, :]) + mod_ref[0:1, :]
        z = dot_nt(h.astype(BF16), w_ref[...])
        za_ref[...] = z[:, :2 * A_W]
        zu_ref[...] = z[:, 2 * A_W:2 * A_W + B_W]
        zp_ref[...] = z[:, 2 * A_W + B_W:]

    return pl.pallas_call(
        body, name="pre_mix", grid=(bl, s // TT),
        in_specs=[_tile_spec(D), _mod_spec(nc), _full_spec((1, D)), _full_spec((D_IN, D))],
        out_specs=[_tile_spec(2 * A_W), _tile_spec(B_W), _tile_spec(C_W)],
        out_shape=[jax.ShapeDtypeStruct((bl, s, 2 * A_W), F32), jax.ShapeDtypeStruct((bl, s, B_W), F32),
                   jax.ShapeDtypeStruct((bl, s, C_W), F32)],
        compiler_params=_cp(("arbitrary", "arbitrary")),
    )(xs, mod, n1, w_int)


def _seg_mean(x, seg_p):
    hi, lo = split_bf16(x)
    return dot_nn(hi, seg_p) + dot_nn(lo, seg_p)


def _sgu_forward(za, sw_ref, sbias, seg_p):
    ge = gelu(za)
    u, v = ge[:, :A_W], ge[:, A_W:]
    dv = v - _seg_mean(v, seg_p)
    rs = lax.rsqrt(_seg_mean(dv * dv, seg_p) + EPS)
    vn = dv * rs
    head = lane_group(A_W, A_W // A_HEADS)
    parts = []
    for c2 in range(TT // CHUNK):
        vb = vn[c2 * CHUNK:(c2 + 1) * CHUNK].astype(BF16)
        sc = sbias
        for h in range(A_HEADS):
            sc = sc + jnp.where(head == h, dot_nn(sw_ref[h], vb), 0.0)
        parts.append(sc)
    sg = jnp.concatenate(parts, axis=0)
    return u * sg, (u, vn, rs, sg)


def _pool_forward(zp, band_ref, icnt, wbd, pscale):
    hi, lo = split_bf16(zp)
    grp = lane_group(C_W, C_W // len(POOL_WINDOWS))
    q = jnp.zeros_like(zp)
    for i in range(len(POOL_WINDOWS)):
        t = dot_nn(band_ref[i], hi) + dot_nn(band_ref[i], lo)
        q = jnp.where(grp == i, t, q)
    q = q * icnt - zp
    o = dot_nn(q.astype(BF16), wbd)
    return o * pscale, (q, o)


def _glu_forward(y, glu_w, glu_b):
    g = gelu(y)
    sg = jax.nn.sigmoid(dot_nn(g.astype(BF16), glu_w) + glu_b)
    return g * sg, (g, sg)


_MIX_CONST_SHAPES = dict(sw=(A_HEADS, CHUNK, CHUNK), sbias=(CHUNK, A_W), seg_p=(A_W, A_W), wbd=(C_W, C_W),
                         pscale=(1, C_W), glu_w=(B_W, B_W), glu_b=(1, B_W), w_out=(D, D), n2=(1, D))


def _mix_const_specs(nc):
    return ([_full_spec(_MIX_CONST_SHAPES[k]) for k in ("sw", "sbias", "seg_p")]
            + [_kind_spec((len(POOL_WINDOWS), TT, TT), nc), _kind_spec((TT, C_W), nc)]
            + [_full_spec(_MIX_CONST_SHAPES[k]) for k in ("wbd", "pscale", "glu_w", "glu_b", "w_out", "n2")])


def _mix_const_args(cst):
    return [cst[k] for k in ("sw", "sbias", "seg_p", "band", "icnt", "wbd", "pscale", "glu_w", "glu_b", "w_out", "n2")]


def post_mix(xs, za, zp, ys, mod, cst, nc):
    bl, s, _ = xs.shape

    def body(x_ref, za_ref, zp_ref, y_ref, mod_ref, sw_ref, sbias_ref, seg_ref, band_ref, icnt_ref, wbd_ref,
             ps_ref, gw_ref, gb_ref, wo_ref, n2_ref, x1_ref, m_ref):
        a, _ = _sgu_forward(za_ref[...], sw_ref, sbias_ref[...], seg_ref[...])
        p, _ = _pool_forward(zp_ref[...], band_ref, icnt_ref[...], wbd_ref[...], ps_ref[...])
        sm, _ = _glu_forward(y_ref[...], gw_ref[...], gb_ref[...])
        cat = jnp.concatenate([a, sm, p], axis=1).astype(BF16)
        m = dot_nn(cat, wo_ref[...])
        _, mn = rms_stats(m)
        m_ref[...] = m
        x1_ref[...] = x_ref[...] + mod_ref[2:3, :] * (mn * n2_ref[...])

    return pl.pallas_call(
        body, name="post_mix", grid=(bl, s // TT),
        in_specs=[_tile_spec(D), _tile_spec(2 * A_W), _tile_spec(C_W), _tile_spec(B_W), _mod_spec(nc)]
        + _mix_const_specs(nc),
        out_specs=[_tile_spec(D), _tile_spec(D)],
        out_shape=[jax.ShapeDtypeStruct((bl, s, D), F32), jax.ShapeDtypeStruct((bl, s, D), F32)],
        compiler_params=_cp(("arbitrary", "arbitrary")),
    )(xs, za, zp, ys, mod, *_mix_const_args(cst))


def post_mix_bwd(dx1, m, za, zp, ys, mod, cst, nc):
    bl, s, _ = m.shape
    nt = s // TT

    def body(dx_ref, m_ref, za_ref, zp_ref, y_ref, mod_ref, sw_ref, sbias_ref, seg_ref, band_ref, icnt_ref,
             wbd_ref, ps_ref, gw_ref, gb_ref, wo_ref, n2_ref,
             dza_ref, dzp_ref, dy_ref, cat_ref, dm_ref, gg_ref, dr_ref, st_ref, dsw_ref, dsb_ref, dwbd_ref):
        first = jnp.logical_and(pl.program_id(0) == 0, pl.program_id(1) == 0)

        @pl.when(first)
        def _():
            dsw_ref[...] = jnp.zeros_like(dsw_ref)
            dsb_ref[...] = jnp.zeros_like(dsb_ref)
            dwbd_ref[...] = jnp.zeros_like(dwbd_ref)

        seg_p = seg_ref[...]
        za = za_ref[...]
        zp_v = zp_ref[...]
        yv = y_ref[...]
        a, (u, vn, rs, sg) = _sgu_forward(za, sw_ref, sbias_ref[...], seg_p)
        p, (q, o) = _pool_forward(zp_v, band_ref, icnt_ref[...], wbd_ref[...], ps_ref[...])
        sm, (g, sig) = _glu_forward(yv, gw_ref[...], gb_ref[...])
        cat_ref[...] = jnp.concatenate([a, sm, p], axis=1).astype(BF16)

        dx = dx_ref[...]
        g1 = mod_ref[2:3, :]
        n2 = n2_ref[...]
        mv = m_ref[...]
        rm, mn = rms_stats(mv)
        st_ref[...] = jnp.zeros_like(st_ref)
        st_ref[0:1, :] = colsum(dx * (mn * n2))
        st_ref[1:2, :] = colsum(dx * g1 * mn)
        dm = rms_bwd(rm, mn, dx * g1 * n2)
        dmb = dm.astype(BF16)
        dm_ref[...] = dmb
        dcat = dot_nt(dmb, wo_ref[...])
        da, dsm, dp = dcat[:, :A_W], dcat[:, A_W:A_W + B_W], dcat[:, A_W + B_W:]

        du = da * sg
        dsv = da * u
        head = lane_group(A_W, A_W // A_HEADS)
        dvn_parts = []
        dsb_acc = jnp.zeros((CHUNK, A_W), F32)
        for c2 in range(TT // CHUNK):
            dsc = dsv[c2 * CHUNK:(c2 + 1) * CHUNK]
            dsc_b = dsc.astype(BF16)
            vb = vn[c2 * CHUNK:(c2 + 1) * CHUNK].astype(BF16)
            dsb_acc = dsb_acc + dsc
            dvn_c = jnp.zeros((CHUNK, A_W), F32)
            for h in range(A_HEADS):
                dsw_ref[h] += dot_nt(jnp.where(head == h, dsc, 0.0).astype(BF16), vb)
                dvn_c = dvn_c + jnp.where(head == h, dot_tn(sw_ref[h], dsc_b), 0.0)
            dvn_parts.append(dvn_c)
        dsb_ref[...] += dsb_acc
        dvn = jnp.concatenate(dvn_parts, axis=0)
        dv = rs * (dvn - _seg_mean(dvn, seg_p) - vn * _seg_mean(dvn * vn, seg_p))
        dza_ref[...] = jnp.concatenate([du, dv], axis=1) * gelu_grad(za)

        ps = ps_ref[...]
        do = dp * ps
        dps = colsum(dp * o)
        dob = do.astype(BF16)
        dwbd_ref[...] += dot_tn(q.astype(BF16), dob)
        dq = dot_nt(dob, wbd_ref[...])
        hi, lo = split_bf16(dq * icnt_ref[...])
        grp = lane_group(C_W, C_W // len(POOL_WINDOWS))
        dzp = -dq
        for i in range(len(POOL_WINDOWS)):
            t = dot_tn(band_ref[i], hi) + dot_tn(band_ref[i], lo)
            dzp = dzp + jnp.where(grp == i, t, 0.0)
        dzp_ref[...] = dzp

        dr = dsm * g * sig * (1.0 - sig)
        drb = dr.astype(BF16)
        dr_ref[...] = drb
        gg_ref[...] = g.astype(BF16)
        dg = dsm * sig + dot_nt(drb, gw_ref[...])
        dy_ref[...] = dg * gelu_grad(yv)
        st_ref[2:3, :] = jnp.concatenate([colsum(dr), dps, jnp.zeros((1, D - B_W - C_W), F32)], axis=1)

    acc = lambda shape: pl.BlockSpec(shape, lambda b, j: (0,) * len(shape))
    return pl.pallas_call(
        body, name="post_mix_bwd", grid=(bl, nt),
        in_specs=[_tile_spec(D), _tile_spec(D), _tile_spec(2 * A_W), _tile_spec(C_W), _tile_spec(B_W), _mod_spec(nc)]
        + _mix_const_specs(nc),
        out_specs=[_tile_spec(2 * A_W), _tile_spec(C_W), _tile_spec(B_W), _tile_spec(D), _tile_spec(D),
                   _tile_spec(B_W), _tile_spec(B_W), _stat_spec(),
                   acc((A_HEADS, CHUNK, CHUNK)), acc((CHUNK, A_W)), acc((C_W, C_W))],
        out_shape=[jax.ShapeDtypeStruct((bl, s, 2 * A_W), F32), jax.ShapeDtypeStruct((bl, s, C_W), F32),
                   jax.ShapeDtypeStruct((bl, s, B_W), F32), jax.ShapeDtypeStruct((bl, s, D), BF16),
                   jax.ShapeDtypeStruct((bl, s, D), BF16), jax.ShapeDtypeStruct((bl, s, B_W), BF16),
                   jax.ShapeDtypeStruct((bl, s, B_W), BF16), jax.ShapeDtypeStruct((bl, nt, 8, D), F32),
                   jax.ShapeDtypeStruct((A_HEADS, CHUNK, CHUNK), F32), jax.ShapeDtypeStruct((CHUNK, A_W), F32),
                   jax.ShapeDtypeStruct((C_W, C_W), F32)],
        compiler_params=_cp(("arbitrary", "arbitrary")),
    )(dx1, m, za, zp, ys, mod, *_mix_const_args(cst))


def pre_mix_bwd(dza, dzu, dzp, xs, dxres, mod, n1, w_int, nc):
    bl, s, _ = xs.shape
    nt = s // TT

    def body(dza_ref, dzu_ref, dzp_ref, x_ref, dres_ref, mod_ref, n_ref, w_ref, dx_ref, h_ref, dz_ref, st_ref):
        dz = jnp.concatenate([dza_ref[...], dzu_ref[...], dzp_ref[...]], axis=1).astype(BF16)
        dz_ref[...] = dz
        dh = dot_nn(dz, w_ref[...])
        r, xn = rms_stats(x_ref[...])
        n1v = n_ref[...]
        sc = mod_ref[1:2, :]
        xg = xn * n1v
        h_ref[...] = (xg * (1.0 + sc) + mod_ref[0:1, :]).astype(BF16)
        dyv = dh * (1.0 + sc)
        st_ref[...] = jnp.zeros_like(st_ref)
        st_ref[0:1, :] = colsum(dh)
        st_ref[1:2, :] = colsum(dh * xg)
        st_ref[2:3, :] = colsum(dyv * xn)
        dx_ref[...] = dres_ref[...] + rms_bwd(r, xn, dyv * n1v)

    return pl.pallas_call(
        body, name="pre_mix_bwd", grid=(bl, nt),
        in_specs=[_tile_spec(2 * A_W), _tile_spec(B_W), _tile_spec(C_W), _tile_spec(D), _tile_spec(D), _mod_spec(nc),
                  _full_spec((1, D)), _full_spec((D_IN, D))],
        out_specs=[_tile_spec(D), _tile_spec(D), _tile_spec(D_IN), _stat_spec()],
        out_shape=[jax.ShapeDtypeStruct((bl, s, D), F32), jax.ShapeDtypeStruct((bl, s, D), BF16),
                   jax.ShapeDtypeStruct((bl, s, D_IN), BF16), jax.ShapeDtypeStruct((bl, nt, 8, D), F32)],
        compiler_params=_cp(("arbitrary", "arbitrary")),
    )(dza, dzu, dzp, xs, dxres, mod, n1, w_int)


def ffn_fwd(x1, mod, n3, n4, wg_t, wu_t, wd, nc):
    bl, s, _ = x1.shape
    nk = D_FF // FF_CHUNK
    tile = pl.BlockSpec((None, TT, D), lambda b, j, k: (b, j, 0))
    modspec = pl.BlockSpec((None, None, 8, D), lambda b, j, k: (b, jnp.where(j >= nc, 1, 0), 0, 0))
    vec = pl.BlockSpec((1, D), lambda b, j, k: (0, 0))
    wspec = pl.BlockSpec((FF_CHUNK, D), lambda b, j, k: (k, 0))

    def body(x_ref, mod_ref, n3_ref, n4_ref, wg_ref, wu_ref, wd_ref, x2_ref, f_ref, h_s, acc_s):
        k = pl.program_id(2)

        @pl.when(k == 0)
        def _():
            _, xn = rms_stats(x_ref[...])
            h_s[...] = (xn * n3_ref[...] * (1.0 + mod_ref[4:5, :]) + mod_ref[3:4, :]).astype(BF16)
            acc_s[...] = jnp.zeros_like(acc_s)

        h = h_s[...]
        gate = dot_nt(h, wg_ref[...])
        up = dot_nt(h, wu_ref[...])
        act = (gate * jax.nn.sigmoid(gate)) * up
        acc_s[...] += dot_nn(act.astype(BF16), wd_ref[...])

        @pl.when(k == nk - 1)
        def _():
            f = acc_s[...]
            f_ref[...] = f
            _, fn = rms_stats(f)
            x2_ref[...] = x_ref[...] + mod_ref[5:6, :] * (fn * n4_ref[...])

    return pl.pallas_call(
        body, name="ffn_fwd", grid=(bl, s // TT, nk),
        in_specs=[tile, modspec, vec, vec, wspec, wspec, wspec],
        out_specs=[tile, tile],
        out_shape=[jax.ShapeDtypeStruct((bl, s, D), F32), jax.ShapeDtypeStruct((bl, s, D), F32)],
        scratch_shapes=[pltpu.VMEM((TT, D), BF16), pltpu.VMEM((TT, D), F32)],
        compiler_params=_cp(("arbitrary", "arbitrary", "arbitrary")),
    )(x1, mod, n3, n4, wg_t, wu_t, wd)


def ffn_bwd(dx2, x1, f, mod, n3, n4, wg_t, wu_t, wd, nc):
    bl, s, _ = x1.shape
    nt = s // TT
    nk = D_FF // FF_CHUNK
    tile = pl.BlockSpec((None, TT, D), lambda b, j, k: (b, j, 0))
    ftile = pl.BlockSpec((None, TT, FF_CHUNK), lambda b, j, k: (b, j, k))
    modspec = pl.BlockSpec((None, None, 8, D), lambda b, j, k: (b, jnp.where(j >= nc, 1, 0), 0, 0))
    vec = pl.BlockSpec((1, D), lambda b, j, k: (0, 0))
    wspec = pl.BlockSpec((FF_CHUNK, D), lambda b, j, k: (k, 0))
    stat = pl.BlockSpec((None, None, 8, D), lambda b, j, k: (b, j, 0, 0))

    def body(dx_ref, x_ref, f_ref, mod_ref, n3_ref, n4_ref, wg_ref, wu_ref, wd_ref,
             dx1_ref, h_ref, df_ref, act_ref, dgate_ref, dup_ref, st_ref, h_s, df_s, acc_s):
        k = pl.program_id(2)

        @pl.when(k == 0)
        def _():
            dx = dx_ref[...]
            g2 = mod_ref[5:6, :]
            n4 = n4_ref[...]
            rf, fn = rms_stats(f_ref[...])
            st_ref[...] = jnp.zeros_like(st_ref)
            st_ref[2:3, :] = colsum(dx * (fn * n4))
            st_ref[4:5, :] = colsum(dx * g2 * fn)
            df = rms_bwd(rf, fn, dx * g2 * n4).astype(BF16)
            df_s[...] = df
            df_ref[...] = df
            _, xn = rms_stats(x_ref[...])
            h = (xn * n3_ref[...] * (1.0 + mod_ref[4:5, :]) + mod_ref[3:4, :]).astype(BF16)
            h_s[...] = h
            h_ref[...] = h
            acc_s[...] = jnp.zeros_like(acc_s)

        h = h_s[...]
        gate = dot_nt(h, wg_ref[...])
        up = dot_nt(h, wu_ref[...])
        sg = jax.nn.sigmoid(gate)
        silu = gate * sg
        dact = dot_nt(df_s[...], wd_ref[...])
        act_ref[...] = (silu * up).astype(BF16)
        dgate = (dact * up * (sg * (1.0 + gate * (1.0 - sg)))).astype(BF16)
        dup = (dact * silu).astype(BF16)
        dgate_ref[...] = dgate
        dup_ref[...] = dup
        acc_s[...] += dot_nn(dgate, wg_ref[...]) + dot_nn(dup, wu_ref[...])

        @pl.when(k == nk - 1)
        def _():
            dh = acc_s[...]
            r, xn = rms_stats(x_ref[...])
            n3 = n3_ref[...]
            sc = mod_ref[4:5, :]
            xg = xn * n3
            dyv = dh * (1.0 + sc)
            st_ref[0:1, :] = colsum(dh)
            st_ref[1:2, :] = colsum(dh * xg)
            st_ref[3:4, :] = colsum(dyv * xn)
            dx1_ref[...] = dx_ref[...] + rms_bwd(r, xn, dyv * n3)

    return pl.pallas_call(
        body, name="ffn_bwd", grid=(bl, nt, nk),
        in_specs=[tile, tile, tile, modspec, vec, vec, wspec, wspec, wspec],
        out_specs=[tile, tile, tile, ftile, ftile, ftile, stat],
        out_shape=[jax.ShapeDtypeStruct((bl, s, D), F32), jax.ShapeDtypeStruct((bl, s, D), BF16),
                   jax.ShapeDtypeStruct((bl, s, D), BF16), jax.ShapeDtypeStruct((bl, s, D_FF), BF16),
                   jax.ShapeDtypeStruct((bl, s, D_FF), BF16), jax.ShapeDtypeStruct((bl, s, D_FF), BF16),
                   jax.ShapeDtypeStruct((bl, nt, 8, D), F32)],
        scratch_shapes=[pltpu.VMEM((TT, D), BF16), pltpu.VMEM((TT, D), BF16), pltpu.VMEM((TT, D), F32)],
        compiler_params=_cp(("arbitrary", "arbitrary", "arbitrary")),
    )(dx2, x1, f, mod, n3, n4, wg_t, wu_t, wd)


def loss_head(xs, target, nc):
    bl, s, _ = xs.shape
    nt = s // TT

    def body(x_ref, t_ref, dx_ref, l_ref):
        j = pl.program_id(1)

        @pl.when(j < nc)
        def _():
            dx_ref[...] = jnp.zeros_like(dx_ref)
            l_ref[...] = jnp.zeros_like(l_ref)

        @pl.when(j >= nc)
        def _():
            e = x_ref[...] - t_ref[...]
            dx_ref[...] = e * (1.0 / D)
            tok = jnp.mean(e * e, axis=-1, keepdims=True)
            l_ref[...] = jnp.zeros_like(l_ref) + 0.5 * jnp.sum(tok, axis=0, keepdims=True)

    return pl.pallas_call(
        body, name="loss_head", grid=(bl, nt),
        in_specs=[_tile_spec(D), pl.BlockSpec((None, TT, D), lambda b, j: (b, jnp.maximum(j - nc, 0), 0))],
        out_specs=[_tile_spec(D), pl.BlockSpec((None, None, 8, 128), lambda b, j: (b, j, 0, 0))],
        out_shape=[jax.ShapeDtypeStruct((bl, s, D), F32), jax.ShapeDtypeStruct((bl, nt, 8, 128), F32)],
        compiler_params=_cp(("arbitrary", "arbitrary")),
    )(xs, target)


def tn_matmul(a, b, name):
    t, ka = a.shape
    n = b.shape[1]
    tk = 512 if ka % 512 == 0 else 256
    tt = 512 if t % 512 == 0 else 256
    nsteps = t // tt

    def body(a_ref, b_ref, o_ref):
        @pl.when(pl.program_id(1) == 0)
        def _():
            o_ref[...] = jnp.zeros_like(o_ref)

        o_ref[...] += dot_tn(a_ref[...], b_ref[...])

    return pl.pallas_call(
        body, name=name, grid=(ka // tk, nsteps),
        in_specs=[pl.BlockSpec((tt, tk), lambda i, s: (s, i)), pl.BlockSpec((tt, n), lambda i, s: (s, 0))],
        out_specs=pl.BlockSpec((tk, n), lambda i, s: (i, 0)),
        out_shape=jax.ShapeDtypeStruct((ka, n), F32),
        compiler_params=_cp(("arbitrary", "arbitrary")),
    )(a, b)


def _quarter(x, q):
    return jnp.concatenate([x[:, k * B_W + q * 128:k * B_W + (q + 1) * 128] for k in range(TC)], axis=1)


def qmm(inp, w, acc, name):
    r = inp.shape[0]
    rt = 128 if r % 128 == 0 else 64
    assert r % rt == 0

    def compute(i_ref, w_ref, o_ref, base):
        x = i_ref[...]
        for q in range(4):
            y = dot_nn(_quarter(x, q).astype(BF16), w_ref[q])
            for k in range(TC):
                lo = k * B_W + q * 128
                if base is None:
                    o_ref[:, lo:lo + 128] = y[:, k * 128:(k + 1) * 128]
                else:
                    o_ref[:, lo:lo + 128] = base[:, lo:lo + 128] + y[:, k * 128:(k + 1) * 128]

    row = pl.BlockSpec((rt, ROW_W), lambda i: (i, 0))
    wspec = pl.BlockSpec((4, 1024, 1024), lambda i: (0, 0, 0))
    if acc is None:
        def body(i_ref, w_ref, o_ref):
            compute(i_ref, w_ref, o_ref, None)
        ins, specs = (inp, w), [row, wspec]
    else:
        def body(i_ref, w_ref, a_ref, o_ref):
            compute(i_ref, w_ref, o_ref, a_ref[...])
        ins, specs = (inp, w, acc), [row, wspec, row]
    return pl.pallas_call(
        body, name=name, grid=(r // rt,), in_specs=specs, out_specs=row,
        out_shape=jax.ShapeDtypeStruct((r, ROW_W), F32),
        compiler_params=_cp(("arbitrary",)),
    )(*ins)


def qtn(a, b, name):
    r = a.shape[0]
    rt = 128 if r % 128 == 0 else 64
    assert r % rt == 0
    nsteps = r // rt

    def body(*refs):
        a_refs, b_refs, o_ref = refs[:TC], refs[TC:2 * TC], refs[2 * TC]

        @pl.when(pl.program_id(1) == 0)
        def _():
            o_ref[...] = jnp.zeros_like(o_ref)

        av = jnp.concatenate([x[...] for x in a_refs], axis=1).astype(BF16)
        bv = jnp.concatenate([x[...] for x in b_refs], axis=1).astype(BF16)
        o_ref[...] += dot_tn(av, bv)

    def col(k):
        return pl.BlockSpec((rt, 128), lambda q, s: (s, k * 4 + q))

    specs = [col(k) for k in range(TC)] * 2
    return pl.pallas_call(
        body, name=name, grid=(4, nsteps), in_specs=specs,
        out_specs=pl.BlockSpec((None, 1024, 1024), lambda q, s: (q, 0, 0)),
        out_shape=jax.ShapeDtypeStruct((4, 1024, 1024), F32),
        compiler_params=_cp(("arbitrary", "arbitrary")),
    )(*([a] * TC + [b] * TC))


def _scan_row(i, rb, ncr, reverse):
    if not reverse:
        return i
    return jnp.where(i < ncr, ncr - 1 - i, rb - 1 - (i - ncr))


def chunk_scan(xs, lam8, ncr, reverse, name):
    bl, rb, _ = xs.shape
    half = ROW_W // 2

    def body(x_ref, l_ref, hp_ref):
        ar, ai = l_ref[:, :half], l_ref[:, half:]

        def step(i, carry):
            hr, hi = carry
            row = _scan_row(i, rb, ncr, reverse)
            hp_ref[pl.ds(row, 1), :] = jnp.concatenate([hr, hi], axis=1)
            xv = x_ref[pl.ds(row, 1), :]
            return (ar * hr - ai * hi + xv[:, :half], ai * hr + ar * hi + xv[:, half:])

        lax.fori_loop(0, rb, step, (jnp.zeros((1, half), F32), jnp.zeros((1, half), F32)))

    blk = pl.BlockSpec((None, rb, ROW_W), lambda b: (b, 0, 0))
    return pl.pallas_call(
        body, name=name, grid=(bl,),
        in_specs=[blk, pl.BlockSpec((1, ROW_W), lambda b: (0, 0))], out_specs=blk,
        out_shape=jax.ShapeDtypeStruct((bl, rb, ROW_W), F32),
        compiler_params=_cp(("arbitrary",)),
    )(xs, lam8)


def chunk_scan_bwd(dhp, hp, lam8, ncr, reverse, name):
    bl, rb, _ = dhp.shape
    half = ROW_W // 2

    def body(d_ref, hp_ref, l_ref, g_ref, dl_ref):
        ar, ai = l_ref[:, :half], l_ref[:, half:]

        def step(n, carry):
            gr, gi, dar, dai = carry
            row = _scan_row(rb - 1 - n, rb, ncr, reverse)
            g_ref[pl.ds(row, 1), :] = jnp.concatenate([gr, gi], axis=1)
            dv = d_ref[pl.ds(row, 1), :]
            pv = hp_ref[pl.ds(row, 1), :]
            pr, pi = pv[:, :half], pv[:, half:]
            dar = dar + gr * pr + gi * pi
            dai = dai + gi * pr - gr * pi
            return (dv[:, :half] + ar * gr + ai * gi, dv[:, half:] + ar * gi - ai * gr, dar, dai)

        z = jnp.zeros((1, half), F32)
        _, _, dar, dai = lax.fori_loop(0, rb, step, (z, z, z, z))
        dl_ref[...] = jnp.zeros_like(dl_ref)
        dl_ref[0:1, :] = jnp.concatenate([dar, dai], axis=1)

    blk = pl.BlockSpec((None, rb, ROW_W), lambda b: (b, 0, 0))
    return pl.pallas_call(
        body, name=name, grid=(bl,),
        in_specs=[blk, blk, pl.BlockSpec((1, ROW_W), lambda b: (0, 0))],
        out_specs=[blk, pl.BlockSpec((None, 8, ROW_W), lambda b: (b, 0, 0))],
        out_shape=[jax.ShapeDtypeStruct((bl, rb, ROW_W), F32), jax.ShapeDtypeStruct((bl, 8, ROW_W), F32)],
        compiler_params=_cp(("arbitrary",)),
    )(dhp, hp, lam8)


def _state_layout(v):
    return v.reshape(4, 4, 2, SSM_P).transpose(1, 0, 2, 3).reshape(SSM_G * SSM_P)


def ssm_build(lam_re, lam_im, log_dt, b_re, b_im, c_re, c_im, d):
    ein = functools.partial(jnp.einsum, precision=HI)
    dt = jnp.exp(log_dt)[..., None]
    mag = jnp.exp(lam_re * dt)
    ang = lam_im * dt
    lr, li = mag * jnp.cos(ang), mag * jnp.sin(ang)
    den = lam_re * lam_re + lam_im * lam_im
    nr = lr - 1.0
    fr = (nr * lam_re + li * lam_im) / den
    fi = (li * lam_re - nr * lam_im) / den
    bbr = fr[..., None] * b_re - fi[..., None] * b_im
    bbi = fr[..., None] * b_im + fi[..., None] * b_re
    pr, pi = [jnp.ones_like(lr)], [jnp.zeros_like(lr)]
    for _ in range(TC):
        pr, pi = pr + [pr[-1] * lr - pi[-1] * li], pi + [pr[-1] * li + pi[-1] * lr]
    pr, pi = jnp.stack(pr), jnp.stack(pi)
    clr = c_re[None] * pr[:, :, :, None, :] - c_im[None] * pi[:, :, :, None, :]
    cli = c_re[None] * pi[:, :, :, None, :] + c_im[None] * pr[:, :, :, None, :]
    eye8 = jnp.eye(8, dtype=F32)
    lag = np.arange(TC)[None, :] - np.arange(TC)[:, None]

    def onehot(sign):
        e = np.zeros((TC, TC, TC), np.float32)
        for n in range(TC):
            e[n] = (sign * lag == n)
        return jnp.asarray(e)

    out = {}
    m6 = 0.0
    for k, sign in ((0, 1), (1, -1)):
        kn = ein('ngip,gpj->ngij', clr[:TC, k], bbr[k]) - ein('ngip,gpj->ngij', cli[:TC, k], bbi[k])
        e = onehot(sign)
        if k == 1:
            e = e.at[0].set(0.0)
        m6 = m6 + ein('nst,ngij->sgjti', e, kn)
    diag0 = ein('st,gij->sgjti', jnp.eye(TC, dtype=F32),
                d[:, :, None] * jnp.eye(SSM_H, dtype=F32)[None]
                + (ein('gip,gpj->gij', clr[0, 1], bbr[1]) - ein('gip,gpj->gij', cli[0, 1], bbi[1])))
    m6 = (m6 + diag0).reshape(TC, 4, 8, SSM_H, TC, SSM_H).transpose(1, 0, 2, 3, 4, 5)
    out["m"] = ein('qsgjti,gh->qsgjthi', m6, eye8).reshape(4, 1024, 1024)
    for k, name in ((0, "f"), (1, "r")):
        es = [TC - 1 - s for s in range(TC)] if k == 0 else list(range(TC))
        sr = jnp.stack([pr[e, k][:, :, None] * bbr[k] - pi[e, k][:, :, None] * bbi[k] for e in es])
        si = jnp.stack([pr[e, k][:, :, None] * bbi[k] + pi[e, k][:, :, None] * bbr[k] for e in es])
        parts = [ein('sqgpj,gh->qsgjhp', v.reshape(TC, 4, 8, SSM_P, SSM_H), eye8) for v in (sr, si)]
        out["bs_" + name] = jnp.stack(parts, axis=4).reshape(4, 1024, 1024)
        et = [t + 1 for t in range(TC)] if k == 0 else [TC - t for t in range(TC)]
        cr = jnp.stack([clr[e, k] for e in et])
        ci = jnp.stack([-cli[e, k] for e in et])
        parts = [ein('tqgip,gh->qgpthi', v.reshape(TC, 4, 8, SSM_H, SSM_P), eye8) for v in (cr, ci)]
        out["cs_" + name] = jnp.stack(parts, axis=1).reshape(4, 1024, 1024)
        out["lam8_" + name] = jnp.concatenate([_state_layout(pr[TC, k]), _state_layout(pi[TC, k])])[None, :]
    return out


def ssm_forward(zu, mats, ncr):
    bl, s, _ = zu.shape
    rb = s // TC
    u = zu.reshape(bl * rb, ROW_W)
    bf = lambda k: mats[k].astype(BF16)
    hps = {}
    y = qmm(u, bf("m"), None, "ssm_intra")
    for dname, rev in (("f", False), ("r", True)):
        xs = qmm(u, bf("bs_" + dname), None, "ssm_state_in_" + dname)
        hp = chunk_scan(xs.reshape(bl, rb, ROW_W), mats["lam8_" + dname], ncr, rev, "ssm_scan_" + dname)
        hps[dname] = hp.reshape(bl * rb, ROW_W)
        y = qmm(hps[dname], bf("cs_" + dname), y, "ssm_readout_" + dname)
    return y.reshape(bl, s, B_W), hps


def ssm_backward(dy, zu, hps, mats, ncr):
    bl, s, _ = zu.shape
    rb = s // TC
    u = zu.reshape(bl * rb, ROW_W)
    dyr = dy.reshape(bl * rb, ROW_W)
    bft = lambda k: jnp.swapaxes(mats[k], 1, 2).astype(BF16)
    cot = {"m": qtn(u, dyr, "ssm_d_intra")}
    du = qmm(dyr, bft("m"), None, "ssm_du_intra")
    for dname, rev in (("f", False), ("r", True)):
        dhp = qmm(dyr, bft("cs_" + dname), None, "ssm_dstate_" + dname)
        g, dl = chunk_scan_bwd(dhp.reshape(bl, rb, ROW_W), hps[dname].reshape(bl, rb, ROW_W), mats["lam8_" + dname],
                               ncr, rev, "ssm_scan_bwd_" + dname)
        g = g.reshape(bl * rb, ROW_W)
        cot["lam8_" + dname] = jnp.sum(dl[:, 0:1, :], axis=0)
        cot["bs_" + dname] = qtn(u, g, "ssm_d_state_in_" + dname)
        cot["cs_" + dname] = qtn(hps[dname], dyr, "ssm_d_readout_" + dname)
        du = qmm(g, bft("bs_" + dname), du, "ssm_du_state_" + dname)
    return du.reshape(bl, s, B_W), cot


def mod_forward(act, w_mod, b_cols):
    nl, _, wc = w_mod.shape
    r = act.shape[0]

    def body(a_ref, w_ref, b_ref, o_ref):
        o_ref[...] = dot_nn(a_ref[...].astype(BF16), w_ref[...].astype(BF16)) + b_ref[...]

    return pl.pallas_call(
        body, name="mod_forward", grid=(nl,),
        in_specs=[pl.BlockSpec((r, D), lambda l: (0, 0)), pl.BlockSpec((None, D, wc), lambda l: (l, 0, 0)),
                  pl.BlockSpec((None, 1, wc), lambda l: (l, 0, 0))],
        out_specs=pl.BlockSpec((None, r, wc), lambda l: (l, 0, 0)),
        out_shape=jax.ShapeDtypeStruct((nl, r, wc), F32),
        compiler_params=_cp(("arbitrary",)),
    )(act, w_mod, b_cols)


def mod_backward(act, dmod, dctx, w_mod):
    nl, _, wc = w_mod.shape
    r = act.shape[0]

    def body(a_ref, d_ref, c_ref, w_ref, gw_ref, gc_ref):
        gw_ref[...] = dot_tn(a_ref[...].astype(BF16), d_ref[...].astype(BF16))
        gc_ref[...] = dot_nt(c_ref[...].astype(BF16), w_ref[...].astype(BF16))

    return pl.pallas_call(
        body, name="mod_backward", grid=(nl,),
        in_specs=[pl.BlockSpec((r, D), lambda l: (0, 0)), pl.BlockSpec((None, r, wc), lambda l: (l, 0, 0)),
                  pl.BlockSpec((None, 8, wc), lambda l: (l, 0, 0)), pl.BlockSpec((None, D, wc), lambda l: (l, 0, 0))],
        out_specs=[pl.BlockSpec((None, D, wc), lambda l: (l, 0, 0)), pl.BlockSpec((None, 8, D), lambda l: (l, 0, 0))],
        out_shape=[jax.ShapeDtypeStruct((nl, D, wc), F32), jax.ShapeDtypeStruct((nl, 8, D), F32)],
        compiler_params=_cp(("arbitrary",)),
    )(act, dmod, dctx, w_mod)


def _place():
    return lax.axis_index("x"), lax.axis_index("y"), lax.axis_index("c")


def all_gather_rows(arrs, name):
    n = len(arrs)
    rs = [a.shape[1] for a in arrs]

    def body(*refs):
        x_refs, o_refs = refs[:n], refs[n:2 * n]
        send_sems, recv_sems, local_sems = refs[2 * n:]
        x, y, c = _place()
        me, sibling = (x, y, c), (x, y, 1 - c)
        chips = [(1 - x, y), (x, 1 - y), (1 - x, 1 - y)]

        def rows(a, px, py, pc):
            return o_refs[a].at[:, pl.ds((4 * px + 2 * py + pc) * rs[a], rs[a]), :]

        def copy(a, k, block, to, src=None):
            return pltpu.make_async_remote_copy(
                src_ref=rows(a, *block) if src is None else src, dst_ref=rows(a, *block),
                send_sem=send_sems.at[a, k], recv_sem=recv_sems.at[a, k], device_id=to, device_id_type=MESH)

        mine = [pltpu.make_async_copy(x_refs[a], rows(a, *me), local_sems.at[a]) for a in range(n)]
        for cp in mine:
            cp.start()
        first = []
        for a in range(n):
            first.append(copy(a, 0, me, sibling, src=x_refs[a]))
            first += [copy(a, 1 + j, me, (*chip, c), src=x_refs[a]) for j, chip in enumerate(chips)]
        for cp in first:
            cp.start()
        passed = []
        for j, chip in enumerate(chips):
            for a in range(n):
                copy(a, 1 + j, (*chip, c), me).wait_recv()
                fwd = copy(a, 4 + j, (*chip, c), sibling)
                fwd.start()
                passed.append(fwd)
        for a in range(n):
            copy(a, 0, sibling, me).wait_recv()
            for j, chip in enumerate(chips):
                copy(a, 4 + j, (*chip, 1 - c), me).wait_recv()
        for cp in first + passed:
            cp.wait_send()
        for cp in mine:
            cp.wait()

    any_spec = pl.BlockSpec(memory_space=pl.ANY)
    return pl.pallas_call(
        body, name=name,
        in_specs=[any_spec] * n, out_specs=[any_spec] * n,
        out_shape=[jax.ShapeDtypeStruct((a.shape[0], N_DEV * a.shape[1], a.shape[2]), a.dtype) for a in arrs],
        scratch_shapes=[pltpu.SemaphoreType.DMA((n, 7)), pltpu.SemaphoreType.DMA((n, 7)), pltpu.SemaphoreType.DMA((n,))],
    )(*arrs)


def all_to_all_rows(arrs, name):
    n = len(arrs)
    rs = [a.shape[1] // N_DEV for a in arrs]
    flips = [(fx, fy, fc) for fx in (0, 1) for fy in (0, 1) for fc in (0, 1)][1:]

    def body(*refs):
        x_refs, o_refs = refs[:n], refs[n:2 * n]
        send_sems, recv_sems, local_sems = refs[2 * n:]
        x, y, c = _place()
        my_idx = 4 * x + 2 * y + c

        def block(a, idx):
            return x_refs[a].at[:, pl.ds(idx * rs[a], rs[a]), :]

        mine = [pltpu.make_async_copy(block(a, my_idx), o_refs[a].at[my_idx], local_sems.at[a]) for a in range(n)]
        for cp in mine:
            cp.start()
        sends = []
        for k, (fx, fy, fc) in enumerate(flips):
            px = 1 - x if fx else x
            py = 1 - y if fy else y
            pc = 1 - c if fc else c
            p_idx = 4 * px + 2 * py + pc
            for a in range(n):
                sends.append(pltpu.make_async_remote_copy(
                    src_ref=block(a, p_idx), dst_ref=o_refs[a].at[my_idx], send_sem=send_sems.at[a, k],
                    recv_sem=recv_sems.at[a, k], device_id=(px, py, pc), device_id_type=MESH))
        for cp in sends:
            cp.start()
        for k, (fx, fy, fc) in enumerate(flips):
            px = 1 - x if fx else x
            py = 1 - y if fy else y
            pc = 1 - c if fc else c
            p_idx = 4 * px + 2 * py + pc
            for a in range(n):
                pltpu.make_async_remote_copy(
                    src_ref=block(a, p_idx), dst_ref=o_refs[a].at[p_idx], send_sem=send_sems.at[a, k],
                    recv_sem=recv_sems.at[a, k], device_id=(px, py, pc), device_id_type=MESH).wait_recv()
        for cp in sends:
            cp.wait_send()
        for cp in mine:
            cp.wait()

    any_spec = pl.BlockSpec(memory_space=pl.ANY)
    return pl.pallas_call(
        body, name=name,
        in_specs=[any_spec] * n, out_specs=[any_spec] * n,
        out_shape=[jax.ShapeDtypeStruct((N_DEV, a.shape[0], r, a.shape[2]), a.dtype) for a, r in zip(arrs, rs)],
        scratch_shapes=[pltpu.SemaphoreType.DMA((n, 7)), pltpu.SemaphoreType.DMA((n, 7)), pltpu.SemaphoreType.DMA((n,))],
    )(*arrs)


def _row_tile(rows, cap):
    best = None
    for t in range(8, min(rows, cap) + 1, 8):
        if rows % t == 0:
            best = t
    return rows if best is None else best


def adamw(w, gparts, m, v, name):
    n, nl, ra, cb = gparts.shape
    ta = _row_tile(ra, max(8, (1 << 19) // (cb * n)))

    def body(w_ref, g_ref, m_ref, v_ref, go_ref, d_ref, mo_ref, vo_ref):
        g = g_ref[0]
        for p in range(1, n):
            g = g + g_ref[p]
        mn = ADAM_B1 * m_ref[...] + (1.0 - ADAM_B1) * g
        vn = ADAM_B2 * v_ref[...] + (1.0 - ADAM_B2) * jnp.square(g)
        m_hat = mn / (1.0 - ADAM_B1 ** ADAM_STEP)
        v_hat = vn / (1.0 - ADAM_B2 ** ADAM_STEP)
        go_ref[...] = g
        d_ref[...] = -ADAM_LR * (m_hat / (jnp.sqrt(v_hat) + ADAM_EPS) + ADAM_WD * w_ref[...])
        mo_ref[...] = mn
        vo_ref[...] = vn

    blk = pl.BlockSpec((None, ta, cb), lambda l, i: (l, i, 0))
    gblk = pl.BlockSpec((n, None, ta, cb), lambda l, i: (0, l, i, 0))
    shp = jax.ShapeDtypeStruct((nl, ra, cb), F32)
    return pl.pallas_call(
        body, name=name, grid=(nl, ra // ta),
        in_specs=[blk, gblk, blk, blk], out_specs=[blk] * 4, out_shape=[shp] * 4,
        compiler_params=_cp(("arbitrary", "arbitrary")),
    )(w, gparts, m, v)


def _sincos_2d(rows, cols, dim):
    quarter = dim // 4
    omega = 1.0 / (10000.0 ** (jnp.arange(quarter, dtype=F32) / quarter))
    r = jnp.arange(rows, dtype=F32)[:, None] * omega
    cc = jnp.arange(cols, dtype=F32)[:, None] * omega
    er = jnp.concatenate([jnp.sin(r), jnp.cos(r)], axis=-1)
    ec = jnp.concatenate([jnp.sin(cc), jnp.cos(cc)], axis=-1)
    pe = jnp.concatenate([jnp.broadcast_to(er[:, None, :], (rows, cols, dim // 2)),
                          jnp.broadcast_to(ec[None, :, :], (rows, cols, dim // 2))], axis=-1)
    return pe.reshape(rows * cols, dim)


def _pool_constants():
    nw = len(POOL_WINDOWS)
    band = np.zeros((2, nw, TT, TT), np.float32)
    icnt = np.zeros((2, TT, C_W), np.float32)
    for kind, n in ((0, TT), (1, GRID_W)):
        for i, w in enumerate(POOL_WINDOWS):
            for t in range(TT):
                base, tl = (t // n) * n, t % n
                lo = min(max(tl - w // 2, 0), n)
                hi = min(max(tl - w // 2 + w, 0), n)
                band[kind, i, t, base + lo:base + hi] = 1.0
                icnt[kind, t, i * (C_W // nw):(i + 1) * (C_W // nw)] = 1.0 / (hi - lo)
    return jnp.asarray(band, BF16), jnp.asarray(icnt, F32)


def _block_diag(blocks):
    n, a, _ = blocks.shape
    return jnp.einsum('gab,gh->gahb', blocks, jnp.eye(n, dtype=F32), precision=HI).reshape(n * a, n * a)


def _block_diag_parts(mat, n):
    a = mat.shape[0] // n
    m4 = mat.reshape(n, a, n, a)
    return jnp.stack([m4[g, :, g, :] for g in range(n)])


_SMALL = ("c_ctx", "b_mod", "norm_mix_pre", "norm_mix_post", "norm_ffn_pre", "norm_ffn_post", "sgu_w", "sgu_b",
          "ssm_lam_re", "ssm_lam_im", "ssm_log_dt", "ssm_b_re", "ssm_b_im", "ssm_c_re", "ssm_c_im", "ssm_d",
          "glu_b", "pool_w", "pool_scale")
_WEIGHTS = ("c_ctx", "w_mod", "b_mod", "norm_mix_pre", "norm_mix_post", "norm_ffn_pre", "norm_ffn_post", "w_in", "w_out",
            "sgu_w", "sgu_b", "ssm_lam_re", "ssm_lam_im", "ssm_log_dt", "ssm_b_re", "ssm_b_im", "ssm_c_re", "ssm_c_im",
            "ssm_d", "glu_w", "glu_b", "pool_w", "pool_scale", "ffn_w_gate", "ffn_w_up", "ffn_w_down")


def _pack_rows(a):
    flat = a.reshape(-1)
    rows = -(-flat.shape[0] // D)
    rows8 = -(-rows // 8) * 8
    return jnp.pad(flat, (0, rows8 * D - flat.shape[0])).reshape(rows8, D)


def _pack(tree):
    return jnp.concatenate([_pack_rows(tree[k]) for k in _SMALL], axis=0)


def _unpack(packed, like):
    out, at = {}, 0
    for k in _SMALL:
        size = int(np.prod(like[k].shape))
        rows8 = -(-(-(-size // D)) // 8) * 8
        out[k] = packed[at:at + rows8].reshape(-1)[:size].reshape(like[k].shape)
        at += rows8
    return out


def kernel(x, c, ctx, c_ctx, w_mod, b_mod, norm_mix_pre, norm_mix_post, norm_ffn_pre, norm_ffn_post, w_in, w_out, sgu_w, sgu_b, ssm_lam_re, ssm_lam_im, ssm_log_dt, ssm_b_re, ssm_b_im, ssm_c_re, ssm_c_im, ssm_d, glu_w, glu_b, pool_w, pool_scale, ffn_w_gate, ffn_w_up, ffn_w_down, loss_target, m_c_ctx, m_w_mod, m_b_mod, m_norm_mix_pre, m_norm_mix_post, m_norm_ffn_pre, m_norm_ffn_post, m_w_in, m_w_out, m_sgu_w, m_sgu_b, m_ssm_lam_re, m_ssm_lam_im, m_ssm_log_dt, m_ssm_b_re, m_ssm_b_im, m_ssm_c_re, m_ssm_c_im, m_ssm_d, m_glu_w, m_glu_b, m_pool_w, m_pool_scale, m_ffn_w_gate, m_ffn_w_up, m_ffn_w_down, v_c_ctx, v_w_mod, v_b_mod, v_norm_mix_pre, v_norm_mix_post, v_norm_ffn_pre, v_norm_ffn_post, v_w_in, v_w_out, v_sgu_w, v_sgu_b, v_ssm_lam_re, v_ssm_lam_im, v_ssm_log_dt, v_ssm_b_re, v_ssm_b_im, v_ssm_c_re, v_ssm_c_im, v_ssm_d, v_glu_w, v_glu_b, v_pool_w, v_pool_scale, v_ffn_w_gate, v_ffn_w_up, v_ffn_w_down):
    wts = dict(c_ctx=c_ctx, w_mod=w_mod, b_mod=b_mod, norm_mix_pre=norm_mix_pre, norm_mix_post=norm_mix_post,
               norm_ffn_pre=norm_ffn_pre, norm_ffn_post=norm_ffn_post, w_in=w_in, w_out=w_out, sgu_w=sgu_w, sgu_b=sgu_b,
               ssm_lam_re=ssm_lam_re, ssm_lam_im=ssm_lam_im, ssm_log_dt=ssm_log_dt, ssm_b_re=ssm_b_re, ssm_b_im=ssm_b_im,
               ssm_c_re=ssm_c_re, ssm_c_im=ssm_c_im, ssm_d=ssm_d, glu_w=glu_w, glu_b=glu_b, pool_w=pool_w,
               pool_scale=pool_scale, ffn_w_gate=ffn_w_gate, ffn_w_up=ffn_w_up, ffn_w_down=ffn_w_down)
    mom_m = dict(c_ctx=m_c_ctx, w_mod=m_w_mod, b_mod=m_b_mod, norm_mix_pre=m_norm_mix_pre, norm_mix_post=m_norm_mix_post,
                 norm_ffn_pre=m_norm_ffn_pre, norm_ffn_post=m_norm_ffn_post, w_in=m_w_in, w_out=m_w_out, sgu_w=m_sgu_w,
                 sgu_b=m_sgu_b, ssm_lam_re=m_ssm_lam_re, ssm_lam_im=m_ssm_lam_im, ssm_log_dt=m_ssm_log_dt,
                 ssm_b_re=m_ssm_b_re, ssm_b_im=m_ssm_b_im, ssm_c_re=m_ssm_c_re, ssm_c_im=m_ssm_c_im, ssm_d=m_ssm_d,
                 glu_w=m_glu_w, glu_b=m_glu_b, pool_w=m_pool_w, pool_scale=m_pool_scale, ffn_w_gate=m_ffn_w_gate,
                 ffn_w_up=m_ffn_w_up, ffn_w_down=m_ffn_w_down)
    mom_v = dict(c_ctx=v_c_ctx, w_mod=v_w_mod, b_mod=v_b_mod, norm_mix_pre=v_norm_mix_pre, norm_mix_post=v_norm_mix_post,
                 norm_ffn_pre=v_norm_ffn_pre, norm_ffn_post=v_norm_ffn_post, w_in=v_w_in, w_out=v_w_out, sgu_w=v_sgu_w,
                 sgu_b=v_sgu_b, ssm_lam_re=v_ssm_lam_re, ssm_lam_im=v_ssm_lam_im, ssm_log_dt=v_ssm_log_dt,
                 ssm_b_re=v_ssm_b_re, ssm_b_im=v_ssm_b_im, ssm_c_re=v_ssm_c_re, ssm_c_im=v_ssm_c_im, ssm_d=v_ssm_d,
                 glu_w=v_glu_w, glu_b=v_glu_b, pool_w=v_pool_w, pool_scale=v_pool_scale, ffn_w_gate=v_ffn_w_gate,
                 ffn_w_up=v_ffn_w_up, ffn_w_down=v_ffn_w_down)

    bl, seq, _ = x.shape
    n_ctx = ctx.shape[1]
    assert n_ctx == TT and seq % TT == 0 and seq % GRID_W == 0
    depth = w_in.shape[0]
    nc = n_ctx // TT
    ncr = n_ctx // TC
    s_all = n_ctx + seq
    nt = s_all // TT
    t_all = bl * s_all
    n_batch = bl * N_DEV
    my_idx = 4 * lax.axis_index("x") + 2 * lax.axis_index("y") + lax.axis_index("c")
    wc = w_mod.shape[2]

    c_rows = jnp.pad(c, ((0, 8 - bl), (0, 0))) if bl < 8 else c
    rc = c_rows.shape[0]
    (c_all,) = all_gather_rows([c_rows[None]], "gather_c")
    c_all = c_all[0].reshape(N_DEV, rc, D)[:, :bl].reshape(n_batch, D)
    r_act = -(-(n_batch + 1) // 16) * 16
    pre_act = jnp.concatenate([c_all, c_ctx[None, :], jnp.zeros((r_act - n_batch - 1, D), F32)], axis=0)
    act = jax.nn.silu(pre_act)
    b_cols = lax.dynamic_slice_in_dim(b_mod, my_idx * wc, wc, axis=1)[:, None, :]
    mod_cols = mod_forward(act, w_mod, b_cols)
    (mod_all,) = all_gather_rows([mod_cols], "gather_mod")
    mod_all = mod_all.reshape(depth, N_DEV, r_act, wc).transpose(0, 2, 1, 3).reshape(depth, r_act, 6, D)
    mod_lat = lax.dynamic_slice_in_dim(mod_all, my_idx * bl, bl, axis=1)
    mod_ctx = jnp.broadcast_to(mod_all[:, n_batch:n_batch + 1], (depth, bl, 6, D))
    mods = jnp.pad(jnp.stack([mod_ctx, mod_lat], axis=2), ((0, 0), (0, 0), (0, 0), (0, 2), (0, 0)))

    tr = lambda a: jnp.swapaxes(a, 1, 2).astype(BF16)
    w_int, w_o, g_w, wg_t, wu_t, w_d = all_gather_rows(
        [tr(w_in), w_out.astype(BF16), glu_w.astype(BF16), tr(ffn_w_gate), tr(ffn_w_up), ffn_w_down.astype(BF16)],
        "gather_weights")

    band, icnt = _pool_constants()
    seg_p = jnp.asarray(np.kron(np.eye(A_HEADS), np.full((A_W // A_HEADS,) * 2, A_HEADS / A_W)), BF16)
    pe = _sincos_2d(seq // GRID_W, GRID_W, D)
    xs = embed_tokens(x, ctx, pe)

    saved = []
    for i in range(depth):
        mats, ssm_vjp = jax.vjp(ssm_build, ssm_lam_re[i], ssm_lam_im[i], ssm_log_dt[i], ssm_b_re[i], ssm_b_im[i],
                                ssm_c_re[i], ssm_c_im[i], ssm_d[i])
        cst = dict(sw=sgu_w[i].astype(BF16),
                   sbias=jnp.repeat(sgu_b[i].T, A_W // A_HEADS, axis=1),
                   seg_p=seg_p, band=band, icnt=icnt, wbd=_block_diag(pool_w[i]).astype(BF16),
                   pscale=pool_scale[i][None, :], glu_w=g_w[i], glu_b=glu_b[i][None, :], w_out=w_o[i],
                   n2=norm_mix_post[i][None, :])
        n1, n3, n4 = norm_mix_pre[i][None, :], norm_ffn_pre[i][None, :], norm_ffn_post[i][None, :]
        za, zu, zp = pre_mix(xs, mods[i], n1, w_int[i], nc)
        ys, hps = ssm_forward(zu, mats, ncr)
        x1, m_pre = post_mix(xs, za, zp, ys, mods[i], cst, nc)
        x2, f_pre = ffn_fwd(x1, mods[i], n3, n4, wg_t[i], wu_t[i], w_d[i], nc)
        saved.append(dict(xs=xs, za=za, zu=zu, zp=zp, ys=ys, hps=hps, x1=x1, m=m_pre, f=f_pre, cst=cst, mats=mats,
                          ssm_vjp=ssm_vjp, n1=n1, n3=n3, n4=n4))
        xs = x2

    dx, loss_parts = loss_head(xs, loss_target, nc)
    loss = lax.psum(jnp.sum(loss_parts[:, :, 0, 0]), ("x", "y", "c"))

    grads = {k: [None] * depth for k in _WEIGHTS}
    big = {k: [None] * depth for k in ("w_in", "w_out", "glu_w", "ffn_w_gate", "ffn_w_up", "ffn_w_down")}
    dmods = [None] * depth
    flat = lambda a: a.reshape(t_all, a.shape[-1])
    for i in reversed(range(depth)):
        sv = saved[i]
        dx1, h2, df, act_b, dgate, dup, st_f = ffn_bwd(dx, sv["x1"], sv["f"], mods[i], sv["n3"], sv["n4"],
                                                       wg_t[i], wu_t[i], w_d[i], nc)
        big["ffn_w_gate"][i] = tn_matmul(flat(dgate), flat(h2), f"grad_ffn_gate_{i}")
        big["ffn_w_up"][i] = tn_matmul(flat(dup), flat(h2), f"grad_ffn_up_{i}")
        big["ffn_w_down"][i] = tn_matmul(flat(act_b), flat(df), f"grad_ffn_down_{i}")
        dza, dzp, dys, cat, dm, gg, dr, st_m, dsw, dsb, dwbd = post_mix_bwd(dx1, sv["m"], sv["za"], sv["zp"], sv["ys"],
                                                                            mods[i], sv["cst"], nc)
        big["w_out"][i] = tn_matmul(flat(cat), flat(dm), "grad_w_out")
        big["glu_w"][i] = tn_matmul(flat(gg), flat(dr), "grad_glu_w")
        dzu, cot = ssm_backward(dys, sv["zu"], sv["hps"], sv["mats"], ncr)
        (grads["ssm_lam_re"][i], grads["ssm_lam_im"][i], grads["ssm_log_dt"][i], grads["ssm_b_re"][i],
         grads["ssm_b_im"][i], grads["ssm_c_re"][i], grads["ssm_c_im"][i], grads["ssm_d"][i]) = sv["ssm_vjp"](cot)
        dx, h1, dz, st_p = pre_mix_bwd(dza, dzu, dzp, sv["xs"], dx1, mods[i], sv["n1"], w_int[i], nc)
        big["w_in"][i] = tn_matmul(flat(dz), flat(h1), "grad_w_in")

        tiles = lambda st, row: st[:, :, row, :]
        allsum = lambda st, row: jnp.sum(tiles(st, row), axis=(0, 1))
        grads["norm_mix_pre"][i] = allsum(st_p, 2)
        grads["norm_mix_post"][i] = allsum(st_m, 1)
        grads["norm_ffn_pre"][i] = allsum(st_f, 3)
        grads["norm_ffn_post"][i] = allsum(st_f, 4)
        misc = allsum(st_m, 2)
        grads["glu_b"][i] = misc[:B_W]
        grads["pool_scale"][i] = misc[B_W:B_W + C_W]
        grads["sgu_w"][i] = dsw
        grads["sgu_b"][i] = jnp.sum(dsb.reshape(CHUNK, A_HEADS, A_W // A_HEADS), axis=2).T
        grads["pool_w"][i] = _block_diag_parts(dwbd, len(POOL_WINDOWS))
        six = jnp.stack([tiles(st_p, 0), tiles(st_p, 1), tiles(st_m, 0), tiles(st_f, 0), tiles(st_f, 1), tiles(st_f, 2)],
                        axis=2)
        d_lat = jnp.sum(six[:, nc:], axis=1).reshape(bl, 6 * D)
        d_ctx = jnp.sum(six[:, :nc], axis=(0, 1)).reshape(1, 6 * D)
        dmods[i] = jnp.concatenate([d_lat, d_ctx, jnp.zeros((8 - (bl + 1) % 8 if (bl + 1) % 8 else 0, 6 * D), F32)],
                                   axis=0)
    grad_x = dx[:, n_ctx:, :]

    dmod_local = jnp.stack(dmods)
    rd = dmod_local.shape[1]
    (dmod_all,) = all_gather_rows([dmod_local], "gather_dmod")
    dmod_cols = lax.dynamic_slice_in_dim(dmod_all, my_idx * wc, wc, axis=2).reshape(depth, N_DEV, rd, wc)
    d_lat_all = dmod_cols[:, :, :bl].reshape(depth, n_batch, wc)
    d_ctx_all = dmod_cols[:, 0, bl]
    for p in range(1, N_DEV):
        d_ctx_all = d_ctx_all + dmod_cols[:, p, bl]
    dmod_rows = jnp.concatenate([d_lat_all, d_ctx_all[:, None, :], jnp.zeros((depth, r_act - n_batch - 1, wc), F32)],
                                axis=1)
    dctx_rows = jnp.pad(d_ctx_all[:, None, :], ((0, 0), (0, 7), (0, 0)))
    g_w_mod, dact_ctx = mod_backward(act, dmod_rows, dctx_rows, w_mod)
    sig_c = jax.nn.sigmoid(c_ctx)
    dsilu_c = sig_c * (1.0 + c_ctx * (1.0 - sig_c))
    small_g = {k: (jnp.stack(grads[k]) if grads[k][0] is not None else None) for k in _SMALL}
    small_g["c_ctx"] = jnp.sum(dact_ctx[:, 0, :], axis=0) * dsilu_c
    small_g["b_mod"] = jnp.stack([jnp.sum(dmods[i][:bl + 1], axis=0) for i in range(depth)])

    packed_g = _pack(small_g)
    rows_s = packed_g.shape[0]
    (gathered,) = all_gather_rows([packed_g[None]], "gather_small_grads")
    gparts = gathered.reshape(N_DEV, 1, rows_s, D)
    small_w = {k: wts[k] for k in _SMALL}
    outs = adamw(_pack(small_w)[None], gparts, _pack({k: mom_m[k] for k in _SMALL})[None],
                 _pack({k: mom_v[k] for k in _SMALL})[None], "adamw_replicated")
    res = {k: [None] * 4 for k in _WEIGHTS}
    for slot, packed in enumerate(outs):
        un = _unpack(packed[0], small_w)
        for k in _SMALL:
            res[k][slot] = un[k]

    order = ("w_in", "w_out", "glu_w", "ffn_w_gate", "ffn_w_up", "ffn_w_down")
    landed = all_to_all_rows([jnp.stack(big[k]) for k in order], "scatter_weight_grads")
    for k, parts in zip(order, landed):
        transposed = k in ("w_in", "ffn_w_gate", "ffn_w_up")
        view = (lambda a: jnp.swapaxes(a, 1, 2)) if transposed else (lambda a: a)
        o4 = adamw(view(wts[k]), parts, view(mom_m[k]), view(mom_v[k]), "adamw_" + k)
        res[k] = [view(o) for o in o4]
    res["w_mod"] = list(adamw(w_mod, g_w_mod[None], m_w_mod, v_w_mod, "adamw_w_mod"))

    return (loss, grad_x, *[res[k][0] for k in _WEIGHTS], *[res[k][1] for k in _WEIGHTS],
            *[res[k][2] for k in _WEIGHTS], *[res[k][3] for k in _WEIGHTS])
```

```python
import functools
import math

import numpy as np
import jax
import jax.numpy as jnp
from jax import lax
from jax.experimental import pallas as pl
from jax.experimental.pallas import tpu as pltpu

F32 = jnp.float32
BF16 = jnp.bfloat16
HI = lax.Precision.HIGHEST
MESH = pl.DeviceIdType.MESH

D = 1024
D_IN = 1280
D_FF = 2816
A_W = 256
B_W = 512
C_W = 256
A_HEADS = 4
CHUNK = 128
SSM_G = 32
SSM_H = 16
SSM_P = 64
GRID_W = 64
POOL_WINDOWS = (2, 4, 8, 16)
EPS = 1e-6
N_DEV = 8

TT = 256
TC = 8
ROW_W = TC * B_W
FF_CHUNK = 256
VMEM_LIMIT = 56 * 1024 * 1024

ADAM_LR = 0.001
ADAM_B1 = 0.9
ADAM_B2 = 0.999
ADAM_EPS = 1e-08
ADAM_WD = 0.01
ADAM_STEP = 10


def _cp(sem):
    return pltpu.CompilerParams(dimension_semantics=sem, vmem_limit_bytes=VMEM_LIMIT)


def dot_nn(a, b):
    return jnp.dot(a, b, preferred_element_type=F32)


def dot_nt(a, b):
    return lax.dot_general(a, b, (((1,), (1,)), ((), ())), preferred_element_type=F32)


def dot_tn(a, b):
    return lax.dot_general(a, b, (((0,), (0,)), ((), ())), preferred_element_type=F32)


def split_bf16(x):
    hi = x.astype(BF16)
    lo = (x - hi.astype(F32)).astype(BF16)
    return hi, lo


def gelu(x):
    return jax.nn.gelu(x)


def gelu_grad(x):
    c = math.sqrt(2.0 / math.pi)
    t = jnp.tanh(c * (x + 0.044715 * x * x * x))
    return 0.5 * (1.0 + t) + 0.5 * x * (1.0 - t * t) * c * (1.0 + 3.0 * 0.044715 * x * x)


def rms_stats(x):
    r = lax.rsqrt(jnp.mean(x * x, axis=-1, keepdims=True) + EPS)
    return r, x * r


def rms_bwd(r, xn, dxn):
    return r * (dxn - xn * jnp.mean(dxn * xn, axis=-1, keepdims=True))


def colsum(x):
    return jnp.sum(x, axis=0, keepdims=True)


def lane_group(width, group):
    return lax.broadcasted_iota(jnp.int32, (1, width), 1) // group


def _tile_spec(width):
    return pl.BlockSpec((None, TT, width), lambda b, j: (b, j, 0))


def _mod_spec(nc):
    return pl.BlockSpec((None, None, 8, D), lambda b, j: (b, jnp.where(j >= nc, 1, 0), 0, 0))


def _full_spec(shape):
    zeros = (0,) * len(shape)
    return pl.BlockSpec(shape, lambda b, j: zeros)


def _kind_spec(shape, nc):
    zeros = (0,) * len(shape)
    return pl.BlockSpec((None,) + shape, lambda b, j: (jnp.where(j >= nc, 1, 0),) + zeros)


def _stat_spec():
    return pl.BlockSpec((None, None, 8, D), lambda b, j: (b, j, 0, 0))


def _chunk_spec():
    return pl.BlockSpec((None, TT // TC, ROW_W), lambda b, j: (b, j, 0))


def _rows_to_chunks(val, scratch, out_ref):
    for cb in range(B_W // 128):
        scratch[cb] = val[:, cb * 128:(cb + 1) * 128]
    for s in range(TC):
        for cb in range(B_W // 128):
            lo = s * B_W + cb * 128
            out_ref[:, lo:lo + 128] = scratch.at[cb][pl.ds(s, TT // TC, stride=TC), :]


def _chunks_to_rows(in_ref, scratch):
    for s in range(TC):
        for cb in range(B_W // 128):
            lo = s * B_W + cb * 128
            scratch.at[cb][pl.ds(s, TT // TC, stride=TC), :] = in_ref[:, lo:lo + 128]
    return jnp.concatenate([scratch[cb] for cb in range(B_W // 128)], axis=1)


def _chunk_scratch():
    return pltpu.VMEM((B_W // 128, TT, 128), F32)


def embed_tokens(x, ctx, pe):
    bl, seq, _ = x.shape
    nc = ctx.shape[1] // TT
    nt = nc + seq // TT

    def body(ctx_ref, x_ref, pe_ref, o_ref):
        j = pl.program_id(1)

        @pl.when(j < nc)
        def _():
            o_ref[...] = ctx_ref[...]

        @pl.when(j >= nc)
        def _():
            o_ref[...] = x_ref[...] + pe_ref[...]

    return pl.pallas_call(
        body, name="embed_tokens", grid=(bl, nt),
        in_specs=[pl.BlockSpec((None, TT, D), lambda b, j: (b, jnp.minimum(j, nc - 1), 0)),
                  pl.BlockSpec((None, TT, D), lambda b, j: (b, jnp.maximum(j - nc, 0), 0)),
                  pl.BlockSpec((TT, D), lambda b, j: (jnp.maximum(j - nc, 0), 0))],
        out_specs=_tile_spec(D),
        out_shape=jax.ShapeDtypeStruct((bl, nt * TT, D), F32),
        compiler_params=_cp(("arbitrary", "arbitrary")),
    )(ctx, x, pe)


def pre_mix(xs, mod, n1, w_int, nc):
    bl, s, _ = xs.shape

    def body(x_ref, mod_ref, n_ref, w_ref, za_ref, zu_ref, zp_ref, u_s):
        r, xn = rms_stats(x_ref[...])
        h = xn * n_ref[...] * (1.0 + mod_ref[1:2, :]) + mod_ref[0:1, :]
        z = dot_nt(h.astype(BF16), w_ref[...])
        za_ref[...] = z[:, :2 * A_W]
        _rows_to_chunks(z[:, 2 * A_W:2 * A_W + B_W], u_s, zu_ref)
        zp_ref[...] = z[:, 2 * A_W + B_W:]

    return pl.pallas_call(
        body, name="pre_mix", grid=(bl, s // TT),
        in_specs=[_tile_spec(D), _mod_spec(nc), _full_spec((1, D)), _full_spec((D_IN, D))],
        out_specs=[_tile_spec(2 * A_W), _chunk_spec(), _tile_spec(C_W)],
        out_shape=[jax.ShapeDtypeStruct((bl, s, 2 * A_W), F32), jax.ShapeDtypeStruct((bl, s // TC, ROW_W), F32),
                   jax.ShapeDtypeStruct((bl, s, C_W), F32)],
        scratch_shapes=[_chunk_scratch()],
        compiler_params=_cp(("arbitrary", "arbitrary")),
    )(xs, mod, n1, w_int)


def _seg_mean(x, seg_p):
    hi, lo = split_bf16(x)
    return dot_nn(hi, seg_p) + dot_nn(lo, seg_p)


def _sgu_forward(za, sw_ref, sbias, seg_p):
    ge = gelu(za)
    u, v = ge[:, :A_W], ge[:, A_W:]
    dv = v - _seg_mean(v, seg_p)
    rs = lax.rsqrt(_seg_mean(dv * dv, seg_p) + EPS)
    vn = dv * rs
    head = lane_group(A_W, A_W // A_HEADS)
    parts = []
    for c2 in range(TT // CHUNK):
        vb = vn[c2 * CHUNK:(c2 + 1) * CHUNK].astype(BF16)
        sc = sbias
        for h in range(A_HEADS):
            sc = sc + jnp.where(head == h, dot_nn(sw_ref[h], vb), 0.0)
        parts.append(sc)
    sg = jnp.concatenate(parts, axis=0)
    return u * sg, (u, vn, rs, sg)


def _pool_forward(zp, band_ref, icnt, wbd, pscale):
    hi, lo = split_bf16(zp)
    grp = lane_group(C_W, C_W // len(POOL_WINDOWS))
    q = jnp.zeros_like(zp)
    for i in range(len(POOL_WINDOWS)):
        t = dot_nn(band_ref[i], hi) + dot_nn(band_ref[i], lo)
        q = jnp.where(grp == i, t, q)
    q = q * icnt - zp
    o = dot_nn(q.astype(BF16), wbd)
    return o * pscale, (q, o)


def _glu_forward(y, glu_w, glu_b):
    g = gelu(y)
    sg = jax.nn.sigmoid(dot_nn(g.astype(BF16), glu_w) + glu_b)
    return g * sg, (g, sg)


_MIX_CONST_SHAPES = dict(sw=(A_HEADS, CHUNK, CHUNK), sbias=(CHUNK, A_W), seg_p=(A_W, A_W), wbd=(C_W, C_W),
                         pscale=(1, C_W), glu_w=(B_W, B_W), glu_b=(1, B_W), w_out=(D, D), n2=(1, D))


def _mix_const_specs(nc):
    return ([_full_spec(_MIX_CONST_SHAPES[k]) for k in ("sw", "sbias", "seg_p")]
            + [_kind_spec((len(POOL_WINDOWS), TT, TT), nc), _kind_spec((TT, C_W), nc)]
            + [_full_spec(_MIX_CONST_SHAPES[k]) for k in ("wbd", "pscale", "glu_w", "glu_b", "w_out", "n2")])


def _mix_const_args(cst):
    return [cst[k] for k in ("sw", "sbias", "seg_p", "band", "icnt", "wbd", "pscale", "glu_w", "glu_b", "w_out", "n2")]


def post_mix(xs, za, zp, ys, mod, cst, nc):
    bl, s, _ = xs.shape

    def body(x_ref, za_ref, zp_ref, y_ref, mod_ref, sw_ref, sbias_ref, seg_ref, band_ref, icnt_ref, wbd_ref,
             ps_ref, gw_ref, gb_ref, wo_ref, n2_ref, x1_ref, m_ref, y_s):
        a, _ = _sgu_forward(za_ref[...], sw_ref, sbias_ref[...], seg_ref[...])
        p, _ = _pool_forward(zp_ref[...], band_ref, icnt_ref[...], wbd_ref[...], ps_ref[...])
        sm, _ = _glu_forward(_chunks_to_rows(y_ref, y_s), gw_ref[...], gb_ref[...])
        cat = jnp.concatenate([a, sm, p], axis=1).astype(BF16)
        m = dot_nn(cat, wo_ref[...])
        _, mn = rms_stats(m)
        m_ref[...] = m
        x1_ref[...] = x_ref[...] + mod_ref[2:3, :] * (mn * n2_ref[...])

    return pl.pallas_call(
        body, name="post_mix", grid=(bl, s // TT),
        in_specs=[_tile_spec(D), _tile_spec(2 * A_W), _tile_spec(C_W), _chunk_spec(), _mod_spec(nc)]
        + _mix_const_specs(nc),
        out_specs=[_tile_spec(D), _tile_spec(D)],
        out_shape=[jax.ShapeDtypeStruct((bl, s, D), F32), jax.ShapeDtypeStruct((bl, s, D), F32)],
        scratch_shapes=[_chunk_scratch()],
        compiler_params=_cp(("arbitrary", "arbitrary")),
    )(xs, za, zp, ys, mod, *_mix_const_args(cst))


def post_mix_bwd(dx1, m, za, zp, ys, mod, cst, nc):
    bl, s, _ = m.shape
    nt = s // TT

    def body(dx_ref, m_ref, za_ref, zp_ref, y_ref, mod_ref, sw_ref, sbias_ref, seg_ref, band_ref, icnt_ref,
             wbd_ref, ps_ref, gw_ref, gb_ref, wo_ref, n2_ref,
             dza_ref, dzp_ref, dy_ref, cat_ref, dm_ref, gg_ref, dr_ref, st_ref, dsw_ref, dsb_ref, dwbd_ref, y_s):
        first = jnp.logical_and(pl.program_id(0) == 0, pl.program_id(1) == 0)

        @pl.when(first)
        def _():
            dsw_ref[...] = jnp.zeros_like(dsw_ref)
            dsb_ref[...] = jnp.zeros_like(dsb_ref)
            dwbd_ref[...] = jnp.zeros_like(dwbd_ref)

        seg_p = seg_ref[...]
        za = za_ref[...]
        zp_v = zp_ref[...]
        yv = _chunks_to_rows(y_ref, y_s)
        a, (u, vn, rs, sg) = _sgu_forward(za, sw_ref, sbias_ref[...], seg_p)
        p, (q, o) = _pool_forward(zp_v, band_ref, icnt_ref[...], wbd_ref[...], ps_ref[...])
        sm, (g, sig) = _glu_forward(yv, gw_ref[...], gb_ref[...])
        cat_ref[...] = jnp.concatenate([a, sm, p], axis=1).astype(BF16)

        dx = dx_ref[...]
        g1 = mod_ref[2:3, :]
        n2 = n2_ref[...]
        mv = m_ref[...]
        rm, mn = rms_stats(mv)
        st_ref[...] = jnp.zeros_like(st_ref)
        st_ref[0:1, :] = colsum(dx * (mn * n2))
        st_ref[1:2, :] = colsum(dx * g1 * mn)
        dm = rms_bwd(rm, mn, dx * g1 * n2)
        dmb = dm.astype(BF16)
        dm_ref[...] = dmb
        dcat = dot_nt(dmb, wo_ref[...])
        da, dsm, dp = dcat[:, :A_W], dcat[:, A_W:A_W + B_W], dcat[:, A_W + B_W:]

        du = da * sg
        dsv = da * u
        head = lane_group(A_W, A_W // A_HEADS)
        dvn_parts = []
        dsb_acc = jnp.zeros((CHUNK, A_W), F32)
        for c2 in range(TT // CHUNK):
            dsc = dsv[c2 * CHUNK:(c2 + 1) * CHUNK]
            dsc_b = dsc.astype(BF16)
            vb = vn[c2 * CHUNK:(c2 + 1) * CHUNK].astype(BF16)
            dsb_acc = dsb_acc + dsc
            dvn_c = jnp.zeros((CHUNK, A_W), F32)
            for h in range(A_HEADS):
                dsw_ref[h] += dot_nt(jnp.where(head == h, dsc, 0.0).astype(BF16), vb)
                dvn_c = dvn_c + jnp.where(head == h, dot_tn(sw_ref[h], dsc_b), 0.0)
            dvn_parts.append(dvn_c)
        dsb_ref[...] += dsb_acc
        dvn = jnp.concatenate(dvn_parts, axis=0)
        dv = rs * (dvn - _seg_mean(dvn, seg_p) - vn * _seg_mean(dvn * vn, seg_p))
        dza_ref[...] = jnp.concatenate([du, dv], axis=1) * gelu_grad(za)

        ps = ps_ref[...]
        do = dp * ps
        dps = colsum(dp * o)
        dob = do.astype(BF16)
        dwbd_ref[...] += dot_tn(q.astype(BF16), dob)
        dq = dot_nt(dob, wbd_ref[...])
        hi, lo = split_bf16(dq * icnt_ref[...])
        grp = lane_group(C_W, C_W // len(POOL_WINDOWS))
        dzp = -dq
        for i in range(len(POOL_WINDOWS)):
            t = dot_tn(band_ref[i], hi) + dot_tn(band_ref[i], lo)
            dzp = dzp + jnp.where(grp == i, t, 0.0)
        dzp_ref[...] = dzp

        dr = dsm * g * sig * (1.0 - sig)
        drb = dr.astype(BF16)
        dr_ref[...] = drb
        gg_ref[...] = g.astype(BF16)
        dg = dsm * sig + dot_nt(drb, gw_ref[...])
        _rows_to_chunks(dg * gelu_grad(yv), y_s, dy_ref)
        st_ref[2:3, :] = jnp.concatenate([colsum(dr), dps, jnp.zeros((1, D - B_W - C_W), F32)], axis=1)

    acc = lambda shape: pl.BlockSpec(shape, lambda b, j: (0,) * len(shape))
    return pl.pallas_call(
        body, name="post_mix_bwd", grid=(bl, nt),
        in_specs=[_tile_spec(D), _tile_spec(D), _tile_spec(2 * A_W), _tile_spec(C_W), _chunk_spec(), _mod_spec(nc)]
        + _mix_const_specs(nc),
        out_specs=[_tile_spec(2 * A_W), _tile_spec(C_W), _chunk_spec(), _tile_spec(D), _tile_spec(D),
                   _tile_spec(B_W), _tile_spec(B_W), _stat_spec(),
                   acc((A_HEADS, CHUNK, CHUNK)), acc((CHUNK, A_W)), acc((C_W, C_W))],
        out_shape=[jax.ShapeDtypeStruct((bl, s, 2 * A_W), F32), jax.ShapeDtypeStruct((bl, s, C_W), F32),
                   jax.ShapeDtypeStruct((bl, s // TC, ROW_W), F32), jax.ShapeDtypeStruct((bl, s, D), BF16),
                   jax.ShapeDtypeStruct((bl, s, D), BF16), jax.ShapeDtypeStruct((bl, s, B_W), BF16),
                   jax.ShapeDtypeStruct((bl, s, B_W), BF16), jax.ShapeDtypeStruct((bl, nt, 8, D), F32),
                   jax.ShapeDtypeStruct((A_HEADS, CHUNK, CHUNK), F32), jax.ShapeDtypeStruct((CHUNK, A_W), F32),
                   jax.ShapeDtypeStruct((C_W, C_W), F32)],
        scratch_shapes=[_chunk_scratch()],
        compiler_params=_cp(("arbitrary", "arbitrary")),
    )(dx1, m, za, zp, ys, mod, *_mix_const_args(cst))


def pre_mix_bwd(dza, dzu, dzp, xs, dxres, mod, n1, w_int, nc):
    bl, s, _ = xs.shape
    nt = s // TT

    def body(dza_ref, dzu_ref, dzp_ref, x_ref, dres_ref, mod_ref, n_ref, w_ref, dx_ref, h_ref, dz_ref, st_ref, u_s):
        dz = jnp.concatenate([dza_ref[...], _chunks_to_rows(dzu_ref, u_s), dzp_ref[...]], axis=1).astype(BF16)
        dz_ref[...] = dz
        dh = dot_nn(dz, w_ref[...])
        r, xn = rms_stats(x_ref[...])
        n1v = n_ref[...]
        sc = mod_ref[1:2, :]
        xg = xn * n1v
        h_ref[...] = (xg * (1.0 + sc) + mod_ref[0:1, :]).astype(BF16)
        dyv = dh * (1.0 + sc)
        st_ref[...] = jnp.zeros_like(st_ref)
        st_ref[0:1, :] = colsum(dh)
        st_ref[1:2, :] = colsum(dh * xg)
        st_ref[2:3, :] = colsum(dyv * xn)
        dx_ref[...] = dres_ref[...] + rms_bwd(r, xn, dyv * n1v)

    return pl.pallas_call(
        body, name="pre_mix_bwd", grid=(bl, nt),
        in_specs=[_tile_spec(2 * A_W), _chunk_spec(), _tile_spec(C_W), _tile_spec(D), _tile_spec(D), _mod_spec(nc),
                  _full_spec((1, D)), _full_spec((D_IN, D))],
        out_specs=[_tile_spec(D), _tile_spec(D), _tile_spec(D_IN), _stat_spec()],
        out_shape=[jax.ShapeDtypeStruct((bl, s, D), F32), jax.ShapeDtypeStruct((bl, s, D), BF16),
                   jax.ShapeDtypeStruct((bl, s, D_IN), BF16), jax.ShapeDtypeStruct((bl, nt, 8, D), F32)],
        scratch_shapes=[_chunk_scratch()],
        compiler_params=_cp(("arbitrary", "arbitrary")),
    )(dza, dzu, dzp, xs, dxres, mod, n1, w_int)


def _ffn_tile(s):
    return 768 if s % 768 == 0 else TT


def _ctx_rows(tf, n_ctx, j):
    return lax.broadcasted_iota(jnp.int32, (tf, 1), 0) + j * tf < n_ctx


def _mod_row(mod_ref, is_ctx, row):
    return jnp.where(is_ctx, mod_ref[0, row:row + 1, :], mod_ref[1, row:row + 1, :])


def ffn_fwd(x1, mod, n3, n4, wg_t, wu_t, wd, n_ctx):
    bl, s, _ = x1.shape
    tf = _ffn_tile(s)
    nk = D_FF // FF_CHUNK
    tile = pl.BlockSpec((None, tf, D), lambda b, j, k: (b, j, 0))
    modspec = pl.BlockSpec((None, 2, 8, D), lambda b, j, k: (b, 0, 0, 0))
    vec = pl.BlockSpec((1, D), lambda b, j, k: (0, 0))
    wspec = pl.BlockSpec((FF_CHUNK, D), lambda b, j, k: (k, 0))

    def body(x_ref, mod_ref, n3_ref, n4_ref, wg_ref, wu_ref, wd_ref, x2_ref, f_ref, h_s, acc_s):
        j, k = pl.program_id(1), pl.program_id(2)

        @pl.when(k == 0)
        def _():
            is_ctx = _ctx_rows(tf, n_ctx, j)
            _, xn = rms_stats(x_ref[...])
            h_s[...] = (xn * n3_ref[...] * (1.0 + _mod_row(mod_ref, is_ctx, 4)) + _mod_row(mod_ref, is_ctx, 3)).astype(BF16)
            acc_s[...] = jnp.zeros_like(acc_s)

        h = h_s[...]
        gate = dot_nt(h, wg_ref[...])
        up = dot_nt(h, wu_ref[...])
        act = (gate * jax.nn.sigmoid(gate)) * up
        acc_s[...] += dot_nn(act.astype(BF16), wd_ref[...])

        @pl.when(k == nk - 1)
        def _():
            f = acc_s[...]
            f_ref[...] = f
            _, fn = rms_stats(f)
            x2_ref[...] = x_ref[...] + _mod_row(mod_ref, _ctx_rows(tf, n_ctx, j), 5) * (fn * n4_ref[...])

    return pl.pallas_call(
        body, name="ffn_fwd", grid=(bl, s // tf, nk),
        in_specs=[tile, modspec, vec, vec, wspec, wspec, wspec],
        out_specs=[tile, tile],
        out_shape=[jax.ShapeDtypeStruct((bl, s, D), F32), jax.ShapeDtypeStruct((bl, s, D), F32)],
        scratch_shapes=[pltpu.VMEM((tf, D), BF16), pltpu.VMEM((tf, D), F32)],
        compiler_params=_cp(("arbitrary", "arbitrary", "arbitrary")),
    )(x1, mod, n3, n4, wg_t, wu_t, wd)


def ffn_bwd(dx2, x1, f, mod, n3, n4, wg_t, wu_t, wd, n_ctx):
    bl, s, _ = x1.shape
    tf = _ffn_tile(s)
    nt = s // tf
    nk = D_FF // FF_CHUNK
    tile = pl.BlockSpec((None, tf, D), lambda b, j, k: (b, j, 0))
    ftile = pl.BlockSpec((None, tf, FF_CHUNK), lambda b, j, k: (b, j, k))
    modspec = pl.BlockSpec((None, 2, 8, D), lambda b, j, k: (b, 0, 0, 0))
    vec = pl.BlockSpec((1, D), lambda b, j, k: (0, 0))
    wspec = pl.BlockSpec((FF_CHUNK, D), lambda b, j, k: (k, 0))
    stat = pl.BlockSpec((None, None, 8, D), lambda b, j, k: (b, j, 0, 0))

    def split_sum(is_ctx, v, st_ref, row):
        st_ref[row:row + 1, :] = colsum(jnp.where(is_ctx, 0.0, v))
        st_ref[row + 5:row + 6, :] = colsum(jnp.where(is_ctx, v, 0.0))

    def body(dx_ref, x_ref, f_ref, mod_ref, n3_ref, n4_ref, wg_ref, wu_ref, wd_ref,
             dx1_ref, h_ref, df_ref, act_ref, dgate_ref, dup_ref, st_ref, h_s, df_s, acc_s):
        j, k = pl.program_id(1), pl.program_id(2)

        @pl.when(k == 0)
        def _():
            is_ctx = _ctx_rows(tf, n_ctx, j)
            dx = dx_ref[...]
            g2 = _mod_row(mod_ref, is_ctx, 5)
            n4 = n4_ref[...]
            rf, fn = rms_stats(f_ref[...])
            st_ref[...] = jnp.zeros_like(st_ref)
            split_sum(is_ctx, dx * (fn * n4), st_ref, 2)
            st_ref[4:5, :] = colsum(dx * g2 * fn)
            df = rms_bwd(rf, fn, dx * g2 * n4).astype(BF16)
            df_s[...] = df
            df_ref[...] = df
            _, xn = rms_stats(x_ref[...])
            h = (xn * n3_ref[...] * (1.0 + _mod_row(mod_ref, is_ctx, 4)) + _mod_row(mod_ref, is_ctx, 3)).astype(BF16)
            h_s[...] = h
            h_ref[...] = h
            acc_s[...] = jnp.zeros_like(acc_s)

        h = h_s[...]
        gate = dot_nt(h, wg_ref[...])
        up = dot_nt(h, wu_ref[...])
        sg = jax.nn.sigmoid(gate)
        silu = gate * sg
        dact = dot_nt(df_s[...], wd_ref[...])
        act_ref[...] = (silu * up).astype(BF16)
        dgate = (dact * up * (sg * (1.0 + gate * (1.0 - sg)))).astype(BF16)
        dup = (dact * silu).astype(BF16)
        dgate_ref[...] = dgate
        dup_ref[...] = dup
        acc_s[...] += dot_nn(dgate, wg_ref[...]) + dot_nn(dup, wu_ref[...])

        @pl.when(k == nk - 1)
        def _():
            is_ctx = _ctx_rows(tf, n_ctx, j)
            dh = acc_s[...]
            r, xn = rms_stats(x_ref[...])
            n3 = n3_ref[...]
            sc = _mod_row(mod_ref, is_ctx, 4)
            xg = xn * n3
            dyv = dh * (1.0 + sc)
            split_sum(is_ctx, dh, st_ref, 0)
            split_sum(is_ctx, dh * xg, st_ref, 1)
            st_ref[3:4, :] = colsum(dyv * xn)
            dx1_ref[...] = dx_ref[...] + rms_bwd(r, xn, dyv * n3)

    return pl.pallas_call(
        body, name="ffn_bwd", grid=(bl, nt, nk),
        in_specs=[tile, tile, tile, modspec, vec, vec, wspec, wspec, wspec],
        out_specs=[tile, tile, tile, ftile, ftile, ftile, stat],
        out_shape=[jax.ShapeDtypeStruct((bl, s, D), F32), jax.ShapeDtypeStruct((bl, s, D), BF16),
                   jax.ShapeDtypeStruct((bl, s, D), BF16), jax.ShapeDtypeStruct((bl, s, D_FF), BF16),
                   jax.ShapeDtypeStruct((bl, s, D_FF), BF16), jax.ShapeDtypeStruct((bl, s, D_FF), BF16),
                   jax.ShapeDtypeStruct((bl, nt, 8, D), F32)],
        scratch_shapes=[pltpu.VMEM((tf, D), BF16), pltpu.VMEM((tf, D), BF16), pltpu.VMEM((tf, D), F32)],
        compiler_params=_cp(("arbitrary", "arbitrary", "arbitrary")),
    )(dx2, x1, f, mod, n3, n4, wg_t, wu_t, wd)


def loss_head(xs, target, nc):
    bl, s, _ = xs.shape
    nt = s // TT

    def body(x_ref, t_ref, dx_ref, l_ref):
        j = pl.program_id(1)

        @pl.when(j < nc)
        def _():
            dx_ref[...] = jnp.zeros_like(dx_ref)
            l_ref[...] = jnp.zeros_like(l_ref)

        @pl.when(j >= nc)
        def _():
            e = x_ref[...] - t_ref[...]
            dx_ref[...] = e * (1.0 / D)
            tok = jnp.mean(e * e, axis=-1, keepdims=True)
            l_ref[...] = jnp.zeros_like(l_ref) + 0.5 * jnp.sum(tok, axis=0, keepdims=True)

    return pl.pallas_call(
        body, name="loss_head", grid=(bl, nt),
        in_specs=[_tile_spec(D), pl.BlockSpec((None, TT, D), lambda b, j: (b, jnp.maximum(j - nc, 0), 0))],
        out_specs=[_tile_spec(D), pl.BlockSpec((None, None, 8, 128), lambda b, j: (b, j, 0, 0))],
        out_shape=[jax.ShapeDtypeStruct((bl, s, D), F32), jax.ShapeDtypeStruct((bl, nt, 8, 128), F32)],
        compiler_params=_cp(("arbitrary", "arbitrary")),
    )(xs, target)


def tn_matmul(a, b, name):
    t, ka = a.shape
    n = b.shape[1]
    tk = ka if ka <= 1408 else ka // 2
    tt = 512 if t % 512 == 0 else 256
    nsteps = t // tt

    def body(a_ref, b_ref, o_ref, acc_s):
        @pl.when(pl.program_id(1) == 0)
        def _():
            acc_s[...] = jnp.zeros_like(acc_s)

        acc_s[...] += dot_tn(a_ref[...], b_ref[...])

        @pl.when(pl.program_id(1) == nsteps - 1)
        def _():
            o_ref[...] = acc_s[...].astype(BF16)

    return pl.pallas_call(
        body, name=name, grid=(ka // tk, nsteps),
        in_specs=[pl.BlockSpec((tt, tk), lambda i, s: (s, i)), pl.BlockSpec((tt, n), lambda i, s: (s, 0))],
        out_specs=pl.BlockSpec((tk, n), lambda i, s: (i, 0)),
        out_shape=jax.ShapeDtypeStruct((ka, n), BF16),
        scratch_shapes=[pltpu.VMEM((tk, n), F32)],
        compiler_params=_cp(("arbitrary", "arbitrary")),
    )(a, b)


def _quarter(x, q):
    return jnp.concatenate([x[:, k * B_W + q * 128:k * B_W + (q + 1) * 128] for k in range(TC)], axis=1)


def qmm(inp, w, acc, name):
    r = inp.shape[0]
    rt = next(t for t in (128, 64, 32) if r % t == 0)

    def compute(i_ref, w_ref, o_ref, base):
        x = i_ref[...]
        for q in range(4):
            y = dot_nn(_quarter(x, q).astype(BF16), w_ref[q])
            for k in range(TC):
                lo = k * B_W + q * 128
                if base is None:
                    o_ref[:, lo:lo + 128] = y[:, k * 128:(k + 1) * 128]
                else:
                    o_ref[:, lo:lo + 128] = base[:, lo:lo + 128] + y[:, k * 128:(k + 1) * 128]

    row = pl.BlockSpec((rt, ROW_W), lambda i: (i, 0))
    wspec = pl.BlockSpec((4, 1024, 1024), lambda i: (0, 0, 0))
    if acc is None:
        def body(i_ref, w_ref, o_ref):
            compute(i_ref, w_ref, o_ref, None)
        ins, specs = (inp, w), [row, wspec]
    else:
        def body(i_ref, w_ref, a_ref, o_ref):
            compute(i_ref, w_ref, o_ref, a_ref[...])
        ins, specs = (inp, w, acc), [row, wspec, row]
    return pl.pallas_call(
        body, name=name, grid=(r // rt,), in_specs=specs, out_specs=row,
        out_shape=jax.ShapeDtypeStruct((r, ROW_W), F32),
        compiler_params=_cp(("arbitrary",)),
    )(*ins)


def qtn(a, b, name):
    r = a.shape[0]
    rt = next(t for t in (128, 64, 32) if r % t == 0)
    nsteps = r // rt

    def body(*refs):
        a_refs, b_refs, o_ref = refs[:TC], refs[TC:2 * TC], refs[2 * TC]

        @pl.when(pl.program_id(1) == 0)
        def _():
            o_ref[...] = jnp.zeros_like(o_ref)

        av = jnp.concatenate([x[...] for x in a_refs], axis=1).astype(BF16)
        bv = jnp.concatenate([x[...] for x in b_refs], axis=1).astype(BF16)
        o_ref[...] += dot_tn(av, bv)

    def col(k):
        return pl.BlockSpec((rt, 128), lambda q, s: (s, k * 4 + q))

    specs = [col(k) for k in range(TC)] * 2
    return pl.pallas_call(
        body, name=name, grid=(4, nsteps), in_specs=specs,
        out_specs=pl.BlockSpec((None, 1024, 1024), lambda q, s: (q, 0, 0)),
        out_shape=jax.ShapeDtypeStruct((4, 1024, 1024), F32),
        compiler_params=_cp(("arbitrary", "arbitrary")),
    )(*([a] * TC + [b] * TC))


def _scan_row(i, rb, ncr, reverse):
    if not reverse:
        return i
    return jnp.where(i < ncr, ncr - 1 - i, rb - 1 - (i - ncr))


def chunk_scan(xs, lam8, ncr, reverse, name):
    bl, rb, _ = xs.shape
    half = ROW_W // 2

    def body(x_ref, l_ref, hp_ref):
        ar, ai = l_ref[:, :half], l_ref[:, half:]

        def step(i, carry):
            hr, hi = carry
            row = _scan_row(i, rb, ncr, reverse)
            hp_ref[pl.ds(row, 1), :] = jnp.concatenate([hr, hi], axis=1)
            xv = x_ref[pl.ds(row, 1), :]
            return (ar * hr - ai * hi + xv[:, :half], ai * hr + ar * hi + xv[:, half:])

        lax.fori_loop(0, rb, step, (jnp.zeros((1, half), F32), jnp.zeros((1, half), F32)))

    blk = pl.BlockSpec((None, rb, ROW_W), lambda b: (b, 0, 0))
    return pl.pallas_call(
        body, name=name, grid=(bl,),
        in_specs=[blk, pl.BlockSpec((1, ROW_W), lambda b: (0, 0))], out_specs=blk,
        out_shape=jax.ShapeDtypeStruct((bl, rb, ROW_W), F32),
        compiler_params=_cp(("arbitrary",)),
    )(xs, lam8)


def chunk_scan_bwd(dhp, hp, lam8, ncr, reverse, name):
    bl, rb, _ = dhp.shape
    half = ROW_W // 2

    def body(d_ref, hp_ref, l_ref, g_ref, dl_ref):
        ar, ai = l_ref[:, :half], l_ref[:, half:]

        def step(n, carry):
            gr, gi, dar, dai = carry
            row = _scan_row(rb - 1 - n, rb, ncr, reverse)
            g_ref[pl.ds(row, 1), :] = jnp.concatenate([gr, gi], axis=1)
            dv = d_ref[pl.ds(row, 1), :]
            pv = hp_ref[pl.ds(row, 1), :]
            pr, pi = pv[:, :half], pv[:, half:]
            dar = dar + gr * pr + gi * pi
            dai = dai + gi * pr - gr * pi
            return (dv[:, :half] + ar * gr + ai * gi, dv[:, half:] + ar * gi - ai * gr, dar, dai)

        z = jnp.zeros((1, half), F32)
        _, _, dar, dai = lax.fori_loop(0, rb, step, (z, z, z, z))
        dl_ref[...] = jnp.zeros_like(dl_ref)
        dl_ref[0:1, :] = jnp.concatenate([dar, dai], axis=1)

    blk = pl.BlockSpec((None, rb, ROW_W), lambda b: (b, 0, 0))
    return pl.pallas_call(
        body, name=name, grid=(bl,),
        in_specs=[blk, blk, pl.BlockSpec((1, ROW_W), lambda b: (0, 0))],
        out_specs=[blk, pl.BlockSpec((None, 8, ROW_W), lambda b: (b, 0, 0))],
        out_shape=[jax.ShapeDtypeStruct((bl, rb, ROW_W), F32), jax.ShapeDtypeStruct((bl, 8, ROW_W), F32)],
        compiler_params=_cp(("arbitrary",)),
    )(dhp, hp, lam8)


def _state_layout(v):
    return v.reshape(4, 4, 2, SSM_P).transpose(1, 0, 2, 3).reshape(SSM_G * SSM_P)


def _quarter_rows(v):
    e = v.shape[0]
    return v.reshape(e, 4, 8, SSM_P, SSM_H).transpose(0, 1, 2, 4, 3).reshape(e, 4, 8 * SSM_H, SSM_P)


def _token_state_map(vr, vi):
    mask = jnp.asarray(np.kron(np.eye(8), np.ones((SSM_H, SSM_P))), F32)
    parts = [jnp.tile(_quarter_rows(v), (1, 1, 1, 8)) * mask for v in (vr, vi)]
    return jnp.concatenate(parts, axis=-1).transpose(1, 0, 2, 3).reshape(4, TC * 8 * SSM_H, 2 * 8 * SSM_P)


def ssm_build(lam_re, lam_im, log_dt, b_re, b_im, c_re, c_im, d):
    dt = jnp.exp(log_dt)[..., None]
    mag = jnp.exp(lam_re * dt)
    ang = lam_im * dt
    lr, li = mag * jnp.cos(ang), mag * jnp.sin(ang)
    den = lam_re * lam_re + lam_im * lam_im
    nr = lr - 1.0
    fr = (nr * lam_re + li * lam_im) / den
    fi = (li * lam_re - nr * lam_im) / den
    bbr = fr[..., None] * b_re - fi[..., None] * b_im
    bbi = fr[..., None] * b_im + fi[..., None] * b_re
    pr, pi = [jnp.ones_like(lr)], [jnp.zeros_like(lr)]
    for _ in range(TC):
        pr, pi = pr + [pr[-1] * lr - pi[-1] * li], pi + [pr[-1] * li + pi[-1] * lr]
    pr, pi = jnp.stack(pr), jnp.stack(pi)
    clr = c_re[None] * pr[:, :, :, None, :] - c_im[None] * pi[:, :, :, None, :]
    cli = c_re[None] * pi[:, :, :, None, :] + c_im[None] * pr[:, :, :, None, :]
    same_group = jnp.asarray(np.kron(np.eye(8), np.ones((SSM_H, SSM_H))), F32)
    ein = functools.partial(jnp.einsum, precision=HI)

    out, lag_blocks = {}, {}
    for k, name in ((0, "f"), (1, "r")):
        ar, ai = _quarter_rows(bbr[k][None])[0], _quarter_rows(bbi[k][None])[0]
        cr = clr[:TC, k].reshape(TC, 4, 8 * SSM_H, SSM_P)
        ci = cli[:TC, k].reshape(TC, 4, 8 * SSM_H, SSM_P)
        lag_blocks[k] = (ein('qap,nqbp->nqab', ar, cr) - ein('qap,nqbp->nqab', ai, ci)) * same_group
        es = [TC - 1 - s for s in range(TC)] if k == 0 else list(range(TC))
        sr = jnp.stack([pr[e, k][:, :, None] * bbr[k] - pi[e, k][:, :, None] * bbi[k] for e in es])
        si = jnp.stack([pr[e, k][:, :, None] * bbi[k] + pi[e, k][:, :, None] * bbr[k] for e in es])
        out["bs_" + name] = _token_state_map(sr, si)
        et = [t + 1 for t in range(TC)] if k == 0 else [TC - t for t in range(TC)]
        crt = jnp.stack([jnp.swapaxes(clr[e, k], 1, 2) for e in et])
        cit = jnp.stack([-jnp.swapaxes(cli[e, k], 1, 2) for e in et])
        out["cst_" + name] = _token_state_map(crt, cit)
        out["lam8_" + name] = jnp.concatenate([_state_layout(pr[TC, k]), _state_layout(pi[TC, k])])[None, :]
    skip = jnp.eye(8 * SSM_H, dtype=F32)[None] * d.reshape(4, 1, 8 * SSM_H)
    center = lag_blocks[0][0] + lag_blocks[1][0] + skip

    def block(s, t):
        return center if s == t else (lag_blocks[0][t - s] if t > s else lag_blocks[1][s - t])

    out["m"] = jnp.concatenate([jnp.concatenate([block(s, t) for t in range(TC)], axis=-1) for s in range(TC)], axis=1)
    return out


def ssm_forward(u3, mats, ncr):
    bl, rb, _ = u3.shape
    u = u3.reshape(bl * rb, ROW_W)
    hps = {}
    y = qmm(u, mats["m"].astype(BF16), None, "ssm_intra")
    for dname, rev in (("f", False), ("r", True)):
        xs = qmm(u, mats["bs_" + dname].astype(BF16), None, "ssm_state_in_" + dname)
        hp = chunk_scan(xs.reshape(bl, rb, ROW_W), mats["lam8_" + dname], ncr, rev, "ssm_scan_" + dname)
        hps[dname] = hp.reshape(bl * rb, ROW_W)
        y = qmm(hps[dname], jnp.swapaxes(mats["cst_" + dname], 1, 2).astype(BF16), y, "ssm_readout_" + dname)
    return y.reshape(bl, rb, ROW_W), hps


def ssm_backward(dy3, u3, hps, mats, ncr):
    bl, rb, _ = u3.shape
    u = u3.reshape(bl * rb, ROW_W)
    dyr = dy3.reshape(bl * rb, ROW_W)
    bft = lambda k: jnp.swapaxes(mats[k], 1, 2).astype(BF16)
    cot = {"m": qtn(u, dyr, "ssm_d_intra")}
    du = qmm(dyr, bft("m"), None, "ssm_du_intra")
    for dname, rev in (("f", False), ("r", True)):
        dhp = qmm(dyr, mats["cst_" + dname].astype(BF16), None, "ssm_dstate_" + dname)
        g, dl = chunk_scan_bwd(dhp.reshape(bl, rb, ROW_W), hps[dname].reshape(bl, rb, ROW_W), mats["lam8_" + dname],
                               ncr, rev, "ssm_scan_bwd_" + dname)
        g = g.reshape(bl * rb, ROW_W)
        cot["lam8_" + dname] = jnp.sum(dl[:, 0:1, :], axis=0)
        cot["bs_" + dname] = qtn(u, g, "ssm_d_state_in_" + dname)
        cot["cst_" + dname] = qtn(dyr, hps[dname], "ssm_d_readout_" + dname)
        du = qmm(g, bft("bs_" + dname), du, "ssm_du_state_" + dname)
    return du.reshape(bl, rb, ROW_W), cot


def mod_forward(act, w_mod, b_cols):
    nl, _, wc = w_mod.shape
    r = act.shape[0]

    def body(a_ref, w_ref, b_ref, o_ref):
        o_ref[...] = dot_nn(a_ref[...].astype(BF16), w_ref[...].astype(BF16)) + b_ref[...]

    return pl.pallas_call(
        body, name="mod_forward", grid=(nl,),
        in_specs=[pl.BlockSpec((r, D), lambda l: (0, 0)), pl.BlockSpec((None, D, wc), lambda l: (l, 0, 0)),
                  pl.BlockSpec((None, 1, wc), lambda l: (l, 0, 0))],
        out_specs=pl.BlockSpec((None, r, wc), lambda l: (l, 0, 0)),
        out_shape=jax.ShapeDtypeStruct((nl, r, wc), F32),
        compiler_params=_cp(("arbitrary",)),
    )(act, w_mod, b_cols)


def mod_backward(act, dmod, dctx, w_mod):
    nl, _, wc = w_mod.shape
    r = act.shape[0]

    def body(a_ref, d_ref, c_ref, w_ref, gw_ref, gc_ref):
        gw_ref[...] = dot_tn(a_ref[...].astype(BF16), d_ref[...].astype(BF16))
        gc_ref[...] = dot_nt(c_ref[...].astype(BF16), w_ref[...].astype(BF16))

    return pl.pallas_call(
        body, name="mod_backward", grid=(nl,),
        in_specs=[pl.BlockSpec((r, D), lambda l: (0, 0)), pl.BlockSpec((None, r, wc), lambda l: (l, 0, 0)),
                  pl.BlockSpec((None, 8, wc), lambda l: (l, 0, 0)), pl.BlockSpec((None, D, wc), lambda l: (l, 0, 0))],
        out_specs=[pl.BlockSpec((None, D, wc), lambda l: (l, 0, 0)), pl.BlockSpec((None, 8, D), lambda l: (l, 0, 0))],
        out_shape=[jax.ShapeDtypeStruct((nl, D, wc), F32), jax.ShapeDtypeStruct((nl, 8, D), F32)],
        compiler_params=_cp(("arbitrary",)),
    )(act, dmod, dctx, w_mod)


def _place():
    return lax.axis_index("x"), lax.axis_index("y"), lax.axis_index("c")


def all_gather_rows(arrs, name):
    n = len(arrs)
    rs = [a.shape[1] for a in arrs]

    def body(*refs):
        x_refs, o_refs = refs[:n], refs[n:2 * n]
        send_sems, recv_sems, local_sems = refs[2 * n:]
        x, y, c = _place()
        me, sibling = (x, y, c), (x, y, 1 - c)
        chips = [(1 - x, y), (x, 1 - y), (1 - x, 1 - y)]

        def rows(a, px, py, pc):
            return o_refs[a].at[:, pl.ds((4 * px + 2 * py + pc) * rs[a], rs[a]), :]

        def copy(a, k, block, to, src=None):
            return pltpu.make_async_remote_copy(
                src_ref=rows(a, *block) if src is None else src, dst_ref=rows(a, *block),
                send_sem=send_sems.at[a, k], recv_sem=recv_sems.at[a, k], device_id=to, device_id_type=MESH)

        mine = [pltpu.make_async_copy(x_refs[a], rows(a, *me), local_sems.at[a]) for a in range(n)]
        for cp in mine:
            cp.start()
        first = []
        for a in range(n):
            first.append(copy(a, 0, me, sibling, src=x_refs[a]))
            first += [copy(a, 1 + j, me, (*chip, c), src=x_refs[a]) for j, chip in enumerate(chips)]
        for cp in first:
            cp.start()
        passed = []
        for j, chip in enumerate(chips):
            for a in range(n):
                copy(a, 1 + j, (*chip, c), me).wait_recv()
                fwd = copy(a, 4 + j, (*chip, c), sibling)
                fwd.start()
                passed.append(fwd)
        for a in range(n):
            copy(a, 0, sibling, me).wait_recv()
            for j, chip in enumerate(chips):
                copy(a, 4 + j, (*chip, 1 - c), me).wait_recv()
        for cp in first + passed:
            cp.wait_send()
        for cp in mine:
            cp.wait()

    any_spec = pl.BlockSpec(memory_space=pl.ANY)
    return pl.pallas_call(
        body, name=name,
        in_specs=[any_spec] * n, out_specs=[any_spec] * n,
        out_shape=[jax.ShapeDtypeStruct((a.shape[0], N_DEV * a.shape[1], a.shape[2]), a.dtype) for a in arrs],
        scratch_shapes=[pltpu.SemaphoreType.DMA((n, 7)), pltpu.SemaphoreType.DMA((n, 7)), pltpu.SemaphoreType.DMA((n,))],
    )(*arrs)


def all_to_all_rows(arrs, name):
    n = len(arrs)
    rs = [a.shape[1] // N_DEV for a in arrs]
    flips = [(fx, fy, fc) for fx in (0, 1) for fy in (0, 1) for fc in (0, 1)][1:]

    def body(*refs):
        x_refs, o_refs = refs[:n], refs[n:2 * n]
        send_sems, recv_sems, local_sems = refs[2 * n:]
        x, y, c = _place()
        my_idx = 4 * x + 2 * y + c

        def block(a, idx):
            return x_refs[a].at[:, pl.ds(idx * rs[a], rs[a]), :]

        mine = [pltpu.make_async_copy(block(a, my_idx), o_refs[a].at[my_idx], local_sems.at[a]) for a in range(n)]
        for cp in mine:
            cp.start()
        sends = []
        for k, (fx, fy, fc) in enumerate(flips):
            px = 1 - x if fx else x
            py = 1 - y if fy else y
            pc = 1 - c if fc else c
            p_idx = 4 * px + 2 * py + pc
            for a in range(n):
                sends.append(pltpu.make_async_remote_copy(
                    src_ref=block(a, p_idx), dst_ref=o_refs[a].at[my_idx], send_sem=send_sems.at[a, k],
                    recv_sem=recv_sems.at[a, k], device_id=(px, py, pc), device_id_type=MESH))
        for cp in sends:
            cp.start()
        for k, (fx, fy, fc) in enumerate(flips):
            px = 1 - x if fx else x
            py = 1 - y if fy else y
            pc = 1 - c if fc else c
            p_idx = 4 * px + 2 * py + pc
            for a in range(n):
                pltpu.make_async_remote_copy(
                    src_ref=block(a, p_idx), dst_ref=o_refs[a].at[p_idx], send_sem=send_sems.at[a, k],
                    recv_sem=recv_sems.at[a, k], device_id=(px, py, pc), device_id_type=MESH).wait_recv()
        for cp in sends:
            cp.wait_send()
        for cp in mine:
            cp.wait()

    any_spec = pl.BlockSpec(memory_space=pl.ANY)
    return pl.pallas_call(
        body, name=name,
        in_specs=[any_spec] * n, out_specs=[any_spec] * n,
        out_shape=[jax.ShapeDtypeStruct((N_DEV, a.shape[0], r, a.shape[2]), a.dtype) for a, r in zip(arrs, rs)],
        scratch_shapes=[pltpu.SemaphoreType.DMA((n, 7)), pltpu.SemaphoreType.DMA((n, 7)), pltpu.SemaphoreType.DMA((n,))],
    )(*arrs)


def _row_tile(rows, cap):
    best = None
    for t in range(16, min(rows, cap) + 1, 16):
        if rows % t == 0:
            best = t
    return rows if best is None else best


def adamw(w, gparts, m, v, name):
    n, nl, ra, cb = gparts.shape
    ta = _row_tile(ra, max(8, (1 << 19) // (cb * n)))

    def body(w_ref, g_ref, m_ref, v_ref, go_ref, d_ref, mo_ref, vo_ref):
        g = g_ref[0].astype(F32)
        for p in range(1, n):
            g = g + g_ref[p].astype(F32)
        mn = ADAM_B1 * m_ref[...] + (1.0 - ADAM_B1) * g
        vn = ADAM_B2 * v_ref[...] + (1.0 - ADAM_B2) * jnp.square(g)
        m_hat = mn / (1.0 - ADAM_B1 ** ADAM_STEP)
        v_hat = vn / (1.0 - ADAM_B2 ** ADAM_STEP)
        go_ref[...] = g
        d_ref[...] = -ADAM_LR * (m_hat / (jnp.sqrt(v_hat) + ADAM_EPS) + ADAM_WD * w_ref[...])
        mo_ref[...] = mn
        vo_ref[...] = vn

    blk = pl.BlockSpec((None, ta, cb), lambda l, i: (l, i, 0))
    gblk = pl.BlockSpec((n, None, ta, cb), lambda l, i: (0, l, i, 0))
    shp = jax.ShapeDtypeStruct((nl, ra, cb), F32)
    return pl.pallas_call(
        body, name=name, grid=(nl, ra // ta),
        in_specs=[blk, gblk, blk, blk], out_specs=[blk] * 4, out_shape=[shp] * 4,
        compiler_params=_cp(("arbitrary", "arbitrary")),
    )(w, gparts, m, v)


def _sincos_2d(rows, cols, dim):
    quarter = dim // 4
    omega = 1.0 / (10000.0 ** (jnp.arange(quarter, dtype=F32) / quarter))
    r = jnp.arange(rows, dtype=F32)[:, None] * omega
    cc = jnp.arange(cols, dtype=F32)[:, None] * omega
    er = jnp.concatenate([jnp.sin(r), jnp.cos(r)], axis=-1)
    ec = jnp.concatenate([jnp.sin(cc), jnp.cos(cc)], axis=-1)
    pe = jnp.concatenate([jnp.broadcast_to(er[:, None, :], (rows, cols, dim // 2)),
                          jnp.broadcast_to(ec[None, :, :], (rows, cols, dim // 2))], axis=-1)
    return pe.reshape(rows * cols, dim)


def _pool_constants():
    nw = len(POOL_WINDOWS)
    band = np.zeros((2, nw, TT, TT), np.float32)
    icnt = np.zeros((2, TT, C_W), np.float32)
    for kind, n in ((0, TT), (1, GRID_W)):
        for i, w in enumerate(POOL_WINDOWS):
            for t in range(TT):
                base, tl = (t // n) * n, t % n
                lo = min(max(tl - w // 2, 0), n)
                hi = min(max(tl - w // 2 + w, 0), n)
                band[kind, i, t, base + lo:base + hi] = 1.0
                icnt[kind, t, i * (C_W // nw):(i + 1) * (C_W // nw)] = 1.0 / (hi - lo)
    return jnp.asarray(band, BF16), jnp.asarray(icnt, F32)


def _block_diag(blocks):
    n, a, _ = blocks.shape
    return jnp.einsum('gab,gh->gahb', blocks, jnp.eye(n, dtype=F32), precision=HI).reshape(n * a, n * a)


def _block_diag_parts(mat, n):
    a = mat.shape[0] // n
    m4 = mat.reshape(n, a, n, a)
    return jnp.stack([m4[g, :, g, :] for g in range(n)])


_SMALL = ("c_ctx", "b_mod", "norm_mix_pre", "norm_mix_post", "norm_ffn_pre", "norm_ffn_post", "sgu_w", "sgu_b",
          "ssm_lam_re", "ssm_lam_im", "ssm_log_dt", "ssm_b_re", "ssm_b_im", "ssm_c_re", "ssm_c_im", "ssm_d",
          "glu_b", "pool_w", "pool_scale")
_WEIGHTS = ("c_ctx", "w_mod", "b_mod", "norm_mix_pre", "norm_mix_post", "norm_ffn_pre", "norm_ffn_post", "w_in", "w_out",
            "sgu_w", "sgu_b", "ssm_lam_re", "ssm_lam_im", "ssm_log_dt", "ssm_b_re", "ssm_b_im", "ssm_c_re", "ssm_c_im",
            "ssm_d", "glu_w", "glu_b", "pool_w", "pool_scale", "ffn_w_gate", "ffn_w_up", "ffn_w_down")


def _pack_rows(a):
    flat = a.reshape(-1)
    rows = -(-flat.shape[0] // D)
    rows8 = -(-rows // 8) * 8
    return jnp.pad(flat, (0, rows8 * D - flat.shape[0])).reshape(rows8, D)


def _pack(tree):
    packed = jnp.concatenate([_pack_rows(tree[k]) for k in _SMALL], axis=0)
    return jnp.pad(packed, ((0, -packed.shape[0] % 64), (0, 0)))


def _unpack(packed, like):
    out, at = {}, 0
    for k in _SMALL:
        size = int(np.prod(like[k].shape))
        rows8 = -(-(-(-size // D)) // 8) * 8
        out[k] = packed[at:at + rows8].reshape(-1)[:size].reshape(like[k].shape)
        at += rows8
    return out


def kernel(x, c, ctx, c_ctx, w_mod, b_mod, norm_mix_pre, norm_mix_post, norm_ffn_pre, norm_ffn_post, w_in, w_out, sgu_w, sgu_b, ssm_lam_re, ssm_lam_im, ssm_log_dt, ssm_b_re, ssm_b_im, ssm_c_re, ssm_c_im, ssm_d, glu_w, glu_b, pool_w, pool_scale, ffn_w_gate, ffn_w_up, ffn_w_down, loss_target, m_c_ctx, m_w_mod, m_b_mod, m_norm_mix_pre, m_norm_mix_post, m_norm_ffn_pre, m_norm_ffn_post, m_w_in, m_w_out, m_sgu_w, m_sgu_b, m_ssm_lam_re, m_ssm_lam_im, m_ssm_log_dt, m_ssm_b_re, m_ssm_b_im, m_ssm_c_re, m_ssm_c_im, m_ssm_d, m_glu_w, m_glu_b, m_pool_w, m_pool_scale, m_ffn_w_gate, m_ffn_w_up, m_ffn_w_down, v_c_ctx, v_w_mod, v_b_mod, v_norm_mix_pre, v_norm_mix_post, v_norm_ffn_pre, v_norm_ffn_post, v_w_in, v_w_out, v_sgu_w, v_sgu_b, v_ssm_lam_re, v_ssm_lam_im, v_ssm_log_dt, v_ssm_b_re, v_ssm_b_im, v_ssm_c_re, v_ssm_c_im, v_ssm_d, v_glu_w, v_glu_b, v_pool_w, v_pool_scale, v_ffn_w_gate, v_ffn_w_up, v_ffn_w_down):
    wts = dict(c_ctx=c_ctx, w_mod=w_mod, b_mod=b_mod, norm_mix_pre=norm_mix_pre, norm_mix_post=norm_mix_post,
               norm_ffn_pre=norm_ffn_pre, norm_ffn_post=norm_ffn_post, w_in=w_in, w_out=w_out, sgu_w=sgu_w, sgu_b=sgu_b,
               ssm_lam_re=ssm_lam_re, ssm_lam_im=ssm_lam_im, ssm_log_dt=ssm_log_dt, ssm_b_re=ssm_b_re, ssm_b_im=ssm_b_im,
               ssm_c_re=ssm_c_re, ssm_c_im=ssm_c_im, ssm_d=ssm_d, glu_w=glu_w, glu_b=glu_b, pool_w=pool_w,
               pool_scale=pool_scale, ffn_w_gate=ffn_w_gate, ffn_w_up=ffn_w_up, ffn_w_down=ffn_w_down)
    mom_m = dict(c_ctx=m_c_ctx, w_mod=m_w_mod, b_mod=m_b_mod, norm_mix_pre=m_norm_mix_pre, norm_mix_post=m_norm_mix_post,
                 norm_ffn_pre=m_norm_ffn_pre, norm_ffn_post=m_norm_ffn_post, w_in=m_w_in, w_out=m_w_out, sgu_w=m_sgu_w,
                 sgu_b=m_sgu_b, ssm_lam_re=m_ssm_lam_re, ssm_lam_im=m_ssm_lam_im, ssm_log_dt=m_ssm_log_dt,
                 ssm_b_re=m_ssm_b_re, ssm_b_im=m_ssm_b_im, ssm_c_re=m_ssm_c_re, ssm_c_im=m_ssm_c_im, ssm_d=m_ssm_d,
                 glu_w=m_glu_w, glu_b=m_glu_b, pool_w=m_pool_w, pool_scale=m_pool_scale, ffn_w_gate=m_ffn_w_gate,
                 ffn_w_up=m_ffn_w_up, ffn_w_down=m_ffn_w_down)
    mom_v = dict(c_ctx=v_c_ctx, w_mod=v_w_mod, b_mod=v_b_mod, norm_mix_pre=v_norm_mix_pre, norm_mix_post=v_norm_mix_post,
                 norm_ffn_pre=v_norm_ffn_pre, norm_ffn_post=v_norm_ffn_post, w_in=v_w_in, w_out=v_w_out, sgu_w=v_sgu_w,
                 sgu_b=v_sgu_b, ssm_lam_re=v_ssm_lam_re, ssm_lam_im=v_ssm_lam_im, ssm_log_dt=v_ssm_log_dt,
                 ssm_b_re=v_ssm_b_re, ssm_b_im=v_ssm_b_im, ssm_c_re=v_ssm_c_re, ssm_c_im=v_ssm_c_im, ssm_d=v_ssm_d,
                 glu_w=v_glu_w, glu_b=v_glu_b, pool_w=v_pool_w, pool_scale=v_pool_scale, ffn_w_gate=v_ffn_w_gate,
                 ffn_w_up=v_ffn_w_up, ffn_w_down=v_ffn_w_down)

    bl, seq, _ = x.shape
    n_ctx = ctx.shape[1]
    assert n_ctx == TT and seq % TT == 0 and seq % GRID_W == 0
    depth = w_in.shape[0]
    nc = n_ctx // TT
    ncr = n_ctx // TC
    s_all = n_ctx + seq
    nt = s_all // TT
    t_all = bl * s_all
    n_batch = bl * N_DEV
    my_idx = 4 * lax.axis_index("x") + 2 * lax.axis_index("y") + lax.axis_index("c")
    wc = w_mod.shape[2]

    c_rows = jnp.pad(c, ((0, 8 - bl), (0, 0))) if bl < 8 else c
    rc = c_rows.shape[0]
    (c_all,) = all_gather_rows([c_rows[None]], "gather_c")
    c_all = c_all[0].reshape(N_DEV, rc, D)[:, :bl].reshape(n_batch, D)
    r_act = -(-(n_batch + 1) // 16) * 16
    pre_act = jnp.concatenate([c_all, c_ctx[None, :], jnp.zeros((r_act - n_batch - 1, D), F32)], axis=0)
    act = jax.nn.silu(pre_act)
    b_cols = lax.dynamic_slice_in_dim(b_mod, my_idx * wc, wc, axis=1)[:, None, :]
    mod_cols = mod_forward(act, w_mod, b_cols)
    (mod_all,) = all_gather_rows([mod_cols], "gather_mod")
    mod_all = mod_all.reshape(depth, N_DEV, r_act, wc).transpose(0, 2, 1, 3).reshape(depth, r_act, 6, D)
    mod_lat = lax.dynamic_slice_in_dim(mod_all, my_idx * bl, bl, axis=1)
    mod_ctx = jnp.broadcast_to(mod_all[:, n_batch:n_batch + 1], (depth, bl, 6, D))
    mods = jnp.pad(jnp.stack([mod_ctx, mod_lat], axis=2), ((0, 0), (0, 0), (0, 0), (0, 2), (0, 0)))

    tr = lambda a: jnp.swapaxes(a, 1, 2).astype(BF16)
    w_int, w_o, g_w, wg_t, wu_t, w_d = all_gather_rows(
        [tr(w_in), w_out.astype(BF16), glu_w.astype(BF16), tr(ffn_w_gate), tr(ffn_w_up), ffn_w_down.astype(BF16)],
        "gather_weights")

    band, icnt = _pool_constants()
    seg_p = jnp.asarray(np.kron(np.eye(A_HEADS), np.full((A_W // A_HEADS,) * 2, A_HEADS / A_W)), BF16)
    pe = _sincos_2d(seq // GRID_W, GRID_W, D)
    xs = embed_tokens(x, ctx, pe)

    saved = []
    for i in range(depth):
        mats, ssm_vjp = jax.vjp(ssm_build, ssm_lam_re[i], ssm_lam_im[i], ssm_log_dt[i], ssm_b_re[i], ssm_b_im[i],
                                ssm_c_re[i], ssm_c_im[i], ssm_d[i])
        cst = dict(sw=sgu_w[i].astype(BF16),
                   sbias=jnp.repeat(sgu_b[i].T, A_W // A_HEADS, axis=1),
                   seg_p=seg_p, band=band, icnt=icnt, wbd=_block_diag(pool_w[i]).astype(BF16),
                   pscale=pool_scale[i][None, :], glu_w=g_w[i], glu_b=glu_b[i][None, :], w_out=w_o[i],
                   n2=norm_mix_post[i][None, :])
        n1, n3, n4 = norm_mix_pre[i][None, :], norm_ffn_pre[i][None, :], norm_ffn_post[i][None, :]
        za, zu, zp = pre_mix(xs, mods[i], n1, w_int[i], nc)
        ys, hps = ssm_forward(zu, mats, ncr)
        x1, m_pre = post_mix(xs, za, zp, ys, mods[i], cst, nc)
        x2, f_pre = ffn_fwd(x1, mods[i], n3, n4, wg_t[i], wu_t[i], w_d[i], n_ctx)
        saved.append(dict(xs=xs, za=za, zu=zu, zp=zp, ys=ys, hps=hps, x1=x1, m=m_pre, f=f_pre, cst=cst, mats=mats,
                          ssm_vjp=ssm_vjp, n1=n1, n3=n3, n4=n4))
        xs = x2

    dx, loss_parts = loss_head(xs, loss_target, nc)
    loss = lax.psum(jnp.sum(loss_parts[:, :, 0, 0]), ("x", "y", "c"))

    grads = {k: [None] * depth for k in _WEIGHTS}
    big = {k: [None] * depth for k in ("w_in", "w_out", "glu_w", "ffn_w_gate", "ffn_w_up", "ffn_w_down")}
    dmods = [None] * depth
    flat = lambda a: a.reshape(t_all, a.shape[-1])
    for i in reversed(range(depth)):
        sv = saved[i]
        dx1, h2, df, act_b, dgate, dup, st_f = ffn_bwd(dx, sv["x1"], sv["f"], mods[i], sv["n3"], sv["n4"],
                                                       wg_t[i], wu_t[i], w_d[i], n_ctx)
        big["ffn_w_gate"][i] = tn_matmul(flat(dgate), flat(h2), f"grad_ffn_gate_{i}")
        big["ffn_w_up"][i] = tn_matmul(flat(dup), flat(h2), f"grad_ffn_up_{i}")
        big["ffn_w_down"][i] = tn_matmul(flat(act_b), flat(df), f"grad_ffn_down_{i}")
        dza, dzp, dys, cat, dm, gg, dr, st_m, dsw, dsb, dwbd = post_mix_bwd(dx1, sv["m"], sv["za"], sv["zp"], sv["ys"],
                                                                            mods[i], sv["cst"], nc)
        big["w_out"][i] = tn_matmul(flat(cat), flat(dm), f"grad_w_out_{i}")
        big["glu_w"][i] = tn_matmul(flat(gg), flat(dr), f"grad_glu_w_{i}")
        dzu, cot = ssm_backward(dys, sv["zu"], sv["hps"], sv["mats"], ncr)
        (grads["ssm_lam_re"][i], grads["ssm_lam_im"][i], grads["ssm_log_dt"][i], grads["ssm_b_re"][i],
         grads["ssm_b_im"][i], grads["ssm_c_re"][i], grads["ssm_c_im"][i], grads["ssm_d"][i]) = sv["ssm_vjp"](cot)
        dx, h1, dz, st_p = pre_mix_bwd(dza, dzu, dzp, sv["xs"], dx1, mods[i], sv["n1"], w_int[i], nc)
        big["w_in"][i] = tn_matmul(flat(dz), flat(h1), f"grad_w_in_{i}")

        tiles = lambda st, row: st[:, :, row, :]
        allsum = lambda st, row: jnp.sum(tiles(st, row), axis=(0, 1))
        grads["norm_mix_pre"][i] = allsum(st_p, 2)
        grads["norm_mix_post"][i] = allsum(st_m, 1)
        grads["norm_ffn_pre"][i] = allsum(st_f, 3)
        grads["norm_ffn_post"][i] = allsum(st_f, 4)
        misc = allsum(st_m, 2)
        grads["glu_b"][i] = misc[:B_W]
        grads["pool_scale"][i] = misc[B_W:B_W + C_W]
        grads["sgu_w"][i] = dsw
        grads["sgu_b"][i] = jnp.sum(dsb.reshape(CHUNK, A_HEADS, A_W // A_HEADS), axis=2).T
        grads["pool_w"][i] = _block_diag_parts(dwbd, len(POOL_WINDOWS))
        mix = (tiles(st_p, 0), tiles(st_p, 1), tiles(st_m, 0))
        d_lat = jnp.stack([jnp.sum(t[:, nc:], axis=1) for t in mix]
                          + [jnp.sum(tiles(st_f, r), axis=1) for r in (0, 1, 2)], axis=1).reshape(bl, 6 * D)
        d_ctx = jnp.concatenate([jnp.sum(t[:, :nc], axis=(0, 1)) for t in mix]
                                + [allsum(st_f, r) for r in (5, 6, 7)]).reshape(1, 6 * D)
        dmods[i] = jnp.concatenate([d_lat, d_ctx, jnp.zeros((8 - (bl + 1) % 8 if (bl + 1) % 8 else 0, 6 * D), F32)],
                                   axis=0)
    grad_x = dx[:, n_ctx:, :]

    dmod_local = jnp.stack(dmods)
    rd = dmod_local.shape[1]
    (dmod_all,) = all_gather_rows([dmod_local], "gather_dmod")
    dmod_cols = lax.dynamic_slice_in_dim(dmod_all, my_idx * wc, wc, axis=2).reshape(depth, N_DEV, rd, wc)
    d_lat_all = dmod_cols[:, :, :bl].reshape(depth, n_batch, wc)
    d_ctx_all = dmod_cols[:, 0, bl]
    for p in range(1, N_DEV):
        d_ctx_all = d_ctx_all + dmod_cols[:, p, bl]
    dmod_rows = jnp.concatenate([d_lat_all, d_ctx_all[:, None, :], jnp.zeros((depth, r_act - n_batch - 1, wc), F32)],
                                axis=1)
    dctx_rows = jnp.pad(d_ctx_all[:, None, :], ((0, 0), (0, 7), (0, 0)))
    g_w_mod, dact_ctx = mod_backward(act, dmod_rows, dctx_rows, w_mod)
    sig_c = jax.nn.sigmoid(c_ctx)
    dsilu_c = sig_c * (1.0 + c_ctx * (1.0 - sig_c))
    small_g = {k: (jnp.stack(grads[k]) if grads[k][0] is not None else None) for k in _SMALL}
    small_g["c_ctx"] = jnp.sum(dact_ctx[:, 0, :], axis=0) * dsilu_c
    small_g["b_mod"] = jnp.stack([jnp.sum(dmods[i][:bl + 1], axis=0) for i in range(depth)])

    packed_g = _pack(small_g)
    rows_s = packed_g.shape[0]
    (gathered,) = all_gather_rows([packed_g[None]], "gather_small_grads")
    gparts = gathered.reshape(N_DEV, 1, rows_s, D)
    small_w = {k: wts[k] for k in _SMALL}
    outs = adamw(_pack(small_w)[None], gparts, _pack({k: mom_m[k] for k in _SMALL})[None],
                 _pack({k: mom_v[k] for k in _SMALL})[None], "adamw_replicated")
    res = {k: [None] * 4 for k in _WEIGHTS}
    for slot, packed in enumerate(outs):
        un = _unpack(packed[0], small_w)
        for k in _SMALL:
            res[k][slot] = un[k]

    order = ("w_in", "w_out", "glu_w", "ffn_w_gate", "ffn_w_up", "ffn_w_down")
    landed = all_to_all_rows([jnp.stack(big[k]) for k in order], "scatter_weight_grads")
    for k, parts in zip(order, landed):
        transposed = k in ("w_in", "ffn_w_gate", "ffn_w_up")
        view = (lambda a: jnp.swapaxes(a, 1, 2)) if transposed else (lambda a: a)
        o4 = adamw(view(wts[k]), parts, view(mom_m[k]), view(mom_v[k]), "adamw_" + k)
        res[k] = [view(o) for o in o4]
    res["w_mod"] = list(adamw(w_mod, g_w_mod[None], m_w_mod, v_w_mod, "adamw_w_mod"))

    return (loss, grad_x, *[res[k][0] for k in _WEIGHTS], *[res[k][1] for k in _WEIGHTS],
            *[res[k][2] for k in _WEIGHTS], *[res[k][3] for k in _WEIGHTS])
```

```python
import functools
import math

import numpy as np
import jax
import jax.numpy as jnp
from jax import lax
from jax.experimental import pallas as pl
from jax.experimental.pallas import tpu as pltpu

F32 = jnp.float32
BF16 = jnp.bfloat16
HI = lax.Precision.HIGHEST
MESH = pl.DeviceIdType.MESH

D = 1024
D_IN = 1280
D_FF = 2816
A_W = 256
B_W = 512
C_W = 256
A_HEADS = 4
CHUNK = 128
SSM_G = 32
SSM_H = 16
SSM_P = 64
GRID_W = 64
POOL_WINDOWS = (2, 4, 8, 16)
EPS = 1e-6
N_DEV = 8

TT = 256
TC = 8
ROW_W = TC * B_W
QW = ROW_W // 4
GQ = 8
FF_CHUNK = 256
VMEM_LIMIT = 56 * 1024 * 1024

ADAM_LR = 0.001
ADAM_B1 = 0.9
ADAM_B2 = 0.999
ADAM_EPS = 1e-08
ADAM_WD = 0.01
ADAM_STEP = 10


def _cp(sem):
    return pltpu.CompilerParams(dimension_semantics=sem, vmem_limit_bytes=VMEM_LIMIT)


def dot_nn(a, b):
    return jnp.dot(a, b, preferred_element_type=F32)


def dot_nt(a, b):
    return lax.dot_general(a, b, (((1,), (1,)), ((), ())), preferred_element_type=F32)


def dot_tn(a, b):
    return lax.dot_general(a, b, (((0,), (0,)), ((), ())), preferred_element_type=F32)


def split_bf16(x):
    hi = x.astype(BF16)
    lo = (x - hi.astype(F32)).astype(BF16)
    return hi, lo


def gelu(x):
    return jax.nn.gelu(x)


def gelu_grad(x):
    c = math.sqrt(2.0 / math.pi)
    t = jnp.tanh(c * (x + 0.044715 * x * x * x))
    return 0.5 * (1.0 + t) + 0.5 * x * (1.0 - t * t) * c * (1.0 + 3.0 * 0.044715 * x * x)


def rms_stats(x):
    r = lax.rsqrt(jnp.mean(x * x, axis=-1, keepdims=True) + EPS)
    return r, x * r


def rms_bwd(r, xn, dxn):
    return r * (dxn - xn * jnp.mean(dxn * xn, axis=-1, keepdims=True))


def colsum(x):
    return jnp.sum(x, axis=0, keepdims=True)


def lane_group(width, group):
    return lax.broadcasted_iota(jnp.int32, (1, width), 1) // group


def _tile_spec(width):
    return pl.BlockSpec((None, TT, width), lambda b, j: (b, j, 0))


def _mod_spec(nc):
    return pl.BlockSpec((None, None, 8, D), lambda b, j: (b, jnp.where(j >= nc, 1, 0), 0, 0))


def _full_spec(shape):
    zeros = (0,) * len(shape)
    return pl.BlockSpec(shape, lambda b, j: zeros)


def _kind_spec(shape, nc):
    zeros = (0,) * len(shape)
    return pl.BlockSpec((None,) + shape, lambda b, j: (jnp.where(j >= nc, 1, 0),) + zeros)


def _stat_spec():
    return pl.BlockSpec((None, None, 8, D), lambda b, j: (b, j, 0, 0))


def _chunk_spec():
    return pl.BlockSpec((None, TT // TC, ROW_W), lambda b, j: (b, j, 0))


def _rows_to_chunks(val, scratch, out_ref):
    for cb in range(B_W // 128):
        scratch[cb] = val[:, cb * 128:(cb + 1) * 128]
    for s in range(TC):
        for cb in range(B_W // 128):
            lo = cb * QW + s * 128
            out_ref[:, lo:lo + 128] = scratch.at[cb][pl.ds(s, TT // TC, stride=TC), :]


def _chunks_to_rows(in_ref, scratch):
    for s in range(TC):
        for cb in range(B_W // 128):
            lo = cb * QW + s * 128
            scratch.at[cb][pl.ds(s, TT // TC, stride=TC), :] = in_ref[:, lo:lo + 128]
    return jnp.concatenate([scratch[cb] for cb in range(B_W // 128)], axis=1)


def _chunk_scratch():
    return pltpu.VMEM((B_W // 128, TT, 128), F32)


def embed_tokens(x, ctx, pe):
    bl, seq, _ = x.shape
    nc = ctx.shape[1] // TT
    nt = nc + seq // TT

    def body(ctx_ref, x_ref, pe_ref, o_ref):
        j = pl.program_id(1)

        @pl.when(j < nc)
        def _():
            o_ref[...] = ctx_ref[...]

        @pl.when(j >= nc)
        def _():
            o_ref[...] = x_ref[...] + pe_ref[...]

    return pl.pallas_call(
        body, name="embed_tokens", grid=(bl, nt),
        in_specs=[pl.BlockSpec((None, TT, D), lambda b, j: (b, jnp.minimum(j, nc - 1), 0)),
                  pl.BlockSpec((None, TT, D), lambda b, j: (b, jnp.maximum(j - nc, 0), 0)),
                  pl.BlockSpec((TT, D), lambda b, j: (jnp.maximum(j - nc, 0), 0))],
        out_specs=_tile_spec(D),
        out_shape=jax.ShapeDtypeStruct((bl, nt * TT, D), F32),
        compiler_params=_cp(("arbitrary", "arbitrary")),
    )(ctx, x, pe)


def pre_mix(xs, mod, n1, w_int, nc):
    bl, s, _ = xs.shape

    def body(x_ref, mod_ref, n_ref, w_ref, za_ref, zu_ref, zp_ref, u_s):
        r, xn = rms_stats(x_ref[...])
        h = xn * n_ref[...] * (1.0 + mod_ref[1:2, :]) + mod_ref[0:1, :]
        z = dot_nt(h.astype(BF16), w_ref[...])
        za_ref[...] = z[:, :2 * A_W]
        _rows_to_chunks(z[:, 2 * A_W:2 * A_W + B_W], u_s, zu_ref)
        zp_ref[...] = z[:, 2 * A_W + B_W:]

    return pl.pallas_call(
        body, name="pre_mix", grid=(bl, s // TT),
        in_specs=[_tile_spec(D), _mod_spec(nc), _full_spec((1, D)), _full_spec((D_IN, D))],
        out_specs=[_tile_spec(2 * A_W), _chunk_spec(), _tile_spec(C_W)],
        out_shape=[jax.ShapeDtypeStruct((bl, s, 2 * A_W), F32), jax.ShapeDtypeStruct((bl, s // TC, ROW_W), F32),
                   jax.ShapeDtypeStruct((bl, s, C_W), F32)],
        scratch_shapes=[_chunk_scratch()],
        compiler_params=_cp(("arbitrary", "arbitrary")),
    )(xs, mod, n1, w_int)


def _seg_mean(x, seg_p):
    hi, lo = split_bf16(x)
    return dot_nn(hi, seg_p) + dot_nn(lo, seg_p)


def _sgu_forward(za, sw_ref, sbias, seg_p):
    ge = gelu(za)
    u, v = ge[:, :A_W], ge[:, A_W:]
    dv = v - _seg_mean(v, seg_p)
    rs = lax.rsqrt(_seg_mean(dv * dv, seg_p) + EPS)
    vn = dv * rs
    head = lane_group(A_W, A_W // A_HEADS)
    parts = []
    for c2 in range(TT // CHUNK):
        vb = vn[c2 * CHUNK:(c2 + 1) * CHUNK].astype(BF16)
        sc = sbias
        for h in range(A_HEADS):
            sc = sc + jnp.where(head == h, dot_nn(sw_ref[h], vb), 0.0)
        parts.append(sc)
    sg = jnp.concatenate(parts, axis=0)
    return u * sg, (u, vn, rs, sg)


def _pool_forward(zp, band_ref, icnt, wbd, pscale):
    hi, lo = split_bf16(zp)
    grp = lane_group(C_W, C_W // len(POOL_WINDOWS))
    q = jnp.zeros_like(zp)
    for i in range(len(POOL_WINDOWS)):
        t = dot_nn(band_ref[i], hi) + dot_nn(band_ref[i], lo)
        q = jnp.where(grp == i, t, q)
    q = q * icnt - zp
    o = dot_nn(q.astype(BF16), wbd)
    return o * pscale, (q, o)


def _glu_forward(y, glu_w, glu_b):
    g = gelu(y)
    sg = jax.nn.sigmoid(dot_nn(g.astype(BF16), glu_w) + glu_b)
    return g * sg, (g, sg)


_MIX_CONST_SHAPES = dict(sw=(A_HEADS, CHUNK, CHUNK), sbias=(CHUNK, A_W), seg_p=(A_W, A_W), wbd=(C_W, C_W),
                         pscale=(1, C_W), glu_w=(B_W, B_W), glu_b=(1, B_W), w_out=(D, D), n2=(1, D))


def _mix_const_specs(nc):
    return ([_full_spec(_MIX_CONST_SHAPES[k]) for k in ("sw", "sbias", "seg_p")]
            + [_kind_spec((len(POOL_WINDOWS), TT, TT), nc), _kind_spec((TT, C_W), nc)]
            + [_full_spec(_MIX_CONST_SHAPES[k]) for k in ("wbd", "pscale", "glu_w", "glu_b", "w_out", "n2")])


def _mix_const_args(cst):
    return [cst[k] for k in ("sw", "sbias", "seg_p", "band", "icnt", "wbd", "pscale", "glu_w", "glu_b", "w_out", "n2")]


def post_mix(xs, za, zp, ys, mod, cst, nc):
    bl, s, _ = xs.shape

    def body(x_ref, za_ref, zp_ref, y_ref, mod_ref, sw_ref, sbias_ref, seg_ref, band_ref, icnt_ref, wbd_ref,
             ps_ref, gw_ref, gb_ref, wo_ref, n2_ref, x1_ref, m_ref, y_s):
        a, _ = _sgu_forward(za_ref[...], sw_ref, sbias_ref[...], seg_ref[...])
        p, _ = _pool_forward(zp_ref[...], band_ref, icnt_ref[...], wbd_ref[...], ps_ref[...])
        sm, _ = _glu_forward(_chunks_to_rows(y_ref, y_s), gw_ref[...], gb_ref[...])
        cat = jnp.concatenate([a, sm, p], axis=1).astype(BF16)
        m = dot_nn(cat, wo_ref[...])
        _, mn = rms_stats(m)
        m_ref[...] = m
        x1_ref[...] = x_ref[...] + mod_ref[2:3, :] * (mn * n2_ref[...])

    return pl.pallas_call(
        body, name="post_mix", grid=(bl, s // TT),
        in_specs=[_tile_spec(D), _tile_spec(2 * A_W), _tile_spec(C_W), _chunk_spec(), _mod_spec(nc)]
        + _mix_const_specs(nc),
        out_specs=[_tile_spec(D), _tile_spec(D)],
        out_shape=[jax.ShapeDtypeStruct((bl, s, D), F32), jax.ShapeDtypeStruct((bl, s, D), F32)],
        scratch_shapes=[_chunk_scratch()],
        compiler_params=_cp(("arbitrary", "arbitrary")),
    )(xs, za, zp, ys, mod, *_mix_const_args(cst))


def post_mix_bwd(dx1, m, za, zp, ys, mod, cst, nc):
    bl, s, _ = m.shape
    nt = s // TT

    def body(dx_ref, m_ref, za_ref, zp_ref, y_ref, mod_ref, sw_ref, sbias_ref, seg_ref, band_ref, icnt_ref,
             wbd_ref, ps_ref, gw_ref, gb_ref, wo_ref, n2_ref,
             dza_ref, dzp_ref, dy_ref, cat_ref, dm_ref, gg_ref, dr_ref, st_ref, dsw_ref, dsb_ref, dwbd_ref, y_s):
        first = jnp.logical_and(pl.program_id(0) == 0, pl.program_id(1) == 0)

        @pl.when(first)
        def _():
            dsw_ref[...] = jnp.zeros_like(dsw_ref)
            dsb_ref[...] = jnp.zeros_like(dsb_ref)
            dwbd_ref[...] = jnp.zeros_like(dwbd_ref)

        seg_p = seg_ref[...]
        za = za_ref[...]
        zp_v = zp_ref[...]
        yv = _chunks_to_rows(y_ref, y_s)
        a, (u, vn, rs, sg) = _sgu_forward(za, sw_ref, sbias_ref[...], seg_p)
        p, (q, o) = _pool_forward(zp_v, band_ref, icnt_ref[...], wbd_ref[...], ps_ref[...])
        sm, (g, sig) = _glu_forward(yv, gw_ref[...], gb_ref[...])
        cat_ref[...] = jnp.concatenate([a, sm, p], axis=1).astype(BF16)

        dx = dx_ref[...]
        g1 = mod_ref[2:3, :]
        n2 = n2_ref[...]
        mv = m_ref[...]
        rm, mn = rms_stats(mv)
        st_ref[...] = jnp.zeros_like(st_ref)
        st_ref[0:1, :] = colsum(dx * (mn * n2))
        st_ref[1:2, :] = colsum(dx * g1 * mn)
        dm = rms_bwd(rm, mn, dx * g1 * n2)
        dmb = dm.astype(BF16)
        dm_ref[...] = dmb
        dcat = dot_nt(dmb, wo_ref[...])
        da, dsm, dp = dcat[:, :A_W], dcat[:, A_W:A_W + B_W], dcat[:, A_W + B_W:]

        du = da * sg
        dsv = da * u
        head = lane_group(A_W, A_W // A_HEADS)
        dvn_parts = []
        dsb_acc = jnp.zeros((CHUNK, A_W), F32)
        for c2 in range(TT // CHUNK):
            dsc = dsv[c2 * CHUNK:(c2 + 1) * CHUNK]
            dsc_b = dsc.astype(BF16)
            vb = vn[c2 * CHUNK:(c2 + 1) * CHUNK].astype(BF16)
            dsb_acc = dsb_acc + dsc
            dvn_c = jnp.zeros((CHUNK, A_W), F32)
            for h in range(A_HEADS):
                dsw_ref[h] += dot_nt(jnp.where(head == h, dsc, 0.0).astype(BF16), vb)
                dvn_c = dvn_c + jnp.where(head == h, dot_tn(sw_ref[h], dsc_b), 0.0)
            dvn_parts.append(dvn_c)
        dsb_ref[...] += dsb_acc
        dvn = jnp.concatenate(dvn_parts, axis=0)
        dv = rs * (dvn - _seg_mean(dvn, seg_p) - vn * _seg_mean(dvn * vn, seg_p))
        dza_ref[...] = jnp.concatenate([du, dv], axis=1) * gelu_grad(za)

        ps = ps_ref[...]
        do = dp * ps
        dps = colsum(dp * o)
        dob = do.astype(BF16)
        dwbd_ref[...] += dot_tn(q.astype(BF16), dob)
        dq = dot_nt(dob, wbd_ref[...])
        hi, lo = split_bf16(dq * icnt_ref[...])
        grp = lane_group(C_W, C_W // len(POOL_WINDOWS))
        dzp = -dq
        for i in range(len(POOL_WINDOWS)):
            t = dot_tn(band_ref[i], hi) + dot_tn(band_ref[i], lo)
            dzp = dzp + jnp.where(grp == i, t, 0.0)
        dzp_ref[...] = dzp

        dr = dsm * g * sig * (1.0 - sig)
        drb = dr.astype(BF16)
        dr_ref[...] = drb
        gg_ref[...] = g.astype(BF16)
        dg = dsm * sig + dot_nt(drb, gw_ref[...])
        _rows_to_chunks(dg * gelu_grad(yv), y_s, dy_ref)
        st_ref[2:3, :] = jnp.concatenate([colsum(dr), dps, jnp.zeros((1, D - B_W - C_W), F32)], axis=1)

    acc = lambda shape: pl.BlockSpec(shape, lambda b, j: (0,) * len(shape))
    return pl.pallas_call(
        body, name="post_mix_bwd", grid=(bl, nt),
        in_specs=[_tile_spec(D), _tile_spec(D), _tile_spec(2 * A_W), _tile_spec(C_W), _chunk_spec(), _mod_spec(nc)]
        + _mix_const_specs(nc),
        out_specs=[_tile_spec(2 * A_W), _tile_spec(C_W), _chunk_spec(), _tile_spec(D), _tile_spec(D),
                   _tile_spec(B_W), _tile_spec(B_W), _stat_spec(),
                   acc((A_HEADS, CHUNK, CHUNK)), acc((CHUNK, A_W)), acc((C_W, C_W))],
        out_shape=[jax.ShapeDtypeStruct((bl, s, 2 * A_W), F32), jax.ShapeDtypeStruct((bl, s, C_W), F32),
                   jax.ShapeDtypeStruct((bl, s // TC, ROW_W), F32), jax.ShapeDtypeStruct((bl, s, D), BF16),
                   jax.ShapeDtypeStruct((bl, s, D), BF16), jax.ShapeDtypeStruct((bl, s, B_W), BF16),
                   jax.ShapeDtypeStruct((bl, s, B_W), BF16), jax.ShapeDtypeStruct((bl, nt, 8, D), F32),
                   jax.ShapeDtypeStruct((A_HEADS, CHUNK, CHUNK), F32), jax.ShapeDtypeStruct((CHUNK, A_W), F32),
                   jax.ShapeDtypeStruct((C_W, C_W), F32)],
        scratch_shapes=[_chunk_scratch()],
        compiler_params=_cp(("arbitrary", "arbitrary")),
    )(dx1, m, za, zp, ys, mod, *_mix_const_args(cst))


def pre_mix_bwd(dza, dzu, dzp, xs, dxres, mod, n1, w_int, nc):
    bl, s, _ = xs.shape
    nt = s // TT

    def body(dza_ref, dzu_ref, dzp_ref, x_ref, dres_ref, mod_ref, n_ref, w_ref, dx_ref, h_ref, dz_ref, st_ref, u_s):
        dz = jnp.concatenate([dza_ref[...], _chunks_to_rows(dzu_ref, u_s), dzp_ref[...]], axis=1).astype(BF16)
        dz_ref[...] = dz
        dh = dot_nn(dz, w_ref[...])
        r, xn = rms_stats(x_ref[...])
        n1v = n_ref[...]
        sc = mod_ref[1:2, :]
        xg = xn * n1v
        h_ref[...] = (xg * (1.0 + sc) + mod_ref[0:1, :]).astype(BF16)
        dyv = dh * (1.0 + sc)
        st_ref[...] = jnp.zeros_like(st_ref)
        st_ref[0:1, :] = colsum(dh)
        st_ref[1:2, :] = colsum(dh * xg)
        st_ref[2:3, :] = colsum(dyv * xn)
        dx_ref[...] = dres_ref[...] + rms_bwd(r, xn, dyv * n1v)

    return pl.pallas_call(
        body, name="pre_mix_bwd", grid=(bl, nt),
        in_specs=[_tile_spec(2 * A_W), _chunk_spec(), _tile_spec(C_W), _tile_spec(D), _tile_spec(D), _mod_spec(nc),
                  _full_spec((1, D)), _full_spec((D_IN, D))],
        out_specs=[_tile_spec(D), _tile_spec(D), _tile_spec(D_IN), _stat_spec()],
        out_shape=[jax.ShapeDtypeStruct((bl, s, D), F32), jax.ShapeDtypeStruct((bl, s, D), BF16),
                   jax.ShapeDtypeStruct((bl, s, D_IN), BF16), jax.ShapeDtypeStruct((bl, nt, 8, D), F32)],
        scratch_shapes=[_chunk_scratch()],
        compiler_params=_cp(("arbitrary", "arbitrary")),
    )(dza, dzu, dzp, xs, dxres, mod, n1, w_int)


def _ffn_tile(s):
    return 768 if s % 768 == 0 else TT


def _ctx_rows(tf, n_ctx, j):
    return lax.broadcasted_iota(jnp.int32, (tf, 1), 0) + j * tf < n_ctx


def _mod_row(mod_ref, is_ctx, row):
    return jnp.where(is_ctx, mod_ref[0, row:row + 1, :], mod_ref[1, row:row + 1, :])


def ffn_fwd(x1, mod, n3, n4, wg_t, wu_t, wd, n_ctx):
    bl, s, _ = x1.shape
    tf = _ffn_tile(s)
    nk = D_FF // FF_CHUNK
    tile = pl.BlockSpec((None, tf, D), lambda b, j, k: (b, j, 0))
    modspec = pl.BlockSpec((None, 2, 8, D), lambda b, j, k: (b, 0, 0, 0))
    vec = pl.BlockSpec((1, D), lambda b, j, k: (0, 0))
    wspec = pl.BlockSpec((FF_CHUNK, D), lambda b, j, k: (k, 0))

    def body(x_ref, mod_ref, n3_ref, n4_ref, wg_ref, wu_ref, wd_ref, x2_ref, f_ref, h_s, acc_s):
        j, k = pl.program_id(1), pl.program_id(2)

        @pl.when(k == 0)
        def _():
            is_ctx = _ctx_rows(tf, n_ctx, j)
            _, xn = rms_stats(x_ref[...])
            h_s[...] = (xn * n3_ref[...] * (1.0 + _mod_row(mod_ref, is_ctx, 4)) + _mod_row(mod_ref, is_ctx, 3)).astype(BF16)
            acc_s[...] = jnp.zeros_like(acc_s)

        h = h_s[...]
        gate = dot_nt(h, wg_ref[...])
        up = dot_nt(h, wu_ref[...])
        act = (gate * jax.nn.sigmoid(gate)) * up
        acc_s[...] += dot_nn(act.astype(BF16), wd_ref[...])

        @pl.when(k == nk - 1)
        def _():
            f = acc_s[...]
            f_ref[...] = f
            _, fn = rms_stats(f)
            x2_ref[...] = x_ref[...] + _mod_row(mod_ref, _ctx_rows(tf, n_ctx, j), 5) * (fn * n4_ref[...])

    return pl.pallas_call(
        body, name="ffn_fwd", grid=(bl, s // tf, nk),
        in_specs=[tile, modspec, vec, vec, wspec, wspec, wspec],
        out_specs=[tile, tile],
        out_shape=[jax.ShapeDtypeStruct((bl, s, D), F32), jax.ShapeDtypeStruct((bl, s, D), F32)],
        scratch_shapes=[pltpu.VMEM((tf, D), BF16), pltpu.VMEM((tf, D), F32)],
        compiler_params=_cp(("arbitrary", "arbitrary", "arbitrary")),
    )(x1, mod, n3, n4, wg_t, wu_t, wd)


def ffn_bwd(dx2, x1, f, mod, n3, n4, wg_t, wu_t, wd, n_ctx):
    bl, s, _ = x1.shape
    tf = _ffn_tile(s)
    nt = s // tf
    nk = D_FF // FF_CHUNK
    tile = pl.BlockSpec((None, tf, D), lambda b, j, k: (b, j, 0))
    ftile = pl.BlockSpec((None, tf, FF_CHUNK), lambda b, j, k: (b, j, k))
    modspec = pl.BlockSpec((None, 2, 8, D), lambda b, j, k: (b, 0, 0, 0))
    vec = pl.BlockSpec((1, D), lambda b, j, k: (0, 0))
    wspec = pl.BlockSpec((FF_CHUNK, D), lambda b, j, k: (k, 0))
    stat = pl.BlockSpec((None, None, 8, D), lambda b, j, k: (b, j, 0, 0))

    def split_sum(is_ctx, v, st_ref, row):
        st_ref[row:row + 1, :] = colsum(jnp.where(is_ctx, 0.0, v))
        st_ref[row + 5:row + 6, :] = colsum(jnp.where(is_ctx, v, 0.0))

    def body(dx_ref, x_ref, f_ref, mod_ref, n3_ref, n4_ref, wg_ref, wu_ref, wd_ref,
             dx1_ref, h_ref, df_ref, act_ref, dgate_ref, dup_ref, st_ref, h_s, df_s, acc_s):
        j, k = pl.program_id(1), pl.program_id(2)

        @pl.when(k == 0)
        def _():
            is_ctx = _ctx_rows(tf, n_ctx, j)
            dx = dx_ref[...]
            g2 = _mod_row(mod_ref, is_ctx, 5)
            n4 = n4_ref[...]
            rf, fn = rms_stats(f_ref[...])
            st_ref[...] = jnp.zeros_like(st_ref)
            split_sum(is_ctx, dx * (fn * n4), st_ref, 2)
            st_ref[4:5, :] = colsum(dx * g2 * fn)
            df = rms_bwd(rf, fn, dx * g2 * n4).astype(BF16)
            df_s[...] = df
            df_ref[...] = df
            _, xn = rms_stats(x_ref[...])
            h = (xn * n3_ref[...] * (1.0 + _mod_row(mod_ref, is_ctx, 4)) + _mod_row(mod_ref, is_ctx, 3)).astype(BF16)
            h_s[...] = h
            h_ref[...] = h
            acc_s[...] = jnp.zeros_like(acc_s)

        h = h_s[...]
        gate = dot_nt(h, wg_ref[...])
        up = dot_nt(h, wu_ref[...])
        sg = jax.nn.sigmoid(gate)
        silu = gate * sg
        dact = dot_nt(df_s[...], wd_ref[...])
        act_ref[...] = (silu * up).astype(BF16)
        dgate = (dact * up * (sg * (1.0 + gate * (1.0 - sg)))).astype(BF16)
        dup = (dact * silu).astype(BF16)
        dgate_ref[...] = dgate
        dup_ref[...] = dup
        acc_s[...] += dot_nn(dgate, wg_ref[...]) + dot_nn(dup, wu_ref[...])

        @pl.when(k == nk - 1)
        def _():
            is_ctx = _ctx_rows(tf, n_ctx, j)
            dh = acc_s[...]
            r, xn = rms_stats(x_ref[...])
            n3 = n3_ref[...]
            sc = _mod_row(mod_ref, is_ctx, 4)
            xg = xn * n3
            dyv = dh * (1.0 + sc)
            split_sum(is_ctx, dh, st_ref, 0)
            split_sum(is_ctx, dh * xg, st_ref, 1)
            st_ref[3:4, :] = colsum(dyv * xn)
            dx1_ref[...] = dx_ref[...] + rms_bwd(r, xn, dyv * n3)

    return pl.pallas_call(
        body, name="ffn_bwd", grid=(bl, nt, nk),
        in_specs=[tile, tile, tile, modspec, vec, vec, wspec, wspec, wspec],
        out_specs=[tile, tile, tile, ftile, ftile, ftile, stat],
        out_shape=[jax.ShapeDtypeStruct((bl, s, D), F32), jax.ShapeDtypeStruct((bl, s, D), BF16),
                   jax.ShapeDtypeStruct((bl, s, D), BF16), jax.ShapeDtypeStruct((bl, s, D_FF), BF16),
                   jax.ShapeDtypeStruct((bl, s, D_FF), BF16), jax.ShapeDtypeStruct((bl, s, D_FF), BF16),
                   jax.ShapeDtypeStruct((bl, nt, 8, D), F32)],
        scratch_shapes=[pltpu.VMEM((tf, D), BF16), pltpu.VMEM((tf, D), BF16), pltpu.VMEM((tf, D), F32)],
        compiler_params=_cp(("arbitrary", "arbitrary", "arbitrary")),
    )(dx2, x1, f, mod, n3, n4, wg_t, wu_t, wd)


def loss_head(xs, target, nc):
    bl, s, _ = xs.shape
    nt = s // TT

    def body(x_ref, t_ref, dx_ref, l_ref):
        j = pl.program_id(1)

        @pl.when(j < nc)
        def _():
            dx_ref[...] = jnp.zeros_like(dx_ref)
            l_ref[...] = jnp.zeros_like(l_ref)

        @pl.when(j >= nc)
        def _():
            e = x_ref[...] - t_ref[...]
            dx_ref[...] = e * (1.0 / D)
            tok = jnp.mean(e * e, axis=-1, keepdims=True)
            l_ref[...] = jnp.zeros_like(l_ref) + 0.5 * jnp.sum(tok, axis=0, keepdims=True)

    return pl.pallas_call(
        body, name="loss_head", grid=(bl, nt),
        in_specs=[_tile_spec(D), pl.BlockSpec((None, TT, D), lambda b, j: (b, jnp.maximum(j - nc, 0), 0))],
        out_specs=[_tile_spec(D), pl.BlockSpec((None, None, 8, 128), lambda b, j: (b, j, 0, 0))],
        out_shape=[jax.ShapeDtypeStruct((bl, s, D), F32), jax.ShapeDtypeStruct((bl, nt, 8, 128), F32)],
        compiler_params=_cp(("arbitrary", "arbitrary")),
    )(xs, target)


def tn_matmul(a, b, name):
    t, ka = a.shape
    n = b.shape[1]
    tk = ka if ka <= 1408 else ka // 2
    tt = 512 if t % 512 == 0 else 256
    nsteps = t // tt

    def body(a_ref, b_ref, o_ref, acc_s):
        @pl.when(pl.program_id(1) == 0)
        def _():
            acc_s[...] = jnp.zeros_like(acc_s)

        acc_s[...] += dot_tn(a_ref[...], b_ref[...])

        @pl.when(pl.program_id(1) == nsteps - 1)
        def _():
            o_ref[...] = acc_s[...].astype(BF16)

    return pl.pallas_call(
        body, name=name, grid=(ka // tk, nsteps),
        in_specs=[pl.BlockSpec((tt, tk), lambda i, s: (s, i)), pl.BlockSpec((tt, n), lambda i, s: (s, 0))],
        out_specs=pl.BlockSpec((tk, n), lambda i, s: (i, 0)),
        out_shape=jax.ShapeDtypeStruct((ka, n), BF16),
        scratch_shapes=[pltpu.VMEM((tk, n), F32)],
        compiler_params=_cp(("arbitrary", "arbitrary")),
    )(a, b)


def qmm(terms, name):
    r = terms[0][0].shape[0]
    rt = r // 2 if r % 16 == 0 and r >= 512 else r
    n = len(terms)

    def body(*refs):
        acc = None
        for k in range(n):
            y = dot_nn(refs[2 * k][...].astype(BF16), refs[2 * k + 1][...])
            acc = y if acc is None else acc + y
        refs[2 * n][...] = acc

    row = pl.BlockSpec((rt, QW), lambda q, i: (i, q))
    wspec = pl.BlockSpec((None, QW, QW), lambda q, i: (q, 0, 0))
    return pl.pallas_call(
        body, name=name, grid=(4, r // rt), in_specs=[row, wspec] * n, out_specs=row,
        out_shape=jax.ShapeDtypeStruct((r, ROW_W), F32),
        compiler_params=_cp(("arbitrary", "arbitrary")),
    )(*[x for term in terms for x in term])


def _same_group(rows, cols, row_group, col_group):
    ri = jnp.bitwise_and(lax.broadcasted_iota(jnp.int32, (rows, cols), 0) // row_group, GQ - 1)
    ci = jnp.bitwise_and(lax.broadcasted_iota(jnp.int32, (rows, cols), 1) // col_group, GQ - 1)
    return ri == ci


def _spread_matrix():
    m = np.zeros((2 * SSM_P, QW), np.float32)
    for reim in range(2):
        for g in range(GQ):
            for p in range(SSM_P):
                m[reim * SSM_P + p, reim * (QW // 2) + g * SSM_P + p] = 1.0
    return jnp.asarray(m, BF16)


def assemble_ts(v, name):
    def body(v_ref, f_ref, big_ref, bigt_ref):
        hi, lo = split_bf16(v_ref[...])
        t = dot_nn(hi, f_ref[...]) + dot_nn(lo, f_ref[...])
        t = jnp.where(_same_group(GQ * SSM_H, QW, SSM_H, SSM_P), t, 0.0)
        big_ref[...] = t.astype(BF16)
        bigt_ref[...] = t.T.astype(BF16)

    return pl.pallas_call(
        body, name=name, grid=(4, TC),
        in_specs=[pl.BlockSpec((None, None, 128, 128), lambda q, e: (q, e, 0, 0)),
                  pl.BlockSpec((128, QW), lambda q, e: (0, 0))],
        out_specs=[pl.BlockSpec((None, 128, QW), lambda q, e: (q, e, 0)),
                   pl.BlockSpec((None, QW, 128), lambda q, e: (q, 0, e))],
        out_shape=[jax.ShapeDtypeStruct((4, QW, QW), BF16), jax.ShapeDtypeStruct((4, QW, QW), BF16)],
        compiler_params=_cp(("arbitrary", "arbitrary")),
    )(v, _spread_matrix())


def assemble_tt(lags, name):
    def body(l_ref, m_ref, mt_ref):
        blocks = [l_ref[n] for n in range(2 * TC - 1)]
        flipped = [b.T.astype(BF16) for b in blocks]
        blocks = [b.astype(BF16) for b in blocks]
        for s in range(TC):
            for t in range(TC):
                m_ref[s * 128:(s + 1) * 128, t * 128:(t + 1) * 128] = blocks[t - s + TC - 1]
                mt_ref[t * 128:(t + 1) * 128, s * 128:(s + 1) * 128] = flipped[t - s + TC - 1]

    return pl.pallas_call(
        body, name=name, grid=(4,),
        in_specs=[pl.BlockSpec((None, 2 * TC - 1, 128, 128), lambda q: (q, 0, 0, 0))],
        out_specs=[pl.BlockSpec((None, QW, QW), lambda q: (q, 0, 0))] * 2,
        out_shape=[jax.ShapeDtypeStruct((4, QW, QW), BF16)] * 2,
        compiler_params=_cp(("arbitrary",)),
    )(lags)


def _qtn_call(body, a, b, out_shape, out_block, extra, name):
    r = a.shape[0]
    col = pl.BlockSpec((r, QW), lambda q: (0, q))
    return pl.pallas_call(
        body, name=name, grid=(4,),
        in_specs=[col, col] + [pl.BlockSpec(x.shape, lambda q: (0, 0)) for x in extra],
        out_specs=pl.BlockSpec((None,) + out_block, lambda q: (q,) + (0,) * len(out_block)),
        out_shape=jax.ShapeDtypeStruct((4,) + out_block, F32),
        compiler_params=_cp(("arbitrary",)),
    )(a, b, *extra)


def qtn_ts(a, b, name):
    def body(a_ref, b_ref, f_ref, o_ref):
        full = dot_tn(a_ref[...].astype(BF16), b_ref[...].astype(BF16))
        keep = _same_group(GQ * SSM_H, QW, SSM_H, SSM_P)
        for e in range(TC):
            hi, lo = split_bf16(jnp.where(keep, full[e * 128:(e + 1) * 128, :], 0.0))
            o_ref[e] = dot_nt(hi, f_ref[...]) + dot_nt(lo, f_ref[...])

    return _qtn_call(body, a, b, None, (TC, 128, 128), [_spread_matrix()], name)


def qtn_tt(a, b, name):
    def body(a_ref, b_ref, o_ref):
        full = dot_tn(a_ref[...].astype(BF16), b_ref[...].astype(BF16))
        for lag in range(-(TC - 1), TC):
            acc = None
            for s in range(TC):
                t = s + lag
                if 0 <= t < TC:
                    blk = full[s * 128:(s + 1) * 128, t * 128:(t + 1) * 128]
                    acc = blk if acc is None else acc + blk
            o_ref[lag + TC - 1] = acc

    return _qtn_call(body, a, b, None, (2 * TC - 1, 128, 128), [], name)


def _scan_row(i, rb, ncr, reverse):
    if not reverse:
        return i
    return jnp.where(i < ncr, ncr - 1 - i, rb - 1 - (i - ncr))


def _swap_re_im(h):
    half = QW // 2
    return jnp.concatenate([h[:, q * QW + (1 - k) * half:q * QW + (2 - k) * half] for q in range(4) for k in range(2)],
                           axis=1)


def chunk_scan(xs, lam_ab, ncr, reverse, name):
    bl, rb, _ = xs.shape

    def body(x_ref, l_ref, hp_ref):
        la, lb = l_ref[0:1, :], l_ref[1:2, :]

        def step(i, h):
            row = _scan_row(i, rb, ncr, reverse)
            hp_ref[pl.ds(row, 1), :] = h
            return la * h + lb * _swap_re_im(h) + x_ref[pl.ds(row, 1), :]

        lax.fori_loop(0, rb, step, jnp.zeros((1, ROW_W), F32))

    blk = pl.BlockSpec((None, rb, ROW_W), lambda b: (b, 0, 0))
    return pl.pallas_call(
        body, name=name, grid=(bl,),
        in_specs=[blk, pl.BlockSpec((8, ROW_W), lambda b: (0, 0))], out_specs=blk,
        out_shape=jax.ShapeDtypeStruct((bl, rb, ROW_W), F32),
        compiler_params=_cp(("arbitrary",)),
    )(xs, lam_ab)


def chunk_scan_bwd(dhp, hp, lam_ab, ncr, reverse, name):
    bl, rb, _ = dhp.shape

    def body(d_ref, hp_ref, l_ref, g_ref, dl_ref):
        la, lb = l_ref[0:1, :], l_ref[1:2, :]

        def step(n, carry):
            g, da, db = carry
            row = _scan_row(rb - 1 - n, rb, ncr, reverse)
            g_ref[pl.ds(row, 1), :] = g
            pv = hp_ref[pl.ds(row, 1), :]
            return (d_ref[pl.ds(row, 1), :] + la * g + _swap_re_im(lb * g), da + g * pv, db + g * _swap_re_im(pv))

        z = jnp.zeros((1, ROW_W), F32)
        _, da, db = lax.fori_loop(0, rb, step, (z, z, z))
        dl_ref[...] = jnp.zeros_like(dl_ref)
        dl_ref[0:1, :] = da
        dl_ref[1:2, :] = db

    blk = pl.BlockSpec((None, rb, ROW_W), lambda b: (b, 0, 0))
    return pl.pallas_call(
        body, name=name, grid=(bl,),
        in_specs=[blk, blk, pl.BlockSpec((8, ROW_W), lambda b: (0, 0))],
        out_specs=[blk, pl.BlockSpec((None, 8, ROW_W), lambda b: (b, 0, 0))],
        out_shape=[jax.ShapeDtypeStruct((bl, rb, ROW_W), F32), jax.ShapeDtypeStruct((bl, 8, ROW_W), F32)],
        compiler_params=_cp(("arbitrary",)),
    )(dhp, hp, lam_ab)


def _quarter_rows(v):
    e = v.shape[0]
    return v.reshape(e, 4, 8, SSM_P, SSM_H).transpose(0, 1, 2, 4, 3).reshape(e, 4, 8 * SSM_H, SSM_P)


def _token_state_map(vr, vi):
    return jnp.concatenate([_quarter_rows(vr), _quarter_rows(vi)], axis=-1).transpose(1, 0, 2, 3)


def ssm_build(lam_re, lam_im, log_dt, b_re, b_im, c_re, c_im, d):
    dt = jnp.exp(log_dt)[..., None]
    mag = jnp.exp(lam_re * dt)
    ang = lam_im * dt
    lr, li = mag * jnp.cos(ang), mag * jnp.sin(ang)
    den = lam_re * lam_re + lam_im * lam_im
    nr = lr - 1.0
    fr = (nr * lam_re + li * lam_im) / den
    fi = (li * lam_re - nr * lam_im) / den
    bbr = fr[..., None] * b_re - fi[..., None] * b_im
    bbi = fr[..., None] * b_im + fi[..., None] * b_re
    pr, pi = [jnp.ones_like(lr)], [jnp.zeros_like(lr)]
    for _ in range(TC):
        pr, pi = pr + [pr[-1] * lr - pi[-1] * li], pi + [pr[-1] * li + pi[-1] * lr]
    pr, pi = jnp.stack(pr), jnp.stack(pi)
    clr = c_re[None] * pr[:, :, :, None, :] - c_im[None] * pi[:, :, :, None, :]
    cli = c_re[None] * pi[:, :, :, None, :] + c_im[None] * pr[:, :, :, None, :]
    same_group = jnp.asarray(np.kron(np.eye(8), np.ones((SSM_H, SSM_H))), F32)
    ein = functools.partial(jnp.einsum, precision=HI)

    out, lag_blocks = {}, {}
    for k, name in ((0, "f"), (1, "r")):
        ar, ai = _quarter_rows(bbr[k][None])[0], _quarter_rows(bbi[k][None])[0]
        cr = clr[:TC, k].reshape(TC, 4, 8 * SSM_H, SSM_P)
        ci = cli[:TC, k].reshape(TC, 4, 8 * SSM_H, SSM_P)
        lag_blocks[k] = (ein('qap,nqbp->nqab', ar, cr) - ein('qap,nqbp->nqab', ai, ci)) * same_group
        es = [TC - 1 - s for s in range(TC)] if k == 0 else list(range(TC))
        sr = jnp.stack([pr[e, k][:, :, None] * bbr[k] - pi[e, k][:, :, None] * bbi[k] for e in es])
        si = jnp.stack([pr[e, k][:, :, None] * bbi[k] + pi[e, k][:, :, None] * bbr[k] for e in es])
        out["bs_" + name] = _token_state_map(sr, si)
        et = [t + 1 for t in range(TC)] if k == 0 else [TC - t for t in range(TC)]
        crt = jnp.stack([jnp.swapaxes(clr[e, k], 1, 2) for e in et])
        cit = jnp.stack([-jnp.swapaxes(cli[e, k], 1, 2) for e in et])
        out["cst_" + name] = _token_state_map(crt, cit)
        l8r, l8i = pr[TC, k].reshape(4, 1, QW // 2), pi[TC, k].reshape(4, 1, QW // 2)
        la = jnp.concatenate([l8r, l8r], axis=1).reshape(1, ROW_W)
        lb = jnp.concatenate([-l8i, l8i], axis=1).reshape(1, ROW_W)
        out["lam_" + name] = jnp.concatenate([la, lb, jnp.zeros((6, ROW_W), F32)], axis=0)
    skip = jnp.eye(8 * SSM_H, dtype=F32)[None] * d.reshape(4, 1, 8 * SSM_H)
    center = lag_blocks[0][0] + lag_blocks[1][0] + skip
    lags = [lag_blocks[1][n] for n in range(TC - 1, 0, -1)] + [center] + [lag_blocks[0][n] for n in range(1, TC)]
    out["lags"] = jnp.stack(lags, axis=1)
    return out


def ssm_operators(mats, tag):
    ops = {}
    ops["m"], ops["mt"] = assemble_tt(mats["lags"], "ssm_map_intra" + tag)
    for dname in ("f", "r"):
        ops["bs_" + dname], ops["bst_" + dname] = assemble_ts(mats["bs_" + dname], f"ssm_map_state_in_{dname}{tag}")
        ops["cst_" + dname], ops["cs_" + dname] = assemble_ts(mats["cst_" + dname], f"ssm_map_readout_{dname}{tag}")
    return ops


def ssm_forward(u3, mats, ops, ncr):
    bl, rb, _ = u3.shape
    u = u3.reshape(bl * rb, ROW_W)
    hps, terms = {}, [(u, ops["m"])]
    for dname, rev in (("f", False), ("r", True)):
        xs = qmm([(u, ops["bs_" + dname])], "ssm_state_in_" + dname)
        hp = chunk_scan(xs.reshape(bl, rb, ROW_W), mats["lam_" + dname], ncr, rev, "ssm_scan_" + dname)
        hps[dname] = hp.reshape(bl * rb, ROW_W)
        terms.append((hps[dname], ops["cs_" + dname]))
    return qmm(terms, "ssm_output").reshape(bl, rb, ROW_W), hps


def ssm_backward(dy3, u3, hps, mats, ops, ncr):
    bl, rb, _ = u3.shape
    u = u3.reshape(bl * rb, ROW_W)
    dyr = dy3.reshape(bl * rb, ROW_W)
    cot = {"lags": qtn_tt(u, dyr, "ssm_d_intra")}
    terms = [(dyr, ops["mt"])]
    for dname, rev in (("f", False), ("r", True)):
        dhp = qmm([(dyr, ops["cst_" + dname])], "ssm_dstate_" + dname)
        g, dl = chunk_scan_bwd(dhp.reshape(bl, rb, ROW_W), hps[dname].reshape(bl, rb, ROW_W), mats["lam_" + dname],
                               ncr, rev, "ssm_scan_bwd_" + dname)
        g = g.reshape(bl * rb, ROW_W)
        cot["lam_" + dname] = jnp.sum(dl, axis=0)
        cot["bs_" + dname] = qtn_ts(u, g, "ssm_d_state_in_" + dname)
        cot["cst_" + dname] = qtn_ts(dyr, hps[dname], "ssm_d_readout_" + dname)
        terms.append((g, ops["bst_" + dname]))
    return qmm(terms, "ssm_input_grad").reshape(bl, rb, ROW_W), cot


def mod_forward(act, w_mod, b_cols):
    nl, _, wc = w_mod.shape
    r = act.shape[0]

    def body(a_ref, w_ref, b_ref, o_ref):
        o_ref[...] = dot_nn(a_ref[...].astype(BF16), w_ref[...].astype(BF16)) + b_ref[...]

    return pl.pallas_call(
        body, name="mod_forward", grid=(nl,),
        in_specs=[pl.BlockSpec((r, D), lambda l: (0, 0)), pl.BlockSpec((None, D, wc), lambda l: (l, 0, 0)),
                  pl.BlockSpec((None, 1, wc), lambda l: (l, 0, 0))],
        out_specs=pl.BlockSpec((None, r, wc), lambda l: (l, 0, 0)),
        out_shape=jax.ShapeDtypeStruct((nl, r, wc), F32),
        compiler_params=_cp(("arbitrary",)),
    )(act, w_mod, b_cols)


def mod_backward(act, dmod, dctx, w_mod):
    nl, _, wc = w_mod.shape
    r = act.shape[0]

    def body(a_ref, d_ref, c_ref, w_ref, gw_ref, gc_ref):
        gw_ref[...] = dot_tn(a_ref[...].astype(BF16), d_ref[...].astype(BF16))
        gc_ref[...] = dot_nt(c_ref[...].astype(BF16), w_ref[...].astype(BF16))

    return pl.pallas_call(
        body, name="mod_backward", grid=(nl,),
        in_specs=[pl.BlockSpec((r, D), lambda l: (0, 0)), pl.BlockSpec((None, r, wc), lambda l: (l, 0, 0)),
                  pl.BlockSpec((None, 8, wc), lambda l: (l, 0, 0)), pl.BlockSpec((None, D, wc), lambda l: (l, 0, 0))],
        out_specs=[pl.BlockSpec((None, D, wc), lambda l: (l, 0, 0)), pl.BlockSpec((None, 8, D), lambda l: (l, 0, 0))],
        out_shape=[jax.ShapeDtypeStruct((nl, D, wc), F32), jax.ShapeDtypeStruct((nl, 8, D), F32)],
        compiler_params=_cp(("arbitrary",)),
    )(act, dmod, dctx, w_mod)


def _place():
    return lax.axis_index("x"), lax.axis_index("y"), lax.axis_index("c")


def all_gather_rows(arrs, name):
    n = len(arrs)
    rs = [a.shape[1] for a in arrs]

    def body(*refs):
        x_refs, o_refs = refs[:n], refs[n:2 * n]
        send_sems, recv_sems, local_sems = refs[2 * n:]
        x, y, c = _place()
        me, sibling = (x, y, c), (x, y, 1 - c)
        chips = [(1 - x, y), (x, 1 - y), (1 - x, 1 - y)]

        def rows(a, px, py, pc):
            return o_refs[a].at[:, pl.ds((4 * px + 2 * py + pc) * rs[a], rs[a]), :]

        def copy(a, k, block, to, src=None):
            return pltpu.make_async_remote_copy(
                src_ref=rows(a, *block) if src is None else src, dst_ref=rows(a, *block),
                send_sem=send_sems.at[a, k], recv_sem=recv_sems.at[a, k], device_id=to, device_id_type=MESH)

        mine = [pltpu.make_async_copy(x_refs[a], rows(a, *me), local_sems.at[a]) for a in range(n)]
        for cp in mine:
            cp.start()
        first = []
        for a in range(n):
            first.append(copy(a, 0, me, sibling, src=x_refs[a]))
            first += [copy(a, 1 + j, me, (*chip, c), src=x_refs[a]) for j, chip in enumerate(chips)]
        for cp in first:
            cp.start()
        passed = []
        for j, chip in enumerate(chips):
            for a in range(n):
                copy(a, 1 + j, (*chip, c), me).wait_recv()
                fwd = copy(a, 4 + j, (*chip, c), sibling)
                fwd.start()
                passed.append(fwd)
        for a in range(n):
            copy(a, 0, sibling, me).wait_recv()
            for j, chip in enumerate(chips):
                copy(a, 4 + j, (*chip, 1 - c), me).wait_recv()
        for cp in first + passed:
            cp.wait_send()
        for cp in mine:
            cp.wait()

    any_spec = pl.BlockSpec(memory_space=pl.ANY)
    return pl.pallas_call(
        body, name=name,
        in_specs=[any_spec] * n, out_specs=[any_spec] * n,
        out_shape=[jax.ShapeDtypeStruct((a.shape[0], N_DEV * a.shape[1], a.shape[2]), a.dtype) for a in arrs],
        scratch_shapes=[pltpu.SemaphoreType.DMA((n, 7)), pltpu.SemaphoreType.DMA((n, 7)), pltpu.SemaphoreType.DMA((n,))],
    )(*arrs)


def all_to_all_rows(arrs, name):
    n = len(arrs)
    rs = [a.shape[1] // N_DEV for a in arrs]
    flips = [(fx, fy, fc) for fx in (0, 1) for fy in (0, 1) for fc in (0, 1)][1:]

    def body(*refs):
        x_refs, o_refs = refs[:n], refs[n:2 * n]
        send_sems, recv_sems, local_sems = refs[2 * n:]
        x, y, c = _place()
        my_idx = 4 * x + 2 * y + c

        def block(a, idx):
            return x_refs[a].at[:, pl.ds(idx * rs[a], rs[a]), :]

        mine = [pltpu.make_async_copy(block(a, my_idx), o_refs[a].at[my_idx], local_sems.at[a]) for a in range(n)]
        for cp in mine:
            cp.start()
        sends = []
        for k, (fx, fy, fc) in enumerate(flips):
            px = 1 - x if fx else x
            py = 1 - y if fy else y
            pc = 1 - c if fc else c
            p_idx = 4 * px + 2 * py + pc
            for a in range(n):
                sends.append(pltpu.make_async_remote_copy(
                    src_ref=block(a, p_idx), dst_ref=o_refs[a].at[my_idx], send_sem=send_sems.at[a, k],
                    recv_sem=recv_sems.at[a, k], device_id=(px, py, pc), device_id_type=MESH))
        for cp in sends:
            cp.start()
        for k, (fx, fy, fc) in enumerate(flips):
            px = 1 - x if fx else x
            py = 1 - y if fy else y
            pc = 1 - c if fc else c
            p_idx = 4 * px + 2 * py + pc
            for a in range(n):
                pltpu.make_async_remote_copy(
                    src_ref=block(a, p_idx), dst_ref=o_refs[a].at[p_idx], send_sem=send_sems.at[a, k],
                    recv_sem=recv_sems.at[a, k], device_id=(px, py, pc), device_id_type=MESH).wait_recv()
        for cp in sends:
            cp.wait_send()
        for cp in mine:
            cp.wait()

    any_spec = pl.BlockSpec(memory_space=pl.ANY)
    return pl.pallas_call(
        body, name=name,
        in_specs=[any_spec] * n, out_specs=[any_spec] * n,
        out_shape=[jax.ShapeDtypeStruct((N_DEV, a.shape[0], r, a.shape[2]), a.dtype) for a, r in zip(arrs, rs)],
        scratch_shapes=[pltpu.SemaphoreType.DMA((n, 7)), pltpu.SemaphoreType.DMA((n, 7)), pltpu.SemaphoreType.DMA((n,))],
    )(*arrs)


def _row_tile(rows, cap):
    best = None
    for t in range(16, min(rows, cap) + 1, 16):
        if rows % t == 0:
            best = t
    return rows if best is None else best


def adamw(w, gparts, m, v, name):
    n, nl, ra, cb = gparts.shape
    ta = _row_tile(ra, max(8, (1 << 19) // (cb * n)))

    def body(w_ref, g_ref, m_ref, v_ref, go_ref, d_ref, mo_ref, vo_ref):
        g = g_ref[0].astype(F32)
        for p in range(1, n):
            g = g + g_ref[p].astype(F32)
        mn = ADAM_B1 * m_ref[...] + (1.0 - ADAM_B1) * g
        vn = ADAM_B2 * v_ref[...] + (1.0 - ADAM_B2) * jnp.square(g)
        m_hat = mn / (1.0 - ADAM_B1 ** ADAM_STEP)
        v_hat = vn / (1.0 - ADAM_B2 ** ADAM_STEP)
        go_ref[...] = g
        d_ref[...] = -ADAM_LR * (m_hat / (jnp.sqrt(v_hat) + ADAM_EPS) + ADAM_WD * w_ref[...])
        mo_ref[...] = mn
        vo_ref[...] = vn

    blk = pl.BlockSpec((None, ta, cb), lambda l, i: (l, i, 0))
    gblk = pl.BlockSpec((n, None, ta, cb), lambda l, i: (0, l, i, 0))
    shp = jax.ShapeDtypeStruct((nl, ra, cb), F32)
    return pl.pallas_call(
        body, name=name, grid=(nl, ra // ta),
        in_specs=[blk, gblk, blk, blk], out_specs=[blk] * 4, out_shape=[shp] * 4,
        compiler_params=_cp(("arbitrary", "arbitrary")),
    )(w, gparts, m, v)


def _sincos_2d(rows, cols, dim):
    quarter = dim // 4
    omega = 1.0 / (10000.0 ** (jnp.arange(quarter, dtype=F32) / quarter))
    r = jnp.arange(rows, dtype=F32)[:, None] * omega
    cc = jnp.arange(cols, dtype=F32)[:, None] * omega
    er = jnp.concatenate([jnp.sin(r), jnp.cos(r)], axis=-1)
    ec = jnp.concatenate([jnp.sin(cc), jnp.cos(cc)], axis=-1)
    pe = jnp.concatenate([jnp.broadcast_to(er[:, None, :], (rows, cols, dim // 2)),
                          jnp.broadcast_to(ec[None, :, :], (rows, cols, dim // 2))], axis=-1)
    return pe.reshape(rows * cols, dim)


def _pool_constants():
    nw = len(POOL_WINDOWS)
    band = np.zeros((2, nw, TT, TT), np.float32)
    icnt = np.zeros((2, TT, C_W), np.float32)
    for kind, n in ((0, TT), (1, GRID_W)):
        for i, w in enumerate(POOL_WINDOWS):
            for t in range(TT):
                base, tl = (t // n) * n, t % n
                lo = min(max(tl - w // 2, 0), n)
                hi = min(max(tl - w // 2 + w, 0), n)
                band[kind, i, t, base + lo:base + hi] = 1.0
                icnt[kind, t, i * (C_W // nw):(i + 1) * (C_W // nw)] = 1.0 / (hi - lo)
    return jnp.asarray(band, BF16), jnp.asarray(icnt, F32)


def _block_diag(blocks):
    n, a, _ = blocks.shape
    return jnp.einsum('gab,gh->gahb', blocks, jnp.eye(n, dtype=F32), precision=HI).reshape(n * a, n * a)


def _block_diag_parts(mat, n):
    a = mat.shape[0] // n
    m4 = mat.reshape(n, a, n, a)
    return jnp.stack([m4[g, :, g, :] for g in range(n)])


_SMALL = ("c_ctx", "b_mod", "norm_mix_pre", "norm_mix_post", "norm_ffn_pre", "norm_ffn_post", "sgu_w", "sgu_b",
          "ssm_lam_re", "ssm_lam_im", "ssm_log_dt", "ssm_b_re", "ssm_b_im", "ssm_c_re", "ssm_c_im", "ssm_d",
          "glu_b", "pool_w", "pool_scale")
_WEIGHTS = ("c_ctx", "w_mod", "b_mod", "norm_mix_pre", "norm_mix_post", "norm_ffn_pre", "norm_ffn_post", "w_in", "w_out",
            "sgu_w", "sgu_b", "ssm_lam_re", "ssm_lam_im", "ssm_log_dt", "ssm_b_re", "ssm_b_im", "ssm_c_re", "ssm_c_im",
            "ssm_d", "glu_w", "glu_b", "pool_w", "pool_scale", "ffn_w_gate", "ffn_w_up", "ffn_w_down")


def _pack_rows(a):
    flat = a.reshape(-1)
    rows = -(-flat.shape[0] // D)
    rows8 = -(-rows // 8) * 8
    return jnp.pad(flat, (0, rows8 * D - flat.shape[0])).reshape(rows8, D)


def _pack(tree):
    packed = jnp.concatenate([_pack_rows(tree[k]) for k in _SMALL], axis=0)
    return jnp.pad(packed, ((0, -packed.shape[0] % 64), (0, 0)))


def _unpack(packed, like):
    out, at = {}, 0
    for k in _SMALL:
        size = int(np.prod(like[k].shape))
        rows8 = -(-(-(-size // D)) // 8) * 8
        out[k] = packed[at:at + rows8].reshape(-1)[:size].reshape(like[k].shape)
        at += rows8
    return out


def kernel(x, c, ctx, c_ctx, w_mod, b_mod, norm_mix_pre, norm_mix_post, norm_ffn_pre, norm_ffn_post, w_in, w_out, sgu_w, sgu_b, ssm_lam_re, ssm_lam_im, ssm_log_dt, ssm_b_re, ssm_b_im, ssm_c_re, ssm_c_im, ssm_d, glu_w, glu_b, pool_w, pool_scale, ffn_w_gate, ffn_w_up, ffn_w_down, loss_target, m_c_ctx, m_w_mod, m_b_mod, m_norm_mix_pre, m_norm_mix_post, m_norm_ffn_pre, m_norm_ffn_post, m_w_in, m_w_out, m_sgu_w, m_sgu_b, m_ssm_lam_re, m_ssm_lam_im, m_ssm_log_dt, m_ssm_b_re, m_ssm_b_im, m_ssm_c_re, m_ssm_c_im, m_ssm_d, m_glu_w, m_glu_b, m_pool_w, m_pool_scale, m_ffn_w_gate, m_ffn_w_up, m_ffn_w_down, v_c_ctx, v_w_mod, v_b_mod, v_norm_mix_pre, v_norm_mix_post, v_norm_ffn_pre, v_norm_ffn_post, v_w_in, v_w_out, v_sgu_w, v_sgu_b, v_ssm_lam_re, v_ssm_lam_im, v_ssm_log_dt, v_ssm_b_re, v_ssm_b_im, v_ssm_c_re, v_ssm_c_im, v_ssm_d, v_glu_w, v_glu_b, v_pool_w, v_pool_scale, v_ffn_w_gate, v_ffn_w_up, v_ffn_w_down):
    wts = dict(c_ctx=c_ctx, w_mod=w_mod, b_mod=b_mod, norm_mix_pre=norm_mix_pre, norm_mix_post=norm_mix_post,
               norm_ffn_pre=norm_ffn_pre, norm_ffn_post=norm_ffn_post, w_in=w_in, w_out=w_out, sgu_w=sgu_w, sgu_b=sgu_b,
               ssm_lam_re=ssm_lam_re, ssm_lam_im=ssm_lam_im, ssm_log_dt=ssm_log_dt, ssm_b_re=ssm_b_re, ssm_b_im=ssm_b_im,
               ssm_c_re=ssm_c_re, ssm_c_im=ssm_c_im, ssm_d=ssm_d, glu_w=glu_w, glu_b=glu_b, pool_w=pool_w,
               pool_scale=pool_scale, ffn_w_gate=ffn_w_gate, ffn_w_up=ffn_w_up, ffn_w_down=ffn_w_down)
    mom_m = dict(c_ctx=m_c_ctx, w_mod=m_w_mod, b_mod=m_b_mod, norm_mix_pre=m_norm_mix_pre, norm_mix_post=m_norm_mix_post,
                 norm_ffn_pre=m_norm_ffn_pre, norm_ffn_post=m_norm_ffn_post, w_in=m_w_in, w_out=m_w_out, sgu_w=m_sgu_w,
                 sgu_b=m_sgu_b, ssm_lam_re=m_ssm_lam_re, ssm_lam_im=m_ssm_lam_im, ssm_log_dt=m_ssm_log_dt,
                 ssm_b_re=m_ssm_b_re, ssm_b_im=m_ssm_b_im, ssm_c_re=m_ssm_c_re, ssm_c_im=m_ssm_c_im, ssm_d=m_ssm_d,
                 glu_w=m_glu_w, glu_b=m_glu_b, pool_w=m_pool_w, pool_scale=m_pool_scale, ffn_w_gate=m_ffn_w_gate,
                 ffn_w_up=m_ffn_w_up, ffn_w_down=m_ffn_w_down)
    mom_v = dict(c_ctx=v_c_ctx, w_mod=v_w_mod, b_mod=v_b_mod, norm_mix_pre=v_norm_mix_pre, norm_mix_post=v_norm_mix_post,
                 norm_ffn_pre=v_norm_ffn_pre, norm_ffn_post=v_norm_ffn_post, w_in=v_w_in, w_out=v_w_out, sgu_w=v_sgu_w,
                 sgu_b=v_sgu_b, ssm_lam_re=v_ssm_lam_re, ssm_lam_im=v_ssm_lam_im, ssm_log_dt=v_ssm_log_dt,
                 ssm_b_re=v_ssm_b_re, ssm_b_im=v_ssm_b_im, ssm_c_re=v_ssm_c_re, ssm_c_im=v_ssm_c_im, ssm_d=v_ssm_d,
                 glu_w=v_glu_w, glu_b=v_glu_b, pool_w=v_pool_w, pool_scale=v_pool_scale, ffn_w_gate=v_ffn_w_gate,
                 ffn_w_up=v_ffn_w_up, ffn_w_down=v_ffn_w_down)

    bl, seq, _ = x.shape
    n_ctx = ctx.shape[1]
    assert n_ctx == TT and seq % TT == 0 and seq % GRID_W == 0
    depth = w_in.shape[0]
    nc = n_ctx // TT
    ncr = n_ctx // TC
    s_all = n_ctx + seq
    nt = s_all // TT
    t_all = bl * s_all
    n_batch = bl * N_DEV
    my_idx = 4 * lax.axis_index("x") + 2 * lax.axis_index("y") + lax.axis_index("c")
    wc = w_mod.shape[2]

    c_rows = jnp.pad(c, ((0, 8 - bl), (0, 0))) if bl < 8 else c
    rc = c_rows.shape[0]
    (c_all,) = all_gather_rows([c_rows[None]], "gather_c")
    c_all = c_all[0].reshape(N_DEV, rc, D)[:, :bl].reshape(n_batch, D)
    r_act = -(-(n_batch + 1) // 16) * 16
    pre_act = jnp.concatenate([c_all, c_ctx[None, :], jnp.zeros((r_act - n_batch - 1, D), F32)], axis=0)
    act = jax.nn.silu(pre_act)
    b_cols = lax.dynamic_slice_in_dim(b_mod, my_idx * wc, wc, axis=1)[:, None, :]
    mod_cols = mod_forward(act, w_mod, b_cols)
    (mod_all,) = all_gather_rows([mod_cols], "gather_mod")
    mod_all = mod_all.reshape(depth, N_DEV, r_act, wc).transpose(0, 2, 1, 3).reshape(depth, r_act, 6, D)
    mod_lat = lax.dynamic_slice_in_dim(mod_all, my_idx * bl, bl, axis=1)
    mod_ctx = jnp.broadcast_to(mod_all[:, n_batch:n_batch + 1], (depth, bl, 6, D))
    mods = jnp.pad(jnp.stack([mod_ctx, mod_lat], axis=2), ((0, 0), (0, 0), (0, 0), (0, 2), (0, 0)))

    tr = lambda a: jnp.swapaxes(a, 1, 2).astype(BF16)
    w_int, w_o, g_w, wg_t, wu_t, w_d = all_gather_rows(
        [tr(w_in), w_out.astype(BF16), glu_w.astype(BF16), tr(ffn_w_gate), tr(ffn_w_up), ffn_w_down.astype(BF16)],
        "gather_weights")

    band, icnt = _pool_constants()
    seg_p = jnp.asarray(np.kron(np.eye(A_HEADS), np.full((A_W // A_HEADS,) * 2, A_HEADS / A_W)), BF16)
    pe = _sincos_2d(seq // GRID_W, GRID_W, D)
    xs = embed_tokens(x, ctx, pe)

    saved = []
    for i in range(depth):
        mats, ssm_vjp = jax.vjp(ssm_build, ssm_lam_re[i], ssm_lam_im[i], ssm_log_dt[i], ssm_b_re[i], ssm_b_im[i],
                                ssm_c_re[i], ssm_c_im[i], ssm_d[i])
        cst = dict(sw=sgu_w[i].astype(BF16),
                   sbias=jnp.repeat(sgu_b[i].T, A_W // A_HEADS, axis=1),
                   seg_p=seg_p, band=band, icnt=icnt, wbd=_block_diag(pool_w[i]).astype(BF16),
                   pscale=pool_scale[i][None, :], glu_w=g_w[i], glu_b=glu_b[i][None, :], w_out=w_o[i],
                   n2=norm_mix_post[i][None, :])
        n1, n3, n4 = norm_mix_pre[i][None, :], norm_ffn_pre[i][None, :], norm_ffn_post[i][None, :]
        za, zu, zp = pre_mix(xs, mods[i], n1, w_int[i], nc)
        ops = ssm_operators(mats, f"_{i}")
        ys, hps = ssm_forward(zu, mats, ops, ncr)
        x1, m_pre = post_mix(xs, za, zp, ys, mods[i], cst, nc)
        x2, f_pre = ffn_fwd(x1, mods[i], n3, n4, wg_t[i], wu_t[i], w_d[i], n_ctx)
        saved.append(dict(xs=xs, za=za, zu=zu, zp=zp, ys=ys, hps=hps, x1=x1, m=m_pre, f=f_pre, cst=cst, mats=mats,
                          ops=ops, ssm_vjp=ssm_vjp, n1=n1, n3=n3, n4=n4))
        xs = x2

    dx, loss_parts = loss_head(xs, loss_target, nc)
    loss = lax.psum(jnp.sum(loss_parts[:, :, 0, 0]), ("x", "y", "c"))

    grads = {k: [None] * depth for k in _WEIGHTS}
    big = {k: [None] * depth for k in ("w_in", "w_out", "glu_w", "ffn_w_gate", "ffn_w_up", "ffn_w_down")}
    dmods = [None] * depth
    flat = lambda a: a.reshape(t_all, a.shape[-1])
    for i in reversed(range(depth)):
        sv = saved[i]
        dx1, h2, df, act_b, dgate, dup, st_f = ffn_bwd(dx, sv["x1"], sv["f"], mods[i], sv["n3"], sv["n4"],
                                                       wg_t[i], wu_t[i], w_d[i], n_ctx)
        big["ffn_w_gate"][i] = tn_matmul(flat(dgate), flat(h2), f"grad_ffn_gate_{i}")
        big["ffn_w_up"][i] = tn_matmul(flat(dup), flat(h2), f"grad_ffn_up_{i}")
        big["ffn_w_down"][i] = tn_matmul(flat(act_b), flat(df), f"grad_ffn_down_{i}")
        dza, dzp, dys, cat, dm, gg, dr, st_m, dsw, dsb, dwbd = post_mix_bwd(dx1, sv["m"], sv["za"], sv["zp"], sv["ys"],
                                                                            mods[i], sv["cst"], nc)
        big["w_out"][i] = tn_matmul(flat(cat), flat(dm), f"grad_w_out_{i}")
        big["glu_w"][i] = tn_matmul(flat(gg), flat(dr), f"grad_glu_w_{i}")
        dzu, cot = ssm_backward(dys, sv["zu"], sv["hps"], sv["mats"], sv["ops"], ncr)
        (grads["ssm_lam_re"][i], grads["ssm_lam_im"][i], grads["ssm_log_dt"][i], grads["ssm_b_re"][i],
         grads["ssm_b_im"][i], grads["ssm_c_re"][i], grads["ssm_c_im"][i], grads["ssm_d"][i]) = sv["ssm_vjp"](cot)
        dx, h1, dz, st_p = pre_mix_bwd(dza, dzu, dzp, sv["xs"], dx1, mods[i], sv["n1"], w_int[i], nc)
        big["w_in"][i] = tn_matmul(flat(dz), flat(h1), f"grad_w_in_{i}")

        tiles = lambda st, row: st[:, :, row, :]
        allsum = lambda st, row: jnp.sum(tiles(st, row), axis=(0, 1))
        grads["norm_mix_pre"][i] = allsum(st_p, 2)
        grads["norm_mix_post"][i] = allsum(st_m, 1)
        grads["norm_ffn_pre"][i] = allsum(st_f, 3)
        grads["norm_ffn_post"][i] = allsum(st_f, 4)
        misc = allsum(st_m, 2)
        grads["glu_b"][i] = misc[:B_W]
        grads["pool_scale"][i] = misc[B_W:B_W + C_W]
        grads["sgu_w"][i] = dsw
        grads["sgu_b"][i] = jnp.sum(dsb.reshape(CHUNK, A_HEADS, A_W // A_HEADS), axis=2).T
        grads["pool_w"][i] = _block_diag_parts(dwbd, len(POOL_WINDOWS))
        mix = (tiles(st_p, 0), tiles(st_p, 1), tiles(st_m, 0))
        d_lat = jnp.stack([jnp.sum(t[:, nc:], axis=1) for t in mix]
                          + [jnp.sum(tiles(st_f, r), axis=1) for r in (0, 1, 2)], axis=1).reshape(bl, 6 * D)
        d_ctx = jnp.concatenate([jnp.sum(t[:, :nc], axis=(0, 1)) for t in mix]
                                + [allsum(st_f, r) for r in (5, 6, 7)]).reshape(1, 6 * D)
        dmods[i] = jnp.concatenate([d_lat, d_ctx, jnp.zeros((8 - (bl + 1) % 8 if (bl + 1) % 8 else 0, 6 * D), F32)],
                                   axis=0)
    grad_x = dx[:, n_ctx:, :]

    dmod_local = jnp.stack(dmods)
    rd = dmod_local.shape[1]
    (dmod_all,) = all_gather_rows([dmod_local], "gather_dmod")
    dmod_cols = lax.dynamic_slice_in_dim(dmod_all, my_idx * wc, wc, axis=2).reshape(depth, N_DEV, rd, wc)
    d_lat_all = dmod_cols[:, :, :bl].reshape(depth, n_batch, wc)
    d_ctx_all = dmod_cols[:, 0, bl]
    for p in range(1, N_DEV):
        d_ctx_all = d_ctx_all + dmod_cols[:, p, bl]
    dmod_rows = jnp.concatenate([d_lat_all, d_ctx_all[:, None, :], jnp.zeros((depth, r_act - n_batch - 1, wc), F32)],
                                axis=1)
    dctx_rows = jnp.pad(d_ctx_all[:, None, :], ((0, 0), (0, 7), (0, 0)))
    g_w_mod, dact_ctx = mod_backward(act, dmod_rows, dctx_rows, w_mod)
    sig_c = jax.nn.sigmoid(c_ctx)
    dsilu_c = sig_c * (1.0 + c_ctx * (1.0 - sig_c))
    small_g = {k: (jnp.stack(grads[k]) if grads[k][0] is not None else None) for k in _SMALL}
    small_g["c_ctx"] = jnp.sum(dact_ctx[:, 0, :], axis=0) * dsilu_c
    small_g["b_mod"] = jnp.stack([jnp.sum(dmods[i][:bl + 1], axis=0) for i in range(depth)])

    packed_g = _pack(small_g).astype(BF16)
    rows_s = packed_g.shape[0]
    (gathered,) = all_gather_rows([packed_g[None]], "gather_small_grads")
    gparts = gathered.reshape(N_DEV, 1, rows_s, D)
    small_w = {k: wts[k] for k in _SMALL}
    outs = adamw(_pack(small_w)[None], gparts, _pack({k: mom_m[k] for k in _SMALL})[None],
                 _pack({k: mom_v[k] for k in _SMALL})[None], "adamw_replicated")
    res = {k: [None] * 4 for k in _WEIGHTS}
    for slot, packed in enumerate(outs):
        un = _unpack(packed[0], small_w)
        for k in _SMALL:
            res[k][slot] = un[k]

    order = ("w_in", "w_out", "glu_w", "ffn_w_gate", "ffn_w_up", "ffn_w_down")
    landed = all_to_all_rows([jnp.stack(big[k]) for k in order], "scatter_weight_grads")
    for k, parts in zip(order, landed):
        transposed = k in ("w_in", "ffn_w_gate", "ffn_w_up")
        view = (lambda a: jnp.swapaxes(a, 1, 2)) if transposed else (lambda a: a)
        o4 = adamw(view(wts[k]), parts, view(mom_m[k]), view(mom_v[k]), "adamw_" + k)
        res[k] = [view(o) for o in o4]
    res["w_mod"] = list(adamw(w_mod, g_w_mod[None], m_w_mod, v_w_mod, "adamw_w_mod"))

    return (loss, grad_x, *[res[k][0] for k in _WEIGHTS], *[res[k][1] for k in _WEIGHTS],
            *[res[k][2] for k in _WEIGHTS], *[res[k][3] for k in _WEIGHTS])
```

```python
import functools
import math

import numpy as np
import jax
import jax.numpy as jnp
from jax import lax
from jax.experimental import pallas as pl
from jax.experimental.pallas import tpu as pltpu

F32 = jnp.float32
BF16 = jnp.bfloat16
HI = lax.Precision.HIGHEST
MESH = pl.DeviceIdType.MESH

D = 1024
D_IN = 1280
D_FF = 2816
A_W = 256
B_W = 512
C_W = 256
A_HEADS = 4
CHUNK = 128
SSM_G = 32
SSM_H = 16
SSM_P = 64
GRID_W = 64
POOL_WINDOWS = (2, 4, 8, 16)
EPS = 1e-6
N_DEV = 8

TT = 256
TC = 8
ROW_W = TC * B_W
QW = ROW_W // 4
GQ = 8
FF_CHUNK = 256
VMEM_LIMIT = 60 * 1024 * 1024

ADAM_LR = 0.001
ADAM_B1 = 0.9
ADAM_B2 = 0.999
ADAM_EPS = 1e-08
ADAM_WD = 0.01
ADAM_STEP = 10


def _cp(sem):
    return pltpu.CompilerParams(dimension_semantics=sem, vmem_limit_bytes=VMEM_LIMIT)


def dot_nn(a, b):
    return jnp.dot(a, b, preferred_element_type=F32)


def dot_nt(a, b):
    return lax.dot_general(a, b, (((1,), (1,)), ((), ())), preferred_element_type=F32)


def dot_tn(a, b):
    return lax.dot_general(a, b, (((0,), (0,)), ((), ())), preferred_element_type=F32)


def split_bf16(x):
    hi = x.astype(BF16)
    lo = (x - hi.astype(F32)).astype(BF16)
    return hi, lo


def gelu(x):
    return jax.nn.gelu(x)


def gelu_grad(x):
    c = math.sqrt(2.0 / math.pi)
    t = jnp.tanh(c * (x + 0.044715 * x * x * x))
    return 0.5 * (1.0 + t) + 0.5 * x * (1.0 - t * t) * c * (1.0 + 3.0 * 0.044715 * x * x)


def rms_stats(x):
    r = lax.rsqrt(jnp.mean(x * x, axis=-1, keepdims=True) + EPS)
    return r, x * r


def rms_bwd(r, xn, dxn):
    return r * (dxn - xn * jnp.mean(dxn * xn, axis=-1, keepdims=True))


def colsum(x):
    return jnp.sum(x, axis=0, keepdims=True)


def lane_group(width, group):
    return lax.broadcasted_iota(jnp.int32, (1, width), 1) // group


def _tile_spec(width):
    return pl.BlockSpec((None, TT, width), lambda b, j: (b, j, 0))


def _mod_spec(nc):
    return pl.BlockSpec((None, None, 8, D), lambda b, j: (b, jnp.where(j >= nc, 1, 0), 0, 0))


def _full_spec(shape):
    zeros = (0,) * len(shape)
    return pl.BlockSpec(shape, lambda b, j: zeros)


def _kind_spec(shape, nc):
    zeros = (0,) * len(shape)
    return pl.BlockSpec((None,) + shape, lambda b, j: (jnp.where(j >= nc, 1, 0),) + zeros)


def _stat_spec():
    return pl.BlockSpec((None, None, 8, D), lambda b, j: (b, j, 0, 0))


def _chunk_spec():
    return pl.BlockSpec((None, TT // TC, ROW_W), lambda b, j: (b, j, 0))


def _rows_to_chunks(val, scratch, out_ref):
    for cb in range(B_W // 128):
        scratch[cb] = val[:, cb * 128:(cb + 1) * 128]
    for s in range(TC):
        for cb in range(B_W // 128):
            lo = cb * QW + s * 128
            out_ref[:, lo:lo + 128] = scratch.at[cb][pl.ds(s, TT // TC, stride=TC), :]


def _chunks_to_rows(in_ref, scratch):
    for s in range(TC):
        for cb in range(B_W // 128):
            lo = cb * QW + s * 128
            scratch.at[cb][pl.ds(s, TT // TC, stride=TC), :] = in_ref[:, lo:lo + 128]
    return jnp.concatenate([scratch[cb] for cb in range(B_W // 128)], axis=1)


def _chunk_scratch():
    return pltpu.VMEM((B_W // 128, TT, 128), F32)


def embed_tokens(x, ctx, pe):
    bl, seq, _ = x.shape
    nc = ctx.shape[1] // TT
    nt = nc + seq // TT

    def body(ctx_ref, x_ref, pe_ref, o_ref):
        j = pl.program_id(1)

        @pl.when(j < nc)
        def _():
            o_ref[...] = ctx_ref[...]

        @pl.when(j >= nc)
        def _():
            o_ref[...] = x_ref[...] + pe_ref[...]

    return pl.pallas_call(
        body, name="embed_tokens", grid=(bl, nt),
        in_specs=[pl.BlockSpec((None, TT, D), lambda b, j: (b, jnp.minimum(j, nc - 1), 0)),
                  pl.BlockSpec((None, TT, D), lambda b, j: (b, jnp.maximum(j - nc, 0), 0)),
                  pl.BlockSpec((TT, D), lambda b, j: (jnp.maximum(j - nc, 0), 0))],
        out_specs=_tile_spec(D),
        out_shape=jax.ShapeDtypeStruct((bl, nt * TT, D), F32),
        compiler_params=_cp(("arbitrary", "arbitrary")),
    )(ctx, x, pe)


def pre_mix(xs, mod, n1, w_int, nc):
    bl, s, _ = xs.shape

    def body(x_ref, mod_ref, n_ref, w_ref, za_ref, zu_ref, zp_ref, u_s):
        r, xn = rms_stats(x_ref[...])
        h = xn * n_ref[...] * (1.0 + mod_ref[1:2, :]) + mod_ref[0:1, :]
        z = dot_nt(h.astype(BF16), w_ref[...])
        za_ref[...] = z[:, :2 * A_W]
        _rows_to_chunks(z[:, 2 * A_W:2 * A_W + B_W], u_s, zu_ref)
        zp_ref[...] = z[:, 2 * A_W + B_W:]

    return pl.pallas_call(
        body, name="pre_mix", grid=(bl, s // TT),
        in_specs=[_tile_spec(D), _mod_spec(nc), _full_spec((1, D)), _full_spec((D_IN, D))],
        out_specs=[_tile_spec(2 * A_W), _chunk_spec(), _tile_spec(C_W)],
        out_shape=[jax.ShapeDtypeStruct((bl, s, 2 * A_W), F32), jax.ShapeDtypeStruct((bl, s // TC, ROW_W), F32),
                   jax.ShapeDtypeStruct((bl, s, C_W), F32)],
        scratch_shapes=[_chunk_scratch()],
        compiler_params=_cp(("arbitrary", "arbitrary")),
    )(xs, mod, n1, w_int)


def _seg_mean(x, seg_p):
    hi, lo = split_bf16(x)
    return dot_nn(hi, seg_p) + dot_nn(lo, seg_p)


def _sgu_forward(za, sw_ref, sbias, seg_p):
    ge = gelu(za)
    u, v = ge[:, :A_W], ge[:, A_W:]
    dv = v - _seg_mean(v, seg_p)
    rs = lax.rsqrt(_seg_mean(dv * dv, seg_p) + EPS)
    vn = dv * rs
    head = lane_group(A_W, A_W // A_HEADS)
    parts = []
    for c2 in range(TT // CHUNK):
        vb = vn[c2 * CHUNK:(c2 + 1) * CHUNK].astype(BF16)
        sc = sbias
        for h in range(A_HEADS):
            sc = sc + jnp.where(head == h, dot_nn(sw_ref[h], vb), 0.0)
        parts.append(sc)
    sg = jnp.concatenate(parts, axis=0)
    return u * sg, (u, vn, rs, sg)


def _pool_forward(zp, band_ref, icnt, wbd, pscale):
    hi, lo = split_bf16(zp)
    grp = lane_group(C_W, C_W // len(POOL_WINDOWS))
    q = jnp.zeros_like(zp)
    for i in range(len(POOL_WINDOWS)):
        t = dot_nn(band_ref[i], hi) + dot_nn(band_ref[i], lo)
        q = jnp.where(grp == i, t, q)
    q = q * icnt - zp
    o = dot_nn(q.astype(BF16), wbd)
    return o * pscale, (q, o)


def _glu_forward(y, glu_w, glu_b):
    g = gelu(y)
    sg = jax.nn.sigmoid(dot_nn(g.astype(BF16), glu_w) + glu_b)
    return g * sg, (g, sg)


_MIX_CONST_SHAPES = dict(sw=(A_HEADS, CHUNK, CHUNK), sbias=(CHUNK, A_W), seg_p=(A_W, A_W), wbd=(C_W, C_W),
                         pscale=(1, C_W), glu_w=(B_W, B_W), glu_b=(1, B_W), w_out=(D, D), n2=(1, D))


def _mix_const_specs(nc):
    return ([_full_spec(_MIX_CONST_SHAPES[k]) for k in ("sw", "sbias", "seg_p")]
            + [_kind_spec((len(POOL_WINDOWS), TT, TT), nc), _kind_spec((TT, C_W), nc)]
            + [_full_spec(_MIX_CONST_SHAPES[k]) for k in ("wbd", "pscale", "glu_w", "glu_b", "w_out", "n2")])


def _mix_const_args(cst):
    return [cst[k] for k in ("sw", "sbias", "seg_p", "band", "icnt", "wbd", "pscale", "glu_w", "glu_b", "w_out", "n2")]


def post_mix(xs, za, zp, ys, mod, cst, nc):
    bl, s, _ = xs.shape

    def body(x_ref, za_ref, zp_ref, y_ref, mod_ref, sw_ref, sbias_ref, seg_ref, band_ref, icnt_ref, wbd_ref,
             ps_ref, gw_ref, gb_ref, wo_ref, n2_ref, x1_ref, m_ref, y_s):
        a, _ = _sgu_forward(za_ref[...], sw_ref, sbias_ref[...], seg_ref[...])
        p, _ = _pool_forward(zp_ref[...], band_ref, icnt_ref[...], wbd_ref[...], ps_ref[...])
        sm, _ = _glu_forward(_chunks_to_rows(y_ref, y_s), gw_ref[...], gb_ref[...])
        cat = jnp.concatenate([a, sm, p], axis=1).astype(BF16)
        m = dot_nn(cat, wo_ref[...])
        _, mn = rms_stats(m)
        m_ref[...] = m
        x1_ref[...] = x_ref[...] + mod_ref[2:3, :] * (mn * n2_ref[...])

    return pl.pallas_call(
        body, name="post_mix", grid=(bl, s // TT),
        in_specs=[_tile_spec(D), _tile_spec(2 * A_W), _tile_spec(C_W), _chunk_spec(), _mod_spec(nc)]
        + _mix_const_specs(nc),
        out_specs=[_tile_spec(D), _tile_spec(D)],
        out_shape=[jax.ShapeDtypeStruct((bl, s, D), F32), jax.ShapeDtypeStruct((bl, s, D), F32)],
        scratch_shapes=[_chunk_scratch()],
        compiler_params=_cp(("arbitrary", "arbitrary")),
    )(xs, za, zp, ys, mod, *_mix_const_args(cst))


def post_mix_bwd(dx1, m, za, zp, ys, mod, cst, nc):
    bl, s, _ = m.shape
    nt = s // TT

    def body(dx_ref, m_ref, za_ref, zp_ref, y_ref, mod_ref, sw_ref, sbias_ref, seg_ref, band_ref, icnt_ref,
             wbd_ref, ps_ref, gw_ref, gb_ref, wo_ref, n2_ref,
             dza_ref, dzp_ref, dy_ref, cat_ref, dm_ref, gg_ref, dr_ref, st_ref, dsw_ref, dsb_ref, dwbd_ref, y_s):
        first = jnp.logical_and(pl.program_id(0) == 0, pl.program_id(1) == 0)

        @pl.when(first)
        def _():
            dsw_ref[...] = jnp.zeros_like(dsw_ref)
            dsb_ref[...] = jnp.zeros_like(dsb_ref)
            dwbd_ref[...] = jnp.zeros_like(dwbd_ref)

        seg_p = seg_ref[...]
        za = za_ref[...]
        zp_v = zp_ref[...]
        yv = _chunks_to_rows(y_ref, y_s)
        a, (u, vn, rs, sg) = _sgu_forward(za, sw_ref, sbias_ref[...], seg_p)
        p, (q, o) = _pool_forward(zp_v, band_ref, icnt_ref[...], wbd_ref[...], ps_ref[...])
        sm, (g, sig) = _glu_forward(yv, gw_ref[...], gb_ref[...])
        cat_ref[...] = jnp.concatenate([a, sm, p], axis=1).astype(BF16)

        dx = dx_ref[...]
        g1 = mod_ref[2:3, :]
        n2 = n2_ref[...]
        mv = m_ref[...]
        rm, mn = rms_stats(mv)
        st_ref[...] = jnp.zeros_like(st_ref)
        st_ref[0:1, :] = colsum(dx * (mn * n2))
        st_ref[1:2, :] = colsum(dx * g1 * mn)
        dm = rms_bwd(rm, mn, dx * g1 * n2)
        dmb = dm.astype(BF16)
        dm_ref[...] = dmb
        dcat = dot_nt(dmb, wo_ref[...])
        da, dsm, dp = dcat[:, :A_W], dcat[:, A_W:A_W + B_W], dcat[:, A_W + B_W:]

        du = da * sg
        dsv = da * u
        head = lane_group(A_W, A_W // A_HEADS)
        dvn_parts = []
        dsb_acc = jnp.zeros((CHUNK, A_W), F32)
        for c2 in range(TT // CHUNK):
            dsc = dsv[c2 * CHUNK:(c2 + 1) * CHUNK]
            dsc_b = dsc.astype(BF16)
            vb = vn[c2 * CHUNK:(c2 + 1) * CHUNK].astype(BF16)
            dsb_acc = dsb_acc + dsc
            dvn_c = jnp.zeros((CHUNK, A_W), F32)
            for h in range(A_HEADS):
                dsw_ref[h] += dot_nt(jnp.where(head == h, dsc, 0.0).astype(BF16), vb)
                dvn_c = dvn_c + jnp.where(head == h, dot_tn(sw_ref[h], dsc_b), 0.0)
            dvn_parts.append(dvn_c)
        dsb_ref[...] += dsb_acc
        dvn = jnp.concatenate(dvn_parts, axis=0)
        dv = rs * (dvn - _seg_mean(dvn, seg_p) - vn * _seg_mean(dvn * vn, seg_p))
        dza_ref[...] = jnp.concatenate([du, dv], axis=1) * gelu_grad(za)

        ps = ps_ref[...]
        do = dp * ps
        dps = colsum(dp * o)
        dob = do.astype(BF16)
        dwbd_ref[...] += dot_tn(q.astype(BF16), dob)
        dq = dot_nt(dob, wbd_ref[...])
        hi, lo = split_bf16(dq * icnt_ref[...])
        grp = lane_group(C_W, C_W // len(POOL_WINDOWS))
        dzp = -dq
        for i in range(len(POOL_WINDOWS)):
            t = dot_tn(band_ref[i], hi) + dot_tn(band_ref[i], lo)
            dzp = dzp + jnp.where(grp == i, t, 0.0)
        dzp_ref[...] = dzp

        dr = dsm * g * sig * (1.0 - sig)
        drb = dr.astype(BF16)
        dr_ref[...] = drb
        gg_ref[...] = g.astype(BF16)
        dg = dsm * sig + dot_nt(drb, gw_ref[...])
        _rows_to_chunks(dg * gelu_grad(yv), y_s, dy_ref)
        st_ref[2:3, :] = jnp.concatenate([colsum(dr), dps, jnp.zeros((1, D - B_W - C_W), F32)], axis=1)

    acc = lambda shape: pl.BlockSpec(shape, lambda b, j: (0,) * len(shape))
    return pl.pallas_call(
        body, name="post_mix_bwd", grid=(bl, nt),
        in_specs=[_tile_spec(D), _tile_spec(D), _tile_spec(2 * A_W), _tile_spec(C_W), _chunk_spec(), _mod_spec(nc)]
        + _mix_const_specs(nc),
        out_specs=[_tile_spec(2 * A_W), _tile_spec(C_W), _chunk_spec(), _tile_spec(D), _tile_spec(D),
                   _tile_spec(B_W), _tile_spec(B_W), _stat_spec(),
                   acc((A_HEADS, CHUNK, CHUNK)), acc((CHUNK, A_W)), acc((C_W, C_W))],
        out_shape=[jax.ShapeDtypeStruct((bl, s, 2 * A_W), F32), jax.ShapeDtypeStruct((bl, s, C_W), F32),
                   jax.ShapeDtypeStruct((bl, s // TC, ROW_W), F32), jax.ShapeDtypeStruct((bl, s, D), BF16),
                   jax.ShapeDtypeStruct((bl, s, D), BF16), jax.ShapeDtypeStruct((bl, s, B_W), BF16),
                   jax.ShapeDtypeStruct((bl, s, B_W), BF16), jax.ShapeDtypeStruct((bl, nt, 8, D), F32),
                   jax.ShapeDtypeStruct((A_HEADS, CHUNK, CHUNK), F32), jax.ShapeDtypeStruct((CHUNK, A_W), F32),
                   jax.ShapeDtypeStruct((C_W, C_W), F32)],
        scratch_shapes=[_chunk_scratch()],
        compiler_params=_cp(("arbitrary", "arbitrary")),
    )(dx1, m, za, zp, ys, mod, *_mix_const_args(cst))


def pre_mix_bwd(dza, dzu, dzp, xs, dxres, mod, n1, w_int, nc):
    bl, s, _ = xs.shape
    nt = s // TT

    def body(dza_ref, dzu_ref, dzp_ref, x_ref, dres_ref, mod_ref, n_ref, w_ref, dx_ref, h_ref, dz_ref, st_ref, u_s):
        dz = jnp.concatenate([dza_ref[...], _chunks_to_rows(dzu_ref, u_s), dzp_ref[...]], axis=1).astype(BF16)
        dz_ref[...] = dz
        dh = dot_nn(dz, w_ref[...])
        r, xn = rms_stats(x_ref[...])
        n1v = n_ref[...]
        sc = mod_ref[1:2, :]
        xg = xn * n1v
        h_ref[...] = (xg * (1.0 + sc) + mod_ref[0:1, :]).astype(BF16)
        dyv = dh * (1.0 + sc)
        st_ref[...] = jnp.zeros_like(st_ref)
        st_ref[0:1, :] = colsum(dh)
        st_ref[1:2, :] = colsum(dh * xg)
        st_ref[2:3, :] = colsum(dyv * xn)
        dx_ref[...] = dres_ref[...] + rms_bwd(r, xn, dyv * n1v)

    return pl.pallas_call(
        body, name="pre_mix_bwd", grid=(bl, nt),
        in_specs=[_tile_spec(2 * A_W), _chunk_spec(), _tile_spec(C_W), _tile_spec(D), _tile_spec(D), _mod_spec(nc),
                  _full_spec((1, D)), _full_spec((D_IN, D))],
        out_specs=[_tile_spec(D), _tile_spec(D), _tile_spec(D_IN), _stat_spec()],
        out_shape=[jax.ShapeDtypeStruct((bl, s, D), F32), jax.ShapeDtypeStruct((bl, s, D), BF16),
                   jax.ShapeDtypeStruct((bl, s, D_IN), BF16), jax.ShapeDtypeStruct((bl, nt, 8, D), F32)],
        scratch_shapes=[_chunk_scratch()],
        compiler_params=_cp(("arbitrary", "arbitrary")),
    )(dza, dzu, dzp, xs, dxres, mod, n1, w_int)


def _ffn_tile(s):
    return 768 if s % 768 == 0 else TT


def _ctx_rows(tf, n_ctx, j):
    return lax.broadcasted_iota(jnp.int32, (tf, 1), 0) + j * tf < n_ctx


def _mod_row(mod_ref, is_ctx, row):
    return jnp.where(is_ctx, mod_ref[0, row:row + 1, :], mod_ref[1, row:row + 1, :])


def ffn_fwd(x1, mod, n3, n4, wg_t, wu_t, wd, n_ctx):
    bl, s, _ = x1.shape
    tf = _ffn_tile(s)
    nk = D_FF // FF_CHUNK
    tile = pl.BlockSpec((None, tf, D), lambda b, j, k: (b, j, 0))
    modspec = pl.BlockSpec((None, 2, 8, D), lambda b, j, k: (b, 0, 0, 0))
    vec = pl.BlockSpec((1, D), lambda b, j, k: (0, 0))
    wspec = pl.BlockSpec((FF_CHUNK, D), lambda b, j, k: (jnp.minimum(k, nk - 1), 0))
    wprev = pl.BlockSpec((FF_CHUNK, D), lambda b, j, k: (jnp.maximum(k - 1, 0), 0))

    def body(x_ref, mod_ref, n3_ref, n4_ref, wg_ref, wu_ref, wd_ref, x2_ref, f_ref, h_s, acc_s, act_s):
        j, k = pl.program_id(1), pl.program_id(2)

        @pl.when(k == 0)
        def _():
            is_ctx = _ctx_rows(tf, n_ctx, j)
            _, xn = rms_stats(x_ref[...])
            h_s[...] = (xn * n3_ref[...] * (1.0 + _mod_row(mod_ref, is_ctx, 4)) + _mod_row(mod_ref, is_ctx, 3)).astype(BF16)
            acc_s[...] = jnp.zeros_like(acc_s)
            act_s[1] = jnp.zeros((tf, FF_CHUNK), BF16)

        acc_s[...] += dot_nn(act_s[(k + 1) % 2], wd_ref[...])
        h = h_s[...]
        gate = dot_nt(h, wg_ref[...])
        up = dot_nt(h, wu_ref[...])
        act_s[k % 2] = ((gate * jax.nn.sigmoid(gate)) * up).astype(BF16)

        @pl.when(k == nk)
        def _():
            f = acc_s[...]
            f_ref[...] = f
            _, fn = rms_stats(f)
            x2_ref[...] = x_ref[...] + _mod_row(mod_ref, _ctx_rows(tf, n_ctx, j), 5) * (fn * n4_ref[...])

    return pl.pallas_call(
        body, name="ffn_fwd", grid=(bl, s // tf, nk + 1),
        in_specs=[tile, modspec, vec, vec, wspec, wspec, wprev],
        out_specs=[tile, tile],
        out_shape=[jax.ShapeDtypeStruct((bl, s, D), F32), jax.ShapeDtypeStruct((bl, s, D), F32)],
        scratch_shapes=[pltpu.VMEM((tf, D), BF16), pltpu.VMEM((tf, D), F32), pltpu.VMEM((2, tf, FF_CHUNK), BF16)],
        compiler_params=_cp(("arbitrary", "arbitrary", "arbitrary")),
    )(x1, mod, n3, n4, wg_t, wu_t, wd)


def ffn_bwd(dx2, x1, f, mod, n3, n4, wg_t, wu_t, wd, n_ctx):
    bl, s, _ = x1.shape
    tf = _ffn_tile(s)
    nt = s // tf
    nk = D_FF // FF_CHUNK
    tile = pl.BlockSpec((None, tf, D), lambda b, j, k: (b, j, 0))
    ftile = pl.BlockSpec((None, tf, FF_CHUNK), lambda b, j, k: (b, j, jnp.minimum(k, nk - 1)))
    modspec = pl.BlockSpec((None, 2, 8, D), lambda b, j, k: (b, 0, 0, 0))
    vec = pl.BlockSpec((1, D), lambda b, j, k: (0, 0))
    wspec = pl.BlockSpec((FF_CHUNK, D), lambda b, j, k: (jnp.minimum(k, nk - 1), 0))
    wprev = pl.BlockSpec((FF_CHUNK, D), lambda b, j, k: (jnp.maximum(k - 1, 0), 0))
    stat = pl.BlockSpec((None, None, 8, D), lambda b, j, k: (b, j, 0, 0))

    def split_sum(is_ctx, v, st_ref, row):
        st_ref[row:row + 1, :] = colsum(jnp.where(is_ctx, 0.0, v))
        st_ref[row + 5:row + 6, :] = colsum(jnp.where(is_ctx, v, 0.0))

    def body(dx_ref, x_ref, f_ref, mod_ref, n3_ref, n4_ref, wg_ref, wu_ref, wd_ref, wgp_ref, wup_ref,
             dx1_ref, h_ref, df_ref, act_ref, dgate_ref, dup_ref, st_ref, h_s, df_s, acc_s, dgate_s, dup_s):
        j, k = pl.program_id(1), pl.program_id(2)

        @pl.when(k == 0)
        def _():
            is_ctx = _ctx_rows(tf, n_ctx, j)
            dx = dx_ref[...]
            g2 = _mod_row(mod_ref, is_ctx, 5)
            n4 = n4_ref[...]
            rf, fn = rms_stats(f_ref[...])
            st_ref[...] = jnp.zeros_like(st_ref)
            split_sum(is_ctx, dx * (fn * n4), st_ref, 2)
            st_ref[4:5, :] = colsum(dx * g2 * fn)
            df = rms_bwd(rf, fn, dx * g2 * n4).astype(BF16)
            df_s[...] = df
            df_ref[...] = df
            _, xn = rms_stats(x_ref[...])
            h = (xn * n3_ref[...] * (1.0 + _mod_row(mod_ref, is_ctx, 4)) + _mod_row(mod_ref, is_ctx, 3)).astype(BF16)
            h_s[...] = h
            h_ref[...] = h
            acc_s[...] = jnp.zeros_like(acc_s)
            dgate_s[1] = jnp.zeros((tf, FF_CHUNK), BF16)
            dup_s[1] = jnp.zeros((tf, FF_CHUNK), BF16)

        prev = (k + 1) % 2
        acc_s[...] += dot_nn(dgate_s[prev], wgp_ref[...]) + dot_nn(dup_s[prev], wup_ref[...])
        h = h_s[...]
        gate = dot_nt(h, wg_ref[...])
        up = dot_nt(h, wu_ref[...])
        sg = jax.nn.sigmoid(gate)
        silu = gate * sg
        dact = dot_nt(df_s[...], wd_ref[...])
        act_ref[...] = (silu * up).astype(BF16)
        dgate = (dact * up * (sg * (1.0 + gate * (1.0 - sg)))).astype(BF16)
        dup = (dact * silu).astype(BF16)
        dgate_ref[...] = dgate
        dup_ref[...] = dup
        dgate_s[k % 2] = dgate
        dup_s[k % 2] = dup

        @pl.when(k == nk)
        def _():
            is_ctx = _ctx_rows(tf, n_ctx, j)
            dh = acc_s[...]
            r, xn = rms_stats(x_ref[...])
            n3 = n3_ref[...]
            sc = _mod_row(mod_ref, is_ctx, 4)
            xg = xn * n3
            dyv = dh * (1.0 + sc)
            split_sum(is_ctx, dh, st_ref, 0)
            split_sum(is_ctx, dh * xg, st_ref, 1)
            st_ref[3:4, :] = colsum(dyv * xn)
            dx1_ref[...] = dx_ref[...] + rms_bwd(r, xn, dyv * n3)

    return pl.pallas_call(
        body, name="ffn_bwd", grid=(bl, nt, nk + 1),
        in_specs=[tile, tile, tile, modspec, vec, vec, wspec, wspec, wspec, wprev, wprev],
        out_specs=[tile, tile, tile, ftile, ftile, ftile, stat],
        out_shape=[jax.ShapeDtypeStruct((bl, s, D), F32), jax.ShapeDtypeStruct((bl, s, D), BF16),
                   jax.ShapeDtypeStruct((bl, s, D), BF16), jax.ShapeDtypeStruct((bl, s, D_FF), BF16),
                   jax.ShapeDtypeStruct((bl, s, D_FF), BF16), jax.ShapeDtypeStruct((bl, s, D_FF), BF16),
                   jax.ShapeDtypeStruct((bl, nt, 8, D), F32)],
        scratch_shapes=[pltpu.VMEM((tf, D), BF16), pltpu.VMEM((tf, D), BF16), pltpu.VMEM((tf, D), F32),
                        pltpu.VMEM((2, tf, FF_CHUNK), BF16), pltpu.VMEM((2, tf, FF_CHUNK), BF16)],
        compiler_params=_cp(("arbitrary", "arbitrary", "arbitrary")),
    )(dx2, x1, f, mod, n3, n4, wg_t, wu_t, wd, wg_t, wu_t)


def loss_head(xs, target, nc):
    bl, s, _ = xs.shape
    nt = s // TT

    def body(x_ref, t_ref, dx_ref, l_ref):
        j = pl.program_id(1)

        @pl.when(j < nc)
        def _():
            dx_ref[...] = jnp.zeros_like(dx_ref)
            l_ref[...] = jnp.zeros_like(l_ref)

        @pl.when(j >= nc)
        def _():
            e = x_ref[...] - t_ref[...]
            dx_ref[...] = e * (1.0 / D)
            tok = jnp.mean(e * e, axis=-1, keepdims=True)
            l_ref[...] = jnp.zeros_like(l_ref) + 0.5 * jnp.sum(tok, axis=0, keepdims=True)

    return pl.pallas_call(
        body, name="loss_head", grid=(bl, nt),
        in_specs=[_tile_spec(D), pl.BlockSpec((None, TT, D), lambda b, j: (b, jnp.maximum(j - nc, 0), 0))],
        out_specs=[_tile_spec(D), pl.BlockSpec((None, None, 8, 128), lambda b, j: (b, j, 0, 0))],
        out_shape=[jax.ShapeDtypeStruct((bl, s, D), F32), jax.ShapeDtypeStruct((bl, nt, 8, 128), F32)],
        compiler_params=_cp(("arbitrary", "arbitrary")),
    )(xs, target)


def tn_matmul(a, b, name):
    t, ka = a.shape
    n = b.shape[1]
    tk = ka if ka <= 1408 else ka // 2
    tt = 512 if t % 512 == 0 else 256
    nsteps = t // tt

    def body(a_ref, b_ref, o_ref, acc_s):
        @pl.when(pl.program_id(1) == 0)
        def _():
            acc_s[...] = jnp.zeros_like(acc_s)

        acc_s[...] += dot_tn(a_ref[...], b_ref[...])

        @pl.when(pl.program_id(1) == nsteps - 1)
        def _():
            o_ref[...] = acc_s[...].astype(BF16)

    return pl.pallas_call(
        body, name=name, grid=(ka // tk, nsteps),
        in_specs=[pl.BlockSpec((tt, tk), lambda i, s: (s, i)), pl.BlockSpec((tt, n), lambda i, s: (s, 0))],
        out_specs=pl.BlockSpec((tk, n), lambda i, s: (i, 0)),
        out_shape=jax.ShapeDtypeStruct((ka, n), BF16),
        scratch_shapes=[pltpu.VMEM((tk, n), F32)],
        compiler_params=_cp(("arbitrary", "arbitrary")),
    )(a, b)


def qmm(terms, name):
    r = terms[0][0].shape[0]
    rt = r // 2 if r % 16 == 0 and r >= 512 else r
    n = len(terms)

    def body(*refs):
        acc = None
        for k in range(n):
            y = dot_nn(refs[2 * k][...].astype(BF16), refs[2 * k + 1][...])
            acc = y if acc is None else acc + y
        refs[2 * n][...] = acc

    row = pl.BlockSpec((rt, QW), lambda q, i: (i, q))
    wspec = pl.BlockSpec((None, QW, QW), lambda q, i: (q, 0, 0))
    return pl.pallas_call(
        body, name=name, grid=(4, r // rt), in_specs=[row, wspec] * n, out_specs=row,
        out_shape=jax.ShapeDtypeStruct((r, ROW_W), F32),
        compiler_params=_cp(("arbitrary", "arbitrary")),
    )(*[x for term in terms for x in term])


def _same_group(rows, cols, row_group, col_group):
    ri = jnp.bitwise_and(lax.broadcasted_iota(jnp.int32, (rows, cols), 0) // row_group, GQ - 1)
    ci = jnp.bitwise_and(lax.broadcasted_iota(jnp.int32, (rows, cols), 1) // col_group, GQ - 1)
    return ri == ci


def _spread_matrix():
    m = np.zeros((2 * SSM_P, QW), np.float32)
    for reim in range(2):
        for g in range(GQ):
            for p in range(SSM_P):
                m[reim * SSM_P + p, reim * (QW // 2) + g * SSM_P + p] = 1.0
    return jnp.asarray(m, BF16)


def assemble_ts(v, name):
    def body(v_ref, f_ref, big_ref, bigt_ref):
        keep = _same_group(GQ * SSM_H, QW, SSM_H, SSM_P)
        for e in range(TC):
            hi, lo = split_bf16(v_ref[e])
            t = jnp.where(keep, dot_nn(hi, f_ref[...]) + dot_nn(lo, f_ref[...]), 0.0)
            big_ref[e * 128:(e + 1) * 128, :] = t.astype(BF16)
            bigt_ref[:, e * 128:(e + 1) * 128] = t.T.astype(BF16)

    return pl.pallas_call(
        body, name=name, grid=(4,),
        in_specs=[pl.BlockSpec((None, TC, 128, 128), lambda q: (q, 0, 0, 0)),
                  pl.BlockSpec((128, QW), lambda q: (0, 0))],
        out_specs=[pl.BlockSpec((None, QW, QW), lambda q: (q, 0, 0))] * 2,
        out_shape=[jax.ShapeDtypeStruct((4, QW, QW), BF16), jax.ShapeDtypeStruct((4, QW, QW), BF16)],
        compiler_params=_cp(("arbitrary",)),
    )(v, _spread_matrix())


def assemble_tt(lags, name):
    def body(l_ref, m_ref, mt_ref):
        blocks = [l_ref[n] for n in range(2 * TC - 1)]
        flipped = [b.T.astype(BF16) for b in blocks]
        blocks = [b.astype(BF16) for b in blocks]
        for s in range(TC):
            for t in range(TC):
                m_ref[s * 128:(s + 1) * 128, t * 128:(t + 1) * 128] = blocks[t - s + TC - 1]
                mt_ref[t * 128:(t + 1) * 128, s * 128:(s + 1) * 128] = flipped[t - s + TC - 1]

    return pl.pallas_call(
        body, name=name, grid=(4,),
        in_specs=[pl.BlockSpec((None, 2 * TC - 1, 128, 128), lambda q: (q, 0, 0, 0))],
        out_specs=[pl.BlockSpec((None, QW, QW), lambda q: (q, 0, 0))] * 2,
        out_shape=[jax.ShapeDtypeStruct((4, QW, QW), BF16)] * 2,
        compiler_params=_cp(("arbitrary",)),
    )(lags)


def _qtn_call(body, a, b, out_shape, out_block, extra, name):
    r = a.shape[0]
    col = pl.BlockSpec((r, QW), lambda q: (0, q))
    return pl.pallas_call(
        body, name=name, grid=(4,),
        in_specs=[col, col] + [pl.BlockSpec(x.shape, lambda q: (0, 0)) for x in extra],
        out_specs=pl.BlockSpec((None,) + out_block, lambda q: (q,) + (0,) * len(out_block)),
        out_shape=jax.ShapeDtypeStruct((4,) + out_block, F32),
        compiler_params=_cp(("arbitrary",)),
    )(a, b, *extra)


def qtn_ts(a, b, name):
    def body(a_ref, b_ref, f_ref, o_ref):
        full = dot_tn(a_ref[...].astype(BF16), b_ref[...].astype(BF16))
        keep = _same_group(GQ * SSM_H, QW, SSM_H, SSM_P)
        for e in range(TC):
            hi, lo = split_bf16(jnp.where(keep, full[e * 128:(e + 1) * 128, :], 0.0))
            o_ref[e] = dot_nt(hi, f_ref[...]) + dot_nt(lo, f_ref[...])

    return _qtn_call(body, a, b, None, (TC, 128, 128), [_spread_matrix()], name)


def qtn_tt(a, b, name):
    def body(a_ref, b_ref, o_ref):
        full = dot_tn(a_ref[...].astype(BF16), b_ref[...].astype(BF16))
        for lag in range(-(TC - 1), TC):
            acc = None
            for s in range(TC):
                t = s + lag
                if 0 <= t < TC:
                    blk = full[s * 128:(s + 1) * 128, t * 128:(t + 1) * 128]
                    acc = blk if acc is None else acc + blk
            o_ref[lag + TC - 1] = acc

    return _qtn_call(body, a, b, None, (2 * TC - 1, 128, 128), [], name)


def _scan_row(i, rb, ncr, reverse):
    if not reverse:
        return i
    return jnp.where(i < ncr, ncr - 1 - i, rb - 1 - (i - ncr))


def _swap_re_im(h):
    half = QW // 2
    return jnp.concatenate([h[:, q * QW + (1 - k) * half:q * QW + (2 - k) * half] for q in range(4) for k in range(2)],
                           axis=1)


def chunk_scan(xs, lam_ab, ncr, reverse, name):
    bl, rb, _ = xs.shape

    def body(x_ref, l_ref, hp_ref):
        la, lb = l_ref[0:1, :], l_ref[1:2, :]

        def step(i, h):
            row = _scan_row(i, rb, ncr, reverse)
            hp_ref[pl.ds(row, 1), :] = h
            return la * h + lb * _swap_re_im(h) + x_ref[pl.ds(row, 1), :]

        lax.fori_loop(0, rb, step, jnp.zeros((1, ROW_W), F32))

    blk = pl.BlockSpec((None, rb, ROW_W), lambda b: (b, 0, 0))
    return pl.pallas_call(
        body, name=name, grid=(bl,),
        in_specs=[blk, pl.BlockSpec((8, ROW_W), lambda b: (0, 0))], out_specs=blk,
        out_shape=jax.ShapeDtypeStruct((bl, rb, ROW_W), F32),
        compiler_params=_cp(("arbitrary",)),
    )(xs, lam_ab)


def chunk_scan_bwd(dhp, hp, lam_ab, ncr, reverse, name):
    bl, rb, _ = dhp.shape

    def body(d_ref, hp_ref, l_ref, g_ref, dl_ref):
        la, lb = l_ref[0:1, :], l_ref[1:2, :]

        dl_ref[...] = jnp.zeros_like(dl_ref)

        def step(n, g):
            row = _scan_row(rb - 1 - n, rb, ncr, reverse)
            g_ref[pl.ds(row, 1), :] = g
            pv = hp_ref[pl.ds(row, 1), :]
            dl_ref[0:1, :] += g * pv
            dl_ref[1:2, :] += g * _swap_re_im(pv)
            return d_ref[pl.ds(row, 1), :] + la * g + _swap_re_im(lb * g)

        lax.fori_loop(0, rb, step, jnp.zeros((1, ROW_W), F32))

    blk = pl.BlockSpec((None, rb, ROW_W), lambda b: (b, 0, 0))
    return pl.pallas_call(
        body, name=name, grid=(bl,),
        in_specs=[blk, blk, pl.BlockSpec((8, ROW_W), lambda b: (0, 0))],
        out_specs=[blk, pl.BlockSpec((None, 8, ROW_W), lambda b: (b, 0, 0))],
        out_shape=[jax.ShapeDtypeStruct((bl, rb, ROW_W), F32), jax.ShapeDtypeStruct((bl, 8, ROW_W), F32)],
        compiler_params=_cp(("arbitrary",)),
    )(dhp, hp, lam_ab)


def _quarter_rows(v):
    e = v.shape[0]
    return v.reshape(e, 4, 8, SSM_P, SSM_H).transpose(0, 1, 2, 4, 3).reshape(e, 4, 8 * SSM_H, SSM_P)


def _token_state_map(vr, vi):
    return jnp.concatenate([_quarter_rows(vr), _quarter_rows(vi)], axis=-1).transpose(1, 0, 2, 3)


def ssm_build(lam_re, lam_im, log_dt, b_re, b_im, c_re, c_im, d):
    dt = jnp.exp(log_dt)[..., None]
    mag = jnp.exp(lam_re * dt)
    ang = lam_im * dt
    lr, li = mag * jnp.cos(ang), mag * jnp.sin(ang)
    den = lam_re * lam_re + lam_im * lam_im
    nr = lr - 1.0
    fr = (nr * lam_re + li * lam_im) / den
    fi = (li * lam_re - nr * lam_im) / den
    bbr = fr[..., None] * b_re - fi[..., None] * b_im
    bbi = fr[..., None] * b_im + fi[..., None] * b_re
    pr, pi = [jnp.ones_like(lr)], [jnp.zeros_like(lr)]
    for _ in range(TC):
        pr, pi = pr + [pr[-1] * lr - pi[-1] * li], pi + [pr[-1] * li + pi[-1] * lr]
    pr, pi = jnp.stack(pr), jnp.stack(pi)
    clr = c_re[None] * pr[:, :, :, None, :] - c_im[None] * pi[:, :, :, None, :]
    cli = c_re[None] * pi[:, :, :, None, :] + c_im[None] * pr[:, :, :, None, :]
    same_group = jnp.asarray(np.kron(np.eye(8), np.ones((SSM_H, SSM_H))), F32)
    ein = functools.partial(jnp.einsum, precision=HI)

    out, lag_blocks = {}, {}
    for k, name in ((0, "f"), (1, "r")):
        ar, ai = _quarter_rows(bbr[k][None])[0], _quarter_rows(bbi[k][None])[0]
        cr = clr[:TC, k].reshape(TC, 4, 8 * SSM_H, SSM_P)
        ci = cli[:TC, k].reshape(TC, 4, 8 * SSM_H, SSM_P)
        lag_blocks[k] = (ein('qap,nqbp->nqab', ar, cr) - ein('qap,nqbp->nqab', ai, ci)) * same_group
        es = [TC - 1 - s for s in range(TC)] if k == 0 else list(range(TC))
        sr = jnp.stack([pr[e, k][:, :, None] * bbr[k] - pi[e, k][:, :, None] * bbi[k] for e in es])
        si = jnp.stack([pr[e, k][:, :, None] * bbi[k] + pi[e, k][:, :, None] * bbr[k] for e in es])
        out["bs_" + name] = _token_state_map(sr, si)
        et = [t + 1 for t in range(TC)] if k == 0 else [TC - t for t in range(TC)]
        crt = jnp.stack([jnp.swapaxes(clr[e, k], 1, 2) for e in et])
        cit = jnp.stack([-jnp.swapaxes(cli[e, k], 1, 2) for e in et])
        out["cst_" + name] = _token_state_map(crt, cit)
        l8r, l8i = pr[TC, k].reshape(4, 1, QW // 2), pi[TC, k].reshape(4, 1, QW // 2)
        la = jnp.concatenate([l8r, l8r], axis=1).reshape(1, ROW_W)
        lb = jnp.concatenate([-l8i, l8i], axis=1).reshape(1, ROW_W)
        out["lam_" + name] = jnp.concatenate([la, lb, jnp.zeros((6, ROW_W), F32)], axis=0)
    skip = jnp.eye(8 * SSM_H, dtype=F32)[None] * d.reshape(4, 1, 8 * SSM_H)
    center = lag_blocks[0][0] + lag_blocks[1][0] + skip
    lags = [lag_blocks[1][n] for n in range(TC - 1, 0, -1)] + [center] + [lag_blocks[0][n] for n in range(1, TC)]
    out["lags"] = jnp.stack(lags, axis=1)
    return out


def ssm_operators(mats, tag):
    ops = {}
    ops["m"], ops["mt"] = assemble_tt(mats["lags"], "ssm_map_intra" + tag)
    for dname in ("f", "r"):
        ops["bs_" + dname], ops["bst_" + dname] = assemble_ts(mats["bs_" + dname], f"ssm_map_state_in_{dname}{tag}")
        ops["cst_" + dname], ops["cs_" + dname] = assemble_ts(mats["cst_" + dname], f"ssm_map_readout_{dname}{tag}")
    return ops


def ssm_forward(u3, mats, ops, ncr):
    bl, rb, _ = u3.shape
    u = u3.reshape(bl * rb, ROW_W)
    hps, terms = {}, [(u, ops["m"])]
    for dname, rev in (("f", False), ("r", True)):
        xs = qmm([(u, ops["bs_" + dname])], "ssm_state_in_" + dname)
        hp = chunk_scan(xs.reshape(bl, rb, ROW_W), mats["lam_" + dname], ncr, rev, "ssm_scan_" + dname)
        hps[dname] = hp.reshape(bl * rb, ROW_W)
        terms.append((hps[dname], ops["cs_" + dname]))
    return qmm(terms, "ssm_output").reshape(bl, rb, ROW_W), hps


def ssm_backward(dy3, u3, hps, mats, ops, ncr):
    bl, rb, _ = u3.shape
    u = u3.reshape(bl * rb, ROW_W)
    dyr = dy3.reshape(bl * rb, ROW_W)
    cot = {"lags": qtn_tt(u, dyr, "ssm_d_intra")}
    terms = [(dyr, ops["mt"])]
    for dname, rev in (("f", False), ("r", True)):
        dhp = qmm([(dyr, ops["cst_" + dname])], "ssm_dstate_" + dname)
        g, dl = chunk_scan_bwd(dhp.reshape(bl, rb, ROW_W), hps[dname].reshape(bl, rb, ROW_W), mats["lam_" + dname],
                               ncr, rev, "ssm_scan_bwd_" + dname)
        g = g.reshape(bl * rb, ROW_W)
        cot["lam_" + dname] = jnp.sum(dl, axis=0)
        cot["bs_" + dname] = qtn_ts(u, g, "ssm_d_state_in_" + dname)
        cot["cst_" + dname] = qtn_ts(dyr, hps[dname], "ssm_d_readout_" + dname)
        terms.append((g, ops["bst_" + dname]))
    return qmm(terms, "ssm_input_grad").reshape(bl, rb, ROW_W), cot


def mod_forward(act, w_mod, b_cols):
    nl, _, wc = w_mod.shape
    r = act.shape[0]

    def body(a_ref, w_ref, b_ref, o_ref):
        o_ref[...] = dot_nn(a_ref[...].astype(BF16), w_ref[...].astype(BF16)) + b_ref[...]

    return pl.pallas_call(
        body, name="mod_forward", grid=(nl,),
        in_specs=[pl.BlockSpec((r, D), lambda l: (0, 0)), pl.BlockSpec((None, D, wc), lambda l: (l, 0, 0)),
                  pl.BlockSpec((None, 1, wc), lambda l: (l, 0, 0))],
        out_specs=pl.BlockSpec((None, r, wc), lambda l: (l, 0, 0)),
        out_shape=jax.ShapeDtypeStruct((nl, r, wc), F32),
        compiler_params=_cp(("arbitrary",)),
    )(act, w_mod, b_cols)


def mod_backward(act, dmod, dctx, w_mod):
    nl, _, wc = w_mod.shape
    r = act.shape[0]

    def body(a_ref, d_ref, c_ref, w_ref, gw_ref, gc_ref):
        gw_ref[...] = dot_tn(a_ref[...].astype(BF16), d_ref[...].astype(BF16))
        gc_ref[...] = dot_nt(c_ref[...].astype(BF16), w_ref[...].astype(BF16))

    return pl.pallas_call(
        body, name="mod_backward", grid=(nl,),
        in_specs=[pl.BlockSpec((r, D), lambda l: (0, 0)), pl.BlockSpec((None, r, wc), lambda l: (l, 0, 0)),
                  pl.BlockSpec((None, 8, wc), lambda l: (l, 0, 0)), pl.BlockSpec((None, D, wc), lambda l: (l, 0, 0))],
        out_specs=[pl.BlockSpec((None, D, wc), lambda l: (l, 0, 0)), pl.BlockSpec((None, 8, D), lambda l: (l, 0, 0))],
        out_shape=[jax.ShapeDtypeStruct((nl, D, wc), F32), jax.ShapeDtypeStruct((nl, 8, D), F32)],
        compiler_params=_cp(("arbitrary",)),
    )(act, dmod, dctx, w_mod)


def _place():
    return lax.axis_index("x"), lax.axis_index("y"), lax.axis_index("c")


def all_gather_rows(arrs, name):
    n = len(arrs)
    rs = [a.shape[1] for a in arrs]

    def body(*refs):
        x_refs, o_refs = refs[:n], refs[n:2 * n]
        send_sems, recv_sems, local_sems = refs[2 * n:]
        x, y, c = _place()
        me, sibling = (x, y, c), (x, y, 1 - c)
        chips = [(1 - x, y), (x, 1 - y), (1 - x, 1 - y)]

        def rows(a, px, py, pc):
            return o_refs[a].at[:, pl.ds((4 * px + 2 * py + pc) * rs[a], rs[a]), :]

        def copy(a, k, block, to, src=None):
            return pltpu.make_async_remote_copy(
                src_ref=rows(a, *block) if src is None else src, dst_ref=rows(a, *block),
                send_sem=send_sems.at[a, k], recv_sem=recv_sems.at[a, k], device_id=to, device_id_type=MESH)

        mine = [pltpu.make_async_copy(x_refs[a], rows(a, *me), local_sems.at[a]) for a in range(n)]
        for cp in mine:
            cp.start()
        first = []
        for a in range(n):
            first.append(copy(a, 0, me, sibling, src=x_refs[a]))
            first += [copy(a, 1 + j, me, (*chip, c), src=x_refs[a]) for j, chip in enumerate(chips)]
        for cp in first:
            cp.start()
        passed = []
        for j, chip in enumerate(chips):
            for a in range(n):
                copy(a, 1 + j, (*chip, c), me).wait_recv()
                fwd = copy(a, 4 + j, (*chip, c), sibling)
                fwd.start()
                passed.append(fwd)
        for a in range(n):
            copy(a, 0, sibling, me).wait_recv()
            for j, chip in enumerate(chips):
                copy(a, 4 + j, (*chip, 1 - c), me).wait_recv()
        for cp in first + passed:
            cp.wait_send()
        for cp in mine:
            cp.wait()

    any_spec = pl.BlockSpec(memory_space=pl.ANY)
    return pl.pallas_call(
        body, name=name,
        in_specs=[any_spec] * n, out_specs=[any_spec] * n,
        out_shape=[jax.ShapeDtypeStruct((a.shape[0], N_DEV * a.shape[1], a.shape[2]), a.dtype) for a in arrs],
        scratch_shapes=[pltpu.SemaphoreType.DMA((n, 7)), pltpu.SemaphoreType.DMA((n, 7)), pltpu.SemaphoreType.DMA((n,))],
    )(*arrs)


def all_to_all_rows(arrs, name):
    n = len(arrs)
    rs = [a.shape[1] // N_DEV for a in arrs]
    flips = [(fx, fy, fc) for fx in (0, 1) for fy in (0, 1) for fc in (0, 1)][1:]

    def body(*refs):
        x_refs, o_refs = refs[:n], refs[n:2 * n]
        send_sems, recv_sems, local_sems = refs[2 * n:]
        x, y, c = _place()
        my_idx = 4 * x + 2 * y + c

        def block(a, idx):
            return x_refs[a].at[:, pl.ds(idx * rs[a], rs[a]), :]

        mine = [pltpu.make_async_copy(block(a, my_idx), o_refs[a].at[my_idx], local_sems.at[a]) for a in range(n)]
        for cp in mine:
            cp.start()
        sends = []
        for k, (fx, fy, fc) in enumerate(flips):
            px = 1 - x if fx else x
            py = 1 - y if fy else y
            pc = 1 - c if fc else c
            p_idx = 4 * px + 2 * py + pc
            for a in range(n):
                sends.append(pltpu.make_async_remote_copy(
                    src_ref=block(a, p_idx), dst_ref=o_refs[a].at[my_idx], send_sem=send_sems.at[a, k],
                    recv_sem=recv_sems.at[a, k], device_id=(px, py, pc), device_id_type=MESH))
        for cp in sends:
            cp.start()
        for k, (fx, fy, fc) in enumerate(flips):
            px = 1 - x if fx else x
            py = 1 - y if fy else y
            pc = 1 - c if fc else c
            p_idx = 4 * px + 2 * py + pc
            for a in range(n):
                pltpu.make_async_remote_copy(
                    src_ref=block(a, p_idx), dst_ref=o_refs[a].at[p_idx], send_sem=send_sems.at[a, k],
                    recv_sem=recv_sems.at[a, k], device_id=(px, py, pc), device_id_type=MESH).wait_recv()
        for cp in sends:
            cp.wait_send()
        for cp in mine:
            cp.wait()

    any_spec = pl.BlockSpec(memory_space=pl.ANY)
    return pl.pallas_call(
        body, name=name,
        in_specs=[any_spec] * n, out_specs=[any_spec] * n,
        out_shape=[jax.ShapeDtypeStruct((N_DEV, a.shape[0], r, a.shape[2]), a.dtype) for a, r in zip(arrs, rs)],
        scratch_shapes=[pltpu.SemaphoreType.DMA((n, 7)), pltpu.SemaphoreType.DMA((n, 7)), pltpu.SemaphoreType.DMA((n,))],
    )(*arrs)


def _row_tile(rows, cap):
    best = None
    for t in range(16, min(rows, cap) + 1, 16):
        if rows % t == 0:
            best = t
    return rows if best is None else best


def adamw(w, gparts, m, v, name):
    n, nl, ra, cb = gparts.shape
    ta = _row_tile(ra, max(8, (1 << 19) // (cb * n)))

    def body(w_ref, g_ref, m_ref, v_ref, go_ref, d_ref, mo_ref, vo_ref):
        g = g_ref[0].astype(F32)
        for p in range(1, n):
            g = g + g_ref[p].astype(F32)
        mn = ADAM_B1 * m_ref[...] + (1.0 - ADAM_B1) * g
        vn = ADAM_B2 * v_ref[...] + (1.0 - ADAM_B2) * jnp.square(g)
        m_hat = mn / (1.0 - ADAM_B1 ** ADAM_STEP)
        v_hat = vn / (1.0 - ADAM_B2 ** ADAM_STEP)
        go_ref[...] = g
        d_ref[...] = -ADAM_LR * (m_hat / (jnp.sqrt(v_hat) + ADAM_EPS) + ADAM_WD * w_ref[...])
        mo_ref[...] = mn
        vo_ref[...] = vn

    blk = pl.BlockSpec((None, ta, cb), lambda l, i: (l, i, 0))
    gblk = pl.BlockSpec((n, None, ta, cb), lambda l, i: (0, l, i, 0))
    shp = jax.ShapeDtypeStruct((nl, ra, cb), F32)
    return pl.pallas_call(
        body, name=name, grid=(nl, ra // ta),
        in_specs=[blk, gblk, blk, blk], out_specs=[blk] * 4, out_shape=[shp] * 4,
        compiler_params=_cp(("arbitrary", "arbitrary")),
    )(w, gparts, m, v)


def _sincos_2d(rows, cols, dim):
    quarter = dim // 4
    omega = 1.0 / (10000.0 ** (jnp.arange(quarter, dtype=F32) / quarter))
    r = jnp.arange(rows, dtype=F32)[:, None] * omega
    cc = jnp.arange(cols, dtype=F32)[:, None] * omega
    er = jnp.concatenate([jnp.sin(r), jnp.cos(r)], axis=-1)
    ec = jnp.concatenate([jnp.sin(cc), jnp.cos(cc)], axis=-1)
    pe = jnp.concatenate([jnp.broadcast_to(er[:, None, :], (rows, cols, dim // 2)),
                          jnp.broadcast_to(ec[None, :, :], (rows, cols, dim // 2))], axis=-1)
    return pe.reshape(rows * cols, dim)


def _pool_constants():
    nw = len(POOL_WINDOWS)
    band = np.zeros((2, nw, TT, TT), np.float32)
    icnt = np.zeros((2, TT, C_W), np.float32)
    for kind, n in ((0, TT), (1, GRID_W)):
        for i, w in enumerate(POOL_WINDOWS):
            for t in range(TT):
                base, tl = (t // n) * n, t % n
                lo = min(max(tl - w // 2, 0), n)
                hi = min(max(tl - w // 2 + w, 0), n)
                band[kind, i, t, base + lo:base + hi] = 1.0
                icnt[kind, t, i * (C_W // nw):(i + 1) * (C_W // nw)] = 1.0 / (hi - lo)
    return jnp.asarray(band, BF16), jnp.asarray(icnt, F32)


def _block_diag(blocks):
    n, a, _ = blocks.shape
    return jnp.einsum('gab,gh->gahb', blocks, jnp.eye(n, dtype=F32), precision=HI).reshape(n * a, n * a)


def _block_diag_parts(mat, n):
    a = mat.shape[0] // n
    m4 = mat.reshape(n, a, n, a)
    return jnp.stack([m4[g, :, g, :] for g in range(n)])


_SMALL = ("c_ctx", "b_mod", "norm_mix_pre", "norm_mix_post", "norm_ffn_pre", "norm_ffn_post", "sgu_w", "sgu_b",
          "ssm_lam_re", "ssm_lam_im", "ssm_log_dt", "ssm_b_re", "ssm_b_im", "ssm_c_re", "ssm_c_im", "ssm_d",
          "glu_b", "pool_w", "pool_scale")
_WEIGHTS = ("c_ctx", "w_mod", "b_mod", "norm_mix_pre", "norm_mix_post", "norm_ffn_pre", "norm_ffn_post", "w_in", "w_out",
            "sgu_w", "sgu_b", "ssm_lam_re", "ssm_lam_im", "ssm_log_dt", "ssm_b_re", "ssm_b_im", "ssm_c_re", "ssm_c_im",
            "ssm_d", "glu_w", "glu_b", "pool_w", "pool_scale", "ffn_w_gate", "ffn_w_up", "ffn_w_down")


def _pack_rows(a):
    flat = a.reshape(-1)
    rows = -(-flat.shape[0] // D)
    rows8 = -(-rows // 8) * 8
    return jnp.pad(flat, (0, rows8 * D - flat.shape[0])).reshape(rows8, D)


def _pack(tree):
    packed = jnp.concatenate([_pack_rows(tree[k]) for k in _SMALL], axis=0)
    return jnp.pad(packed, ((0, -packed.shape[0] % 64), (0, 0)))


def _unpack(packed, like):
    out, at = {}, 0
    for k in _SMALL:
        size = int(np.prod(like[k].shape))
        rows8 = -(-(-(-size // D)) // 8) * 8
        out[k] = packed[at:at + rows8].reshape(-1)[:size].reshape(like[k].shape)
        at += rows8
    return out


def kernel(x, c, ctx, c_ctx, w_mod, b_mod, norm_mix_pre, norm_mix_post, norm_ffn_pre, norm_ffn_post, w_in, w_out, sgu_w, sgu_b, ssm_lam_re, ssm_lam_im, ssm_log_dt, ssm_b_re, ssm_b_im, ssm_c_re, ssm_c_im, ssm_d, glu_w, glu_b, pool_w, pool_scale, ffn_w_gate, ffn_w_up, ffn_w_down, loss_target, m_c_ctx, m_w_mod, m_b_mod, m_norm_mix_pre, m_norm_mix_post, m_norm_ffn_pre, m_norm_ffn_post, m_w_in, m_w_out, m_sgu_w, m_sgu_b, m_ssm_lam_re, m_ssm_lam_im, m_ssm_log_dt, m_ssm_b_re, m_ssm_b_im, m_ssm_c_re, m_ssm_c_im, m_ssm_d, m_glu_w, m_glu_b, m_pool_w, m_pool_scale, m_ffn_w_gate, m_ffn_w_up, m_ffn_w_down, v_c_ctx, v_w_mod, v_b_mod, v_norm_mix_pre, v_norm_mix_post, v_norm_ffn_pre, v_norm_ffn_post, v_w_in, v_w_out, v_sgu_w, v_sgu_b, v_ssm_lam_re, v_ssm_lam_im, v_ssm_log_dt, v_ssm_b_re, v_ssm_b_im, v_ssm_c_re, v_ssm_c_im, v_ssm_d, v_glu_w, v_glu_b, v_pool_w, v_pool_scale, v_ffn_w_gate, v_ffn_w_up, v_ffn_w_down):
    wts = dict(c_ctx=c_ctx, w_mod=w_mod, b_mod=b_mod, norm_mix_pre=norm_mix_pre, norm_mix_post=norm_mix_post,
               norm_ffn_pre=norm_ffn_pre, norm_ffn_post=norm_ffn_post, w_in=w_in, w_out=w_out, sgu_w=sgu_w, sgu_b=sgu_b,
               ssm_lam_re=ssm_lam_re, ssm_lam_im=ssm_lam_im, ssm_log_dt=ssm_log_dt, ssm_b_re=ssm_b_re, ssm_b_im=ssm_b_im,
               ssm_c_re=ssm_c_re, ssm_c_im=ssm_c_im, ssm_d=ssm_d, glu_w=glu_w, glu_b=glu_b, pool_w=pool_w,
               pool_scale=pool_scale, ffn_w_gate=ffn_w_gate, ffn_w_up=ffn_w_up, ffn_w_down=ffn_w_down)
    mom_m = dict(c_ctx=m_c_ctx, w_mod=m_w_mod, b_mod=m_b_mod, norm_mix_pre=m_norm_mix_pre, norm_mix_post=m_norm_mix_post,
                 norm_ffn_pre=m_norm_ffn_pre, norm_ffn_post=m_norm_ffn_post, w_in=m_w_in, w_out=m_w_out, sgu_w=m_sgu_w,
                 sgu_b=m_sgu_b, ssm_lam_re=m_ssm_lam_re, ssm_lam_im=m_ssm_lam_im, ssm_log_dt=m_ssm_log_dt,
                 ssm_b_re=m_ssm_b_re, ssm_b_im=m_ssm_b_im, ssm_c_re=m_ssm_c_re, ssm_c_im=m_ssm_c_im, ssm_d=m_ssm_d,
                 glu_w=m_glu_w, glu_b=m_glu_b, pool_w=m_pool_w, pool_scale=m_pool_scale, ffn_w_gate=m_ffn_w_gate,
                 ffn_w_up=m_ffn_w_up, ffn_w_down=m_ffn_w_down)
    mom_v = dict(c_ctx=v_c_ctx, w_mod=v_w_mod, b_mod=v_b_mod, norm_mix_pre=v_norm_mix_pre, norm_mix_post=v_norm_mix_post,
                 norm_ffn_pre=v_norm_ffn_pre, norm_ffn_post=v_norm_ffn_post, w_in=v_w_in, w_out=v_w_out, sgu_w=v_sgu_w,
                 sgu_b=v_sgu_b, ssm_lam_re=v_ssm_lam_re, ssm_lam_im=v_ssm_lam_im, ssm_log_dt=v_ssm_log_dt,
                 ssm_b_re=v_ssm_b_re, ssm_b_im=v_ssm_b_im, ssm_c_re=v_ssm_c_re, ssm_c_im=v_ssm_c_im, ssm_d=v_ssm_d,
                 glu_w=v_glu_w, glu_b=v_glu_b, pool_w=v_pool_w, pool_scale=v_pool_scale, ffn_w_gate=v_ffn_w_gate,
                 ffn_w_up=v_ffn_w_up, ffn_w_down=v_ffn_w_down)

    bl, seq, _ = x.shape
    n_ctx = ctx.shape[1]
    assert n_ctx == TT and seq % TT == 0 and seq % GRID_W == 0
    depth = w_in.shape[0]
    nc = n_ctx // TT
    ncr = n_ctx // TC
    s_all = n_ctx + seq
    nt = s_all // TT
    t_all = bl * s_all
    n_batch = bl * N_DEV
    my_idx = 4 * lax.axis_index("x") + 2 * lax.axis_index("y") + lax.axis_index("c")
    wc = w_mod.shape[2]

    c_rows = jnp.pad(c, ((0, 8 - bl), (0, 0))) if bl < 8 else c
    rc = c_rows.shape[0]
    (c_all,) = all_gather_rows([c_rows[None]], "gather_c")
    c_all = c_all[0].reshape(N_DEV, rc, D)[:, :bl].reshape(n_batch, D)
    r_act = -(-(n_batch + 1) // 16) * 16
    pre_act = jnp.concatenate([c_all, c_ctx[None, :], jnp.zeros((r_act - n_batch - 1, D), F32)], axis=0)
    act = jax.nn.silu(pre_act)
    b_cols = lax.dynamic_slice_in_dim(b_mod, my_idx * wc, wc, axis=1)[:, None, :]
    mod_cols = mod_forward(act, w_mod, b_cols)
    (mod_all,) = all_gather_rows([mod_cols], "gather_mod")
    mod_all = mod_all.reshape(depth, N_DEV, r_act, wc).transpose(0, 2, 1, 3).reshape(depth, r_act, 6, D)
    mod_lat = lax.dynamic_slice_in_dim(mod_all, my_idx * bl, bl, axis=1)
    mod_ctx = jnp.broadcast_to(mod_all[:, n_batch:n_batch + 1], (depth, bl, 6, D))
    mods = jnp.pad(jnp.stack([mod_ctx, mod_lat], axis=2), ((0, 0), (0, 0), (0, 0), (0, 2), (0, 0)))

    tr = lambda a: jnp.swapaxes(a, 1, 2).astype(BF16)
    w_int, w_o, g_w, wg_t, wu_t, w_d = all_gather_rows(
        [tr(w_in), w_out.astype(BF16), glu_w.astype(BF16), tr(ffn_w_gate), tr(ffn_w_up), ffn_w_down.astype(BF16)],
        "gather_weights")

    band, icnt = _pool_constants()
    seg_p = jnp.asarray(np.kron(np.eye(A_HEADS), np.full((A_W // A_HEADS,) * 2, A_HEADS / A_W)), BF16)
    pe = _sincos_2d(seq // GRID_W, GRID_W, D)
    xs = embed_tokens(x, ctx, pe)

    saved = []
    for i in range(depth):
        mats, ssm_vjp = jax.vjp(ssm_build, ssm_lam_re[i], ssm_lam_im[i], ssm_log_dt[i], ssm_b_re[i], ssm_b_im[i],
                                ssm_c_re[i], ssm_c_im[i], ssm_d[i])
        cst = dict(sw=sgu_w[i].astype(BF16),
                   sbias=jnp.repeat(sgu_b[i].T, A_W // A_HEADS, axis=1),
                   seg_p=seg_p, band=band, icnt=icnt, wbd=_block_diag(pool_w[i]).astype(BF16),
                   pscale=pool_scale[i][None, :], glu_w=g_w[i], glu_b=glu_b[i][None, :], w_out=w_o[i],
                   n2=norm_mix_post[i][None, :])
        n1, n3, n4 = norm_mix_pre[i][None, :], norm_ffn_pre[i][None, :], norm_ffn_post[i][None, :]
        za, zu, zp = pre_mix(xs, mods[i], n1, w_int[i], nc)
        ops = ssm_operators(mats, f"_{i}")
        ys, hps = ssm_forward(zu, mats, ops, ncr)
        x1, m_pre = post_mix(xs, za, zp, ys, mods[i], cst, nc)
        x2, f_pre = ffn_fwd(x1, mods[i], n3, n4, wg_t[i], wu_t[i], w_d[i], n_ctx)
        saved.append(dict(xs=xs, za=za, zu=zu, zp=zp, ys=ys, hps=hps, x1=x1, m=m_pre, f=f_pre, cst=cst, mats=mats,
                          ops=ops, ssm_vjp=ssm_vjp, n1=n1, n3=n3, n4=n4))
        xs = x2

    dx, loss_parts = loss_head(xs, loss_target, nc)
    loss = lax.psum(jnp.sum(loss_parts[:, :, 0, 0]), ("x", "y", "c"))

    grads = {k: [None] * depth for k in _WEIGHTS}
    big = {k: [None] * depth for k in ("w_in", "w_out", "glu_w", "ffn_w_gate", "ffn_w_up", "ffn_w_down")}
    dmods = [None] * depth
    flat = lambda a: a.reshape(t_all, a.shape[-1])
    for i in reversed(range(depth)):
        sv = saved[i]
        dx1, h2, df, act_b, dgate, dup, st_f = ffn_bwd(dx, sv["x1"], sv["f"], mods[i], sv["n3"], sv["n4"],
                                                       wg_t[i], wu_t[i], w_d[i], n_ctx)
        big["ffn_w_gate"][i] = tn_matmul(flat(dgate), flat(h2), f"grad_ffn_gate_{i}")
        big["ffn_w_up"][i] = tn_matmul(flat(dup), flat(h2), f"grad_ffn_up_{i}")
        big["ffn_w_down"][i] = tn_matmul(flat(act_b), flat(df), f"grad_ffn_down_{i}")
        dza, dzp, dys, cat, dm, gg, dr, st_m, dsw, dsb, dwbd = post_mix_bwd(dx1, sv["m"], sv["za"], sv["zp"], sv["ys"],
                                                                            mods[i], sv["cst"], nc)
        big["w_out"][i] = tn_matmul(flat(cat), flat(dm), f"grad_w_out_{i}")
        big["glu_w"][i] = tn_matmul(flat(gg), flat(dr), f"grad_glu_w_{i}")
        dzu, cot = ssm_backward(dys, sv["zu"], sv["hps"], sv["mats"], sv["ops"], ncr)
        (grads["ssm_lam_re"][i], grads["ssm_lam_im"][i], grads["ssm_log_dt"][i], grads["ssm_b_re"][i],
         grads["ssm_b_im"][i], grads["ssm_c_re"][i], grads["ssm_c_im"][i], grads["ssm_d"][i]) = sv["ssm_vjp"](cot)
        dx, h1, dz, st_p = pre_mix_bwd(dza, dzu, dzp, sv["xs"], dx1, mods[i], sv["n1"], w_int[i], nc)
        big["w_in"][i] = tn_matmul(flat(dz), flat(h1), f"grad_w_in_{i}")

        tiles = lambda st, row: st[:, :, row, :]
        allsum = lambda st, row: jnp.sum(tiles(st, row), axis=(0, 1))
        grads["norm_mix_pre"][i] = allsum(st_p, 2)
        grads["norm_mix_post"][i] = allsum(st_m, 1)
        grads["norm_ffn_pre"][i] = allsum(st_f, 3)
        grads["norm_ffn_post"][i] = allsum(st_f, 4)
        misc = allsum(st_m, 2)
        grads["glu_b"][i] = misc[:B_W]
        grads["pool_scale"][i] = misc[B_W:B_W + C_W]
        grads["sgu_w"][i] = dsw
        grads["sgu_b"][i] = jnp.sum(dsb.reshape(CHUNK, A_HEADS, A_W // A_HEADS), axis=2).T
        grads["pool_w"][i] = _block_diag_parts(dwbd, len(POOL_WINDOWS))
        mix = (tiles(st_p, 0), tiles(st_p, 1), tiles(st_m, 0))
        d_lat = jnp.stack([jnp.sum(t[:, nc:], axis=1) for t in mix]
                          + [jnp.sum(tiles(st_f, r), axis=1) for r in (0, 1, 2)], axis=1).reshape(bl, 6 * D)
        d_ctx = jnp.concatenate([jnp.sum(t[:, :nc], axis=(0, 1)) for t in mix]
                                + [allsum(st_f, r) for r in (5, 6, 7)]).reshape(1, 6 * D)
        dmods[i] = jnp.concatenate([d_lat, d_ctx, jnp.zeros((8 - (bl + 1) % 8 if (bl + 1) % 8 else 0, 6 * D), F32)],
                                   axis=0)
    grad_x = dx[:, n_ctx:, :]

    dmod_local = jnp.stack(dmods)
    rd = dmod_local.shape[1]
    (dmod_all,) = all_gather_rows([dmod_local], "gather_dmod")
    dmod_cols = lax.dynamic_slice_in_dim(dmod_all, my_idx * wc, wc, axis=2).reshape(depth, N_DEV, rd, wc)
    d_lat_all = dmod_cols[:, :, :bl].reshape(depth, n_batch, wc)
    d_ctx_all = dmod_cols[:, 0, bl]
    for p in range(1, N_DEV):
        d_ctx_all = d_ctx_all + dmod_cols[:, p, bl]
    dmod_rows = jnp.concatenate([d_lat_all, d_ctx_all[:, None, :], jnp.zeros((depth, r_act - n_batch - 1, wc), F32)],
                                axis=1)
    dctx_rows = jnp.pad(d_ctx_all[:, None, :], ((0, 0), (0, 7), (0, 0)))
    g_w_mod, dact_ctx = mod_backward(act, dmod_rows, dctx_rows, w_mod)
    sig_c = jax.nn.sigmoid(c_ctx)
    dsilu_c = sig_c * (1.0 + c_ctx * (1.0 - sig_c))
    small_g = {k: (jnp.stack(grads[k]) if grads[k][0] is not None else None) for k in _SMALL}
    small_g["c_ctx"] = jnp.sum(dact_ctx[:, 0, :], axis=0) * dsilu_c
    small_g["b_mod"] = jnp.stack([jnp.sum(dmods[i][:bl + 1], axis=0) for i in range(depth)])

    packed_g = _pack(small_g).astype(BF16)
    rows_s = packed_g.shape[0]
    (gathered,) = all_gather_rows([packed_g[None]], "gather_small_grads")
    gparts = gathered.reshape(N_DEV, 1, rows_s, D)
    small_w = {k: wts[k] for k in _SMALL}
    outs = adamw(_pack(small_w)[None], gparts, _pack({k: mom_m[k] for k in _SMALL})[None],
                 _pack({k: mom_v[k] for k in _SMALL})[None], "adamw_replicated")
    res = {k: [None] * 4 for k in _WEIGHTS}
    for slot, packed in enumerate(outs):
        un = _unpack(packed[0], small_w)
        for k in _SMALL:
            res[k][slot] = un[k]

    order = ("w_in", "w_out", "glu_w", "ffn_w_gate", "ffn_w_up", "ffn_w_down")
    landed = all_to_all_rows([jnp.stack(big[k]) for k in order], "scatter_weight_grads")
    for k, parts in zip(order, landed):
        transposed = k in ("w_in", "ffn_w_gate", "ffn_w_up")
        view = (lambda a: jnp.swapaxes(a, 1, 2)) if transposed else (lambda a: a)
        o4 = adamw(view(wts[k]), parts, view(mom_m[k]), view(mom_v[k]), "adamw_" + k)
        res[k] = [view(o) for o in o4]
    res["w_mod"] = list(adamw(w_mod, g_w_mod[None], m_w_mod, v_w_mod, "adamw_w_mod"))

    return (loss, grad_x, *[res[k][0] for k in _WEIGHTS], *[res[k][1] for k in _WEIGHTS],
            *[res[k][2] for k in _WEIGHTS], *[res[k][3] for k in _WEIGHTS])
```

```python
import functools
import math

import numpy as np
import jax
import jax.numpy as jnp
from jax import lax
from jax.experimental import pallas as pl
from jax.experimental.pallas import tpu as pltpu

F32 = jnp.float32
BF16 = jnp.bfloat16
HI = lax.Precision.HIGHEST
MESH = pl.DeviceIdType.MESH

D = 1024
D_IN = 1280
D_FF = 2816
A_W = 256
B_W = 512
C_W = 256
A_HEADS = 4
CHUNK = 128
SSM_G = 32
SSM_H = 16
SSM_P = 64
GRID_W = 64
POOL_WINDOWS = (2, 4, 8, 16)
EPS = 1e-6
N_DEV = 8

TT = 256
TC = 8
ROW_W = TC * B_W
QW = ROW_W // 4
GQ = 8
FF_CHUNK = 256
VMEM_LIMIT = 60 * 1024 * 1024

ADAM_LR = 0.001
ADAM_B1 = 0.9
ADAM_B2 = 0.999
ADAM_EPS = 1e-08
ADAM_WD = 0.01
ADAM_STEP = 10


def _cp(sem):
    return pltpu.CompilerParams(dimension_semantics=sem, vmem_limit_bytes=VMEM_LIMIT)


def dot_nn(a, b):
    return jnp.dot(a, b, preferred_element_type=F32)


def dot_nt(a, b):
    return lax.dot_general(a, b, (((1,), (1,)), ((), ())), preferred_element_type=F32)


def dot_tn(a, b):
    return lax.dot_general(a, b, (((0,), (0,)), ((), ())), preferred_element_type=F32)


def split_bf16(x):
    hi = x.astype(BF16)
    lo = (x - hi.astype(F32)).astype(BF16)
    return hi, lo


def gelu(x):
    return jax.nn.gelu(x)


def gelu_grad(x):
    c = math.sqrt(2.0 / math.pi)
    t = jnp.tanh(c * (x + 0.044715 * x * x * x))
    return 0.5 * (1.0 + t) + 0.5 * x * (1.0 - t * t) * c * (1.0 + 3.0 * 0.044715 * x * x)


def rms_stats(x):
    r = lax.rsqrt(jnp.mean(x * x, axis=-1, keepdims=True) + EPS)
    return r, x * r


def rms_bwd(r, xn, dxn):
    return r * (dxn - xn * jnp.mean(dxn * xn, axis=-1, keepdims=True))


def colsum(x):
    return jnp.sum(x, axis=0, keepdims=True)


def lane_group(width, group):
    return lax.broadcasted_iota(jnp.int32, (1, width), 1) // group


def _tile_spec(width):
    return pl.BlockSpec((None, TT, width), lambda b, j: (b, j, 0))


def _mod_spec(nc):
    return pl.BlockSpec((None, None, 8, D), lambda b, j: (b, jnp.where(j >= nc, 1, 0), 0, 0))


def _full_spec(shape):
    zeros = (0,) * len(shape)
    return pl.BlockSpec(shape, lambda b, j: zeros)


def _kind_spec(shape, nc):
    zeros = (0,) * len(shape)
    return pl.BlockSpec((None,) + shape, lambda b, j: (jnp.where(j >= nc, 1, 0),) + zeros)


def _stat_spec():
    return pl.BlockSpec((None, None, 8, D), lambda b, j: (b, j, 0, 0))


def _chunk_spec():
    return pl.BlockSpec((None, TT // TC, ROW_W), lambda b, j: (b, j, 0))


def _rows_to_chunks(val, scratch, out_ref):
    for cb in range(B_W // 128):
        scratch[cb] = val[:, cb * 128:(cb + 1) * 128]
    for s in range(TC):
        for cb in range(B_W // 128):
            lo = cb * QW + s * 128
            out_ref[:, lo:lo + 128] = scratch.at[cb][pl.ds(s, TT // TC, stride=TC), :]


def _chunks_to_rows(in_ref, scratch):
    for s in range(TC):
        for cb in range(B_W // 128):
            lo = cb * QW + s * 128
            scratch.at[cb][pl.ds(s, TT // TC, stride=TC), :] = in_ref[:, lo:lo + 128]
    return jnp.concatenate([scratch[cb] for cb in range(B_W // 128)], axis=1)


def _chunk_scratch():
    return pltpu.VMEM((B_W // 128, TT, 128), F32)


def embed_tokens(x, ctx, pe):
    bl, seq, _ = x.shape
    nc = ctx.shape[1] // TT
    nt = nc + seq // TT

    def body(ctx_ref, x_ref, pe_ref, o_ref):
        j = pl.program_id(1)

        @pl.when(j < nc)
        def _():
            o_ref[...] = ctx_ref[...]

        @pl.when(j >= nc)
        def _():
            o_ref[...] = x_ref[...] + pe_ref[...]

    return pl.pallas_call(
        body, name="embed_tokens", grid=(bl, nt),
        in_specs=[pl.BlockSpec((None, TT, D), lambda b, j: (b, jnp.minimum(j, nc - 1), 0)),
                  pl.BlockSpec((None, TT, D), lambda b, j: (b, jnp.maximum(j - nc, 0), 0)),
                  pl.BlockSpec((TT, D), lambda b, j: (jnp.maximum(j - nc, 0), 0))],
        out_specs=_tile_spec(D),
        out_shape=jax.ShapeDtypeStruct((bl, nt * TT, D), F32),
        compiler_params=_cp(("arbitrary", "arbitrary")),
    )(ctx, x, pe)


def pre_mix(xs, mod, n1, w_int, nc):
    bl, s, _ = xs.shape

    def body(x_ref, mod_ref, n_ref, w_ref, za_ref, zu_ref, zp_ref, u_s):
        r, xn = rms_stats(x_ref[...])
        h = xn * n_ref[...] * (1.0 + mod_ref[1:2, :]) + mod_ref[0:1, :]
        z = dot_nt(h.astype(BF16), w_ref[...])
        za_ref[...] = z[:, :2 * A_W]
        _rows_to_chunks(z[:, 2 * A_W:2 * A_W + B_W], u_s, zu_ref)
        zp_ref[...] = z[:, 2 * A_W + B_W:]

    return pl.pallas_call(
        body, name="pre_mix", grid=(bl, s // TT),
        in_specs=[_tile_spec(D), _mod_spec(nc), _full_spec((1, D)), _full_spec((D_IN, D))],
        out_specs=[_tile_spec(2 * A_W), _chunk_spec(), _tile_spec(C_W)],
        out_shape=[jax.ShapeDtypeStruct((bl, s, 2 * A_W), F32), jax.ShapeDtypeStruct((bl, s // TC, ROW_W), F32),
                   jax.ShapeDtypeStruct((bl, s, C_W), F32)],
        scratch_shapes=[_chunk_scratch()],
        compiler_params=_cp(("arbitrary", "arbitrary")),
    )(xs, mod, n1, w_int)


def _seg_mean(x, seg_p):
    hi, lo = split_bf16(x)
    return dot_nn(hi, seg_p) + dot_nn(lo, seg_p)


def _sgu_forward(za, sw_ref, sbias, seg_p):
    ge = gelu(za)
    u, v = ge[:, :A_W], ge[:, A_W:]
    dv = v - _seg_mean(v, seg_p)
    rs = lax.rsqrt(_seg_mean(dv * dv, seg_p) + EPS)
    vn = dv * rs
    head = lane_group(A_W, A_W // A_HEADS)
    parts = []
    for c2 in range(TT // CHUNK):
        vb = vn[c2 * CHUNK:(c2 + 1) * CHUNK].astype(BF16)
        sc = sbias
        for h in range(A_HEADS):
            sc = sc + jnp.where(head == h, dot_nn(sw_ref[h], vb), 0.0)
        parts.append(sc)
    sg = jnp.concatenate(parts, axis=0)
    return u * sg, (u, vn, rs, sg)


def _pool_forward(zp, band_ref, icnt, wbd, pscale):
    hi, lo = split_bf16(zp)
    grp = lane_group(C_W, C_W // len(POOL_WINDOWS))
    q = jnp.zeros_like(zp)
    for i in range(len(POOL_WINDOWS)):
        t = dot_nn(band_ref[i], hi) + dot_nn(band_ref[i], lo)
        q = jnp.where(grp == i, t, q)
    q = q * icnt - zp
    o = dot_nn(q.astype(BF16), wbd)
    return o * pscale, (q, o)


def _glu_forward(y, glu_w, glu_b):
    g = gelu(y)
    sg = jax.nn.sigmoid(dot_nn(g.astype(BF16), glu_w) + glu_b)
    return g * sg, (g, sg)


_MIX_CONST_SHAPES = dict(sw=(A_HEADS, CHUNK, CHUNK), sbias=(CHUNK, A_W), seg_p=(A_W, A_W), wbd=(C_W, C_W),
                         pscale=(1, C_W), glu_w=(B_W, B_W), glu_b=(1, B_W), w_out=(D, D), n2=(1, D))


def _mix_const_specs(nc):
    return ([_full_spec(_MIX_CONST_SHAPES[k]) for k in ("sw", "sbias", "seg_p")]
            + [_kind_spec((len(POOL_WINDOWS), TT, TT), nc), _kind_spec((TT, C_W), nc)]
            + [_full_spec(_MIX_CONST_SHAPES[k]) for k in ("wbd", "pscale", "glu_w", "glu_b", "w_out", "n2")])


def _mix_const_args(cst):
    return [cst[k] for k in ("sw", "sbias", "seg_p", "band", "icnt", "wbd", "pscale", "glu_w", "glu_b", "w_out", "n2")]


def post_mix(xs, za, zp, ys, mod, cst, nc):
    bl, s, _ = xs.shape

    def body(x_ref, za_ref, zp_ref, y_ref, mod_ref, sw_ref, sbias_ref, seg_ref, band_ref, icnt_ref, wbd_ref,
             ps_ref, gw_ref, gb_ref, wo_ref, n2_ref, x1_ref, m_ref, y_s):
        a, _ = _sgu_forward(za_ref[...], sw_ref, sbias_ref[...], seg_ref[...])
        p, _ = _pool_forward(zp_ref[...], band_ref, icnt_ref[...], wbd_ref[...], ps_ref[...])
        sm, _ = _glu_forward(_chunks_to_rows(y_ref, y_s), gw_ref[...], gb_ref[...])
        cat = jnp.concatenate([a, sm, p], axis=1).astype(BF16)
        m = dot_nn(cat, wo_ref[...])
        _, mn = rms_stats(m)
        m_ref[...] = m
        x1_ref[...] = x_ref[...] + mod_ref[2:3, :] * (mn * n2_ref[...])

    return pl.pallas_call(
        body, name="post_mix", grid=(bl, s // TT),
        in_specs=[_tile_spec(D), _tile_spec(2 * A_W), _tile_spec(C_W), _chunk_spec(), _mod_spec(nc)]
        + _mix_const_specs(nc),
        out_specs=[_tile_spec(D), _tile_spec(D)],
        out_shape=[jax.ShapeDtypeStruct((bl, s, D), F32), jax.ShapeDtypeStruct((bl, s, D), F32)],
        scratch_shapes=[_chunk_scratch()],
        compiler_params=_cp(("arbitrary", "arbitrary")),
    )(xs, za, zp, ys, mod, *_mix_const_args(cst))


def post_mix_bwd(dx1, m, za, zp, ys, mod, cst, nc):
    bl, s, _ = m.shape
    nt = s // TT

    def body(dx_ref, m_ref, za_ref, zp_ref, y_ref, mod_ref, sw_ref, sbias_ref, seg_ref, band_ref, icnt_ref,
             wbd_ref, ps_ref, gw_ref, gb_ref, wo_ref, n2_ref,
             dza_ref, dzp_ref, dy_ref, cat_ref, dm_ref, gg_ref, dr_ref, st_ref, dsw_ref, dsb_ref, dwbd_ref, y_s):
        first = jnp.logical_and(pl.program_id(0) == 0, pl.program_id(1) == 0)

        @pl.when(first)
        def _():
            dsw_ref[...] = jnp.zeros_like(dsw_ref)
            dsb_ref[...] = jnp.zeros_like(dsb_ref)
            dwbd_ref[...] = jnp.zeros_like(dwbd_ref)

        seg_p = seg_ref[...]
        za = za_ref[...]
        zp_v = zp_ref[...]
        yv = _chunks_to_rows(y_ref, y_s)
        a, (u, vn, rs, sg) = _sgu_forward(za, sw_ref, sbias_ref[...], seg_p)
        p, (q, o) = _pool_forward(zp_v, band_ref, icnt_ref[...], wbd_ref[...], ps_ref[...])
        sm, (g, sig) = _glu_forward(yv, gw_ref[...], gb_ref[...])
        cat_ref[...] = jnp.concatenate([a, sm, p], axis=1).astype(BF16)

        dx = dx_ref[...]
        g1 = mod_ref[2:3, :]
        n2 = n2_ref[...]
        mv = m_ref[...]
        rm, mn = rms_stats(mv)
        st_ref[...] = jnp.zeros_like(st_ref)
        st_ref[0:1, :] = colsum(dx * (mn * n2))
        st_ref[1:2, :] = colsum(dx * g1 * mn)
        dm = rms_bwd(rm, mn, dx * g1 * n2)
        dmb = dm.astype(BF16)
        dm_ref[...] = dmb
        dcat = dot_nt(dmb, wo_ref[...])
        da, dsm, dp = dcat[:, :A_W], dcat[:, A_W:A_W + B_W], dcat[:, A_W + B_W:]

        du = da * sg
        dsv = da * u
        head = lane_group(A_W, A_W // A_HEADS)
        dvn_parts = []
        dsb_acc = jnp.zeros((CHUNK, A_W), F32)
        for c2 in range(TT // CHUNK):
            dsc = dsv[c2 * CHUNK:(c2 + 1) * CHUNK]
            dsc_b = dsc.astype(BF16)
            vb = vn[c2 * CHUNK:(c2 + 1) * CHUNK].astype(BF16)
            dsb_acc = dsb_acc + dsc
            dvn_c = jnp.zeros((CHUNK, A_W), F32)
            for h in range(A_HEADS):
                dsw_ref[h] += dot_nt(jnp.where(head == h, dsc, 0.0).astype(BF16), vb)
                dvn_c = dvn_c + jnp.where(head == h, dot_tn(sw_ref[h], dsc_b), 0.0)
            dvn_parts.append(dvn_c)
        dsb_ref[...] += dsb_acc
        dvn = jnp.concatenate(dvn_parts, axis=0)
        dv = rs * (dvn - _seg_mean(dvn, seg_p) - vn * _seg_mean(dvn * vn, seg_p))
        dza_ref[...] = jnp.concatenate([du, dv], axis=1) * gelu_grad(za)

        ps = ps_ref[...]
        do = dp * ps
        dps = colsum(dp * o)
        dob = do.astype(BF16)
        dwbd_ref[...] += dot_tn(q.astype(BF16), dob)
        dq = dot_nt(dob, wbd_ref[...])
        hi, lo = split_bf16(dq * icnt_ref[...])
        grp = lane_group(C_W, C_W // len(POOL_WINDOWS))
        dzp = -dq
        for i in range(len(POOL_WINDOWS)):
            t = dot_tn(band_ref[i], hi) + dot_tn(band_ref[i], lo)
            dzp = dzp + jnp.where(grp == i, t, 0.0)
        dzp_ref[...] = dzp

        dr = dsm * g * sig * (1.0 - sig)
        drb = dr.astype(BF16)
        dr_ref[...] = drb
        gg_ref[...] = g.astype(BF16)
        dg = dsm * sig + dot_nt(drb, gw_ref[...])
        _rows_to_chunks(dg * gelu_grad(yv), y_s, dy_ref)
        st_ref[2:3, :] = jnp.concatenate([colsum(dr), dps, jnp.zeros((1, D - B_W - C_W), F32)], axis=1)

    acc = lambda shape: pl.BlockSpec(shape, lambda b, j: (0,) * len(shape))
    return pl.pallas_call(
        body, name="post_mix_bwd", grid=(bl, nt),
        in_specs=[_tile_spec(D), _tile_spec(D), _tile_spec(2 * A_W), _tile_spec(C_W), _chunk_spec(), _mod_spec(nc)]
        + _mix_const_specs(nc),
        out_specs=[_tile_spec(2 * A_W), _tile_spec(C_W), _chunk_spec(), _tile_spec(D), _tile_spec(D),
                   _tile_spec(B_W), _tile_spec(B_W), _stat_spec(),
                   acc((A_HEADS, CHUNK, CHUNK)), acc((CHUNK, A_W)), acc((C_W, C_W))],
        out_shape=[jax.ShapeDtypeStruct((bl, s, 2 * A_W), F32), jax.ShapeDtypeStruct((bl, s, C_W), F32),
                   jax.ShapeDtypeStruct((bl, s // TC, ROW_W), F32), jax.ShapeDtypeStruct((bl, s, D), BF16),
                   jax.ShapeDtypeStruct((bl, s, D), BF16), jax.ShapeDtypeStruct((bl, s, B_W), BF16),
                   jax.ShapeDtypeStruct((bl, s, B_W), BF16), jax.ShapeDtypeStruct((bl, nt, 8, D), F32),
                   jax.ShapeDtypeStruct((A_HEADS, CHUNK, CHUNK), F32), jax.ShapeDtypeStruct((CHUNK, A_W), F32),
                   jax.ShapeDtypeStruct((C_W, C_W), F32)],
        scratch_shapes=[_chunk_scratch()],
        compiler_params=_cp(("arbitrary", "arbitrary")),
    )(dx1, m, za, zp, ys, mod, *_mix_const_args(cst))


def pre_mix_bwd(dza, dzu, dzp, xs, dxres, mod, n1, w_int, nc):
    bl, s, _ = xs.shape
    nt = s // TT

    def body(dza_ref, dzu_ref, dzp_ref, x_ref, dres_ref, mod_ref, n_ref, w_ref, dx_ref, h_ref, dz_ref, st_ref, u_s):
        dz = jnp.concatenate([dza_ref[...], _chunks_to_rows(dzu_ref, u_s), dzp_ref[...]], axis=1).astype(BF16)
        dz_ref[...] = dz
        dh = dot_nn(dz, w_ref[...])
        r, xn = rms_stats(x_ref[...])
        n1v = n_ref[...]
        sc = mod_ref[1:2, :]
        xg = xn * n1v
        h_ref[...] = (xg * (1.0 + sc) + mod_ref[0:1, :]).astype(BF16)
        dyv = dh * (1.0 + sc)
        st_ref[...] = jnp.zeros_like(st_ref)
        st_ref[0:1, :] = colsum(dh)
        st_ref[1:2, :] = colsum(dh * xg)
        st_ref[2:3, :] = colsum(dyv * xn)
        dx_ref[...] = dres_ref[...] + rms_bwd(r, xn, dyv * n1v)

    return pl.pallas_call(
        body, name="pre_mix_bwd", grid=(bl, nt),
        in_specs=[_tile_spec(2 * A_W), _chunk_spec(), _tile_spec(C_W), _tile_spec(D), _tile_spec(D), _mod_spec(nc),
                  _full_spec((1, D)), _full_spec((D_IN, D))],
        out_specs=[_tile_spec(D), _tile_spec(D), _tile_spec(D_IN), _stat_spec()],
        out_shape=[jax.ShapeDtypeStruct((bl, s, D), F32), jax.ShapeDtypeStruct((bl, s, D), BF16),
                   jax.ShapeDtypeStruct((bl, s, D_IN), BF16), jax.ShapeDtypeStruct((bl, nt, 8, D), F32)],
        scratch_shapes=[_chunk_scratch()],
        compiler_params=_cp(("arbitrary", "arbitrary")),
    )(dza, dzu, dzp, xs, dxres, mod, n1, w_int)


def _ffn_tile(s):
    return 768 if s % 768 == 0 else TT


def _ctx_rows(tf, n_ctx, j):
    return lax.broadcasted_iota(jnp.int32, (tf, 1), 0) + j * tf < n_ctx


def _mod_row(mod_ref, is_ctx, row):
    return jnp.where(is_ctx, mod_ref[0, row:row + 1, :], mod_ref[1, row:row + 1, :])


def ffn_fwd(x1, mod, n3, n4, wg_t, wu_t, wd, n_ctx):
    bl, s, _ = x1.shape
    tf = _ffn_tile(s)
    nk = D_FF // FF_CHUNK
    tile = pl.BlockSpec((None, tf, D), lambda b, j, k: (b, j, 0))
    modspec = pl.BlockSpec((None, 2, 8, D), lambda b, j, k: (b, 0, 0, 0))
    vec = pl.BlockSpec((1, D), lambda b, j, k: (0, 0))
    wspec = pl.BlockSpec((FF_CHUNK, D), lambda b, j, k: (k, 0))

    def body(x_ref, mod_ref, n3_ref, n4_ref, wg_ref, wu_ref, wd_ref, x2_ref, f_ref, h_s, acc_s):
        j, k = pl.program_id(1), pl.program_id(2)

        @pl.when(k == 0)
        def _():
            is_ctx = _ctx_rows(tf, n_ctx, j)
            _, xn = rms_stats(x_ref[...])
            h_s[...] = (xn * n3_ref[...] * (1.0 + _mod_row(mod_ref, is_ctx, 4)) + _mod_row(mod_ref, is_ctx, 3)).astype(BF16)
            acc_s[...] = jnp.zeros_like(acc_s)

        h = h_s[...]
        gate = dot_nt(h, wg_ref[...])
        up = dot_nt(h, wu_ref[...])
        act = (gate * jax.nn.sigmoid(gate)) * up
        acc_s[...] += dot_nn(act.astype(BF16), wd_ref[...])

        @pl.when(k == nk - 1)
        def _():
            f = acc_s[...]
            f_ref[...] = f
            _, fn = rms_stats(f)
            x2_ref[...] = x_ref[...] + _mod_row(mod_ref, _ctx_rows(tf, n_ctx, j), 5) * (fn * n4_ref[...])

    return pl.pallas_call(
        body, name="ffn_fwd", grid=(bl, s // tf, nk),
        in_specs=[tile, modspec, vec, vec, wspec, wspec, wspec],
        out_specs=[tile, tile],
        out_shape=[jax.ShapeDtypeStruct((bl, s, D), F32), jax.ShapeDtypeStruct((bl, s, D), F32)],
        scratch_shapes=[pltpu.VMEM((tf, D), BF16), pltpu.VMEM((tf, D), F32)],
        compiler_params=_cp(("arbitrary", "arbitrary", "arbitrary")),
    )(x1, mod, n3, n4, wg_t, wu_t, wd)


def ffn_bwd(dx2, x1, f, mod, n3, n4, wg_t, wu_t, wd, n_ctx):
    bl, s, _ = x1.shape
    tf = _ffn_tile(s)
    nt = s // tf
    nk = D_FF // FF_CHUNK
    tile = pl.BlockSpec((None, tf, D), lambda b, j, k: (b, j, 0))
    ftile = pl.BlockSpec((None, tf, FF_CHUNK), lambda b, j, k: (b, j, jnp.minimum(k, nk - 1)))
    modspec = pl.BlockSpec((None, 2, 8, D), lambda b, j, k: (b, 0, 0, 0))
    vec = pl.BlockSpec((1, D), lambda b, j, k: (0, 0))
    wspec = pl.BlockSpec((FF_CHUNK, D), lambda b, j, k: (jnp.minimum(k, nk - 1), 0))
    wprev = pl.BlockSpec((FF_CHUNK, D), lambda b, j, k: (jnp.maximum(k - 1, 0), 0))
    stat = pl.BlockSpec((None, None, 8, D), lambda b, j, k: (b, j, 0, 0))

    def split_sum(is_ctx, v, st_ref, row):
        st_ref[row:row + 1, :] = colsum(jnp.where(is_ctx, 0.0, v))
        st_ref[row + 5:row + 6, :] = colsum(jnp.where(is_ctx, v, 0.0))

    def body(dx_ref, x_ref, f_ref, mod_ref, n3_ref, n4_ref, wg_ref, wu_ref, wd_ref, wgp_ref, wup_ref,
             dx1_ref, h_ref, df_ref, act_ref, dgate_ref, dup_ref, st_ref, h_s, df_s, acc_s, dgate_s, dup_s):
        j, k = pl.program_id(1), pl.program_id(2)

        @pl.when(k == 0)
        def _():
            is_ctx = _ctx_rows(tf, n_ctx, j)
            dx = dx_ref[...]
            g2 = _mod_row(mod_ref, is_ctx, 5)
            n4 = n4_ref[...]
            rf, fn = rms_stats(f_ref[...])
            st_ref[...] = jnp.zeros_like(st_ref)
            split_sum(is_ctx, dx * (fn * n4), st_ref, 2)
            st_ref[4:5, :] = colsum(dx * g2 * fn)
            df = rms_bwd(rf, fn, dx * g2 * n4).astype(BF16)
            df_s[...] = df
            df_ref[...] = df
            _, xn = rms_stats(x_ref[...])
            h = (xn * n3_ref[...] * (1.0 + _mod_row(mod_ref, is_ctx, 4)) + _mod_row(mod_ref, is_ctx, 3)).astype(BF16)
            h_s[...] = h
            h_ref[...] = h
            acc_s[...] = jnp.zeros_like(acc_s)
            dgate_s[1] = jnp.zeros((tf, FF_CHUNK), BF16)
            dup_s[1] = jnp.zeros((tf, FF_CHUNK), BF16)

        prev = (k + 1) % 2
        acc_s[...] += dot_nn(dgate_s[prev], wgp_ref[...]) + dot_nn(dup_s[prev], wup_ref[...])
        h = h_s[...]
        gate = dot_nt(h, wg_ref[...])
        up = dot_nt(h, wu_ref[...])
        sg = jax.nn.sigmoid(gate)
        silu = gate * sg
        dact = dot_nt(df_s[...], wd_ref[...])
        act_ref[...] = (silu * up).astype(BF16)
        dgate = (dact * up * (sg * (1.0 + gate * (1.0 - sg)))).astype(BF16)
        dup = (dact * silu).astype(BF16)
        dgate_ref[...] = dgate
        dup_ref[...] = dup
        dgate_s[k % 2] = dgate
        dup_s[k % 2] = dup

        @pl.when(k == nk)
        def _():
            is_ctx = _ctx_rows(tf, n_ctx, j)
            dh = acc_s[...]
            r, xn = rms_stats(x_ref[...])
            n3 = n3_ref[...]
            sc = _mod_row(mod_ref, is_ctx, 4)
            xg = xn * n3
            dyv = dh * (1.0 + sc)
            split_sum(is_ctx, dh, st_ref, 0)
            split_sum(is_ctx, dh * xg, st_ref, 1)
            st_ref[3:4, :] = colsum(dyv * xn)
            dx1_ref[...] = dx_ref[...] + rms_bwd(r, xn, dyv * n3)

    return pl.pallas_call(
        body, name="ffn_bwd", grid=(bl, nt, nk + 1),
        in_specs=[tile, tile, tile, modspec, vec, vec, wspec, wspec, wspec, wprev, wprev],
        out_specs=[tile, tile, tile, ftile, ftile, ftile, stat],
        out_shape=[jax.ShapeDtypeStruct((bl, s, D), F32), jax.ShapeDtypeStruct((bl, s, D), BF16),
                   jax.ShapeDtypeStruct((bl, s, D), BF16), jax.ShapeDtypeStruct((bl, s, D_FF), BF16),
                   jax.ShapeDtypeStruct((bl, s, D_FF), BF16), jax.ShapeDtypeStruct((bl, s, D_FF), BF16),
                   jax.ShapeDtypeStruct((bl, nt, 8, D), F32)],
        scratch_shapes=[pltpu.VMEM((tf, D), BF16), pltpu.VMEM((tf, D), BF16), pltpu.VMEM((tf, D), F32),
                        pltpu.VMEM((2, tf, FF_CHUNK), BF16), pltpu.VMEM((2, tf, FF_CHUNK), BF16)],
        compiler_params=_cp(("arbitrary", "arbitrary", "arbitrary")),
    )(dx2, x1, f, mod, n3, n4, wg_t, wu_t, wd, wg_t, wu_t)


def loss_head(xs, target, nc):
    bl, s, _ = xs.shape
    nt = s // TT

    def body(x_ref, t_ref, dx_ref, l_ref):
        j = pl.program_id(1)

        @pl.when(j < nc)
        def _():
            dx_ref[...] = jnp.zeros_like(dx_ref)
            l_ref[...] = jnp.zeros_like(l_ref)

        @pl.when(j >= nc)
        def _():
            e = x_ref[...] - t_ref[...]
            dx_ref[...] = e * (1.0 / D)
            tok = jnp.mean(e * e, axis=-1, keepdims=True)
            l_ref[...] = jnp.zeros_like(l_ref) + 0.5 * jnp.sum(tok, axis=0, keepdims=True)

    return pl.pallas_call(
        body, name="loss_head", grid=(bl, nt),
        in_specs=[_tile_spec(D), pl.BlockSpec((None, TT, D), lambda b, j: (b, jnp.maximum(j - nc, 0), 0))],
        out_specs=[_tile_spec(D), pl.BlockSpec((None, None, 8, 128), lambda b, j: (b, j, 0, 0))],
        out_shape=[jax.ShapeDtypeStruct((bl, s, D), F32), jax.ShapeDtypeStruct((bl, nt, 8, 128), F32)],
        compiler_params=_cp(("arbitrary", "arbitrary")),
    )(xs, target)


def tn_matmul(a, b, name):
    t, ka = a.shape
    n = b.shape[1]
    tk = ka if ka <= 1408 else ka // 2
    tt = 512 if t % 512 == 0 else 256
    nsteps = t // tt

    def body(a_ref, b_ref, o_ref, acc_s):
        @pl.when(pl.program_id(1) == 0)
        def _():
            acc_s[...] = jnp.zeros_like(acc_s)

        acc_s[...] += dot_tn(a_ref[...], b_ref[...])

        @pl.when(pl.program_id(1) == nsteps - 1)
        def _():
            o_ref[...] = acc_s[...].astype(BF16)

    return pl.pallas_call(
        body, name=name, grid=(ka // tk, nsteps),
        in_specs=[pl.BlockSpec((tt, tk), lambda i, s: (s, i)), pl.BlockSpec((tt, n), lambda i, s: (s, 0))],
        out_specs=pl.BlockSpec((tk, n), lambda i, s: (i, 0)),
        out_shape=jax.ShapeDtypeStruct((ka, n), BF16),
        scratch_shapes=[pltpu.VMEM((tk, n), F32)],
        compiler_params=_cp(("arbitrary", "arbitrary")),
    )(a, b)


def qmm(terms, name):
    r = terms[0][0].shape[0]
    rt = r // 2 if r % 16 == 0 and r >= 512 else r
    n = len(terms)

    def body(*refs):
        acc = None
        for k in range(n):
            y = dot_nn(refs[2 * k][...].astype(BF16), refs[2 * k + 1][...])
            acc = y if acc is None else acc + y
        refs[2 * n][...] = acc

    row = pl.BlockSpec((rt, QW), lambda q, i: (i, q))
    wspec = pl.BlockSpec((None, QW, QW), lambda q, i: (q, 0, 0))
    return pl.pallas_call(
        body, name=name, grid=(4, r // rt), in_specs=[row, wspec] * n, out_specs=row,
        out_shape=jax.ShapeDtypeStruct((r, ROW_W), F32),
        compiler_params=_cp(("arbitrary", "arbitrary")),
    )(*[x for term in terms for x in term])


def _same_group(rows, cols, row_group, col_group):
    ri = jnp.bitwise_and(lax.broadcasted_iota(jnp.int32, (rows, cols), 0) // row_group, GQ - 1)
    ci = jnp.bitwise_and(lax.broadcasted_iota(jnp.int32, (rows, cols), 1) // col_group, GQ - 1)
    return ri == ci


def _spread_matrix():
    m = np.zeros((2 * SSM_P, QW), np.float32)
    for reim in range(2):
        for g in range(GQ):
            for p in range(SSM_P):
                m[reim * SSM_P + p, reim * (QW // 2) + g * SSM_P + p] = 1.0
    return jnp.asarray(m, BF16)


def assemble_ts(v, name):
    def body(v_ref, f_ref, big_ref, bigt_ref):
        keep = _same_group(GQ * SSM_H, QW, SSM_H, SSM_P)
        for e in range(TC):
            hi, lo = split_bf16(v_ref[e])
            t = jnp.where(keep, dot_nn(hi, f_ref[...]) + dot_nn(lo, f_ref[...]), 0.0)
            big_ref[e * 128:(e + 1) * 128, :] = t.astype(BF16)
            bigt_ref[:, e * 128:(e + 1) * 128] = t.T.astype(BF16)

    return pl.pallas_call(
        body, name=name, grid=(4,),
        in_specs=[pl.BlockSpec((None, TC, 128, 128), lambda q: (q, 0, 0, 0)),
                  pl.BlockSpec((128, QW), lambda q: (0, 0))],
        out_specs=[pl.BlockSpec((None, QW, QW), lambda q: (q, 0, 0))] * 2,
        out_shape=[jax.ShapeDtypeStruct((4, QW, QW), BF16), jax.ShapeDtypeStruct((4, QW, QW), BF16)],
        compiler_params=_cp(("arbitrary",)),
    )(v, _spread_matrix())


def assemble_tt(lags, name):
    def body(l_ref, m_ref, mt_ref):
        blocks = [l_ref[n] for n in range(2 * TC - 1)]
        flipped = [b.T.astype(BF16) for b in blocks]
        blocks = [b.astype(BF16) for b in blocks]
        for s in range(TC):
            for t in range(TC):
                m_ref[s * 128:(s + 1) * 128, t * 128:(t + 1) * 128] = blocks[t - s + TC - 1]
                mt_ref[t * 128:(t + 1) * 128, s * 128:(s + 1) * 128] = flipped[t - s + TC - 1]

    return pl.pallas_call(
        body, name=name, grid=(4,),
        in_specs=[pl.BlockSpec((None, 2 * TC - 1, 128, 128), lambda q: (q, 0, 0, 0))],
        out_specs=[pl.BlockSpec((None, QW, QW), lambda q: (q, 0, 0))] * 2,
        out_shape=[jax.ShapeDtypeStruct((4, QW, QW), BF16)] * 2,
        compiler_params=_cp(("arbitrary",)),
    )(lags)


def _qtn_call(body, a, b, out_shape, out_block, extra, name):
    r = a.shape[0]
    col = pl.BlockSpec((r, QW), lambda q: (0, q))
    return pl.pallas_call(
        body, name=name, grid=(4,),
        in_specs=[col, col] + [pl.BlockSpec(x.shape, lambda q: (0, 0)) for x in extra],
        out_specs=pl.BlockSpec((None,) + out_block, lambda q: (q,) + (0,) * len(out_block)),
        out_shape=jax.ShapeDtypeStruct((4,) + out_block, F32),
        compiler_params=_cp(("arbitrary",)),
    )(a, b, *extra)


def qtn_ts(a, b, name):
    def body(a_ref, b_ref, f_ref, o_ref):
        full = dot_tn(a_ref[...].astype(BF16), b_ref[...].astype(BF16))
        keep = _same_group(GQ * SSM_H, QW, SSM_H, SSM_P)
        for e in range(TC):
            hi, lo = split_bf16(jnp.where(keep, full[e * 128:(e + 1) * 128, :], 0.0))
            o_ref[e] = dot_nt(hi, f_ref[...]) + dot_nt(lo, f_ref[...])

    return _qtn_call(body, a, b, None, (TC, 128, 128), [_spread_matrix()], name)


def qtn_tt(a, b, name):
    def body(a_ref, b_ref, o_ref):
        full = dot_tn(a_ref[...].astype(BF16), b_ref[...].astype(BF16))
        for lag in range(-(TC - 1), TC):
            acc = None
            for s in range(TC):
                t = s + lag
                if 0 <= t < TC:
                    blk = full[s * 128:(s + 1) * 128, t * 128:(t + 1) * 128]
                    acc = blk if acc is None else acc + blk
            o_ref[lag + TC - 1] = acc

    return _qtn_call(body, a, b, None, (2 * TC - 1, 128, 128), [], name)


def _scan_row(i, rb, ncr, reverse):
    if not reverse:
        return i
    return jnp.where(i < ncr, ncr - 1 - i, rb - 1 - (i - ncr))


def _swap_re_im(h):
    half = QW // 2
    return jnp.concatenate([h[:, q * QW + (1 - k) * half:q * QW + (2 - k) * half] for q in range(4) for k in range(2)],
                           axis=1)


def chunk_scan(xs, lam_ab, ncr, reverse, name):
    bl, rb, _ = xs.shape

    def body(x_ref, l_ref, hp_ref):
        la, lb = l_ref[0:1, :], l_ref[1:2, :]

        def step(i, h):
            row = _scan_row(i, rb, ncr, reverse)
            hp_ref[pl.ds(row, 1), :] = h
            return la * h + lb * _swap_re_im(h) + x_ref[pl.ds(row, 1), :]

        lax.fori_loop(0, rb, step, jnp.zeros((1, ROW_W), F32))

    blk = pl.BlockSpec((None, rb, ROW_W), lambda b: (b, 0, 0))
    return pl.pallas_call(
        body, name=name, grid=(bl,),
        in_specs=[blk, pl.BlockSpec((8, ROW_W), lambda b: (0, 0))], out_specs=blk,
        out_shape=jax.ShapeDtypeStruct((bl, rb, ROW_W), F32),
        compiler_params=_cp(("arbitrary",)),
    )(xs, lam_ab)


def chunk_scan_bwd(dhp, hp, lam_ab, ncr, reverse, name):
    bl, rb, _ = dhp.shape

    def body(d_ref, hp_ref, l_ref, g_ref, dl_ref):
        la, lb = l_ref[0:1, :], l_ref[1:2, :]

        dl_ref[...] = jnp.zeros_like(dl_ref)

        def step(n, g):
            row = _scan_row(rb - 1 - n, rb, ncr, reverse)
            g_ref[pl.ds(row, 1), :] = g
            pv = hp_ref[pl.ds(row, 1), :]
            dl_ref[0:1, :] += g * pv
            dl_ref[1:2, :] += g * _swap_re_im(pv)
            return d_ref[pl.ds(row, 1), :] + la * g + _swap_re_im(lb * g)

        lax.fori_loop(0, rb, step, jnp.zeros((1, ROW_W), F32))

    blk = pl.BlockSpec((None, rb, ROW_W), lambda b: (b, 0, 0))
    return pl.pallas_call(
        body, name=name, grid=(bl,),
        in_specs=[blk, blk, pl.BlockSpec((8, ROW_W), lambda b: (0, 0))],
        out_specs=[blk, pl.BlockSpec((None, 8, ROW_W), lambda b: (b, 0, 0))],
        out_shape=[jax.ShapeDtypeStruct((bl, rb, ROW_W), F32), jax.ShapeDtypeStruct((bl, 8, ROW_W), F32)],
        compiler_params=_cp(("arbitrary",)),
    )(dhp, hp, lam_ab)


def _quarter_rows(v):
    e = v.shape[0]
    return v.reshape(e, 4, 8, SSM_P, SSM_H).transpose(0, 1, 2, 4, 3).reshape(e, 4, 8 * SSM_H, SSM_P)


def _token_state_map(vr, vi):
    return jnp.concatenate([_quarter_rows(vr), _quarter_rows(vi)], axis=-1).transpose(1, 0, 2, 3)


def ssm_build(lam_re, lam_im, log_dt, b_re, b_im, c_re, c_im, d):
    dt = jnp.exp(log_dt)[..., None]
    mag = jnp.exp(lam_re * dt)
    ang = lam_im * dt
    lr, li = mag * jnp.cos(ang), mag * jnp.sin(ang)
    den = lam_re * lam_re + lam_im * lam_im
    nr = lr - 1.0
    fr = (nr * lam_re + li * lam_im) / den
    fi = (li * lam_re - nr * lam_im) / den
    bbr = fr[..., None] * b_re - fi[..., None] * b_im
    bbi = fr[..., None] * b_im + fi[..., None] * b_re
    pr, pi = [jnp.ones_like(lr)], [jnp.zeros_like(lr)]
    for _ in range(TC):
        pr, pi = pr + [pr[-1] * lr - pi[-1] * li], pi + [pr[-1] * li + pi[-1] * lr]
    pr, pi = jnp.stack(pr), jnp.stack(pi)
    clr = c_re[None] * pr[:, :, :, None, :] - c_im[None] * pi[:, :, :, None, :]
    cli = c_re[None] * pi[:, :, :, None, :] + c_im[None] * pr[:, :, :, None, :]
    same_group = jnp.asarray(np.kron(np.eye(8), np.ones((SSM_H, SSM_H))), F32)
    ein = functools.partial(jnp.einsum, precision=HI)

    out, lag_blocks = {}, {}
    for k, name in ((0, "f"), (1, "r")):
        ar, ai = _quarter_rows(bbr[k][None])[0], _quarter_rows(bbi[k][None])[0]
        cr = clr[:TC, k].reshape(TC, 4, 8 * SSM_H, SSM_P)
        ci = cli[:TC, k].reshape(TC, 4, 8 * SSM_H, SSM_P)
        lag_blocks[k] = (ein('qap,nqbp->nqab', ar, cr) - ein('qap,nqbp->nqab', ai, ci)) * same_group
        es = [TC - 1 - s for s in range(TC)] if k == 0 else list(range(TC))
        sr = jnp.stack([pr[e, k][:, :, None] * bbr[k] - pi[e, k][:, :, None] * bbi[k] for e in es])
        si = jnp.stack([pr[e, k][:, :, None] * bbi[k] + pi[e, k][:, :, None] * bbr[k] for e in es])
        out["bs_" + name] = _token_state_map(sr, si)
        et = [t + 1 for t in range(TC)] if k == 0 else [TC - t for t in range(TC)]
        crt = jnp.stack([jnp.swapaxes(clr[e, k], 1, 2) for e in et])
        cit = jnp.stack([-jnp.swapaxes(cli[e, k], 1, 2) for e in et])
        out["cst_" + name] = _token_state_map(crt, cit)
        l8r, l8i = pr[TC, k].reshape(4, 1, QW // 2), pi[TC, k].reshape(4, 1, QW // 2)
        la = jnp.concatenate([l8r, l8r], axis=1).reshape(1, ROW_W)
        lb = jnp.concatenate([-l8i, l8i], axis=1).reshape(1, ROW_W)
        out["lam_" + name] = jnp.concatenate([la, lb, jnp.zeros((6, ROW_W), F32)], axis=0)
    skip = jnp.eye(8 * SSM_H, dtype=F32)[None] * d.reshape(4, 1, 8 * SSM_H)
    center = lag_blocks[0][0] + lag_blocks[1][0] + skip
    lags = [lag_blocks[1][n] for n in range(TC - 1, 0, -1)] + [center] + [lag_blocks[0][n] for n in range(1, TC)]
    out["lags"] = jnp.stack(lags, axis=1)
    return out


def ssm_operators(mats, tag):
    ops = {}
    ops["m"], ops["mt"] = assemble_tt(mats["lags"], "ssm_map_intra" + tag)
    for dname in ("f", "r"):
        ops["bs_" + dname], ops["bst_" + dname] = assemble_ts(mats["bs_" + dname], f"ssm_map_state_in_{dname}{tag}")
        ops["cst_" + dname], ops["cs_" + dname] = assemble_ts(mats["cst_" + dname], f"ssm_map_readout_{dname}{tag}")
    return ops


def ssm_forward(u3, mats, ops, ncr):
    bl, rb, _ = u3.shape
    u = u3.reshape(bl * rb, ROW_W)
    hps, terms = {}, [(u, ops["m"])]
    for dname, rev in (("f", False), ("r", True)):
        xs = qmm([(u, ops["bs_" + dname])], "ssm_state_in_" + dname)
        hp = chunk_scan(xs.reshape(bl, rb, ROW_W), mats["lam_" + dname], ncr, rev, "ssm_scan_" + dname)
        hps[dname] = hp.reshape(bl * rb, ROW_W)
        terms.append((hps[dname], ops["cs_" + dname]))
    return qmm(terms, "ssm_output").reshape(bl, rb, ROW_W), hps


def ssm_backward(dy3, u3, hps, mats, ops, ncr):
    bl, rb, _ = u3.shape
    u = u3.reshape(bl * rb, ROW_W)
    dyr = dy3.reshape(bl * rb, ROW_W)
    cot = {"lags": qtn_tt(u, dyr, "ssm_d_intra")}
    terms = [(dyr, ops["mt"])]
    for dname, rev in (("f", False), ("r", True)):
        dhp = qmm([(dyr, ops["cst_" + dname])], "ssm_dstate_" + dname)
        g, dl = chunk_scan_bwd(dhp.reshape(bl, rb, ROW_W), hps[dname].reshape(bl, rb, ROW_W), mats["lam_" + dname],
                               ncr, rev, "ssm_scan_bwd_" + dname)
        g = g.reshape(bl * rb, ROW_W)
        cot["lam_" + dname] = jnp.sum(dl, axis=0)
        cot["bs_" + dname] = qtn_ts(u, g, "ssm_d_state_in_" + dname)
        cot["cst_" + dname] = qtn_ts(dyr, hps[dname], "ssm_d_readout_" + dname)
        terms.append((g, ops["bst_" + dname]))
    return qmm(terms, "ssm_input_grad").reshape(bl, rb, ROW_W), cot


def mod_forward(act, w_mod, b_cols):
    nl, _, wc = w_mod.shape
    r = act.shape[0]

    def body(a_ref, w_ref, b_ref, o_ref):
        o_ref[...] = dot_nn(a_ref[...].astype(BF16), w_ref[...].astype(BF16)) + b_ref[...]

    return pl.pallas_call(
        body, name="mod_forward", grid=(nl,),
        in_specs=[pl.BlockSpec((r, D), lambda l: (0, 0)), pl.BlockSpec((None, D, wc), lambda l: (l, 0, 0)),
                  pl.BlockSpec((None, 1, wc), lambda l: (l, 0, 0))],
        out_specs=pl.BlockSpec((None, r, wc), lambda l: (l, 0, 0)),
        out_shape=jax.ShapeDtypeStruct((nl, r, wc), F32),
        compiler_params=_cp(("arbitrary",)),
    )(act, w_mod, b_cols)


def mod_backward(act, dmod, dctx, w_mod):
    nl, _, wc = w_mod.shape
    r = act.shape[0]

    def body(a_ref, d_ref, c_ref, w_ref, gw_ref, gc_ref):
        gw_ref[...] = dot_tn(a_ref[...].astype(BF16), d_ref[...].astype(BF16))
        gc_ref[...] = dot_nt(c_ref[...].astype(BF16), w_ref[...].astype(BF16))

    return pl.pallas_call(
        body, name="mod_backward", grid=(nl,),
        in_specs=[pl.BlockSpec((r, D), lambda l: (0, 0)), pl.BlockSpec((None, r, wc), lambda l: (l, 0, 0)),
                  pl.BlockSpec((None, 8, wc), lambda l: (l, 0, 0)), pl.BlockSpec((None, D, wc), lambda l: (l, 0, 0))],
        out_specs=[pl.BlockSpec((None, D, wc), lambda l: (l, 0, 0)), pl.BlockSpec((None, 8, D), lambda l: (l, 0, 0))],
        out_shape=[jax.ShapeDtypeStruct((nl, D, wc), F32), jax.ShapeDtypeStruct((nl, 8, D), F32)],
        compiler_params=_cp(("arbitrary",)),
    )(act, dmod, dctx, w_mod)


def _place():
    return lax.axis_index("x"), lax.axis_index("y"), lax.axis_index("c")


def all_gather_rows(arrs, name):
    n = len(arrs)
    rs = [a.shape[1] for a in arrs]

    def body(*refs):
        x_refs, o_refs = refs[:n], refs[n:2 * n]
        send_sems, recv_sems, local_sems = refs[2 * n:]
        x, y, c = _place()
        me, sibling = (x, y, c), (x, y, 1 - c)
        chips = [(1 - x, y), (x, 1 - y), (1 - x, 1 - y)]

        def rows(a, px, py, pc):
            return o_refs[a].at[:, pl.ds((4 * px + 2 * py + pc) * rs[a], rs[a]), :]

        def copy(a, k, block, to, src=None):
            return pltpu.make_async_remote_copy(
                src_ref=rows(a, *block) if src is None else src, dst_ref=rows(a, *block),
                send_sem=send_sems.at[a, k], recv_sem=recv_sems.at[a, k], device_id=to, device_id_type=MESH)

        mine = [pltpu.make_async_copy(x_refs[a], rows(a, *me), local_sems.at[a]) for a in range(n)]
        for cp in mine:
            cp.start()
        first = []
        for a in range(n):
            first.append(copy(a, 0, me, sibling, src=x_refs[a]))
            first += [copy(a, 1 + j, me, (*chip, c), src=x_refs[a]) for j, chip in enumerate(chips)]
        for cp in first:
            cp.start()
        passed = []
        for j, chip in enumerate(chips):
            for a in range(n):
                copy(a, 1 + j, (*chip, c), me).wait_recv()
                fwd = copy(a, 4 + j, (*chip, c), sibling)
                fwd.start()
                passed.append(fwd)
        for a in range(n):
            copy(a, 0, sibling, me).wait_recv()
            for j, chip in enumerate(chips):
                copy(a, 4 + j, (*chip, 1 - c), me).wait_recv()
        for cp in first + passed:
            cp.wait_send()
        for cp in mine:
            cp.wait()

    any_spec = pl.BlockSpec(memory_space=pl.ANY)
    return pl.pallas_call(
        body, name=name,
        in_specs=[any_spec] * n, out_specs=[any_spec] * n,
        out_shape=[jax.ShapeDtypeStruct((a.shape[0], N_DEV * a.shape[1], a.shape[2]), a.dtype) for a in arrs],
        scratch_shapes=[pltpu.SemaphoreType.DMA((n, 7)), pltpu.SemaphoreType.DMA((n, 7)), pltpu.SemaphoreType.DMA((n,))],
    )(*arrs)


def all_to_all_rows(arrs, name):
    n = len(arrs)
    rs = [a.shape[1] // N_DEV for a in arrs]
    flips = [(fx, fy, fc) for fx in (0, 1) for fy in (0, 1) for fc in (0, 1)][1:]

    def body(*refs):
        x_refs, o_refs = refs[:n], refs[n:2 * n]
        send_sems, recv_sems, local_sems = refs[2 * n:]
        x, y, c = _place()
        my_idx = 4 * x + 2 * y + c

        def block(a, idx):
            return x_refs[a].at[:, pl.ds(idx * rs[a], rs[a]), :]

        mine = [pltpu.make_async_copy(block(a, my_idx), o_refs[a].at[my_idx], local_sems.at[a]) for a in range(n)]
        for cp in mine:
            cp.start()
        sends = []
        for k, (fx, fy, fc) in enumerate(flips):
            px = 1 - x if fx else x
            py = 1 - y if fy else y
            pc = 1 - c if fc else c
            p_idx = 4 * px + 2 * py + pc
            for a in range(n):
                sends.append(pltpu.make_async_remote_copy(
                    src_ref=block(a, p_idx), dst_ref=o_refs[a].at[my_idx], send_sem=send_sems.at[a, k],
                    recv_sem=recv_sems.at[a, k], device_id=(px, py, pc), device_id_type=MESH))
        for cp in sends:
            cp.start()
        for k, (fx, fy, fc) in enumerate(flips):
            px = 1 - x if fx else x
            py = 1 - y if fy else y
            pc = 1 - c if fc else c
            p_idx = 4 * px + 2 * py + pc
            for a in range(n):
                pltpu.make_async_remote_copy(
                    src_ref=block(a, p_idx), dst_ref=o_refs[a].at[p_idx], send_sem=send_sems.at[a, k],
                    recv_sem=recv_sems.at[a, k], device_id=(px, py, pc), device_id_type=MESH).wait_recv()
        for cp in sends:
            cp.wait_send()
        for cp in mine:
            cp.wait()

    any_spec = pl.BlockSpec(memory_space=pl.ANY)
    return pl.pallas_call(
        body, name=name,
        in_specs=[any_spec] * n, out_specs=[any_spec] * n,
        out_shape=[jax.ShapeDtypeStruct((N_DEV, a.shape[0], r, a.shape[2]), a.dtype) for a, r in zip(arrs, rs)],
        scratch_shapes=[pltpu.SemaphoreType.DMA((n, 7)), pltpu.SemaphoreType.DMA((n, 7)), pltpu.SemaphoreType.DMA((n,))],
    )(*arrs)


def _peers():
    x, y, c = _place()
    out = []
    for fx in (0, 1):
        for fy in (0, 1):
            for fc in (0, 1):
                if fx or fy or fc:
                    px, py, pc = (1 - x if fx else x), (1 - y if fy else y), (1 - c if fc else c)
                    out.append(((px, py, pc), 4 * px + 2 * py + pc))
    return out, 4 * x + 2 * y + c


def scatter_start(arrs, name):
    n = len(arrs)
    rs = [a.shape[1] // N_DEV for a in arrs]
    lands = [lax.empty((N_DEV, a.shape[0], r, a.shape[2]), a.dtype) for a, r in zip(arrs, rs)]

    def body(*refs):
        x_refs, land_refs = refs[:n], refs[n:2 * n]
        send_sems, recv_sems = refs[2 * n], refs[2 * n + 1]
        token = refs[-1]
        peers, my_idx = _peers()
        for k, (peer, p_idx) in enumerate(peers):
            for a in range(n):
                pltpu.make_async_remote_copy(
                    src_ref=x_refs[a].at[:, pl.ds(p_idx * rs[a], rs[a]), :], dst_ref=land_refs[a].at[my_idx],
                    send_sem=send_sems.at[a * 7 + k], recv_sem=recv_sems.at[a * 7 + k], device_id=peer,
                    device_id_type=MESH).start()
        token[...] = jnp.zeros_like(token)

    hbm = pl.BlockSpec(memory_space=pltpu.HBM)
    sem = pl.BlockSpec(memory_space=pltpu.SEMAPHORE)
    outs = pl.pallas_call(
        body, name=name,
        out_shape=(pltpu.SemaphoreType.DMA((n * 7,)), pltpu.SemaphoreType.DMA((n * 7,)))
        + tuple(pltpu.HBM(a.shape, a.dtype) for a in arrs) + tuple(pltpu.HBM(z.shape, z.dtype) for z in lands)
        + (jax.ShapeDtypeStruct((8, 128), F32),),
        in_specs=[hbm] * (2 * n),
        out_specs=(sem, sem) + (hbm,) * (2 * n) + (pl.BlockSpec(memory_space=pltpu.VMEM),),
        input_output_aliases={i: i + 2 for i in range(2 * n)},
        compiler_params=pltpu.CompilerParams(has_side_effects=pltpu.SideEffectType.DATAFLOW_SIDE_EFFECTING),
    )(*[pltpu.with_memory_space_constraint(a, pltpu.HBM) for a in arrs],
      *[pltpu.with_memory_space_constraint(z, pltpu.HBM) for z in lands])
    return outs[0], outs[1], list(outs[2:2 + n]), list(outs[2 + n:2 + 2 * n]), outs[-1]


def scatter_wait(send_sems, recv_sems, arrs, lands, after, name):
    n = len(arrs)
    rs = [a.shape[1] // N_DEV for a in arrs]

    def body(*refs):
        x_refs, land_refs = refs[:n], refs[n:2 * n]
        s_sems, r_sems = refs[2 * n], refs[2 * n + 1]
        peers, my_idx = _peers()
        for k, (peer, p_idx) in enumerate(peers):
            for a in range(n):
                copy = pltpu.make_async_remote_copy(
                    src_ref=x_refs[a].at[:, pl.ds(p_idx * rs[a], rs[a]), :], dst_ref=land_refs[a].at[p_idx],
                    send_sem=s_sems.at[a * 7 + k], recv_sem=r_sems.at[a * 7 + k], device_id=peer, device_id_type=MESH)
                copy.wait_send()
                copy.wait_recv()

    hbm = pl.BlockSpec(memory_space=pltpu.HBM)
    sem = pl.BlockSpec(memory_space=pltpu.SEMAPHORE)
    outs = pl.pallas_call(
        body, name=name,
        out_shape=tuple(pltpu.HBM(a.shape, a.dtype) for a in arrs) + tuple(pltpu.HBM(z.shape, z.dtype) for z in lands),
        in_specs=[hbm] * (2 * n) + [sem, sem, pl.BlockSpec(memory_space=pl.ANY)],
        out_specs=(hbm,) * (2 * n),
        input_output_aliases={i: i for i in range(2 * n)},
        compiler_params=pltpu.CompilerParams(has_side_effects=pltpu.SideEffectType.DATAFLOW_SIDE_EFFECTING),
    )(*arrs, *lands, send_sems, recv_sems, after)
    return list(outs[:n]), list(outs[n:])


def _row_tile(rows, cap):
    best = None
    for t in range(16, min(rows, cap) + 1, 16):
        if rows % t == 0:
            best = t
    return rows if best is None else best


def adamw(w, gparts, m, v, name):
    per_layer = isinstance(gparts, (list, tuple))
    glist = list(gparts) if per_layer else [gparts]
    n, _, ra, cb = glist[0].shape
    nl = w.shape[0]
    ng = len(glist)
    ta = _row_tile(ra, max(8, (1 << 19) // (cb * n)))

    def slot_sum(g_ref):
        g = g_ref[0].astype(F32)
        for p in range(1, n):
            g = g + g_ref[p].astype(F32)
        return g

    def body(*refs):
        w_ref, g_refs = refs[0], refs[1:1 + ng]
        m_ref, v_ref, go_ref, d_ref, mo_ref, vo_ref = refs[1 + ng:]
        g = slot_sum(g_refs[0])
        for layer in range(1, ng):
            g = jnp.where(pl.program_id(0) == layer, slot_sum(g_refs[layer]), g)
        mn = ADAM_B1 * m_ref[...] + (1.0 - ADAM_B1) * g
        vn = ADAM_B2 * v_ref[...] + (1.0 - ADAM_B2) * jnp.square(g)
        m_hat = mn / (1.0 - ADAM_B1 ** ADAM_STEP)
        v_hat = vn / (1.0 - ADAM_B2 ** ADAM_STEP)
        go_ref[...] = g
        d_ref[...] = -ADAM_LR * (m_hat / (jnp.sqrt(v_hat) + ADAM_EPS) + ADAM_WD * w_ref[...])
        mo_ref[...] = mn
        vo_ref[...] = vn

    blk = pl.BlockSpec((None, ta, cb), lambda l, i: (l, i, 0))
    if per_layer:
        gblk = pl.BlockSpec((n, None, ta, cb), lambda l, i: (0, 0, i, 0))
    else:
        gblk = pl.BlockSpec((n, None, ta, cb), lambda l, i: (0, l, i, 0))
    shp = jax.ShapeDtypeStruct((nl, ra, cb), F32)
    return pl.pallas_call(
        body, name=name, grid=(nl, ra // ta),
        in_specs=[blk] + [gblk] * ng + [blk, blk], out_specs=[blk] * 4, out_shape=[shp] * 4,
        compiler_params=_cp(("arbitrary", "arbitrary")),
    )(w, *glist, m, v)


def _sincos_2d(rows, cols, dim):
    quarter = dim // 4
    omega = 1.0 / (10000.0 ** (jnp.arange(quarter, dtype=F32) / quarter))
    r = jnp.arange(rows, dtype=F32)[:, None] * omega
    cc = jnp.arange(cols, dtype=F32)[:, None] * omega
    er = jnp.concatenate([jnp.sin(r), jnp.cos(r)], axis=-1)
    ec = jnp.concatenate([jnp.sin(cc), jnp.cos(cc)], axis=-1)
    pe = jnp.concatenate([jnp.broadcast_to(er[:, None, :], (rows, cols, dim // 2)),
                          jnp.broadcast_to(ec[None, :, :], (rows, cols, dim // 2))], axis=-1)
    return pe.reshape(rows * cols, dim)


def _pool_constants():
    nw = len(POOL_WINDOWS)
    band = np.zeros((2, nw, TT, TT), np.float32)
    icnt = np.zeros((2, TT, C_W), np.float32)
    for kind, n in ((0, TT), (1, GRID_W)):
        for i, w in enumerate(POOL_WINDOWS):
            for t in range(TT):
                base, tl = (t // n) * n, t % n
                lo = min(max(tl - w // 2, 0), n)
                hi = min(max(tl - w // 2 + w, 0), n)
                band[kind, i, t, base + lo:base + hi] = 1.0
                icnt[kind, t, i * (C_W // nw):(i + 1) * (C_W // nw)] = 1.0 / (hi - lo)
    return jnp.asarray(band, BF16), jnp.asarray(icnt, F32)


def _block_diag(blocks):
    n, a, _ = blocks.shape
    return jnp.einsum('gab,gh->gahb', blocks, jnp.eye(n, dtype=F32), precision=HI).reshape(n * a, n * a)


def _block_diag_parts(mat, n):
    a = mat.shape[0] // n
    m4 = mat.reshape(n, a, n, a)
    return jnp.stack([m4[g, :, g, :] for g in range(n)])


_SMALL = ("c_ctx", "b_mod", "norm_mix_pre", "norm_mix_post", "norm_ffn_pre", "norm_ffn_post", "sgu_w", "sgu_b",
          "ssm_lam_re", "ssm_lam_im", "ssm_log_dt", "ssm_b_re", "ssm_b_im", "ssm_c_re", "ssm_c_im", "ssm_d",
          "glu_b", "pool_w", "pool_scale")
_WEIGHTS = ("c_ctx", "w_mod", "b_mod", "norm_mix_pre", "norm_mix_post", "norm_ffn_pre", "norm_ffn_post", "w_in", "w_out",
            "sgu_w", "sgu_b", "ssm_lam_re", "ssm_lam_im", "ssm_log_dt", "ssm_b_re", "ssm_b_im", "ssm_c_re", "ssm_c_im",
            "ssm_d", "glu_w", "glu_b", "pool_w", "pool_scale", "ffn_w_gate", "ffn_w_up", "ffn_w_down")


def _pack_rows(a):
    flat = a.reshape(-1)
    rows = -(-flat.shape[0] // D)
    rows8 = -(-rows // 8) * 8
    return jnp.pad(flat, (0, rows8 * D - flat.shape[0])).reshape(rows8, D)


def _pack(tree):
    packed = jnp.concatenate([_pack_rows(tree[k]) for k in _SMALL], axis=0)
    return jnp.pad(packed, ((0, -packed.shape[0] % 64), (0, 0)))


def _unpack(packed, like):
    out, at = {}, 0
    for k in _SMALL:
        size = int(np.prod(like[k].shape))
        rows8 = -(-(-(-size // D)) // 8) * 8
        out[k] = packed[at:at + rows8].reshape(-1)[:size].reshape(like[k].shape)
        at += rows8
    return out


def kernel(x, c, ctx, c_ctx, w_mod, b_mod, norm_mix_pre, norm_mix_post, norm_ffn_pre, norm_ffn_post, w_in, w_out, sgu_w, sgu_b, ssm_lam_re, ssm_lam_im, ssm_log_dt, ssm_b_re, ssm_b_im, ssm_c_re, ssm_c_im, ssm_d, glu_w, glu_b, pool_w, pool_scale, ffn_w_gate, ffn_w_up, ffn_w_down, loss_target, m_c_ctx, m_w_mod, m_b_mod, m_norm_mix_pre, m_norm_mix_post, m_norm_ffn_pre, m_norm_ffn_post, m_w_in, m_w_out, m_sgu_w, m_sgu_b, m_ssm_lam_re, m_ssm_lam_im, m_ssm_log_dt, m_ssm_b_re, m_ssm_b_im, m_ssm_c_re, m_ssm_c_im, m_ssm_d, m_glu_w, m_glu_b, m_pool_w, m_pool_scale, m_ffn_w_gate, m_ffn_w_up, m_ffn_w_down, v_c_ctx, v_w_mod, v_b_mod, v_norm_mix_pre, v_norm_mix_post, v_norm_ffn_pre, v_norm_ffn_post, v_w_in, v_w_out, v_sgu_w, v_sgu_b, v_ssm_lam_re, v_ssm_lam_im, v_ssm_log_dt, v_ssm_b_re, v_ssm_b_im, v_ssm_c_re, v_ssm_c_im, v_ssm_d, v_glu_w, v_glu_b, v_pool_w, v_pool_scale, v_ffn_w_gate, v_ffn_w_up, v_ffn_w_down):
    wts = dict(c_ctx=c_ctx, w_mod=w_mod, b_mod=b_mod, norm_mix_pre=norm_mix_pre, norm_mix_post=norm_mix_post,
               norm_ffn_pre=norm_ffn_pre, norm_ffn_post=norm_ffn_post, w_in=w_in, w_out=w_out, sgu_w=sgu_w, sgu_b=sgu_b,
               ssm_lam_re=ssm_lam_re, ssm_lam_im=ssm_lam_im, ssm_log_dt=ssm_log_dt, ssm_b_re=ssm_b_re, ssm_b_im=ssm_b_im,
               ssm_c_re=ssm_c_re, ssm_c_im=ssm_c_im, ssm_d=ssm_d, glu_w=glu_w, glu_b=glu_b, pool_w=pool_w,
               pool_scale=pool_scale, ffn_w_gate=ffn_w_gate, ffn_w_up=ffn_w_up, ffn_w_down=ffn_w_down)
    mom_m = dict(c_ctx=m_c_ctx, w_mod=m_w_mod, b_mod=m_b_mod, norm_mix_pre=m_norm_mix_pre, norm_mix_post=m_norm_mix_post,
                 norm_ffn_pre=m_norm_ffn_pre, norm_ffn_post=m_norm_ffn_post, w_in=m_w_in, w_out=m_w_out, sgu_w=m_sgu_w,
                 sgu_b=m_sgu_b, ssm_lam_re=m_ssm_lam_re, ssm_lam_im=m_ssm_lam_im, ssm_log_dt=m_ssm_log_dt,
                 ssm_b_re=m_ssm_b_re, ssm_b_im=m_ssm_b_im, ssm_c_re=m_ssm_c_re, ssm_c_im=m_ssm_c_im, ssm_d=m_ssm_d,
                 glu_w=m_glu_w, glu_b=m_glu_b, pool_w=m_pool_w, pool_scale=m_pool_scale, ffn_w_gate=m_ffn_w_gate,
                 ffn_w_up=m_ffn_w_up, ffn_w_down=m_ffn_w_down)
    mom_v = dict(c_ctx=v_c_ctx, w_mod=v_w_mod, b_mod=v_b_mod, norm_mix_pre=v_norm_mix_pre, norm_mix_post=v_norm_mix_post,
                 norm_ffn_pre=v_norm_ffn_pre, norm_ffn_post=v_norm_ffn_post, w_in=v_w_in, w_out=v_w_out, sgu_w=v_sgu_w,
                 sgu_b=v_sgu_b, ssm_lam_re=v_ssm_lam_re, ssm_lam_im=v_ssm_lam_im, ssm_log_dt=v_ssm_log_dt,
                 ssm_b_re=v_ssm_b_re, ssm_b_im=v_ssm_b_im, ssm_c_re=v_ssm_c_re, ssm_c_im=v_ssm_c_im, ssm_d=v_ssm_d,
                 glu_w=v_glu_w, glu_b=v_glu_b, pool_w=v_pool_w, pool_scale=v_pool_scale, ffn_w_gate=v_ffn_w_gate,
                 ffn_w_up=v_ffn_w_up, ffn_w_down=v_ffn_w_down)

    bl, seq, _ = x.shape
    n_ctx = ctx.shape[1]
    assert n_ctx == TT and seq % TT == 0 and seq % GRID_W == 0
    depth = w_in.shape[0]
    nc = n_ctx // TT
    ncr = n_ctx // TC
    s_all = n_ctx + seq
    nt = s_all // TT
    t_all = bl * s_all
    n_batch = bl * N_DEV
    my_idx = 4 * lax.axis_index("x") + 2 * lax.axis_index("y") + lax.axis_index("c")
    wc = w_mod.shape[2]

    c_rows = jnp.pad(c, ((0, 8 - bl), (0, 0))) if bl < 8 else c
    rc = c_rows.shape[0]
    (c_all,) = all_gather_rows([c_rows[None]], "gather_c")
    c_all = c_all[0].reshape(N_DEV, rc, D)[:, :bl].reshape(n_batch, D)
    r_act = -(-(n_batch + 1) // 16) * 16
    pre_act = jnp.concatenate([c_all, c_ctx[None, :], jnp.zeros((r_act - n_batch - 1, D), F32)], axis=0)
    act = jax.nn.silu(pre_act)
    b_cols = lax.dynamic_slice_in_dim(b_mod, my_idx * wc, wc, axis=1)[:, None, :]
    mod_cols = mod_forward(act, w_mod, b_cols)
    (mod_all,) = all_gather_rows([mod_cols], "gather_mod")
    mod_all = mod_all.reshape(depth, N_DEV, r_act, wc).transpose(0, 2, 1, 3).reshape(depth, r_act, 6, D)
    mod_lat = lax.dynamic_slice_in_dim(mod_all, my_idx * bl, bl, axis=1)
    mod_ctx = jnp.broadcast_to(mod_all[:, n_batch:n_batch + 1], (depth, bl, 6, D))
    mods = jnp.pad(jnp.stack([mod_ctx, mod_lat], axis=2), ((0, 0), (0, 0), (0, 0), (0, 2), (0, 0)))

    tr = lambda a: jnp.swapaxes(a, 1, 2).astype(BF16)
    w_int, w_o, g_w, wg_t, wu_t, w_d = all_gather_rows(
        [tr(w_in), w_out.astype(BF16), glu_w.astype(BF16), tr(ffn_w_gate), tr(ffn_w_up), ffn_w_down.astype(BF16)],
        "gather_weights")

    band, icnt = _pool_constants()
    seg_p = jnp.asarray(np.kron(np.eye(A_HEADS), np.full((A_W // A_HEADS,) * 2, A_HEADS / A_W)), BF16)
    pe = _sincos_2d(seq // GRID_W, GRID_W, D)
    xs = embed_tokens(x, ctx, pe)

    saved = []
    for i in range(depth):
        mats, ssm_vjp = jax.vjp(ssm_build, ssm_lam_re[i], ssm_lam_im[i], ssm_log_dt[i], ssm_b_re[i], ssm_b_im[i],
                                ssm_c_re[i], ssm_c_im[i], ssm_d[i])
        cst = dict(sw=sgu_w[i].astype(BF16),
                   sbias=jnp.repeat(sgu_b[i].T, A_W // A_HEADS, axis=1),
                   seg_p=seg_p, band=band, icnt=icnt, wbd=_block_diag(pool_w[i]).astype(BF16),
                   pscale=pool_scale[i][None, :], glu_w=g_w[i], glu_b=glu_b[i][None, :], w_out=w_o[i],
                   n2=norm_mix_post[i][None, :])
        n1, n3, n4 = norm_mix_pre[i][None, :], norm_ffn_pre[i][None, :], norm_ffn_post[i][None, :]
        za, zu, zp = pre_mix(xs, mods[i], n1, w_int[i], nc)
        ops = ssm_operators(mats, f"_{i}")
        ys, hps = ssm_forward(zu, mats, ops, ncr)
        x1, m_pre = post_mix(xs, za, zp, ys, mods[i], cst, nc)
        x2, f_pre = ffn_fwd(x1, mods[i], n3, n4, wg_t[i], wu_t[i], w_d[i], n_ctx)
        saved.append(dict(xs=xs, za=za, zu=zu, zp=zp, ys=ys, hps=hps, x1=x1, m=m_pre, f=f_pre, cst=cst, mats=mats,
                          ops=ops, ssm_vjp=ssm_vjp, n1=n1, n3=n3, n4=n4))
        xs = x2

    dx, loss_parts = loss_head(xs, loss_target, nc)
    loss = lax.psum(jnp.sum(loss_parts[:, :, 0, 0]), ("x", "y", "c"))

    grads = {k: [None] * depth for k in _WEIGHTS}
    big = {k: [None] * depth for k in ("w_in", "w_out", "glu_w", "ffn_w_gate", "ffn_w_up", "ffn_w_down")}
    dmods = [None] * depth
    scatter_order = ("w_in", "w_out", "glu_w", "ffn_w_gate", "ffn_w_up", "ffn_w_down")
    in_flight = [None] * depth
    flat = lambda a: a.reshape(t_all, a.shape[-1])
    for i in reversed(range(depth)):
        sv = saved[i]
        dx1, h2, df, act_b, dgate, dup, st_f = ffn_bwd(dx, sv["x1"], sv["f"], mods[i], sv["n3"], sv["n4"],
                                                       wg_t[i], wu_t[i], w_d[i], n_ctx)
        big["ffn_w_gate"][i] = tn_matmul(flat(dgate), flat(h2), f"grad_ffn_gate_{i}")
        big["ffn_w_up"][i] = tn_matmul(flat(dup), flat(h2), f"grad_ffn_up_{i}")
        big["ffn_w_down"][i] = tn_matmul(flat(act_b), flat(df), f"grad_ffn_down_{i}")
        dza, dzp, dys, cat, dm, gg, dr, st_m, dsw, dsb, dwbd = post_mix_bwd(dx1, sv["m"], sv["za"], sv["zp"], sv["ys"],
                                                                            mods[i], sv["cst"], nc)
        big["w_out"][i] = tn_matmul(flat(cat), flat(dm), f"grad_w_out_{i}")
        big["glu_w"][i] = tn_matmul(flat(gg), flat(dr), f"grad_glu_w_{i}")
        dzu, cot = ssm_backward(dys, sv["zu"], sv["hps"], sv["mats"], sv["ops"], ncr)
        (grads["ssm_lam_re"][i], grads["ssm_lam_im"][i], grads["ssm_log_dt"][i], grads["ssm_b_re"][i],
         grads["ssm_b_im"][i], grads["ssm_c_re"][i], grads["ssm_c_im"][i], grads["ssm_d"][i]) = sv["ssm_vjp"](cot)
        dx, h1, dz, st_p = pre_mix_bwd(dza, dzu, dzp, sv["xs"], dx1, mods[i], sv["n1"], w_int[i], nc)
        big["w_in"][i] = tn_matmul(flat(dz), flat(h1), f"grad_w_in_{i}")

        tiles = lambda st, row: st[:, :, row, :]
        allsum = lambda st, row: jnp.sum(tiles(st, row), axis=(0, 1))
        grads["norm_mix_pre"][i] = allsum(st_p, 2)
        grads["norm_mix_post"][i] = allsum(st_m, 1)
        grads["norm_ffn_pre"][i] = allsum(st_f, 3)
        grads["norm_ffn_post"][i] = allsum(st_f, 4)
        misc = allsum(st_m, 2)
        grads["glu_b"][i] = misc[:B_W]
        grads["pool_scale"][i] = misc[B_W:B_W + C_W]
        grads["sgu_w"][i] = dsw
        grads["sgu_b"][i] = jnp.sum(dsb.reshape(CHUNK, A_HEADS, A_W // A_HEADS), axis=2).T
        grads["pool_w"][i] = _block_diag_parts(dwbd, len(POOL_WINDOWS))
        mix = (tiles(st_p, 0), tiles(st_p, 1), tiles(st_m, 0))
        d_lat = jnp.stack([jnp.sum(t[:, nc:], axis=1) for t in mix]
                          + [jnp.sum(tiles(st_f, r), axis=1) for r in (0, 1, 2)], axis=1).reshape(bl, 6 * D)
        d_ctx = jnp.concatenate([jnp.sum(t[:, :nc], axis=(0, 1)) for t in mix]
                                + [allsum(st_f, r) for r in (5, 6, 7)]).reshape(1, 6 * D)
        dmods[i] = jnp.concatenate([d_lat, d_ctx, jnp.zeros((8 - (bl + 1) % 8 if (bl + 1) % 8 else 0, 6 * D), F32)],
                                   axis=0)
        in_flight[i] = scatter_start([big[k][i][None] for k in scatter_order], f"scatter_start_{i}")
        token = in_flight[i][4][0:1, 0:1]
        if i > 0:
            saved[i - 1]["n3"] = saved[i - 1]["n3"] + token
        else:
            dmods[i] = dmods[i] + token
    grad_x = dx[:, n_ctx:, :]

    dmod_local = jnp.stack(dmods)
    rd = dmod_local.shape[1]
    (dmod_all,) = all_gather_rows([dmod_local], "gather_dmod")
    dmod_cols = lax.dynamic_slice_in_dim(dmod_all, my_idx * wc, wc, axis=2).reshape(depth, N_DEV, rd, wc)
    d_lat_all = dmod_cols[:, :, :bl].reshape(depth, n_batch, wc)
    d_ctx_all = dmod_cols[:, 0, bl]
    for p in range(1, N_DEV):
        d_ctx_all = d_ctx_all + dmod_cols[:, p, bl]
    dmod_rows = jnp.concatenate([d_lat_all, d_ctx_all[:, None, :], jnp.zeros((depth, r_act - n_batch - 1, wc), F32)],
                                axis=1)
    dctx_rows = jnp.pad(d_ctx_all[:, None, :], ((0, 0), (0, 7), (0, 0)))
    g_w_mod, dact_ctx = mod_backward(act, dmod_rows, dctx_rows, w_mod)
    sig_c = jax.nn.sigmoid(c_ctx)
    dsilu_c = sig_c * (1.0 + c_ctx * (1.0 - sig_c))
    small_g = {k: (jnp.stack(grads[k]) if grads[k][0] is not None else None) for k in _SMALL}
    small_g["c_ctx"] = jnp.sum(dact_ctx[:, 0, :], axis=0) * dsilu_c
    small_g["b_mod"] = jnp.stack([jnp.sum(dmods[i][:bl + 1], axis=0) for i in range(depth)])

    packed_g = _pack(small_g).astype(BF16)
    rows_s = packed_g.shape[0]
    (gathered,) = all_gather_rows([packed_g[None]], "gather_small_grads")
    gparts = gathered.reshape(N_DEV, 1, rows_s, D)
    small_w = {k: wts[k] for k in _SMALL}
    outs = adamw(_pack(small_w)[None], gparts, _pack({k: mom_m[k] for k in _SMALL})[None],
                 _pack({k: mom_v[k] for k in _SMALL})[None], "adamw_replicated")
    res = {k: [None] * 4 for k in _WEIGHTS}
    for slot, packed in enumerate(outs):
        un = _unpack(packed[0], small_w)
        for k in _SMALL:
            res[k][slot] = un[k]

    landed = []
    for i in range(depth):
        send_sems, recv_sems, arrs_thru, lands_thru, _ = in_flight[i]
        sent, lands = scatter_wait(send_sems, recv_sems, arrs_thru, lands_thru, outs[0], f"scatter_wait_{i}")
        own = [lax.dynamic_slice_in_dim(a, my_idx * (a.shape[1] // N_DEV), a.shape[1] // N_DEV, axis=1)[None]
               for a in sent]
        landed.append([lax.dynamic_update_slice_in_dim(z, o, my_idx, axis=0) for z, o in zip(lands, own)])
    for pos, k in enumerate(scatter_order):
        transposed = k in ("w_in", "ffn_w_gate", "ffn_w_up")
        view = (lambda a: jnp.swapaxes(a, 1, 2)) if transposed else (lambda a: a)
        o4 = adamw(view(wts[k]), [landed[i][pos] for i in range(depth)], view(mom_m[k]), view(mom_v[k]), "adamw_" + k)
        res[k] = [view(o) for o in o4]
    res["w_mod"] = list(adamw(w_mod, g_w_mod[None], m_w_mod, v_w_mod, "adamw_w_mod"))

    return (loss, grad_x, *[res[k][0] for k in _WEIGHTS], *[res[k][1] for k in _WEIGHTS],
            *[res[k][2] for k in _WEIGHTS], *[res[k][3] for k in _WEIGHTS])
```

```python
import functools
import math

import numpy as np
import jax
import jax.numpy as jnp
from jax import lax
from jax.experimental import pallas as pl
from jax.experimental.pallas import tpu as pltpu

F32 = jnp.float32
BF16 = jnp.bfloat16
HI = lax.Precision.HIGHEST
MESH = pl.DeviceIdType.MESH

D = 1024
D_IN = 1280
D_FF = 2816
A_W = 256
B_W = 512
C_W = 256
A_HEADS = 4
CHUNK = 128
SSM_G = 32
SSM_H = 16
SSM_P = 64
GRID_W = 64
POOL_WINDOWS = (2, 4, 8, 16)
EPS = 1e-6
N_DEV = 8

TT = 256
TC = 8
ROW_W = TC * B_W
QW = ROW_W // 4
GQ = 8
FF_CHUNK = 256
VMEM_LIMIT = 60 * 1024 * 1024

ADAM_LR = 0.001
ADAM_B1 = 0.9
ADAM_B2 = 0.999
ADAM_EPS = 1e-08
ADAM_WD = 0.01
ADAM_STEP = 10


def _cp(sem):
    return pltpu.CompilerParams(dimension_semantics=sem, vmem_limit_bytes=VMEM_LIMIT)


def dot_nn(a, b):
    return jnp.dot(a, b, preferred_element_type=F32)


def dot_nt(a, b):
    return lax.dot_general(a, b, (((1,), (1,)), ((), ())), preferred_element_type=F32)


def dot_tn(a, b):
    return lax.dot_general(a, b, (((0,), (0,)), ((), ())), preferred_element_type=F32)


def split_bf16(x):
    hi = x.astype(BF16)
    lo = (x - hi.astype(F32)).astype(BF16)
    return hi, lo


def gelu(x):
    return jax.nn.gelu(x)


def gelu_grad(x):
    c = math.sqrt(2.0 / math.pi)
    t = jnp.tanh(c * (x + 0.044715 * x * x * x))
    return 0.5 * (1.0 + t) + 0.5 * x * (1.0 - t * t) * c * (1.0 + 3.0 * 0.044715 * x * x)


def rms_stats(x):
    r = lax.rsqrt(jnp.mean(x * x, axis=-1, keepdims=True) + EPS)
    return r, x * r


def rms_bwd(r, xn, dxn):
    return r * (dxn - xn * jnp.mean(dxn * xn, axis=-1, keepdims=True))


def colsum(x):
    return jnp.sum(x, axis=0, keepdims=True)


def lane_group(width, group):
    return lax.broadcasted_iota(jnp.int32, (1, width), 1) // group


def _tile_spec(width):
    return pl.BlockSpec((None, TT, width), lambda b, j: (b, j, 0))


def _mod_spec(nc):
    return pl.BlockSpec((None, None, 8, D), lambda b, j: (b, jnp.where(j >= nc, 1, 0), 0, 0))


def _full_spec(shape):
    zeros = (0,) * len(shape)
    return pl.BlockSpec(shape, lambda b, j: zeros)


def _kind_spec(shape, nc):
    zeros = (0,) * len(shape)
    return pl.BlockSpec((None,) + shape, lambda b, j: (jnp.where(j >= nc, 1, 0),) + zeros)


def _stat_spec():
    return pl.BlockSpec((None, None, 8, D), lambda b, j: (b, j, 0, 0))


def _chunk_spec():
    return pl.BlockSpec((None, TT // TC, ROW_W), lambda b, j: (b, j, 0))


def _rows_to_chunks(val, scratch, out_ref):
    for cb in range(B_W // 128):
        scratch[cb] = val[:, cb * 128:(cb + 1) * 128]
    for s in range(TC):
        for cb in range(B_W // 128):
            lo = cb * QW + s * 128
            out_ref[:, lo:lo + 128] = scratch.at[cb][pl.ds(s, TT // TC, stride=TC), :]


def _chunks_to_rows(in_ref, scratch):
    for s in range(TC):
        for cb in range(B_W // 128):
            lo = cb * QW + s * 128
            scratch.at[cb][pl.ds(s, TT // TC, stride=TC), :] = in_ref[:, lo:lo + 128]
    return jnp.concatenate([scratch[cb] for cb in range(B_W // 128)], axis=1)


def _chunk_scratch():
    return pltpu.VMEM((B_W // 128, TT, 128), F32)


def embed_tokens(x, ctx, pe):
    bl, seq, _ = x.shape
    nc = ctx.shape[1] // TT
    nt = nc + seq // TT

    def body(ctx_ref, x_ref, pe_ref, o_ref):
        j = pl.program_id(1)

        @pl.when(j < nc)
        def _():
            o_ref[...] = ctx_ref[...]

        @pl.when(j >= nc)
        def _():
            o_ref[...] = x_ref[...] + pe_ref[...]

    return pl.pallas_call(
        body, name="embed_tokens", grid=(bl, nt),
        in_specs=[pl.BlockSpec((None, TT, D), lambda b, j: (b, jnp.minimum(j, nc - 1), 0)),
                  pl.BlockSpec((None, TT, D), lambda b, j: (b, jnp.maximum(j - nc, 0), 0)),
                  pl.BlockSpec((TT, D), lambda b, j: (jnp.maximum(j - nc, 0), 0))],
        out_specs=_tile_spec(D),
        out_shape=jax.ShapeDtypeStruct((bl, nt * TT, D), F32),
        compiler_params=_cp(("arbitrary", "arbitrary")),
    )(ctx, x, pe)


def pre_mix(xs, mod, n1, w_int, nc):
    bl, s, _ = xs.shape

    def body(x_ref, mod_ref, n_ref, w_ref, za_ref, zu_ref, zp_ref, u_s):
        r, xn = rms_stats(x_ref[...])
        h = xn * n_ref[...] * (1.0 + mod_ref[1:2, :]) + mod_ref[0:1, :]
        z = dot_nt(h.astype(BF16), w_ref[...])
        za_ref[...] = z[:, :2 * A_W]
        _rows_to_chunks(z[:, 2 * A_W:2 * A_W + B_W], u_s, zu_ref)
        zp_ref[...] = z[:, 2 * A_W + B_W:]

    return pl.pallas_call(
        body, name="pre_mix", grid=(bl, s // TT),
        in_specs=[_tile_spec(D), _mod_spec(nc), _full_spec((1, D)), _full_spec((D_IN, D))],
        out_specs=[_tile_spec(2 * A_W), _chunk_spec(), _tile_spec(C_W)],
        out_shape=[jax.ShapeDtypeStruct((bl, s, 2 * A_W), F32), jax.ShapeDtypeStruct((bl, s // TC, ROW_W), F32),
                   jax.ShapeDtypeStruct((bl, s, C_W), F32)],
        scratch_shapes=[_chunk_scratch()],
        compiler_params=_cp(("arbitrary", "arbitrary")),
    )(xs, mod, n1, w_int)


def _seg_mean(x, seg_p):
    hi, lo = split_bf16(x)
    return dot_nn(hi, seg_p) + dot_nn(lo, seg_p)


def _sgu_forward(za, sw_ref, sbias, seg_p):
    ge = gelu(za)
    u, v = ge[:, :A_W], ge[:, A_W:]
    dv = v - _seg_mean(v, seg_p)
    rs = lax.rsqrt(_seg_mean(dv * dv, seg_p) + EPS)
    vn = dv * rs
    head = lane_group(A_W, A_W // A_HEADS)
    parts = []
    for c2 in range(TT // CHUNK):
        vb = vn[c2 * CHUNK:(c2 + 1) * CHUNK].astype(BF16)
        sc = sbias
        for h in range(A_HEADS):
            sc = sc + jnp.where(head == h, dot_nn(sw_ref[h], vb), 0.0)
        parts.append(sc)
    sg = jnp.concatenate(parts, axis=0)
    return u * sg, (u, vn, rs, sg)


def _pool_forward(zp, band_ref, icnt, wbd, pscale):
    hi, lo = split_bf16(zp)
    grp = lane_group(C_W, C_W // len(POOL_WINDOWS))
    q = jnp.zeros_like(zp)
    for i in range(len(POOL_WINDOWS)):
        t = dot_nn(band_ref[i], hi) + dot_nn(band_ref[i], lo)
        q = jnp.where(grp == i, t, q)
    q = q * icnt - zp
    o = dot_nn(q.astype(BF16), wbd)
    return o * pscale, (q, o)


def _glu_forward(y, glu_w, glu_b):
    g = gelu(y)
    sg = jax.nn.sigmoid(dot_nn(g.astype(BF16), glu_w) + glu_b)
    return g * sg, (g, sg)


_MIX_CONST_SHAPES = dict(sw=(A_HEADS, CHUNK, CHUNK), sbias=(CHUNK, A_W), seg_p=(A_W, A_W), wbd=(C_W, C_W),
                         pscale=(1, C_W), glu_w=(B_W, B_W), glu_b=(1, B_W), w_out=(D, D), n2=(1, D))


def _mix_const_specs(nc):
    return ([_full_spec(_MIX_CONST_SHAPES[k]) for k in ("sw", "sbias", "seg_p")]
            + [_kind_spec((len(POOL_WINDOWS), TT, TT), nc), _kind_spec((TT, C_W), nc)]
            + [_full_spec(_MIX_CONST_SHAPES[k]) for k in ("wbd", "pscale", "glu_w", "glu_b", "w_out", "n2")])


def _mix_const_args(cst):
    return [cst[k] for k in ("sw", "sbias", "seg_p", "band", "icnt", "wbd", "pscale", "glu_w", "glu_b", "w_out", "n2")]


def post_mix(xs, za, zp, ys, mod, cst, nc):
    bl, s, _ = xs.shape

    def body(x_ref, za_ref, zp_ref, y_ref, mod_ref, sw_ref, sbias_ref, seg_ref, band_ref, icnt_ref, wbd_ref,
             ps_ref, gw_ref, gb_ref, wo_ref, n2_ref, x1_ref, m_ref, y_s):
        a, _ = _sgu_forward(za_ref[...], sw_ref, sbias_ref[...], seg_ref[...])
        p, _ = _pool_forward(zp_ref[...], band_ref, icnt_ref[...], wbd_ref[...], ps_ref[...])
        sm, _ = _glu_forward(_chunks_to_rows(y_ref, y_s), gw_ref[...], gb_ref[...])
        cat = jnp.concatenate([a, sm, p], axis=1).astype(BF16)
        m = dot_nn(cat, wo_ref[...])
        _, mn = rms_stats(m)
        m_ref[...] = m
        x1_ref[...] = x_ref[...] + mod_ref[2:3, :] * (mn * n2_ref[...])

    return pl.pallas_call(
        body, name="post_mix", grid=(bl, s // TT),
        in_specs=[_tile_spec(D), _tile_spec(2 * A_W), _tile_spec(C_W), _chunk_spec(), _mod_spec(nc)]
        + _mix_const_specs(nc),
        out_specs=[_tile_spec(D), _tile_spec(D)],
        out_shape=[jax.ShapeDtypeStruct((bl, s, D), F32), jax.ShapeDtypeStruct((bl, s, D), F32)],
        scratch_shapes=[_chunk_scratch()],
        compiler_params=_cp(("arbitrary", "arbitrary")),
    )(xs, za, zp, ys, mod, *_mix_const_args(cst))


def post_mix_bwd(dx1, m, za, zp, ys, mod, cst, nc):
    bl, s, _ = m.shape
    nt = s // TT

    def body(dx_ref, m_ref, za_ref, zp_ref, y_ref, mod_ref, sw_ref, sbias_ref, seg_ref, band_ref, icnt_ref,
             wbd_ref, ps_ref, gw_ref, gb_ref, wo_ref, n2_ref,
             dza_ref, dzp_ref, dy_ref, cat_ref, dm_ref, gg_ref, dr_ref, st_ref, dsw_ref, dsb_ref, dwbd_ref, y_s):
        first = jnp.logical_and(pl.program_id(0) == 0, pl.program_id(1) == 0)

        @pl.when(first)
        def _():
            dsw_ref[...] = jnp.zeros_like(dsw_ref)
            dsb_ref[...] = jnp.zeros_like(dsb_ref)
            dwbd_ref[...] = jnp.zeros_like(dwbd_ref)

        seg_p = seg_ref[...]
        za = za_ref[...]
        zp_v = zp_ref[...]
        yv = _chunks_to_rows(y_ref, y_s)
        a, (u, vn, rs, sg) = _sgu_forward(za, sw_ref, sbias_ref[...], seg_p)
        p, (q, o) = _pool_forward(zp_v, band_ref, icnt_ref[...], wbd_ref[...], ps_ref[...])
        sm, (g, sig) = _glu_forward(yv, gw_ref[...], gb_ref[...])
        cat_ref[...] = jnp.concatenate([a, sm, p], axis=1).astype(BF16)

        dx = dx_ref[...]
        g1 = mod_ref[2:3, :]
        n2 = n2_ref[...]
        mv = m_ref[...]
        rm, mn = rms_stats(mv)
        st_ref[...] = jnp.zeros_like(st_ref)
        st_ref[0:1, :] = colsum(dx * (mn * n2))
        st_ref[1:2, :] = colsum(dx * g1 * mn)
        dm = rms_bwd(rm, mn, dx * g1 * n2)
        dmb = dm.astype(BF16)
        dm_ref[...] = dmb
        dcat = dot_nt(dmb, wo_ref[...])
        da, dsm, dp = dcat[:, :A_W], dcat[:, A_W:A_W + B_W], dcat[:, A_W + B_W:]

        du = da * sg
        dsv = da * u
        head = lane_group(A_W, A_W // A_HEADS)
        dvn_parts = []
        dsb_acc = jnp.zeros((CHUNK, A_W), F32)
        for c2 in range(TT // CHUNK):
            dsc = dsv[c2 * CHUNK:(c2 + 1) * CHUNK]
            dsc_b = dsc.astype(BF16)
            vb = vn[c2 * CHUNK:(c2 + 1) * CHUNK].astype(BF16)
            dsb_acc = dsb_acc + dsc
            dvn_c = jnp.zeros((CHUNK, A_W), F32)
            for h in range(A_HEADS):
                dsw_ref[h] += dot_nt(jnp.where(head == h, dsc, 0.0).astype(BF16), vb)
                dvn_c = dvn_c + jnp.where(head == h, dot_tn(sw_ref[h], dsc_b), 0.0)
            dvn_parts.append(dvn_c)
        dsb_ref[...] += dsb_acc
        dvn = jnp.concatenate(dvn_parts, axis=0)
        dv = rs * (dvn - _seg_mean(dvn, seg_p) - vn * _seg_mean(dvn * vn, seg_p))
        dza_ref[...] = jnp.concatenate([du, dv], axis=1) * gelu_grad(za)

        ps = ps_ref[...]
        do = dp * ps
        dps = colsum(dp * o)
        dob = do.astype(BF16)
        dwbd_ref[...] += dot_tn(q.astype(BF16), dob)
        dq = dot_nt(dob, wbd_ref[...])
        hi, lo = split_bf16(dq * icnt_ref[...])
        grp = lane_group(C_W, C_W // len(POOL_WINDOWS))
        dzp = -dq
        for i in range(len(POOL_WINDOWS)):
            t = dot_tn(band_ref[i], hi) + dot_tn(band_ref[i], lo)
            dzp = dzp + jnp.where(grp == i, t, 0.0)
        dzp_ref[...] = dzp

        dr = dsm * g * sig * (1.0 - sig)
        drb = dr.astype(BF16)
        dr_ref[...] = drb
        gg_ref[...] = g.astype(BF16)
        dg = dsm * sig + dot_nt(drb, gw_ref[...])
        _rows_to_chunks(dg * gelu_grad(yv), y_s, dy_ref)
        st_ref[2:3, :] = jnp.concatenate([colsum(dr), dps, jnp.zeros((1, D - B_W - C_W), F32)], axis=1)

    acc = lambda shape: pl.BlockSpec(shape, lambda b, j: (0,) * len(shape))
    return pl.pallas_call(
        body, name="post_mix_bwd", grid=(bl, nt),
        in_specs=[_tile_spec(D), _tile_spec(D), _tile_spec(2 * A_W), _tile_spec(C_W), _chunk_spec(), _mod_spec(nc)]
        + _mix_const_specs(nc),
        out_specs=[_tile_spec(2 * A_W), _tile_spec(C_W), _chunk_spec(), _tile_spec(D), _tile_spec(D),
                   _tile_spec(B_W), _tile_spec(B_W), _stat_spec(),
                   acc((A_HEADS, CHUNK, CHUNK)), acc((CHUNK, A_W)), acc((C_W, C_W))],
        out_shape=[jax.ShapeDtypeStruct((bl, s, 2 * A_W), F32), jax.ShapeDtypeStruct((bl, s, C_W), F32),
                   jax.ShapeDtypeStruct((bl, s // TC, ROW_W), F32), jax.ShapeDtypeStruct((bl, s, D), BF16),
                   jax.ShapeDtypeStruct((bl, s, D), BF16), jax.ShapeDtypeStruct((bl, s, B_W), BF16),
                   jax.ShapeDtypeStruct((bl, s, B_W), BF16), jax.ShapeDtypeStruct((bl, nt, 8, D), F32),
                   jax.ShapeDtypeStruct((A_HEADS, CHUNK, CHUNK), F32), jax.ShapeDtypeStruct((CHUNK, A_W), F32),
                   jax.ShapeDtypeStruct((C_W, C_W), F32)],
        scratch_shapes=[_chunk_scratch()],
        compiler_params=_cp(("arbitrary", "arbitrary")),
    )(dx1, m, za, zp, ys, mod, *_mix_const_args(cst))


def pre_mix_bwd(dza, dzu, dzp, xs, dxres, mod, n1, w_int, nc):
    bl, s, _ = xs.shape
    nt = s // TT

    def body(dza_ref, dzu_ref, dzp_ref, x_ref, dres_ref, mod_ref, n_ref, w_ref, dx_ref, h_ref, dz_ref, st_ref, u_s):
        dz = jnp.concatenate([dza_ref[...], _chunks_to_rows(dzu_ref, u_s), dzp_ref[...]], axis=1).astype(BF16)
        dz_ref[...] = dz
        dh = dot_nn(dz, w_ref[...])
        r, xn = rms_stats(x_ref[...])
        n1v = n_ref[...]
        sc = mod_ref[1:2, :]
        xg = xn * n1v
        h_ref[...] = (xg * (1.0 + sc) + mod_ref[0:1, :]).astype(BF16)
        dyv = dh * (1.0 + sc)
        st_ref[...] = jnp.zeros_like(st_ref)
        st_ref[0:1, :] = colsum(dh)
        st_ref[1:2, :] = colsum(dh * xg)
        st_ref[2:3, :] = colsum(dyv * xn)
        dx_ref[...] = dres_ref[...] + rms_bwd(r, xn, dyv * n1v)

    return pl.pallas_call(
        body, name="pre_mix_bwd", grid=(bl, nt),
        in_specs=[_tile_spec(2 * A_W), _chunk_spec(), _tile_spec(C_W), _tile_spec(D), _tile_spec(D), _mod_spec(nc),
                  _full_spec((1, D)), _full_spec((D_IN, D))],
        out_specs=[_tile_spec(D), _tile_spec(D), _tile_spec(D_IN), _stat_spec()],
        out_shape=[jax.ShapeDtypeStruct((bl, s, D), F32), jax.ShapeDtypeStruct((bl, s, D), BF16),
                   jax.ShapeDtypeStruct((bl, s, D_IN), BF16), jax.ShapeDtypeStruct((bl, nt, 8, D), F32)],
        scratch_shapes=[_chunk_scratch()],
        compiler_params=_cp(("arbitrary", "arbitrary")),
    )(dza, dzu, dzp, xs, dxres, mod, n1, w_int)


def _ffn_tile(s):
    return 768 if s % 768 == 0 else TT


def _ctx_rows(tf, n_ctx, j):
    return lax.broadcasted_iota(jnp.int32, (tf, 1), 0) + j * tf < n_ctx


def _mod_row(mod_ref, is_ctx, row):
    return jnp.where(is_ctx, mod_ref[0, row:row + 1, :], mod_ref[1, row:row + 1, :])


def ffn_fwd(x1, mod, n3, n4, wg_t, wu_t, wd, n_ctx):
    bl, s, _ = x1.shape
    tf = _ffn_tile(s)
    nk = D_FF // FF_CHUNK
    tile = pl.BlockSpec((None, tf, D), lambda b, j, k: (b, j, 0))
    modspec = pl.BlockSpec((None, 2, 8, D), lambda b, j, k: (b, 0, 0, 0))
    vec = pl.BlockSpec((1, D), lambda b, j, k: (0, 0))
    wspec = pl.BlockSpec((FF_CHUNK, D), lambda b, j, k: (k, 0))

    def body(x_ref, mod_ref, n3_ref, n4_ref, wg_ref, wu_ref, wd_ref, x2_ref, f_ref, h_s, acc_s):
        j, k = pl.program_id(1), pl.program_id(2)

        @pl.when(k == 0)
        def _():
            is_ctx = _ctx_rows(tf, n_ctx, j)
            _, xn = rms_stats(x_ref[...])
            h_s[...] = (xn * n3_ref[...] * (1.0 + _mod_row(mod_ref, is_ctx, 4)) + _mod_row(mod_ref, is_ctx, 3)).astype(BF16)
            acc_s[...] = jnp.zeros_like(acc_s)

        h = h_s[...]
        gate = dot_nt(h, wg_ref[...])
        up = dot_nt(h, wu_ref[...])
        act = (gate * jax.nn.sigmoid(gate)) * up
        acc_s[...] += dot_nn(act.astype(BF16), wd_ref[...])

        @pl.when(k == nk - 1)
        def _():
            f = acc_s[...]
            f_ref[...] = f
            _, fn = rms_stats(f)
            x2_ref[...] = x_ref[...] + _mod_row(mod_ref, _ctx_rows(tf, n_ctx, j), 5) * (fn * n4_ref[...])

    return pl.pallas_call(
        body, name="ffn_fwd", grid=(bl, s // tf, nk),
        in_specs=[tile, modspec, vec, vec, wspec, wspec, wspec],
        out_specs=[tile, tile],
        out_shape=[jax.ShapeDtypeStruct((bl, s, D), F32), jax.ShapeDtypeStruct((bl, s, D), F32)],
        scratch_shapes=[pltpu.VMEM((tf, D), BF16), pltpu.VMEM((tf, D), F32)],
        compiler_params=_cp(("arbitrary", "arbitrary", "arbitrary")),
    )(x1, mod, n3, n4, wg_t, wu_t, wd)


def ffn_bwd(dx2, x1, f, mod, n3, n4, wg_t, wu_t, wd, n_ctx):
    bl, s, _ = x1.shape
    tf = _ffn_tile(s)
    nt = s // tf
    nk = D_FF // FF_CHUNK
    tile = pl.BlockSpec((None, tf, D), lambda b, j, k: (b, j, 0))
    ftile = pl.BlockSpec((None, tf, FF_CHUNK), lambda b, j, k: (b, j, jnp.minimum(k, nk - 1)))
    modspec = pl.BlockSpec((None, 2, 8, D), lambda b, j, k: (b, 0, 0, 0))
    vec = pl.BlockSpec((1, D), lambda b, j, k: (0, 0))
    wspec = pl.BlockSpec((FF_CHUNK, D), lambda b, j, k: (jnp.minimum(k, nk - 1), 0))
    wprev = pl.BlockSpec((FF_CHUNK, D), lambda b, j, k: (jnp.maximum(k - 1, 0), 0))
    stat = pl.BlockSpec((None, None, 8, D), lambda b, j, k: (b, j, 0, 0))

    def split_sum(is_ctx, v, st_ref, row):
        st_ref[row:row + 1, :] = colsum(jnp.where(is_ctx, 0.0, v))
        st_ref[row + 5:row + 6, :] = colsum(jnp.where(is_ctx, v, 0.0))

    def body(dx_ref, x_ref, f_ref, mod_ref, n3_ref, n4_ref, wg_ref, wu_ref, wd_ref, wgp_ref, wup_ref,
             dx1_ref, h_ref, df_ref, act_ref, dgate_ref, dup_ref, st_ref, h_s, df_s, acc_s, dgate_s, dup_s):
        j, k = pl.program_id(1), pl.program_id(2)

        @pl.when(k == 0)
        def _():
            is_ctx = _ctx_rows(tf, n_ctx, j)
            dx = dx_ref[...]
            g2 = _mod_row(mod_ref, is_ctx, 5)
            n4 = n4_ref[...]
            rf, fn = rms_stats(f_ref[...])
            st_ref[...] = jnp.zeros_like(st_ref)
            split_sum(is_ctx, dx * (fn * n4), st_ref, 2)
            st_ref[4:5, :] = colsum(dx * g2 * fn)
            df = rms_bwd(rf, fn, dx * g2 * n4).astype(BF16)
            df_s[...] = df
            df_ref[...] = df
            _, xn = rms_stats(x_ref[...])
            h = (xn * n3_ref[...] * (1.0 + _mod_row(mod_ref, is_ctx, 4)) + _mod_row(mod_ref, is_ctx, 3)).astype(BF16)
            h_s[...] = h
            h_ref[...] = h
            acc_s[...] = jnp.zeros_like(acc_s)
            dgate_s[1] = jnp.zeros((tf, FF_CHUNK), BF16)
            dup_s[1] = jnp.zeros((tf, FF_CHUNK), BF16)

        prev = (k + 1) % 2
        acc_s[...] += dot_nn(dgate_s[prev], wgp_ref[...]) + dot_nn(dup_s[prev], wup_ref[...])
        h = h_s[...]
        gate = dot_nt(h, wg_ref[...])
        up = dot_nt(h, wu_ref[...])
        sg = jax.nn.sigmoid(gate)
        silu = gate * sg
        dact = dot_nt(df_s[...], wd_ref[...])
        act_ref[...] = (silu * up).astype(BF16)
        dgate = (dact * up * (sg * (1.0 + gate * (1.0 - sg)))).astype(BF16)
        dup = (dact * silu).astype(BF16)
        dgate_ref[...] = dgate
        dup_ref[...] = dup
        dgate_s[k % 2] = dgate
        dup_s[k % 2] = dup

        @pl.when(k == nk)
        def _():
            is_ctx = _ctx_rows(tf, n_ctx, j)
            dh = acc_s[...]
            r, xn = rms_stats(x_ref[...])
            n3 = n3_ref[...]
            sc = _mod_row(mod_ref, is_ctx, 4)
            xg = xn * n3
            dyv = dh * (1.0 + sc)
            split_sum(is_ctx, dh, st_ref, 0)
            split_sum(is_ctx, dh * xg, st_ref, 1)
            st_ref[3:4, :] = colsum(dyv * xn)
            dx1_ref[...] = dx_ref[...] + rms_bwd(r, xn, dyv * n3)

    return pl.pallas_call(
        body, name="ffn_bwd", grid=(bl, nt, nk + 1),
        in_specs=[tile, tile, tile, modspec, vec, vec, wspec, wspec, wspec, wprev, wprev],
        out_specs=[tile, tile, tile, ftile, ftile, ftile, stat],
        out_shape=[jax.ShapeDtypeStruct((bl, s, D), F32), jax.ShapeDtypeStruct((bl, s, D), BF16),
                   jax.ShapeDtypeStruct((bl, s, D), BF16), jax.ShapeDtypeStruct((bl, s, D_FF), BF16),
                   jax.ShapeDtypeStruct((bl, s, D_FF), BF16), jax.ShapeDtypeStruct((bl, s, D_FF), BF16),
                   jax.ShapeDtypeStruct((bl, nt, 8, D), F32)],
        scratch_shapes=[pltpu.VMEM((tf, D), BF16), pltpu.VMEM((tf, D), BF16), pltpu.VMEM((tf, D), F32),
                        pltpu.VMEM((2, tf, FF_CHUNK), BF16), pltpu.VMEM((2, tf, FF_CHUNK), BF16)],
        compiler_params=_cp(("arbitrary", "arbitrary", "arbitrary")),
    )(dx2, x1, f, mod, n3, n4, wg_t, wu_t, wd, wg_t, wu_t)


def loss_head(xs, target, nc):
    bl, s, _ = xs.shape
    nt = s // TT

    def body(x_ref, t_ref, dx_ref, l_ref):
        j = pl.program_id(1)

        @pl.when(j < nc)
        def _():
            dx_ref[...] = jnp.zeros_like(dx_ref)
            l_ref[...] = jnp.zeros_like(l_ref)

        @pl.when(j >= nc)
        def _():
            e = x_ref[...] - t_ref[...]
            dx_ref[...] = e * (1.0 / D)
            tok = jnp.mean(e * e, axis=-1, keepdims=True)
            l_ref[...] = jnp.zeros_like(l_ref) + 0.5 * jnp.sum(tok, axis=0, keepdims=True)

    return pl.pallas_call(
        body, name="loss_head", grid=(bl, nt),
        in_specs=[_tile_spec(D), pl.BlockSpec((None, TT, D), lambda b, j: (b, jnp.maximum(j - nc, 0), 0))],
        out_specs=[_tile_spec(D), pl.BlockSpec((None, None, 8, 128), lambda b, j: (b, j, 0, 0))],
        out_shape=[jax.ShapeDtypeStruct((bl, s, D), F32), jax.ShapeDtypeStruct((bl, nt, 8, 128), F32)],
        compiler_params=_cp(("arbitrary", "arbitrary")),
    )(xs, target)


def tn_matmul(a, b, name):
    t, ka = a.shape
    n = b.shape[1]
    tk = ka if ka <= 1408 else ka // 2
    tt = 512 if t % 512 == 0 else 256
    nsteps = t // tt

    def body(a_ref, b_ref, o_ref, acc_s):
        @pl.when(pl.program_id(1) == 0)
        def _():
            acc_s[...] = jnp.zeros_like(acc_s)

        acc_s[...] += dot_tn(a_ref[...], b_ref[...])

        @pl.when(pl.program_id(1) == nsteps - 1)
        def _():
            o_ref[...] = acc_s[...].astype(BF16)

    return pl.pallas_call(
        body, name=name, grid=(ka // tk, nsteps),
        in_specs=[pl.BlockSpec((tt, tk), lambda i, s: (s, i)), pl.BlockSpec((tt, n), lambda i, s: (s, 0))],
        out_specs=pl.BlockSpec((tk, n), lambda i, s: (i, 0)),
        out_shape=jax.ShapeDtypeStruct((ka, n), BF16),
        scratch_shapes=[pltpu.VMEM((tk, n), F32)],
        compiler_params=_cp(("arbitrary", "arbitrary")),
    )(a, b)


def qmm(terms, name):
    r = terms[0][0].shape[0]
    rt = r // 2 if r % 16 == 0 and r >= 512 else r
    n = len(terms)

    def body(*refs):
        acc = None
        for k in range(n):
            y = dot_nn(refs[2 * k][...].astype(BF16), refs[2 * k + 1][...])
            acc = y if acc is None else acc + y
        refs[2 * n][...] = acc

    row = pl.BlockSpec((rt, QW), lambda q, i: (i, q))
    wspec = pl.BlockSpec((None, QW, QW), lambda q, i: (q, 0, 0))
    return pl.pallas_call(
        body, name=name, grid=(4, r // rt), in_specs=[row, wspec] * n, out_specs=row,
        out_shape=jax.ShapeDtypeStruct((r, ROW_W), F32),
        compiler_params=_cp(("arbitrary", "arbitrary")),
    )(*[x for term in terms for x in term])


def _same_group(rows, cols, row_group, col_group):
    ri = jnp.bitwise_and(lax.broadcasted_iota(jnp.int32, (rows, cols), 0) // row_group, GQ - 1)
    ci = jnp.bitwise_and(lax.broadcasted_iota(jnp.int32, (rows, cols), 1) // col_group, GQ - 1)
    return ri == ci


def _spread_matrix():
    m = np.zeros((2 * SSM_P, QW), np.float32)
    for reim in range(2):
        for g in range(GQ):
            for p in range(SSM_P):
                m[reim * SSM_P + p, reim * (QW // 2) + g * SSM_P + p] = 1.0
    return jnp.asarray(m, BF16)


def assemble_ts(v, name):
    def body(v_ref, f_ref, big_ref, bigt_ref):
        keep = _same_group(GQ * SSM_H, QW, SSM_H, SSM_P)
        for e in range(TC):
            hi, lo = split_bf16(v_ref[e])
            t = jnp.where(keep, dot_nn(hi, f_ref[...]) + dot_nn(lo, f_ref[...]), 0.0)
            big_ref[e * 128:(e + 1) * 128, :] = t.astype(BF16)
            bigt_ref[:, e * 128:(e + 1) * 128] = t.T.astype(BF16)

    return pl.pallas_call(
        body, name=name, grid=(4,),
        in_specs=[pl.BlockSpec((None, TC, 128, 128), lambda q: (q, 0, 0, 0)),
                  pl.BlockSpec((128, QW), lambda q: (0, 0))],
        out_specs=[pl.BlockSpec((None, QW, QW), lambda q: (q, 0, 0))] * 2,
        out_shape=[jax.ShapeDtypeStruct((4, QW, QW), BF16), jax.ShapeDtypeStruct((4, QW, QW), BF16)],
        compiler_params=_cp(("arbitrary",)),
    )(v, _spread_matrix())


def assemble_tt(lags, name):
    def body(l_ref, m_ref, mt_ref):
        blocks = [l_ref[n] for n in range(2 * TC - 1)]
        flipped = [b.T.astype(BF16) for b in blocks]
        blocks = [b.astype(BF16) for b in blocks]
        for s in range(TC):
            for t in range(TC):
                m_ref[s * 128:(s + 1) * 128, t * 128:(t + 1) * 128] = blocks[t - s + TC - 1]
                mt_ref[t * 128:(t + 1) * 128, s * 128:(s + 1) * 128] = flipped[t - s + TC - 1]

    return pl.pallas_call(
        body, name=name, grid=(4,),
        in_specs=[pl.BlockSpec((None, 2 * TC - 1, 128, 128), lambda q: (q, 0, 0, 0))],
        out_specs=[pl.BlockSpec((None, QW, QW), lambda q: (q, 0, 0))] * 2,
        out_shape=[jax.ShapeDtypeStruct((4, QW, QW), BF16)] * 2,
        compiler_params=_cp(("arbitrary",)),
    )(lags)


def _qtn_call(body, a, b, out_shape, out_block, extra, name):
    r = a.shape[0]
    col = pl.BlockSpec((r, QW), lambda q: (0, q))
    return pl.pallas_call(
        body, name=name, grid=(4,),
        in_specs=[col, col] + [pl.BlockSpec(x.shape, lambda q: (0, 0)) for x in extra],
        out_specs=pl.BlockSpec((None,) + out_block, lambda q: (q,) + (0,) * len(out_block)),
        out_shape=jax.ShapeDtypeStruct((4,) + out_block, F32),
        compiler_params=_cp(("arbitrary",)),
    )(a, b, *extra)


def qtn_ts(a, b, name):
    def body(a_ref, b_ref, f_ref, o_ref):
        full = dot_tn(a_ref[...].astype(BF16), b_ref[...].astype(BF16))
        keep = _same_group(GQ * SSM_H, QW, SSM_H, SSM_P)
        for e in range(TC):
            hi, lo = split_bf16(jnp.where(keep, full[e * 128:(e + 1) * 128, :], 0.0))
            o_ref[e] = dot_nt(hi, f_ref[...]) + dot_nt(lo, f_ref[...])

    return _qtn_call(body, a, b, None, (TC, 128, 128), [_spread_matrix()], name)


def qtn_tt(a, b, name):
    def body(a_ref, b_ref, o_ref):
        full = dot_tn(a_ref[...].astype(BF16), b_ref[...].astype(BF16))
        for lag in range(-(TC - 1), TC):
            acc = None
            for s in range(TC):
                t = s + lag
                if 0 <= t < TC:
                    blk = full[s * 128:(s + 1) * 128, t * 128:(t + 1) * 128]
                    acc = blk if acc is None else acc + blk
            o_ref[lag + TC - 1] = acc

    return _qtn_call(body, a, b, None, (2 * TC - 1, 128, 128), [], name)


def _scan_row(i, rb, ncr, reverse):
    if not reverse:
        return i
    return jnp.where(i < ncr, ncr - 1 - i, rb - 1 - (i - ncr))


def _swap_re_im(h):
    half = QW // 2
    return jnp.concatenate([h[:, q * QW + (1 - k) * half:q * QW + (2 - k) * half] for q in range(4) for k in range(2)],
                           axis=1)


def chunk_scan(xs, lam_ab, ncr, reverse, name):
    bl, rb, _ = xs.shape

    def body(x_ref, l_ref, hp_ref):
        la, lb = l_ref[0:1, :], l_ref[1:2, :]

        def step(i, h):
            row = _scan_row(i, rb, ncr, reverse)
            hp_ref[pl.ds(row, 1), :] = h
            return la * h + lb * _swap_re_im(h) + x_ref[pl.ds(row, 1), :]

        lax.fori_loop(0, rb, step, jnp.zeros((1, ROW_W), F32))

    blk = pl.BlockSpec((None, rb, ROW_W), lambda b: (b, 0, 0))
    return pl.pallas_call(
        body, name=name, grid=(bl,),
        in_specs=[blk, pl.BlockSpec((8, ROW_W), lambda b: (0, 0))], out_specs=blk,
        out_shape=jax.ShapeDtypeStruct((bl, rb, ROW_W), F32),
        compiler_params=_cp(("arbitrary",)),
    )(xs, lam_ab)


def chunk_scan_bwd(dhp, hp, lam_ab, ncr, reverse, name):
    bl, rb, _ = dhp.shape

    def body(d_ref, hp_ref, l_ref, g_ref, dl_ref):
        la, lb = l_ref[0:1, :], l_ref[1:2, :]

        dl_ref[...] = jnp.zeros_like(dl_ref)

        def step(n, g):
            row = _scan_row(rb - 1 - n, rb, ncr, reverse)
            g_ref[pl.ds(row, 1), :] = g
            pv = hp_ref[pl.ds(row, 1), :]
            dl_ref[0:1, :] += g * pv
            dl_ref[1:2, :] += g * _swap_re_im(pv)
            return d_ref[pl.ds(row, 1), :] + la * g + _swap_re_im(lb * g)

        lax.fori_loop(0, rb, step, jnp.zeros((1, ROW_W), F32))

    blk = pl.BlockSpec((None, rb, ROW_W), lambda b: (b, 0, 0))
    return pl.pallas_call(
        body, name=name, grid=(bl,),
        in_specs=[blk, blk, pl.BlockSpec((8, ROW_W), lambda b: (0, 0))],
        out_specs=[blk, pl.BlockSpec((None, 8, ROW_W), lambda b: (b, 0, 0))],
        out_shape=[jax.ShapeDtypeStruct((bl, rb, ROW_W), F32), jax.ShapeDtypeStruct((bl, 8, ROW_W), F32)],
        compiler_params=_cp(("arbitrary",)),
    )(dhp, hp, lam_ab)


def _quarter_rows(v):
    e = v.shape[0]
    return v.reshape(e, 4, 8, SSM_P, SSM_H).transpose(0, 1, 2, 4, 3).reshape(e, 4, 8 * SSM_H, SSM_P)


def _token_state_map(vr, vi):
    return jnp.concatenate([_quarter_rows(vr), _quarter_rows(vi)], axis=-1).transpose(1, 0, 2, 3)


def ssm_build(lam_re, lam_im, log_dt, b_re, b_im, c_re, c_im, d):
    dt = jnp.exp(log_dt)[..., None]
    mag = jnp.exp(lam_re * dt)
    ang = lam_im * dt
    lr, li = mag * jnp.cos(ang), mag * jnp.sin(ang)
    den = lam_re * lam_re + lam_im * lam_im
    nr = lr - 1.0
    fr = (nr * lam_re + li * lam_im) / den
    fi = (li * lam_re - nr * lam_im) / den
    bbr = fr[..., None] * b_re - fi[..., None] * b_im
    bbi = fr[..., None] * b_im + fi[..., None] * b_re
    pr, pi = [jnp.ones_like(lr)], [jnp.zeros_like(lr)]
    for _ in range(TC):
        pr, pi = pr + [pr[-1] * lr - pi[-1] * li], pi + [pr[-1] * li + pi[-1] * lr]
    pr, pi = jnp.stack(pr), jnp.stack(pi)
    clr = c_re[None] * pr[:, :, :, None, :] - c_im[None] * pi[:, :, :, None, :]
    cli = c_re[None] * pi[:, :, :, None, :] + c_im[None] * pr[:, :, :, None, :]
    same_group = jnp.asarray(np.kron(np.eye(8), np.ones((SSM_H, SSM_H))), F32)
    ein = functools.partial(jnp.einsum, precision=HI)

    out, lag_blocks = {}, {}
    for k, name in ((0, "f"), (1, "r")):
        ar, ai = _quarter_rows(bbr[k][None])[0], _quarter_rows(bbi[k][None])[0]
        cr = clr[:TC, k].reshape(TC, 4, 8 * SSM_H, SSM_P)
        ci = cli[:TC, k].reshape(TC, 4, 8 * SSM_H, SSM_P)
        lag_blocks[k] = (ein('qap,nqbp->nqab', ar, cr) - ein('qap,nqbp->nqab', ai, ci)) * same_group
        es = [TC - 1 - s for s in range(TC)] if k == 0 else list(range(TC))
        sr = jnp.stack([pr[e, k][:, :, None] * bbr[k] - pi[e, k][:, :, None] * bbi[k] for e in es])
        si = jnp.stack([pr[e, k][:, :, None] * bbi[k] + pi[e, k][:, :, None] * bbr[k] for e in es])
        out["bs_" + name] = _token_state_map(sr, si)
        et = [t + 1 for t in range(TC)] if k == 0 else [TC - t for t in range(TC)]
        crt = jnp.stack([jnp.swapaxes(clr[e, k], 1, 2) for e in et])
        cit = jnp.stack([-jnp.swapaxes(cli[e, k], 1, 2) for e in et])
        out["cst_" + name] = _token_state_map(crt, cit)
        l8r, l8i = pr[TC, k].reshape(4, 1, QW // 2), pi[TC, k].reshape(4, 1, QW // 2)
        la = jnp.concatenate([l8r, l8r], axis=1).reshape(1, ROW_W)
        lb = jnp.concatenate([-l8i, l8i], axis=1).reshape(1, ROW_W)
        out["lam_" + name] = jnp.concatenate([la, lb, jnp.zeros((6, ROW_W), F32)], axis=0)
    skip = jnp.eye(8 * SSM_H, dtype=F32)[None] * d.reshape(4, 1, 8 * SSM_H)
    center = lag_blocks[0][0] + lag_blocks[1][0] + skip
    lags = [lag_blocks[1][n] for n in range(TC - 1, 0, -1)] + [center] + [lag_blocks[0][n] for n in range(1, TC)]
    out["lags"] = jnp.stack(lags, axis=1)
    return out


def ssm_operators(mats, tag):
    ops = {}
    ops["m"], ops["mt"] = assemble_tt(mats["lags"], "ssm_map_intra" + tag)
    for dname in ("f", "r"):
        ops["bs_" + dname], ops["bst_" + dname] = assemble_ts(mats["bs_" + dname], f"ssm_map_state_in_{dname}{tag}")
        ops["cst_" + dname], ops["cs_" + dname] = assemble_ts(mats["cst_" + dname], f"ssm_map_readout_{dname}{tag}")
    return ops


def ssm_forward(u3, mats, ops, ncr):
    bl, rb, _ = u3.shape
    u = u3.reshape(bl * rb, ROW_W)
    hps, terms = {}, [(u, ops["m"])]
    for dname, rev in (("f", False), ("r", True)):
        xs = qmm([(u, ops["bs_" + dname])], "ssm_state_in_" + dname)
        hp = chunk_scan(xs.reshape(bl, rb, ROW_W), mats["lam_" + dname], ncr, rev, "ssm_scan_" + dname)
        hps[dname] = hp.reshape(bl * rb, ROW_W)
        terms.append((hps[dname], ops["cs_" + dname]))
    return qmm(terms, "ssm_output").reshape(bl, rb, ROW_W), hps


def ssm_backward(dy3, u3, hps, mats, ops, ncr):
    bl, rb, _ = u3.shape
    u = u3.reshape(bl * rb, ROW_W)
    dyr = dy3.reshape(bl * rb, ROW_W)
    cot = {"lags": qtn_tt(u, dyr, "ssm_d_intra")}
    terms = [(dyr, ops["mt"])]
    for dname, rev in (("f", False), ("r", True)):
        dhp = qmm([(dyr, ops["cst_" + dname])], "ssm_dstate_" + dname)
        g, dl = chunk_scan_bwd(dhp.reshape(bl, rb, ROW_W), hps[dname].reshape(bl, rb, ROW_W), mats["lam_" + dname],
                               ncr, rev, "ssm_scan_bwd_" + dname)
        g = g.reshape(bl * rb, ROW_W)
        cot["lam_" + dname] = jnp.sum(dl, axis=0)
        cot["bs_" + dname] = qtn_ts(u, g, "ssm_d_state_in_" + dname)
        cot["cst_" + dname] = qtn_ts(dyr, hps[dname], "ssm_d_readout_" + dname)
        terms.append((g, ops["bst_" + dname]))
    return qmm(terms, "ssm_input_grad").reshape(bl, rb, ROW_W), cot


def mod_forward(act, w_mod, b_cols):
    nl, _, wc = w_mod.shape
    r = act.shape[0]

    def body(a_ref, w_ref, b_ref, o_ref):
        o_ref[...] = dot_nn(a_ref[...].astype(BF16), w_ref[...].astype(BF16)) + b_ref[...]

    return pl.pallas_call(
        body, name="mod_forward", grid=(nl,),
        in_specs=[pl.BlockSpec((r, D), lambda l: (0, 0)), pl.BlockSpec((None, D, wc), lambda l: (l, 0, 0)),
                  pl.BlockSpec((None, 1, wc), lambda l: (l, 0, 0))],
        out_specs=pl.BlockSpec((None, r, wc), lambda l: (l, 0, 0)),
        out_shape=jax.ShapeDtypeStruct((nl, r, wc), F32),
        compiler_params=_cp(("arbitrary",)),
    )(act, w_mod, b_cols)


def mod_backward(act, dmod, dctx, w_mod):
    nl, _, wc = w_mod.shape
    r = act.shape[0]

    def body(a_ref, d_ref, c_ref, w_ref, gw_ref, gc_ref):
        gw_ref[...] = dot_tn(a_ref[...].astype(BF16), d_ref[...].astype(BF16))
        gc_ref[...] = dot_nt(c_ref[...].astype(BF16), w_ref[...].astype(BF16))

    return pl.pallas_call(
        body, name="mod_backward", grid=(nl,),
        in_specs=[pl.BlockSpec((r, D), lambda l: (0, 0)), pl.BlockSpec((None, r, wc), lambda l: (l, 0, 0)),
                  pl.BlockSpec((None, 8, wc), lambda l: (l, 0, 0)), pl.BlockSpec((None, D, wc), lambda l: (l, 0, 0))],
        out_specs=[pl.BlockSpec((None, D, wc), lambda l: (l, 0, 0)), pl.BlockSpec((None, 8, D), lambda l: (l, 0, 0))],
        out_shape=[jax.ShapeDtypeStruct((nl, D, wc), F32), jax.ShapeDtypeStruct((nl, 8, D), F32)],
        compiler_params=_cp(("arbitrary",)),
    )(act, dmod, dctx, w_mod)


def _place():
    return lax.axis_index("x"), lax.axis_index("y"), lax.axis_index("c")


def all_gather_rows(arrs, name):
    n = len(arrs)
    rs = [a.shape[1] for a in arrs]

    def body(*refs):
        x_refs, o_refs = refs[:n], refs[n:2 * n]
        send_sems, recv_sems, local_sems = refs[2 * n:]
        x, y, c = _place()
        me, sibling = (x, y, c), (x, y, 1 - c)
        chips = [(1 - x, y), (x, 1 - y), (1 - x, 1 - y)]

        def rows(a, px, py, pc):
            return o_refs[a].at[:, pl.ds((4 * px + 2 * py + pc) * rs[a], rs[a]), :]

        def copy(a, k, block, to, src=None):
            return pltpu.make_async_remote_copy(
                src_ref=rows(a, *block) if src is None else src, dst_ref=rows(a, *block),
                send_sem=send_sems.at[a, k], recv_sem=recv_sems.at[a, k], device_id=to, device_id_type=MESH)

        mine = [pltpu.make_async_copy(x_refs[a], rows(a, *me), local_sems.at[a]) for a in range(n)]
        for cp in mine:
            cp.start()
        first = []
        for a in range(n):
            first.append(copy(a, 0, me, sibling, src=x_refs[a]))
            first += [copy(a, 1 + j, me, (*chip, c), src=x_refs[a]) for j, chip in enumerate(chips)]
        for cp in first:
            cp.start()
        passed = []
        for j, chip in enumerate(chips):
            for a in range(n):
                copy(a, 1 + j, (*chip, c), me).wait_recv()
                fwd = copy(a, 4 + j, (*chip, c), sibling)
                fwd.start()
                passed.append(fwd)
        for a in range(n):
            copy(a, 0, sibling, me).wait_recv()
            for j, chip in enumerate(chips):
                copy(a, 4 + j, (*chip, 1 - c), me).wait_recv()
        for cp in first + passed:
            cp.wait_send()
        for cp in mine:
            cp.wait()

    any_spec = pl.BlockSpec(memory_space=pl.ANY)
    return pl.pallas_call(
        body, name=name,
        in_specs=[any_spec] * n, out_specs=[any_spec] * n,
        out_shape=[jax.ShapeDtypeStruct((a.shape[0], N_DEV * a.shape[1], a.shape[2]), a.dtype) for a in arrs],
        scratch_shapes=[pltpu.SemaphoreType.DMA((n, 7)), pltpu.SemaphoreType.DMA((n, 7)), pltpu.SemaphoreType.DMA((n,))],
    )(*arrs)


def all_to_all_rows(arrs, name):
    n = len(arrs)
    rs = [a.shape[1] // N_DEV for a in arrs]
    flips = [(fx, fy, fc) for fx in (0, 1) for fy in (0, 1) for fc in (0, 1)][1:]

    def body(*refs):
        x_refs, o_refs = refs[:n], refs[n:2 * n]
        send_sems, recv_sems, local_sems = refs[2 * n:]
        x, y, c = _place()
        my_idx = 4 * x + 2 * y + c

        def block(a, idx):
            return x_refs[a].at[:, pl.ds(idx * rs[a], rs[a]), :]

        mine = [pltpu.make_async_copy(block(a, my_idx), o_refs[a].at[my_idx], local_sems.at[a]) for a in range(n)]
        for cp in mine:
            cp.start()
        sends = []
        for k, (fx, fy, fc) in enumerate(flips):
            px = 1 - x if fx else x
            py = 1 - y if fy else y
            pc = 1 - c if fc else c
            p_idx = 4 * px + 2 * py + pc
            for a in range(n):
                sends.append(pltpu.make_async_remote_copy(
                    src_ref=block(a, p_idx), dst_ref=o_refs[a].at[my_idx], send_sem=send_sems.at[a, k],
                    recv_sem=recv_sems.at[a, k], device_id=(px, py, pc), device_id_type=MESH))
        for cp in sends:
            cp.start()
        for k, (fx, fy, fc) in enumerate(flips):
            px = 1 - x if fx else x
            py = 1 - y if fy else y
            pc = 1 - c if fc else c
            p_idx = 4 * px + 2 * py + pc
            for a in range(n):
                pltpu.make_async_remote_copy(
                    src_ref=block(a, p_idx), dst_ref=o_refs[a].at[p_idx], send_sem=send_sems.at[a, k],
                    recv_sem=recv_sems.at[a, k], device_id=(px, py, pc), device_id_type=MESH).wait_recv()
        for cp in sends:
            cp.wait_send()
        for cp in mine:
            cp.wait()

    any_spec = pl.BlockSpec(memory_space=pl.ANY)
    return pl.pallas_call(
        body, name=name,
        in_specs=[any_spec] * n, out_specs=[any_spec] * n,
        out_shape=[jax.ShapeDtypeStruct((N_DEV, a.shape[0], r, a.shape[2]), a.dtype) for a, r in zip(arrs, rs)],
        scratch_shapes=[pltpu.SemaphoreType.DMA((n, 7)), pltpu.SemaphoreType.DMA((n, 7)), pltpu.SemaphoreType.DMA((n,))],
    )(*arrs)


def _peers():
    x, y, c = _place()
    out = []
    for fx in (0, 1):
        for fy in (0, 1):
            for fc in (0, 1):
                if fx or fy or fc:
                    px, py, pc = (1 - x if fx else x), (1 - y if fy else y), (1 - c if fc else c)
                    out.append(((px, py, pc), 4 * px + 2 * py + pc))
    return out, 4 * x + 2 * y + c


def _split_call(body, name, ins, n_sem_out, thru, extra_out_shape, extra_out_specs, sem_ins=(), after=None):
    hbm = pl.BlockSpec(memory_space=pltpu.HBM)
    sem = pl.BlockSpec(memory_space=pltpu.SEMAPHORE)
    n_thru = len(thru)
    tail_in = [sem] * len(sem_ins) + ([pl.BlockSpec(memory_space=pl.ANY)] if after is not None else [])
    return pl.pallas_call(
        body, name=name,
        out_shape=tuple(n_sem_out) + tuple(pltpu.HBM(a.shape, a.dtype) for a in thru) + tuple(extra_out_shape),
        in_specs=[hbm] * n_thru + tail_in,
        out_specs=(sem,) * len(n_sem_out) + (hbm,) * n_thru + tuple(extra_out_specs),
        input_output_aliases={i: i + len(n_sem_out) for i in range(n_thru)},
        compiler_params=pltpu.CompilerParams(has_side_effects=pltpu.SideEffectType.DATAFLOW_SIDE_EFFECTING),
    )(*ins, *sem_ins, *([after] if after is not None else []))


def gather_start(shards, name):
    n = len(shards)
    rs = [a.shape[1] for a in shards]
    lands = [lax.empty((a.shape[0], N_DEV * a.shape[1], a.shape[2]), a.dtype) for a in shards]

    def body(*refs):
        x_refs, land_refs = refs[:n], refs[n:2 * n]
        send_sems, recv_sems = refs[2 * n], refs[2 * n + 1]
        peers, my_idx = _peers()
        for k, (peer, _) in enumerate(peers):
            for a in range(n):
                pltpu.make_async_remote_copy(
                    src_ref=x_refs[a], dst_ref=land_refs[a].at[:, pl.ds(my_idx * rs[a], rs[a]), :],
                    send_sem=send_sems.at[a * 7 + k], recv_sem=recv_sems.at[a * 7 + k], device_id=peer,
                    device_id_type=MESH).start()
        refs[-1][...] = jnp.zeros_like(refs[-1])

    ins = [pltpu.with_memory_space_constraint(a, pltpu.HBM) for a in list(shards) + lands]
    outs = _split_call(body, name, ins, [pltpu.SemaphoreType.DMA((n * 7,))] * 2, ins,
                       [jax.ShapeDtypeStruct((8, 128), F32)], [pl.BlockSpec(memory_space=pltpu.VMEM)])
    return outs[0], outs[1], list(outs[2:2 + n]), list(outs[2 + n:2 + 2 * n]), outs[-1]


def gather_wait(send_sems, recv_sems, shards, lands, after, name):
    n = len(shards)
    rs = [a.shape[1] for a in shards]

    def body(*refs):
        x_refs, land_refs = refs[:n], refs[n:2 * n]
        s_sems, r_sems = refs[2 * n], refs[2 * n + 1]
        peers, _ = _peers()
        for k, (peer, p_idx) in enumerate(peers):
            for a in range(n):
                copy = pltpu.make_async_remote_copy(
                    src_ref=x_refs[a], dst_ref=land_refs[a].at[:, pl.ds(p_idx * rs[a], rs[a]), :],
                    send_sem=s_sems.at[a * 7 + k], recv_sem=r_sems.at[a * 7 + k], device_id=peer, device_id_type=MESH)
                copy.wait_send()
                copy.wait_recv()

    outs = _split_call(body, name, list(shards) + list(lands), [], list(shards) + list(lands), [], [],
                       sem_ins=(send_sems, recv_sems), after=after)
    my_idx = 4 * lax.axis_index("x") + 2 * lax.axis_index("y") + lax.axis_index("c")
    return [lax.dynamic_update_slice_in_dim(z, s, my_idx * r, axis=1) for z, s, r in zip(outs[n:], outs[:n], rs)]


def scatter_start(arrs, name):
    n = len(arrs)
    rs = [a.shape[1] // N_DEV for a in arrs]
    lands = [lax.empty((N_DEV, a.shape[0], r, a.shape[2]), a.dtype) for a, r in zip(arrs, rs)]

    def body(*refs):
        x_refs, land_refs = refs[:n], refs[n:2 * n]
        send_sems, recv_sems = refs[2 * n], refs[2 * n + 1]
        token = refs[-1]
        peers, my_idx = _peers()
        for k, (peer, p_idx) in enumerate(peers):
            for a in range(n):
                pltpu.make_async_remote_copy(
                    src_ref=x_refs[a].at[:, pl.ds(p_idx * rs[a], rs[a]), :], dst_ref=land_refs[a].at[my_idx],
                    send_sem=send_sems.at[a * 7 + k], recv_sem=recv_sems.at[a * 7 + k], device_id=peer,
                    device_id_type=MESH).start()
        token[...] = jnp.zeros_like(token)

    hbm = pl.BlockSpec(memory_space=pltpu.HBM)
    sem = pl.BlockSpec(memory_space=pltpu.SEMAPHORE)
    outs = pl.pallas_call(
        body, name=name,
        out_shape=(pltpu.SemaphoreType.DMA((n * 7,)), pltpu.SemaphoreType.DMA((n * 7,)))
        + tuple(pltpu.HBM(a.shape, a.dtype) for a in arrs) + tuple(pltpu.HBM(z.shape, z.dtype) for z in lands)
        + (jax.ShapeDtypeStruct((8, 128), F32),),
        in_specs=[hbm] * (2 * n),
        out_specs=(sem, sem) + (hbm,) * (2 * n) + (pl.BlockSpec(memory_space=pltpu.VMEM),),
        input_output_aliases={i: i + 2 for i in range(2 * n)},
        compiler_params=pltpu.CompilerParams(has_side_effects=pltpu.SideEffectType.DATAFLOW_SIDE_EFFECTING),
    )(*[pltpu.with_memory_space_constraint(a, pltpu.HBM) for a in arrs],
      *[pltpu.with_memory_space_constraint(z, pltpu.HBM) for z in lands])
    return outs[0], outs[1], list(outs[2:2 + n]), list(outs[2 + n:2 + 2 * n]), outs[-1]


def scatter_wait(send_sems, recv_sems, arrs, lands, after, name):
    n = len(arrs)
    rs = [a.shape[1] // N_DEV for a in arrs]

    def body(*refs):
        x_refs, land_refs = refs[:n], refs[n:2 * n]
        s_sems, r_sems = refs[2 * n], refs[2 * n + 1]
        peers, my_idx = _peers()
        for k, (peer, p_idx) in enumerate(peers):
            for a in range(n):
                copy = pltpu.make_async_remote_copy(
                    src_ref=x_refs[a].at[:, pl.ds(p_idx * rs[a], rs[a]), :], dst_ref=land_refs[a].at[p_idx],
                    send_sem=s_sems.at[a * 7 + k], recv_sem=r_sems.at[a * 7 + k], device_id=peer, device_id_type=MESH)
                copy.wait_send()
                copy.wait_recv()

    hbm = pl.BlockSpec(memory_space=pltpu.HBM)
    sem = pl.BlockSpec(memory_space=pltpu.SEMAPHORE)
    outs = pl.pallas_call(
        body, name=name,
        out_shape=tuple(pltpu.HBM(a.shape, a.dtype) for a in arrs) + tuple(pltpu.HBM(z.shape, z.dtype) for z in lands),
        in_specs=[hbm] * (2 * n) + [sem, sem, pl.BlockSpec(memory_space=pl.ANY)],
        out_specs=(hbm,) * (2 * n),
        input_output_aliases={i: i for i in range(2 * n)},
        compiler_params=pltpu.CompilerParams(has_side_effects=pltpu.SideEffectType.DATAFLOW_SIDE_EFFECTING),
    )(*arrs, *lands, send_sems, recv_sems, after)
    return list(outs[:n]), list(outs[n:])


def _row_tile(rows, cap):
    best = None
    for t in range(16, min(rows, cap) + 1, 16):
        if rows % t == 0:
            best = t
    return rows if best is None else best


def adamw(w, gparts, m, v, name):
    per_layer = isinstance(gparts, (list, tuple))
    glist = list(gparts) if per_layer else [gparts]
    n, _, ra, cb = glist[0].shape
    nl = w.shape[0]
    ng = len(glist)
    ta = _row_tile(ra, max(8, (1 << 19) // (cb * n)))

    def slot_sum(g_ref):
        g = g_ref[0].astype(F32)
        for p in range(1, n):
            g = g + g_ref[p].astype(F32)
        return g

    def body(*refs):
        w_ref, g_refs = refs[0], refs[1:1 + ng]
        m_ref, v_ref, go_ref, d_ref, mo_ref, vo_ref = refs[1 + ng:]
        g = slot_sum(g_refs[0])
        for layer in range(1, ng):
            g = jnp.where(pl.program_id(0) == layer, slot_sum(g_refs[layer]), g)
        mn = ADAM_B1 * m_ref[...] + (1.0 - ADAM_B1) * g
        vn = ADAM_B2 * v_ref[...] + (1.0 - ADAM_B2) * jnp.square(g)
        m_hat = mn / (1.0 - ADAM_B1 ** ADAM_STEP)
        v_hat = vn / (1.0 - ADAM_B2 ** ADAM_STEP)
        go_ref[...] = g
        d_ref[...] = -ADAM_LR * (m_hat / (jnp.sqrt(v_hat) + ADAM_EPS) + ADAM_WD * w_ref[...])
        mo_ref[...] = mn
        vo_ref[...] = vn

    blk = pl.BlockSpec((None, ta, cb), lambda l, i: (l, i, 0))
    if per_layer:
        gblk = pl.BlockSpec((n, None, ta, cb), lambda l, i: (0, 0, i, 0))
    else:
        gblk = pl.BlockSpec((n, None, ta, cb), lambda l, i: (0, l, i, 0))
    shp = jax.ShapeDtypeStruct((nl, ra, cb), F32)
    return pl.pallas_call(
        body, name=name, grid=(nl, ra // ta),
        in_specs=[blk] + [gblk] * ng + [blk, blk], out_specs=[blk] * 4, out_shape=[shp] * 4,
        compiler_params=_cp(("arbitrary", "arbitrary")),
    )(w, *glist, m, v)


def _sincos_2d(rows, cols, dim):
    quarter = dim // 4
    omega = 1.0 / (10000.0 ** (jnp.arange(quarter, dtype=F32) / quarter))
    r = jnp.arange(rows, dtype=F32)[:, None] * omega
    cc = jnp.arange(cols, dtype=F32)[:, None] * omega
    er = jnp.concatenate([jnp.sin(r), jnp.cos(r)], axis=-1)
    ec = jnp.concatenate([jnp.sin(cc), jnp.cos(cc)], axis=-1)
    pe = jnp.concatenate([jnp.broadcast_to(er[:, None, :], (rows, cols, dim // 2)),
                          jnp.broadcast_to(ec[None, :, :], (rows, cols, dim // 2))], axis=-1)
    return pe.reshape(rows * cols, dim)


def _pool_constants():
    nw = len(POOL_WINDOWS)
    band = np.zeros((2, nw, TT, TT), np.float32)
    icnt = np.zeros((2, TT, C_W), np.float32)
    for kind, n in ((0, TT), (1, GRID_W)):
        for i, w in enumerate(POOL_WINDOWS):
            for t in range(TT):
                base, tl = (t // n) * n, t % n
                lo = min(max(tl - w // 2, 0), n)
                hi = min(max(tl - w // 2 + w, 0), n)
                band[kind, i, t, base + lo:base + hi] = 1.0
                icnt[kind, t, i * (C_W // nw):(i + 1) * (C_W // nw)] = 1.0 / (hi - lo)
    return jnp.asarray(band, BF16), jnp.asarray(icnt, F32)


def _block_diag(blocks):
    n, a, _ = blocks.shape
    return jnp.einsum('gab,gh->gahb', blocks, jnp.eye(n, dtype=F32), precision=HI).reshape(n * a, n * a)


def _block_diag_parts(mat, n):
    a = mat.shape[0] // n
    m4 = mat.reshape(n, a, n, a)
    return jnp.stack([m4[g, :, g, :] for g in range(n)])


_SMALL = ("c_ctx", "b_mod", "norm_mix_pre", "norm_mix_post", "norm_ffn_pre", "norm_ffn_post", "sgu_w", "sgu_b",
          "ssm_lam_re", "ssm_lam_im", "ssm_log_dt", "ssm_b_re", "ssm_b_im", "ssm_c_re", "ssm_c_im", "ssm_d",
          "glu_b", "pool_w", "pool_scale")
_WEIGHTS = ("c_ctx", "w_mod", "b_mod", "norm_mix_pre", "norm_mix_post", "norm_ffn_pre", "norm_ffn_post", "w_in", "w_out",
            "sgu_w", "sgu_b", "ssm_lam_re", "ssm_lam_im", "ssm_log_dt", "ssm_b_re", "ssm_b_im", "ssm_c_re", "ssm_c_im",
            "ssm_d", "glu_w", "glu_b", "pool_w", "pool_scale", "ffn_w_gate", "ffn_w_up", "ffn_w_down")


def _pack_rows(a):
    flat = a.reshape(-1)
    rows = -(-flat.shape[0] // D)
    rows8 = -(-rows // 8) * 8
    return jnp.pad(flat, (0, rows8 * D - flat.shape[0])).reshape(rows8, D)


def _pack(tree):
    packed = jnp.concatenate([_pack_rows(tree[k]) for k in _SMALL], axis=0)
    return jnp.pad(packed, ((0, -packed.shape[0] % 64), (0, 0)))


def _unpack(packed, like):
    out, at = {}, 0
    for k in _SMALL:
        size = int(np.prod(like[k].shape))
        rows8 = -(-(-(-size // D)) // 8) * 8
        out[k] = packed[at:at + rows8].reshape(-1)[:size].reshape(like[k].shape)
        at += rows8
    return out


def kernel(x, c, ctx, c_ctx, w_mod, b_mod, norm_mix_pre, norm_mix_post, norm_ffn_pre, norm_ffn_post, w_in, w_out, sgu_w, sgu_b, ssm_lam_re, ssm_lam_im, ssm_log_dt, ssm_b_re, ssm_b_im, ssm_c_re, ssm_c_im, ssm_d, glu_w, glu_b, pool_w, pool_scale, ffn_w_gate, ffn_w_up, ffn_w_down, loss_target, m_c_ctx, m_w_mod, m_b_mod, m_norm_mix_pre, m_norm_mix_post, m_norm_ffn_pre, m_norm_ffn_post, m_w_in, m_w_out, m_sgu_w, m_sgu_b, m_ssm_lam_re, m_ssm_lam_im, m_ssm_log_dt, m_ssm_b_re, m_ssm_b_im, m_ssm_c_re, m_ssm_c_im, m_ssm_d, m_glu_w, m_glu_b, m_pool_w, m_pool_scale, m_ffn_w_gate, m_ffn_w_up, m_ffn_w_down, v_c_ctx, v_w_mod, v_b_mod, v_norm_mix_pre, v_norm_mix_post, v_norm_ffn_pre, v_norm_ffn_post, v_w_in, v_w_out, v_sgu_w, v_sgu_b, v_ssm_lam_re, v_ssm_lam_im, v_ssm_log_dt, v_ssm_b_re, v_ssm_b_im, v_ssm_c_re, v_ssm_c_im, v_ssm_d, v_glu_w, v_glu_b, v_pool_w, v_pool_scale, v_ffn_w_gate, v_ffn_w_up, v_ffn_w_down):
    wts = dict(c_ctx=c_ctx, w_mod=w_mod, b_mod=b_mod, norm_mix_pre=norm_mix_pre, norm_mix_post=norm_mix_post,
               norm_ffn_pre=norm_ffn_pre, norm_ffn_post=norm_ffn_post, w_in=w_in, w_out=w_out, sgu_w=sgu_w, sgu_b=sgu_b,
               ssm_lam_re=ssm_lam_re, ssm_lam_im=ssm_lam_im, ssm_log_dt=ssm_log_dt, ssm_b_re=ssm_b_re, ssm_b_im=ssm_b_im,
               ssm_c_re=ssm_c_re, ssm_c_im=ssm_c_im, ssm_d=ssm_d, glu_w=glu_w, glu_b=glu_b, pool_w=pool_w,
               pool_scale=pool_scale, ffn_w_gate=ffn_w_gate, ffn_w_up=ffn_w_up, ffn_w_down=ffn_w_down)
    mom_m = dict(c_ctx=m_c_ctx, w_mod=m_w_mod, b_mod=m_b_mod, norm_mix_pre=m_norm_mix_pre, norm_mix_post=m_norm_mix_post,
                 norm_ffn_pre=m_norm_ffn_pre, norm_ffn_post=m_norm_ffn_post, w_in=m_w_in, w_out=m_w_out, sgu_w=m_sgu_w,
                 sgu_b=m_sgu_b, ssm_lam_re=m_ssm_lam_re, ssm_lam_im=m_ssm_lam_im, ssm_log_dt=m_ssm_log_dt,
                 ssm_b_re=m_ssm_b_re, ssm_b_im=m_ssm_b_im, ssm_c_re=m_ssm_c_re, ssm_c_im=m_ssm_c_im, ssm_d=m_ssm_d,
                 glu_w=m_glu_w, glu_b=m_glu_b, pool_w=m_pool_w, pool_scale=m_pool_scale, ffn_w_gate=m_ffn_w_gate,
                 ffn_w_up=m_ffn_w_up, ffn_w_down=m_ffn_w_down)
    mom_v = dict(c_ctx=v_c_ctx, w_mod=v_w_mod, b_mod=v_b_mod, norm_mix_pre=v_norm_mix_pre, norm_mix_post=v_norm_mix_post,
                 norm_ffn_pre=v_norm_ffn_pre, norm_ffn_post=v_norm_ffn_post, w_in=v_w_in, w_out=v_w_out, sgu_w=v_sgu_w,
                 sgu_b=v_sgu_b, ssm_lam_re=v_ssm_lam_re, ssm_lam_im=v_ssm_lam_im, ssm_log_dt=v_ssm_log_dt,
                 ssm_b_re=v_ssm_b_re, ssm_b_im=v_ssm_b_im, ssm_c_re=v_ssm_c_re, ssm_c_im=v_ssm_c_im, ssm_d=v_ssm_d,
                 glu_w=v_glu_w, glu_b=v_glu_b, pool_w=v_pool_w, pool_scale=v_pool_scale, ffn_w_gate=v_ffn_w_gate,
                 ffn_w_up=v_ffn_w_up, ffn_w_down=v_ffn_w_down)

    bl, seq, _ = x.shape
    n_ctx = ctx.shape[1]
    assert n_ctx == TT and seq % TT == 0 and seq % GRID_W == 0
    depth = w_in.shape[0]
    nc = n_ctx // TT
    ncr = n_ctx // TC
    s_all = n_ctx + seq
    nt = s_all // TT
    t_all = bl * s_all
    n_batch = bl * N_DEV
    my_idx = 4 * lax.axis_index("x") + 2 * lax.axis_index("y") + lax.axis_index("c")
    wc = w_mod.shape[2]

    c_rows = jnp.pad(c, ((0, 8 - bl), (0, 0))) if bl < 8 else c
    rc = c_rows.shape[0]
    (c_all,) = all_gather_rows([c_rows[None]], "gather_c")
    c_all = c_all[0].reshape(N_DEV, rc, D)[:, :bl].reshape(n_batch, D)
    r_act = -(-(n_batch + 1) // 16) * 16
    pre_act = jnp.concatenate([c_all, c_ctx[None, :], jnp.zeros((r_act - n_batch - 1, D), F32)], axis=0)
    act = jax.nn.silu(pre_act)
    b_cols = lax.dynamic_slice_in_dim(b_mod, my_idx * wc, wc, axis=1)[:, None, :]
    mod_cols = mod_forward(act, w_mod, b_cols)
    (mod_all,) = all_gather_rows([mod_cols], "gather_mod")
    mod_all = mod_all.reshape(depth, N_DEV, r_act, wc).transpose(0, 2, 1, 3).reshape(depth, r_act, 6, D)
    mod_lat = lax.dynamic_slice_in_dim(mod_all, my_idx * bl, bl, axis=1)
    mod_ctx = jnp.broadcast_to(mod_all[:, n_batch:n_batch + 1], (depth, bl, 6, D))
    mods = jnp.pad(jnp.stack([mod_ctx, mod_lat], axis=2), ((0, 0), (0, 0), (0, 0), (0, 2), (0, 0)))

    tr = lambda a: jnp.swapaxes(a, 1, 2).astype(BF16)
    shards = dict(w_in=tr(w_in), w_out=w_out.astype(BF16), glu_w=glu_w.astype(BF16), gate=tr(ffn_w_gate),
                  up=tr(ffn_w_up), down=ffn_w_down.astype(BF16))
    mix_keys, ffn_keys = ("w_in", "w_out", "glu_w"), ("gate", "up", "down")
    layer = lambda k, i: shards[k][i:i + 1]
    full = [dict() for _ in range(depth)]
    for k, g in zip(mix_keys, all_gather_rows([layer(k, 0) for k in mix_keys], "gather_mix_weights_0")):
        full[0][k] = g[0]
    weights_in_flight = {0: (ffn_keys, gather_start([layer(k, 0) for k in ffn_keys], "gather_start_ffn_0"))}
    for i in range(1, depth):
        weights_in_flight[i] = (mix_keys + ffn_keys,
                                gather_start([layer(k, i) for k in mix_keys + ffn_keys], f"gather_start_layer_{i}"))
    start_token = sum(fl[1][4][0:1, 0:1] for fl in weights_in_flight.values())

    def land_weights(i, after, name):
        keys, (send_sems, recv_sems, sent, lands, _) = weights_in_flight[i]
        for k, g in zip(keys, gather_wait(send_sems, recv_sems, sent, lands, after, name)):
            full[i][k] = g[0]

    band, icnt = _pool_constants()
    seg_p = jnp.asarray(np.kron(np.eye(A_HEADS), np.full((A_W // A_HEADS,) * 2, A_HEADS / A_W)), BF16)
    pe = _sincos_2d(seq // GRID_W, GRID_W, D)
    xs = embed_tokens(x, ctx, pe)

    saved = []
    for i in range(depth):
        mats, ssm_vjp = jax.vjp(ssm_build, ssm_lam_re[i], ssm_lam_im[i], ssm_log_dt[i], ssm_b_re[i], ssm_b_im[i],
                                ssm_c_re[i], ssm_c_im[i], ssm_d[i])
        if i > 0:
            land_weights(i, xs, f"gather_wait_layer_{i}")
        cst = dict(sw=sgu_w[i].astype(BF16),
                   sbias=jnp.repeat(sgu_b[i].T, A_W // A_HEADS, axis=1),
                   seg_p=seg_p, band=band, icnt=icnt, wbd=_block_diag(pool_w[i]).astype(BF16),
                   pscale=pool_scale[i][None, :], glu_w=full[i]["glu_w"], glu_b=glu_b[i][None, :], w_out=full[i]["w_out"],
                   n2=norm_mix_post[i][None, :])
        n1, n3, n4 = norm_mix_pre[i][None, :], norm_ffn_pre[i][None, :], norm_ffn_post[i][None, :]
        if i == 0:
            n1 = n1 + start_token
        za, zu, zp = pre_mix(xs, mods[i], n1, full[i]["w_in"], nc)
        ops = ssm_operators(mats, f"_{i}")
        ys, hps = ssm_forward(zu, mats, ops, ncr)
        x1, m_pre = post_mix(xs, za, zp, ys, mods[i], cst, nc)
        if i == 0:
            land_weights(0, x1, "gather_wait_ffn_0")
        x2, f_pre = ffn_fwd(x1, mods[i], n3, n4, full[i]["gate"], full[i]["up"], full[i]["down"], n_ctx)
        saved.append(dict(xs=xs, za=za, zu=zu, zp=zp, ys=ys, hps=hps, x1=x1, m=m_pre, f=f_pre, cst=cst, mats=mats,
                          ops=ops, ssm_vjp=ssm_vjp, n1=n1, n3=n3, n4=n4))
        xs = x2

    dx, loss_parts = loss_head(xs, loss_target, nc)
    loss = lax.psum(jnp.sum(loss_parts[:, :, 0, 0]), ("x", "y", "c"))

    grads = {k: [None] * depth for k in _WEIGHTS}
    big = {k: [None] * depth for k in ("w_in", "w_out", "glu_w", "ffn_w_gate", "ffn_w_up", "ffn_w_down")}
    dmods = [None] * depth
    scatter_groups = (("ffn_w_gate", "ffn_w_up", "ffn_w_down"), ("w_out", "glu_w"), ("w_in",))
    in_flight = []

    def send_grads(i, group):
        flight = scatter_start([big[k][i][None] for k in scatter_groups[group]], f"scatter_start_{i}_{group}")
        in_flight.append((i, group, flight))
        return flight[4][0:1, 0:1]

    flat = lambda a: a.reshape(t_all, a.shape[-1])
    for i in reversed(range(depth)):
        sv = saved[i]
        dx1, h2, df, act_b, dgate, dup, st_f = ffn_bwd(dx, sv["x1"], sv["f"], mods[i], sv["n3"], sv["n4"],
                                                       full[i]["gate"], full[i]["up"], full[i]["down"], n_ctx)
        big["ffn_w_gate"][i] = tn_matmul(flat(dgate), flat(h2), f"grad_ffn_gate_{i}")
        big["ffn_w_up"][i] = tn_matmul(flat(dup), flat(h2), f"grad_ffn_up_{i}")
        big["ffn_w_down"][i] = tn_matmul(flat(act_b), flat(df), f"grad_ffn_down_{i}")
        cst_i = dict(sv["cst"], n2=sv["cst"]["n2"] + send_grads(i, 0))
        dza, dzp, dys, cat, dm, gg, dr, st_m, dsw, dsb, dwbd = post_mix_bwd(dx1, sv["m"], sv["za"], sv["zp"], sv["ys"],
                                                                            mods[i], cst_i, nc)
        big["w_out"][i] = tn_matmul(flat(cat), flat(dm), f"grad_w_out_{i}")
        big["glu_w"][i] = tn_matmul(flat(gg), flat(dr), f"grad_glu_w_{i}")
        mats_i = dict(sv["mats"], lam_f=sv["mats"]["lam_f"] + send_grads(i, 1))
        dzu, cot = ssm_backward(dys, sv["zu"], sv["hps"], mats_i, sv["ops"], ncr)
        (grads["ssm_lam_re"][i], grads["ssm_lam_im"][i], grads["ssm_log_dt"][i], grads["ssm_b_re"][i],
         grads["ssm_b_im"][i], grads["ssm_c_re"][i], grads["ssm_c_im"][i], grads["ssm_d"][i]) = sv["ssm_vjp"](cot)
        dx, h1, dz, st_p = pre_mix_bwd(dza, dzu, dzp, sv["xs"], dx1, mods[i], sv["n1"], full[i]["w_in"], nc)
        big["w_in"][i] = tn_matmul(flat(dz), flat(h1), f"grad_w_in_{i}")

        tiles = lambda st, row: st[:, :, row, :]
        allsum = lambda st, row: jnp.sum(tiles(st, row), axis=(0, 1))
        grads["norm_mix_pre"][i] = allsum(st_p, 2)
        grads["norm_mix_post"][i] = allsum(st_m, 1)
        grads["norm_ffn_pre"][i] = allsum(st_f, 3)
        grads["norm_ffn_post"][i] = allsum(st_f, 4)
        misc = allsum(st_m, 2)
        grads["glu_b"][i] = misc[:B_W]
        grads["pool_scale"][i] = misc[B_W:B_W + C_W]
        grads["sgu_w"][i] = dsw
        grads["sgu_b"][i] = jnp.sum(dsb.reshape(CHUNK, A_HEADS, A_W // A_HEADS), axis=2).T
        grads["pool_w"][i] = _block_diag_parts(dwbd, len(POOL_WINDOWS))
        mix = (tiles(st_p, 0), tiles(st_p, 1), tiles(st_m, 0))
        d_lat = jnp.stack([jnp.sum(t[:, nc:], axis=1) for t in mix]
                          + [jnp.sum(tiles(st_f, r), axis=1) for r in (0, 1, 2)], axis=1).reshape(bl, 6 * D)
        d_ctx = jnp.concatenate([jnp.sum(t[:, :nc], axis=(0, 1)) for t in mix]
                                + [allsum(st_f, r) for r in (5, 6, 7)]).reshape(1, 6 * D)
        dmods[i] = jnp.concatenate([d_lat, d_ctx, jnp.zeros((8 - (bl + 1) % 8 if (bl + 1) % 8 else 0, 6 * D), F32)],
                                   axis=0)
        token = send_grads(i, 2)
        if i > 0:
            saved[i - 1]["n3"] = saved[i - 1]["n3"] + token
        else:
            dmods[i] = dmods[i] + token
    grad_x = dx[:, n_ctx:, :]

    dmod_local = jnp.stack(dmods)
    rd = dmod_local.shape[1]
    (dmod_all,) = all_gather_rows([dmod_local], "gather_dmod")
    dmod_cols = lax.dynamic_slice_in_dim(dmod_all, my_idx * wc, wc, axis=2).reshape(depth, N_DEV, rd, wc)
    d_lat_all = dmod_cols[:, :, :bl].reshape(depth, n_batch, wc)
    d_ctx_all = dmod_cols[:, 0, bl]
    for p in range(1, N_DEV):
        d_ctx_all = d_ctx_all + dmod_cols[:, p, bl]
    dmod_rows = jnp.concatenate([d_lat_all, d_ctx_all[:, None, :], jnp.zeros((depth, r_act - n_batch - 1, wc), F32)],
                                axis=1)
    dctx_rows = jnp.pad(d_ctx_all[:, None, :], ((0, 0), (0, 7), (0, 0)))
    g_w_mod, dact_ctx = mod_backward(act, dmod_rows, dctx_rows, w_mod)
    sig_c = jax.nn.sigmoid(c_ctx)
    dsilu_c = sig_c * (1.0 + c_ctx * (1.0 - sig_c))
    small_g = {k: (jnp.stack(grads[k]) if grads[k][0] is not None else None) for k in _SMALL}
    small_g["c_ctx"] = jnp.sum(dact_ctx[:, 0, :], axis=0) * dsilu_c
    small_g["b_mod"] = jnp.stack([jnp.sum(dmods[i][:bl + 1], axis=0) for i in range(depth)])

    packed_g = _pack(small_g).astype(BF16)
    rows_s = packed_g.shape[0]
    (gathered,) = all_gather_rows([packed_g[None]], "gather_small_grads")
    gparts = gathered.reshape(N_DEV, 1, rows_s, D)
    small_w = {k: wts[k] for k in _SMALL}
    outs = adamw(_pack(small_w)[None], gparts, _pack({k: mom_m[k] for k in _SMALL})[None],
                 _pack({k: mom_v[k] for k in _SMALL})[None], "adamw_replicated")
    res = {k: [None] * 4 for k in _WEIGHTS}
    for slot, packed in enumerate(outs):
        un = _unpack(packed[0], small_w)
        for k in _SMALL:
            res[k][slot] = un[k]

    landed = {}
    for i, group, (send_sems, recv_sems, arrs_thru, lands_thru, _) in in_flight:
        sent, lands = scatter_wait(send_sems, recv_sems, arrs_thru, lands_thru, outs[0], f"scatter_wait_{i}_{group}")
        for k, a, z in zip(scatter_groups[group], sent, lands):
            r = a.shape[1] // N_DEV
            own = lax.dynamic_slice_in_dim(a, my_idx * r, r, axis=1)[None]
            landed[k, i] = lax.dynamic_update_slice_in_dim(z, own, my_idx, axis=0)
    for k in big:
        transposed = k in ("w_in", "ffn_w_gate", "ffn_w_up")
        view = (lambda a: jnp.swapaxes(a, 1, 2)) if transposed else (lambda a: a)
        o4 = adamw(view(wts[k]), [landed[k, i] for i in range(depth)], view(mom_m[k]), view(mom_v[k]), "adamw_" + k)
        res[k] = [view(o) for o in o4]
    res["w_mod"] = list(adamw(w_mod, g_w_mod[None], m_w_mod, v_w_mod, "adamw_w_mod"))

    return (loss, grad_x, *[res[k][0] for k in _WEIGHTS], *[res[k][1] for k in _WEIGHTS],
            *[res[k][2] for k in _WEIGHTS], *[res[k][3] for k in _WEIGHTS])
```

```python
import functools
import math

import numpy as np
import jax
import jax.numpy as jnp
from jax import lax
from jax.experimental import pallas as pl
from jax.experimental.pallas import tpu as pltpu

F32 = jnp.float32
BF16 = jnp.bfloat16
HI = lax.Precision.HIGHEST
MESH = pl.DeviceIdType.MESH

D = 1024
D_IN = 1280
D_FF = 2816
A_W = 256
B_W = 512
C_W = 256
A_HEADS = 4
CHUNK = 128
SSM_G = 32
SSM_H = 16
SSM_P = 64
GRID_W = 64
POOL_WINDOWS = (2, 4, 8, 16)
EPS = 1e-6
N_DEV = 8

TT = 256
TC = 8
ROW_W = TC * B_W
QW = ROW_W // 4
GQ = 8
FF_CHUNK = 256
VMEM_LIMIT = 60 * 1024 * 1024

ADAM_LR = 0.001
ADAM_B1 = 0.9
ADAM_B2 = 0.999
ADAM_EPS = 1e-08
ADAM_WD = 0.01
ADAM_STEP = 10


def _cp(sem):
    return pltpu.CompilerParams(dimension_semantics=sem, vmem_limit_bytes=VMEM_LIMIT)


def dot_nn(a, b):
    return jnp.dot(a, b, preferred_element_type=F32)


def dot_nt(a, b):
    return lax.dot_general(a, b, (((1,), (1,)), ((), ())), preferred_element_type=F32)


def dot_tn(a, b):
    return lax.dot_general(a, b, (((0,), (0,)), ((), ())), preferred_element_type=F32)


def split_bf16(x):
    hi = x.astype(BF16)
    lo = (x - hi.astype(F32)).astype(BF16)
    return hi, lo


def gelu(x):
    return jax.nn.gelu(x)


def gelu_grad(x):
    c = math.sqrt(2.0 / math.pi)
    t = jnp.tanh(c * (x + 0.044715 * x * x * x))
    return 0.5 * (1.0 + t) + 0.5 * x * (1.0 - t * t) * c * (1.0 + 3.0 * 0.044715 * x * x)


def rms_stats(x):
    r = lax.rsqrt(jnp.mean(x * x, axis=-1, keepdims=True) + EPS)
    return r, x * r


def rms_bwd(r, xn, dxn):
    return r * (dxn - xn * jnp.mean(dxn * xn, axis=-1, keepdims=True))


def colsum(x):
    return jnp.sum(x, axis=0, keepdims=True)


def lane_group(width, group):
    return lax.broadcasted_iota(jnp.int32, (1, width), 1) // group


def _tile_spec(width):
    return pl.BlockSpec((None, TT, width), lambda b, j: (b, j, 0))


def _mod_spec(nc):
    return pl.BlockSpec((None, None, 8, D), lambda b, j: (b, jnp.where(j >= nc, 1, 0), 0, 0))


def _full_spec(shape):
    zeros = (0,) * len(shape)
    return pl.BlockSpec(shape, lambda b, j: zeros)


def _kind_spec(shape, nc):
    zeros = (0,) * len(shape)
    return pl.BlockSpec((None,) + shape, lambda b, j: (jnp.where(j >= nc, 1, 0),) + zeros)


def _stat_spec():
    return pl.BlockSpec((None, None, 8, D), lambda b, j: (b, j, 0, 0))


def _chunk_spec():
    return pl.BlockSpec((None, TT // TC, ROW_W), lambda b, j: (b, j, 0))


def _rows_to_chunks(val, scratch, out_ref):
    for cb in range(B_W // 128):
        scratch[cb] = val[:, cb * 128:(cb + 1) * 128]
    for s in range(TC):
        for cb in range(B_W // 128):
            lo = cb * QW + s * 128
            out_ref[:, lo:lo + 128] = scratch.at[cb][pl.ds(s, TT // TC, stride=TC), :]


def _chunks_to_rows(in_ref, scratch):
    for s in range(TC):
        for cb in range(B_W // 128):
            lo = cb * QW + s * 128
            scratch.at[cb][pl.ds(s, TT // TC, stride=TC), :] = in_ref[:, lo:lo + 128]
    return jnp.concatenate([scratch[cb] for cb in range(B_W // 128)], axis=1)


def _chunk_scratch():
    return pltpu.VMEM((B_W // 128, TT, 128), F32)


def embed_tokens(x, ctx, pe):
    bl, seq, _ = x.shape
    nc = ctx.shape[1] // TT
    nt = nc + seq // TT

    def body(ctx_ref, x_ref, pe_ref, o_ref):
        j = pl.program_id(1)

        @pl.when(j < nc)
        def _():
            o_ref[...] = ctx_ref[...]

        @pl.when(j >= nc)
        def _():
            o_ref[...] = x_ref[...] + pe_ref[...]

    return pl.pallas_call(
        body, name="embed_tokens", grid=(bl, nt),
        in_specs=[pl.BlockSpec((None, TT, D), lambda b, j: (b, jnp.minimum(j, nc - 1), 0)),
                  pl.BlockSpec((None, TT, D), lambda b, j: (b, jnp.maximum(j - nc, 0), 0)),
                  pl.BlockSpec((TT, D), lambda b, j: (jnp.maximum(j - nc, 0), 0))],
        out_specs=_tile_spec(D),
        out_shape=jax.ShapeDtypeStruct((bl, nt * TT, D), F32),
        compiler_params=_cp(("arbitrary", "arbitrary")),
    )(ctx, x, pe)


def pre_mix(xs, mod, n1, w_int, nc):
    bl, s, _ = xs.shape

    def body(x_ref, mod_ref, n_ref, w_ref, za_ref, zu_ref, zp_ref, u_s):
        r, xn = rms_stats(x_ref[...])
        h = xn * n_ref[...] * (1.0 + mod_ref[1:2, :]) + mod_ref[0:1, :]
        z = dot_nt(h.astype(BF16), w_ref[...])
        za_ref[...] = z[:, :2 * A_W]
        _rows_to_chunks(z[:, 2 * A_W:2 * A_W + B_W], u_s, zu_ref)
        zp_ref[...] = z[:, 2 * A_W + B_W:]

    return pl.pallas_call(
        body, name="pre_mix", grid=(bl, s // TT),
        in_specs=[_tile_spec(D), _mod_spec(nc), _full_spec((1, D)), _full_spec((D_IN, D))],
        out_specs=[_tile_spec(2 * A_W), _chunk_spec(), _tile_spec(C_W)],
        out_shape=[jax.ShapeDtypeStruct((bl, s, 2 * A_W), F32), jax.ShapeDtypeStruct((bl, s // TC, ROW_W), F32),
                   jax.ShapeDtypeStruct((bl, s, C_W), F32)],
        scratch_shapes=[_chunk_scratch()],
        compiler_params=_cp(("arbitrary", "arbitrary")),
    )(xs, mod, n1, w_int)


def _seg_mean(x, seg_p):
    hi, lo = split_bf16(x)
    return dot_nn(hi, seg_p) + dot_nn(lo, seg_p)


def _sgu_forward(za, sw_ref, sbias, seg_p):
    ge = gelu(za)
    u, v = ge[:, :A_W], ge[:, A_W:]
    dv = v - _seg_mean(v, seg_p)
    rs = lax.rsqrt(_seg_mean(dv * dv, seg_p) + EPS)
    vn = dv * rs
    head = lane_group(A_W, A_W // A_HEADS)
    parts = []
    for c2 in range(TT // CHUNK):
        vb = vn[c2 * CHUNK:(c2 + 1) * CHUNK].astype(BF16)
        sc = sbias
        for h in range(A_HEADS):
            sc = sc + jnp.where(head == h, dot_nn(sw_ref[h], vb), 0.0)
        parts.append(sc)
    sg = jnp.concatenate(parts, axis=0)
    return u * sg, (u, vn, rs, sg)


def _pool_forward(zp, band_ref, icnt, wbd, pscale):
    hi, lo = split_bf16(zp)
    grp = lane_group(C_W, C_W // len(POOL_WINDOWS))
    q = jnp.zeros_like(zp)
    for i in range(len(POOL_WINDOWS)):
        t = dot_nn(band_ref[i], hi) + dot_nn(band_ref[i], lo)
        q = jnp.where(grp == i, t, q)
    q = q * icnt - zp
    o = dot_nn(q.astype(BF16), wbd)
    return o * pscale, (q, o)


def _glu_forward(y, glu_w, glu_b):
    g = gelu(y)
    sg = jax.nn.sigmoid(dot_nn(g.astype(BF16), glu_w) + glu_b)
    return g * sg, (g, sg)


_MIX_CONST_SHAPES = dict(sw=(A_HEADS, CHUNK, CHUNK), sbias=(CHUNK, A_W), seg_p=(A_W, A_W), wbd=(C_W, C_W),
                         pscale=(1, C_W), glu_w=(B_W, B_W), glu_b=(1, B_W), w_out=(D, D), n2=(1, D))


def _mix_const_specs(nc):
    return ([_full_spec(_MIX_CONST_SHAPES[k]) for k in ("sw", "sbias", "seg_p")]
            + [_kind_spec((len(POOL_WINDOWS), TT, TT), nc), _kind_spec((TT, C_W), nc)]
            + [_full_spec(_MIX_CONST_SHAPES[k]) for k in ("wbd", "pscale", "glu_w", "glu_b", "w_out", "n2")])


def _mix_const_args(cst):
    return [cst[k] for k in ("sw", "sbias", "seg_p", "band", "icnt", "wbd", "pscale", "glu_w", "glu_b", "w_out", "n2")]


def post_mix(xs, za, zp, ys, mod, cst, nc):
    bl, s, _ = xs.shape

    def body(x_ref, za_ref, zp_ref, y_ref, mod_ref, sw_ref, sbias_ref, seg_ref, band_ref, icnt_ref, wbd_ref,
             ps_ref, gw_ref, gb_ref, wo_ref, n2_ref, x1_ref, m_ref, y_s):
        a, _ = _sgu_forward(za_ref[...], sw_ref, sbias_ref[...], seg_ref[...])
        p, _ = _pool_forward(zp_ref[...], band_ref, icnt_ref[...], wbd_ref[...], ps_ref[...])
        sm, _ = _glu_forward(_chunks_to_rows(y_ref, y_s), gw_ref[...], gb_ref[...])
        cat = jnp.concatenate([a, sm, p], axis=1).astype(BF16)
        m = dot_nn(cat, wo_ref[...])
        _, mn = rms_stats(m)
        m_ref[...] = m
        x1_ref[...] = x_ref[...] + mod_ref[2:3, :] * (mn * n2_ref[...])

    return pl.pallas_call(
        body, name="post_mix", grid=(bl, s // TT),
        in_specs=[_tile_spec(D), _tile_spec(2 * A_W), _tile_spec(C_W), _chunk_spec(), _mod_spec(nc)]
        + _mix_const_specs(nc),
        out_specs=[_tile_spec(D), _tile_spec(D)],
        out_shape=[jax.ShapeDtypeStruct((bl, s, D), F32), jax.ShapeDtypeStruct((bl, s, D), F32)],
        scratch_shapes=[_chunk_scratch()],
        compiler_params=_cp(("arbitrary", "arbitrary")),
    )(xs, za, zp, ys, mod, *_mix_const_args(cst))


def post_mix_bwd(dx1, m, za, zp, ys, mod, cst, nc):
    bl, s, _ = m.shape
    nt = s // TT

    def body(dx_ref, m_ref, za_ref, zp_ref, y_ref, mod_ref, sw_ref, sbias_ref, seg_ref, band_ref, icnt_ref,
             wbd_ref, ps_ref, gw_ref, gb_ref, wo_ref, n2_ref,
             dza_ref, dzp_ref, dy_ref, cat_ref, dm_ref, gg_ref, dr_ref, st_ref, dsw_ref, dsb_ref, dwbd_ref, y_s):
        first = jnp.logical_and(pl.program_id(0) == 0, pl.program_id(1) == 0)

        @pl.when(first)
        def _():
            dsw_ref[...] = jnp.zeros_like(dsw_ref)
            dsb_ref[...] = jnp.zeros_like(dsb_ref)
            dwbd_ref[...] = jnp.zeros_like(dwbd_ref)

        seg_p = seg_ref[...]
        za = za_ref[...]
        zp_v = zp_ref[...]
        yv = _chunks_to_rows(y_ref, y_s)
        a, (u, vn, rs, sg) = _sgu_forward(za, sw_ref, sbias_ref[...], seg_p)
        p, (q, o) = _pool_forward(zp_v, band_ref, icnt_ref[...], wbd_ref[...], ps_ref[...])
        sm, (g, sig) = _glu_forward(yv, gw_ref[...], gb_ref[...])
        cat_ref[...] = jnp.concatenate([a, sm, p], axis=1).astype(BF16)

        dx = dx_ref[...]
        g1 = mod_ref[2:3, :]
        n2 = n2_ref[...]
        mv = m_ref[...]
        rm, mn = rms_stats(mv)
        st_ref[...] = jnp.zeros_like(st_ref)
        st_ref[0:1, :] = colsum(dx * (mn * n2))
        st_ref[1:2, :] = colsum(dx * g1 * mn)
        dm = rms_bwd(rm, mn, dx * g1 * n2)
        dmb = dm.astype(BF16)
        dm_ref[...] = dmb
        dcat = dot_nt(dmb, wo_ref[...])
        da, dsm, dp = dcat[:, :A_W], dcat[:, A_W:A_W + B_W], dcat[:, A_W + B_W:]

        du = da * sg
        dsv = da * u
        head = lane_group(A_W, A_W // A_HEADS)
        dvn_parts = []
        dsb_acc = jnp.zeros((CHUNK, A_W), F32)
        for c2 in range(TT // CHUNK):
            dsc = dsv[c2 * CHUNK:(c2 + 1) * CHUNK]
            dsc_b = dsc.astype(BF16)
            vb = vn[c2 * CHUNK:(c2 + 1) * CHUNK].astype(BF16)
            dsb_acc = dsb_acc + dsc
            dvn_c = jnp.zeros((CHUNK, A_W), F32)
            for h in range(A_HEADS):
                dsw_ref[h] += dot_nt(jnp.where(head == h, dsc, 0.0).astype(BF16), vb)
                dvn_c = dvn_c + jnp.where(head == h, dot_tn(sw_ref[h], dsc_b), 0.0)
            dvn_parts.append(dvn_c)
        dsb_ref[...] += dsb_acc
        dvn = jnp.concatenate(dvn_parts, axis=0)
        dv = rs * (dvn - _seg_mean(dvn, seg_p) - vn * _seg_mean(dvn * vn, seg_p))
        dza_ref[...] = jnp.concatenate([du, dv], axis=1) * gelu_grad(za)

        ps = ps_ref[...]
        do = dp * ps
        dps = colsum(dp * o)
        dob = do.astype(BF16)
        dwbd_ref[...] += dot_tn(q.astype(BF16), dob)
        dq = dot_nt(dob, wbd_ref[...])
        hi, lo = split_bf16(dq * icnt_ref[...])
        grp = lane_group(C_W, C_W // len(POOL_WINDOWS))
        dzp = -dq
        for i in range(len(POOL_WINDOWS)):
            t = dot_tn(band_ref[i], hi) + dot_tn(band_ref[i], lo)
            dzp = dzp + jnp.where(grp == i, t, 0.0)
        dzp_ref[...] = dzp

        dr = dsm * g * sig * (1.0 - sig)
        drb = dr.astype(BF16)
        dr_ref[...] = drb
        gg_ref[...] = g.astype(BF16)
        dg = dsm * sig + dot_nt(drb, gw_ref[...])
        _rows_to_chunks(dg * gelu_grad(yv), y_s, dy_ref)
        st_ref[2:3, :] = jnp.concatenate([colsum(dr), dps, jnp.zeros((1, D - B_W - C_W), F32)], axis=1)

    acc = lambda shape: pl.BlockSpec(shape, lambda b, j: (0,) * len(shape))
    return pl.pallas_call(
        body, name="post_mix_bwd", grid=(bl, nt),
        in_specs=[_tile_spec(D), _tile_spec(D), _tile_spec(2 * A_W), _tile_spec(C_W), _chunk_spec(), _mod_spec(nc)]
        + _mix_const_specs(nc),
        out_specs=[_tile_spec(2 * A_W), _tile_spec(C_W), _chunk_spec(), _tile_spec(D), _tile_spec(D),
                   _tile_spec(B_W), _tile_spec(B_W), _stat_spec(),
                   acc((A_HEADS, CHUNK, CHUNK)), acc((CHUNK, A_W)), acc((C_W, C_W))],
        out_shape=[jax.ShapeDtypeStruct((bl, s, 2 * A_W), F32), jax.ShapeDtypeStruct((bl, s, C_W), F32),
                   jax.ShapeDtypeStruct((bl, s // TC, ROW_W), F32), jax.ShapeDtypeStruct((bl, s, D), BF16),
                   jax.ShapeDtypeStruct((bl, s, D), BF16), jax.ShapeDtypeStruct((bl, s, B_W), BF16),
                   jax.ShapeDtypeStruct((bl, s, B_W), BF16), jax.ShapeDtypeStruct((bl, nt, 8, D), F32),
                   jax.ShapeDtypeStruct((A_HEADS, CHUNK, CHUNK), F32), jax.ShapeDtypeStruct((CHUNK, A_W), F32),
                   jax.ShapeDtypeStruct((C_W, C_W), F32)],
        scratch_shapes=[_chunk_scratch()],
        compiler_params=_cp(("arbitrary", "arbitrary")),
    )(dx1, m, za, zp, ys, mod, *_mix_const_args(cst))


def pre_mix_bwd(dza, dzu, dzp, xs, dxres, mod, n1, w_int, nc):
    bl, s, _ = xs.shape
    nt = s // TT

    def body(dza_ref, dzu_ref, dzp_ref, x_ref, dres_ref, mod_ref, n_ref, w_ref, dx_ref, h_ref, dz_ref, st_ref, u_s):
        dz = jnp.concatenate([dza_ref[...], _chunks_to_rows(dzu_ref, u_s), dzp_ref[...]], axis=1).astype(BF16)
        dz_ref[...] = dz
        dh = dot_nn(dz, w_ref[...])
        r, xn = rms_stats(x_ref[...])
        n1v = n_ref[...]
        sc = mod_ref[1:2, :]
        xg = xn * n1v
        h_ref[...] = (xg * (1.0 + sc) + mod_ref[0:1, :]).astype(BF16)
        dyv = dh * (1.0 + sc)
        st_ref[...] = jnp.zeros_like(st_ref)
        st_ref[0:1, :] = colsum(dh)
        st_ref[1:2, :] = colsum(dh * xg)
        st_ref[2:3, :] = colsum(dyv * xn)
        dx_ref[...] = dres_ref[...] + rms_bwd(r, xn, dyv * n1v)

    return pl.pallas_call(
        body, name="pre_mix_bwd", grid=(bl, nt),
        in_specs=[_tile_spec(2 * A_W), _chunk_spec(), _tile_spec(C_W), _tile_spec(D), _tile_spec(D), _mod_spec(nc),
                  _full_spec((1, D)), _full_spec((D_IN, D))],
        out_specs=[_tile_spec(D), _tile_spec(D), _tile_spec(D_IN), _stat_spec()],
        out_shape=[jax.ShapeDtypeStruct((bl, s, D), F32), jax.ShapeDtypeStruct((bl, s, D), BF16),
                   jax.ShapeDtypeStruct((bl, s, D_IN), BF16), jax.ShapeDtypeStruct((bl, nt, 8, D), F32)],
        scratch_shapes=[_chunk_scratch()],
        compiler_params=_cp(("arbitrary", "arbitrary")),
    )(dza, dzu, dzp, xs, dxres, mod, n1, w_int)


def _ffn_tile(s):
    return 768 if s % 768 == 0 else TT


def _ctx_rows(tf, n_ctx, j):
    return lax.broadcasted_iota(jnp.int32, (tf, 1), 0) + j * tf < n_ctx


def _mod_row(mod_ref, is_ctx, row):
    return jnp.where(is_ctx, mod_ref[0, row:row + 1, :], mod_ref[1, row:row + 1, :])


def ffn_fwd(x1, mod, n3, n4, wg_t, wu_t, wd, n_ctx):
    bl, s, _ = x1.shape
    tf = _ffn_tile(s)
    nk = D_FF // FF_CHUNK
    tile = pl.BlockSpec((None, tf, D), lambda b, j, k: (b, j, 0))
    modspec = pl.BlockSpec((None, 2, 8, D), lambda b, j, k: (b, 0, 0, 0))
    vec = pl.BlockSpec((1, D), lambda b, j, k: (0, 0))
    wspec = pl.BlockSpec((FF_CHUNK, D), lambda b, j, k: (k, 0))

    def body(x_ref, mod_ref, n3_ref, n4_ref, wg_ref, wu_ref, wd_ref, x2_ref, f_ref, h_s, acc_s):
        j, k = pl.program_id(1), pl.program_id(2)

        @pl.when(k == 0)
        def _():
            is_ctx = _ctx_rows(tf, n_ctx, j)
            _, xn = rms_stats(x_ref[...])
            h_s[...] = (xn * n3_ref[...] * (1.0 + _mod_row(mod_ref, is_ctx, 4)) + _mod_row(mod_ref, is_ctx, 3)).astype(BF16)
            acc_s[...] = jnp.zeros_like(acc_s)

        h = h_s[...]
        gate = dot_nt(h, wg_ref[...])
        up = dot_nt(h, wu_ref[...])
        act = (gate * jax.nn.sigmoid(gate)) * up
        acc_s[...] += dot_nn(act.astype(BF16), wd_ref[...])

        @pl.when(k == nk - 1)
        def _():
            f = acc_s[...]
            f_ref[...] = f
            _, fn = rms_stats(f)
            x2_ref[...] = x_ref[...] + _mod_row(mod_ref, _ctx_rows(tf, n_ctx, j), 5) * (fn * n4_ref[...])

    return pl.pallas_call(
        body, name="ffn_fwd", grid=(bl, s // tf, nk),
        in_specs=[tile, modspec, vec, vec, wspec, wspec, wspec],
        out_specs=[tile, tile],
        out_shape=[jax.ShapeDtypeStruct((bl, s, D), F32), jax.ShapeDtypeStruct((bl, s, D), F32)],
        scratch_shapes=[pltpu.VMEM((tf, D), BF16), pltpu.VMEM((tf, D), F32)],
        compiler_params=_cp(("arbitrary", "arbitrary", "arbitrary")),
    )(x1, mod, n3, n4, wg_t, wu_t, wd)


def ffn_bwd(dx2, x1, f, mod, n3, n4, wg_t, wu_t, wd, n_ctx):
    bl, s, _ = x1.shape
    tf = _ffn_tile(s)
    nt = s // tf
    nk = D_FF // FF_CHUNK
    tile = pl.BlockSpec((None, tf, D), lambda b, j, k: (b, j, 0))
    ftile = pl.BlockSpec((None, tf, FF_CHUNK), lambda b, j, k: (b, j, jnp.minimum(k, nk - 1)))
    modspec = pl.BlockSpec((None, 2, 8, D), lambda b, j, k: (b, 0, 0, 0))
    vec = pl.BlockSpec((1, D), lambda b, j, k: (0, 0))
    wspec = pl.BlockSpec((FF_CHUNK, D), lambda b, j, k: (jnp.minimum(k, nk - 1), 0))
    wprev = pl.BlockSpec((FF_CHUNK, D), lambda b, j, k: (jnp.maximum(k - 1, 0), 0))
    stat = pl.BlockSpec((None, None, 8, D), lambda b, j, k: (b, j, 0, 0))

    def split_sum(is_ctx, v, st_ref, row):
        st_ref[row:row + 1, :] = colsum(jnp.where(is_ctx, 0.0, v))
        st_ref[row + 5:row + 6, :] = colsum(jnp.where(is_ctx, v, 0.0))

    def body(dx_ref, x_ref, f_ref, mod_ref, n3_ref, n4_ref, wg_ref, wu_ref, wd_ref, wgp_ref, wup_ref,
             dx1_ref, h_ref, df_ref, act_ref, dgate_ref, dup_ref, st_ref, h_s, df_s, acc_s, dgate_s, dup_s):
        j, k = pl.program_id(1), pl.program_id(2)

        @pl.when(k == 0)
        def _():
            is_ctx = _ctx_rows(tf, n_ctx, j)
            dx = dx_ref[...]
            g2 = _mod_row(mod_ref, is_ctx, 5)
            n4 = n4_ref[...]
            rf, fn = rms_stats(f_ref[...])
            st_ref[...] = jnp.zeros_like(st_ref)
            split_sum(is_ctx, dx * (fn * n4), st_ref, 2)
            st_ref[4:5, :] = colsum(dx * g2 * fn)
            df = rms_bwd(rf, fn, dx * g2 * n4).astype(BF16)
            df_s[...] = df
            df_ref[...] = df
            _, xn = rms_stats(x_ref[...])
            h = (xn * n3_ref[...] * (1.0 + _mod_row(mod_ref, is_ctx, 4)) + _mod_row(mod_ref, is_ctx, 3)).astype(BF16)
            h_s[...] = h
            h_ref[...] = h
            acc_s[...] = jnp.zeros_like(acc_s)
            dgate_s[1] = jnp.zeros((tf, FF_CHUNK), BF16)
            dup_s[1] = jnp.zeros((tf, FF_CHUNK), BF16)

        prev = (k + 1) % 2
        acc_s[...] += dot_nn(dgate_s[prev], wgp_ref[...]) + dot_nn(dup_s[prev], wup_ref[...])
        h = h_s[...]
        gate = dot_nt(h, wg_ref[...])
        up = dot_nt(h, wu_ref[...])
        sg = jax.nn.sigmoid(gate)
        silu = gate * sg
        dact = dot_nt(df_s[...], wd_ref[...])
        act_ref[...] = (silu * up).astype(BF16)
        dgate = (dact * up * (sg * (1.0 + gate * (1.0 - sg)))).astype(BF16)
        dup = (dact * silu).astype(BF16)
        dgate_ref[...] = dgate
        dup_ref[...] = dup
        dgate_s[k % 2] = dgate
        dup_s[k % 2] = dup

        @pl.when(k == nk)
        def _():
            is_ctx = _ctx_rows(tf, n_ctx, j)
            dh = acc_s[...]
            r, xn = rms_stats(x_ref[...])
            n3 = n3_ref[...]
            sc = _mod_row(mod_ref, is_ctx, 4)
            xg = xn * n3
            dyv = dh * (1.0 + sc)
            split_sum(is_ctx, dh, st_ref, 0)
            split_sum(is_ctx, dh * xg, st_ref, 1)
            st_ref[3:4, :] = colsum(dyv * xn)
            dx1_ref[...] = dx_ref[...] + rms_bwd(r, xn, dyv * n3)

    return pl.pallas_call(
        body, name="ffn_bwd", grid=(bl, nt, nk + 1),
        in_specs=[tile, tile, tile, modspec, vec, vec, wspec, wspec, wspec, wprev, wprev],
        out_specs=[tile, tile, tile, ftile, ftile, ftile, stat],
        out_shape=[jax.ShapeDtypeStruct((bl, s, D), F32), jax.ShapeDtypeStruct((bl, s, D), BF16),
                   jax.ShapeDtypeStruct((bl, s, D), BF16), jax.ShapeDtypeStruct((bl, s, D_FF), BF16),
                   jax.ShapeDtypeStruct((bl, s, D_FF), BF16), jax.ShapeDtypeStruct((bl, s, D_FF), BF16),
                   jax.ShapeDtypeStruct((bl, nt, 8, D), F32)],
        scratch_shapes=[pltpu.VMEM((tf, D), BF16), pltpu.VMEM((tf, D), BF16), pltpu.VMEM((tf, D), F32),
                        pltpu.VMEM((2, tf, FF_CHUNK), BF16), pltpu.VMEM((2, tf, FF_CHUNK), BF16)],
        compiler_params=_cp(("arbitrary", "arbitrary", "arbitrary")),
    )(dx2, x1, f, mod, n3, n4, wg_t, wu_t, wd, wg_t, wu_t)


def loss_head(xs, target, nc):
    bl, s, _ = xs.shape
    nt = s // TT

    def body(x_ref, t_ref, dx_ref, l_ref):
        j = pl.program_id(1)

        @pl.when(j < nc)
        def _():
            dx_ref[...] = jnp.zeros_like(dx_ref)
            l_ref[...] = jnp.zeros_like(l_ref)

        @pl.when(j >= nc)
        def _():
            e = x_ref[...] - t_ref[...]
            dx_ref[...] = e * (1.0 / D)
            tok = jnp.mean(e * e, axis=-1, keepdims=True)
            l_ref[...] = jnp.zeros_like(l_ref) + 0.5 * jnp.sum(tok, axis=0, keepdims=True)

    return pl.pallas_call(
        body, name="loss_head", grid=(bl, nt),
        in_specs=[_tile_spec(D), pl.BlockSpec((None, TT, D), lambda b, j: (b, jnp.maximum(j - nc, 0), 0))],
        out_specs=[_tile_spec(D), pl.BlockSpec((None, None, 8, 128), lambda b, j: (b, j, 0, 0))],
        out_shape=[jax.ShapeDtypeStruct((bl, s, D), F32), jax.ShapeDtypeStruct((bl, nt, 8, 128), F32)],
        compiler_params=_cp(("arbitrary", "arbitrary")),
    )(xs, target)


def tn_matmul(a, b, name):
    t, ka = a.shape
    n = b.shape[1]
    tk = ka if ka <= 1408 else ka // 2
    tt = 512 if t % 512 == 0 else 256
    nsteps = t // tt

    def body(a_ref, b_ref, o_ref, acc_s):
        @pl.when(pl.program_id(1) == 0)
        def _():
            acc_s[...] = jnp.zeros_like(acc_s)

        acc_s[...] += dot_tn(a_ref[...], b_ref[...])

        @pl.when(pl.program_id(1) == nsteps - 1)
        def _():
            o_ref[...] = acc_s[...].astype(BF16)

    return pl.pallas_call(
        body, name=name, grid=(ka // tk, nsteps),
        in_specs=[pl.BlockSpec((tt, tk), lambda i, s: (s, i)), pl.BlockSpec((tt, n), lambda i, s: (s, 0))],
        out_specs=pl.BlockSpec((tk, n), lambda i, s: (i, 0)),
        out_shape=jax.ShapeDtypeStruct((ka, n), BF16),
        scratch_shapes=[pltpu.VMEM((tk, n), F32)],
        compiler_params=_cp(("arbitrary", "arbitrary")),
    )(a, b)


def qmm(terms, name):
    r = terms[0][0].shape[0]
    rt = r // 2 if r % 16 == 0 and r >= 512 else r
    n = len(terms)

    def body(*refs):
        acc = None
        for k in range(n):
            y = dot_nn(refs[2 * k][...].astype(BF16), refs[2 * k + 1][...])
            acc = y if acc is None else acc + y
        refs[2 * n][...] = acc

    row = pl.BlockSpec((rt, QW), lambda q, i: (i, q))
    wspec = pl.BlockSpec((None, QW, QW), lambda q, i: (q, 0, 0))
    return pl.pallas_call(
        body, name=name, grid=(4, r // rt), in_specs=[row, wspec] * n, out_specs=row,
        out_shape=jax.ShapeDtypeStruct((r, ROW_W), F32),
        compiler_params=_cp(("arbitrary", "arbitrary")),
    )(*[x for term in terms for x in term])


def _same_group(rows, cols, row_group, col_group):
    ri = jnp.bitwise_and(lax.broadcasted_iota(jnp.int32, (rows, cols), 0) // row_group, GQ - 1)
    ci = jnp.bitwise_and(lax.broadcasted_iota(jnp.int32, (rows, cols), 1) // col_group, GQ - 1)
    return ri == ci


def _spread_matrix():
    m = np.zeros((2 * SSM_P, QW), np.float32)
    for reim in range(2):
        for g in range(GQ):
            for p in range(SSM_P):
                m[reim * SSM_P + p, reim * (QW // 2) + g * SSM_P + p] = 1.0
    return jnp.asarray(m, BF16)


def assemble_ts(v, name):
    def body(v_ref, f_ref, big_ref, bigt_ref):
        keep = _same_group(GQ * SSM_H, QW, SSM_H, SSM_P)
        for e in range(TC):
            hi, lo = split_bf16(v_ref[e])
            t = jnp.where(keep, dot_nn(hi, f_ref[...]) + dot_nn(lo, f_ref[...]), 0.0)
            big_ref[e * 128:(e + 1) * 128, :] = t.astype(BF16)
            bigt_ref[:, e * 128:(e + 1) * 128] = t.T.astype(BF16)

    return pl.pallas_call(
        body, name=name, grid=(4,),
        in_specs=[pl.BlockSpec((None, TC, 128, 128), lambda q: (q, 0, 0, 0)),
                  pl.BlockSpec((128, QW), lambda q: (0, 0))],
        out_specs=[pl.BlockSpec((None, QW, QW), lambda q: (q, 0, 0))] * 2,
        out_shape=[jax.ShapeDtypeStruct((4, QW, QW), BF16), jax.ShapeDtypeStruct((4, QW, QW), BF16)],
        compiler_params=_cp(("arbitrary",)),
    )(v, _spread_matrix())


def assemble_tt(lags, name):
    def body(l_ref, m_ref, mt_ref):
        blocks = [l_ref[n] for n in range(2 * TC - 1)]
        flipped = [b.T.astype(BF16) for b in blocks]
        blocks = [b.astype(BF16) for b in blocks]
        for s in range(TC):
            for t in range(TC):
                m_ref[s * 128:(s + 1) * 128, t * 128:(t + 1) * 128] = blocks[t - s + TC - 1]
                mt_ref[t * 128:(t + 1) * 128, s * 128:(s + 1) * 128] = flipped[t - s + TC - 1]

    return pl.pallas_call(
        body, name=name, grid=(4,),
        in_specs=[pl.BlockSpec((None, 2 * TC - 1, 128, 128), lambda q: (q, 0, 0, 0))],
        out_specs=[pl.BlockSpec((None, QW, QW), lambda q: (q, 0, 0))] * 2,
        out_shape=[jax.ShapeDtypeStruct((4, QW, QW), BF16)] * 2,
        compiler_params=_cp(("arbitrary",)),
    )(lags)


def _qtn_call(body, a, b, out_shape, out_block, extra, name):
    r = a.shape[0]
    col = pl.BlockSpec((r, QW), lambda q: (0, q))
    return pl.pallas_call(
        body, name=name, grid=(4,),
        in_specs=[col, col] + [pl.BlockSpec(x.shape, lambda q: (0, 0)) for x in extra],
        out_specs=pl.BlockSpec((None,) + out_block, lambda q: (q,) + (0,) * len(out_block)),
        out_shape=jax.ShapeDtypeStruct((4,) + out_block, F32),
        compiler_params=_cp(("arbitrary",)),
    )(a, b, *extra)


def qtn_ts(a, b, name):
    def body(a_ref, b_ref, f_ref, o_ref):
        full = dot_tn(a_ref[...].astype(BF16), b_ref[...].astype(BF16))
        keep = _same_group(GQ * SSM_H, QW, SSM_H, SSM_P)
        for e in range(TC):
            hi, lo = split_bf16(jnp.where(keep, full[e * 128:(e + 1) * 128, :], 0.0))
            o_ref[e] = dot_nt(hi, f_ref[...]) + dot_nt(lo, f_ref[...])

    return _qtn_call(body, a, b, None, (TC, 128, 128), [_spread_matrix()], name)


def qtn_tt(a, b, name):
    def body(a_ref, b_ref, o_ref):
        full = dot_tn(a_ref[...].astype(BF16), b_ref[...].astype(BF16))
        for lag in range(-(TC - 1), TC):
            acc = None
            for s in range(TC):
                t = s + lag
                if 0 <= t < TC:
                    blk = full[s * 128:(s + 1) * 128, t * 128:(t + 1) * 128]
                    acc = blk if acc is None else acc + blk
            o_ref[lag + TC - 1] = acc

    return _qtn_call(body, a, b, None, (2 * TC - 1, 128, 128), [], name)


def _scan_row(i, rb, ncr, reverse):
    if not reverse:
        return i
    return jnp.where(i < ncr, ncr - 1 - i, rb - 1 - (i - ncr))


def _swap_re_im(h):
    half = QW // 2
    return jnp.concatenate([h[:, q * QW + (1 - k) * half:q * QW + (2 - k) * half] for q in range(4) for k in range(2)],
                           axis=1)


def chunk_scan(xs, lam_ab, ncr, reverse, name):
    bl, rb, _ = xs.shape

    def body(x_ref, l_ref, hp_ref):
        la, lb = l_ref[0:1, :], l_ref[1:2, :]

        def step(i, h):
            row = _scan_row(i, rb, ncr, reverse)
            hp_ref[pl.ds(row, 1), :] = h
            return la * h + lb * _swap_re_im(h) + x_ref[pl.ds(row, 1), :]

        lax.fori_loop(0, rb, step, jnp.zeros((1, ROW_W), F32))

    blk = pl.BlockSpec((None, rb, ROW_W), lambda b: (b, 0, 0))
    return pl.pallas_call(
        body, name=name, grid=(bl,),
        in_specs=[blk, pl.BlockSpec((8, ROW_W), lambda b: (0, 0))], out_specs=blk,
        out_shape=jax.ShapeDtypeStruct((bl, rb, ROW_W), F32),
        compiler_params=_cp(("arbitrary",)),
    )(xs, lam_ab)


def chunk_scan_bwd(dhp, hp, lam_ab, ncr, reverse, name):
    bl, rb, _ = dhp.shape

    def body(d_ref, hp_ref, l_ref, g_ref, dl_ref):
        la, lb = l_ref[0:1, :], l_ref[1:2, :]

        dl_ref[...] = jnp.zeros_like(dl_ref)

        def step(n, g):
            row = _scan_row(rb - 1 - n, rb, ncr, reverse)
            g_ref[pl.ds(row, 1), :] = g
            pv = hp_ref[pl.ds(row, 1), :]
            dl_ref[0:1, :] += g * pv
            dl_ref[1:2, :] += g * _swap_re_im(pv)
            return d_ref[pl.ds(row, 1), :] + la * g + _swap_re_im(lb * g)

        lax.fori_loop(0, rb, step, jnp.zeros((1, ROW_W), F32))

    blk = pl.BlockSpec((None, rb, ROW_W), lambda b: (b, 0, 0))
    return pl.pallas_call(
        body, name=name, grid=(bl,),
        in_specs=[blk, blk, pl.BlockSpec((8, ROW_W), lambda b: (0, 0))],
        out_specs=[blk, pl.BlockSpec((None, 8, ROW_W), lambda b: (b, 0, 0))],
        out_shape=[jax.ShapeDtypeStruct((bl, rb, ROW_W), F32), jax.ShapeDtypeStruct((bl, 8, ROW_W), F32)],
        compiler_params=_cp(("arbitrary",)),
    )(dhp, hp, lam_ab)


def _quarter_rows(v):
    e = v.shape[0]
    return v.reshape(e, 4, 8, SSM_P, SSM_H).transpose(0, 1, 2, 4, 3).reshape(e, 4, 8 * SSM_H, SSM_P)


def _token_state_map(vr, vi):
    return jnp.concatenate([_quarter_rows(vr), _quarter_rows(vi)], axis=-1).transpose(1, 0, 2, 3)


def ssm_build(lam_re, lam_im, log_dt, b_re, b_im, c_re, c_im, d):
    dt = jnp.exp(log_dt)[..., None]
    mag = jnp.exp(lam_re * dt)
    ang = lam_im * dt
    lr, li = mag * jnp.cos(ang), mag * jnp.sin(ang)
    den = lam_re * lam_re + lam_im * lam_im
    nr = lr - 1.0
    fr = (nr * lam_re + li * lam_im) / den
    fi = (li * lam_re - nr * lam_im) / den
    bbr = fr[..., None] * b_re - fi[..., None] * b_im
    bbi = fr[..., None] * b_im + fi[..., None] * b_re
    pr, pi = [jnp.ones_like(lr)], [jnp.zeros_like(lr)]
    for _ in range(TC):
        pr, pi = pr + [pr[-1] * lr - pi[-1] * li], pi + [pr[-1] * li + pi[-1] * lr]
    pr, pi = jnp.stack(pr), jnp.stack(pi)
    clr = c_re[None] * pr[:, :, :, None, :] - c_im[None] * pi[:, :, :, None, :]
    cli = c_re[None] * pi[:, :, :, None, :] + c_im[None] * pr[:, :, :, None, :]
    same_group = jnp.asarray(np.kron(np.eye(8), np.ones((SSM_H, SSM_H))), F32)
    ein = functools.partial(jnp.einsum, precision=HI)

    out, lag_blocks = {}, {}
    for k, name in ((0, "f"), (1, "r")):
        ar, ai = _quarter_rows(bbr[k][None])[0], _quarter_rows(bbi[k][None])[0]
        cr = clr[:TC, k].reshape(TC, 4, 8 * SSM_H, SSM_P)
        ci = cli[:TC, k].reshape(TC, 4, 8 * SSM_H, SSM_P)
        lag_blocks[k] = (ein('qap,nqbp->nqab', ar, cr) - ein('qap,nqbp->nqab', ai, ci)) * same_group
        es = [TC - 1 - s for s in range(TC)] if k == 0 else list(range(TC))
        sr = jnp.stack([pr[e, k][:, :, None] * bbr[k] - pi[e, k][:, :, None] * bbi[k] for e in es])
        si = jnp.stack([pr[e, k][:, :, None] * bbi[k] + pi[e, k][:, :, None] * bbr[k] for e in es])
        out["bs_" + name] = _token_state_map(sr, si)
        et = [t + 1 for t in range(TC)] if k == 0 else [TC - t for t in range(TC)]
        crt = jnp.stack([jnp.swapaxes(clr[e, k], 1, 2) for e in et])
        cit = jnp.stack([-jnp.swapaxes(cli[e, k], 1, 2) for e in et])
        out["cst_" + name] = _token_state_map(crt, cit)
        l8r, l8i = pr[TC, k].reshape(4, 1, QW // 2), pi[TC, k].reshape(4, 1, QW // 2)
        la = jnp.concatenate([l8r, l8r], axis=1).reshape(1, ROW_W)
        lb = jnp.concatenate([-l8i, l8i], axis=1).reshape(1, ROW_W)
        out["lam_" + name] = jnp.concatenate([la, lb, jnp.zeros((6, ROW_W), F32)], axis=0)
    skip = jnp.eye(8 * SSM_H, dtype=F32)[None] * d.reshape(4, 1, 8 * SSM_H)
    center = lag_blocks[0][0] + lag_blocks[1][0] + skip
    lags = [lag_blocks[1][n] for n in range(TC - 1, 0, -1)] + [center] + [lag_blocks[0][n] for n in range(1, TC)]
    out["lags"] = jnp.stack(lags, axis=1)
    return out


def ssm_operators(mats, tag):
    ops = {}
    ops["m"], ops["mt"] = assemble_tt(mats["lags"], "ssm_map_intra" + tag)
    for dname in ("f", "r"):
        ops["bs_" + dname], ops["bst_" + dname] = assemble_ts(mats["bs_" + dname], f"ssm_map_state_in_{dname}{tag}")
        ops["cst_" + dname], ops["cs_" + dname] = assemble_ts(mats["cst_" + dname], f"ssm_map_readout_{dname}{tag}")
    return ops


def ssm_forward(u3, mats, ops, ncr):
    bl, rb, _ = u3.shape
    u = u3.reshape(bl * rb, ROW_W)
    hps, terms = {}, [(u, ops["m"])]
    for dname, rev in (("f", False), ("r", True)):
        xs = qmm([(u, ops["bs_" + dname])], "ssm_state_in_" + dname)
        hp = chunk_scan(xs.reshape(bl, rb, ROW_W), mats["lam_" + dname], ncr, rev, "ssm_scan_" + dname)
        hps[dname] = hp.reshape(bl * rb, ROW_W)
        terms.append((hps[dname], ops["cs_" + dname]))
    return qmm(terms, "ssm_output").reshape(bl, rb, ROW_W), hps


def ssm_backward(dy3, u3, hps, mats, ops, ncr):
    bl, rb, _ = u3.shape
    u = u3.reshape(bl * rb, ROW_W)
    dyr = dy3.reshape(bl * rb, ROW_W)
    cot = {"lags": qtn_tt(u, dyr, "ssm_d_intra")}
    terms = [(dyr, ops["mt"])]
    for dname, rev in (("f", False), ("r", True)):
        dhp = qmm([(dyr, ops["cst_" + dname])], "ssm_dstate_" + dname)
        g, dl = chunk_scan_bwd(dhp.reshape(bl, rb, ROW_W), hps[dname].reshape(bl, rb, ROW_W), mats["lam_" + dname],
                               ncr, rev, "ssm_scan_bwd_" + dname)
        g = g.reshape(bl * rb, ROW_W)
        cot["lam_" + dname] = jnp.sum(dl, axis=0)
        cot["bs_" + dname] = qtn_ts(u, g, "ssm_d_state_in_" + dname)
        cot["cst_" + dname] = qtn_ts(dyr, hps[dname], "ssm_d_readout_" + dname)
        terms.append((g, ops["bst_" + dname]))
    return qmm(terms, "ssm_input_grad").reshape(bl, rb, ROW_W), cot


def mod_forward(act, w_mod, b_cols):
    nl, _, wc = w_mod.shape
    r = act.shape[0]

    def body(a_ref, w_ref, b_ref, o_ref):
        o_ref[...] = dot_nn(a_ref[...].astype(BF16), w_ref[...].astype(BF16)) + b_ref[...]

    return pl.pallas_call(
        body, name="mod_forward", grid=(nl,),
        in_specs=[pl.BlockSpec((r, D), lambda l: (0, 0)), pl.BlockSpec((None, D, wc), lambda l: (l, 0, 0)),
                  pl.BlockSpec((None, 1, wc), lambda l: (l, 0, 0))],
        out_specs=pl.BlockSpec((None, r, wc), lambda l: (l, 0, 0)),
        out_shape=jax.ShapeDtypeStruct((nl, r, wc), F32),
        compiler_params=_cp(("arbitrary",)),
    )(act, w_mod, b_cols)


def mod_backward(act, dmod, dctx, w_mod):
    nl, _, wc = w_mod.shape
    r = act.shape[0]

    def body(a_ref, d_ref, c_ref, w_ref, gw_ref, gc_ref):
        gw_ref[...] = dot_tn(a_ref[...].astype(BF16), d_ref[...].astype(BF16))
        gc_ref[...] = dot_nt(c_ref[...].astype(BF16), w_ref[...].astype(BF16))

    return pl.pallas_call(
        body, name="mod_backward", grid=(nl,),
        in_specs=[pl.BlockSpec((r, D), lambda l: (0, 0)), pl.BlockSpec((None, r, wc), lambda l: (l, 0, 0)),
                  pl.BlockSpec((None, 8, wc), lambda l: (l, 0, 0)), pl.BlockSpec((None, D, wc), lambda l: (l, 0, 0))],
        out_specs=[pl.BlockSpec((None, D, wc), lambda l: (l, 0, 0)), pl.BlockSpec((None, 8, D), lambda l: (l, 0, 0))],
        out_shape=[jax.ShapeDtypeStruct((nl, D, wc), F32), jax.ShapeDtypeStruct((nl, 8, D), F32)],
        compiler_params=_cp(("arbitrary",)),
    )(act, dmod, dctx, w_mod)


def _place():
    return lax.axis_index("x"), lax.axis_index("y"), lax.axis_index("c")


def all_gather_rows(arrs, name):
    n = len(arrs)
    rs = [a.shape[1] for a in arrs]

    def body(*refs):
        x_refs, o_refs = refs[:n], refs[n:2 * n]
        send_sems, recv_sems, local_sems = refs[2 * n:]
        x, y, c = _place()
        me, sibling = (x, y, c), (x, y, 1 - c)
        chips = [(1 - x, y), (x, 1 - y), (1 - x, 1 - y)]

        def rows(a, px, py, pc):
            return o_refs[a].at[:, pl.ds((4 * px + 2 * py + pc) * rs[a], rs[a]), :]

        def copy(a, k, block, to, src=None):
            return pltpu.make_async_remote_copy(
                src_ref=rows(a, *block) if src is None else src, dst_ref=rows(a, *block),
                send_sem=send_sems.at[a, k], recv_sem=recv_sems.at[a, k], device_id=to, device_id_type=MESH)

        mine = [pltpu.make_async_copy(x_refs[a], rows(a, *me), local_sems.at[a]) for a in range(n)]
        for cp in mine:
            cp.start()
        first = []
        for a in range(n):
            first.append(copy(a, 0, me, sibling, src=x_refs[a]))
            first += [copy(a, 1 + j, me, (*chip, c), src=x_refs[a]) for j, chip in enumerate(chips)]
        for cp in first:
            cp.start()
        passed = []
        for j, chip in enumerate(chips):
            for a in range(n):
                copy(a, 1 + j, (*chip, c), me).wait_recv()
                fwd = copy(a, 4 + j, (*chip, c), sibling)
                fwd.start()
                passed.append(fwd)
        for a in range(n):
            copy(a, 0, sibling, me).wait_recv()
            for j, chip in enumerate(chips):
                copy(a, 4 + j, (*chip, 1 - c), me).wait_recv()
        for cp in first + passed:
            cp.wait_send()
        for cp in mine:
            cp.wait()

    any_spec = pl.BlockSpec(memory_space=pl.ANY)
    return pl.pallas_call(
        body, name=name,
        in_specs=[any_spec] * n, out_specs=[any_spec] * n,
        out_shape=[jax.ShapeDtypeStruct((a.shape[0], N_DEV * a.shape[1], a.shape[2]), a.dtype) for a in arrs],
        scratch_shapes=[pltpu.SemaphoreType.DMA((n, 7)), pltpu.SemaphoreType.DMA((n, 7)), pltpu.SemaphoreType.DMA((n,))],
    )(*arrs)


def all_to_all_rows(arrs, name):
    n = len(arrs)
    rs = [a.shape[1] // N_DEV for a in arrs]
    flips = [(fx, fy, fc) for fx in (0, 1) for fy in (0, 1) for fc in (0, 1)][1:]

    def body(*refs):
        x_refs, o_refs = refs[:n], refs[n:2 * n]
        send_sems, recv_sems, local_sems = refs[2 * n:]
        x, y, c = _place()
        my_idx = 4 * x + 2 * y + c

        def block(a, idx):
            return x_refs[a].at[:, pl.ds(idx * rs[a], rs[a]), :]

        mine = [pltpu.make_async_copy(block(a, my_idx), o_refs[a].at[my_idx], local_sems.at[a]) for a in range(n)]
        for cp in mine:
            cp.start()
        sends = []
        for k, (fx, fy, fc) in enumerate(flips):
            px = 1 - x if fx else x
            py = 1 - y if fy else y
            pc = 1 - c if fc else c
            p_idx = 4 * px + 2 * py + pc
            for a in range(n):
                sends.append(pltpu.make_async_remote_copy(
                    src_ref=block(a, p_idx), dst_ref=o_refs[a].at[my_idx], send_sem=send_sems.at[a, k],
                    recv_sem=recv_sems.at[a, k], device_id=(px, py, pc), device_id_type=MESH))
        for cp in sends:
            cp.start()
        for k, (fx, fy, fc) in enumerate(flips):
            px = 1 - x if fx else x
            py = 1 - y if fy else y
            pc = 1 - c if fc else c
            p_idx = 4 * px + 2 * py + pc
            for a in range(n):
                pltpu.make_async_remote_copy(
                    src_ref=block(a, p_idx), dst_ref=o_refs[a].at[p_idx], send_sem=send_sems.at[a, k],
                    recv_sem=recv_sems.at[a, k], device_id=(px, py, pc), device_id_type=MESH).wait_recv()
        for cp in sends:
            cp.wait_send()
        for cp in mine:
            cp.wait()

    any_spec = pl.BlockSpec(memory_space=pl.ANY)
    return pl.pallas_call(
        body, name=name,
        in_specs=[any_spec] * n, out_specs=[any_spec] * n,
        out_shape=[jax.ShapeDtypeStruct((N_DEV, a.shape[0], r, a.shape[2]), a.dtype) for a, r in zip(arrs, rs)],
        scratch_shapes=[pltpu.SemaphoreType.DMA((n, 7)), pltpu.SemaphoreType.DMA((n, 7)), pltpu.SemaphoreType.DMA((n,))],
    )(*arrs)


def _peers():
    x, y, c = _place()
    out = []
    for fx in (0, 1):
        for fy in (0, 1):
            for fc in (0, 1):
                if fx or fy or fc:
                    px, py, pc = (1 - x if fx else x), (1 - y if fy else y), (1 - c if fc else c)
                    out.append(((px, py, pc), 4 * px + 2 * py + pc))
    return out, 4 * x + 2 * y + c


def _split_call(body, name, ins, n_sem_out, thru, extra_out_shape, extra_out_specs, sem_ins=(), after=None):
    hbm = pl.BlockSpec(memory_space=pltpu.HBM)
    sem = pl.BlockSpec(memory_space=pltpu.SEMAPHORE)
    n_thru = len(thru)
    tail_in = [sem] * len(sem_ins) + ([pl.BlockSpec(memory_space=pl.ANY)] if after is not None else [])
    return pl.pallas_call(
        body, name=name,
        out_shape=tuple(n_sem_out) + tuple(pltpu.HBM(a.shape, a.dtype) for a in thru) + tuple(extra_out_shape),
        in_specs=[hbm] * n_thru + tail_in,
        out_specs=(sem,) * len(n_sem_out) + (hbm,) * n_thru + tuple(extra_out_specs),
        input_output_aliases={i: i + len(n_sem_out) for i in range(n_thru)},
        compiler_params=pltpu.CompilerParams(has_side_effects=pltpu.SideEffectType.DATAFLOW_SIDE_EFFECTING),
    )(*ins, *sem_ins, *([after] if after is not None else []))


def gather_start(shards, after, name):
    n = len(shards)
    rs = [a.shape[1] for a in shards]
    lands = [lax.empty((a.shape[0], N_DEV * a.shape[1], a.shape[2]), a.dtype) for a in shards]

    def body(*refs):
        x_refs, land_refs = refs[:n], refs[n:2 * n]
        send_sems, recv_sems = refs[2 * n + 1], refs[2 * n + 2]
        peers, my_idx = _peers()
        for k, (peer, _) in enumerate(peers):
            for a in range(n):
                pltpu.make_async_remote_copy(
                    src_ref=x_refs[a], dst_ref=land_refs[a].at[:, pl.ds(my_idx * rs[a], rs[a]), :],
                    send_sem=send_sems.at[a * 7 + k], recv_sem=recv_sems.at[a * 7 + k], device_id=peer,
                    device_id_type=MESH).start()
        refs[-1][...] = jnp.zeros_like(refs[-1])

    ins = [pltpu.with_memory_space_constraint(a, pltpu.HBM) for a in list(shards) + lands]
    outs = _split_call(body, name, ins, [pltpu.SemaphoreType.DMA((n * 7,))] * 2, ins,
                       [jax.ShapeDtypeStruct((8, 128), F32)], [pl.BlockSpec(memory_space=pltpu.VMEM)], after=after)
    return outs[0], outs[1], list(outs[2:2 + n]), list(outs[2 + n:2 + 2 * n]), outs[-1]


def gather_wait(send_sems, recv_sems, shards, lands, after, name):
    n = len(shards)
    rs = [a.shape[1] for a in shards]

    def body(*refs):
        x_refs, land_refs = refs[:n], refs[n:2 * n]
        s_sems, r_sems = refs[2 * n], refs[2 * n + 1]
        peers, _ = _peers()
        for k, (peer, p_idx) in enumerate(peers):
            for a in range(n):
                copy = pltpu.make_async_remote_copy(
                    src_ref=x_refs[a], dst_ref=land_refs[a].at[:, pl.ds(p_idx * rs[a], rs[a]), :],
                    send_sem=s_sems.at[a * 7 + k], recv_sem=r_sems.at[a * 7 + k], device_id=peer, device_id_type=MESH)
                copy.wait_send()
                copy.wait_recv()

    outs = _split_call(body, name, list(shards) + list(lands), [], list(shards) + list(lands), [], [],
                       sem_ins=(send_sems, recv_sems), after=after)
    my_idx = 4 * lax.axis_index("x") + 2 * lax.axis_index("y") + lax.axis_index("c")
    return [lax.dynamic_update_slice_in_dim(z, s, my_idx * r, axis=1) for z, s, r in zip(outs[n:], outs[:n], rs)]


def scatter_start(arrs, name):
    n = len(arrs)
    rs = [a.shape[1] // N_DEV for a in arrs]
    lands = [lax.empty((N_DEV, a.shape[0], r, a.shape[2]), a.dtype) for a, r in zip(arrs, rs)]

    def body(*refs):
        x_refs, land_refs = refs[:n], refs[n:2 * n]
        send_sems, recv_sems = refs[2 * n], refs[2 * n + 1]
        token = refs[-1]
        peers, my_idx = _peers()
        for k, (peer, p_idx) in enumerate(peers):
            for a in range(n):
                pltpu.make_async_remote_copy(
                    src_ref=x_refs[a].at[:, pl.ds(p_idx * rs[a], rs[a]), :], dst_ref=land_refs[a].at[my_idx],
                    send_sem=send_sems.at[a * 7 + k], recv_sem=recv_sems.at[a * 7 + k], device_id=peer,
                    device_id_type=MESH).start()
        token[...] = jnp.zeros_like(token)

    hbm = pl.BlockSpec(memory_space=pltpu.HBM)
    sem = pl.BlockSpec(memory_space=pltpu.SEMAPHORE)
    outs = pl.pallas_call(
        body, name=name,
        out_shape=(pltpu.SemaphoreType.DMA((n * 7,)), pltpu.SemaphoreType.DMA((n * 7,)))
        + tuple(pltpu.HBM(a.shape, a.dtype) for a in arrs) + tuple(pltpu.HBM(z.shape, z.dtype) for z in lands)
        + (jax.ShapeDtypeStruct((8, 128), F32),),
        in_specs=[hbm] * (2 * n),
        out_specs=(sem, sem) + (hbm,) * (2 * n) + (pl.BlockSpec(memory_space=pltpu.VMEM),),
        input_output_aliases={i: i + 2 for i in range(2 * n)},
        compiler_params=pltpu.CompilerParams(has_side_effects=pltpu.SideEffectType.DATAFLOW_SIDE_EFFECTING),
    )(*[pltpu.with_memory_space_constraint(a, pltpu.HBM) for a in arrs],
      *[pltpu.with_memory_space_constraint(z, pltpu.HBM) for z in lands])
    return outs[0], outs[1], list(outs[2:2 + n]), list(outs[2 + n:2 + 2 * n]), outs[-1]


def scatter_wait(send_sems, recv_sems, arrs, lands, after, name):
    n = len(arrs)
    rs = [a.shape[1] // N_DEV for a in arrs]

    def body(*refs):
        x_refs, land_refs = refs[:n], refs[n:2 * n]
        s_sems, r_sems = refs[2 * n], refs[2 * n + 1]
        peers, my_idx = _peers()
        for k, (peer, p_idx) in enumerate(peers):
            for a in range(n):
                copy = pltpu.make_async_remote_copy(
                    src_ref=x_refs[a].at[:, pl.ds(p_idx * rs[a], rs[a]), :], dst_ref=land_refs[a].at[p_idx],
                    send_sem=s_sems.at[a * 7 + k], recv_sem=r_sems.at[a * 7 + k], device_id=peer, device_id_type=MESH)
                copy.wait_send()
                copy.wait_recv()

    hbm = pl.BlockSpec(memory_space=pltpu.HBM)
    sem = pl.BlockSpec(memory_space=pltpu.SEMAPHORE)
    outs = pl.pallas_call(
        body, name=name,
        out_shape=tuple(pltpu.HBM(a.shape, a.dtype) for a in arrs) + tuple(pltpu.HBM(z.shape, z.dtype) for z in lands),
        in_specs=[hbm] * (2 * n) + [sem, sem, pl.BlockSpec(memory_space=pl.ANY)],
        out_specs=(hbm,) * (2 * n),
        input_output_aliases={i: i for i in range(2 * n)},
        compiler_params=pltpu.CompilerParams(has_side_effects=pltpu.SideEffectType.DATAFLOW_SIDE_EFFECTING),
    )(*arrs, *lands, send_sems, recv_sems, after)
    return list(outs[:n]), list(outs[n:])


def _row_tile(rows, cap):
    best = None
    for t in range(16, min(rows, cap) + 1, 16):
        if rows % t == 0:
            best = t
    return rows if best is None else best


def adamw(w, gparts, m, v, name):
    per_layer = isinstance(gparts, (list, tuple))
    glist = list(gparts) if per_layer else [gparts]
    n, _, ra, cb = glist[0].shape
    nl = w.shape[0]
    ng = len(glist)
    ta = _row_tile(ra, max(8, (1 << 19) // (cb * n)))

    def slot_sum(g_ref):
        g = g_ref[0].astype(F32)
        for p in range(1, n):
            g = g + g_ref[p].astype(F32)
        return g

    def body(*refs):
        w_ref, g_refs = refs[0], refs[1:1 + ng]
        m_ref, v_ref, go_ref, d_ref, mo_ref, vo_ref = refs[1 + ng:]
        g = slot_sum(g_refs[0])
        for layer in range(1, ng):
            g = jnp.where(pl.program_id(0) == layer, slot_sum(g_refs[layer]), g)
        mn = ADAM_B1 * m_ref[...] + (1.0 - ADAM_B1) * g
        vn = ADAM_B2 * v_ref[...] + (1.0 - ADAM_B2) * jnp.square(g)
        m_hat = mn / (1.0 - ADAM_B1 ** ADAM_STEP)
        v_hat = vn / (1.0 - ADAM_B2 ** ADAM_STEP)
        go_ref[...] = g
        d_ref[...] = -ADAM_LR * (m_hat / (jnp.sqrt(v_hat) + ADAM_EPS) + ADAM_WD * w_ref[...])
        mo_ref[...] = mn
        vo_ref[...] = vn

    blk = pl.BlockSpec((None, ta, cb), lambda l, i: (l, i, 0))
    if per_layer:
        gblk = pl.BlockSpec((n, None, ta, cb), lambda l, i: (0, 0, i, 0))
    else:
        gblk = pl.BlockSpec((n, None, ta, cb), lambda l, i: (0, l, i, 0))
    shp = jax.ShapeDtypeStruct((nl, ra, cb), F32)
    return pl.pallas_call(
        body, name=name, grid=(nl, ra // ta),
        in_specs=[blk] + [gblk] * ng + [blk, blk], out_specs=[blk] * 4, out_shape=[shp] * 4,
        compiler_params=_cp(("arbitrary", "arbitrary")),
    )(w, *glist, m, v)


def _sincos_2d(rows, cols, dim):
    quarter = dim // 4
    omega = 1.0 / (10000.0 ** (jnp.arange(quarter, dtype=F32) / quarter))
    r = jnp.arange(rows, dtype=F32)[:, None] * omega
    cc = jnp.arange(cols, dtype=F32)[:, None] * omega
    er = jnp.concatenate([jnp.sin(r), jnp.cos(r)], axis=-1)
    ec = jnp.concatenate([jnp.sin(cc), jnp.cos(cc)], axis=-1)
    pe = jnp.concatenate([jnp.broadcast_to(er[:, None, :], (rows, cols, dim // 2)),
                          jnp.broadcast_to(ec[None, :, :], (rows, cols, dim // 2))], axis=-1)
    return pe.reshape(rows * cols, dim)


def _pool_constants():
    nw = len(POOL_WINDOWS)
    band = np.zeros((2, nw, TT, TT), np.float32)
    icnt = np.zeros((2, TT, C_W), np.float32)
    for kind, n in ((0, TT), (1, GRID_W)):
        for i, w in enumerate(POOL_WINDOWS):
            for t in range(TT):
                base, tl = (t // n) * n, t % n
                lo = min(max(tl - w // 2, 0), n)
                hi = min(max(tl - w // 2 + w, 0), n)
                band[kind, i, t, base + lo:base + hi] = 1.0
                icnt[kind, t, i * (C_W // nw):(i + 1) * (C_W // nw)] = 1.0 / (hi - lo)
    return jnp.asarray(band, BF16), jnp.asarray(icnt, F32)


def _block_diag(blocks):
    n, a, _ = blocks.shape
    return jnp.einsum('gab,gh->gahb', blocks, jnp.eye(n, dtype=F32), precision=HI).reshape(n * a, n * a)


def _block_diag_parts(mat, n):
    a = mat.shape[0] // n
    m4 = mat.reshape(n, a, n, a)
    return jnp.stack([m4[g, :, g, :] for g in range(n)])


_SMALL = ("c_ctx", "b_mod", "norm_mix_pre", "norm_mix_post", "norm_ffn_pre", "norm_ffn_post", "sgu_w", "sgu_b",
          "ssm_lam_re", "ssm_lam_im", "ssm_log_dt", "ssm_b_re", "ssm_b_im", "ssm_c_re", "ssm_c_im", "ssm_d",
          "glu_b", "pool_w", "pool_scale")
_WEIGHTS = ("c_ctx", "w_mod", "b_mod", "norm_mix_pre", "norm_mix_post", "norm_ffn_pre", "norm_ffn_post", "w_in", "w_out",
            "sgu_w", "sgu_b", "ssm_lam_re", "ssm_lam_im", "ssm_log_dt", "ssm_b_re", "ssm_b_im", "ssm_c_re", "ssm_c_im",
            "ssm_d", "glu_w", "glu_b", "pool_w", "pool_scale", "ffn_w_gate", "ffn_w_up", "ffn_w_down")


def _pack_rows(a):
    flat = a.reshape(-1)
    rows = -(-flat.shape[0] // D)
    rows8 = -(-rows // 8) * 8
    return jnp.pad(flat, (0, rows8 * D - flat.shape[0])).reshape(rows8, D)


def _pack(tree):
    packed = jnp.concatenate([_pack_rows(tree[k]) for k in _SMALL], axis=0)
    return jnp.pad(packed, ((0, -packed.shape[0] % 64), (0, 0)))


def _unpack(packed, like):
    out, at = {}, 0
    for k in _SMALL:
        size = int(np.prod(like[k].shape))
        rows8 = -(-(-(-size // D)) // 8) * 8
        out[k] = packed[at:at + rows8].reshape(-1)[:size].reshape(like[k].shape)
        at += rows8
    return out


def kernel(x, c, ctx, c_ctx, w_mod, b_mod, norm_mix_pre, norm_mix_post, norm_ffn_pre, norm_ffn_post, w_in, w_out, sgu_w, sgu_b, ssm_lam_re, ssm_lam_im, ssm_log_dt, ssm_b_re, ssm_b_im, ssm_c_re, ssm_c_im, ssm_d, glu_w, glu_b, pool_w, pool_scale, ffn_w_gate, ffn_w_up, ffn_w_down, loss_target, m_c_ctx, m_w_mod, m_b_mod, m_norm_mix_pre, m_norm_mix_post, m_norm_ffn_pre, m_norm_ffn_post, m_w_in, m_w_out, m_sgu_w, m_sgu_b, m_ssm_lam_re, m_ssm_lam_im, m_ssm_log_dt, m_ssm_b_re, m_ssm_b_im, m_ssm_c_re, m_ssm_c_im, m_ssm_d, m_glu_w, m_glu_b, m_pool_w, m_pool_scale, m_ffn_w_gate, m_ffn_w_up, m_ffn_w_down, v_c_ctx, v_w_mod, v_b_mod, v_norm_mix_pre, v_norm_mix_post, v_norm_ffn_pre, v_norm_ffn_post, v_w_in, v_w_out, v_sgu_w, v_sgu_b, v_ssm_lam_re, v_ssm_lam_im, v_ssm_log_dt, v_ssm_b_re, v_ssm_b_im, v_ssm_c_re, v_ssm_c_im, v_ssm_d, v_glu_w, v_glu_b, v_pool_w, v_pool_scale, v_ffn_w_gate, v_ffn_w_up, v_ffn_w_down):
    wts = dict(c_ctx=c_ctx, w_mod=w_mod, b_mod=b_mod, norm_mix_pre=norm_mix_pre, norm_mix_post=norm_mix_post,
               norm_ffn_pre=norm_ffn_pre, norm_ffn_post=norm_ffn_post, w_in=w_in, w_out=w_out, sgu_w=sgu_w, sgu_b=sgu_b,
               ssm_lam_re=ssm_lam_re, ssm_lam_im=ssm_lam_im, ssm_log_dt=ssm_log_dt, ssm_b_re=ssm_b_re, ssm_b_im=ssm_b_im,
               ssm_c_re=ssm_c_re, ssm_c_im=ssm_c_im, ssm_d=ssm_d, glu_w=glu_w, glu_b=glu_b, pool_w=pool_w,
               pool_scale=pool_scale, ffn_w_gate=ffn_w_gate, ffn_w_up=ffn_w_up, ffn_w_down=ffn_w_down)
    mom_m = dict(c_ctx=m_c_ctx, w_mod=m_w_mod, b_mod=m_b_mod, norm_mix_pre=m_norm_mix_pre, norm_mix_post=m_norm_mix_post,
                 norm_ffn_pre=m_norm_ffn_pre, norm_ffn_post=m_norm_ffn_post, w_in=m_w_in, w_out=m_w_out, sgu_w=m_sgu_w,
                 sgu_b=m_sgu_b, ssm_lam_re=m_ssm_lam_re, ssm_lam_im=m_ssm_lam_im, ssm_log_dt=m_ssm_log_dt,
                 ssm_b_re=m_ssm_b_re, ssm_b_im=m_ssm_b_im, ssm_c_re=m_ssm_c_re, ssm_c_im=m_ssm_c_im, ssm_d=m_ssm_d,
                 glu_w=m_glu_w, glu_b=m_glu_b, pool_w=m_pool_w, pool_scale=m_pool_scale, ffn_w_gate=m_ffn_w_gate,
                 ffn_w_up=m_ffn_w_up, ffn_w_down=m_ffn_w_down)
    mom_v = dict(c_ctx=v_c_ctx, w_mod=v_w_mod, b_mod=v_b_mod, norm_mix_pre=v_norm_mix_pre, norm_mix_post=v_norm_mix_post,
                 norm_ffn_pre=v_norm_ffn_pre, norm_ffn_post=v_norm_ffn_post, w_in=v_w_in, w_out=v_w_out, sgu_w=v_sgu_w,
                 sgu_b=v_sgu_b, ssm_lam_re=v_ssm_lam_re, ssm_lam_im=v_ssm_lam_im, ssm_log_dt=v_ssm_log_dt,
                 ssm_b_re=v_ssm_b_re, ssm_b_im=v_ssm_b_im, ssm_c_re=v_ssm_c_re, ssm_c_im=v_ssm_c_im, ssm_d=v_ssm_d,
                 glu_w=v_glu_w, glu_b=v_glu_b, pool_w=v_pool_w, pool_scale=v_pool_scale, ffn_w_gate=v_ffn_w_gate,
                 ffn_w_up=v_ffn_w_up, ffn_w_down=v_ffn_w_down)

    bl, seq, _ = x.shape
    n_ctx = ctx.shape[1]
    assert n_ctx == TT and seq % TT == 0 and seq % GRID_W == 0
    depth = w_in.shape[0]
    nc = n_ctx // TT
    ncr = n_ctx // TC
    s_all = n_ctx + seq
    nt = s_all // TT
    t_all = bl * s_all
    n_batch = bl * N_DEV
    my_idx = 4 * lax.axis_index("x") + 2 * lax.axis_index("y") + lax.axis_index("c")
    wc = w_mod.shape[2]

    c_rows = jnp.pad(c, ((0, 8 - bl), (0, 0))) if bl < 8 else c
    rc = c_rows.shape[0]
    (c_all,) = all_gather_rows([c_rows[None]], "gather_c")
    c_all = c_all[0].reshape(N_DEV, rc, D)[:, :bl].reshape(n_batch, D)
    r_act = -(-(n_batch + 1) // 16) * 16
    pre_act = jnp.concatenate([c_all, c_ctx[None, :], jnp.zeros((r_act - n_batch - 1, D), F32)], axis=0)
    act = jax.nn.silu(pre_act)
    b_cols = lax.dynamic_slice_in_dim(b_mod, my_idx * wc, wc, axis=1)[:, None, :]
    mod_cols = mod_forward(act, w_mod, b_cols)
    (mod_all,) = all_gather_rows([mod_cols], "gather_mod")
    mod_all = mod_all.reshape(depth, N_DEV, r_act, wc).transpose(0, 2, 1, 3).reshape(depth, r_act, 6, D)
    mod_lat = lax.dynamic_slice_in_dim(mod_all, my_idx * bl, bl, axis=1)
    mod_ctx = jnp.broadcast_to(mod_all[:, n_batch:n_batch + 1], (depth, bl, 6, D))
    mods = jnp.pad(jnp.stack([mod_ctx, mod_lat], axis=2), ((0, 0), (0, 0), (0, 0), (0, 2), (0, 0)))

    tr = lambda a: jnp.swapaxes(a, 1, 2).astype(BF16)
    shards = dict(w_in=tr(w_in), w_out=w_out.astype(BF16), glu_w=glu_w.astype(BF16), gate=tr(ffn_w_gate),
                  up=tr(ffn_w_up), down=ffn_w_down.astype(BF16))
    mix_keys, ffn_keys = ("w_in", "w_out", "glu_w"), ("gate", "up", "down")
    layer = lambda k, i: shards[k][i:i + 1]
    full = [dict() for _ in range(depth)]
    for k, g in zip(mix_keys, all_gather_rows([layer(k, 0) for k in mix_keys], "gather_mix_weights_0")):
        full[0][k] = g[0]
    first_done = mods[0, 0, 0, 0:1, 0:128] + full[0]["w_in"][0:1, 0:128].astype(F32)
    weights_in_flight = {0: (ffn_keys, gather_start([layer(k, 0) for k in ffn_keys], first_done, "gather_start_ffn_0"))}
    for i in range(1, depth):
        prev_token = weights_in_flight[i - 1][1][4]
        weights_in_flight[i] = (mix_keys + ffn_keys, gather_start([layer(k, i) for k in mix_keys + ffn_keys], prev_token,
                                                                  f"gather_start_layer_{i}"))
    start_token = sum(fl[1][4][0:1, 0:1] for fl in weights_in_flight.values())

    def land_weights(i, after, name):
        keys, (send_sems, recv_sems, sent, lands, _) = weights_in_flight[i]
        for k, g in zip(keys, gather_wait(send_sems, recv_sems, sent, lands, after, name)):
            full[i][k] = g[0]

    band, icnt = _pool_constants()
    seg_p = jnp.asarray(np.kron(np.eye(A_HEADS), np.full((A_W // A_HEADS,) * 2, A_HEADS / A_W)), BF16)
    pe = _sincos_2d(seq // GRID_W, GRID_W, D)
    xs = embed_tokens(x, ctx, pe)

    saved = []
    for i in range(depth):
        mats, ssm_vjp = jax.vjp(ssm_build, ssm_lam_re[i], ssm_lam_im[i], ssm_log_dt[i], ssm_b_re[i], ssm_b_im[i],
                                ssm_c_re[i], ssm_c_im[i], ssm_d[i])
        if i > 0:
            land_weights(i, xs, f"gather_wait_layer_{i}")
        cst = dict(sw=sgu_w[i].astype(BF16),
                   sbias=jnp.repeat(sgu_b[i].T, A_W // A_HEADS, axis=1),
                   seg_p=seg_p, band=band, icnt=icnt, wbd=_block_diag(pool_w[i]).astype(BF16),
                   pscale=pool_scale[i][None, :], glu_w=full[i]["glu_w"], glu_b=glu_b[i][None, :], w_out=full[i]["w_out"],
                   n2=norm_mix_post[i][None, :])
        n1, n3, n4 = norm_mix_pre[i][None, :], norm_ffn_pre[i][None, :], norm_ffn_post[i][None, :]
        if i == 0:
            n1 = n1 + start_token
        za, zu, zp = pre_mix(xs, mods[i], n1, full[i]["w_in"], nc)
        ops = ssm_operators(mats, f"_{i}")
        ys, hps = ssm_forward(zu, mats, ops, ncr)
        x1, m_pre = post_mix(xs, za, zp, ys, mods[i], cst, nc)
        if i == 0:
            land_weights(0, x1, "gather_wait_ffn_0")
        x2, f_pre = ffn_fwd(x1, mods[i], n3, n4, full[i]["gate"], full[i]["up"], full[i]["down"], n_ctx)
        saved.append(dict(xs=xs, za=za, zu=zu, zp=zp, ys=ys, hps=hps, x1=x1, m=m_pre, f=f_pre, cst=cst, mats=mats,
                          ops=ops, ssm_vjp=ssm_vjp, n1=n1, n3=n3, n4=n4))
        xs = x2

    dx, loss_parts = loss_head(xs, loss_target, nc)
    loss = lax.psum(jnp.sum(loss_parts[:, :, 0, 0]), ("x", "y", "c"))

    grads = {k: [None] * depth for k in _WEIGHTS}
    big = {k: [None] * depth for k in ("w_in", "w_out", "glu_w", "ffn_w_gate", "ffn_w_up", "ffn_w_down")}
    dmods = [None] * depth
    scatter_groups = (("ffn_w_gate", "ffn_w_up", "ffn_w_down"), ("w_out", "glu_w"), ("w_in",))
    in_flight = []

    def send_grads(i, group):
        flight = scatter_start([big[k][i][None] for k in scatter_groups[group]], f"scatter_start_{i}_{group}")
        in_flight.append((i, group, flight))
        return flight[4][0:1, 0:1]

    flat = lambda a: a.reshape(t_all, a.shape[-1])
    for i in reversed(range(depth)):
        sv = saved[i]
        dx1, h2, df, act_b, dgate, dup, st_f = ffn_bwd(dx, sv["x1"], sv["f"], mods[i], sv["n3"], sv["n4"],
                                                       full[i]["gate"], full[i]["up"], full[i]["down"], n_ctx)
        big["ffn_w_gate"][i] = tn_matmul(flat(dgate), flat(h2), f"grad_ffn_gate_{i}")
        big["ffn_w_up"][i] = tn_matmul(flat(dup), flat(h2), f"grad_ffn_up_{i}")
        big["ffn_w_down"][i] = tn_matmul(flat(act_b), flat(df), f"grad_ffn_down_{i}")
        cst_i = dict(sv["cst"], n2=sv["cst"]["n2"] + send_grads(i, 0))
        dza, dzp, dys, cat, dm, gg, dr, st_m, dsw, dsb, dwbd = post_mix_bwd(dx1, sv["m"], sv["za"], sv["zp"], sv["ys"],
                                                                            mods[i], cst_i, nc)
        big["w_out"][i] = tn_matmul(flat(cat), flat(dm), f"grad_w_out_{i}")
        big["glu_w"][i] = tn_matmul(flat(gg), flat(dr), f"grad_glu_w_{i}")
        mats_i = dict(sv["mats"], lam_f=sv["mats"]["lam_f"] + send_grads(i, 1))
        dzu, cot = ssm_backward(dys, sv["zu"], sv["hps"], mats_i, sv["ops"], ncr)
        (grads["ssm_lam_re"][i], grads["ssm_lam_im"][i], grads["ssm_log_dt"][i], grads["ssm_b_re"][i],
         grads["ssm_b_im"][i], grads["ssm_c_re"][i], grads["ssm_c_im"][i], grads["ssm_d"][i]) = sv["ssm_vjp"](cot)
        dx, h1, dz, st_p = pre_mix_bwd(dza, dzu, dzp, sv["xs"], dx1, mods[i], sv["n1"], full[i]["w_in"], nc)
        big["w_in"][i] = tn_matmul(flat(dz), flat(h1), f"grad_w_in_{i}")

        tiles = lambda st, row: st[:, :, row, :]
        allsum = lambda st, row: jnp.sum(tiles(st, row), axis=(0, 1))
        grads["norm_mix_pre"][i] = allsum(st_p, 2)
        grads["norm_mix_post"][i] = allsum(st_m, 1)
        grads["norm_ffn_pre"][i] = allsum(st_f, 3)
        grads["norm_ffn_post"][i] = allsum(st_f, 4)
        misc = allsum(st_m, 2)
        grads["glu_b"][i] = misc[:B_W]
        grads["pool_scale"][i] = misc[B_W:B_W + C_W]
        grads["sgu_w"][i] = dsw
        grads["sgu_b"][i] = jnp.sum(dsb.reshape(CHUNK, A_HEADS, A_W // A_HEADS), axis=2).T
        grads["pool_w"][i] = _block_diag_parts(dwbd, len(POOL_WINDOWS))
        mix = (tiles(st_p, 0), tiles(st_p, 1), tiles(st_m, 0))
        d_lat = jnp.stack([jnp.sum(t[:, nc:], axis=1) for t in mix]
                          + [jnp.sum(tiles(st_f, r), axis=1) for r in (0, 1, 2)], axis=1).reshape(bl, 6 * D)
        d_ctx = jnp.concatenate([jnp.sum(t[:, :nc], axis=(0, 1)) for t in mix]
                                + [allsum(st_f, r) for r in (5, 6, 7)]).reshape(1, 6 * D)
        dmods[i] = jnp.concatenate([d_lat, d_ctx, jnp.zeros((8 - (bl + 1) % 8 if (bl + 1) % 8 else 0, 6 * D), F32)],
                                   axis=0)
        token = send_grads(i, 2)
        if i > 0:
            saved[i - 1]["n3"] = saved[i - 1]["n3"] + token
        else:
            dmods[i] = dmods[i] + token
    grad_x = dx[:, n_ctx:, :]

    dmod_local = jnp.stack(dmods)
    rd = dmod_local.shape[1]
    (dmod_all,) = all_gather_rows([dmod_local], "gather_dmod")
    dmod_cols = lax.dynamic_slice_in_dim(dmod_all, my_idx * wc, wc, axis=2).reshape(depth, N_DEV, rd, wc)
    d_lat_all = dmod_cols[:, :, :bl].reshape(depth, n_batch, wc)
    d_ctx_all = dmod_cols[:, 0, bl]
    for p in range(1, N_DEV):
        d_ctx_all = d_ctx_all + dmod_cols[:, p, bl]
    dmod_rows = jnp.concatenate([d_lat_all, d_ctx_all[:, None, :], jnp.zeros((depth, r_act - n_batch - 1, wc), F32)],
                                axis=1)
    dctx_rows = jnp.pad(d_ctx_all[:, None, :], ((0, 0), (0, 7), (0, 0)))
    g_w_mod, dact_ctx = mod_backward(act, dmod_rows, dctx_rows, w_mod)
    sig_c = jax.nn.sigmoid(c_ctx)
    dsilu_c = sig_c * (1.0 + c_ctx * (1.0 - sig_c))
    small_g = {k: (jnp.stack(grads[k]) if grads[k][0] is not None else None) for k in _SMALL}
    small_g["c_ctx"] = jnp.sum(dact_ctx[:, 0, :], axis=0) * dsilu_c
    small_g["b_mod"] = jnp.stack([jnp.sum(dmods[i][:bl + 1], axis=0) for i in range(depth)])

    packed_g = _pack(small_g).astype(BF16)
    rows_s = packed_g.shape[0]
    (gathered,) = all_gather_rows([packed_g[None]], "gather_small_grads")
    gparts = gathered.reshape(N_DEV, 1, rows_s, D)
    small_w = {k: wts[k] for k in _SMALL}
    outs = adamw(_pack(small_w)[None], gparts, _pack({k: mom_m[k] for k in _SMALL})[None],
                 _pack({k: mom_v[k] for k in _SMALL})[None], "adamw_replicated")
    res = {k: [None] * 4 for k in _WEIGHTS}
    for slot, packed in enumerate(outs):
        un = _unpack(packed[0], small_w)
        for k in _SMALL:
            res[k][slot] = un[k]

    landed = {}
    for i, group, (send_sems, recv_sems, arrs_thru, lands_thru, _) in in_flight:
        sent, lands = scatter_wait(send_sems, recv_sems, arrs_thru, lands_thru, outs[0], f"scatter_wait_{i}_{group}")
        for k, a, z in zip(scatter_groups[group], sent, lands):
            r = a.shape[1] // N_DEV
            own = lax.dynamic_slice_in_dim(a, my_idx * r, r, axis=1)[None]
            landed[k, i] = lax.dynamic_update_slice_in_dim(z, own, my_idx, axis=0)
    for k in big:
        transposed = k in ("w_in", "ffn_w_gate", "ffn_w_up")
        view = (lambda a: jnp.swapaxes(a, 1, 2)) if transposed else (lambda a: a)
        o4 = adamw(view(wts[k]), [landed[k, i] for i in range(depth)], view(mom_m[k]), view(mom_v[k]), "adamw_" + k)
        res[k] = [view(o) for o in o4]
    res["w_mod"] = list(adamw(w_mod, g_w_mod[None], m_w_mod, v_w_mod, "adamw_w_mod"))

    return (loss, grad_x, *[res[k][0] for k in _WEIGHTS], *[res[k][1] for k in _WEIGHTS],
            *[res[k][2] for k in _WEIGHTS], *[res[k][3] for k in _WEIGHTS])
```

```python
import functools
import math

import numpy as np
import jax
import jax.numpy as jnp
from jax import lax
from jax.experimental import pallas as pl
from jax.experimental.pallas import tpu as pltpu

F32 = jnp.float32
BF16 = jnp.bfloat16
HI = lax.Precision.HIGHEST
MESH = pl.DeviceIdType.MESH

D = 1024
D_IN = 1280
D_FF = 2816
A_W = 256
B_W = 512
C_W = 256
A_HEADS = 4
CHUNK = 128
SSM_G = 32
SSM_H = 16
SSM_P = 64
GRID_W = 64
POOL_WINDOWS = (2, 4, 8, 16)
EPS = 1e-6
N_DEV = 8

TT = 256
TC = 8
ROW_W = TC * B_W
QW = ROW_W // 4
GQ = 8
FF_CHUNK = 256
VMEM_LIMIT = 60 * 1024 * 1024

ADAM_LR = 0.001
ADAM_B1 = 0.9
ADAM_B2 = 0.999
ADAM_EPS = 1e-08
ADAM_WD = 0.01
ADAM_STEP = 10


def _cp(sem):
    return pltpu.CompilerParams(dimension_semantics=sem, vmem_limit_bytes=VMEM_LIMIT)


def dot_nn(a, b):
    return jnp.dot(a, b, preferred_element_type=F32)


def dot_nt(a, b):
    return lax.dot_general(a, b, (((1,), (1,)), ((), ())), preferred_element_type=F32)


def dot_tn(a, b):
    return lax.dot_general(a, b, (((0,), (0,)), ((), ())), preferred_element_type=F32)


def split_bf16(x):
    hi = x.astype(BF16)
    lo = (x - hi.astype(F32)).astype(BF16)
    return hi, lo


def gelu(x):
    return jax.nn.gelu(x)


def gelu_grad(x):
    c = math.sqrt(2.0 / math.pi)
    t = jnp.tanh(c * (x + 0.044715 * x * x * x))
    return 0.5 * (1.0 + t) + 0.5 * x * (1.0 - t * t) * c * (1.0 + 3.0 * 0.044715 * x * x)


def rms_stats(x):
    r = lax.rsqrt(jnp.mean(x * x, axis=-1, keepdims=True) + EPS)
    return r, x * r


def rms_bwd(r, xn, dxn):
    return r * (dxn - xn * jnp.mean(dxn * xn, axis=-1, keepdims=True))


def colsum(x):
    return jnp.sum(x, axis=0, keepdims=True)


def lane_group(width, group):
    return lax.broadcasted_iota(jnp.int32, (1, width), 1) // group


def _tile_spec(width):
    return pl.BlockSpec((None, TT, width), lambda b, j: (b, j, 0))


def _mod_spec(nc):
    return pl.BlockSpec((None, None, 8, D), lambda b, j: (b, jnp.where(j >= nc, 1, 0), 0, 0))


def _full_spec(shape):
    zeros = (0,) * len(shape)
    return pl.BlockSpec(shape, lambda b, j: zeros)


def _kind_spec(shape, nc):
    zeros = (0,) * len(shape)
    return pl.BlockSpec((None,) + shape, lambda b, j: (jnp.where(j >= nc, 1, 0),) + zeros)


def _stat_spec():
    return pl.BlockSpec((None, None, 8, D), lambda b, j: (b, j, 0, 0))


def _chunk_spec():
    return pl.BlockSpec((None, TT // TC, ROW_W), lambda b, j: (b, j, 0))


def _rows_to_chunks(val, scratch, out_ref):
    for cb in range(B_W // 128):
        scratch[cb] = val[:, cb * 128:(cb + 1) * 128]
    for s in range(TC):
        for cb in range(B_W // 128):
            lo = cb * QW + s * 128
            out_ref[:, lo:lo + 128] = scratch.at[cb][pl.ds(s, TT // TC, stride=TC), :]


def _chunks_to_rows(in_ref, scratch):
    for s in range(TC):
        for cb in range(B_W // 128):
            lo = cb * QW + s * 128
            scratch.at[cb][pl.ds(s, TT // TC, stride=TC), :] = in_ref[:, lo:lo + 128]
    return jnp.concatenate([scratch[cb] for cb in range(B_W // 128)], axis=1)


def _chunk_scratch():
    return pltpu.VMEM((B_W // 128, TT, 128), F32)


def embed_tokens(x, ctx, pe):
    bl, seq, _ = x.shape
    nc = ctx.shape[1] // TT
    nt = nc + seq // TT

    def body(ctx_ref, x_ref, pe_ref, o_ref):
        j = pl.program_id(1)

        @pl.when(j < nc)
        def _():
            o_ref[...] = ctx_ref[...]

        @pl.when(j >= nc)
        def _():
            o_ref[...] = x_ref[...] + pe_ref[...]

    return pl.pallas_call(
        body, name="embed_tokens", grid=(bl, nt),
        in_specs=[pl.BlockSpec((None, TT, D), lambda b, j: (b, jnp.minimum(j, nc - 1), 0)),
                  pl.BlockSpec((None, TT, D), lambda b, j: (b, jnp.maximum(j - nc, 0), 0)),
                  pl.BlockSpec((TT, D), lambda b, j: (jnp.maximum(j - nc, 0), 0))],
        out_specs=_tile_spec(D),
        out_shape=jax.ShapeDtypeStruct((bl, nt * TT, D), F32),
        compiler_params=_cp(("arbitrary", "arbitrary")),
    )(ctx, x, pe)


def pre_mix(xs, mod, n1, w_int, nc):
    bl, s, _ = xs.shape

    def body(x_ref, mod_ref, n_ref, w_ref, za_ref, zu_ref, zp_ref, u_s):
        r, xn = rms_stats(x_ref[...])
        h = xn * n_ref[...] * (1.0 + mod_ref[1:2, :]) + mod_ref[0:1, :]
        z = dot_nt(h.astype(BF16), w_ref[...])
        za_ref[...] = z[:, :2 * A_W]
        _rows_to_chunks(z[:, 2 * A_W:2 * A_W + B_W], u_s, zu_ref)
        zp_ref[...] = z[:, 2 * A_W + B_W:]

    return pl.pallas_call(
        body, name="pre_mix", grid=(bl, s // TT),
        in_specs=[_tile_spec(D), _mod_spec(nc), _full_spec((1, D)), _full_spec((D_IN, D))],
        out_specs=[_tile_spec(2 * A_W), _chunk_spec(), _tile_spec(C_W)],
        out_shape=[jax.ShapeDtypeStruct((bl, s, 2 * A_W), F32), jax.ShapeDtypeStruct((bl, s // TC, ROW_W), F32),
                   jax.ShapeDtypeStruct((bl, s, C_W), F32)],
        scratch_shapes=[_chunk_scratch()],
        compiler_params=_cp(("arbitrary", "arbitrary")),
    )(xs, mod, n1, w_int)


def _seg_mean(x, seg_p):
    hi, lo = split_bf16(x)
    return dot_nn(hi, seg_p) + dot_nn(lo, seg_p)


def _sgu_forward(za, sw_ref, sbias, seg_p):
    ge = gelu(za)
    u, v = ge[:, :A_W], ge[:, A_W:]
    dv = v - _seg_mean(v, seg_p)
    rs = lax.rsqrt(_seg_mean(dv * dv, seg_p) + EPS)
    vn = dv * rs
    head = lane_group(A_W, A_W // A_HEADS)
    parts = []
    for c2 in range(TT // CHUNK):
        vb = vn[c2 * CHUNK:(c2 + 1) * CHUNK].astype(BF16)
        sc = sbias
        for h in range(A_HEADS):
            sc = sc + jnp.where(head == h, dot_nn(sw_ref[h], vb), 0.0)
        parts.append(sc)
    sg = jnp.concatenate(parts, axis=0)
    return u * sg, (u, vn, rs, sg)


def _pool_forward(zp, band_ref, icnt, wbd, pscale):
    hi, lo = split_bf16(zp)
    grp = lane_group(C_W, C_W // len(POOL_WINDOWS))
    q = jnp.zeros_like(zp)
    for i in range(len(POOL_WINDOWS)):
        t = dot_nn(band_ref[i], hi) + dot_nn(band_ref[i], lo)
        q = jnp.where(grp == i, t, q)
    q = q * icnt - zp
    o = dot_nn(q.astype(BF16), wbd)
    return o * pscale, (q, o)


def _glu_forward(y, glu_w, glu_b):
    g = gelu(y)
    sg = jax.nn.sigmoid(dot_nn(g.astype(BF16), glu_w) + glu_b)
    return g * sg, (g, sg)


_MIX_CONST_SHAPES = dict(sw=(A_HEADS, CHUNK, CHUNK), sbias=(CHUNK, A_W), seg_p=(A_W, A_W), wbd=(C_W, C_W),
                         pscale=(1, C_W), glu_w=(B_W, B_W), glu_b=(1, B_W), w_out=(D, D), n2=(1, D))


def _mix_const_specs(nc):
    return ([_full_spec(_MIX_CONST_SHAPES[k]) for k in ("sw", "sbias", "seg_p")]
            + [_kind_spec((len(POOL_WINDOWS), TT, TT), nc), _kind_spec((TT, C_W), nc)]
            + [_full_spec(_MIX_CONST_SHAPES[k]) for k in ("wbd", "pscale", "glu_w", "glu_b", "w_out", "n2")])


def _mix_const_args(cst):
    return [cst[k] for k in ("sw", "sbias", "seg_p", "band", "icnt", "wbd", "pscale", "glu_w", "glu_b", "w_out", "n2")]


def post_mix(xs, za, zp, ys, mod, cst, nc):
    bl, s, _ = xs.shape

    def body(x_ref, za_ref, zp_ref, y_ref, mod_ref, sw_ref, sbias_ref, seg_ref, band_ref, icnt_ref, wbd_ref,
             ps_ref, gw_ref, gb_ref, wo_ref, n2_ref, x1_ref, m_ref, y_s):
        a, _ = _sgu_forward(za_ref[...], sw_ref, sbias_ref[...], seg_ref[...])
        p, _ = _pool_forward(zp_ref[...], band_ref, icnt_ref[...], wbd_ref[...], ps_ref[...])
        sm, _ = _glu_forward(_chunks_to_rows(y_ref, y_s), gw_ref[...], gb_ref[...])
        cat = jnp.concatenate([a, sm, p], axis=1).astype(BF16)
        m = dot_nn(cat, wo_ref[...])
        _, mn = rms_stats(m)
        m_ref[...] = m
        x1_ref[...] = x_ref[...] + mod_ref[2:3, :] * (mn * n2_ref[...])

    return pl.pallas_call(
        body, name="post_mix", grid=(bl, s // TT),
        in_specs=[_tile_spec(D), _tile_spec(2 * A_W), _tile_spec(C_W), _chunk_spec(), _mod_spec(nc)]
        + _mix_const_specs(nc),
        out_specs=[_tile_spec(D), _tile_spec(D)],
        out_shape=[jax.ShapeDtypeStruct((bl, s, D), F32), jax.ShapeDtypeStruct((bl, s, D), F32)],
        scratch_shapes=[_chunk_scratch()],
        compiler_params=_cp(("arbitrary", "arbitrary")),
    )(xs, za, zp, ys, mod, *_mix_const_args(cst))


def post_mix_bwd(dx1, m, za, zp, ys, mod, cst, nc):
    bl, s, _ = m.shape
    nt = s // TT

    def body(dx_ref, m_ref, za_ref, zp_ref, y_ref, mod_ref, sw_ref, sbias_ref, seg_ref, band_ref, icnt_ref,
             wbd_ref, ps_ref, gw_ref, gb_ref, wo_ref, n2_ref,
             dza_ref, dzp_ref, dy_ref, cat_ref, dm_ref, gg_ref, dr_ref, st_ref, dsw_ref, dsb_ref, dwbd_ref, y_s):
        first = jnp.logical_and(pl.program_id(0) == 0, pl.program_id(1) == 0)

        @pl.when(first)
        def _():
            dsw_ref[...] = jnp.zeros_like(dsw_ref)
            dsb_ref[...] = jnp.zeros_like(dsb_ref)
            dwbd_ref[...] = jnp.zeros_like(dwbd_ref)

        seg_p = seg_ref[...]
        za = za_ref[...]
        zp_v = zp_ref[...]
        yv = _chunks_to_rows(y_ref, y_s)
        a, (u, vn, rs, sg) = _sgu_forward(za, sw_ref, sbias_ref[...], seg_p)
        p, (q, o) = _pool_forward(zp_v, band_ref, icnt_ref[...], wbd_ref[...], ps_ref[...])
        sm, (g, sig) = _glu_forward(yv, gw_ref[...], gb_ref[...])
        cat_ref[...] = jnp.concatenate([a, sm, p], axis=1).astype(BF16)

        dx = dx_ref[...]
        g1 = mod_ref[2:3, :]
        n2 = n2_ref[...]
        mv = m_ref[...]
        rm, mn = rms_stats(mv)
        st_ref[...] = jnp.zeros_like(st_ref)
        st_ref[0:1, :] = colsum(dx * (mn * n2))
        st_ref[1:2, :] = colsum(dx * g1 * mn)
        dm = rms_bwd(rm, mn, dx * g1 * n2)
        dmb = dm.astype(BF16)
        dm_ref[...] = dmb
        dcat = dot_nt(dmb, wo_ref[...])
        da, dsm, dp = dcat[:, :A_W], dcat[:, A_W:A_W + B_W], dcat[:, A_W + B_W:]

        du = da * sg
        dsv = da * u
        head = lane_group(A_W, A_W // A_HEADS)
        dvn_parts = []
        dsb_acc = jnp.zeros((CHUNK, A_W), F32)
        for c2 in range(TT // CHUNK):
            dsc = dsv[c2 * CHUNK:(c2 + 1) * CHUNK]
            dsc_b = dsc.astype(BF16)
            vb = vn[c2 * CHUNK:(c2 + 1) * CHUNK].astype(BF16)
            dsb_acc = dsb_acc + dsc
            dvn_c = jnp.zeros((CHUNK, A_W), F32)
            for h in range(A_HEADS):
                dsw_ref[h] += dot_nt(jnp.where(head == h, dsc, 0.0).astype(BF16), vb)
                dvn_c = dvn_c + jnp.where(head == h, dot_tn(sw_ref[h], dsc_b), 0.0)
            dvn_parts.append(dvn_c)
        dsb_ref[...] += dsb_acc
        dvn = jnp.concatenate(dvn_parts, axis=0)
        dv = rs * (dvn - _seg_mean(dvn, seg_p) - vn * _seg_mean(dvn * vn, seg_p))
        dza_ref[...] = jnp.concatenate([du, dv], axis=1) * gelu_grad(za)

        ps = ps_ref[...]
        do = dp * ps
        dps = colsum(dp * o)
        dob = do.astype(BF16)
        dwbd_ref[...] += dot_tn(q.astype(BF16), dob)
        dq = dot_nt(dob, wbd_ref[...])
        hi, lo = split_bf16(dq * icnt_ref[...])
        grp = lane_group(C_W, C_W // len(POOL_WINDOWS))
        dzp = -dq
        for i in range(len(POOL_WINDOWS)):
            t = dot_tn(band_ref[i], hi) + dot_tn(band_ref[i], lo)
            dzp = dzp + jnp.where(grp == i, t, 0.0)
        dzp_ref[...] = dzp

        dr = dsm * g * sig * (1.0 - sig)
        drb = dr.astype(BF16)
        dr_ref[...] = drb
        gg_ref[...] = g.astype(BF16)
        dg = dsm * sig + dot_nt(drb, gw_ref[...])
        _rows_to_chunks(dg * gelu_grad(yv), y_s, dy_ref)
        st_ref[2:3, :] = jnp.concatenate([colsum(dr), dps, jnp.zeros((1, D - B_W - C_W), F32)], axis=1)

    acc = lambda shape: pl.BlockSpec(shape, lambda b, j: (0,) * len(shape))
    return pl.pallas_call(
        body, name="post_mix_bwd", grid=(bl, nt),
        in_specs=[_tile_spec(D), _tile_spec(D), _tile_spec(2 * A_W), _tile_spec(C_W), _chunk_spec(), _mod_spec(nc)]
        + _mix_const_specs(nc),
        out_specs=[_tile_spec(2 * A_W), _tile_spec(C_W), _chunk_spec(), _tile_spec(D), _tile_spec(D),
                   _tile_spec(B_W), _tile_spec(B_W), _stat_spec(),
                   acc((A_HEADS, CHUNK, CHUNK)), acc((CHUNK, A_W)), acc((C_W, C_W))],
        out_shape=[jax.ShapeDtypeStruct((bl, s, 2 * A_W), F32), jax.ShapeDtypeStruct((bl, s, C_W), F32),
                   jax.ShapeDtypeStruct((bl, s // TC, ROW_W), F32), jax.ShapeDtypeStruct((bl, s, D), BF16),
                   jax.ShapeDtypeStruct((bl, s, D), BF16), jax.ShapeDtypeStruct((bl, s, B_W), BF16),
                   jax.ShapeDtypeStruct((bl, s, B_W), BF16), jax.ShapeDtypeStruct((bl, nt, 8, D), F32),
                   jax.ShapeDtypeStruct((A_HEADS, CHUNK, CHUNK), F32), jax.ShapeDtypeStruct((CHUNK, A_W), F32),
                   jax.ShapeDtypeStruct((C_W, C_W), F32)],
        scratch_shapes=[_chunk_scratch()],
        compiler_params=_cp(("arbitrary", "arbitrary")),
    )(dx1, m, za, zp, ys, mod, *_mix_const_args(cst))


def pre_mix_bwd(dza, dzu, dzp, xs, dxres, mod, n1, w_int, nc):
    bl, s, _ = xs.shape
    nt = s // TT

    def body(dza_ref, dzu_ref, dzp_ref, x_ref, dres_ref, mod_ref, n_ref, w_ref, dx_ref, h_ref, dz_ref, st_ref, u_s):
        dz = jnp.concatenate([dza_ref[...], _chunks_to_rows(dzu_ref, u_s), dzp_ref[...]], axis=1).astype(BF16)
        dz_ref[...] = dz
        dh = dot_nn(dz, w_ref[...])
        r, xn = rms_stats(x_ref[...])
        n1v = n_ref[...]
        sc = mod_ref[1:2, :]
        xg = xn * n1v
        h_ref[...] = (xg * (1.0 + sc) + mod_ref[0:1, :]).astype(BF16)
        dyv = dh * (1.0 + sc)
        st_ref[...] = jnp.zeros_like(st_ref)
        st_ref[0:1, :] = colsum(dh)
        st_ref[1:2, :] = colsum(dh * xg)
        st_ref[2:3, :] = colsum(dyv * xn)
        dx_ref[...] = dres_ref[...] + rms_bwd(r, xn, dyv * n1v)

    return pl.pallas_call(
        body, name="pre_mix_bwd", grid=(bl, nt),
        in_specs=[_tile_spec(2 * A_W), _chunk_spec(), _tile_spec(C_W), _tile_spec(D), _tile_spec(D), _mod_spec(nc),
                  _full_spec((1, D)), _full_spec((D_IN, D))],
        out_specs=[_tile_spec(D), _tile_spec(D), _tile_spec(D_IN), _stat_spec()],
        out_shape=[jax.ShapeDtypeStruct((bl, s, D), F32), jax.ShapeDtypeStruct((bl, s, D), BF16),
                   jax.ShapeDtypeStruct((bl, s, D_IN), BF16), jax.ShapeDtypeStruct((bl, nt, 8, D), F32)],
        scratch_shapes=[_chunk_scratch()],
        compiler_params=_cp(("arbitrary", "arbitrary")),
    )(dza, dzu, dzp, xs, dxres, mod, n1, w_int)


def _ffn_tile(s):
    return 768 if s % 768 == 0 else TT


def _ctx_rows(tf, n_ctx, j):
    return lax.broadcasted_iota(jnp.int32, (tf, 1), 0) + j * tf < n_ctx


def _mod_row(mod_ref, is_ctx, row):
    return jnp.where(is_ctx, mod_ref[0, row:row + 1, :], mod_ref[1, row:row + 1, :])


def ffn_fwd(x1, mod, n3, n4, wg_t, wu_t, wd, n_ctx):
    bl, s, _ = x1.shape
    tf = _ffn_tile(s)
    nk = D_FF // FF_CHUNK
    tile = pl.BlockSpec((None, tf, D), lambda b, j, k: (b, j, 0))
    modspec = pl.BlockSpec((None, 2, 8, D), lambda b, j, k: (b, 0, 0, 0))
    vec = pl.BlockSpec((1, D), lambda b, j, k: (0, 0))
    wspec = pl.BlockSpec((FF_CHUNK, D), lambda b, j, k: (k, 0))
    ftile = pl.BlockSpec((None, tf, FF_CHUNK), lambda b, j, k: (b, j, k))

    def body(x_ref, mod_ref, n3_ref, n4_ref, wg_ref, wu_ref, wd_ref, x2_ref, f_ref, gate_ref, up_ref, h_s, acc_s):
        j, k = pl.program_id(1), pl.program_id(2)

        @pl.when(k == 0)
        def _():
            is_ctx = _ctx_rows(tf, n_ctx, j)
            _, xn = rms_stats(x_ref[...])
            h_s[...] = (xn * n3_ref[...] * (1.0 + _mod_row(mod_ref, is_ctx, 4)) + _mod_row(mod_ref, is_ctx, 3)).astype(BF16)
            acc_s[...] = jnp.zeros_like(acc_s)

        h = h_s[...]
        gate = dot_nt(h, wg_ref[...])
        up = dot_nt(h, wu_ref[...])
        gate_ref[...] = gate.astype(BF16)
        up_ref[...] = up.astype(BF16)
        act = (gate * jax.nn.sigmoid(gate)) * up
        acc_s[...] += dot_nn(act.astype(BF16), wd_ref[...])

        @pl.when(k == nk - 1)
        def _():
            f = acc_s[...]
            f_ref[...] = f
            _, fn = rms_stats(f)
            x2_ref[...] = x_ref[...] + _mod_row(mod_ref, _ctx_rows(tf, n_ctx, j), 5) * (fn * n4_ref[...])

    return pl.pallas_call(
        body, name="ffn_fwd", grid=(bl, s // tf, nk),
        in_specs=[tile, modspec, vec, vec, wspec, wspec, wspec],
        out_specs=[tile, tile, ftile, ftile],
        out_shape=[jax.ShapeDtypeStruct((bl, s, D), F32), jax.ShapeDtypeStruct((bl, s, D), F32),
                   jax.ShapeDtypeStruct((bl, s, D_FF), BF16), jax.ShapeDtypeStruct((bl, s, D_FF), BF16)],
        scratch_shapes=[pltpu.VMEM((tf, D), BF16), pltpu.VMEM((tf, D), F32)],
        compiler_params=_cp(("arbitrary", "arbitrary", "arbitrary")),
    )(x1, mod, n3, n4, wg_t, wu_t, wd)


def ffn_bwd(dx2, x1, f, gate_b, up_b, mod, n3, n4, wg_t, wu_t, wd, n_ctx):
    bl, s, _ = x1.shape
    tf = _ffn_tile(s)
    nt = s // tf
    nk = D_FF // FF_CHUNK
    tile = pl.BlockSpec((None, tf, D), lambda b, j, k: (b, j, 0))
    ftile = pl.BlockSpec((None, tf, FF_CHUNK), lambda b, j, k: (b, j, jnp.minimum(k, nk - 1)))
    modspec = pl.BlockSpec((None, 2, 8, D), lambda b, j, k: (b, 0, 0, 0))
    vec = pl.BlockSpec((1, D), lambda b, j, k: (0, 0))
    wspec = pl.BlockSpec((FF_CHUNK, D), lambda b, j, k: (jnp.minimum(k, nk - 1), 0))
    wprev = pl.BlockSpec((FF_CHUNK, D), lambda b, j, k: (jnp.maximum(k - 1, 0), 0))
    stat = pl.BlockSpec((None, None, 8, D), lambda b, j, k: (b, j, 0, 0))

    def split_sum(is_ctx, v, st_ref, row):
        st_ref[row:row + 1, :] = colsum(jnp.where(is_ctx, 0.0, v))
        st_ref[row + 5:row + 6, :] = colsum(jnp.where(is_ctx, v, 0.0))

    def body(dx_ref, x_ref, f_ref, gate_ref, up_ref, mod_ref, n3_ref, n4_ref, wd_ref, wgp_ref, wup_ref,
             dx1_ref, h_ref, df_ref, act_ref, dgate_ref, dup_ref, st_ref, df_s, acc_s, dgate_s, dup_s):
        j, k = pl.program_id(1), pl.program_id(2)

        @pl.when(k == 0)
        def _():
            is_ctx = _ctx_rows(tf, n_ctx, j)
            dx = dx_ref[...]
            g2 = _mod_row(mod_ref, is_ctx, 5)
            n4 = n4_ref[...]
            rf, fn = rms_stats(f_ref[...])
            st_ref[...] = jnp.zeros_like(st_ref)
            split_sum(is_ctx, dx * (fn * n4), st_ref, 2)
            st_ref[4:5, :] = colsum(dx * g2 * fn)
            df = rms_bwd(rf, fn, dx * g2 * n4).astype(BF16)
            df_s[...] = df
            df_ref[...] = df
            _, xn = rms_stats(x_ref[...])
            h_ref[...] = (xn * n3_ref[...] * (1.0 + _mod_row(mod_ref, is_ctx, 4)) + _mod_row(mod_ref, is_ctx, 3)).astype(BF16)
            acc_s[...] = jnp.zeros_like(acc_s)
            dgate_s[1] = jnp.zeros((tf, FF_CHUNK), BF16)
            dup_s[1] = jnp.zeros((tf, FF_CHUNK), BF16)

        prev = (k + 1) % 2
        acc_s[...] += dot_nn(dgate_s[prev], wgp_ref[...]) + dot_nn(dup_s[prev], wup_ref[...])
        gate = gate_ref[...].astype(F32)
        up = up_ref[...].astype(F32)
        sg = jax.nn.sigmoid(gate)
        silu = gate * sg
        dact = dot_nt(df_s[...], wd_ref[...])
        act_ref[...] = (silu * up).astype(BF16)
        dgate = (dact * up * (sg * (1.0 + gate * (1.0 - sg)))).astype(BF16)
        dup = (dact * silu).astype(BF16)
        dgate_ref[...] = dgate
        dup_ref[...] = dup
        dgate_s[k % 2] = dgate
        dup_s[k % 2] = dup

        @pl.when(k == nk)
        def _():
            is_ctx = _ctx_rows(tf, n_ctx, j)
            dh = acc_s[...]
            r, xn = rms_stats(x_ref[...])
            n3 = n3_ref[...]
            sc = _mod_row(mod_ref, is_ctx, 4)
            xg = xn * n3
            dyv = dh * (1.0 + sc)
            split_sum(is_ctx, dh, st_ref, 0)
            split_sum(is_ctx, dh * xg, st_ref, 1)
            st_ref[3:4, :] = colsum(dyv * xn)
            dx1_ref[...] = dx_ref[...] + rms_bwd(r, xn, dyv * n3)

    return pl.pallas_call(
        body, name="ffn_bwd", grid=(bl, nt, nk + 1),
        in_specs=[tile, tile, tile, ftile, ftile, modspec, vec, vec, wspec, wprev, wprev],
        out_specs=[tile, tile, tile, ftile, ftile, ftile, stat],
        out_shape=[jax.ShapeDtypeStruct((bl, s, D), F32), jax.ShapeDtypeStruct((bl, s, D), BF16),
                   jax.ShapeDtypeStruct((bl, s, D), BF16), jax.ShapeDtypeStruct((bl, s, D_FF), BF16),
                   jax.ShapeDtypeStruct((bl, s, D_FF), BF16), jax.ShapeDtypeStruct((bl, s, D_FF), BF16),
                   jax.ShapeDtypeStruct((bl, nt, 8, D), F32)],
        scratch_shapes=[pltpu.VMEM((tf, D), BF16), pltpu.VMEM((tf, D), F32),
                        pltpu.VMEM((2, tf, FF_CHUNK), BF16), pltpu.VMEM((2, tf, FF_CHUNK), BF16)],
        compiler_params=_cp(("arbitrary", "arbitrary", "arbitrary")),
    )(dx2, x1, f, gate_b, up_b, mod, n3, n4, wd, wg_t, wu_t)


def loss_head(xs, target, nc):
    bl, s, _ = xs.shape
    nt = s // TT

    def body(x_ref, t_ref, dx_ref, l_ref):
        j = pl.program_id(1)

        @pl.when(j < nc)
        def _():
            dx_ref[...] = jnp.zeros_like(dx_ref)
            l_ref[...] = jnp.zeros_like(l_ref)

        @pl.when(j >= nc)
        def _():
            e = x_ref[...] - t_ref[...]
            dx_ref[...] = e * (1.0 / D)
            tok = jnp.mean(e * e, axis=-1, keepdims=True)
            l_ref[...] = jnp.zeros_like(l_ref) + 0.5 * jnp.sum(tok, axis=0, keepdims=True)

    return pl.pallas_call(
        body, name="loss_head", grid=(bl, nt),
        in_specs=[_tile_spec(D), pl.BlockSpec((None, TT, D), lambda b, j: (b, jnp.maximum(j - nc, 0), 0))],
        out_specs=[_tile_spec(D), pl.BlockSpec((None, None, 8, 128), lambda b, j: (b, j, 0, 0))],
        out_shape=[jax.ShapeDtypeStruct((bl, s, D), F32), jax.ShapeDtypeStruct((bl, nt, 8, 128), F32)],
        compiler_params=_cp(("arbitrary", "arbitrary")),
    )(xs, target)


def tn_matmul(a, b, name):
    t, ka = a.shape
    n = b.shape[1]
    tk = ka if ka <= 1408 else ka // 2
    tt = 512 if t % 512 == 0 else 256
    nsteps = t // tt

    def body(a_ref, b_ref, o_ref, acc_s):
        @pl.when(pl.program_id(1) == 0)
        def _():
            acc_s[...] = jnp.zeros_like(acc_s)

        acc_s[...] += dot_tn(a_ref[...], b_ref[...])

        @pl.when(pl.program_id(1) == nsteps - 1)
        def _():
            o_ref[...] = acc_s[...].astype(BF16)

    return pl.pallas_call(
        body, name=name, grid=(ka // tk, nsteps),
        in_specs=[pl.BlockSpec((tt, tk), lambda i, s: (s, i)), pl.BlockSpec((tt, n), lambda i, s: (s, 0))],
        out_specs=pl.BlockSpec((tk, n), lambda i, s: (i, 0)),
        out_shape=jax.ShapeDtypeStruct((ka, n), BF16),
        scratch_shapes=[pltpu.VMEM((tk, n), F32)],
        compiler_params=_cp(("arbitrary", "arbitrary")),
    )(a, b)


def qmm(terms, name):
    r = terms[0][0].shape[0]
    rt = r // 2 if r % 16 == 0 and r >= 512 else r
    n = len(terms)

    def body(*refs):
        acc = None
        for k in range(n):
            y = dot_nn(refs[2 * k][...].astype(BF16), refs[2 * k + 1][...])
            acc = y if acc is None else acc + y
        refs[2 * n][...] = acc

    row = pl.BlockSpec((rt, QW), lambda q, i: (i, q))
    wspec = pl.BlockSpec((None, QW, QW), lambda q, i: (q, 0, 0))
    return pl.pallas_call(
        body, name=name, grid=(4, r // rt), in_specs=[row, wspec] * n, out_specs=row,
        out_shape=jax.ShapeDtypeStruct((r, ROW_W), F32),
        compiler_params=_cp(("arbitrary", "arbitrary")),
    )(*[x for term in terms for x in term])


def _same_group(rows, cols, row_group, col_group):
    ri = jnp.bitwise_and(lax.broadcasted_iota(jnp.int32, (rows, cols), 0) // row_group, GQ - 1)
    ci = jnp.bitwise_and(lax.broadcasted_iota(jnp.int32, (rows, cols), 1) // col_group, GQ - 1)
    return ri == ci


def _spread_matrix():
    m = np.zeros((2 * SSM_P, QW), np.float32)
    for reim in range(2):
        for g in range(GQ):
            for p in range(SSM_P):
                m[reim * SSM_P + p, reim * (QW // 2) + g * SSM_P + p] = 1.0
    return jnp.asarray(m, BF16)


def assemble_ts(v, name):
    def body(v_ref, f_ref, big_ref, bigt_ref):
        keep = _same_group(GQ * SSM_H, QW, SSM_H, SSM_P)
        for e in range(TC):
            hi, lo = split_bf16(v_ref[e])
            t = jnp.where(keep, dot_nn(hi, f_ref[...]) + dot_nn(lo, f_ref[...]), 0.0)
            big_ref[e * 128:(e + 1) * 128, :] = t.astype(BF16)
            bigt_ref[:, e * 128:(e + 1) * 128] = t.T.astype(BF16)

    return pl.pallas_call(
        body, name=name, grid=(4,),
        in_specs=[pl.BlockSpec((None, TC, 128, 128), lambda q: (q, 0, 0, 0)),
                  pl.BlockSpec((128, QW), lambda q: (0, 0))],
        out_specs=[pl.BlockSpec((None, QW, QW), lambda q: (q, 0, 0))] * 2,
        out_shape=[jax.ShapeDtypeStruct((4, QW, QW), BF16), jax.ShapeDtypeStruct((4, QW, QW), BF16)],
        compiler_params=_cp(("arbitrary",)),
    )(v, _spread_matrix())


def assemble_tt(lags, name):
    def body(l_ref, m_ref, mt_ref):
        blocks = [l_ref[n] for n in range(2 * TC - 1)]
        flipped = [b.T.astype(BF16) for b in blocks]
        blocks = [b.astype(BF16) for b in blocks]
        for s in range(TC):
            for t in range(TC):
                m_ref[s * 128:(s + 1) * 128, t * 128:(t + 1) * 128] = blocks[t - s + TC - 1]
                mt_ref[t * 128:(t + 1) * 128, s * 128:(s + 1) * 128] = flipped[t - s + TC - 1]

    return pl.pallas_call(
        body, name=name, grid=(4,),
        in_specs=[pl.BlockSpec((None, 2 * TC - 1, 128, 128), lambda q: (q, 0, 0, 0))],
        out_specs=[pl.BlockSpec((None, QW, QW), lambda q: (q, 0, 0))] * 2,
        out_shape=[jax.ShapeDtypeStruct((4, QW, QW), BF16)] * 2,
        compiler_params=_cp(("arbitrary",)),
    )(lags)


def _qtn_call(body, a, b, out_shape, out_block, extra, name):
    r = a.shape[0]
    col = pl.BlockSpec((r, QW), lambda q: (0, q))
    return pl.pallas_call(
        body, name=name, grid=(4,),
        in_specs=[col, col] + [pl.BlockSpec(x.shape, lambda q: (0, 0)) for x in extra],
        out_specs=pl.BlockSpec((None,) + out_block, lambda q: (q,) + (0,) * len(out_block)),
        out_shape=jax.ShapeDtypeStruct((4,) + out_block, F32),
        compiler_params=_cp(("arbitrary",)),
    )(a, b, *extra)


def qtn_ts(a, b, name):
    def body(a_ref, b_ref, f_ref, o_ref):
        full = dot_tn(a_ref[...].astype(BF16), b_ref[...].astype(BF16))
        keep = _same_group(GQ * SSM_H, QW, SSM_H, SSM_P)
        for e in range(TC):
            hi, lo = split_bf16(jnp.where(keep, full[e * 128:(e + 1) * 128, :], 0.0))
            o_ref[e] = dot_nt(hi, f_ref[...]) + dot_nt(lo, f_ref[...])

    return _qtn_call(body, a, b, None, (TC, 128, 128), [_spread_matrix()], name)


def qtn_tt(a, b, name):
    def body(a_ref, b_ref, o_ref):
        full = dot_tn(a_ref[...].astype(BF16), b_ref[...].astype(BF16))
        for lag in range(-(TC - 1), TC):
            acc = None
            for s in range(TC):
                t = s + lag
                if 0 <= t < TC:
                    blk = full[s * 128:(s + 1) * 128, t * 128:(t + 1) * 128]
                    acc = blk if acc is None else acc + blk
            o_ref[lag + TC - 1] = acc

    return _qtn_call(body, a, b, None, (2 * TC - 1, 128, 128), [], name)


def _scan_row(i, rb, ncr, reverse):
    if not reverse:
        return i
    return jnp.where(i < ncr, ncr - 1 - i, rb - 1 - (i - ncr))


def _swap_re_im(h):
    half = QW // 2
    return jnp.concatenate([h[:, q * QW + (1 - k) * half:q * QW + (2 - k) * half] for q in range(4) for k in range(2)],
                           axis=1)


def chunk_scan(xs, lam_ab, ncr, reverse, name):
    bl, rb, _ = xs.shape

    def body(x_ref, l_ref, hp_ref):
        la, lb = l_ref[0:1, :], l_ref[1:2, :]

        def step(i, h):
            row = _scan_row(i, rb, ncr, reverse)
            hp_ref[pl.ds(row, 1), :] = h
            return la * h + lb * _swap_re_im(h) + x_ref[pl.ds(row, 1), :]

        lax.fori_loop(0, rb, step, jnp.zeros((1, ROW_W), F32))

    blk = pl.BlockSpec((None, rb, ROW_W), lambda b: (b, 0, 0))
    return pl.pallas_call(
        body, name=name, grid=(bl,),
        in_specs=[blk, pl.BlockSpec((8, ROW_W), lambda b: (0, 0))], out_specs=blk,
        out_shape=jax.ShapeDtypeStruct((bl, rb, ROW_W), F32),
        compiler_params=_cp(("arbitrary",)),
    )(xs, lam_ab)


def chunk_scan_bwd(dhp, hp, lam_ab, ncr, reverse, name):
    bl, rb, _ = dhp.shape

    def body(d_ref, hp_ref, l_ref, g_ref, dl_ref):
        la, lb = l_ref[0:1, :], l_ref[1:2, :]

        dl_ref[...] = jnp.zeros_like(dl_ref)

        def step(n, g):
            row = _scan_row(rb - 1 - n, rb, ncr, reverse)
            g_ref[pl.ds(row, 1), :] = g
            pv = hp_ref[pl.ds(row, 1), :]
            dl_ref[0:1, :] += g * pv
            dl_ref[1:2, :] += g * _swap_re_im(pv)
            return d_ref[pl.ds(row, 1), :] + la * g + _swap_re_im(lb * g)

        lax.fori_loop(0, rb, step, jnp.zeros((1, ROW_W), F32))

    blk = pl.BlockSpec((None, rb, ROW_W), lambda b: (b, 0, 0))
    return pl.pallas_call(
        body, name=name, grid=(bl,),
        in_specs=[blk, blk, pl.BlockSpec((8, ROW_W), lambda b: (0, 0))],
        out_specs=[blk, pl.BlockSpec((None, 8, ROW_W), lambda b: (b, 0, 0))],
        out_shape=[jax.ShapeDtypeStruct((bl, rb, ROW_W), F32), jax.ShapeDtypeStruct((bl, 8, ROW_W), F32)],
        compiler_params=_cp(("arbitrary",)),
    )(dhp, hp, lam_ab)


def _quarter_rows(v):
    e = v.shape[0]
    return v.reshape(e, 4, 8, SSM_P, SSM_H).transpose(0, 1, 2, 4, 3).reshape(e, 4, 8 * SSM_H, SSM_P)


def _token_state_map(vr, vi):
    return jnp.concatenate([_quarter_rows(vr), _quarter_rows(vi)], axis=-1).transpose(1, 0, 2, 3)


def ssm_build(lam_re, lam_im, log_dt, b_re, b_im, c_re, c_im, d):
    dt = jnp.exp(log_dt)[..., None]
    mag = jnp.exp(lam_re * dt)
    ang = lam_im * dt
    lr, li = mag * jnp.cos(ang), mag * jnp.sin(ang)
    den = lam_re * lam_re + lam_im * lam_im
    nr = lr - 1.0
    fr = (nr * lam_re + li * lam_im) / den
    fi = (li * lam_re - nr * lam_im) / den
    bbr = fr[..., None] * b_re - fi[..., None] * b_im
    bbi = fr[..., None] * b_im + fi[..., None] * b_re
    pr, pi = [jnp.ones_like(lr)], [jnp.zeros_like(lr)]
    for _ in range(TC):
        pr, pi = pr + [pr[-1] * lr - pi[-1] * li], pi + [pr[-1] * li + pi[-1] * lr]
    pr, pi = jnp.stack(pr), jnp.stack(pi)
    clr = c_re[None] * pr[:, :, :, None, :] - c_im[None] * pi[:, :, :, None, :]
    cli = c_re[None] * pi[:, :, :, None, :] + c_im[None] * pr[:, :, :, None, :]
    same_group = jnp.asarray(np.kron(np.eye(8), np.ones((SSM_H, SSM_H))), F32)
    ein = functools.partial(jnp.einsum, precision=HI)

    out, lag_blocks = {}, {}
    for k, name in ((0, "f"), (1, "r")):
        ar, ai = _quarter_rows(bbr[k][None])[0], _quarter_rows(bbi[k][None])[0]
        cr = clr[:TC, k].reshape(TC, 4, 8 * SSM_H, SSM_P)
        ci = cli[:TC, k].reshape(TC, 4, 8 * SSM_H, SSM_P)
        lag_blocks[k] = (ein('qap,nqbp->nqab', ar, cr) - ein('qap,nqbp->nqab', ai, ci)) * same_group
        es = [TC - 1 - s for s in range(TC)] if k == 0 else list(range(TC))
        sr = jnp.stack([pr[e, k][:, :, None] * bbr[k] - pi[e, k][:, :, None] * bbi[k] for e in es])
        si = jnp.stack([pr[e, k][:, :, None] * bbi[k] + pi[e, k][:, :, None] * bbr[k] for e in es])
        out["bs_" + name] = _token_state_map(sr, si)
        et = [t + 1 for t in range(TC)] if k == 0 else [TC - t for t in range(TC)]
        crt = jnp.stack([jnp.swapaxes(clr[e, k], 1, 2) for e in et])
        cit = jnp.stack([-jnp.swapaxes(cli[e, k], 1, 2) for e in et])
        out["cst_" + name] = _token_state_map(crt, cit)
        l8r, l8i = pr[TC, k].reshape(4, 1, QW // 2), pi[TC, k].reshape(4, 1, QW // 2)
        la = jnp.concatenate([l8r, l8r], axis=1).reshape(1, ROW_W)
        lb = jnp.concatenate([-l8i, l8i], axis=1).reshape(1, ROW_W)
        out["lam_" + name] = jnp.concatenate([la, lb, jnp.zeros((6, ROW_W), F32)], axis=0)
    skip = jnp.eye(8 * SSM_H, dtype=F32)[None] * d.reshape(4, 1, 8 * SSM_H)
    center = lag_blocks[0][0] + lag_blocks[1][0] + skip
    lags = [lag_blocks[1][n] for n in range(TC - 1, 0, -1)] + [center] + [lag_blocks[0][n] for n in range(1, TC)]
    out["lags"] = jnp.stack(lags, axis=1)
    return out


def ssm_operators(mats, tag):
    ops = {}
    ops["m"], ops["mt"] = assemble_tt(mats["lags"], "ssm_map_intra" + tag)
    for dname in ("f", "r"):
        ops["bs_" + dname], ops["bst_" + dname] = assemble_ts(mats["bs_" + dname], f"ssm_map_state_in_{dname}{tag}")
        ops["cst_" + dname], ops["cs_" + dname] = assemble_ts(mats["cst_" + dname], f"ssm_map_readout_{dname}{tag}")
    return ops


def ssm_forward(u3, mats, ops, ncr):
    bl, rb, _ = u3.shape
    u = u3.reshape(bl * rb, ROW_W)
    hps, terms = {}, [(u, ops["m"])]
    for dname, rev in (("f", False), ("r", True)):
        xs = qmm([(u, ops["bs_" + dname])], "ssm_state_in_" + dname)
        hp = chunk_scan(xs.reshape(bl, rb, ROW_W), mats["lam_" + dname], ncr, rev, "ssm_scan_" + dname)
        hps[dname] = hp.reshape(bl * rb, ROW_W)
        terms.append((hps[dname], ops["cs_" + dname]))
    return qmm(terms, "ssm_output").reshape(bl, rb, ROW_W), hps


def ssm_backward(dy3, u3, hps, mats, ops, ncr):
    bl, rb, _ = u3.shape
    u = u3.reshape(bl * rb, ROW_W)
    dyr = dy3.reshape(bl * rb, ROW_W)
    cot = {"lags": qtn_tt(u, dyr, "ssm_d_intra")}
    terms = [(dyr, ops["mt"])]
    for dname, rev in (("f", False), ("r", True)):
        dhp = qmm([(dyr, ops["cst_" + dname])], "ssm_dstate_" + dname)
        g, dl = chunk_scan_bwd(dhp.reshape(bl, rb, ROW_W), hps[dname].reshape(bl, rb, ROW_W), mats["lam_" + dname],
                               ncr, rev, "ssm_scan_bwd_" + dname)
        g = g.reshape(bl * rb, ROW_W)
        cot["lam_" + dname] = jnp.sum(dl, axis=0)
        cot["bs_" + dname] = qtn_ts(u, g, "ssm_d_state_in_" + dname)
        cot["cst_" + dname] = qtn_ts(dyr, hps[dname], "ssm_d_readout_" + dname)
        terms.append((g, ops["bst_" + dname]))
    return qmm(terms, "ssm_input_grad").reshape(bl, rb, ROW_W), cot


def mod_forward(act, w_mod, b_cols):
    nl, _, wc = w_mod.shape
    r = act.shape[0]

    def body(a_ref, w_ref, b_ref, o_ref):
        o_ref[...] = dot_nn(a_ref[...].astype(BF16), w_ref[...].astype(BF16)) + b_ref[...]

    return pl.pallas_call(
        body, name="mod_forward", grid=(nl,),
        in_specs=[pl.BlockSpec((r, D), lambda l: (0, 0)), pl.BlockSpec((None, D, wc), lambda l: (l, 0, 0)),
                  pl.BlockSpec((None, 1, wc), lambda l: (l, 0, 0))],
        out_specs=pl.BlockSpec((None, r, wc), lambda l: (l, 0, 0)),
        out_shape=jax.ShapeDtypeStruct((nl, r, wc), F32),
        compiler_params=_cp(("arbitrary",)),
    )(act, w_mod, b_cols)


def mod_backward(act, dmod, dctx, w_mod):
    nl, _, wc = w_mod.shape
    r = act.shape[0]

    def body(a_ref, d_ref, c_ref, w_ref, gw_ref, gc_ref):
        gw_ref[...] = dot_tn(a_ref[...].astype(BF16), d_ref[...].astype(BF16))
        gc_ref[...] = dot_nt(c_ref[...].astype(BF16), w_ref[...].astype(BF16))

    return pl.pallas_call(
        body, name="mod_backward", grid=(nl,),
        in_specs=[pl.BlockSpec((r, D), lambda l: (0, 0)), pl.BlockSpec((None, r, wc), lambda l: (l, 0, 0)),
                  pl.BlockSpec((None, 8, wc), lambda l: (l, 0, 0)), pl.BlockSpec((None, D, wc), lambda l: (l, 0, 0))],
        out_specs=[pl.BlockSpec((None, D, wc), lambda l: (l, 0, 0)), pl.BlockSpec((None, 8, D), lambda l: (l, 0, 0))],
        out_shape=[jax.ShapeDtypeStruct((nl, D, wc), F32), jax.ShapeDtypeStruct((nl, 8, D), F32)],
        compiler_params=_cp(("arbitrary",)),
    )(act, dmod, dctx, w_mod)


def _place():
    return lax.axis_index("x"), lax.axis_index("y"), lax.axis_index("c")


def all_gather_rows(arrs, name):
    n = len(arrs)
    rs = [a.shape[1] for a in arrs]

    def body(*refs):
        x_refs, o_refs = refs[:n], refs[n:2 * n]
        send_sems, recv_sems, local_sems = refs[2 * n:]
        x, y, c = _place()
        me, sibling = (x, y, c), (x, y, 1 - c)
        chips = [(1 - x, y), (x, 1 - y), (1 - x, 1 - y)]

        def rows(a, px, py, pc):
            return o_refs[a].at[:, pl.ds((4 * px + 2 * py + pc) * rs[a], rs[a]), :]

        def copy(a, k, block, to, src=None):
            return pltpu.make_async_remote_copy(
                src_ref=rows(a, *block) if src is None else src, dst_ref=rows(a, *block),
                send_sem=send_sems.at[a, k], recv_sem=recv_sems.at[a, k], device_id=to, device_id_type=MESH)

        mine = [pltpu.make_async_copy(x_refs[a], rows(a, *me), local_sems.at[a]) for a in range(n)]
        for cp in mine:
            cp.start()
        first = []
        for a in range(n):
            first.append(copy(a, 0, me, sibling, src=x_refs[a]))
            first += [copy(a, 1 + j, me, (*chip, c), src=x_refs[a]) for j, chip in enumerate(chips)]
        for cp in first:
            cp.start()
        passed = []
        for j, chip in enumerate(chips):
            for a in range(n):
                copy(a, 1 + j, (*chip, c), me).wait_recv()
                fwd = copy(a, 4 + j, (*chip, c), sibling)
                fwd.start()
                passed.append(fwd)
        for a in range(n):
            copy(a, 0, sibling, me).wait_recv()
            for j, chip in enumerate(chips):
                copy(a, 4 + j, (*chip, 1 - c), me).wait_recv()
        for cp in first + passed:
            cp.wait_send()
        for cp in mine:
            cp.wait()

    any_spec = pl.BlockSpec(memory_space=pl.ANY)
    return pl.pallas_call(
        body, name=name,
        in_specs=[any_spec] * n, out_specs=[any_spec] * n,
        out_shape=[jax.ShapeDtypeStruct((a.shape[0], N_DEV * a.shape[1], a.shape[2]), a.dtype) for a in arrs],
        scratch_shapes=[pltpu.SemaphoreType.DMA((n, 7)), pltpu.SemaphoreType.DMA((n, 7)), pltpu.SemaphoreType.DMA((n,))],
    )(*arrs)


def all_to_all_rows(arrs, name):
    n = len(arrs)
    rs = [a.shape[1] // N_DEV for a in arrs]
    flips = [(fx, fy, fc) for fx in (0, 1) for fy in (0, 1) for fc in (0, 1)][1:]

    def body(*refs):
        x_refs, o_refs = refs[:n], refs[n:2 * n]
        send_sems, recv_sems, local_sems = refs[2 * n:]
        x, y, c = _place()
        my_idx = 4 * x + 2 * y + c

        def block(a, idx):
            return x_refs[a].at[:, pl.ds(idx * rs[a], rs[a]), :]

        mine = [pltpu.make_async_copy(block(a, my_idx), o_refs[a].at[my_idx], local_sems.at[a]) for a in range(n)]
        for cp in mine:
            cp.start()
        sends = []
        for k, (fx, fy, fc) in enumerate(flips):
            px = 1 - x if fx else x
            py = 1 - y if fy else y
            pc = 1 - c if fc else c
            p_idx = 4 * px + 2 * py + pc
            for a in range(n):
                sends.append(pltpu.make_async_remote_copy(
                    src_ref=block(a, p_idx), dst_ref=o_refs[a].at[my_idx], send_sem=send_sems.at[a, k],
                    recv_sem=recv_sems.at[a, k], device_id=(px, py, pc), device_id_type=MESH))
        for cp in sends:
            cp.start()
        for k, (fx, fy, fc) in enumerate(flips):
            px = 1 - x if fx else x
            py = 1 - y if fy else y
            pc = 1 - c if fc else c
            p_idx = 4 * px + 2 * py + pc
            for a in range(n):
                pltpu.make_async_remote_copy(
                    src_ref=block(a, p_idx), dst_ref=o_refs[a].at[p_idx], send_sem=send_sems.at[a, k],
                    recv_sem=recv_sems.at[a, k], device_id=(px, py, pc), device_id_type=MESH).wait_recv()
        for cp in sends:
            cp.wait_send()
        for cp in mine:
            cp.wait()

    any_spec = pl.BlockSpec(memory_space=pl.ANY)
    return pl.pallas_call(
        body, name=name,
        in_specs=[any_spec] * n, out_specs=[any_spec] * n,
        out_shape=[jax.ShapeDtypeStruct((N_DEV, a.shape[0], r, a.shape[2]), a.dtype) for a, r in zip(arrs, rs)],
        scratch_shapes=[pltpu.SemaphoreType.DMA((n, 7)), pltpu.SemaphoreType.DMA((n, 7)), pltpu.SemaphoreType.DMA((n,))],
    )(*arrs)


def _peers():
    x, y, c = _place()
    out = []
    for fx in (0, 1):
        for fy in (0, 1):
            for fc in (0, 1):
                if fx or fy or fc:
                    px, py, pc = (1 - x if fx else x), (1 - y if fy else y), (1 - c if fc else c)
                    out.append(((px, py, pc), 4 * px + 2 * py + pc))
    return out, 4 * x + 2 * y + c


def _split_call(body, name, ins, n_sem_out, thru, extra_out_shape, extra_out_specs, sem_ins=(), after=None):
    hbm = pl.BlockSpec(memory_space=pltpu.HBM)
    sem = pl.BlockSpec(memory_space=pltpu.SEMAPHORE)
    n_thru = len(thru)
    tail_in = [sem] * len(sem_ins) + ([pl.BlockSpec(memory_space=pl.ANY)] if after is not None else [])
    return pl.pallas_call(
        body, name=name,
        out_shape=tuple(n_sem_out) + tuple(pltpu.HBM(a.shape, a.dtype) for a in thru) + tuple(extra_out_shape),
        in_specs=[hbm] * n_thru + tail_in,
        out_specs=(sem,) * len(n_sem_out) + (hbm,) * n_thru + tuple(extra_out_specs),
        input_output_aliases={i: i + len(n_sem_out) for i in range(n_thru)},
        compiler_params=pltpu.CompilerParams(has_side_effects=pltpu.SideEffectType.DATAFLOW_SIDE_EFFECTING),
    )(*ins, *sem_ins, *([after] if after is not None else []))


def gather_start(shards, after, name):
    n = len(shards)
    rs = [a.shape[1] for a in shards]
    lands = [lax.empty((a.shape[0], N_DEV * a.shape[1], a.shape[2]), a.dtype) for a in shards]

    def body(*refs):
        x_refs, land_refs = refs[:n], refs[n:2 * n]
        send_sems, recv_sems = refs[2 * n + 1], refs[2 * n + 2]
        peers, my_idx = _peers()
        for k, (peer, _) in enumerate(peers):
            for a in range(n):
                pltpu.make_async_remote_copy(
                    src_ref=x_refs[a], dst_ref=land_refs[a].at[:, pl.ds(my_idx * rs[a], rs[a]), :],
                    send_sem=send_sems.at[a * 7 + k], recv_sem=recv_sems.at[a * 7 + k], device_id=peer,
                    device_id_type=MESH).start()
        refs[-1][...] = jnp.zeros_like(refs[-1])

    ins = [pltpu.with_memory_space_constraint(a, pltpu.HBM) for a in list(shards) + lands]
    outs = _split_call(body, name, ins, [pltpu.SemaphoreType.DMA((n * 7,))] * 2, ins,
                       [jax.ShapeDtypeStruct((8, 128), F32)], [pl.BlockSpec(memory_space=pltpu.VMEM)], after=after)
    return outs[0], outs[1], list(outs[2:2 + n]), list(outs[2 + n:2 + 2 * n]), outs[-1]


def gather_wait(send_sems, recv_sems, shards, lands, after, name):
    n = len(shards)
    rs = [a.shape[1] for a in shards]

    def body(*refs):
        x_refs, land_refs = refs[:n], refs[n:2 * n]
        s_sems, r_sems = refs[2 * n], refs[2 * n + 1]
        peers, _ = _peers()
        for k, (peer, p_idx) in enumerate(peers):
            for a in range(n):
                copy = pltpu.make_async_remote_copy(
                    src_ref=x_refs[a], dst_ref=land_refs[a].at[:, pl.ds(p_idx * rs[a], rs[a]), :],
                    send_sem=s_sems.at[a * 7 + k], recv_sem=r_sems.at[a * 7 + k], device_id=peer, device_id_type=MESH)
                copy.wait_send()
                copy.wait_recv()

    outs = _split_call(body, name, list(shards) + list(lands), [], list(shards) + list(lands), [], [],
                       sem_ins=(send_sems, recv_sems), after=after)
    my_idx = 4 * lax.axis_index("x") + 2 * lax.axis_index("y") + lax.axis_index("c")
    return [lax.dynamic_update_slice_in_dim(z, s, my_idx * r, axis=1) for z, s, r in zip(outs[n:], outs[:n], rs)]


def scatter_start(arrs, name):
    n = len(arrs)
    rs = [a.shape[1] // N_DEV for a in arrs]
    lands = [lax.empty((N_DEV, a.shape[0], r, a.shape[2]), a.dtype) for a, r in zip(arrs, rs)]

    def body(*refs):
        x_refs, land_refs = refs[:n], refs[n:2 * n]
        send_sems, recv_sems = refs[2 * n], refs[2 * n + 1]
        token = refs[-1]
        peers, my_idx = _peers()
        for k, (peer, p_idx) in enumerate(peers):
            for a in range(n):
                pltpu.make_async_remote_copy(
                    src_ref=x_refs[a].at[:, pl.ds(p_idx * rs[a], rs[a]), :], dst_ref=land_refs[a].at[my_idx],
                    send_sem=send_sems.at[a * 7 + k], recv_sem=recv_sems.at[a * 7 + k], device_id=peer,
                    device_id_type=MESH).start()
        token[...] = jnp.zeros_like(token)

    hbm = pl.BlockSpec(memory_space=pltpu.HBM)
    sem = pl.BlockSpec(memory_space=pltpu.SEMAPHORE)
    outs = pl.pallas_call(
        body, name=name,
        out_shape=(pltpu.SemaphoreType.DMA((n * 7,)), pltpu.SemaphoreType.DMA((n * 7,)))
        + tuple(pltpu.HBM(a.shape, a.dtype) for a in arrs) + tuple(pltpu.HBM(z.shape, z.dtype) for z in lands)
        + (jax.ShapeDtypeStruct((8, 128), F32),),
        in_specs=[hbm] * (2 * n),
        out_specs=(sem, sem) + (hbm,) * (2 * n) + (pl.BlockSpec(memory_space=pltpu.VMEM),),
        input_output_aliases={i: i + 2 for i in range(2 * n)},
        compiler_params=pltpu.CompilerParams(has_side_effects=pltpu.SideEffectType.DATAFLOW_SIDE_EFFECTING),
    )(*[pltpu.with_memory_space_constraint(a, pltpu.HBM) for a in arrs],
      *[pltpu.with_memory_space_constraint(z, pltpu.HBM) for z in lands])
    return outs[0], outs[1], list(outs[2:2 + n]), list(outs[2 + n:2 + 2 * n]), outs[-1]


def scatter_wait(send_sems, recv_sems, arrs, lands, after, name):
    n = len(arrs)
    rs = [a.shape[1] // N_DEV for a in arrs]

    def body(*refs):
        x_refs, land_refs = refs[:n], refs[n:2 * n]
        s_sems, r_sems = refs[2 * n], refs[2 * n + 1]
        peers, my_idx = _peers()
        for k, (peer, p_idx) in enumerate(peers):
            for a in range(n):
                copy = pltpu.make_async_remote_copy(
                    src_ref=x_refs[a].at[:, pl.ds(p_idx * rs[a], rs[a]), :], dst_ref=land_refs[a].at[p_idx],
                    send_sem=s_sems.at[a * 7 + k], recv_sem=r_sems.at[a * 7 + k], device_id=peer, device_id_type=MESH)
                copy.wait_send()
                copy.wait_recv()

    hbm = pl.BlockSpec(memory_space=pltpu.HBM)
    sem = pl.BlockSpec(memory_space=pltpu.SEMAPHORE)
    outs = pl.pallas_call(
        body, name=name,
        out_shape=tuple(pltpu.HBM(a.shape, a.dtype) for a in arrs) + tuple(pltpu.HBM(z.shape, z.dtype) for z in lands),
        in_specs=[hbm] * (2 * n) + [sem, sem, pl.BlockSpec(memory_space=pl.ANY)],
        out_specs=(hbm,) * (2 * n),
        input_output_aliases={i: i for i in range(2 * n)},
        compiler_params=pltpu.CompilerParams(has_side_effects=pltpu.SideEffectType.DATAFLOW_SIDE_EFFECTING),
    )(*arrs, *lands, send_sems, recv_sems, after)
    return list(outs[:n]), list(outs[n:])


def _row_tile(rows, cap):
    best = None
    for t in range(16, min(rows, cap) + 1, 16):
        if rows % t == 0:
            best = t
    return rows if best is None else best


def adamw(w, gparts, m, v, name):
    per_layer = isinstance(gparts, (list, tuple))
    glist = list(gparts) if per_layer else [gparts]
    n, _, ra, cb = glist[0].shape
    nl = w.shape[0]
    ng = len(glist)
    ta = _row_tile(ra, max(8, (1 << 19) // (cb * n)))

    def slot_sum(g_ref):
        g = g_ref[0].astype(F32)
        for p in range(1, n):
            g = g + g_ref[p].astype(F32)
        return g

    def body(*refs):
        w_ref, g_refs = refs[0], refs[1:1 + ng]
        m_ref, v_ref, go_ref, d_ref, mo_ref, vo_ref = refs[1 + ng:]
        g = slot_sum(g_refs[0])
        for layer in range(1, ng):
            g = jnp.where(pl.program_id(0) == layer, slot_sum(g_refs[layer]), g)
        mn = ADAM_B1 * m_ref[...] + (1.0 - ADAM_B1) * g
        vn = ADAM_B2 * v_ref[...] + (1.0 - ADAM_B2) * jnp.square(g)
        m_hat = mn / (1.0 - ADAM_B1 ** ADAM_STEP)
        v_hat = vn / (1.0 - ADAM_B2 ** ADAM_STEP)
        go_ref[...] = g
        d_ref[...] = -ADAM_LR * (m_hat / (jnp.sqrt(v_hat) + ADAM_EPS) + ADAM_WD * w_ref[...])
        mo_ref[...] = mn
        vo_ref[...] = vn

    blk = pl.BlockSpec((None, ta, cb), lambda l, i: (l, i, 0))
    if per_layer:
        gblk = pl.BlockSpec((n, None, ta, cb), lambda l, i: (0, 0, i, 0))
    else:
        gblk = pl.BlockSpec((n, None, ta, cb), lambda l, i: (0, l, i, 0))
    shp = jax.ShapeDtypeStruct((nl, ra, cb), F32)
    return pl.pallas_call(
        body, name=name, grid=(nl, ra // ta),
        in_specs=[blk] + [gblk] * ng + [blk, blk], out_specs=[blk] * 4, out_shape=[shp] * 4,
        compiler_params=_cp(("arbitrary", "arbitrary")),
    )(w, *glist, m, v)


def _sincos_2d(rows, cols, dim):
    quarter = dim // 4
    omega = 1.0 / (10000.0 ** (jnp.arange(quarter, dtype=F32) / quarter))
    r = jnp.arange(rows, dtype=F32)[:, None] * omega
    cc = jnp.arange(cols, dtype=F32)[:, None] * omega
    er = jnp.concatenate([jnp.sin(r), jnp.cos(r)], axis=-1)
    ec = jnp.concatenate([jnp.sin(cc), jnp.cos(cc)], axis=-1)
    pe = jnp.concatenate([jnp.broadcast_to(er[:, None, :], (rows, cols, dim // 2)),
                          jnp.broadcast_to(ec[None, :, :], (rows, cols, dim // 2))], axis=-1)
    return pe.reshape(rows * cols, dim)


def _pool_constants():
    nw = len(POOL_WINDOWS)
    band = np.zeros((2, nw, TT, TT), np.float32)
    icnt = np.zeros((2, TT, C_W), np.float32)
    for kind, n in ((0, TT), (1, GRID_W)):
        for i, w in enumerate(POOL_WINDOWS):
            for t in range(TT):
                base, tl = (t // n) * n, t % n
                lo = min(max(tl - w // 2, 0), n)
                hi = min(max(tl - w // 2 + w, 0), n)
                band[kind, i, t, base + lo:base + hi] = 1.0
                icnt[kind, t, i * (C_W // nw):(i + 1) * (C_W // nw)] = 1.0 / (hi - lo)
    return jnp.asarray(band, BF16), jnp.asarray(icnt, F32)


def _block_diag(blocks):
    n, a, _ = blocks.shape
    return jnp.einsum('gab,gh->gahb', blocks, jnp.eye(n, dtype=F32), precision=HI).reshape(n * a, n * a)


def _block_diag_parts(mat, n):
    a = mat.shape[0] // n
    m4 = mat.reshape(n, a, n, a)
    return jnp.stack([m4[g, :, g, :] for g in range(n)])


_SMALL = ("c_ctx", "b_mod", "norm_mix_pre", "norm_mix_post", "norm_ffn_pre", "norm_ffn_post", "sgu_w", "sgu_b",
          "ssm_lam_re", "ssm_lam_im", "ssm_log_dt", "ssm_b_re", "ssm_b_im", "ssm_c_re", "ssm_c_im", "ssm_d",
          "glu_b", "pool_w", "pool_scale")
_WEIGHTS = ("c_ctx", "w_mod", "b_mod", "norm_mix_pre", "norm_mix_post", "norm_ffn_pre", "norm_ffn_post", "w_in", "w_out",
            "sgu_w", "sgu_b", "ssm_lam_re", "ssm_lam_im", "ssm_log_dt", "ssm_b_re", "ssm_b_im", "ssm_c_re", "ssm_c_im",
            "ssm_d", "glu_w", "glu_b", "pool_w", "pool_scale", "ffn_w_gate", "ffn_w_up", "ffn_w_down")


def _pack_rows(a):
    flat = a.reshape(-1)
    rows = -(-flat.shape[0] // D)
    rows8 = -(-rows // 8) * 8
    return jnp.pad(flat, (0, rows8 * D - flat.shape[0])).reshape(rows8, D)


def _pack(tree):
    packed = jnp.concatenate([_pack_rows(tree[k]) for k in _SMALL], axis=0)
    return jnp.pad(packed, ((0, -packed.shape[0] % 64), (0, 0)))


def _unpack(packed, like):
    out, at = {}, 0
    for k in _SMALL:
        size = int(np.prod(like[k].shape))
        rows8 = -(-(-(-size // D)) // 8) * 8
        out[k] = packed[at:at + rows8].reshape(-1)[:size].reshape(like[k].shape)
        at += rows8
    return out


def kernel(x, c, ctx, c_ctx, w_mod, b_mod, norm_mix_pre, norm_mix_post, norm_ffn_pre, norm_ffn_post, w_in, w_out, sgu_w, sgu_b, ssm_lam_re, ssm_lam_im, ssm_log_dt, ssm_b_re, ssm_b_im, ssm_c_re, ssm_c_im, ssm_d, glu_w, glu_b, pool_w, pool_scale, ffn_w_gate, ffn_w_up, ffn_w_down, loss_target, m_c_ctx, m_w_mod, m_b_mod, m_norm_mix_pre, m_norm_mix_post, m_norm_ffn_pre, m_norm_ffn_post, m_w_in, m_w_out, m_sgu_w, m_sgu_b, m_ssm_lam_re, m_ssm_lam_im, m_ssm_log_dt, m_ssm_b_re, m_ssm_b_im, m_ssm_c_re, m_ssm_c_im, m_ssm_d, m_glu_w, m_glu_b, m_pool_w, m_pool_scale, m_ffn_w_gate, m_ffn_w_up, m_ffn_w_down, v_c_ctx, v_w_mod, v_b_mod, v_norm_mix_pre, v_norm_mix_post, v_norm_ffn_pre, v_norm_ffn_post, v_w_in, v_w_out, v_sgu_w, v_sgu_b, v_ssm_lam_re, v_ssm_lam_im, v_ssm_log_dt, v_ssm_b_re, v_ssm_b_im, v_ssm_c_re, v_ssm_c_im, v_ssm_d, v_glu_w, v_glu_b, v_pool_w, v_pool_scale, v_ffn_w_gate, v_ffn_w_up, v_ffn_w_down):
    wts = dict(c_ctx=c_ctx, w_mod=w_mod, b_mod=b_mod, norm_mix_pre=norm_mix_pre, norm_mix_post=norm_mix_post,
               norm_ffn_pre=norm_ffn_pre, norm_ffn_post=norm_ffn_post, w_in=w_in, w_out=w_out, sgu_w=sgu_w, sgu_b=sgu_b,
               ssm_lam_re=ssm_lam_re, ssm_lam_im=ssm_lam_im, ssm_log_dt=ssm_log_dt, ssm_b_re=ssm_b_re, ssm_b_im=ssm_b_im,
               ssm_c_re=ssm_c_re, ssm_c_im=ssm_c_im, ssm_d=ssm_d, glu_w=glu_w, glu_b=glu_b, pool_w=pool_w,
               pool_scale=pool_scale, ffn_w_gate=ffn_w_gate, ffn_w_up=ffn_w_up, ffn_w_down=ffn_w_down)
    mom_m = dict(c_ctx=m_c_ctx, w_mod=m_w_mod, b_mod=m_b_mod, norm_mix_pre=m_norm_mix_pre, norm_mix_post=m_norm_mix_post,
                 norm_ffn_pre=m_norm_ffn_pre, norm_ffn_post=m_norm_ffn_post, w_in=m_w_in, w_out=m_w_out, sgu_w=m_sgu_w,
                 sgu_b=m_sgu_b, ssm_lam_re=m_ssm_lam_re, ssm_lam_im=m_ssm_lam_im, ssm_log_dt=m_ssm_log_dt,
                 ssm_b_re=m_ssm_b_re, ssm_b_im=m_ssm_b_im, ssm_c_re=m_ssm_c_re, ssm_c_im=m_ssm_c_im, ssm_d=m_ssm_d,
                 glu_w=m_glu_w, glu_b=m_glu_b, pool_w=m_pool_w, pool_scale=m_pool_scale, ffn_w_gate=m_ffn_w_gate,
                 ffn_w_up=m_ffn_w_up, ffn_w_down=m_ffn_w_down)
    mom_v = dict(c_ctx=v_c_ctx, w_mod=v_w_mod, b_mod=v_b_mod, norm_mix_pre=v_norm_mix_pre, norm_mix_post=v_norm_mix_post,
                 norm_ffn_pre=v_norm_ffn_pre, norm_ffn_post=v_norm_ffn_post, w_in=v_w_in, w_out=v_w_out, sgu_w=v_sgu_w,
                 sgu_b=v_sgu_b, ssm_lam_re=v_ssm_lam_re, ssm_lam_im=v_ssm_lam_im, ssm_log_dt=v_ssm_log_dt,
                 ssm_b_re=v_ssm_b_re, ssm_b_im=v_ssm_b_im, ssm_c_re=v_ssm_c_re, ssm_c_im=v_ssm_c_im, ssm_d=v_ssm_d,
                 glu_w=v_glu_w, glu_b=v_glu_b, pool_w=v_pool_w, pool_scale=v_pool_scale, ffn_w_gate=v_ffn_w_gate,
                 ffn_w_up=v_ffn_w_up, ffn_w_down=v_ffn_w_down)

    bl, seq, _ = x.shape
    n_ctx = ctx.shape[1]
    assert n_ctx == TT and seq % TT == 0 and seq % GRID_W == 0
    depth = w_in.shape[0]
    nc = n_ctx // TT
    ncr = n_ctx // TC
    s_all = n_ctx + seq
    nt = s_all // TT
    t_all = bl * s_all
    n_batch = bl * N_DEV
    my_idx = 4 * lax.axis_index("x") + 2 * lax.axis_index("y") + lax.axis_index("c")
    wc = w_mod.shape[2]

    c_rows = jnp.pad(c, ((0, 8 - bl), (0, 0))) if bl < 8 else c
    rc = c_rows.shape[0]
    (c_all,) = all_gather_rows([c_rows[None]], "gather_c")
    c_all = c_all[0].reshape(N_DEV, rc, D)[:, :bl].reshape(n_batch, D)
    r_act = -(-(n_batch + 1) // 16) * 16
    pre_act = jnp.concatenate([c_all, c_ctx[None, :], jnp.zeros((r_act - n_batch - 1, D), F32)], axis=0)
    act = jax.nn.silu(pre_act)
    b_cols = lax.dynamic_slice_in_dim(b_mod, my_idx * wc, wc, axis=1)[:, None, :]
    mod_cols = mod_forward(act, w_mod, b_cols)
    (mod_all,) = all_gather_rows([mod_cols], "gather_mod")
    mod_all = mod_all.reshape(depth, N_DEV, r_act, wc).transpose(0, 2, 1, 3).reshape(depth, r_act, 6, D)
    mod_lat = lax.dynamic_slice_in_dim(mod_all, my_idx * bl, bl, axis=1)
    mod_ctx = jnp.broadcast_to(mod_all[:, n_batch:n_batch + 1], (depth, bl, 6, D))
    mods = jnp.pad(jnp.stack([mod_ctx, mod_lat], axis=2), ((0, 0), (0, 0), (0, 0), (0, 2), (0, 0)))

    tr = lambda a: jnp.swapaxes(a, 1, 2).astype(BF16)
    shards = dict(w_in=tr(w_in), w_out=w_out.astype(BF16), glu_w=glu_w.astype(BF16), gate=tr(ffn_w_gate),
                  up=tr(ffn_w_up), down=ffn_w_down.astype(BF16))
    mix_keys, ffn_keys = ("w_in", "w_out", "glu_w"), ("gate", "up", "down")
    layer = lambda k, i: shards[k][i:i + 1]
    full = [dict() for _ in range(depth)]
    for k, g in zip(mix_keys, all_gather_rows([layer(k, 0) for k in mix_keys], "gather_mix_weights_0")):
        full[0][k] = g[0]
    first_done = mods[0, 0, 0, 0:1, 0:128] + full[0]["w_in"][0:1, 0:128].astype(F32)
    weights_in_flight = {0: (ffn_keys, gather_start([layer(k, 0) for k in ffn_keys], first_done, "gather_start_ffn_0"))}
    for i in range(1, depth):
        prev_token = weights_in_flight[i - 1][1][4]
        weights_in_flight[i] = (mix_keys + ffn_keys, gather_start([layer(k, i) for k in mix_keys + ffn_keys], prev_token,
                                                                  f"gather_start_layer_{i}"))
    start_token = sum(fl[1][4][0:1, 0:1] for fl in weights_in_flight.values())

    def land_weights(i, after, name):
        keys, (send_sems, recv_sems, sent, lands, _) = weights_in_flight[i]
        for k, g in zip(keys, gather_wait(send_sems, recv_sems, sent, lands, after, name)):
            full[i][k] = g[0]

    band, icnt = _pool_constants()
    seg_p = jnp.asarray(np.kron(np.eye(A_HEADS), np.full((A_W // A_HEADS,) * 2, A_HEADS / A_W)), BF16)
    pe = _sincos_2d(seq // GRID_W, GRID_W, D)
    xs = embed_tokens(x, ctx, pe)

    saved = []
    for i in range(depth):
        mats, ssm_vjp = jax.vjp(ssm_build, ssm_lam_re[i], ssm_lam_im[i], ssm_log_dt[i], ssm_b_re[i], ssm_b_im[i],
                                ssm_c_re[i], ssm_c_im[i], ssm_d[i])
        if i > 0:
            land_weights(i, xs, f"gather_wait_layer_{i}")
        cst = dict(sw=sgu_w[i].astype(BF16),
                   sbias=jnp.repeat(sgu_b[i].T, A_W // A_HEADS, axis=1),
                   seg_p=seg_p, band=band, icnt=icnt, wbd=_block_diag(pool_w[i]).astype(BF16),
                   pscale=pool_scale[i][None, :], glu_w=full[i]["glu_w"], glu_b=glu_b[i][None, :], w_out=full[i]["w_out"],
                   n2=norm_mix_post[i][None, :])
        n1, n3, n4 = norm_mix_pre[i][None, :], norm_ffn_pre[i][None, :], norm_ffn_post[i][None, :]
        if i == 0:
            n1 = n1 + start_token
        za, zu, zp = pre_mix(xs, mods[i], n1, full[i]["w_in"], nc)
        ops = ssm_operators(mats, f"_{i}")
        ys, hps = ssm_forward(zu, mats, ops, ncr)
        x1, m_pre = post_mix(xs, za, zp, ys, mods[i], cst, nc)
        if i == 0:
            land_weights(0, x1, "gather_wait_ffn_0")
        x2, f_pre, gate_b, up_b = ffn_fwd(x1, mods[i], n3, n4, full[i]["gate"], full[i]["up"], full[i]["down"], n_ctx)
        saved.append(dict(xs=xs, za=za, zu=zu, zp=zp, ys=ys, hps=hps, x1=x1, m=m_pre, f=f_pre, gate=gate_b, up=up_b,
                          cst=cst, mats=mats,
                          ops=ops, ssm_vjp=ssm_vjp, n1=n1, n3=n3, n4=n4))
        xs = x2

    dx, loss_parts = loss_head(xs, loss_target, nc)
    loss = lax.psum(jnp.sum(loss_parts[:, :, 0, 0]), ("x", "y", "c"))

    grads = {k: [None] * depth for k in _WEIGHTS}
    big = {k: [None] * depth for k in ("w_in", "w_out", "glu_w", "ffn_w_gate", "ffn_w_up", "ffn_w_down")}
    dmods = [None] * depth
    scatter_groups = (("ffn_w_gate", "ffn_w_up", "ffn_w_down"), ("w_out", "glu_w"), ("w_in",))
    in_flight = []

    def send_grads(i, group):
        flight = scatter_start([big[k][i][None] for k in scatter_groups[group]], f"scatter_start_{i}_{group}")
        in_flight.append((i, group, flight))
        return flight[4][0:1, 0:1]

    flat = lambda a: a.reshape(t_all, a.shape[-1])
    for i in reversed(range(depth)):
        sv = saved[i]
        dx1, h2, df, act_b, dgate, dup, st_f = ffn_bwd(dx, sv["x1"], sv["f"], sv["gate"], sv["up"], mods[i], sv["n3"],
                                                       sv["n4"], full[i]["gate"], full[i]["up"], full[i]["down"], n_ctx)
        big["ffn_w_gate"][i] = tn_matmul(flat(dgate), flat(h2), f"grad_ffn_gate_{i}")
        big["ffn_w_up"][i] = tn_matmul(flat(dup), flat(h2), f"grad_ffn_up_{i}")
        big["ffn_w_down"][i] = tn_matmul(flat(act_b), flat(df), f"grad_ffn_down_{i}")
        cst_i = dict(sv["cst"], n2=sv["cst"]["n2"] + send_grads(i, 0))
        dza, dzp, dys, cat, dm, gg, dr, st_m, dsw, dsb, dwbd = post_mix_bwd(dx1, sv["m"], sv["za"], sv["zp"], sv["ys"],
                                                                            mods[i], cst_i, nc)
        big["w_out"][i] = tn_matmul(flat(cat), flat(dm), f"grad_w_out_{i}")
        big["glu_w"][i] = tn_matmul(flat(gg), flat(dr), f"grad_glu_w_{i}")
        mats_i = dict(sv["mats"], lam_f=sv["mats"]["lam_f"] + send_grads(i, 1))
        dzu, cot = ssm_backward(dys, sv["zu"], sv["hps"], mats_i, sv["ops"], ncr)
        (grads["ssm_lam_re"][i], grads["ssm_lam_im"][i], grads["ssm_log_dt"][i], grads["ssm_b_re"][i],
         grads["ssm_b_im"][i], grads["ssm_c_re"][i], grads["ssm_c_im"][i], grads["ssm_d"][i]) = sv["ssm_vjp"](cot)
        dx, h1, dz, st_p = pre_mix_bwd(dza, dzu, dzp, sv["xs"], dx1, mods[i], sv["n1"], full[i]["w_in"], nc)
        big["w_in"][i] = tn_matmul(flat(dz), flat(h1), f"grad_w_in_{i}")

        tiles = lambda st, row: st[:, :, row, :]
        allsum = lambda st, row: jnp.sum(tiles(st, row), axis=(0, 1))
        grads["norm_mix_pre"][i] = allsum(st_p, 2)
        grads["norm_mix_post"][i] = allsum(st_m, 1)
        grads["norm_ffn_pre"][i] = allsum(st_f, 3)
        grads["norm_ffn_post"][i] = allsum(st_f, 4)
        misc = allsum(st_m, 2)
        grads["glu_b"][i] = misc[:B_W]
        grads["pool_scale"][i] = misc[B_W:B_W + C_W]
        grads["sgu_w"][i] = dsw
        grads["sgu_b"][i] = jnp.sum(dsb.reshape(CHUNK, A_HEADS, A_W // A_HEADS), axis=2).T
        grads["pool_w"][i] = _block_diag_parts(dwbd, len(POOL_WINDOWS))
        mix = (tiles(st_p, 0), tiles(st_p, 1), tiles(st_m, 0))
        d_lat = jnp.stack([jnp.sum(t[:, nc:], axis=1) for t in mix]
                          + [jnp.sum(tiles(st_f, r), axis=1) for r in (0, 1, 2)], axis=1).reshape(bl, 6 * D)
        d_ctx = jnp.concatenate([jnp.sum(t[:, :nc], axis=(0, 1)) for t in mix]
                                + [allsum(st_f, r) for r in (5, 6, 7)]).reshape(1, 6 * D)
        dmods[i] = jnp.concatenate([d_lat, d_ctx, jnp.zeros((8 - (bl + 1) % 8 if (bl + 1) % 8 else 0, 6 * D), F32)],
                                   axis=0)
        token = send_grads(i, 2)
        if i > 0:
            saved[i - 1]["n3"] = saved[i - 1]["n3"] + token
        else:
            dmods[i] = dmods[i] + token
    grad_x = dx[:, n_ctx:, :]

    dmod_local = jnp.stack(dmods)
    rd = dmod_local.shape[1]
    (dmod_all,) = all_gather_rows([dmod_local], "gather_dmod")
    dmod_cols = lax.dynamic_slice_in_dim(dmod_all, my_idx * wc, wc, axis=2).reshape(depth, N_DEV, rd, wc)
    d_lat_all = dmod_cols[:, :, :bl].reshape(depth, n_batch, wc)
    d_ctx_all = dmod_cols[:, 0, bl]
    for p in range(1, N_DEV):
        d_ctx_all = d_ctx_all + dmod_cols[:, p, bl]
    dmod_rows = jnp.concatenate([d_lat_all, d_ctx_all[:, None, :], jnp.zeros((depth, r_act - n_batch - 1, wc), F32)],
                                axis=1)
    dctx_rows = jnp.pad(d_ctx_all[:, None, :], ((0, 0), (0, 7), (0, 0)))
    g_w_mod, dact_ctx = mod_backward(act, dmod_rows, dctx_rows, w_mod)
    sig_c = jax.nn.sigmoid(c_ctx)
    dsilu_c = sig_c * (1.0 + c_ctx * (1.0 - sig_c))
    small_g = {k: (jnp.stack(grads[k]) if grads[k][0] is not None else None) for k in _SMALL}
    small_g["c_ctx"] = jnp.sum(dact_ctx[:, 0, :], axis=0) * dsilu_c
    small_g["b_mod"] = jnp.stack([jnp.sum(dmods[i][:bl + 1], axis=0) for i in range(depth)])

    packed_g = _pack(small_g).astype(BF16)
    rows_s = packed_g.shape[0]
    small_flight = gather_start([packed_g[None]], g_w_mod, "gather_start_small_grads")
    res = {k: [None] * 4 for k in _WEIGHTS}

    landed = {}
    for i, group, (send_sems, recv_sems, arrs_thru, lands_thru, _) in in_flight:
        sent, lands = scatter_wait(send_sems, recv_sems, arrs_thru, lands_thru, small_flight[4],
                                   f"scatter_wait_{i}_{group}")
        for k, a, z in zip(scatter_groups[group], sent, lands):
            r = a.shape[1] // N_DEV
            own = lax.dynamic_slice_in_dim(a, my_idx * r, r, axis=1)[None]
            landed[k, i] = lax.dynamic_update_slice_in_dim(z, own, my_idx, axis=0)
    for k in big:
        transposed = k in ("w_in", "ffn_w_gate", "ffn_w_up")
        view = (lambda a: jnp.swapaxes(a, 1, 2)) if transposed else (lambda a: a)
        o4 = adamw(view(wts[k]), [landed[k, i] for i in range(depth)], view(mom_m[k]), view(mom_v[k]), "adamw_" + k)
        res[k] = [view(o) for o in o4]
    res["w_mod"] = list(adamw(w_mod, g_w_mod[None], m_w_mod, v_w_mod, "adamw_w_mod"))

    (gathered,) = gather_wait(small_flight[0], small_flight[1], small_flight[2], small_flight[3], res["w_mod"][0],
                              "gather_wait_small_grads")
    small_w = {k: wts[k] for k in _SMALL}
    outs = adamw(_pack(small_w)[None], gathered.reshape(N_DEV, 1, rows_s, D), _pack({k: mom_m[k] for k in _SMALL})[None],
                 _pack({k: mom_v[k] for k in _SMALL})[None], "adamw_replicated")
    for slot, packed in enumerate(outs):
        un = _unpack(packed[0], small_w)
        for k in _SMALL:
            res[k][slot] = un[k]

    return (loss, grad_x, *[res[k][0] for k in _WEIGHTS], *[res[k][1] for k in _WEIGHTS],
            *[res[k][2] for k in _WEIGHTS], *[res[k][3] for k in _WEIGHTS])
```

```python
import functools
import math

import numpy as np
import jax
import jax.numpy as jnp
from jax import lax
from jax.experimental import pallas as pl
from jax.experimental.pallas import tpu as pltpu

F32 = jnp.float32
BF16 = jnp.bfloat16
HI = lax.Precision.HIGHEST
MESH = pl.DeviceIdType.MESH

D = 1024
D_IN = 1280
D_FF = 2816
A_W = 256
B_W = 512
C_W = 256
A_HEADS = 4
CHUNK = 128
SSM_G = 32
SSM_H = 16
SSM_P = 64
GRID_W = 64
POOL_WINDOWS = (2, 4, 8, 16)
EPS = 1e-6
N_DEV = 8

TT = 256
TC = 8
ROW_W = TC * B_W
QW = ROW_W // 4
GQ = 8
FF_CHUNK = 256
VMEM_LIMIT = 60 * 1024 * 1024

ADAM_LR = 0.001
ADAM_B1 = 0.9
ADAM_B2 = 0.999
ADAM_EPS = 1e-08
ADAM_WD = 0.01
ADAM_STEP = 10


def _cp(sem):
    return pltpu.CompilerParams(dimension_semantics=sem, vmem_limit_bytes=VMEM_LIMIT)


def dot_nn(a, b):
    return jnp.dot(a, b, preferred_element_type=F32)


def dot_nt(a, b):
    return lax.dot_general(a, b, (((1,), (1,)), ((), ())), preferred_element_type=F32)


def dot_tn(a, b):
    return lax.dot_general(a, b, (((0,), (0,)), ((), ())), preferred_element_type=F32)


def split_bf16(x):
    hi = x.astype(BF16)
    lo = (x - hi.astype(F32)).astype(BF16)
    return hi, lo


def gelu(x):
    return jax.nn.gelu(x)


def gelu_grad(x):
    c = math.sqrt(2.0 / math.pi)
    t = jnp.tanh(c * (x + 0.044715 * x * x * x))
    return 0.5 * (1.0 + t) + 0.5 * x * (1.0 - t * t) * c * (1.0 + 3.0 * 0.044715 * x * x)


def rms_stats(x):
    r = lax.rsqrt(jnp.mean(x * x, axis=-1, keepdims=True) + EPS)
    return r, x * r


def rms_bwd(r, xn, dxn):
    return r * (dxn - xn * jnp.mean(dxn * xn, axis=-1, keepdims=True))


def colsum(x):
    return jnp.sum(x, axis=0, keepdims=True)


def lane_group(width, group):
    return lax.broadcasted_iota(jnp.int32, (1, width), 1) // group


def _tile_spec(width):
    return pl.BlockSpec((None, TT, width), lambda b, j: (b, j, 0))


def _mod_spec(nc):
    return pl.BlockSpec((None, None, 8, D), lambda b, j: (b, jnp.where(j >= nc, 1, 0), 0, 0))


def _full_spec(shape):
    zeros = (0,) * len(shape)
    return pl.BlockSpec(shape, lambda b, j: zeros)


def _kind_spec(shape, nc):
    zeros = (0,) * len(shape)
    return pl.BlockSpec((None,) + shape, lambda b, j: (jnp.where(j >= nc, 1, 0),) + zeros)


def _stat_spec():
    return pl.BlockSpec((None, None, 8, D), lambda b, j: (b, j, 0, 0))


def _chunk_spec():
    return pl.BlockSpec((None, TT // TC, ROW_W), lambda b, j: (b, j, 0))


def _rows_to_chunks(val, scratch, out_ref):
    for cb in range(B_W // 128):
        scratch[cb] = val[:, cb * 128:(cb + 1) * 128]
    for s in range(TC):
        for cb in range(B_W // 128):
            lo = cb * QW + s * 128
            out_ref[:, lo:lo + 128] = scratch.at[cb][pl.ds(s, TT // TC, stride=TC), :]


def _chunks_to_rows(in_ref, scratch):
    for s in range(TC):
        for cb in range(B_W // 128):
            lo = cb * QW + s * 128
            scratch.at[cb][pl.ds(s, TT // TC, stride=TC), :] = in_ref[:, lo:lo + 128]
    return jnp.concatenate([scratch[cb] for cb in range(B_W // 128)], axis=1)


def _chunk_scratch():
    return pltpu.VMEM((B_W // 128, TT, 128), F32)


def embed_tokens(x, ctx, pe):
    bl, seq, _ = x.shape
    nc = ctx.shape[1] // TT
    nt = nc + seq // TT

    def body(ctx_ref, x_ref, pe_ref, o_ref):
        j = pl.program_id(1)

        @pl.when(j < nc)
        def _():
            o_ref[...] = ctx_ref[...]

        @pl.when(j >= nc)
        def _():
            o_ref[...] = x_ref[...] + pe_ref[...]

    return pl.pallas_call(
        body, name="embed_tokens", grid=(bl, nt),
        in_specs=[pl.BlockSpec((None, TT, D), lambda b, j: (b, jnp.minimum(j, nc - 1), 0)),
                  pl.BlockSpec((None, TT, D), lambda b, j: (b, jnp.maximum(j - nc, 0), 0)),
                  pl.BlockSpec((TT, D), lambda b, j: (jnp.maximum(j - nc, 0), 0))],
        out_specs=_tile_spec(D),
        out_shape=jax.ShapeDtypeStruct((bl, nt * TT, D), F32),
        compiler_params=_cp(("arbitrary", "arbitrary")),
    )(ctx, x, pe)


def pre_mix(xs, mod, n1, w_int, nc):
    bl, s, _ = xs.shape

    def body(x_ref, mod_ref, n_ref, w_ref, za_ref, zu_ref, zp_ref, u_s):
        r, xn = rms_stats(x_ref[...])
        h = xn * n_ref[...] * (1.0 + mod_ref[1:2, :]) + mod_ref[0:1, :]
        z = dot_nt(h.astype(BF16), w_ref[...])
        za_ref[...] = z[:, :2 * A_W]
        _rows_to_chunks(z[:, 2 * A_W:2 * A_W + B_W], u_s, zu_ref)
        zp_ref[...] = z[:, 2 * A_W + B_W:]

    return pl.pallas_call(
        body, name="pre_mix", grid=(bl, s // TT),
        in_specs=[_tile_spec(D), _mod_spec(nc), _full_spec((1, D)), _full_spec((D_IN, D))],
        out_specs=[_tile_spec(2 * A_W), _chunk_spec(), _tile_spec(C_W)],
        out_shape=[jax.ShapeDtypeStruct((bl, s, 2 * A_W), F32), jax.ShapeDtypeStruct((bl, s // TC, ROW_W), F32),
                   jax.ShapeDtypeStruct((bl, s, C_W), F32)],
        scratch_shapes=[_chunk_scratch()],
        compiler_params=_cp(("arbitrary", "arbitrary")),
    )(xs, mod, n1, w_int)


def _seg_mean(x, seg_p):
    hi, lo = split_bf16(x)
    return dot_nn(hi, seg_p) + dot_nn(lo, seg_p)


def _sgu_forward(za, sw_ref, sbias, seg_p):
    ge = gelu(za)
    u, v = ge[:, :A_W], ge[:, A_W:]
    dv = v - _seg_mean(v, seg_p)
    rs = lax.rsqrt(_seg_mean(dv * dv, seg_p) + EPS)
    vn = dv * rs
    head = lane_group(A_W, A_W // A_HEADS)
    parts = []
    for c2 in range(TT // CHUNK):
        vb = vn[c2 * CHUNK:(c2 + 1) * CHUNK].astype(BF16)
        sc = sbias
        for h in range(A_HEADS):
            sc = sc + jnp.where(head == h, dot_nn(sw_ref[h], vb), 0.0)
        parts.append(sc)
    sg = jnp.concatenate(parts, axis=0)
    return u * sg, (u, vn, rs, sg)


def _pool_forward(zp, band_ref, icnt, wbd, pscale):
    hi, lo = split_bf16(zp)
    grp = lane_group(C_W, C_W // len(POOL_WINDOWS))
    q = jnp.zeros_like(zp)
    for i in range(len(POOL_WINDOWS)):
        t = dot_nn(band_ref[i], hi) + dot_nn(band_ref[i], lo)
        q = jnp.where(grp == i, t, q)
    q = q * icnt - zp
    o = dot_nn(q.astype(BF16), wbd)
    return o * pscale, (q, o)


def _glu_forward(y, glu_w, glu_b):
    g = gelu(y)
    sg = jax.nn.sigmoid(dot_nn(g.astype(BF16), glu_w) + glu_b)
    return g * sg, (g, sg)


_MIX_CONST_SHAPES = dict(sw=(A_HEADS, CHUNK, CHUNK), sbias=(CHUNK, A_W), seg_p=(A_W, A_W), wbd=(C_W, C_W),
                         pscale=(1, C_W), glu_w=(B_W, B_W), glu_b=(1, B_W), w_out=(D, D), n2=(1, D))


def _mix_const_specs(nc):
    return ([_full_spec(_MIX_CONST_SHAPES[k]) for k in ("sw", "sbias", "seg_p")]
            + [_kind_spec((len(POOL_WINDOWS), TT, TT), nc), _kind_spec((TT, C_W), nc)]
            + [_full_spec(_MIX_CONST_SHAPES[k]) for k in ("wbd", "pscale", "glu_w", "glu_b", "w_out", "n2")])


def _mix_const_args(cst):
    return [cst[k] for k in ("sw", "sbias", "seg_p", "band", "icnt", "wbd", "pscale", "glu_w", "glu_b", "w_out", "n2")]


def post_mix(xs, za, zp, ys, mod, cst, nc):
    bl, s, _ = xs.shape

    def body(x_ref, za_ref, zp_ref, y_ref, mod_ref, sw_ref, sbias_ref, seg_ref, band_ref, icnt_ref, wbd_ref,
             ps_ref, gw_ref, gb_ref, wo_ref, n2_ref, x1_ref, m_ref, y_s):
        a, _ = _sgu_forward(za_ref[...], sw_ref, sbias_ref[...], seg_ref[...])
        p, _ = _pool_forward(zp_ref[...], band_ref, icnt_ref[...], wbd_ref[...], ps_ref[...])
        sm, _ = _glu_forward(_chunks_to_rows(y_ref, y_s), gw_ref[...], gb_ref[...])
        cat = jnp.concatenate([a, sm, p], axis=1).astype(BF16)
        m = dot_nn(cat, wo_ref[...])
        _, mn = rms_stats(m)
        m_ref[...] = m
        x1_ref[...] = x_ref[...] + mod_ref[2:3, :] * (mn * n2_ref[...])

    return pl.pallas_call(
        body, name="post_mix", grid=(bl, s // TT),
        in_specs=[_tile_spec(D), _tile_spec(2 * A_W), _tile_spec(C_W), _chunk_spec(), _mod_spec(nc)]
        + _mix_const_specs(nc),
        out_specs=[_tile_spec(D), _tile_spec(D)],
        out_shape=[jax.ShapeDtypeStruct((bl, s, D), F32), jax.ShapeDtypeStruct((bl, s, D), F32)],
        scratch_shapes=[_chunk_scratch()],
        compiler_params=_cp(("arbitrary", "arbitrary")),
    )(xs, za, zp, ys, mod, *_mix_const_args(cst))


def post_mix_bwd(dx1, m, za, zp, ys, mod, cst, nc):
    bl, s, _ = m.shape
    nt = s // TT

    def body(dx_ref, m_ref, za_ref, zp_ref, y_ref, mod_ref, sw_ref, sbias_ref, seg_ref, band_ref, icnt_ref,
             wbd_ref, ps_ref, gw_ref, gb_ref, wo_ref, n2_ref,
             dza_ref, dzp_ref, dy_ref, cat_ref, dm_ref, gg_ref, dr_ref, st_ref, dsw_ref, dsb_ref, dwbd_ref, y_s):
        first = jnp.logical_and(pl.program_id(0) == 0, pl.program_id(1) == 0)

        @pl.when(first)
        def _():
            dsw_ref[...] = jnp.zeros_like(dsw_ref)
            dsb_ref[...] = jnp.zeros_like(dsb_ref)
            dwbd_ref[...] = jnp.zeros_like(dwbd_ref)

        seg_p = seg_ref[...]
        za = za_ref[...]
        zp_v = zp_ref[...]
        yv = _chunks_to_rows(y_ref, y_s)
        a, (u, vn, rs, sg) = _sgu_forward(za, sw_ref, sbias_ref[...], seg_p)
        p, (q, o) = _pool_forward(zp_v, band_ref, icnt_ref[...], wbd_ref[...], ps_ref[...])
        sm, (g, sig) = _glu_forward(yv, gw_ref[...], gb_ref[...])
        cat_ref[...] = jnp.concatenate([a, sm, p], axis=1).astype(BF16)

        dx = dx_ref[...]
        g1 = mod_ref[2:3, :]
        n2 = n2_ref[...]
        mv = m_ref[...]
        rm, mn = rms_stats(mv)
        st_ref[...] = jnp.zeros_like(st_ref)
        st_ref[0:1, :] = colsum(dx * (mn * n2))
        st_ref[1:2, :] = colsum(dx * g1 * mn)
        dm = rms_bwd(rm, mn, dx * g1 * n2)
        dmb = dm.astype(BF16)
        dm_ref[...] = dmb
        dcat = dot_nt(dmb, wo_ref[...])
        da, dsm, dp = dcat[:, :A_W], dcat[:, A_W:A_W + B_W], dcat[:, A_W + B_W:]

        du = da * sg
        dsv = da * u
        head = lane_group(A_W, A_W // A_HEADS)
        dvn_parts = []
        dsb_acc = jnp.zeros((CHUNK, A_W), F32)
        for c2 in range(TT // CHUNK):
            dsc = dsv[c2 * CHUNK:(c2 + 1) * CHUNK]
            dsc_b = dsc.astype(BF16)
            vb = vn[c2 * CHUNK:(c2 + 1) * CHUNK].astype(BF16)
            dsb_acc = dsb_acc + dsc
            dvn_c = jnp.zeros((CHUNK, A_W), F32)
            for h in range(A_HEADS):
                dsw_ref[h] += dot_nt(jnp.where(head == h, dsc, 0.0).astype(BF16), vb)
                dvn_c = dvn_c + jnp.where(head == h, dot_tn(sw_ref[h], dsc_b), 0.0)
            dvn_parts.append(dvn_c)
        dsb_ref[...] += dsb_acc
        dvn = jnp.concatenate(dvn_parts, axis=0)
        dv = rs * (dvn - _seg_mean(dvn, seg_p) - vn * _seg_mean(dvn * vn, seg_p))
        dza_ref[...] = jnp.concatenate([du, dv], axis=1) * gelu_grad(za)

        ps = ps_ref[...]
        do = dp * ps
        dps = colsum(dp * o)
        dob = do.astype(BF16)
        dwbd_ref[...] += dot_tn(q.astype(BF16), dob)
        dq = dot_nt(dob, wbd_ref[...])
        hi, lo = split_bf16(dq * icnt_ref[...])
        grp = lane_group(C_W, C_W // len(POOL_WINDOWS))
        dzp = -dq
        for i in range(len(POOL_WINDOWS)):
            t = dot_tn(band_ref[i], hi) + dot_tn(band_ref[i], lo)
            dzp = dzp + jnp.where(grp == i, t, 0.0)
        dzp_ref[...] = dzp

        dr = dsm * g * sig * (1.0 - sig)
        drb = dr.astype(BF16)
        dr_ref[...] = drb
        gg_ref[...] = g.astype(BF16)
        dg = dsm * sig + dot_nt(drb, gw_ref[...])
        _rows_to_chunks(dg * gelu_grad(yv), y_s, dy_ref)
        st_ref[2:3, :] = jnp.concatenate([colsum(dr), dps, jnp.zeros((1, D - B_W - C_W), F32)], axis=1)

    acc = lambda shape: pl.BlockSpec(shape, lambda b, j: (0,) * len(shape))
    return pl.pallas_call(
        body, name="post_mix_bwd", grid=(bl, nt),
        in_specs=[_tile_spec(D), _tile_spec(D), _tile_spec(2 * A_W), _tile_spec(C_W), _chunk_spec(), _mod_spec(nc)]
        + _mix_const_specs(nc),
        out_specs=[_tile_spec(2 * A_W), _tile_spec(C_W), _chunk_spec(), _tile_spec(D), _tile_spec(D),
                   _tile_spec(B_W), _tile_spec(B_W), _stat_spec(),
                   acc((A_HEADS, CHUNK, CHUNK)), acc((CHUNK, A_W)), acc((C_W, C_W))],
        out_shape=[jax.ShapeDtypeStruct((bl, s, 2 * A_W), F32), jax.ShapeDtypeStruct((bl, s, C_W), F32),
                   jax.ShapeDtypeStruct((bl, s // TC, ROW_W), F32), jax.ShapeDtypeStruct((bl, s, D), BF16),
                   jax.ShapeDtypeStruct((bl, s, D), BF16), jax.ShapeDtypeStruct((bl, s, B_W), BF16),
                   jax.ShapeDtypeStruct((bl, s, B_W), BF16), jax.ShapeDtypeStruct((bl, nt, 8, D), F32),
                   jax.ShapeDtypeStruct((A_HEADS, CHUNK, CHUNK), F32), jax.ShapeDtypeStruct((CHUNK, A_W), F32),
                   jax.ShapeDtypeStruct((C_W, C_W), F32)],
        scratch_shapes=[_chunk_scratch()],
        compiler_params=_cp(("arbitrary", "arbitrary")),
    )(dx1, m, za, zp, ys, mod, *_mix_const_args(cst))


def pre_mix_bwd(dza, dzu, dzp, xs, dxres, mod, n1, w_int, nc):
    bl, s, _ = xs.shape
    nt = s // TT

    def body(dza_ref, dzu_ref, dzp_ref, x_ref, dres_ref, mod_ref, n_ref, w_ref, dx_ref, h_ref, dz_ref, st_ref, u_s):
        dz = jnp.concatenate([dza_ref[...], _chunks_to_rows(dzu_ref, u_s), dzp_ref[...]], axis=1).astype(BF16)
        dz_ref[...] = dz
        dh = dot_nn(dz, w_ref[...])
        r, xn = rms_stats(x_ref[...])
        n1v = n_ref[...]
        sc = mod_ref[1:2, :]
        xg = xn * n1v
        h_ref[...] = (xg * (1.0 + sc) + mod_ref[0:1, :]).astype(BF16)
        dyv = dh * (1.0 + sc)
        st_ref[...] = jnp.zeros_like(st_ref)
        st_ref[0:1, :] = colsum(dh)
        st_ref[1:2, :] = colsum(dh * xg)
        st_ref[2:3, :] = colsum(dyv * xn)
        dx_ref[...] = dres_ref[...] + rms_bwd(r, xn, dyv * n1v)

    return pl.pallas_call(
        body, name="pre_mix_bwd", grid=(bl, nt),
        in_specs=[_tile_spec(2 * A_W), _chunk_spec(), _tile_spec(C_W), _tile_spec(D), _tile_spec(D), _mod_spec(nc),
                  _full_spec((1, D)), _full_spec((D_IN, D))],
        out_specs=[_tile_spec(D), _tile_spec(D), _tile_spec(D_IN), _stat_spec()],
        out_shape=[jax.ShapeDtypeStruct((bl, s, D), F32), jax.ShapeDtypeStruct((bl, s, D), BF16),
                   jax.ShapeDtypeStruct((bl, s, D_IN), BF16), jax.ShapeDtypeStruct((bl, nt, 8, D), F32)],
        scratch_shapes=[_chunk_scratch()],
        compiler_params=_cp(("arbitrary", "arbitrary")),
    )(dza, dzu, dzp, xs, dxres, mod, n1, w_int)


def _ffn_tile(s):
    return 768 if s % 768 == 0 else TT


def _slabs(v, n_ctx):
    return (v,) if v.shape[0] == n_ctx else (v[:n_ctx], v[n_ctx:])


def _mod_rows(mod_ref, j, row):
    lat = mod_ref[1, row:row + 1, :]
    return jnp.where(j == 0, mod_ref[0, row:row + 1, :], lat), lat


def _by_slab(fn, n_ctx, *vals_and_rows):
    outs = []
    for s in range(len(_slabs(next(v for v in vals_and_rows if not isinstance(v, tuple)), n_ctx))):
        outs.append(fn(*[v[s] if isinstance(v, tuple) else _slabs(v, n_ctx)[s] for v in vals_and_rows]))
    return outs[0] if len(outs) == 1 else jnp.concatenate(outs, axis=0)


def _split_sum(v, n_ctx, j, st_ref, row):
    parts = [colsum(p) for p in _slabs(v, n_ctx)]
    first_is_ctx = j == 0
    rest = parts[1] if len(parts) > 1 else jnp.zeros_like(parts[0])
    st_ref[row:row + 1, :] = rest + jnp.where(first_is_ctx, 0.0, parts[0])
    st_ref[row + 5:row + 6, :] = jnp.where(first_is_ctx, parts[0], 0.0)


def ffn_fwd(x1, mod, n3, n4, wg_t, wu_t, wd, n_ctx):
    bl, s, _ = x1.shape
    tf = _ffn_tile(s)
    nk = D_FF // FF_CHUNK
    tile = pl.BlockSpec((None, tf, D), lambda b, j, k: (b, j, 0))
    modspec = pl.BlockSpec((None, 2, 8, D), lambda b, j, k: (b, 0, 0, 0))
    vec = pl.BlockSpec((1, D), lambda b, j, k: (0, 0))
    wspec = pl.BlockSpec((FF_CHUNK, D), lambda b, j, k: (k, 0))
    ftile = pl.BlockSpec((None, tf, FF_CHUNK), lambda b, j, k: (b, j, k))

    def body(x_ref, mod_ref, n3_ref, n4_ref, wg_ref, wu_ref, wd_ref, x2_ref, f_ref, gate_ref, up_ref, h_s, acc_s):
        j, k = pl.program_id(1), pl.program_id(2)

        @pl.when(k == 0)
        def _():
            _, xn = rms_stats(x_ref[...])
            n3 = n3_ref[...]
            h_s[...] = _by_slab(lambda v, sh, sc: (v * n3 * (1.0 + sc) + sh).astype(BF16), n_ctx, xn,
                                _mod_rows(mod_ref, j, 3), _mod_rows(mod_ref, j, 4))
            acc_s[...] = jnp.zeros_like(acc_s)

        h = h_s[...]
        gate = dot_nt(h, wg_ref[...])
        up = dot_nt(h, wu_ref[...])
        gate_ref[...] = gate.astype(BF16)
        up_ref[...] = up.astype(BF16)
        act = (gate * jax.nn.sigmoid(gate)) * up
        acc_s[...] += dot_nn(act.astype(BF16), wd_ref[...])

        @pl.when(k == nk - 1)
        def _():
            f = acc_s[...]
            f_ref[...] = f
            _, fn = rms_stats(f)
            n4 = n4_ref[...]
            x2_ref[...] = _by_slab(lambda xv, fv, g2: xv + g2 * (fv * n4), n_ctx, x_ref[...], fn, _mod_rows(mod_ref, j, 5))

    return pl.pallas_call(
        body, name="ffn_fwd", grid=(bl, s // tf, nk),
        in_specs=[tile, modspec, vec, vec, wspec, wspec, wspec],
        out_specs=[tile, tile, ftile, ftile],
        out_shape=[jax.ShapeDtypeStruct((bl, s, D), F32), jax.ShapeDtypeStruct((bl, s, D), F32),
                   jax.ShapeDtypeStruct((bl, s, D_FF), BF16), jax.ShapeDtypeStruct((bl, s, D_FF), BF16)],
        scratch_shapes=[pltpu.VMEM((tf, D), BF16), pltpu.VMEM((tf, D), F32)],
        compiler_params=_cp(("arbitrary", "arbitrary", "arbitrary")),
    )(x1, mod, n3, n4, wg_t, wu_t, wd)


def ffn_bwd(dx2, x1, f, gate_b, up_b, mod, n3, n4, wg_t, wu_t, wd, n_ctx):
    bl, s, _ = x1.shape
    tf = _ffn_tile(s)
    nt = s // tf
    nk = D_FF // FF_CHUNK
    tile = pl.BlockSpec((None, tf, D), lambda b, j, k: (b, j, 0))
    ftile = pl.BlockSpec((None, tf, FF_CHUNK), lambda b, j, k: (b, j, jnp.minimum(k, nk - 1)))
    modspec = pl.BlockSpec((None, 2, 8, D), lambda b, j, k: (b, 0, 0, 0))
    vec = pl.BlockSpec((1, D), lambda b, j, k: (0, 0))
    wspec = pl.BlockSpec((FF_CHUNK, D), lambda b, j, k: (jnp.minimum(k, nk - 1), 0))
    wprev = pl.BlockSpec((FF_CHUNK, D), lambda b, j, k: (jnp.maximum(k - 1, 0), 0))
    stat = pl.BlockSpec((None, None, 8, D), lambda b, j, k: (b, j, 0, 0))

    def body(dx_ref, x_ref, f_ref, gate_ref, up_ref, mod_ref, n3_ref, n4_ref, wd_ref, wgp_ref, wup_ref,
             dx1_ref, h_ref, df_ref, act_ref, dgate_ref, dup_ref, st_ref, df_s, acc_s, dgate_s, dup_s):
        j, k = pl.program_id(1), pl.program_id(2)

        @pl.when(k == 0)
        def _():
            dx = dx_ref[...]
            n4 = n4_ref[...]
            rf, fn = rms_stats(f_ref[...])
            st_ref[...] = jnp.zeros_like(st_ref)
            _split_sum(dx * (fn * n4), n_ctx, j, st_ref, 2)
            dxg = _by_slab(lambda dv, g2: dv * g2, n_ctx, dx, _mod_rows(mod_ref, j, 5))
            st_ref[4:5, :] = colsum(dxg * fn)
            df = rms_bwd(rf, fn, dxg * n4).astype(BF16)
            df_s[...] = df
            df_ref[...] = df
            _, xn = rms_stats(x_ref[...])
            n3 = n3_ref[...]
            h_ref[...] = _by_slab(lambda v, sh, sc: (v * n3 * (1.0 + sc) + sh).astype(BF16), n_ctx, xn,
                                  _mod_rows(mod_ref, j, 3), _mod_rows(mod_ref, j, 4))
            acc_s[...] = jnp.zeros_like(acc_s)
            dgate_s[1] = jnp.zeros((tf, FF_CHUNK), BF16)
            dup_s[1] = jnp.zeros((tf, FF_CHUNK), BF16)

        prev = (k + 1) % 2
        acc_s[...] += dot_nn(dgate_s[prev], wgp_ref[...]) + dot_nn(dup_s[prev], wup_ref[...])
        gate = gate_ref[...].astype(F32)
        up = up_ref[...].astype(F32)
        sg = jax.nn.sigmoid(gate)
        silu = gate * sg
        dact = dot_nt(df_s[...], wd_ref[...])
        act_ref[...] = (silu * up).astype(BF16)
        dgate = (dact * up * (sg * (1.0 + gate * (1.0 - sg)))).astype(BF16)
        dup = (dact * silu).astype(BF16)
        dgate_ref[...] = dgate
        dup_ref[...] = dup
        dgate_s[k % 2] = dgate
        dup_s[k % 2] = dup

        @pl.when(k == nk)
        def _():
            dh = acc_s[...]
            r, xn = rms_stats(x_ref[...])
            n3 = n3_ref[...]
            xg = xn * n3
            dyv = _by_slab(lambda dv, sc: dv * (1.0 + sc), n_ctx, dh, _mod_rows(mod_ref, j, 4))
            _split_sum(dh, n_ctx, j, st_ref, 0)
            _split_sum(dh * xg, n_ctx, j, st_ref, 1)
            st_ref[3:4, :] = colsum(dyv * xn)
            dx1_ref[...] = dx_ref[...] + rms_bwd(r, xn, dyv * n3)

    return pl.pallas_call(
        body, name="ffn_bwd", grid=(bl, nt, nk + 1),
        in_specs=[tile, tile, tile, ftile, ftile, modspec, vec, vec, wspec, wprev, wprev],
        out_specs=[tile, tile, tile, ftile, ftile, ftile, stat],
        out_shape=[jax.ShapeDtypeStruct((bl, s, D), F32), jax.ShapeDtypeStruct((bl, s, D), BF16),
                   jax.ShapeDtypeStruct((bl, s, D), BF16), jax.ShapeDtypeStruct((bl, s, D_FF), BF16),
                   jax.ShapeDtypeStruct((bl, s, D_FF), BF16), jax.ShapeDtypeStruct((bl, s, D_FF), BF16),
                   jax.ShapeDtypeStruct((bl, nt, 8, D), F32)],
        scratch_shapes=[pltpu.VMEM((tf, D), BF16), pltpu.VMEM((tf, D), F32),
                        pltpu.VMEM((2, tf, FF_CHUNK), BF16), pltpu.VMEM((2, tf, FF_CHUNK), BF16)],
        compiler_params=_cp(("arbitrary", "arbitrary", "arbitrary")),
    )(dx2, x1, f, gate_b, up_b, mod, n3, n4, wd, wg_t, wu_t)


def loss_head(xs, target, nc):
    bl, s, _ = xs.shape
    nt = s // TT

    def body(x_ref, t_ref, dx_ref, l_ref):
        j = pl.program_id(1)

        @pl.when(j < nc)
        def _():
            dx_ref[...] = jnp.zeros_like(dx_ref)
            l_ref[...] = jnp.zeros_like(l_ref)

        @pl.when(j >= nc)
        def _():
            e = x_ref[...] - t_ref[...]
            dx_ref[...] = e * (1.0 / D)
            tok = jnp.mean(e * e, axis=-1, keepdims=True)
            l_ref[...] = jnp.zeros_like(l_ref) + 0.5 * jnp.sum(tok, axis=0, keepdims=True)

    return pl.pallas_call(
        body, name="loss_head", grid=(bl, nt),
        in_specs=[_tile_spec(D), pl.BlockSpec((None, TT, D), lambda b, j: (b, jnp.maximum(j - nc, 0), 0))],
        out_specs=[_tile_spec(D), pl.BlockSpec((None, None, 8, 128), lambda b, j: (b, j, 0, 0))],
        out_shape=[jax.ShapeDtypeStruct((bl, s, D), F32), jax.ShapeDtypeStruct((bl, nt, 8, 128), F32)],
        compiler_params=_cp(("arbitrary", "arbitrary")),
    )(xs, target)


def tn_matmul(a, b, name):
    t, ka = a.shape
    n = b.shape[1]
    tk = ka if ka <= 1408 else ka // 2
    tt = next(x for x in (1024, 512, 256) if t % x == 0)
    nsteps = t // tt

    def body(a_ref, b_ref, o_ref, acc_s):
        @pl.when(pl.program_id(1) == 0)
        def _():
            acc_s[...] = jnp.zeros_like(acc_s)

        acc_s[...] += dot_tn(a_ref[...], b_ref[...])

        @pl.when(pl.program_id(1) == nsteps - 1)
        def _():
            o_ref[...] = acc_s[...].astype(BF16)

    return pl.pallas_call(
        body, name=name, grid=(ka // tk, nsteps),
        in_specs=[pl.BlockSpec((tt, tk), lambda i, s: (s, i)), pl.BlockSpec((tt, n), lambda i, s: (s, 0))],
        out_specs=pl.BlockSpec((tk, n), lambda i, s: (i, 0)),
        out_shape=jax.ShapeDtypeStruct((ka, n), BF16),
        scratch_shapes=[pltpu.VMEM((tk, n), F32)],
        compiler_params=_cp(("arbitrary", "arbitrary")),
    )(a, b)


def qmm(terms, name):
    r = terms[0][0].shape[0]
    rt = r // 2 if r % 16 == 0 and r >= 512 else r
    n = len(terms)

    def body(*refs):
        acc = None
        for k in range(n):
            y = dot_nn(refs[2 * k][...].astype(BF16), refs[2 * k + 1][...])
            acc = y if acc is None else acc + y
        refs[2 * n][...] = acc

    row = pl.BlockSpec((rt, QW), lambda q, i: (i, q))
    wspec = pl.BlockSpec((None, QW, QW), lambda q, i: (q, 0, 0))
    return pl.pallas_call(
        body, name=name, grid=(4, r // rt), in_specs=[row, wspec] * n, out_specs=row,
        out_shape=jax.ShapeDtypeStruct((r, ROW_W), F32),
        compiler_params=_cp(("arbitrary", "arbitrary")),
    )(*[x for term in terms for x in term])


def _same_group(rows, cols, row_group, col_group):
    ri = jnp.bitwise_and(lax.broadcasted_iota(jnp.int32, (rows, cols), 0) // row_group, GQ - 1)
    ci = jnp.bitwise_and(lax.broadcasted_iota(jnp.int32, (rows, cols), 1) // col_group, GQ - 1)
    return ri == ci


def _spread_matrix():
    m = np.zeros((2 * SSM_P, QW), np.float32)
    for reim in range(2):
        for g in range(GQ):
            for p in range(SSM_P):
                m[reim * SSM_P + p, reim * (QW // 2) + g * SSM_P + p] = 1.0
    return jnp.asarray(m, BF16)


def assemble_ts(v, name):
    def body(v_ref, f_ref, big_ref, bigt_ref):
        keep = _same_group(GQ * SSM_H, QW, SSM_H, SSM_P)
        for e in range(TC):
            hi, lo = split_bf16(v_ref[e])
            t = jnp.where(keep, dot_nn(hi, f_ref[...]) + dot_nn(lo, f_ref[...]), 0.0)
            big_ref[e * 128:(e + 1) * 128, :] = t.astype(BF16)
            bigt_ref[:, e * 128:(e + 1) * 128] = t.T.astype(BF16)

    return pl.pallas_call(
        body, name=name, grid=(4,),
        in_specs=[pl.BlockSpec((None, TC, 128, 128), lambda q: (q, 0, 0, 0)),
                  pl.BlockSpec((128, QW), lambda q: (0, 0))],
        out_specs=[pl.BlockSpec((None, QW, QW), lambda q: (q, 0, 0))] * 2,
        out_shape=[jax.ShapeDtypeStruct((4, QW, QW), BF16), jax.ShapeDtypeStruct((4, QW, QW), BF16)],
        compiler_params=_cp(("arbitrary",)),
    )(v, _spread_matrix())


def assemble_tt(lags, name):
    def body(l_ref, m_ref, mt_ref):
        blocks = [l_ref[n] for n in range(2 * TC - 1)]
        flipped = [b.T.astype(BF16) for b in blocks]
        blocks = [b.astype(BF16) for b in blocks]
        for s in range(TC):
            for t in range(TC):
                m_ref[s * 128:(s + 1) * 128, t * 128:(t + 1) * 128] = blocks[t - s + TC - 1]
                mt_ref[t * 128:(t + 1) * 128, s * 128:(s + 1) * 128] = flipped[t - s + TC - 1]

    return pl.pallas_call(
        body, name=name, grid=(4,),
        in_specs=[pl.BlockSpec((None, 2 * TC - 1, 128, 128), lambda q: (q, 0, 0, 0))],
        out_specs=[pl.BlockSpec((None, QW, QW), lambda q: (q, 0, 0))] * 2,
        out_shape=[jax.ShapeDtypeStruct((4, QW, QW), BF16)] * 2,
        compiler_params=_cp(("arbitrary",)),
    )(lags)


def _qtn_call(body, a, b, out_shape, out_block, extra, name):
    r = a.shape[0]
    col = pl.BlockSpec((r, QW), lambda q: (0, q))
    return pl.pallas_call(
        body, name=name, grid=(4,),
        in_specs=[col, col] + [pl.BlockSpec(x.shape, lambda q: (0, 0)) for x in extra],
        out_specs=pl.BlockSpec((None,) + out_block, lambda q: (q,) + (0,) * len(out_block)),
        out_shape=jax.ShapeDtypeStruct((4,) + out_block, F32),
        compiler_params=_cp(("arbitrary",)),
    )(a, b, *extra)


def qtn_ts(a, b, name):
    def body(a_ref, b_ref, f_ref, o_ref):
        full = dot_tn(a_ref[...].astype(BF16), b_ref[...].astype(BF16))
        keep = _same_group(GQ * SSM_H, QW, SSM_H, SSM_P)
        for e in range(TC):
            hi, lo = split_bf16(jnp.where(keep, full[e * 128:(e + 1) * 128, :], 0.0))
            o_ref[e] = dot_nt(hi, f_ref[...]) + dot_nt(lo, f_ref[...])

    return _qtn_call(body, a, b, None, (TC, 128, 128), [_spread_matrix()], name)


def qtn_tt(a, b, name):
    def body(a_ref, b_ref, o_ref):
        full = dot_tn(a_ref[...].astype(BF16), b_ref[...].astype(BF16))
        for lag in range(-(TC - 1), TC):
            acc = None
            for s in range(TC):
                t = s + lag
                if 0 <= t < TC:
                    blk = full[s * 128:(s + 1) * 128, t * 128:(t + 1) * 128]
                    acc = blk if acc is None else acc + blk
            o_ref[lag + TC - 1] = acc

    return _qtn_call(body, a, b, None, (2 * TC - 1, 128, 128), [], name)


def _scan_row(i, rb, ncr, reverse):
    if not reverse:
        return i
    return jnp.where(i < ncr, ncr - 1 - i, rb - 1 - (i - ncr))


def _swap_re_im(h):
    half = QW // 2
    return jnp.concatenate([h[:, q * QW + (1 - k) * half:q * QW + (2 - k) * half] for q in range(4) for k in range(2)],
                           axis=1)


def chunk_scan(xs, lam_ab, ncr, reverse, name):
    bl, rb, _ = xs.shape

    def body(x_ref, l_ref, hp_ref):
        la, lb = l_ref[0:1, :], l_ref[1:2, :]

        def step(i, h):
            row = _scan_row(i, rb, ncr, reverse)
            hp_ref[pl.ds(row, 1), :] = h
            return la * h + lb * _swap_re_im(h) + x_ref[pl.ds(row, 1), :]

        lax.fori_loop(0, rb, step, jnp.zeros((1, ROW_W), F32))

    blk = pl.BlockSpec((None, rb, ROW_W), lambda b: (b, 0, 0))
    return pl.pallas_call(
        body, name=name, grid=(bl,),
        in_specs=[blk, pl.BlockSpec((8, ROW_W), lambda b: (0, 0))], out_specs=blk,
        out_shape=jax.ShapeDtypeStruct((bl, rb, ROW_W), F32),
        compiler_params=_cp(("arbitrary",)),
    )(xs, lam_ab)


def chunk_scan_bwd(dhp, hp, lam_ab, ncr, reverse, name):
    bl, rb, _ = dhp.shape

    def body(d_ref, hp_ref, l_ref, g_ref, dl_ref):
        la, lb = l_ref[0:1, :], l_ref[1:2, :]

        dl_ref[...] = jnp.zeros_like(dl_ref)

        def step(n, g):
            row = _scan_row(rb - 1 - n, rb, ncr, reverse)
            g_ref[pl.ds(row, 1), :] = g
            pv = hp_ref[pl.ds(row, 1), :]
            dl_ref[0:1, :] += g * pv
            dl_ref[1:2, :] += g * _swap_re_im(pv)
            return d_ref[pl.ds(row, 1), :] + la * g + _swap_re_im(lb * g)

        lax.fori_loop(0, rb, step, jnp.zeros((1, ROW_W), F32))

    blk = pl.BlockSpec((None, rb, ROW_W), lambda b: (b, 0, 0))
    return pl.pallas_call(
        body, name=name, grid=(bl,),
        in_specs=[blk, blk, pl.BlockSpec((8, ROW_W), lambda b: (0, 0))],
        out_specs=[blk, pl.BlockSpec((None, 8, ROW_W), lambda b: (b, 0, 0))],
        out_shape=[jax.ShapeDtypeStruct((bl, rb, ROW_W), F32), jax.ShapeDtypeStruct((bl, 8, ROW_W), F32)],
        compiler_params=_cp(("arbitrary",)),
    )(dhp, hp, lam_ab)


def _quarter_rows(v):
    e = v.shape[0]
    return v.reshape(e, 4, 8, SSM_P, SSM_H).transpose(0, 1, 2, 4, 3).reshape(e, 4, 8 * SSM_H, SSM_P)


def _token_state_map(vr, vi):
    return jnp.concatenate([_quarter_rows(vr), _quarter_rows(vi)], axis=-1).transpose(1, 0, 2, 3)


def ssm_build(lam_re, lam_im, log_dt, b_re, b_im, c_re, c_im, d):
    dt = jnp.exp(log_dt)[..., None]
    mag = jnp.exp(lam_re * dt)
    ang = lam_im * dt
    lr, li = mag * jnp.cos(ang), mag * jnp.sin(ang)
    den = lam_re * lam_re + lam_im * lam_im
    nr = lr - 1.0
    fr = (nr * lam_re + li * lam_im) / den
    fi = (li * lam_re - nr * lam_im) / den
    bbr = fr[..., None] * b_re - fi[..., None] * b_im
    bbi = fr[..., None] * b_im + fi[..., None] * b_re
    pr, pi = [jnp.ones_like(lr)], [jnp.zeros_like(lr)]
    for _ in range(TC):
        pr, pi = pr + [pr[-1] * lr - pi[-1] * li], pi + [pr[-1] * li + pi[-1] * lr]
    pr, pi = jnp.stack(pr), jnp.stack(pi)
    clr = c_re[None] * pr[:, :, :, None, :] - c_im[None] * pi[:, :, :, None, :]
    cli = c_re[None] * pi[:, :, :, None, :] + c_im[None] * pr[:, :, :, None, :]
    same_group = jnp.asarray(np.kron(np.eye(8), np.ones((SSM_H, SSM_H))), F32)
    ein = functools.partial(jnp.einsum, precision=HI)

    out, lag_blocks = {}, {}
    for k, name in ((0, "f"), (1, "r")):
        ar, ai = _quarter_rows(bbr[k][None])[0], _quarter_rows(bbi[k][None])[0]
        cr = clr[:TC, k].reshape(TC, 4, 8 * SSM_H, SSM_P)
        ci = cli[:TC, k].reshape(TC, 4, 8 * SSM_H, SSM_P)
        lag_blocks[k] = (ein('qap,nqbp->nqab', ar, cr) - ein('qap,nqbp->nqab', ai, ci)) * same_group
        es = [TC - 1 - s for s in range(TC)] if k == 0 else list(range(TC))
        sr = jnp.stack([pr[e, k][:, :, None] * bbr[k] - pi[e, k][:, :, None] * bbi[k] for e in es])
        si = jnp.stack([pr[e, k][:, :, None] * bbi[k] + pi[e, k][:, :, None] * bbr[k] for e in es])
        out["bs_" + name] = _token_state_map(sr, si)
        et = [t + 1 for t in range(TC)] if k == 0 else [TC - t for t in range(TC)]
        crt = jnp.stack([jnp.swapaxes(clr[e, k], 1, 2) for e in et])
        cit = jnp.stack([-jnp.swapaxes(cli[e, k], 1, 2) for e in et])
        out["cst_" + name] = _token_state_map(crt, cit)
        l8r, l8i = pr[TC, k].reshape(4, 1, QW // 2), pi[TC, k].reshape(4, 1, QW // 2)
        la = jnp.concatenate([l8r, l8r], axis=1).reshape(1, ROW_W)
        lb = jnp.concatenate([-l8i, l8i], axis=1).reshape(1, ROW_W)
        out["lam_" + name] = jnp.concatenate([la, lb, jnp.zeros((6, ROW_W), F32)], axis=0)
    skip = jnp.eye(8 * SSM_H, dtype=F32)[None] * d.reshape(4, 1, 8 * SSM_H)
    center = lag_blocks[0][0] + lag_blocks[1][0] + skip
    lags = [lag_blocks[1][n] for n in range(TC - 1, 0, -1)] + [center] + [lag_blocks[0][n] for n in range(1, TC)]
    out["lags"] = jnp.stack(lags, axis=1)
    return out


def ssm_operators(mats, tag):
    ops = {}
    ops["m"], ops["mt"] = assemble_tt(mats["lags"], "ssm_map_intra" + tag)
    for dname in ("f", "r"):
        ops["bs_" + dname], ops["bst_" + dname] = assemble_ts(mats["bs_" + dname], f"ssm_map_state_in_{dname}{tag}")
        ops["cst_" + dname], ops["cs_" + dname] = assemble_ts(mats["cst_" + dname], f"ssm_map_readout_{dname}{tag}")
    return ops


def ssm_forward(u3, mats, ops, ncr):
    bl, rb, _ = u3.shape
    u = u3.reshape(bl * rb, ROW_W)
    hps, terms = {}, [(u, ops["m"])]
    for dname, rev in (("f", False), ("r", True)):
        xs = qmm([(u, ops["bs_" + dname])], "ssm_state_in_" + dname)
        hp = chunk_scan(xs.reshape(bl, rb, ROW_W), mats["lam_" + dname], ncr, rev, "ssm_scan_" + dname)
        hps[dname] = hp.reshape(bl * rb, ROW_W)
        terms.append((hps[dname], ops["cs_" + dname]))
    return qmm(terms, "ssm_output").reshape(bl, rb, ROW_W), hps


def ssm_backward(dy3, u3, hps, mats, ops, ncr):
    bl, rb, _ = u3.shape
    u = u3.reshape(bl * rb, ROW_W)
    dyr = dy3.reshape(bl * rb, ROW_W)
    cot = {"lags": qtn_tt(u, dyr, "ssm_d_intra")}
    terms = [(dyr, ops["mt"])]
    for dname, rev in (("f", False), ("r", True)):
        dhp = qmm([(dyr, ops["cst_" + dname])], "ssm_dstate_" + dname)
        g, dl = chunk_scan_bwd(dhp.reshape(bl, rb, ROW_W), hps[dname].reshape(bl, rb, ROW_W), mats["lam_" + dname],
                               ncr, rev, "ssm_scan_bwd_" + dname)
        g = g.reshape(bl * rb, ROW_W)
        cot["lam_" + dname] = jnp.sum(dl, axis=0)
        cot["bs_" + dname] = qtn_ts(u, g, "ssm_d_state_in_" + dname)
        cot["cst_" + dname] = qtn_ts(dyr, hps[dname], "ssm_d_readout_" + dname)
        terms.append((g, ops["bst_" + dname]))
    return qmm(terms, "ssm_input_grad").reshape(bl, rb, ROW_W), cot


def mod_forward(act, w_mod, b_cols):
    nl, _, wc = w_mod.shape
    r = act.shape[0]

    def body(a_ref, w_ref, b_ref, o_ref):
        o_ref[...] = dot_nn(a_ref[...].astype(BF16), w_ref[...].astype(BF16)) + b_ref[...]

    return pl.pallas_call(
        body, name="mod_forward", grid=(nl,),
        in_specs=[pl.BlockSpec((r, D), lambda l: (0, 0)), pl.BlockSpec((None, D, wc), lambda l: (l, 0, 0)),
                  pl.BlockSpec((None, 1, wc), lambda l: (l, 0, 0))],
        out_specs=pl.BlockSpec((None, r, wc), lambda l: (l, 0, 0)),
        out_shape=jax.ShapeDtypeStruct((nl, r, wc), F32),
        compiler_params=_cp(("arbitrary",)),
    )(act, w_mod, b_cols)


def mod_backward(act, dmod, dctx, w_mod):
    nl, _, wc = w_mod.shape
    r = act.shape[0]

    def body(a_ref, d_ref, c_ref, w_ref, gw_ref, gc_ref):
        gw_ref[...] = dot_tn(a_ref[...].astype(BF16), d_ref[...].astype(BF16))
        gc_ref[...] = dot_nt(c_ref[...].astype(BF16), w_ref[...].astype(BF16))

    return pl.pallas_call(
        body, name="mod_backward", grid=(nl,),
        in_specs=[pl.BlockSpec((r, D), lambda l: (0, 0)), pl.BlockSpec((None, r, wc), lambda l: (l, 0, 0)),
                  pl.BlockSpec((None, 8, wc), lambda l: (l, 0, 0)), pl.BlockSpec((None, D, wc), lambda l: (l, 0, 0))],
        out_specs=[pl.BlockSpec((None, D, wc), lambda l: (l, 0, 0)), pl.BlockSpec((None, 8, D), lambda l: (l, 0, 0))],
        out_shape=[jax.ShapeDtypeStruct((nl, D, wc), F32), jax.ShapeDtypeStruct((nl, 8, D), F32)],
        compiler_params=_cp(("arbitrary",)),
    )(act, dmod, dctx, w_mod)


def _place():
    return lax.axis_index("x"), lax.axis_index("y"), lax.axis_index("c")


def all_gather_rows(arrs, name):
    n = len(arrs)
    rs = [a.shape[1] for a in arrs]

    def body(*refs):
        x_refs, o_refs = refs[:n], refs[n:2 * n]
        send_sems, recv_sems, local_sems = refs[2 * n:]
        x, y, c = _place()
        me, sibling = (x, y, c), (x, y, 1 - c)
        chips = [(1 - x, y), (x, 1 - y), (1 - x, 1 - y)]

        def rows(a, px, py, pc):
            return o_refs[a].at[:, pl.ds((4 * px + 2 * py + pc) * rs[a], rs[a]), :]

        def copy(a, k, block, to, src=None):
            return pltpu.make_async_remote_copy(
                src_ref=rows(a, *block) if src is None else src, dst_ref=rows(a, *block),
                send_sem=send_sems.at[a, k], recv_sem=recv_sems.at[a, k], device_id=to, device_id_type=MESH)

        mine = [pltpu.make_async_copy(x_refs[a], rows(a, *me), local_sems.at[a]) for a in range(n)]
        for cp in mine:
            cp.start()
        first = []
        for a in range(n):
            first.append(copy(a, 0, me, sibling, src=x_refs[a]))
            first += [copy(a, 1 + j, me, (*chip, c), src=x_refs[a]) for j, chip in enumerate(chips)]
        for cp in first:
            cp.start()
        passed = []
        for j, chip in enumerate(chips):
            for a in range(n):
                copy(a, 1 + j, (*chip, c), me).wait_recv()
                fwd = copy(a, 4 + j, (*chip, c), sibling)
                fwd.start()
                passed.append(fwd)
        for a in range(n):
            copy(a, 0, sibling, me).wait_recv()
            for j, chip in enumerate(chips):
                copy(a, 4 + j, (*chip, 1 - c), me).wait_recv()
        for cp in first + passed:
            cp.wait_send()
        for cp in mine:
            cp.wait()

    any_spec = pl.BlockSpec(memory_space=pl.ANY)
    return pl.pallas_call(
        body, name=name,
        in_specs=[any_spec] * n, out_specs=[any_spec] * n,
        out_shape=[jax.ShapeDtypeStruct((a.shape[0], N_DEV * a.shape[1], a.shape[2]), a.dtype) for a in arrs],
        scratch_shapes=[pltpu.SemaphoreType.DMA((n, 7)), pltpu.SemaphoreType.DMA((n, 7)), pltpu.SemaphoreType.DMA((n,))],
    )(*arrs)


def all_to_all_rows(arrs, name):
    n = len(arrs)
    rs = [a.shape[1] // N_DEV for a in arrs]
    flips = [(fx, fy, fc) for fx in (0, 1) for fy in (0, 1) for fc in (0, 1)][1:]

    def body(*refs):
        x_refs, o_refs = refs[:n], refs[n:2 * n]
        send_sems, recv_sems, local_sems = refs[2 * n:]
        x, y, c = _place()
        my_idx = 4 * x + 2 * y + c

        def block(a, idx):
            return x_refs[a].at[:, pl.ds(idx * rs[a], rs[a]), :]

        mine = [pltpu.make_async_copy(block(a, my_idx), o_refs[a].at[my_idx], local_sems.at[a]) for a in range(n)]
        for cp in mine:
            cp.start()
        sends = []
        for k, (fx, fy, fc) in enumerate(flips):
            px = 1 - x if fx else x
            py = 1 - y if fy else y
            pc = 1 - c if fc else c
            p_idx = 4 * px + 2 * py + pc
            for a in range(n):
                sends.append(pltpu.make_async_remote_copy(
                    src_ref=block(a, p_idx), dst_ref=o_refs[a].at[my_idx], send_sem=send_sems.at[a, k],
                    recv_sem=recv_sems.at[a, k], device_id=(px, py, pc), device_id_type=MESH))
        for cp in sends:
            cp.start()
        for k, (fx, fy, fc) in enumerate(flips):
            px = 1 - x if fx else x
            py = 1 - y if fy else y
            pc = 1 - c if fc else c
            p_idx = 4 * px + 2 * py + pc
            for a in range(n):
                pltpu.make_async_remote_copy(
                    src_ref=block(a, p_idx), dst_ref=o_refs[a].at[p_idx], send_sem=send_sems.at[a, k],
                    recv_sem=recv_sems.at[a, k], device_id=(px, py, pc), device_id_type=MESH).wait_recv()
        for cp in sends:
            cp.wait_send()
        for cp in mine:
            cp.wait()

    any_spec = pl.BlockSpec(memory_space=pl.ANY)
    return pl.pallas_call(
        body, name=name,
        in_specs=[any_spec] * n, out_specs=[any_spec] * n,
        out_shape=[jax.ShapeDtypeStruct((N_DEV, a.shape[0], r, a.shape[2]), a.dtype) for a, r in zip(arrs, rs)],
        scratch_shapes=[pltpu.SemaphoreType.DMA((n, 7)), pltpu.SemaphoreType.DMA((n, 7)), pltpu.SemaphoreType.DMA((n,))],
    )(*arrs)


def _peers():
    x, y, c = _place()
    out = []
    for fx in (0, 1):
        for fy in (0, 1):
            for fc in (0, 1):
                if fx or fy or fc:
                    px, py, pc = (1 - x if fx else x), (1 - y if fy else y), (1 - c if fc else c)
                    out.append(((px, py, pc), 4 * px + 2 * py + pc))
    return out, 4 * x + 2 * y + c


def _split_call(body, name, ins, n_sem_out, thru, extra_out_shape, extra_out_specs, sem_ins=(), after=None):
    hbm = pl.BlockSpec(memory_space=pltpu.HBM)
    sem = pl.BlockSpec(memory_space=pltpu.SEMAPHORE)
    n_thru = len(thru)
    tail_in = [sem] * len(sem_ins) + ([pl.BlockSpec(memory_space=pl.ANY)] if after is not None else [])
    return pl.pallas_call(
        body, name=name,
        out_shape=tuple(n_sem_out) + tuple(pltpu.HBM(a.shape, a.dtype) for a in thru) + tuple(extra_out_shape),
        in_specs=[hbm] * n_thru + tail_in,
        out_specs=(sem,) * len(n_sem_out) + (hbm,) * n_thru + tuple(extra_out_specs),
        input_output_aliases={i: i + len(n_sem_out) for i in range(n_thru)},
        compiler_params=pltpu.CompilerParams(has_side_effects=pltpu.SideEffectType.DATAFLOW_SIDE_EFFECTING),
    )(*ins, *sem_ins, *([after] if after is not None else []))


def gather_start(shards, after, name):
    n = len(shards)
    rs = [a.shape[1] for a in shards]
    lands = [lax.empty((a.shape[0], N_DEV * a.shape[1], a.shape[2]), a.dtype) for a in shards]

    def body(*refs):
        x_refs, land_refs = refs[:n], refs[n:2 * n]
        send_sems, recv_sems = refs[2 * n + 1], refs[2 * n + 2]
        peers, my_idx = _peers()
        for k, (peer, _) in enumerate(peers):
            for a in range(n):
                pltpu.make_async_remote_copy(
                    src_ref=x_refs[a], dst_ref=land_refs[a].at[:, pl.ds(my_idx * rs[a], rs[a]), :],
                    send_sem=send_sems.at[a * 7 + k], recv_sem=recv_sems.at[a * 7 + k], device_id=peer,
                    device_id_type=MESH).start()
        refs[-1][...] = jnp.zeros_like(refs[-1])

    ins = [pltpu.with_memory_space_constraint(a, pltpu.HBM) for a in list(shards) + lands]
    outs = _split_call(body, name, ins, [pltpu.SemaphoreType.DMA((n * 7,))] * 2, ins,
                       [jax.ShapeDtypeStruct((8, 128), F32)], [pl.BlockSpec(memory_space=pltpu.VMEM)], after=after)
    return outs[0], outs[1], list(outs[2:2 + n]), list(outs[2 + n:2 + 2 * n]), outs[-1]


def gather_wait(send_sems, recv_sems, shards, lands, after, name):
    n = len(shards)
    rs = [a.shape[1] for a in shards]

    def body(*refs):
        x_refs, land_refs = refs[:n], refs[n:2 * n]
        s_sems, r_sems = refs[2 * n], refs[2 * n + 1]
        peers, _ = _peers()
        for k, (peer, p_idx) in enumerate(peers):
            for a in range(n):
                copy = pltpu.make_async_remote_copy(
                    src_ref=x_refs[a], dst_ref=land_refs[a].at[:, pl.ds(p_idx * rs[a], rs[a]), :],
                    send_sem=s_sems.at[a * 7 + k], recv_sem=r_sems.at[a * 7 + k], device_id=peer, device_id_type=MESH)
                copy.wait_send()
                copy.wait_recv()

    outs = _split_call(body, name, list(shards) + list(lands), [], list(shards) + list(lands), [], [],
                       sem_ins=(send_sems, recv_sems), after=after)
    my_idx = 4 * lax.axis_index("x") + 2 * lax.axis_index("y") + lax.axis_index("c")
    return [lax.dynamic_update_slice_in_dim(z, s, my_idx * r, axis=1) for z, s, r in zip(outs[n:], outs[:n], rs)]


def scatter_start(arrs, name):
    n = len(arrs)
    rs = [a.shape[1] // N_DEV for a in arrs]
    lands = [lax.empty((N_DEV, a.shape[0], r, a.shape[2]), a.dtype) for a, r in zip(arrs, rs)]

    def body(*refs):
        x_refs, land_refs = refs[:n], refs[n:2 * n]
        send_sems, recv_sems = refs[2 * n], refs[2 * n + 1]
        token = refs[-1]
        peers, my_idx = _peers()
        for k, (peer, p_idx) in enumerate(peers):
            for a in range(n):
                pltpu.make_async_remote_copy(
                    src_ref=x_refs[a].at[:, pl.ds(p_idx * rs[a], rs[a]), :], dst_ref=land_refs[a].at[my_idx],
                    send_sem=send_sems.at[a * 7 + k], recv_sem=recv_sems.at[a * 7 + k], device_id=peer,
                    device_id_type=MESH).start()
        token[...] = jnp.zeros_like(token)

    hbm = pl.BlockSpec(memory_space=pltpu.HBM)
    sem = pl.BlockSpec(memory_space=pltpu.SEMAPHORE)
    outs = pl.pallas_call(
        body, name=name,
        out_shape=(pltpu.SemaphoreType.DMA((n * 7,)), pltpu.SemaphoreType.DMA((n * 7,)))
        + tuple(pltpu.HBM(a.shape, a.dtype) for a in arrs) + tuple(pltpu.HBM(z.shape, z.dtype) for z in lands)
        + (jax.ShapeDtypeStruct((8, 128), F32),),
        in_specs=[hbm] * (2 * n),
        out_specs=(sem, sem) + (hbm,) * (2 * n) + (pl.BlockSpec(memory_space=pltpu.VMEM),),
        input_output_aliases={i: i + 2 for i in range(2 * n)},
        compiler_params=pltpu.CompilerParams(has_side_effects=pltpu.SideEffectType.DATAFLOW_SIDE_EFFECTING),
    )(*[pltpu.with_memory_space_constraint(a, pltpu.HBM) for a in arrs],
      *[pltpu.with_memory_space_constraint(z, pltpu.HBM) for z in lands])
    return outs[0], outs[1], list(outs[2:2 + n]), list(outs[2 + n:2 + 2 * n]), outs[-1]


def scatter_wait(send_sems, recv_sems, arrs, lands, after, name):
    n = len(arrs)
    rs = [a.shape[1] // N_DEV for a in arrs]

    def body(*refs):
        x_refs, land_refs = refs[:n], refs[n:2 * n]
        s_sems, r_sems = refs[2 * n], refs[2 * n + 1]
        peers, my_idx = _peers()
        for k, (peer, p_idx) in enumerate(peers):
            for a in range(n):
                copy = pltpu.make_async_remote_copy(
                    src_ref=x_refs[a].at[:, pl.ds(p_idx * rs[a], rs[a]), :], dst_ref=land_refs[a].at[p_idx],
                    send_sem=s_sems.at[a * 7 + k], recv_sem=r_sems.at[a * 7 + k], device_id=peer, device_id_type=MESH)
                copy.wait_send()
                copy.wait_recv()

    hbm = pl.BlockSpec(memory_space=pltpu.HBM)
    sem = pl.BlockSpec(memory_space=pltpu.SEMAPHORE)
    outs = pl.pallas_call(
        body, name=name,
        out_shape=tuple(pltpu.HBM(a.shape, a.dtype) for a in arrs) + tuple(pltpu.HBM(z.shape, z.dtype) for z in lands),
        in_specs=[hbm] * (2 * n) + [sem, sem, pl.BlockSpec(memory_space=pl.ANY)],
        out_specs=(hbm,) * (2 * n),
        input_output_aliases={i: i for i in range(2 * n)},
        compiler_params=pltpu.CompilerParams(has_side_effects=pltpu.SideEffectType.DATAFLOW_SIDE_EFFECTING),
    )(*arrs, *lands, send_sems, recv_sems, after)
    return list(outs[:n]), list(outs[n:])


def _row_tile(rows, cap):
    best = None
    for t in range(16, min(rows, cap) + 1, 16):
        if rows % t == 0:
            best = t
    return rows if best is None else best


def adamw(w, gparts, m, v, name):
    per_layer = isinstance(gparts, (list, tuple))
    glist = list(gparts) if per_layer else [gparts]
    n, _, ra, cb = glist[0].shape
    nl = w.shape[0]
    ng = len(glist)
    ta = _row_tile(ra, max(8, (1 << 19) // (cb * n)))

    def slot_sum(g_ref):
        g = g_ref[0].astype(F32)
        for p in range(1, n):
            g = g + g_ref[p].astype(F32)
        return g

    def body(*refs):
        w_ref, g_refs = refs[0], refs[1:1 + ng]
        m_ref, v_ref, go_ref, d_ref, mo_ref, vo_ref = refs[1 + ng:]
        g = slot_sum(g_refs[0])
        for layer in range(1, ng):
            g = jnp.where(pl.program_id(0) == layer, slot_sum(g_refs[layer]), g)
        mn = ADAM_B1 * m_ref[...] + (1.0 - ADAM_B1) * g
        vn = ADAM_B2 * v_ref[...] + (1.0 - ADAM_B2) * jnp.square(g)
        m_hat = mn / (1.0 - ADAM_B1 ** ADAM_STEP)
        v_hat = vn / (1.0 - ADAM_B2 ** ADAM_STEP)
        go_ref[...] = g
        d_ref[...] = -ADAM_LR * (m_hat / (jnp.sqrt(v_hat) + ADAM_EPS) + ADAM_WD * w_ref[...])
        mo_ref[...] = mn
        vo_ref[...] = vn

    blk = pl.BlockSpec((None, ta, cb), lambda l, i: (l, i, 0))
    if per_layer:
        gblk = pl.BlockSpec((n, None, ta, cb), lambda l, i: (0, 0, i, 0))
    else:
        gblk = pl.BlockSpec((n, None, ta, cb), lambda l, i: (0, l, i, 0))
    shp = jax.ShapeDtypeStruct((nl, ra, cb), F32)
    return pl.pallas_call(
        body, name=name, grid=(nl, ra // ta),
        in_specs=[blk] + [gblk] * ng + [blk, blk], out_specs=[blk] * 4, out_shape=[shp] * 4,
        compiler_params=_cp(("arbitrary", "arbitrary")),
    )(w, *glist, m, v)


def _sincos_2d(rows, cols, dim):
    quarter = dim // 4
    omega = 1.0 / (10000.0 ** (jnp.arange(quarter, dtype=F32) / quarter))
    r = jnp.arange(rows, dtype=F32)[:, None] * omega
    cc = jnp.arange(cols, dtype=F32)[:, None] * omega
    er = jnp.concatenate([jnp.sin(r), jnp.cos(r)], axis=-1)
    ec = jnp.concatenate([jnp.sin(cc), jnp.cos(cc)], axis=-1)
    pe = jnp.concatenate([jnp.broadcast_to(er[:, None, :], (rows, cols, dim // 2)),
                          jnp.broadcast_to(ec[None, :, :], (rows, cols, dim // 2))], axis=-1)
    return pe.reshape(rows * cols, dim)


def _pool_constants():
    nw = len(POOL_WINDOWS)
    band = np.zeros((2, nw, TT, TT), np.float32)
    icnt = np.zeros((2, TT, C_W), np.float32)
    for kind, n in ((0, TT), (1, GRID_W)):
        for i, w in enumerate(POOL_WINDOWS):
            for t in range(TT):
                base, tl = (t // n) * n, t % n
                lo = min(max(tl - w // 2, 0), n)
                hi = min(max(tl - w // 2 + w, 0), n)
                band[kind, i, t, base + lo:base + hi] = 1.0
                icnt[kind, t, i * (C_W // nw):(i + 1) * (C_W // nw)] = 1.0 / (hi - lo)
    return jnp.asarray(band, BF16), jnp.asarray(icnt, F32)


def _block_diag(blocks):
    n, a, _ = blocks.shape
    return jnp.einsum('gab,gh->gahb', blocks, jnp.eye(n, dtype=F32), precision=HI).reshape(n * a, n * a)


def _block_diag_parts(mat, n):
    a = mat.shape[0] // n
    m4 = mat.reshape(n, a, n, a)
    return jnp.stack([m4[g, :, g, :] for g in range(n)])


_SMALL = ("c_ctx", "b_mod", "norm_mix_pre", "norm_mix_post", "norm_ffn_pre", "norm_ffn_post", "sgu_w", "sgu_b",
          "ssm_lam_re", "ssm_lam_im", "ssm_log_dt", "ssm_b_re", "ssm_b_im", "ssm_c_re", "ssm_c_im", "ssm_d",
          "glu_b", "pool_w", "pool_scale")
_WEIGHTS = ("c_ctx", "w_mod", "b_mod", "norm_mix_pre", "norm_mix_post", "norm_ffn_pre", "norm_ffn_post", "w_in", "w_out",
            "sgu_w", "sgu_b", "ssm_lam_re", "ssm_lam_im", "ssm_log_dt", "ssm_b_re", "ssm_b_im", "ssm_c_re", "ssm_c_im",
            "ssm_d", "glu_w", "glu_b", "pool_w", "pool_scale", "ffn_w_gate", "ffn_w_up", "ffn_w_down")


def _pack_rows(a):
    flat = a.reshape(-1)
    rows = -(-flat.shape[0] // D)
    rows8 = -(-rows // 8) * 8
    return jnp.pad(flat, (0, rows8 * D - flat.shape[0])).reshape(rows8, D)


def _pack(tree):
    packed = jnp.concatenate([_pack_rows(tree[k]) for k in _SMALL], axis=0)
    return jnp.pad(packed, ((0, -packed.shape[0] % 64), (0, 0)))


def _unpack(packed, like):
    out, at = {}, 0
    for k in _SMALL:
        size = int(np.prod(like[k].shape))
        rows8 = -(-(-(-size // D)) // 8) * 8
        out[k] = packed[at:at + rows8].reshape(-1)[:size].reshape(like[k].shape)
        at += rows8
    return out


def kernel(x, c, ctx, c_ctx, w_mod, b_mod, norm_mix_pre, norm_mix_post, norm_ffn_pre, norm_ffn_post, w_in, w_out, sgu_w, sgu_b, ssm_lam_re, ssm_lam_im, ssm_log_dt, ssm_b_re, ssm_b_im, ssm_c_re, ssm_c_im, ssm_d, glu_w, glu_b, pool_w, pool_scale, ffn_w_gate, ffn_w_up, ffn_w_down, loss_target, m_c_ctx, m_w_mod, m_b_mod, m_norm_mix_pre, m_norm_mix_post, m_norm_ffn_pre, m_norm_ffn_post, m_w_in, m_w_out, m_sgu_w, m_sgu_b, m_ssm_lam_re, m_ssm_lam_im, m_ssm_log_dt, m_ssm_b_re, m_ssm_b_im, m_ssm_c_re, m_ssm_c_im, m_ssm_d, m_glu_w, m_glu_b, m_pool_w, m_pool_scale, m_ffn_w_gate, m_ffn_w_up, m_ffn_w_down, v_c_ctx, v_w_mod, v_b_mod, v_norm_mix_pre, v_norm_mix_post, v_norm_ffn_pre, v_norm_ffn_post, v_w_in, v_w_out, v_sgu_w, v_sgu_b, v_ssm_lam_re, v_ssm_lam_im, v_ssm_log_dt, v_ssm_b_re, v_ssm_b_im, v_ssm_c_re, v_ssm_c_im, v_ssm_d, v_glu_w, v_glu_b, v_pool_w, v_pool_scale, v_ffn_w_gate, v_ffn_w_up, v_ffn_w_down):
    wts = dict(c_ctx=c_ctx, w_mod=w_mod, b_mod=b_mod, norm_mix_pre=norm_mix_pre, norm_mix_post=norm_mix_post,
               norm_ffn_pre=norm_ffn_pre, norm_ffn_post=norm_ffn_post, w_in=w_in, w_out=w_out, sgu_w=sgu_w, sgu_b=sgu_b,
               ssm_lam_re=ssm_lam_re, ssm_lam_im=ssm_lam_im, ssm_log_dt=ssm_log_dt, ssm_b_re=ssm_b_re, ssm_b_im=ssm_b_im,
               ssm_c_re=ssm_c_re, ssm_c_im=ssm_c_im, ssm_d=ssm_d, glu_w=glu_w, glu_b=glu_b, pool_w=pool_w,
               pool_scale=pool_scale, ffn_w_gate=ffn_w_gate, ffn_w_up=ffn_w_up, ffn_w_down=ffn_w_down)
    mom_m = dict(c_ctx=m_c_ctx, w_mod=m_w_mod, b_mod=m_b_mod, norm_mix_pre=m_norm_mix_pre, norm_mix_post=m_norm_mix_post,
                 norm_ffn_pre=m_norm_ffn_pre, norm_ffn_post=m_norm_ffn_post, w_in=m_w_in, w_out=m_w_out, sgu_w=m_sgu_w,
                 sgu_b=m_sgu_b, ssm_lam_re=m_ssm_lam_re, ssm_lam_im=m_ssm_lam_im, ssm_log_dt=m_ssm_log_dt,
                 ssm_b_re=m_ssm_b_re, ssm_b_im=m_ssm_b_im, ssm_c_re=m_ssm_c_re, ssm_c_im=m_ssm_c_im, ssm_d=m_ssm_d,
                 glu_w=m_glu_w, glu_b=m_glu_b, pool_w=m_pool_w, pool_scale=m_pool_scale, ffn_w_gate=m_ffn_w_gate,
                 ffn_w_up=m_ffn_w_up, ffn_w_down=m_ffn_w_down)
    mom_v = dict(c_ctx=v_c_ctx, w_mod=v_w_mod, b_mod=v_b_mod, norm_mix_pre=v_norm_mix_pre, norm_mix_post=v_norm_mix_post,
                 norm_ffn_pre=v_norm_ffn_pre, norm_ffn_post=v_norm_ffn_post, w_in=v_w_in, w_out=v_w_out, sgu_w=v_sgu_w,
                 sgu_b=v_sgu_b, ssm_lam_re=v_ssm_lam_re, ssm_lam_im=v_ssm_lam_im, ssm_log_dt=v_ssm_log_dt,
                 ssm_b_re=v_ssm_b_re, ssm_b_im=v_ssm_b_im, ssm_c_re=v_ssm_c_re, ssm_c_im=v_ssm_c_im, ssm_d=v_ssm_d,
                 glu_w=v_glu_w, glu_b=v_glu_b, pool_w=v_pool_w, pool_scale=v_pool_scale, ffn_w_gate=v_ffn_w_gate,
                 ffn_w_up=v_ffn_w_up, ffn_w_down=v_ffn_w_down)

    bl, seq, _ = x.shape
    n_ctx = ctx.shape[1]
    assert n_ctx == TT and seq % TT == 0 and seq % GRID_W == 0
    depth = w_in.shape[0]
    nc = n_ctx // TT
    ncr = n_ctx // TC
    s_all = n_ctx + seq
    nt = s_all // TT
    t_all = bl * s_all
    n_batch = bl * N_DEV
    my_idx = 4 * lax.axis_index("x") + 2 * lax.axis_index("y") + lax.axis_index("c")
    wc = w_mod.shape[2]

    c_rows = jnp.pad(c, ((0, 8 - bl), (0, 0))) if bl < 8 else c
    rc = c_rows.shape[0]
    (c_all,) = all_gather_rows([c_rows[None]], "gather_c")
    c_all = c_all[0].reshape(N_DEV, rc, D)[:, :bl].reshape(n_batch, D)
    r_act = -(-(n_batch + 1) // 16) * 16
    pre_act = jnp.concatenate([c_all, c_ctx[None, :], jnp.zeros((r_act - n_batch - 1, D), F32)], axis=0)
    act = jax.nn.silu(pre_act)
    b_cols = lax.dynamic_slice_in_dim(b_mod, my_idx * wc, wc, axis=1)[:, None, :]
    mod_cols = mod_forward(act, w_mod, b_cols)
    (mod_all,) = all_gather_rows([mod_cols], "gather_mod")
    mod_all = mod_all.reshape(depth, N_DEV, r_act, wc).transpose(0, 2, 1, 3).reshape(depth, r_act, 6, D)
    mod_lat = lax.dynamic_slice_in_dim(mod_all, my_idx * bl, bl, axis=1)
    mod_ctx = jnp.broadcast_to(mod_all[:, n_batch:n_batch + 1], (depth, bl, 6, D))
    mods = jnp.pad(jnp.stack([mod_ctx, mod_lat], axis=2), ((0, 0), (0, 0), (0, 0), (0, 2), (0, 0)))

    tr = lambda a: jnp.swapaxes(a, 1, 2).astype(BF16)
    shards = dict(w_in=tr(w_in), w_out=w_out.astype(BF16), glu_w=glu_w.astype(BF16), gate=tr(ffn_w_gate),
                  up=tr(ffn_w_up), down=ffn_w_down.astype(BF16))
    mix_keys, ffn_keys = ("w_in", "w_out", "glu_w"), ("gate", "up", "down")
    layer = lambda k, i: shards[k][i:i + 1]
    full = [dict() for _ in range(depth)]
    for k, g in zip(mix_keys, all_gather_rows([layer(k, 0) for k in mix_keys], "gather_mix_weights_0")):
        full[0][k] = g[0]
    first_done = mods[0, 0, 0, 0:1, 0:128] + full[0]["w_in"][0:1, 0:128].astype(F32)
    weights_in_flight = {0: (ffn_keys, gather_start([layer(k, 0) for k in ffn_keys], first_done, "gather_start_ffn_0"))}
    for i in range(1, depth):
        prev_token = weights_in_flight[i - 1][1][4]
        weights_in_flight[i] = (mix_keys + ffn_keys, gather_start([layer(k, i) for k in mix_keys + ffn_keys], prev_token,
                                                                  f"gather_start_layer_{i}"))
    start_token = sum(fl[1][4][0:1, 0:1] for fl in weights_in_flight.values())

    def land_weights(i, after, name):
        keys, (send_sems, recv_sems, sent, lands, _) = weights_in_flight[i]
        for k, g in zip(keys, gather_wait(send_sems, recv_sems, sent, lands, after, name)):
            full[i][k] = g[0]

    band, icnt = _pool_constants()
    seg_p = jnp.asarray(np.kron(np.eye(A_HEADS), np.full((A_W // A_HEADS,) * 2, A_HEADS / A_W)), BF16)
    pe = _sincos_2d(seq // GRID_W, GRID_W, D)
    xs = embed_tokens(x, ctx, pe)

    saved = []
    for i in range(depth):
        mats, ssm_vjp = jax.vjp(ssm_build, ssm_lam_re[i], ssm_lam_im[i], ssm_log_dt[i], ssm_b_re[i], ssm_b_im[i],
                                ssm_c_re[i], ssm_c_im[i], ssm_d[i])
        if i > 0:
            land_weights(i, xs, f"gather_wait_layer_{i}")
        cst = dict(sw=sgu_w[i].astype(BF16),
                   sbias=jnp.repeat(sgu_b[i].T, A_W // A_HEADS, axis=1),
                   seg_p=seg_p, band=band, icnt=icnt, wbd=_block_diag(pool_w[i]).astype(BF16),
                   pscale=pool_scale[i][None, :], glu_w=full[i]["glu_w"], glu_b=glu_b[i][None, :], w_out=full[i]["w_out"],
                   n2=norm_mix_post[i][None, :])
        n1, n3, n4 = norm_mix_pre[i][None, :], norm_ffn_pre[i][None, :], norm_ffn_post[i][None, :]
        if i == 0:
            n1 = n1 + start_token
        za, zu, zp = pre_mix(xs, mods[i], n1, full[i]["w_in"], nc)
        ops = ssm_operators(mats, f"_{i}")
        ys, hps = ssm_forward(zu, mats, ops, ncr)
        x1, m_pre = post_mix(xs, za, zp, ys, mods[i], cst, nc)
        if i == 0:
            land_weights(0, x1, "gather_wait_ffn_0")
        x2, f_pre, gate_b, up_b = ffn_fwd(x1, mods[i], n3, n4, full[i]["gate"], full[i]["up"], full[i]["down"], n_ctx)
        saved.append(dict(xs=xs, za=za, zu=zu, zp=zp, ys=ys, hps=hps, x1=x1, m=m_pre, f=f_pre, gate=gate_b, up=up_b,
                          cst=cst, mats=mats,
                          ops=ops, ssm_vjp=ssm_vjp, n1=n1, n3=n3, n4=n4))
        xs = x2

    dx, loss_parts = loss_head(xs, loss_target, nc)
    loss = lax.psum(jnp.sum(loss_parts[:, :, 0, 0]), ("x", "y", "c"))

    grads = {k: [None] * depth for k in _WEIGHTS}
    big = {k: [None] * depth for k in ("w_in", "w_out", "glu_w", "ffn_w_gate", "ffn_w_up", "ffn_w_down")}
    dmods = [None] * depth
    scatter_groups = (("ffn_w_gate", "ffn_w_up", "ffn_w_down"), ("w_out", "glu_w"), ("w_in",))
    in_flight = []

    def send_grads(i, group):
        flight = scatter_start([big[k][i][None] for k in scatter_groups[group]], f"scatter_start_{i}_{group}")
        in_flight.append((i, group, flight))
        return flight[4][0:1, 0:1]

    flat = lambda a: a.reshape(t_all, a.shape[-1])
    for i in reversed(range(depth)):
        sv = saved[i]
        dx1, h2, df, act_b, dgate, dup, st_f = ffn_bwd(dx, sv["x1"], sv["f"], sv["gate"], sv["up"], mods[i], sv["n3"],
                                                       sv["n4"], full[i]["gate"], full[i]["up"], full[i]["down"], n_ctx)
        big["ffn_w_gate"][i] = tn_matmul(flat(dgate), flat(h2), f"grad_ffn_gate_{i}")
        big["ffn_w_up"][i] = tn_matmul(flat(dup), flat(h2), f"grad_ffn_up_{i}")
        big["ffn_w_down"][i] = tn_matmul(flat(act_b), flat(df), f"grad_ffn_down_{i}")
        cst_i = dict(sv["cst"], n2=sv["cst"]["n2"] + send_grads(i, 0))
        dza, dzp, dys, cat, dm, gg, dr, st_m, dsw, dsb, dwbd = post_mix_bwd(dx1, sv["m"], sv["za"], sv["zp"], sv["ys"],
                                                                            mods[i], cst_i, nc)
        big["w_out"][i] = tn_matmul(flat(cat), flat(dm), f"grad_w_out_{i}")
        big["glu_w"][i] = tn_matmul(flat(gg), flat(dr), f"grad_glu_w_{i}")
        mats_i = dict(sv["mats"], lam_f=sv["mats"]["lam_f"] + send_grads(i, 1))
        dzu, cot = ssm_backward(dys, sv["zu"], sv["hps"], mats_i, sv["ops"], ncr)
        (grads["ssm_lam_re"][i], grads["ssm_lam_im"][i], grads["ssm_log_dt"][i], grads["ssm_b_re"][i],
         grads["ssm_b_im"][i], grads["ssm_c_re"][i], grads["ssm_c_im"][i], grads["ssm_d"][i]) = sv["ssm_vjp"](cot)
        dx, h1, dz, st_p = pre_mix_bwd(dza, dzu, dzp, sv["xs"], dx1, mods[i], sv["n1"], full[i]["w_in"], nc)
        big["w_in"][i] = tn_matmul(flat(dz), flat(h1), f"grad_w_in_{i}")

        tiles = lambda st, row: st[:, :, row, :]
        allsum = lambda st, row: jnp.sum(tiles(st, row), axis=(0, 1))
        grads["norm_mix_pre"][i] = allsum(st_p, 2)
        grads["norm_mix_post"][i] = allsum(st_m, 1)
        grads["norm_ffn_pre"][i] = allsum(st_f, 3)
        grads["norm_ffn_post"][i] = allsum(st_f, 4)
        misc = allsum(st_m, 2)
        grads["glu_b"][i] = misc[:B_W]
        grads["pool_scale"][i] = misc[B_W:B_W + C_W]
        grads["sgu_w"][i] = dsw
        grads["sgu_b"][i] = jnp.sum(dsb.reshape(CHUNK, A_HEADS, A_W // A_HEADS), axis=2).T
        grads["pool_w"][i] = _block_diag_parts(dwbd, len(POOL_WINDOWS))
        mix = (tiles(st_p, 0), tiles(st_p, 1), tiles(st_m, 0))
        d_lat = jnp.stack([jnp.sum(t[:, nc:], axis=1) for t in mix]
                          + [jnp.sum(tiles(st_f, r), axis=1) for r in (0, 1, 2)], axis=1).reshape(bl, 6 * D)
        d_ctx = jnp.concatenate([jnp.sum(t[:, :nc], axis=(0, 1)) for t in mix]
                                + [allsum(st_f, r) for r in (5, 6, 7)]).reshape(1, 6 * D)
        dmods[i] = jnp.concatenate([d_lat, d_ctx, jnp.zeros((8 - (bl + 1) % 8 if (bl + 1) % 8 else 0, 6 * D), F32)],
                                   axis=0)
        token = send_grads(i, 2)
        if i > 0:
            saved[i - 1]["n3"] = saved[i - 1]["n3"] + token
        else:
            dmods[i] = dmods[i] + token
    grad_x = dx[:, n_ctx:, :]

    dmod_local = jnp.stack(dmods)
    rd = dmod_local.shape[1]
    (dmod_all,) = all_gather_rows([dmod_local], "gather_dmod")
    dmod_cols = lax.dynamic_slice_in_dim(dmod_all, my_idx * wc, wc, axis=2).reshape(depth, N_DEV, rd, wc)
    d_lat_all = dmod_cols[:, :, :bl].reshape(depth, n_batch, wc)
    d_ctx_all = dmod_cols[:, 0, bl]
    for p in range(1, N_DEV):
        d_ctx_all = d_ctx_all + dmod_cols[:, p, bl]
    dmod_rows = jnp.concatenate([d_lat_all, d_ctx_all[:, None, :], jnp.zeros((depth, r_act - n_batch - 1, wc), F32)],
                                axis=1)
    dctx_rows = jnp.pad(d_ctx_all[:, None, :], ((0, 0), (0, 7), (0, 0)))
    g_w_mod, dact_ctx = mod_backward(act, dmod_rows, dctx_rows, w_mod)
    sig_c = jax.nn.sigmoid(c_ctx)
    dsilu_c = sig_c * (1.0 + c_ctx * (1.0 - sig_c))
    small_g = {k: (jnp.stack(grads[k]) if grads[k][0] is not None else None) for k in _SMALL}
    small_g["c_ctx"] = jnp.sum(dact_ctx[:, 0, :], axis=0) * dsilu_c
    small_g["b_mod"] = jnp.stack([jnp.sum(dmods[i][:bl + 1], axis=0) for i in range(depth)])

    packed_g = _pack(small_g).astype(BF16)
    rows_s = packed_g.shape[0]
    (gathered,) = all_gather_rows([packed_g[None]], "gather_small_grads")
    res = {k: [None] * 4 for k in _WEIGHTS}

    landed = {}
    for i, group, (send_sems, recv_sems, arrs_thru, lands_thru, _) in in_flight:
        sent, lands = scatter_wait(send_sems, recv_sems, arrs_thru, lands_thru, gathered, f"scatter_wait_{i}_{group}")
        for k, a, z in zip(scatter_groups[group], sent, lands):
            r = a.shape[1] // N_DEV
            own = lax.dynamic_slice_in_dim(a, my_idx * r, r, axis=1)[None]
            landed[k, i] = lax.dynamic_update_slice_in_dim(z, own, my_idx, axis=0)
    for k in big:
        transposed = k in ("w_in", "ffn_w_gate", "ffn_w_up")
        view = (lambda a: jnp.swapaxes(a, 1, 2)) if transposed else (lambda a: a)
        o4 = adamw(view(wts[k]), [landed[k, i] for i in range(depth)], view(mom_m[k]), view(mom_v[k]), "adamw_" + k)
        res[k] = [view(o) for o in o4]
    res["w_mod"] = list(adamw(w_mod, g_w_mod[None], m_w_mod, v_w_mod, "adamw_w_mod"))

    small_w = {k: wts[k] for k in _SMALL}
    outs = adamw(_pack(small_w)[None], gathered.reshape(N_DEV, 1, rows_s, D), _pack({k: mom_m[k] for k in _SMALL})[None],
                 _pack({k: mom_v[k] for k in _SMALL})[None], "adamw_replicated")
    for slot, packed in enumerate(outs):
        un = _unpack(packed[0], small_w)
        for k in _SMALL:
            res[k][slot] = un[k]

    return (loss, grad_x, *[res[k][0] for k in _WEIGHTS], *[res[k][1] for k in _WEIGHTS],
            *[res[k][2] for k in _WEIGHTS], *[res[k][3] for k in _WEIGHTS])
```

```python
import functools
import math

import numpy as np
import jax
import jax.numpy as jnp
from jax import lax
from jax.experimental import pallas as pl
from jax.experimental.pallas import tpu as pltpu

F32 = jnp.float32
BF16 = jnp.bfloat16
HI = lax.Precision.HIGHEST
MESH = pl.DeviceIdType.MESH

D = 1024
D_IN = 1280
D_FF = 2816
A_W = 256
B_W = 512
C_W = 256
A_HEADS = 4
CHUNK = 128
SSM_G = 32
SSM_H = 16
SSM_P = 64
GRID_W = 64
POOL_WINDOWS = (2, 4, 8, 16)
EPS = 1e-6
N_DEV = 8

TT = 256
TC = 8
ROW_W = TC * B_W
QW = ROW_W // 4
GQ = 8
FF_CHUNK = 256
VMEM_LIMIT = 60 * 1024 * 1024

ADAM_LR = 0.001
ADAM_B1 = 0.9
ADAM_B2 = 0.999
ADAM_EPS = 1e-08
ADAM_WD = 0.01
ADAM_STEP = 10


def _cp(sem):
    return pltpu.CompilerParams(dimension_semantics=sem, vmem_limit_bytes=VMEM_LIMIT)


def dot_nn(a, b):
    return jnp.dot(a, b, preferred_element_type=F32)


def dot_nt(a, b):
    return lax.dot_general(a, b, (((1,), (1,)), ((), ())), preferred_element_type=F32)


def dot_tn(a, b):
    return lax.dot_general(a, b, (((0,), (0,)), ((), ())), preferred_element_type=F32)


def split_bf16(x):
    hi = x.astype(BF16)
    lo = (x - hi.astype(F32)).astype(BF16)
    return hi, lo


def gelu(x):
    return jax.nn.gelu(x)


def gelu_grad(x):
    c = math.sqrt(2.0 / math.pi)
    t = jnp.tanh(c * (x + 0.044715 * x * x * x))
    return 0.5 * (1.0 + t) + 0.5 * x * (1.0 - t * t) * c * (1.0 + 3.0 * 0.044715 * x * x)


def rms_stats(x):
    r = lax.rsqrt(jnp.mean(x * x, axis=-1, keepdims=True) + EPS)
    return r, x * r


def rms_bwd(r, xn, dxn):
    return r * (dxn - xn * jnp.mean(dxn * xn, axis=-1, keepdims=True))


def colsum(x):
    return jnp.sum(x, axis=0, keepdims=True)


def lane_group(width, group):
    return lax.broadcasted_iota(jnp.int32, (1, width), 1) // group


def _tile_spec(width):
    return pl.BlockSpec((None, TT, width), lambda b, j: (b, j, 0))


def _mod_spec(nc):
    return pl.BlockSpec((None, None, 8, D), lambda b, j: (b, jnp.where(j >= nc, 1, 0), 0, 0))


def _full_spec(shape):
    zeros = (0,) * len(shape)
    return pl.BlockSpec(shape, lambda b, j: zeros)


def _kind_spec(shape, nc):
    zeros = (0,) * len(shape)
    return pl.BlockSpec((None,) + shape, lambda b, j: (jnp.where(j >= nc, 1, 0),) + zeros)


def _stat_spec():
    return pl.BlockSpec((None, None, 8, D), lambda b, j: (b, j, 0, 0))


def _chunk_spec():
    return pl.BlockSpec((None, TT // TC, ROW_W), lambda b, j: (b, j, 0))


def _rows_to_chunks(val, scratch, out_ref):
    for cb in range(B_W // 128):
        scratch[cb] = val[:, cb * 128:(cb + 1) * 128]
    for s in range(TC):
        for cb in range(B_W // 128):
            lo = cb * QW + s * 128
            out_ref[:, lo:lo + 128] = scratch.at[cb][pl.ds(s, TT // TC, stride=TC), :]


def _chunks_to_rows(in_ref, scratch):
    for s in range(TC):
        for cb in range(B_W // 128):
            lo = cb * QW + s * 128
            scratch.at[cb][pl.ds(s, TT // TC, stride=TC), :] = in_ref[:, lo:lo + 128]
    return jnp.concatenate([scratch[cb] for cb in range(B_W // 128)], axis=1)


def _chunk_scratch():
    return pltpu.VMEM((B_W // 128, TT, 128), F32)


def embed_tokens(x, ctx, pe):
    bl, seq, _ = x.shape
    nc = ctx.shape[1] // TT
    nt = nc + seq // TT

    def body(ctx_ref, x_ref, pe_ref, o_ref):
        j = pl.program_id(1)

        @pl.when(j < nc)
        def _():
            o_ref[...] = ctx_ref[...]

        @pl.when(j >= nc)
        def _():
            o_ref[...] = x_ref[...] + pe_ref[...]

    return pl.pallas_call(
        body, name="embed_tokens", grid=(bl, nt),
        in_specs=[pl.BlockSpec((None, TT, D), lambda b, j: (b, jnp.minimum(j, nc - 1), 0)),
                  pl.BlockSpec((None, TT, D), lambda b, j: (b, jnp.maximum(j - nc, 0), 0)),
                  pl.BlockSpec((TT, D), lambda b, j: (jnp.maximum(j - nc, 0), 0))],
        out_specs=_tile_spec(D),
        out_shape=jax.ShapeDtypeStruct((bl, nt * TT, D), F32),
        compiler_params=_cp(("arbitrary", "arbitrary")),
    )(ctx, x, pe)


def pre_mix(xs, mod, n1, w_int, nc):
    bl, s, _ = xs.shape

    def body(x_ref, mod_ref, n_ref, w_ref, za_ref, zu_ref, zp_ref, u_s):
        r, xn = rms_stats(x_ref[...])
        h = xn * n_ref[...] * (1.0 + mod_ref[1:2, :]) + mod_ref[0:1, :]
        z = dot_nt(h.astype(BF16), w_ref[...])
        za_ref[...] = z[:, :2 * A_W]
        _rows_to_chunks(z[:, 2 * A_W:2 * A_W + B_W], u_s, zu_ref)
        zp_ref[...] = z[:, 2 * A_W + B_W:]

    return pl.pallas_call(
        body, name="pre_mix", grid=(bl, s // TT),
        in_specs=[_tile_spec(D), _mod_spec(nc), _full_spec((1, D)), _full_spec((D_IN, D))],
        out_specs=[_tile_spec(2 * A_W), _chunk_spec(), _tile_spec(C_W)],
        out_shape=[jax.ShapeDtypeStruct((bl, s, 2 * A_W), F32), jax.ShapeDtypeStruct((bl, s // TC, ROW_W), F32),
                   jax.ShapeDtypeStruct((bl, s, C_W), F32)],
        scratch_shapes=[_chunk_scratch()],
        compiler_params=_cp(("arbitrary", "arbitrary")),
    )(xs, mod, n1, w_int)


def _seg_mean(x, seg_p):
    hi, lo = split_bf16(x)
    return dot_nn(hi, seg_p) + dot_nn(lo, seg_p)


def _sgu_forward(za, sw_ref, sbias, seg_p):
    ge = gelu(za)
    u, v = ge[:, :A_W], ge[:, A_W:]
    dv = v - _seg_mean(v, seg_p)
    rs = lax.rsqrt(_seg_mean(dv * dv, seg_p) + EPS)
    vn = dv * rs
    head = lane_group(A_W, A_W // A_HEADS)
    parts = []
    for c2 in range(TT // CHUNK):
        vb = vn[c2 * CHUNK:(c2 + 1) * CHUNK].astype(BF16)
        sc = sbias
        for h in range(A_HEADS):
            sc = sc + jnp.where(head == h, dot_nn(sw_ref[h], vb), 0.0)
        parts.append(sc)
    sg = jnp.concatenate(parts, axis=0)
    return u * sg, (u, vn, rs, sg)


def _pool_forward(zp, band_ref, icnt, wbd, pscale):
    hi, lo = split_bf16(zp)
    grp = lane_group(C_W, C_W // len(POOL_WINDOWS))
    q = jnp.zeros_like(zp)
    for i in range(len(POOL_WINDOWS)):
        t = dot_nn(band_ref[i], hi) + dot_nn(band_ref[i], lo)
        q = jnp.where(grp == i, t, q)
    q = q * icnt - zp
    o = dot_nn(q.astype(BF16), wbd)
    return o * pscale, (q, o)


def _glu_forward(y, glu_w, glu_b):
    g = gelu(y)
    sg = jax.nn.sigmoid(dot_nn(g.astype(BF16), glu_w) + glu_b)
    return g * sg, (g, sg)


_MIX_CONST_SHAPES = dict(sw=(A_HEADS, CHUNK, CHUNK), sbias=(CHUNK, A_W), seg_p=(A_W, A_W), wbd=(C_W, C_W),
                         pscale=(1, C_W), glu_w=(B_W, B_W), glu_b=(1, B_W), w_out=(D, D), n2=(1, D))


def _mix_const_specs(nc):
    return ([_full_spec(_MIX_CONST_SHAPES[k]) for k in ("sw", "sbias", "seg_p")]
            + [_kind_spec((len(POOL_WINDOWS), TT, TT), nc), _kind_spec((TT, C_W), nc)]
            + [_full_spec(_MIX_CONST_SHAPES[k]) for k in ("wbd", "pscale", "glu_w", "glu_b", "w_out", "n2")])


def _mix_const_args(cst):
    return [cst[k] for k in ("sw", "sbias", "seg_p", "band", "icnt", "wbd", "pscale", "glu_w", "glu_b", "w_out", "n2")]


def post_mix(xs, za, zp, ys, mod, cst, nc):
    bl, s, _ = xs.shape

    def body(x_ref, za_ref, zp_ref, y_ref, mod_ref, sw_ref, sbias_ref, seg_ref, band_ref, icnt_ref, wbd_ref,
             ps_ref, gw_ref, gb_ref, wo_ref, n2_ref, x1_ref, m_ref, y_s):
        a, _ = _sgu_forward(za_ref[...], sw_ref, sbias_ref[...], seg_ref[...])
        p, _ = _pool_forward(zp_ref[...], band_ref, icnt_ref[...], wbd_ref[...], ps_ref[...])
        sm, _ = _glu_forward(_chunks_to_rows(y_ref, y_s), gw_ref[...], gb_ref[...])
        cat = jnp.concatenate([a, sm, p], axis=1).astype(BF16)
        m = dot_nn(cat, wo_ref[...])
        _, mn = rms_stats(m)
        m_ref[...] = m
        x1_ref[...] = x_ref[...] + mod_ref[2:3, :] * (mn * n2_ref[...])

    return pl.pallas_call(
        body, name="post_mix", grid=(bl, s // TT),
        in_specs=[_tile_spec(D), _tile_spec(2 * A_W), _tile_spec(C_W), _chunk_spec(), _mod_spec(nc)]
        + _mix_const_specs(nc),
        out_specs=[_tile_spec(D), _tile_spec(D)],
        out_shape=[jax.ShapeDtypeStruct((bl, s, D), F32), jax.ShapeDtypeStruct((bl, s, D), F32)],
        scratch_shapes=[_chunk_scratch()],
        compiler_params=_cp(("arbitrary", "arbitrary")),
    )(xs, za, zp, ys, mod, *_mix_const_args(cst))


def post_mix_bwd(dx1, m, za, zp, ys, mod, cst, nc):
    bl, s, _ = m.shape
    nt = s // TT

    def body(dx_ref, m_ref, za_ref, zp_ref, y_ref, mod_ref, sw_ref, sbias_ref, seg_ref, band_ref, icnt_ref,
             wbd_ref, ps_ref, gw_ref, gb_ref, wo_ref, n2_ref,
             dza_ref, dzp_ref, dy_ref, cat_ref, dm_ref, gg_ref, dr_ref, st_ref, dsw_ref, dsb_ref, dwbd_ref, y_s):
        first = jnp.logical_and(pl.program_id(0) == 0, pl.program_id(1) == 0)

        @pl.when(first)
        def _():
            dsw_ref[...] = jnp.zeros_like(dsw_ref)
            dsb_ref[...] = jnp.zeros_like(dsb_ref)
            dwbd_ref[...] = jnp.zeros_like(dwbd_ref)

        seg_p = seg_ref[...]
        za = za_ref[...]
        zp_v = zp_ref[...]
        yv = _chunks_to_rows(y_ref, y_s)
        a, (u, vn, rs, sg) = _sgu_forward(za, sw_ref, sbias_ref[...], seg_p)
        p, (q, o) = _pool_forward(zp_v, band_ref, icnt_ref[...], wbd_ref[...], ps_ref[...])
        sm, (g, sig) = _glu_forward(yv, gw_ref[...], gb_ref[...])
        cat_ref[...] = jnp.concatenate([a, sm, p], axis=1).astype(BF16)

        dx = dx_ref[...]
        g1 = mod_ref[2:3, :]
        n2 = n2_ref[...]
        mv = m_ref[...]
        rm, mn = rms_stats(mv)
        st_ref[...] = jnp.zeros_like(st_ref)
        st_ref[0:1, :] = colsum(dx * (mn * n2))
        st_ref[1:2, :] = colsum(dx * g1 * mn)
        dm = rms_bwd(rm, mn, dx * g1 * n2)
        dmb = dm.astype(BF16)
        dm_ref[...] = dmb
        dcat = dot_nt(dmb, wo_ref[...])
        da, dsm, dp = dcat[:, :A_W], dcat[:, A_W:A_W + B_W], dcat[:, A_W + B_W:]

        du = da * sg
        dsv = da * u
        head = lane_group(A_W, A_W // A_HEADS)
        dvn_parts = []
        dsb_acc = jnp.zeros((CHUNK, A_W), F32)
        for c2 in range(TT // CHUNK):
            dsc = dsv[c2 * CHUNK:(c2 + 1) * CHUNK]
            dsc_b = dsc.astype(BF16)
            vb = vn[c2 * CHUNK:(c2 + 1) * CHUNK].astype(BF16)
            dsb_acc = dsb_acc + dsc
            dvn_c = jnp.zeros((CHUNK, A_W), F32)
            for h in range(A_HEADS):
                dsw_ref[h] += dot_nt(jnp.where(head == h, dsc, 0.0).astype(BF16), vb)
                dvn_c = dvn_c + jnp.where(head == h, dot_tn(sw_ref[h], dsc_b), 0.0)
            dvn_parts.append(dvn_c)
        dsb_ref[...] += dsb_acc
        dvn = jnp.concatenate(dvn_parts, axis=0)
        dv = rs * (dvn - _seg_mean(dvn, seg_p) - vn * _seg_mean(dvn * vn, seg_p))
        dza_ref[...] = jnp.concatenate([du, dv], axis=1) * gelu_grad(za)

        ps = ps_ref[...]
        do = dp * ps
        dps = colsum(dp * o)
        dob = do.astype(BF16)
        dwbd_ref[...] += dot_tn(q.astype(BF16), dob)
        dq = dot_nt(dob, wbd_ref[...])
        hi, lo = split_bf16(dq * icnt_ref[...])
        grp = lane_group(C_W, C_W // len(POOL_WINDOWS))
        dzp = -dq
        for i in range(len(POOL_WINDOWS)):
            t = dot_tn(band_ref[i], hi) + dot_tn(band_ref[i], lo)
            dzp = dzp + jnp.where(grp == i, t, 0.0)
        dzp_ref[...] = dzp

        dr = dsm * g * sig * (1.0 - sig)
        drb = dr.astype(BF16)
        dr_ref[...] = drb
        gg_ref[...] = g.astype(BF16)
        dg = dsm * sig + dot_nt(drb, gw_ref[...])
        _rows_to_chunks(dg * gelu_grad(yv), y_s, dy_ref)
        st_ref[2:3, :] = jnp.concatenate([colsum(dr), dps, jnp.zeros((1, D - B_W - C_W), F32)], axis=1)

    acc = lambda shape: pl.BlockSpec(shape, lambda b, j: (0,) * len(shape))
    return pl.pallas_call(
        body, name="post_mix_bwd", grid=(bl, nt),
        in_specs=[_tile_spec(D), _tile_spec(D), _tile_spec(2 * A_W), _tile_spec(C_W), _chunk_spec(), _mod_spec(nc)]
        + _mix_const_specs(nc),
        out_specs=[_tile_spec(2 * A_W), _tile_spec(C_W), _chunk_spec(), _tile_spec(D), _tile_spec(D),
                   _tile_spec(B_W), _tile_spec(B_W), _stat_spec(),
                   acc((A_HEADS, CHUNK, CHUNK)), acc((CHUNK, A_W)), acc((C_W, C_W))],
        out_shape=[jax.ShapeDtypeStruct((bl, s, 2 * A_W), F32), jax.ShapeDtypeStruct((bl, s, C_W), F32),
                   jax.ShapeDtypeStruct((bl, s // TC, ROW_W), F32), jax.ShapeDtypeStruct((bl, s, D), BF16),
                   jax.ShapeDtypeStruct((bl, s, D), BF16), jax.ShapeDtypeStruct((bl, s, B_W), BF16),
                   jax.ShapeDtypeStruct((bl, s, B_W), BF16), jax.ShapeDtypeStruct((bl, nt, 8, D), F32),
                   jax.ShapeDtypeStruct((A_HEADS, CHUNK, CHUNK), F32), jax.ShapeDtypeStruct((CHUNK, A_W), F32),
                   jax.ShapeDtypeStruct((C_W, C_W), F32)],
        scratch_shapes=[_chunk_scratch()],
        compiler_params=_cp(("arbitrary", "arbitrary")),
    )(dx1, m, za, zp, ys, mod, *_mix_const_args(cst))


def pre_mix_bwd(dza, dzu, dzp, xs, dxres, mod, n1, w_int, nc):
    bl, s, _ = xs.shape
    nt = s // TT

    def body(dza_ref, dzu_ref, dzp_ref, x_ref, dres_ref, mod_ref, n_ref, w_ref, dx_ref, h_ref, dz_ref, st_ref, u_s):
        dz = jnp.concatenate([dza_ref[...], _chunks_to_rows(dzu_ref, u_s), dzp_ref[...]], axis=1).astype(BF16)
        dz_ref[...] = dz
        dh = dot_nn(dz, w_ref[...])
        r, xn = rms_stats(x_ref[...])
        n1v = n_ref[...]
        sc = mod_ref[1:2, :]
        xg = xn * n1v
        h_ref[...] = (xg * (1.0 + sc) + mod_ref[0:1, :]).astype(BF16)
        dyv = dh * (1.0 + sc)
        st_ref[...] = jnp.zeros_like(st_ref)
        st_ref[0:1, :] = colsum(dh)
        st_ref[1:2, :] = colsum(dh * xg)
        st_ref[2:3, :] = colsum(dyv * xn)
        dx_ref[...] = dres_ref[...] + rms_bwd(r, xn, dyv * n1v)

    return pl.pallas_call(
        body, name="pre_mix_bwd", grid=(bl, nt),
        in_specs=[_tile_spec(2 * A_W), _chunk_spec(), _tile_spec(C_W), _tile_spec(D), _tile_spec(D), _mod_spec(nc),
                  _full_spec((1, D)), _full_spec((D_IN, D))],
        out_specs=[_tile_spec(D), _tile_spec(D), _tile_spec(D_IN), _stat_spec()],
        out_shape=[jax.ShapeDtypeStruct((bl, s, D), F32), jax.ShapeDtypeStruct((bl, s, D), BF16),
                   jax.ShapeDtypeStruct((bl, s, D_IN), BF16), jax.ShapeDtypeStruct((bl, nt, 8, D), F32)],
        scratch_shapes=[_chunk_scratch()],
        compiler_params=_cp(("arbitrary", "arbitrary")),
    )(dza, dzu, dzp, xs, dxres, mod, n1, w_int)


def _ffn_tile(s):
    return 768 if s % 768 == 0 else TT


def _slabs(v, n_ctx):
    return (v,) if v.shape[0] == n_ctx else (v[:n_ctx], v[n_ctx:])


def _mod_rows(mod_ref, j, row):
    lat = mod_ref[1, row:row + 1, :]
    return jnp.where(j == 0, mod_ref[0, row:row + 1, :], lat), lat


def _by_slab(fn, n_ctx, *vals_and_rows):
    outs = []
    for s in range(len(_slabs(next(v for v in vals_and_rows if not isinstance(v, tuple)), n_ctx))):
        outs.append(fn(*[v[s] if isinstance(v, tuple) else _slabs(v, n_ctx)[s] for v in vals_and_rows]))
    return outs[0] if len(outs) == 1 else jnp.concatenate(outs, axis=0)


def _split_sum(v, n_ctx, j, st_ref, row):
    parts = [colsum(p) for p in _slabs(v, n_ctx)]
    first_is_ctx = j == 0
    rest = parts[1] if len(parts) > 1 else jnp.zeros_like(parts[0])
    st_ref[row:row + 1, :] = rest + jnp.where(first_is_ctx, 0.0, parts[0])
    st_ref[row + 5:row + 6, :] = jnp.where(first_is_ctx, parts[0], 0.0)


def ffn_fwd(x1, mod, n3, n4, wg_t, wu_t, wd, n_ctx):
    bl, s, _ = x1.shape
    tf = _ffn_tile(s)
    nk = D_FF // FF_CHUNK
    tile = pl.BlockSpec((None, tf, D), lambda b, j, k: (b, j, 0))
    modspec = pl.BlockSpec((None, 2, 8, D), lambda b, j, k: (b, 0, 0, 0))
    vec = pl.BlockSpec((1, D), lambda b, j, k: (0, 0))
    wspec = pl.BlockSpec((FF_CHUNK, D), lambda b, j, k: (k, 0))
    ftile = pl.BlockSpec((None, tf, FF_CHUNK), lambda b, j, k: (b, j, k))

    def body(x_ref, mod_ref, n3_ref, n4_ref, wg_ref, wu_ref, wd_ref, x2_ref, f_ref, gate_ref, up_ref, h_s, acc_s):
        j, k = pl.program_id(1), pl.program_id(2)

        @pl.when(k == 0)
        def _():
            _, xn = rms_stats(x_ref[...])
            n3 = n3_ref[...]
            h_s[...] = _by_slab(lambda v, sh, sc: (v * n3 * (1.0 + sc) + sh).astype(BF16), n_ctx, xn,
                                _mod_rows(mod_ref, j, 3), _mod_rows(mod_ref, j, 4))
            acc_s[...] = jnp.zeros_like(acc_s)

        h = h_s[...]
        gate = dot_nt(h, wg_ref[...])
        up = dot_nt(h, wu_ref[...])
        gate_ref[...] = gate.astype(BF16)
        up_ref[...] = up.astype(BF16)
        act = (gate * jax.nn.sigmoid(gate)) * up
        acc_s[...] += dot_nn(act.astype(BF16), wd_ref[...])

        @pl.when(k == nk - 1)
        def _():
            f = acc_s[...]
            f_ref[...] = f
            _, fn = rms_stats(f)
            n4 = n4_ref[...]
            x2_ref[...] = _by_slab(lambda xv, fv, g2: xv + g2 * (fv * n4), n_ctx, x_ref[...], fn, _mod_rows(mod_ref, j, 5))

    return pl.pallas_call(
        body, name="ffn_fwd", grid=(bl, s // tf, nk),
        in_specs=[tile, modspec, vec, vec, wspec, wspec, wspec],
        out_specs=[tile, tile, ftile, ftile],
        out_shape=[jax.ShapeDtypeStruct((bl, s, D), F32), jax.ShapeDtypeStruct((bl, s, D), F32),
                   jax.ShapeDtypeStruct((bl, s, D_FF), BF16), jax.ShapeDtypeStruct((bl, s, D_FF), BF16)],
        scratch_shapes=[pltpu.VMEM((tf, D), BF16), pltpu.VMEM((tf, D), F32)],
        compiler_params=_cp(("arbitrary", "arbitrary", "arbitrary")),
    )(x1, mod, n3, n4, wg_t, wu_t, wd)


def ffn_bwd(dx2, x1, f, gate_b, up_b, mod, n3, n4, wg_t, wu_t, wd, n_ctx):
    bl, s, _ = x1.shape
    tf = _ffn_tile(s)
    nt = s // tf
    nk = D_FF // FF_CHUNK
    tile = pl.BlockSpec((None, tf, D), lambda b, j, k: (b, j, 0))
    ftile = pl.BlockSpec((None, tf, FF_CHUNK), lambda b, j, k: (b, j, jnp.minimum(k, nk - 1)))
    modspec = pl.BlockSpec((None, 2, 8, D), lambda b, j, k: (b, 0, 0, 0))
    vec = pl.BlockSpec((1, D), lambda b, j, k: (0, 0))
    wspec = pl.BlockSpec((FF_CHUNK, D), lambda b, j, k: (jnp.minimum(k, nk - 1), 0))
    wprev = pl.BlockSpec((FF_CHUNK, D), lambda b, j, k: (jnp.maximum(k - 1, 0), 0))
    stat = pl.BlockSpec((None, None, 8, D), lambda b, j, k: (b, j, 0, 0))

    def body(dx_ref, x_ref, f_ref, gate_ref, up_ref, mod_ref, n3_ref, n4_ref, wd_ref, wgp_ref, wup_ref,
             dx1_ref, h_ref, df_ref, act_ref, dgate_ref, dup_ref, st_ref, df_s, acc_s, dgate_s, dup_s):
        j, k = pl.program_id(1), pl.program_id(2)

        @pl.when(k == 0)
        def _():
            dx = dx_ref[...]
            n4 = n4_ref[...]
            rf, fn = rms_stats(f_ref[...])
            st_ref[...] = jnp.zeros_like(st_ref)
            _split_sum(dx * (fn * n4), n_ctx, j, st_ref, 2)
            dxg = _by_slab(lambda dv, g2: dv * g2, n_ctx, dx, _mod_rows(mod_ref, j, 5))
            st_ref[4:5, :] = colsum(dxg * fn)
            df = rms_bwd(rf, fn, dxg * n4).astype(BF16)
            df_s[...] = df
            df_ref[...] = df
            _, xn = rms_stats(x_ref[...])
            n3 = n3_ref[...]
            h_ref[...] = _by_slab(lambda v, sh, sc: (v * n3 * (1.0 + sc) + sh).astype(BF16), n_ctx, xn,
                                  _mod_rows(mod_ref, j, 3), _mod_rows(mod_ref, j, 4))
            acc_s[...] = jnp.zeros_like(acc_s)
            dgate_s[1] = jnp.zeros((tf, FF_CHUNK), BF16)
            dup_s[1] = jnp.zeros((tf, FF_CHUNK), BF16)

        prev = (k + 1) % 2
        acc_s[...] += dot_nn(dgate_s[prev], wgp_ref[...]) + dot_nn(dup_s[prev], wup_ref[...])
        gate = gate_ref[...].astype(F32)
        up = up_ref[...].astype(F32)
        sg = jax.nn.sigmoid(gate)
        silu = gate * sg
        dact = dot_nt(df_s[...], wd_ref[...])
        act_ref[...] = (silu * up).astype(BF16)
        dgate = (dact * up * (sg * (1.0 + gate * (1.0 - sg)))).astype(BF16)
        dup = (dact * silu).astype(BF16)
        dgate_ref[...] = dgate
        dup_ref[...] = dup
        dgate_s[k % 2] = dgate
        dup_s[k % 2] = dup

        @pl.when(k == nk)
        def _():
            dh = acc_s[...]
            r, xn = rms_stats(x_ref[...])
            n3 = n3_ref[...]
            xg = xn * n3
            dyv = _by_slab(lambda dv, sc: dv * (1.0 + sc), n_ctx, dh, _mod_rows(mod_ref, j, 4))
            _split_sum(dh, n_ctx, j, st_ref, 0)
            _split_sum(dh * xg, n_ctx, j, st_ref, 1)
            st_ref[3:4, :] = colsum(dyv * xn)
            dx1_ref[...] = dx_ref[...] + rms_bwd(r, xn, dyv * n3)

    return pl.pallas_call(
        body, name="ffn_bwd", grid=(bl, nt, nk + 1),
        in_specs=[tile, tile, tile, ftile, ftile, modspec, vec, vec, wspec, wprev, wprev],
        out_specs=[tile, tile, tile, ftile, ftile, ftile, stat],
        out_shape=[jax.ShapeDtypeStruct((bl, s, D), F32), jax.ShapeDtypeStruct((bl, s, D), BF16),
                   jax.ShapeDtypeStruct((bl, s, D), BF16), jax.ShapeDtypeStruct((bl, s, D_FF), BF16),
                   jax.ShapeDtypeStruct((bl, s, D_FF), BF16), jax.ShapeDtypeStruct((bl, s, D_FF), BF16),
                   jax.ShapeDtypeStruct((bl, nt, 8, D), F32)],
        scratch_shapes=[pltpu.VMEM((tf, D), BF16), pltpu.VMEM((tf, D), F32),
                        pltpu.VMEM((2, tf, FF_CHUNK), BF16), pltpu.VMEM((2, tf, FF_CHUNK), BF16)],
        compiler_params=_cp(("arbitrary", "arbitrary", "arbitrary")),
    )(dx2, x1, f, gate_b, up_b, mod, n3, n4, wd, wg_t, wu_t)


def loss_head(xs, target, nc):
    bl, s, _ = xs.shape
    nt = s // TT

    def body(x_ref, t_ref, dx_ref, l_ref):
        j = pl.program_id(1)

        @pl.when(j < nc)
        def _():
            dx_ref[...] = jnp.zeros_like(dx_ref)
            l_ref[...] = jnp.zeros_like(l_ref)

        @pl.when(j >= nc)
        def _():
            e = x_ref[...] - t_ref[...]
            dx_ref[...] = e * (1.0 / D)
            tok = jnp.mean(e * e, axis=-1, keepdims=True)
            l_ref[...] = jnp.zeros_like(l_ref) + 0.5 * jnp.sum(tok, axis=0, keepdims=True)

    return pl.pallas_call(
        body, name="loss_head", grid=(bl, nt),
        in_specs=[_tile_spec(D), pl.BlockSpec((None, TT, D), lambda b, j: (b, jnp.maximum(j - nc, 0), 0))],
        out_specs=[_tile_spec(D), pl.BlockSpec((None, None, 8, 128), lambda b, j: (b, j, 0, 0))],
        out_shape=[jax.ShapeDtypeStruct((bl, s, D), F32), jax.ShapeDtypeStruct((bl, nt, 8, 128), F32)],
        compiler_params=_cp(("arbitrary", "arbitrary")),
    )(xs, target)


def tn_matmul(a, b, name):
    t, ka = a.shape
    n = b.shape[1]
    tk = ka if ka <= 1408 else ka // 2
    tt = next(x for x in (1024, 512, 256) if t % x == 0)
    nsteps = t // tt

    def body(a_ref, b_ref, o_ref, acc_s):
        @pl.when(pl.program_id(1) == 0)
        def _():
            acc_s[...] = jnp.zeros_like(acc_s)

        acc_s[...] += dot_tn(b_ref[...], a_ref[...])

        @pl.when(pl.program_id(1) == nsteps - 1)
        def _():
            o_ref[...] = acc_s[...].T.astype(BF16)

    return pl.pallas_call(
        body, name=name, grid=(ka // tk, nsteps),
        in_specs=[pl.BlockSpec((tt, tk), lambda i, s: (s, i)), pl.BlockSpec((tt, n), lambda i, s: (s, 0))],
        out_specs=pl.BlockSpec((tk, n), lambda i, s: (i, 0)),
        out_shape=jax.ShapeDtypeStruct((ka, n), BF16),
        scratch_shapes=[pltpu.VMEM((n, tk), F32)],
        compiler_params=_cp(("arbitrary", "arbitrary")),
    )(a, b)


def qmm(terms, name):
    r = terms[0][0].shape[0]
    rt = r // 2 if r % 16 == 0 and r >= 512 else r
    n = len(terms)

    def body(*refs):
        acc = None
        for k in range(n):
            y = dot_nn(refs[2 * k][...].astype(BF16), refs[2 * k + 1][...])
            acc = y if acc is None else acc + y
        refs[2 * n][...] = acc

    row = pl.BlockSpec((rt, QW), lambda q, i: (i, q))
    wspec = pl.BlockSpec((None, QW, QW), lambda q, i: (q, 0, 0))
    return pl.pallas_call(
        body, name=name, grid=(4, r // rt), in_specs=[row, wspec] * n, out_specs=row,
        out_shape=jax.ShapeDtypeStruct((r, ROW_W), F32),
        compiler_params=_cp(("arbitrary", "arbitrary")),
    )(*[x for term in terms for x in term])


def _same_group(rows, cols, row_group, col_group):
    ri = jnp.bitwise_and(lax.broadcasted_iota(jnp.int32, (rows, cols), 0) // row_group, GQ - 1)
    ci = jnp.bitwise_and(lax.broadcasted_iota(jnp.int32, (rows, cols), 1) // col_group, GQ - 1)
    return ri == ci


def _spread_matrix():
    m = np.zeros((2 * SSM_P, QW), np.float32)
    for reim in range(2):
        for g in range(GQ):
            for p in range(SSM_P):
                m[reim * SSM_P + p, reim * (QW // 2) + g * SSM_P + p] = 1.0
    return jnp.asarray(m, BF16)


def assemble_ts(v, name):
    def body(v_ref, f_ref, big_ref, bigt_ref):
        keep = _same_group(GQ * SSM_H, QW, SSM_H, SSM_P)
        for e in range(TC):
            hi, lo = split_bf16(v_ref[e])
            t = jnp.where(keep, dot_nn(hi, f_ref[...]) + dot_nn(lo, f_ref[...]), 0.0)
            big_ref[e * 128:(e + 1) * 128, :] = t.astype(BF16)
            bigt_ref[:, e * 128:(e + 1) * 128] = t.T.astype(BF16)

    return pl.pallas_call(
        body, name=name, grid=(4,),
        in_specs=[pl.BlockSpec((None, TC, 128, 128), lambda q: (q, 0, 0, 0)),
                  pl.BlockSpec((128, QW), lambda q: (0, 0))],
        out_specs=[pl.BlockSpec((None, QW, QW), lambda q: (q, 0, 0))] * 2,
        out_shape=[jax.ShapeDtypeStruct((4, QW, QW), BF16), jax.ShapeDtypeStruct((4, QW, QW), BF16)],
        compiler_params=_cp(("arbitrary",)),
    )(v, _spread_matrix())


def assemble_tt(lags, name):
    def body(l_ref, m_ref, mt_ref):
        blocks = [l_ref[n] for n in range(2 * TC - 1)]
        flipped = [b.T.astype(BF16) for b in blocks]
        blocks = [b.astype(BF16) for b in blocks]
        for s in range(TC):
            for t in range(TC):
                m_ref[s * 128:(s + 1) * 128, t * 128:(t + 1) * 128] = blocks[t - s + TC - 1]
                mt_ref[t * 128:(t + 1) * 128, s * 128:(s + 1) * 128] = flipped[t - s + TC - 1]

    return pl.pallas_call(
        body, name=name, grid=(4,),
        in_specs=[pl.BlockSpec((None, 2 * TC - 1, 128, 128), lambda q: (q, 0, 0, 0))],
        out_specs=[pl.BlockSpec((None, QW, QW), lambda q: (q, 0, 0))] * 2,
        out_shape=[jax.ShapeDtypeStruct((4, QW, QW), BF16)] * 2,
        compiler_params=_cp(("arbitrary",)),
    )(lags)


def _qtn_call(body, a, b, out_shape, out_block, extra, name):
    r = a.shape[0]
    col = pl.BlockSpec((r, QW), lambda q: (0, q))
    return pl.pallas_call(
        body, name=name, grid=(4,),
        in_specs=[col, col] + [pl.BlockSpec(x.shape, lambda q: (0, 0)) for x in extra],
        out_specs=pl.BlockSpec((None,) + out_block, lambda q: (q,) + (0,) * len(out_block)),
        out_shape=jax.ShapeDtypeStruct((4,) + out_block, F32),
        compiler_params=_cp(("arbitrary",)),
    )(a, b, *extra)


def qtn_ts(a, b, name):
    def body(a_ref, b_ref, f_ref, o_ref):
        full = dot_tn(a_ref[...].astype(BF16), b_ref[...].astype(BF16))
        keep = _same_group(GQ * SSM_H, QW, SSM_H, SSM_P)
        for e in range(TC):
            hi, lo = split_bf16(jnp.where(keep, full[e * 128:(e + 1) * 128, :], 0.0))
            o_ref[e] = dot_nt(hi, f_ref[...]) + dot_nt(lo, f_ref[...])

    return _qtn_call(body, a, b, None, (TC, 128, 128), [_spread_matrix()], name)


def qtn_tt(a, b, name):
    def body(a_ref, b_ref, o_ref):
        full = dot_tn(a_ref[...].astype(BF16), b_ref[...].astype(BF16))
        for lag in range(-(TC - 1), TC):
            acc = None
            for s in range(TC):
                t = s + lag
                if 0 <= t < TC:
                    blk = full[s * 128:(s + 1) * 128, t * 128:(t + 1) * 128]
                    acc = blk if acc is None else acc + blk
            o_ref[lag + TC - 1] = acc

    return _qtn_call(body, a, b, None, (2 * TC - 1, 128, 128), [], name)


def _scan_row(i, rb, ncr, reverse):
    if not reverse:
        return i
    return jnp.where(i < ncr, ncr - 1 - i, rb - 1 - (i - ncr))


def _swap_re_im(h):
    half = QW // 2
    return jnp.concatenate([h[:, q * QW + (1 - k) * half:q * QW + (2 - k) * half] for q in range(4) for k in range(2)],
                           axis=1)


def chunk_scan(xs, lam_ab, ncr, reverse, name):
    bl, rb, _ = xs.shape

    def body(x_ref, l_ref, hp_ref):
        la, lb = l_ref[0:1, :], l_ref[1:2, :]

        def step(i, h):
            row = _scan_row(i, rb, ncr, reverse)
            hp_ref[pl.ds(row, 1), :] = h
            return la * h + lb * _swap_re_im(h) + x_ref[pl.ds(row, 1), :]

        lax.fori_loop(0, rb, step, jnp.zeros((1, ROW_W), F32))

    blk = pl.BlockSpec((None, rb, ROW_W), lambda b: (b, 0, 0))
    return pl.pallas_call(
        body, name=name, grid=(bl,),
        in_specs=[blk, pl.BlockSpec((8, ROW_W), lambda b: (0, 0))], out_specs=blk,
        out_shape=jax.ShapeDtypeStruct((bl, rb, ROW_W), F32),
        compiler_params=_cp(("arbitrary",)),
    )(xs, lam_ab)


def chunk_scan_bwd(dhp, hp, lam_ab, ncr, reverse, name):
    bl, rb, _ = dhp.shape

    def body(d_ref, hp_ref, l_ref, g_ref, dl_ref):
        la, lb = l_ref[0:1, :], l_ref[1:2, :]

        dl_ref[...] = jnp.zeros_like(dl_ref)

        def step(n, g):
            row = _scan_row(rb - 1 - n, rb, ncr, reverse)
            g_ref[pl.ds(row, 1), :] = g
            pv = hp_ref[pl.ds(row, 1), :]
            dl_ref[0:1, :] += g * pv
            dl_ref[1:2, :] += g * _swap_re_im(pv)
            return d_ref[pl.ds(row, 1), :] + la * g + _swap_re_im(lb * g)

        lax.fori_loop(0, rb, step, jnp.zeros((1, ROW_W), F32))

    blk = pl.BlockSpec((None, rb, ROW_W), lambda b: (b, 0, 0))
    return pl.pallas_call(
        body, name=name, grid=(bl,),
        in_specs=[blk, blk, pl.BlockSpec((8, ROW_W), lambda b: (0, 0))],
        out_specs=[blk, pl.BlockSpec((None, 8, ROW_W), lambda b: (b, 0, 0))],
        out_shape=[jax.ShapeDtypeStruct((bl, rb, ROW_W), F32), jax.ShapeDtypeStruct((bl, 8, ROW_W), F32)],
        compiler_params=_cp(("arbitrary",)),
    )(dhp, hp, lam_ab)


def _quarter_rows(v):
    e = v.shape[0]
    return v.reshape(e, 4, 8, SSM_P, SSM_H).transpose(0, 1, 2, 4, 3).reshape(e, 4, 8 * SSM_H, SSM_P)


def _token_state_map(vr, vi):
    return jnp.concatenate([_quarter_rows(vr), _quarter_rows(vi)], axis=-1).transpose(1, 0, 2, 3)


def ssm_build(lam_re, lam_im, log_dt, b_re, b_im, c_re, c_im, d):
    dt = jnp.exp(log_dt)[..., None]
    mag = jnp.exp(lam_re * dt)
    ang = lam_im * dt
    lr, li = mag * jnp.cos(ang), mag * jnp.sin(ang)
    den = lam_re * lam_re + lam_im * lam_im
    nr = lr - 1.0
    fr = (nr * lam_re + li * lam_im) / den
    fi = (li * lam_re - nr * lam_im) / den
    bbr = fr[..., None] * b_re - fi[..., None] * b_im
    bbi = fr[..., None] * b_im + fi[..., None] * b_re
    pr, pi = [jnp.ones_like(lr)], [jnp.zeros_like(lr)]
    for _ in range(TC):
        pr, pi = pr + [pr[-1] * lr - pi[-1] * li], pi + [pr[-1] * li + pi[-1] * lr]
    pr, pi = jnp.stack(pr), jnp.stack(pi)
    clr = c_re[None] * pr[:, :, :, None, :] - c_im[None] * pi[:, :, :, None, :]
    cli = c_re[None] * pi[:, :, :, None, :] + c_im[None] * pr[:, :, :, None, :]
    same_group = jnp.asarray(np.kron(np.eye(8), np.ones((SSM_H, SSM_H))), F32)
    ein = functools.partial(jnp.einsum, precision=HI)

    out, lag_blocks = {}, {}
    for k, name in ((0, "f"), (1, "r")):
        ar, ai = _quarter_rows(bbr[k][None])[0], _quarter_rows(bbi[k][None])[0]
        cr = clr[:TC, k].reshape(TC, 4, 8 * SSM_H, SSM_P)
        ci = cli[:TC, k].reshape(TC, 4, 8 * SSM_H, SSM_P)
        lag_blocks[k] = (ein('qap,nqbp->nqab', ar, cr) - ein('qap,nqbp->nqab', ai, ci)) * same_group
        es = [TC - 1 - s for s in range(TC)] if k == 0 else list(range(TC))
        sr = jnp.stack([pr[e, k][:, :, None] * bbr[k] - pi[e, k][:, :, None] * bbi[k] for e in es])
        si = jnp.stack([pr[e, k][:, :, None] * bbi[k] + pi[e, k][:, :, None] * bbr[k] for e in es])
        out["bs_" + name] = _token_state_map(sr, si)
        et = [t + 1 for t in range(TC)] if k == 0 else [TC - t for t in range(TC)]
        crt = jnp.stack([jnp.swapaxes(clr[e, k], 1, 2) for e in et])
        cit = jnp.stack([-jnp.swapaxes(cli[e, k], 1, 2) for e in et])
        out["cst_" + name] = _token_state_map(crt, cit)
        l8r, l8i = pr[TC, k].reshape(4, 1, QW // 2), pi[TC, k].reshape(4, 1, QW // 2)
        la = jnp.concatenate([l8r, l8r], axis=1).reshape(1, ROW_W)
        lb = jnp.concatenate([-l8i, l8i], axis=1).reshape(1, ROW_W)
        out["lam_" + name] = jnp.concatenate([la, lb, jnp.zeros((6, ROW_W), F32)], axis=0)
    skip = jnp.eye(8 * SSM_H, dtype=F32)[None] * d.reshape(4, 1, 8 * SSM_H)
    center = lag_blocks[0][0] + lag_blocks[1][0] + skip
    lags = [lag_blocks[1][n] for n in range(TC - 1, 0, -1)] + [center] + [lag_blocks[0][n] for n in range(1, TC)]
    out["lags"] = jnp.stack(lags, axis=1)
    return out


def ssm_operators(mats, tag):
    ops = {}
    ops["m"], ops["mt"] = assemble_tt(mats["lags"], "ssm_map_intra" + tag)
    for dname in ("f", "r"):
        ops["bs_" + dname], ops["bst_" + dname] = assemble_ts(mats["bs_" + dname], f"ssm_map_state_in_{dname}{tag}")
        ops["cst_" + dname], ops["cs_" + dname] = assemble_ts(mats["cst_" + dname], f"ssm_map_readout_{dname}{tag}")
    return ops


def ssm_forward(u3, mats, ops, ncr):
    bl, rb, _ = u3.shape
    u = u3.reshape(bl * rb, ROW_W)
    hps, terms = {}, [(u, ops["m"])]
    for dname, rev in (("f", False), ("r", True)):
        xs = qmm([(u, ops["bs_" + dname])], "ssm_state_in_" + dname)
        hp = chunk_scan(xs.reshape(bl, rb, ROW_W), mats["lam_" + dname], ncr, rev, "ssm_scan_" + dname)
        hps[dname] = hp.reshape(bl * rb, ROW_W)
        terms.append((hps[dname], ops["cs_" + dname]))
    return qmm(terms, "ssm_output").reshape(bl, rb, ROW_W), hps


def ssm_backward(dy3, u3, hps, mats, ops, ncr):
    bl, rb, _ = u3.shape
    u = u3.reshape(bl * rb, ROW_W)
    dyr = dy3.reshape(bl * rb, ROW_W)
    cot = {"lags": qtn_tt(u, dyr, "ssm_d_intra")}
    terms = [(dyr, ops["mt"])]
    for dname, rev in (("f", False), ("r", True)):
        dhp = qmm([(dyr, ops["cst_" + dname])], "ssm_dstate_" + dname)
        g, dl = chunk_scan_bwd(dhp.reshape(bl, rb, ROW_W), hps[dname].reshape(bl, rb, ROW_W), mats["lam_" + dname],
                               ncr, rev, "ssm_scan_bwd_" + dname)
        g = g.reshape(bl * rb, ROW_W)
        cot["lam_" + dname] = jnp.sum(dl, axis=0)
        cot["bs_" + dname] = qtn_ts(u, g, "ssm_d_state_in_" + dname)
        cot["cst_" + dname] = qtn_ts(dyr, hps[dname], "ssm_d_readout_" + dname)
        terms.append((g, ops["bst_" + dname]))
    return qmm(terms, "ssm_input_grad").reshape(bl, rb, ROW_W), cot


def mod_forward(act, w_mod, b_cols):
    nl, _, wc = w_mod.shape
    r = act.shape[0]

    def body(a_ref, w_ref, b_ref, o_ref):
        o_ref[...] = dot_nn(a_ref[...].astype(BF16), w_ref[...].astype(BF16)) + b_ref[...]

    return pl.pallas_call(
        body, name="mod_forward", grid=(nl,),
        in_specs=[pl.BlockSpec((r, D), lambda l: (0, 0)), pl.BlockSpec((None, D, wc), lambda l: (l, 0, 0)),
                  pl.BlockSpec((None, 1, wc), lambda l: (l, 0, 0))],
        out_specs=pl.BlockSpec((None, r, wc), lambda l: (l, 0, 0)),
        out_shape=jax.ShapeDtypeStruct((nl, r, wc), F32),
        compiler_params=_cp(("arbitrary",)),
    )(act, w_mod, b_cols)


def mod_backward(act, dmod, dctx, w_mod):
    nl, _, wc = w_mod.shape
    r = act.shape[0]

    def body(a_ref, d_ref, c_ref, w_ref, gw_ref, gc_ref):
        gw_ref[...] = dot_tn(a_ref[...].astype(BF16), d_ref[...].astype(BF16))
        gc_ref[...] = dot_nt(c_ref[...].astype(BF16), w_ref[...].astype(BF16))

    return pl.pallas_call(
        body, name="mod_backward", grid=(nl,),
        in_specs=[pl.BlockSpec((r, D), lambda l: (0, 0)), pl.BlockSpec((None, r, wc), lambda l: (l, 0, 0)),
                  pl.BlockSpec((None, 8, wc), lambda l: (l, 0, 0)), pl.BlockSpec((None, D, wc), lambda l: (l, 0, 0))],
        out_specs=[pl.BlockSpec((None, D, wc), lambda l: (l, 0, 0)), pl.BlockSpec((None, 8, D), lambda l: (l, 0, 0))],
        out_shape=[jax.ShapeDtypeStruct((nl, D, wc), F32), jax.ShapeDtypeStruct((nl, 8, D), F32)],
        compiler_params=_cp(("arbitrary",)),
    )(act, dmod, dctx, w_mod)


def _place():
    return lax.axis_index("x"), lax.axis_index("y"), lax.axis_index("c")


def all_gather_rows(arrs, name):
    n = len(arrs)
    rs = [a.shape[1] for a in arrs]

    def body(*refs):
        x_refs, o_refs = refs[:n], refs[n:2 * n]
        send_sems, recv_sems, local_sems = refs[2 * n:]
        x, y, c = _place()
        me, sibling = (x, y, c), (x, y, 1 - c)
        chips = [(1 - x, y), (x, 1 - y), (1 - x, 1 - y)]

        def rows(a, px, py, pc):
            return o_refs[a].at[:, pl.ds((4 * px + 2 * py + pc) * rs[a], rs[a]), :]

        def copy(a, k, block, to, src=None):
            return pltpu.make_async_remote_copy(
                src_ref=rows(a, *block) if src is None else src, dst_ref=rows(a, *block),
                send_sem=send_sems.at[a, k], recv_sem=recv_sems.at[a, k], device_id=to, device_id_type=MESH)

        mine = [pltpu.make_async_copy(x_refs[a], rows(a, *me), local_sems.at[a]) for a in range(n)]
        for cp in mine:
            cp.start()
        first = []
        for a in range(n):
            first.append(copy(a, 0, me, sibling, src=x_refs[a]))
            first += [copy(a, 1 + j, me, (*chip, c), src=x_refs[a]) for j, chip in enumerate(chips)]
        for cp in first:
            cp.start()
        passed = []
        for j, chip in enumerate(chips):
            for a in range(n):
                copy(a, 1 + j, (*chip, c), me).wait_recv()
                fwd = copy(a, 4 + j, (*chip, c), sibling)
                fwd.start()
                passed.append(fwd)
        for a in range(n):
            copy(a, 0, sibling, me).wait_recv()
            for j, chip in enumerate(chips):
                copy(a, 4 + j, (*chip, 1 - c), me).wait_recv()
        for cp in first + passed:
            cp.wait_send()
        for cp in mine:
            cp.wait()

    any_spec = pl.BlockSpec(memory_space=pl.ANY)
    return pl.pallas_call(
        body, name=name,
        in_specs=[any_spec] * n, out_specs=[any_spec] * n,
        out_shape=[jax.ShapeDtypeStruct((a.shape[0], N_DEV * a.shape[1], a.shape[2]), a.dtype) for a in arrs],
        scratch_shapes=[pltpu.SemaphoreType.DMA((n, 7)), pltpu.SemaphoreType.DMA((n, 7)), pltpu.SemaphoreType.DMA((n,))],
    )(*arrs)


def all_to_all_rows(arrs, name):
    n = len(arrs)
    rs = [a.shape[1] // N_DEV for a in arrs]
    flips = [(fx, fy, fc) for fx in (0, 1) for fy in (0, 1) for fc in (0, 1)][1:]

    def body(*refs):
        x_refs, o_refs = refs[:n], refs[n:2 * n]
        send_sems, recv_sems, local_sems = refs[2 * n:]
        x, y, c = _place()
        my_idx = 4 * x + 2 * y + c

        def block(a, idx):
            return x_refs[a].at[:, pl.ds(idx * rs[a], rs[a]), :]

        mine = [pltpu.make_async_copy(block(a, my_idx), o_refs[a].at[my_idx], local_sems.at[a]) for a in range(n)]
        for cp in mine:
            cp.start()
        sends = []
        for k, (fx, fy, fc) in enumerate(flips):
            px = 1 - x if fx else x
            py = 1 - y if fy else y
            pc = 1 - c if fc else c
            p_idx = 4 * px + 2 * py + pc
            for a in range(n):
                sends.append(pltpu.make_async_remote_copy(
                    src_ref=block(a, p_idx), dst_ref=o_refs[a].at[my_idx], send_sem=send_sems.at[a, k],
                    recv_sem=recv_sems.at[a, k], device_id=(px, py, pc), device_id_type=MESH))
        for cp in sends:
            cp.start()
        for k, (fx, fy, fc) in enumerate(flips):
            px = 1 - x if fx else x
            py = 1 - y if fy else y
            pc = 1 - c if fc else c
            p_idx = 4 * px + 2 * py + pc
            for a in range(n):
                pltpu.make_async_remote_copy(
                    src_ref=block(a, p_idx), dst_ref=o_refs[a].at[p_idx], send_sem=send_sems.at[a, k],
                    recv_sem=recv_sems.at[a, k], device_id=(px, py, pc), device_id_type=MESH).wait_recv()
        for cp in sends:
            cp.wait_send()
        for cp in mine:
            cp.wait()

    any_spec = pl.BlockSpec(memory_space=pl.ANY)
    return pl.pallas_call(
        body, name=name,
        in_specs=[any_spec] * n, out_specs=[any_spec] * n,
        out_shape=[jax.ShapeDtypeStruct((N_DEV, a.shape[0], r, a.shape[2]), a.dtype) for a, r in zip(arrs, rs)],
        scratch_shapes=[pltpu.SemaphoreType.DMA((n, 7)), pltpu.SemaphoreType.DMA((n, 7)), pltpu.SemaphoreType.DMA((n,))],
    )(*arrs)


def _peers():
    x, y, c = _place()
    out = []
    for fx in (0, 1):
        for fy in (0, 1):
            for fc in (0, 1):
                if fx or fy or fc:
                    px, py, pc = (1 - x if fx else x), (1 - y if fy else y), (1 - c if fc else c)
                    out.append(((px, py, pc), 4 * px + 2 * py + pc))
    return out, 4 * x + 2 * y + c


def _split_call(body, name, ins, n_sem_out, thru, extra_out_shape, extra_out_specs, sem_ins=(), after=None):
    hbm = pl.BlockSpec(memory_space=pltpu.HBM)
    sem = pl.BlockSpec(memory_space=pltpu.SEMAPHORE)
    n_thru = len(thru)
    tail_in = [sem] * len(sem_ins) + ([pl.BlockSpec(memory_space=pl.ANY)] if after is not None else [])
    return pl.pallas_call(
        body, name=name,
        out_shape=tuple(n_sem_out) + tuple(pltpu.HBM(a.shape, a.dtype) for a in thru) + tuple(extra_out_shape),
        in_specs=[hbm] * n_thru + tail_in,
        out_specs=(sem,) * len(n_sem_out) + (hbm,) * n_thru + tuple(extra_out_specs),
        input_output_aliases={i: i + len(n_sem_out) for i in range(n_thru)},
        compiler_params=pltpu.CompilerParams(has_side_effects=pltpu.SideEffectType.DATAFLOW_SIDE_EFFECTING),
    )(*ins, *sem_ins, *([after] if after is not None else []))


def gather_start(shards, after, name):
    n = len(shards)
    rs = [a.shape[1] for a in shards]
    lands = [lax.empty((a.shape[0], N_DEV * a.shape[1], a.shape[2]), a.dtype) for a in shards]

    def body(*refs):
        x_refs, land_refs = refs[:n], refs[n:2 * n]
        send_sems, recv_sems = refs[2 * n + 1], refs[2 * n + 2]
        peers, my_idx = _peers()
        for k, (peer, _) in enumerate(peers):
            for a in range(n):
                pltpu.make_async_remote_copy(
                    src_ref=x_refs[a], dst_ref=land_refs[a].at[:, pl.ds(my_idx * rs[a], rs[a]), :],
                    send_sem=send_sems.at[a * 7 + k], recv_sem=recv_sems.at[a * 7 + k], device_id=peer,
                    device_id_type=MESH).start()
        refs[-1][...] = jnp.zeros_like(refs[-1])

    ins = [pltpu.with_memory_space_constraint(a, pltpu.HBM) for a in list(shards) + lands]
    outs = _split_call(body, name, ins, [pltpu.SemaphoreType.DMA((n * 7,))] * 2, ins,
                       [jax.ShapeDtypeStruct((8, 128), F32)], [pl.BlockSpec(memory_space=pltpu.VMEM)], after=after)
    return outs[0], outs[1], list(outs[2:2 + n]), list(outs[2 + n:2 + 2 * n]), outs[-1]


def gather_wait(send_sems, recv_sems, shards, lands, after, name):
    n = len(shards)
    rs = [a.shape[1] for a in shards]

    def body(*refs):
        x_refs, land_refs = refs[:n], refs[n:2 * n]
        s_sems, r_sems = refs[2 * n], refs[2 * n + 1]
        peers, _ = _peers()
        for k, (peer, p_idx) in enumerate(peers):
            for a in range(n):
                copy = pltpu.make_async_remote_copy(
                    src_ref=x_refs[a], dst_ref=land_refs[a].at[:, pl.ds(p_idx * rs[a], rs[a]), :],
                    send_sem=s_sems.at[a * 7 + k], recv_sem=r_sems.at[a * 7 + k], device_id=peer, device_id_type=MESH)
                copy.wait_send()
                copy.wait_recv()

    outs = _split_call(body, name, list(shards) + list(lands), [], list(shards) + list(lands), [], [],
                       sem_ins=(send_sems, recv_sems), after=after)
    my_idx = 4 * lax.axis_index("x") + 2 * lax.axis_index("y") + lax.axis_index("c")
    return [lax.dynamic_update_slice_in_dim(z, s, my_idx * r, axis=1) for z, s, r in zip(outs[n:], outs[:n], rs)]


def scatter_start(arrs, name):
    n = len(arrs)
    rs = [a.shape[1] // N_DEV for a in arrs]
    lands = [lax.empty((N_DEV, a.shape[0], r, a.shape[2]), a.dtype) for a, r in zip(arrs, rs)]

    def body(*refs):
        x_refs, land_refs = refs[:n], refs[n:2 * n]
        send_sems, recv_sems = refs[2 * n], refs[2 * n + 1]
        token = refs[-1]
        peers, my_idx = _peers()
        for k, (peer, p_idx) in enumerate(peers):
            for a in range(n):
                pltpu.make_async_remote_copy(
                    src_ref=x_refs[a].at[:, pl.ds(p_idx * rs[a], rs[a]), :], dst_ref=land_refs[a].at[my_idx],
                    send_sem=send_sems.at[a * 7 + k], recv_sem=recv_sems.at[a * 7 + k], device_id=peer,
                    device_id_type=MESH).start()
        token[...] = jnp.zeros_like(token)

    hbm = pl.BlockSpec(memory_space=pltpu.HBM)
    sem = pl.BlockSpec(memory_space=pltpu.SEMAPHORE)
    outs = pl.pallas_call(
        body, name=name,
        out_shape=(pltpu.SemaphoreType.DMA((n * 7,)), pltpu.SemaphoreType.DMA((n * 7,)))
        + tuple(pltpu.HBM(a.shape, a.dtype) for a in arrs) + tuple(pltpu.HBM(z.shape, z.dtype) for z in lands)
        + (jax.ShapeDtypeStruct((8, 128), F32),),
        in_specs=[hbm] * (2 * n),
        out_specs=(sem, sem) + (hbm,) * (2 * n) + (pl.BlockSpec(memory_space=pltpu.VMEM),),
        input_output_aliases={i: i + 2 for i in range(2 * n)},
        compiler_params=pltpu.CompilerParams(has_side_effects=pltpu.SideEffectType.DATAFLOW_SIDE_EFFECTING),
    )(*[pltpu.with_memory_space_constraint(a, pltpu.HBM) for a in arrs],
      *[pltpu.with_memory_space_constraint(z, pltpu.HBM) for z in lands])
    return outs[0], outs[1], list(outs[2:2 + n]), list(outs[2 + n:2 + 2 * n]), outs[-1]


def scatter_wait(send_sems, recv_sems, arrs, lands, after, name):
    n = len(arrs)
    rs = [a.shape[1] // N_DEV for a in arrs]

    def body(*refs):
        x_refs, land_refs = refs[:n], refs[n:2 * n]
        s_sems, r_sems = refs[2 * n], refs[2 * n + 1]
        peers, my_idx = _peers()
        for k, (peer, p_idx) in enumerate(peers):
            for a in range(n):
                copy = pltpu.make_async_remote_copy(
                    src_ref=x_refs[a].at[:, pl.ds(p_idx * rs[a], rs[a]), :], dst_ref=land_refs[a].at[p_idx],
                    send_sem=s_sems.at[a * 7 + k], recv_sem=r_sems.at[a * 7 + k], device_id=peer, device_id_type=MESH)
                copy.wait_send()
                copy.wait_recv()

    hbm = pl.BlockSpec(memory_space=pltpu.HBM)
    sem = pl.BlockSpec(memory_space=pltpu.SEMAPHORE)
    outs = pl.pallas_call(
        body, name=name,
        out_shape=tuple(pltpu.HBM(a.shape, a.dtype) for a in arrs) + tuple(pltpu.HBM(z.shape, z.dtype) for z in lands),
        in_specs=[hbm] * (2 * n) + [sem, sem, pl.BlockSpec(memory_space=pl.ANY)],
        out_specs=(hbm,) * (2 * n),
        input_output_aliases={i: i for i in range(2 * n)},
        compiler_params=pltpu.CompilerParams(has_side_effects=pltpu.SideEffectType.DATAFLOW_SIDE_EFFECTING),
    )(*arrs, *lands, send_sems, recv_sems, after)
    return list(outs[:n]), list(outs[n:])


def _row_tile(rows, cap):
    best = None
    for t in range(16, min(rows, cap) + 1, 16):
        if rows % t == 0:
            best = t
    return rows if best is None else best


def adamw(w, gparts, m, v, name):
    per_layer = isinstance(gparts, (list, tuple))
    glist = list(gparts) if per_layer else [gparts]
    n, _, ra, cb = glist[0].shape
    nl = w.shape[0]
    ng = len(glist)
    ta = _row_tile(ra, max(8, (1 << 19) // (cb * n)))

    def slot_sum(g_ref):
        g = g_ref[0].astype(F32)
        for p in range(1, n):
            g = g + g_ref[p].astype(F32)
        return g

    def body(*refs):
        w_ref, g_refs = refs[0], refs[1:1 + ng]
        m_ref, v_ref, go_ref, d_ref, mo_ref, vo_ref = refs[1 + ng:]
        g = slot_sum(g_refs[0])
        for layer in range(1, ng):
            g = jnp.where(pl.program_id(0) == layer, slot_sum(g_refs[layer]), g)
        mn = ADAM_B1 * m_ref[...] + (1.0 - ADAM_B1) * g
        vn = ADAM_B2 * v_ref[...] + (1.0 - ADAM_B2) * jnp.square(g)
        m_hat = mn / (1.0 - ADAM_B1 ** ADAM_STEP)
        v_hat = vn / (1.0 - ADAM_B2 ** ADAM_STEP)
        go_ref[...] = g
        d_ref[...] = -ADAM_LR * (m_hat / (jnp.sqrt(v_hat) + ADAM_EPS) + ADAM_WD * w_ref[...])
        mo_ref[...] = mn
        vo_ref[...] = vn

    blk = pl.BlockSpec((None, ta, cb), lambda l, i: (l, i, 0))
    if per_layer:
        gblk = pl.BlockSpec((n, None, ta, cb), lambda l, i: (0, 0, i, 0))
    else:
        gblk = pl.BlockSpec((n, None, ta, cb), lambda l, i: (0, l, i, 0))
    shp = jax.ShapeDtypeStruct((nl, ra, cb), F32)
    return pl.pallas_call(
        body, name=name, grid=(nl, ra // ta),
        in_specs=[blk] + [gblk] * ng + [blk, blk], out_specs=[blk] * 4, out_shape=[shp] * 4,
        compiler_params=_cp(("arbitrary", "arbitrary")),
    )(w, *glist, m, v)


def _sincos_2d(rows, cols, dim):
    quarter = dim // 4
    omega = 1.0 / (10000.0 ** (jnp.arange(quarter, dtype=F32) / quarter))
    r = jnp.arange(rows, dtype=F32)[:, None] * omega
    cc = jnp.arange(cols, dtype=F32)[:, None] * omega
    er = jnp.concatenate([jnp.sin(r), jnp.cos(r)], axis=-1)
    ec = jnp.concatenate([jnp.sin(cc), jnp.cos(cc)], axis=-1)
    pe = jnp.concatenate([jnp.broadcast_to(er[:, None, :], (rows, cols, dim // 2)),
                          jnp.broadcast_to(ec[None, :, :], (rows, cols, dim // 2))], axis=-1)
    return pe.reshape(rows * cols, dim)


def _pool_constants():
    nw = len(POOL_WINDOWS)
    band = np.zeros((2, nw, TT, TT), np.float32)
    icnt = np.zeros((2, TT, C_W), np.float32)
    for kind, n in ((0, TT), (1, GRID_W)):
        for i, w in enumerate(POOL_WINDOWS):
            for t in range(TT):
                base, tl = (t // n) * n, t % n
                lo = min(max(tl - w // 2, 0), n)
                hi = min(max(tl - w // 2 + w, 0), n)
                band[kind, i, t, base + lo:base + hi] = 1.0
                icnt[kind, t, i * (C_W // nw):(i + 1) * (C_W // nw)] = 1.0 / (hi - lo)
    return jnp.asarray(band, BF16), jnp.asarray(icnt, F32)


def _block_diag(blocks):
    n, a, _ = blocks.shape
    return jnp.einsum('gab,gh->gahb', blocks, jnp.eye(n, dtype=F32), precision=HI).reshape(n * a, n * a)


def _block_diag_parts(mat, n):
    a = mat.shape[0] // n
    m4 = mat.reshape(n, a, n, a)
    return jnp.stack([m4[g, :, g, :] for g in range(n)])


_SMALL = ("c_ctx", "b_mod", "norm_mix_pre", "norm_mix_post", "norm_ffn_pre", "norm_ffn_post", "sgu_w", "sgu_b",
          "ssm_lam_re", "ssm_lam_im", "ssm_log_dt", "ssm_b_re", "ssm_b_im", "ssm_c_re", "ssm_c_im", "ssm_d",
          "glu_b", "pool_w", "pool_scale")
_WEIGHTS = ("c_ctx", "w_mod", "b_mod", "norm_mix_pre", "norm_mix_post", "norm_ffn_pre", "norm_ffn_post", "w_in", "w_out",
            "sgu_w", "sgu_b", "ssm_lam_re", "ssm_lam_im", "ssm_log_dt", "ssm_b_re", "ssm_b_im", "ssm_c_re", "ssm_c_im",
            "ssm_d", "glu_w", "glu_b", "pool_w", "pool_scale", "ffn_w_gate", "ffn_w_up", "ffn_w_down")


def _pack_rows(a):
    flat = a.reshape(-1)
    rows = -(-flat.shape[0] // D)
    rows8 = -(-rows // 8) * 8
    return jnp.pad(flat, (0, rows8 * D - flat.shape[0])).reshape(rows8, D)


def _pack(tree):
    packed = jnp.concatenate([_pack_rows(tree[k]) for k in _SMALL], axis=0)
    return jnp.pad(packed, ((0, -packed.shape[0] % 64), (0, 0)))


def _unpack(packed, like):
    out, at = {}, 0
    for k in _SMALL:
        size = int(np.prod(like[k].shape))
        rows8 = -(-(-(-size // D)) // 8) * 8
        out[k] = packed[at:at + rows8].reshape(-1)[:size].reshape(like[k].shape)
        at += rows8
    return out


def kernel(x, c, ctx, c_ctx, w_mod, b_mod, norm_mix_pre, norm_mix_post, norm_ffn_pre, norm_ffn_post, w_in, w_out, sgu_w, sgu_b, ssm_lam_re, ssm_lam_im, ssm_log_dt, ssm_b_re, ssm_b_im, ssm_c_re, ssm_c_im, ssm_d, glu_w, glu_b, pool_w, pool_scale, ffn_w_gate, ffn_w_up, ffn_w_down, loss_target, m_c_ctx, m_w_mod, m_b_mod, m_norm_mix_pre, m_norm_mix_post, m_norm_ffn_pre, m_norm_ffn_post, m_w_in, m_w_out, m_sgu_w, m_sgu_b, m_ssm_lam_re, m_ssm_lam_im, m_ssm_log_dt, m_ssm_b_re, m_ssm_b_im, m_ssm_c_re, m_ssm_c_im, m_ssm_d, m_glu_w, m_glu_b, m_pool_w, m_pool_scale, m_ffn_w_gate, m_ffn_w_up, m_ffn_w_down, v_c_ctx, v_w_mod, v_b_mod, v_norm_mix_pre, v_norm_mix_post, v_norm_ffn_pre, v_norm_ffn_post, v_w_in, v_w_out, v_sgu_w, v_sgu_b, v_ssm_lam_re, v_ssm_lam_im, v_ssm_log_dt, v_ssm_b_re, v_ssm_b_im, v_ssm_c_re, v_ssm_c_im, v_ssm_d, v_glu_w, v_glu_b, v_pool_w, v_pool_scale, v_ffn_w_gate, v_ffn_w_up, v_ffn_w_down):
    wts = dict(c_ctx=c_ctx, w_mod=w_mod, b_mod=b_mod, norm_mix_pre=norm_mix_pre, norm_mix_post=norm_mix_post,
               norm_ffn_pre=norm_ffn_pre, norm_ffn_post=norm_ffn_post, w_in=w_in, w_out=w_out, sgu_w=sgu_w, sgu_b=sgu_b,
               ssm_lam_re=ssm_lam_re, ssm_lam_im=ssm_lam_im, ssm_log_dt=ssm_log_dt, ssm_b_re=ssm_b_re, ssm_b_im=ssm_b_im,
               ssm_c_re=ssm_c_re, ssm_c_im=ssm_c_im, ssm_d=ssm_d, glu_w=glu_w, glu_b=glu_b, pool_w=pool_w,
               pool_scale=pool_scale, ffn_w_gate=ffn_w_gate, ffn_w_up=ffn_w_up, ffn_w_down=ffn_w_down)
    mom_m = dict(c_ctx=m_c_ctx, w_mod=m_w_mod, b_mod=m_b_mod, norm_mix_pre=m_norm_mix_pre, norm_mix_post=m_norm_mix_post,
                 norm_ffn_pre=m_norm_ffn_pre, norm_ffn_post=m_norm_ffn_post, w_in=m_w_in, w_out=m_w_out, sgu_w=m_sgu_w,
                 sgu_b=m_sgu_b, ssm_lam_re=m_ssm_lam_re, ssm_lam_im=m_ssm_lam_im, ssm_log_dt=m_ssm_log_dt,
                 ssm_b_re=m_ssm_b_re, ssm_b_im=m_ssm_b_im, ssm_c_re=m_ssm_c_re, ssm_c_im=m_ssm_c_im, ssm_d=m_ssm_d,
                 glu_w=m_glu_w, glu_b=m_glu_b, pool_w=m_pool_w, pool_scale=m_pool_scale, ffn_w_gate=m_ffn_w_gate,
                 ffn_w_up=m_ffn_w_up, ffn_w_down=m_ffn_w_down)
    mom_v = dict(c_ctx=v_c_ctx, w_mod=v_w_mod, b_mod=v_b_mod, norm_mix_pre=v_norm_mix_pre, norm_mix_post=v_norm_mix_post,
                 norm_ffn_pre=v_norm_ffn_pre, norm_ffn_post=v_norm_ffn_post, w_in=v_w_in, w_out=v_w_out, sgu_w=v_sgu_w,
                 sgu_b=v_sgu_b, ssm_lam_re=v_ssm_lam_re, ssm_lam_im=v_ssm_lam_im, ssm_log_dt=v_ssm_log_dt,
                 ssm_b_re=v_ssm_b_re, ssm_b_im=v_ssm_b_im, ssm_c_re=v_ssm_c_re, ssm_c_im=v_ssm_c_im, ssm_d=v_ssm_d,
                 glu_w=v_glu_w, glu_b=v_glu_b, pool_w=v_pool_w, pool_scale=v_pool_scale, ffn_w_gate=v_ffn_w_gate,
                 ffn_w_up=v_ffn_w_up, ffn_w_down=v_ffn_w_down)

    bl, seq, _ = x.shape
    n_ctx = ctx.shape[1]
    assert n_ctx == TT and seq % TT == 0 and seq % GRID_W == 0
    depth = w_in.shape[0]
    nc = n_ctx // TT
    ncr = n_ctx // TC
    s_all = n_ctx + seq
    nt = s_all // TT
    t_all = bl * s_all
    n_batch = bl * N_DEV
    my_idx = 4 * lax.axis_index("x") + 2 * lax.axis_index("y") + lax.axis_index("c")
    wc = w_mod.shape[2]

    c_rows = jnp.pad(c, ((0, 8 - bl), (0, 0))) if bl < 8 else c
    rc = c_rows.shape[0]
    (c_all,) = all_gather_rows([c_rows[None]], "gather_c")
    c_all = c_all[0].reshape(N_DEV, rc, D)[:, :bl].reshape(n_batch, D)
    r_act = -(-(n_batch + 1) // 16) * 16
    pre_act = jnp.concatenate([c_all, c_ctx[None, :], jnp.zeros((r_act - n_batch - 1, D), F32)], axis=0)
    act = jax.nn.silu(pre_act)
    b_cols = lax.dynamic_slice_in_dim(b_mod, my_idx * wc, wc, axis=1)[:, None, :]
    mod_cols = mod_forward(act, w_mod, b_cols)
    (mod_all,) = all_gather_rows([mod_cols], "gather_mod")
    mod_all = mod_all.reshape(depth, N_DEV, r_act, wc).transpose(0, 2, 1, 3).reshape(depth, r_act, 6, D)
    mod_lat = lax.dynamic_slice_in_dim(mod_all, my_idx * bl, bl, axis=1)
    mod_ctx = jnp.broadcast_to(mod_all[:, n_batch:n_batch + 1], (depth, bl, 6, D))
    mods = jnp.pad(jnp.stack([mod_ctx, mod_lat], axis=2), ((0, 0), (0, 0), (0, 0), (0, 2), (0, 0)))

    tr = lambda a: jnp.swapaxes(a, 1, 2).astype(BF16)
    shards = dict(w_in=tr(w_in), w_out=w_out.astype(BF16), glu_w=glu_w.astype(BF16), gate=tr(ffn_w_gate),
                  up=tr(ffn_w_up), down=ffn_w_down.astype(BF16))
    mix_keys, ffn_keys = ("w_in", "w_out", "glu_w"), ("gate", "up", "down")
    layer = lambda k, i: shards[k][i:i + 1]
    full = [dict() for _ in range(depth)]
    for k, g in zip(mix_keys, all_gather_rows([layer(k, 0) for k in mix_keys], "gather_mix_weights_0")):
        full[0][k] = g[0]
    first_done = mods[0, 0, 0, 0:1, 0:128] + full[0]["w_in"][0:1, 0:128].astype(F32)
    weights_in_flight = {0: (ffn_keys, gather_start([layer(k, 0) for k in ffn_keys], first_done, "gather_start_ffn_0"))}
    for i in range(1, depth):
        prev_token = weights_in_flight[i - 1][1][4]
        weights_in_flight[i] = (mix_keys + ffn_keys, gather_start([layer(k, i) for k in mix_keys + ffn_keys], prev_token,
                                                                  f"gather_start_layer_{i}"))
    start_token = sum(fl[1][4][0:1, 0:1] for fl in weights_in_flight.values())

    def land_weights(i, after, name):
        keys, (send_sems, recv_sems, sent, lands, _) = weights_in_flight[i]
        for k, g in zip(keys, gather_wait(send_sems, recv_sems, sent, lands, after, name)):
            full[i][k] = g[0]

    band, icnt = _pool_constants()
    seg_p = jnp.asarray(np.kron(np.eye(A_HEADS), np.full((A_W // A_HEADS,) * 2, A_HEADS / A_W)), BF16)
    pe = _sincos_2d(seq // GRID_W, GRID_W, D)
    xs = embed_tokens(x, ctx, pe)

    saved = []
    for i in range(depth):
        mats, ssm_vjp = jax.vjp(ssm_build, ssm_lam_re[i], ssm_lam_im[i], ssm_log_dt[i], ssm_b_re[i], ssm_b_im[i],
                                ssm_c_re[i], ssm_c_im[i], ssm_d[i])
        if i > 0:
            land_weights(i, xs, f"gather_wait_layer_{i}")
        cst = dict(sw=sgu_w[i].astype(BF16),
                   sbias=jnp.repeat(sgu_b[i].T, A_W // A_HEADS, axis=1),
                   seg_p=seg_p, band=band, icnt=icnt, wbd=_block_diag(pool_w[i]).astype(BF16),
                   pscale=pool_scale[i][None, :], glu_w=full[i]["glu_w"], glu_b=glu_b[i][None, :], w_out=full[i]["w_out"],
                   n2=norm_mix_post[i][None, :])
        n1, n3, n4 = norm_mix_pre[i][None, :], norm_ffn_pre[i][None, :], norm_ffn_post[i][None, :]
        if i == 0:
            n1 = n1 + start_token
        za, zu, zp = pre_mix(xs, mods[i], n1, full[i]["w_in"], nc)
        ops = ssm_operators(mats, f"_{i}")
        ys, hps = ssm_forward(zu, mats, ops, ncr)
        x1, m_pre = post_mix(xs, za, zp, ys, mods[i], cst, nc)
        if i == 0:
            land_weights(0, x1, "gather_wait_ffn_0")
        x2, f_pre, gate_b, up_b = ffn_fwd(x1, mods[i], n3, n4, full[i]["gate"], full[i]["up"], full[i]["down"], n_ctx)
        saved.append(dict(xs=xs, za=za, zu=zu, zp=zp, ys=ys, hps=hps, x1=x1, m=m_pre, f=f_pre, gate=gate_b, up=up_b,
                          cst=cst, mats=mats,
                          ops=ops, ssm_vjp=ssm_vjp, n1=n1, n3=n3, n4=n4))
        xs = x2

    dx, loss_parts = loss_head(xs, loss_target, nc)
    loss = lax.psum(jnp.sum(loss_parts[:, :, 0, 0]), ("x", "y", "c"))

    grads = {k: [None] * depth for k in _WEIGHTS}
    big = {k: [None] * depth for k in ("w_in", "w_out", "glu_w", "ffn_w_gate", "ffn_w_up", "ffn_w_down")}
    dmods = [None] * depth
    scatter_groups = (("ffn_w_gate", "ffn_w_up", "ffn_w_down"), ("w_out", "glu_w"), ("w_in",))
    in_flight = []

    def send_grads(i, group):
        flight = scatter_start([big[k][i][None] for k in scatter_groups[group]], f"scatter_start_{i}_{group}")
        in_flight.append((i, group, flight))
        return flight[4][0:1, 0:1]

    flat = lambda a: a.reshape(t_all, a.shape[-1])
    for i in reversed(range(depth)):
        sv = saved[i]
        dx1, h2, df, act_b, dgate, dup, st_f = ffn_bwd(dx, sv["x1"], sv["f"], sv["gate"], sv["up"], mods[i], sv["n3"],
                                                       sv["n4"], full[i]["gate"], full[i]["up"], full[i]["down"], n_ctx)
        big["ffn_w_gate"][i] = tn_matmul(flat(dgate), flat(h2), f"grad_ffn_gate_{i}")
        big["ffn_w_up"][i] = tn_matmul(flat(dup), flat(h2), f"grad_ffn_up_{i}")
        big["ffn_w_down"][i] = tn_matmul(flat(act_b), flat(df), f"grad_ffn_down_{i}")
        cst_i = dict(sv["cst"], n2=sv["cst"]["n2"] + send_grads(i, 0))
        dza, dzp, dys, cat, dm, gg, dr, st_m, dsw, dsb, dwbd = post_mix_bwd(dx1, sv["m"], sv["za"], sv["zp"], sv["ys"],
                                                                            mods[i], cst_i, nc)
        big["w_out"][i] = tn_matmul(flat(cat), flat(dm), f"grad_w_out_{i}")
        big["glu_w"][i] = tn_matmul(flat(gg), flat(dr), f"grad_glu_w_{i}")
        mats_i = dict(sv["mats"], lam_f=sv["mats"]["lam_f"] + send_grads(i, 1))
        dzu, cot = ssm_backward(dys, sv["zu"], sv["hps"], mats_i, sv["ops"], ncr)
        (grads["ssm_lam_re"][i], grads["ssm_lam_im"][i], grads["ssm_log_dt"][i], grads["ssm_b_re"][i],
         grads["ssm_b_im"][i], grads["ssm_c_re"][i], grads["ssm_c_im"][i], grads["ssm_d"][i]) = sv["ssm_vjp"](cot)
        dx, h1, dz, st_p = pre_mix_bwd(dza, dzu, dzp, sv["xs"], dx1, mods[i], sv["n1"], full[i]["w_in"], nc)
        big["w_in"][i] = tn_matmul(flat(dz), flat(h1), f"grad_w_in_{i}")

        tiles = lambda st, row: st[:, :, row, :]
        allsum = lambda st, row: jnp.sum(tiles(st, row), axis=(0, 1))
        grads["norm_mix_pre"][i] = allsum(st_p, 2)
        grads["norm_mix_post"][i] = allsum(st_m, 1)
        grads["norm_ffn_pre"][i] = allsum(st_f, 3)
        grads["norm_ffn_post"][i] = allsum(st_f, 4)
        misc = allsum(st_m, 2)
        grads["glu_b"][i] = misc[:B_W]
        grads["pool_scale"][i] = misc[B_W:B_W + C_W]
        grads["sgu_w"][i] = dsw
        grads["sgu_b"][i] = jnp.sum(dsb.reshape(CHUNK, A_HEADS, A_W // A_HEADS), axis=2).T
        grads["pool_w"][i] = _block_diag_parts(dwbd, len(POOL_WINDOWS))
        mix = (tiles(st_p, 0), tiles(st_p, 1), tiles(st_m, 0))
        d_lat = jnp.stack([jnp.sum(t[:, nc:], axis=1) for t in mix]
                          + [jnp.sum(tiles(st_f, r), axis=1) for r in (0, 1, 2)], axis=1).reshape(bl, 6 * D)
        d_ctx = jnp.concatenate([jnp.sum(t[:, :nc], axis=(0, 1)) for t in mix]
                                + [allsum(st_f, r) for r in (5, 6, 7)]).reshape(1, 6 * D)
        dmods[i] = jnp.concatenate([d_lat, d_ctx, jnp.zeros((8 - (bl + 1) % 8 if (bl + 1) % 8 else 0, 6 * D), F32)],
                                   axis=0)
        token = send_grads(i, 2)
        if i > 0:
            saved[i - 1]["n3"] = saved[i - 1]["n3"] + token
        else:
            dmods[i] = dmods[i] + token
    grad_x = dx[:, n_ctx:, :]

    dmod_local = jnp.stack(dmods)
    rd = dmod_local.shape[1]
    (dmod_all,) = all_gather_rows([dmod_local], "gather_dmod")
    dmod_cols = lax.dynamic_slice_in_dim(dmod_all, my_idx * wc, wc, axis=2).reshape(depth, N_DEV, rd, wc)
    d_lat_all = dmod_cols[:, :, :bl].reshape(depth, n_batch, wc)
    d_ctx_all = dmod_cols[:, 0, bl]
    for p in range(1, N_DEV):
        d_ctx_all = d_ctx_all + dmod_cols[:, p, bl]
    dmod_rows = jnp.concatenate([d_lat_all, d_ctx_all[:, None, :], jnp.zeros((depth, r_act - n_batch - 1, wc), F32)],
                                axis=1)
    dctx_rows = jnp.pad(d_ctx_all[:, None, :], ((0, 0), (0, 7), (0, 0)))
    g_w_mod, dact_ctx = mod_backward(act, dmod_rows, dctx_rows, w_mod)
    sig_c = jax.nn.sigmoid(c_ctx)
    dsilu_c = sig_c * (1.0 + c_ctx * (1.0 - sig_c))
    small_g = {k: (jnp.stack(grads[k]) if grads[k][0] is not None else None) for k in _SMALL}
    small_g["c_ctx"] = jnp.sum(dact_ctx[:, 0, :], axis=0) * dsilu_c
    small_g["b_mod"] = jnp.stack([jnp.sum(dmods[i][:bl + 1], axis=0) for i in range(depth)])

    packed_g = _pack(small_g).astype(BF16)
    rows_s = packed_g.shape[0]
    (gathered,) = all_gather_rows([packed_g[None]], "gather_small_grads")
    res = {k: [None] * 4 for k in _WEIGHTS}

    landed = {}
    for i, group, (send_sems, recv_sems, arrs_thru, lands_thru, _) in in_flight:
        sent, lands = scatter_wait(send_sems, recv_sems, arrs_thru, lands_thru, gathered, f"scatter_wait_{i}_{group}")
        for k, a, z in zip(scatter_groups[group], sent, lands):
            r = a.shape[1] // N_DEV
            own = lax.dynamic_slice_in_dim(a, my_idx * r, r, axis=1)[None]
            landed[k, i] = lax.dynamic_update_slice_in_dim(z, own, my_idx, axis=0)
    for k in big:
        transposed = k in ("w_in", "ffn_w_gate", "ffn_w_up")
        view = (lambda a: jnp.swapaxes(a, 1, 2)) if transposed else (lambda a: a)
        o4 = adamw(view(wts[k]), [landed[k, i] for i in range(depth)], view(mom_m[k]), view(mom_v[k]), "adamw_" + k)
        res[k] = [view(o) for o in o4]
    res["w_mod"] = list(adamw(w_mod, g_w_mod[None], m_w_mod, v_w_mod, "adamw_w_mod"))

    small_w = {k: wts[k] for k in _SMALL}
    outs = adamw(_pack(small_w)[None], gathered.reshape(N_DEV, 1, rows_s, D), _pack({k: mom_m[k] for k in _SMALL})[None],
                 _pack({k: mom_v[k] for k in _SMALL})[None], "adamw_replicated")
    for slot, packed in enumerate(outs):
        un = _unpack(packed[0], small_w)
        for k in _SMALL:
            res[k][slot] = un[k]

    return (loss, grad_x, *[res[k][0] for k in _WEIGHTS], *[res[k][1] for k in _WEIGHTS],
            *[res[k][2] for k in _WEIGHTS], *[res[k][3] for k in _WEIGHTS])
```

```python
import functools
import math

import numpy as np
import jax
import jax.numpy as jnp
from jax import lax
from jax.experimental import pallas as pl
from jax.experimental.pallas import tpu as pltpu

F32 = jnp.float32
BF16 = jnp.bfloat16
HI = lax.Precision.HIGHEST
MESH = pl.DeviceIdType.MESH

D = 1024
D_IN = 1280
D_FF = 2816
A_W = 256
B_W = 512
C_W = 256
A_HEADS = 4
CHUNK = 128
SSM_G = 32
SSM_H = 16
SSM_P = 64
GRID_W = 64
POOL_WINDOWS = (2, 4, 8, 16)
EPS = 1e-6
N_DEV = 8

TT = 256
TC = 8
ROW_W = TC * B_W
QW = ROW_W // 4
GQ = 8
FF_CHUNK = 256
VMEM_LIMIT = 60 * 1024 * 1024

ADAM_LR = 0.001
ADAM_B1 = 0.9
ADAM_B2 = 0.999
ADAM_EPS = 1e-08
ADAM_WD = 0.01
ADAM_STEP = 10


def _cp(sem):
    return pltpu.CompilerParams(dimension_semantics=sem, vmem_limit_bytes=VMEM_LIMIT)


def dot_nn(a, b):
    return jnp.dot(a, b, preferred_element_type=F32)


def dot_nt(a, b):
    return lax.dot_general(a, b, (((1,), (1,)), ((), ())), preferred_element_type=F32)


def dot_tn(a, b):
    return lax.dot_general(a, b, (((0,), (0,)), ((), ())), preferred_element_type=F32)


def split_bf16(x):
    hi = x.astype(BF16)
    lo = (x - hi.astype(F32)).astype(BF16)
    return hi, lo


def gelu(x):
    return jax.nn.gelu(x)


def gelu_grad(x):
    c = math.sqrt(2.0 / math.pi)
    t = jnp.tanh(c * (x + 0.044715 * x * x * x))
    return 0.5 * (1.0 + t) + 0.5 * x * (1.0 - t * t) * c * (1.0 + 3.0 * 0.044715 * x * x)


def rms_stats(x):
    r = lax.rsqrt(jnp.mean(x * x, axis=-1, keepdims=True) + EPS)
    return r, x * r


def rms_bwd(r, xn, dxn):
    return r * (dxn - xn * jnp.mean(dxn * xn, axis=-1, keepdims=True))


def colsum(x):
    return jnp.sum(x, axis=0, keepdims=True)


def lane_group(width, group):
    return lax.broadcasted_iota(jnp.int32, (1, width), 1) // group


def _tile_spec(width):
    return pl.BlockSpec((None, TT, width), lambda b, j: (b, j, 0))


def _mod_spec(nc):
    return pl.BlockSpec((None, None, 8, D), lambda b, j: (b, jnp.where(j >= nc, 1, 0), 0, 0))


def _full_spec(shape):
    zeros = (0,) * len(shape)
    return pl.BlockSpec(shape, lambda b, j: zeros)


def _kind_spec(shape, nc):
    zeros = (0,) * len(shape)
    return pl.BlockSpec((None,) + shape, lambda b, j: (jnp.where(j >= nc, 1, 0),) + zeros)


def _stat_spec():
    return pl.BlockSpec((None, None, 8, D), lambda b, j: (b, j, 0, 0))


def _chunk_spec():
    return pl.BlockSpec((None, TT // TC, ROW_W), lambda b, j: (b, j, 0))


def _rows_to_chunks(val, scratch, out_ref):
    for cb in range(B_W // 128):
        scratch[cb] = val[:, cb * 128:(cb + 1) * 128]
    for s in range(TC):
        for cb in range(B_W // 128):
            lo = cb * QW + s * 128
            out_ref[:, lo:lo + 128] = scratch.at[cb][pl.ds(s, TT // TC, stride=TC), :]


def _chunks_to_rows(in_ref, scratch):
    for s in range(TC):
        for cb in range(B_W // 128):
            lo = cb * QW + s * 128
            scratch.at[cb][pl.ds(s, TT // TC, stride=TC), :] = in_ref[:, lo:lo + 128]
    return jnp.concatenate([scratch[cb] for cb in range(B_W // 128)], axis=1)


def _chunk_scratch():
    return pltpu.VMEM((B_W // 128, TT, 128), F32)


def embed_tokens(x, ctx, pe):
    bl, seq, _ = x.shape
    nc = ctx.shape[1] // TT
    nt = nc + seq // TT

    def body(ctx_ref, x_ref, pe_ref, o_ref):
        j = pl.program_id(1)

        @pl.when(j < nc)
        def _():
            o_ref[...] = ctx_ref[...]

        @pl.when(j >= nc)
        def _():
            o_ref[...] = x_ref[...] + pe_ref[...]

    return pl.pallas_call(
        body, name="embed_tokens", grid=(bl, nt),
        in_specs=[pl.BlockSpec((None, TT, D), lambda b, j: (b, jnp.minimum(j, nc - 1), 0)),
                  pl.BlockSpec((None, TT, D), lambda b, j: (b, jnp.maximum(j - nc, 0), 0)),
                  pl.BlockSpec((TT, D), lambda b, j: (jnp.maximum(j - nc, 0), 0))],
        out_specs=_tile_spec(D),
        out_shape=jax.ShapeDtypeStruct((bl, nt * TT, D), F32),
        compiler_params=_cp(("arbitrary", "arbitrary")),
    )(ctx, x, pe)


def pre_mix(xs, mod, n1, w_int, nc):
    bl, s, _ = xs.shape

    def body(x_ref, mod_ref, n_ref, w_ref, za_ref, zu_ref, zp_ref, u_s):
        r, xn = rms_stats(x_ref[...])
        h = xn * n_ref[...] * (1.0 + mod_ref[1:2, :]) + mod_ref[0:1, :]
        z = dot_nt(h.astype(BF16), w_ref[...])
        za_ref[...] = z[:, :2 * A_W]
        _rows_to_chunks(z[:, 2 * A_W:2 * A_W + B_W], u_s, zu_ref)
        zp_ref[...] = z[:, 2 * A_W + B_W:]

    return pl.pallas_call(
        body, name="pre_mix", grid=(bl, s // TT),
        in_specs=[_tile_spec(D), _mod_spec(nc), _full_spec((1, D)), _full_spec((D_IN, D))],
        out_specs=[_tile_spec(2 * A_W), _chunk_spec(), _tile_spec(C_W)],
        out_shape=[jax.ShapeDtypeStruct((bl, s, 2 * A_W), F32), jax.ShapeDtypeStruct((bl, s // TC, ROW_W), F32),
                   jax.ShapeDtypeStruct((bl, s, C_W), F32)],
        scratch_shapes=[_chunk_scratch()],
        compiler_params=_cp(("arbitrary", "arbitrary")),
    )(xs, mod, n1, w_int)


def _seg_mean(x, seg_p):
    hi, lo = split_bf16(x)
    return dot_nn(hi, seg_p) + dot_nn(lo, seg_p)


def _sgu_forward(za, sw_ref, sbias, seg_p):
    ge = gelu(za)
    u, v = ge[:, :A_W], ge[:, A_W:]
    dv = v - _seg_mean(v, seg_p)
    rs = lax.rsqrt(_seg_mean(dv * dv, seg_p) + EPS)
    vn = dv * rs
    head = lane_group(A_W, A_W // A_HEADS)
    parts = []
    for c2 in range(TT // CHUNK):
        vb = vn[c2 * CHUNK:(c2 + 1) * CHUNK].astype(BF16)
        sc = sbias
        for h in range(A_HEADS):
            sc = sc + jnp.where(head == h, dot_nn(sw_ref[h], vb), 0.0)
        parts.append(sc)
    sg = jnp.concatenate(parts, axis=0)
    return u * sg, (u, vn, rs, sg)


def _pool_forward(zp, band_ref, icnt, wbd, pscale):
    hi, lo = split_bf16(zp)
    grp = lane_group(C_W, C_W // len(POOL_WINDOWS))
    q = jnp.zeros_like(zp)
    for i in range(len(POOL_WINDOWS)):
        t = dot_nn(band_ref[i], hi) + dot_nn(band_ref[i], lo)
        q = jnp.where(grp == i, t, q)
    q = q * icnt - zp
    o = dot_nn(q.astype(BF16), wbd)
    return o * pscale, (q, o)


def _glu_forward(y, glu_w, glu_b):
    g = gelu(y)
    sg = jax.nn.sigmoid(dot_nn(g.astype(BF16), glu_w) + glu_b)
    return g * sg, (g, sg)


_MIX_CONST_SHAPES = dict(sw=(A_HEADS, CHUNK, CHUNK), sbias=(CHUNK, A_W), seg_p=(A_W, A_W), wbd=(C_W, C_W),
                         pscale=(1, C_W), glu_w=(B_W, B_W), glu_b=(1, B_W), w_out=(D, D), n2=(1, D))


def _mix_const_specs(nc):
    return ([_full_spec(_MIX_CONST_SHAPES[k]) for k in ("sw", "sbias", "seg_p")]
            + [_kind_spec((len(POOL_WINDOWS), TT, TT), nc), _kind_spec((TT, C_W), nc)]
            + [_full_spec(_MIX_CONST_SHAPES[k]) for k in ("wbd", "pscale", "glu_w", "glu_b", "w_out", "n2")])


def _mix_const_args(cst):
    return [cst[k] for k in ("sw", "sbias", "seg_p", "band", "icnt", "wbd", "pscale", "glu_w", "glu_b", "w_out", "n2")]


def post_mix(xs, za, zp, ys, mod, cst, nc):
    bl, s, _ = xs.shape

    def body(x_ref, za_ref, zp_ref, y_ref, mod_ref, sw_ref, sbias_ref, seg_ref, band_ref, icnt_ref, wbd_ref,
             ps_ref, gw_ref, gb_ref, wo_ref, n2_ref, x1_ref, m_ref, y_s):
        a, _ = _sgu_forward(za_ref[...], sw_ref, sbias_ref[...], seg_ref[...])
        p, _ = _pool_forward(zp_ref[...], band_ref, icnt_ref[...], wbd_ref[...], ps_ref[...])
        sm, _ = _glu_forward(_chunks_to_rows(y_ref, y_s), gw_ref[...], gb_ref[...])
        cat = jnp.concatenate([a, sm, p], axis=1).astype(BF16)
        m = dot_nn(cat, wo_ref[...])
        _, mn = rms_stats(m)
        m_ref[...] = m
        x1_ref[...] = x_ref[...] + mod_ref[2:3, :] * (mn * n2_ref[...])

    return pl.pallas_call(
        body, name="post_mix", grid=(bl, s // TT),
        in_specs=[_tile_spec(D), _tile_spec(2 * A_W), _tile_spec(C_W), _chunk_spec(), _mod_spec(nc)]
        + _mix_const_specs(nc),
        out_specs=[_tile_spec(D), _tile_spec(D)],
        out_shape=[jax.ShapeDtypeStruct((bl, s, D), F32), jax.ShapeDtypeStruct((bl, s, D), F32)],
        scratch_shapes=[_chunk_scratch()],
        compiler_params=_cp(("arbitrary", "arbitrary")),
    )(xs, za, zp, ys, mod, *_mix_const_args(cst))


def post_mix_bwd(dx1, m, za, zp, ys, mod, cst, nc):
    bl, s, _ = m.shape
    nt = s // TT

    def body(dx_ref, m_ref, za_ref, zp_ref, y_ref, mod_ref, sw_ref, sbias_ref, seg_ref, band_ref, icnt_ref,
             wbd_ref, ps_ref, gw_ref, gb_ref, wo_ref, n2_ref,
             dza_ref, dzp_ref, dy_ref, cat_ref, dm_ref, gg_ref, dr_ref, st_ref, dsw_ref, dsb_ref, dwbd_ref, y_s):
        first = jnp.logical_and(pl.program_id(0) == 0, pl.program_id(1) == 0)

        @pl.when(first)
        def _():
            dsw_ref[...] = jnp.zeros_like(dsw_ref)
            dsb_ref[...] = jnp.zeros_like(dsb_ref)
            dwbd_ref[...] = jnp.zeros_like(dwbd_ref)

        seg_p = seg_ref[...]
        za = za_ref[...]
        zp_v = zp_ref[...]
        yv = _chunks_to_rows(y_ref, y_s)
        a, (u, vn, rs, sg) = _sgu_forward(za, sw_ref, sbias_ref[...], seg_p)
        p, (q, o) = _pool_forward(zp_v, band_ref, icnt_ref[...], wbd_ref[...], ps_ref[...])
        sm, (g, sig) = _glu_forward(yv, gw_ref[...], gb_ref[...])
        cat_ref[...] = jnp.concatenate([a, sm, p], axis=1).astype(BF16)

        dx = dx_ref[...]
        g1 = mod_ref[2:3, :]
        n2 = n2_ref[...]
        mv = m_ref[...]
        rm, mn = rms_stats(mv)
        st_ref[...] = jnp.zeros_like(st_ref)
        st_ref[0:1, :] = colsum(dx * (mn * n2))
        st_ref[1:2, :] = colsum(dx * g1 * mn)
        dm = rms_bwd(rm, mn, dx * g1 * n2)
        dmb = dm.astype(BF16)
        dm_ref[...] = dmb
        dcat = dot_nt(dmb, wo_ref[...])
        da, dsm, dp = dcat[:, :A_W], dcat[:, A_W:A_W + B_W], dcat[:, A_W + B_W:]

        du = da * sg
        dsv = da * u
        head = lane_group(A_W, A_W // A_HEADS)
        dvn_parts = []
        dsb_acc = jnp.zeros((CHUNK, A_W), F32)
        for c2 in range(TT // CHUNK):
            dsc = dsv[c2 * CHUNK:(c2 + 1) * CHUNK]
            dsc_b = dsc.astype(BF16)
            vb = vn[c2 * CHUNK:(c2 + 1) * CHUNK].astype(BF16)
            dsb_acc = dsb_acc + dsc
            dvn_c = jnp.zeros((CHUNK, A_W), F32)
            for h in range(A_HEADS):
                dsw_ref[h] += dot_nt(jnp.where(head == h, dsc, 0.0).astype(BF16), vb)
                dvn_c = dvn_c + jnp.where(head == h, dot_tn(sw_ref[h], dsc_b), 0.0)
            dvn_parts.append(dvn_c)
        dsb_ref[...] += dsb_acc
        dvn = jnp.concatenate(dvn_parts, axis=0)
        dv = rs * (dvn - _seg_mean(dvn, seg_p) - vn * _seg_mean(dvn * vn, seg_p))
        dza_ref[...] = jnp.concatenate([du, dv], axis=1) * gelu_grad(za)

        ps = ps_ref[...]
        do = dp * ps
        dps = colsum(dp * o)
        dob = do.astype(BF16)
        dwbd_ref[...] += dot_tn(q.astype(BF16), dob)
        dq = dot_nt(dob, wbd_ref[...])
        hi, lo = split_bf16(dq * icnt_ref[...])
        grp = lane_group(C_W, C_W // len(POOL_WINDOWS))
        dzp = -dq
        for i in range(len(POOL_WINDOWS)):
            t = dot_tn(band_ref[i], hi) + dot_tn(band_ref[i], lo)
            dzp = dzp + jnp.where(grp == i, t, 0.0)
        dzp_ref[...] = dzp

        dr = dsm * g * sig * (1.0 - sig)
        drb = dr.astype(BF16)
        dr_ref[...] = drb
        gg_ref[...] = g.astype(BF16)
        dg = dsm * sig + dot_nt(drb, gw_ref[...])
        _rows_to_chunks(dg * gelu_grad(yv), y_s, dy_ref)
        st_ref[2:3, :] = jnp.concatenate([colsum(dr), dps, jnp.zeros((1, D - B_W - C_W), F32)], axis=1)

    acc = lambda shape: pl.BlockSpec(shape, lambda b, j: (0,) * len(shape))
    return pl.pallas_call(
        body, name="post_mix_bwd", grid=(bl, nt),
        in_specs=[_tile_spec(D), _tile_spec(D), _tile_spec(2 * A_W), _tile_spec(C_W), _chunk_spec(), _mod_spec(nc)]
        + _mix_const_specs(nc),
        out_specs=[_tile_spec(2 * A_W), _tile_spec(C_W), _chunk_spec(), _tile_spec(D), _tile_spec(D),
                   _tile_spec(B_W), _tile_spec(B_W), _stat_spec(),
                   acc((A_HEADS, CHUNK, CHUNK)), acc((CHUNK, A_W)), acc((C_W, C_W))],
        out_shape=[jax.ShapeDtypeStruct((bl, s, 2 * A_W), F32), jax.ShapeDtypeStruct((bl, s, C_W), F32),
                   jax.ShapeDtypeStruct((bl, s // TC, ROW_W), F32), jax.ShapeDtypeStruct((bl, s, D), BF16),
                   jax.ShapeDtypeStruct((bl, s, D), BF16), jax.ShapeDtypeStruct((bl, s, B_W), BF16),
                   jax.ShapeDtypeStruct((bl, s, B_W), BF16), jax.ShapeDtypeStruct((bl, nt, 8, D), F32),
                   jax.ShapeDtypeStruct((A_HEADS, CHUNK, CHUNK), F32), jax.ShapeDtypeStruct((CHUNK, A_W), F32),
                   jax.ShapeDtypeStruct((C_W, C_W), F32)],
        scratch_shapes=[_chunk_scratch()],
        compiler_params=_cp(("arbitrary", "arbitrary")),
    )(dx1, m, za, zp, ys, mod, *_mix_const_args(cst))


def pre_mix_bwd(dza, dzu, dzp, xs, dxres, mod, n1, w_int, nc):
    bl, s, _ = xs.shape
    nt = s // TT

    def body(dza_ref, dzu_ref, dzp_ref, x_ref, dres_ref, mod_ref, n_ref, w_ref, dx_ref, h_ref, dz_ref, st_ref, u_s):
        dz = jnp.concatenate([dza_ref[...], _chunks_to_rows(dzu_ref, u_s), dzp_ref[...]], axis=1).astype(BF16)
        dz_ref[...] = dz
        dh = dot_nn(dz, w_ref[...])
        r, xn = rms_stats(x_ref[...])
        n1v = n_ref[...]
        sc = mod_ref[1:2, :]
        xg = xn * n1v
        h_ref[...] = (xg * (1.0 + sc) + mod_ref[0:1, :]).astype(BF16)
        dyv = dh * (1.0 + sc)
        st_ref[...] = jnp.zeros_like(st_ref)
        st_ref[0:1, :] = colsum(dh)
        st_ref[1:2, :] = colsum(dh * xg)
        st_ref[2:3, :] = colsum(dyv * xn)
        dx_ref[...] = dres_ref[...] + rms_bwd(r, xn, dyv * n1v)

    return pl.pallas_call(
        body, name="pre_mix_bwd", grid=(bl, nt),
        in_specs=[_tile_spec(2 * A_W), _chunk_spec(), _tile_spec(C_W), _tile_spec(D), _tile_spec(D), _mod_spec(nc),
                  _full_spec((1, D)), _full_spec((D_IN, D))],
        out_specs=[_tile_spec(D), _tile_spec(D), _tile_spec(D_IN), _stat_spec()],
        out_shape=[jax.ShapeDtypeStruct((bl, s, D), F32), jax.ShapeDtypeStruct((bl, s, D), BF16),
                   jax.ShapeDtypeStruct((bl, s, D_IN), BF16), jax.ShapeDtypeStruct((bl, nt, 8, D), F32)],
        scratch_shapes=[_chunk_scratch()],
        compiler_params=_cp(("arbitrary", "arbitrary")),
    )(dza, dzu, dzp, xs, dxres, mod, n1, w_int)


def _ffn_tile(s):
    return 768 if s % 768 == 0 else TT


def _slabs(v, n_ctx):
    return (v,) if v.shape[0] == n_ctx else (v[:n_ctx], v[n_ctx:])


def _mod_rows(mod_ref, j, row):
    lat = mod_ref[1, row:row + 1, :]
    return jnp.where(j == 0, mod_ref[0, row:row + 1, :], lat), lat


def _by_slab(fn, n_ctx, *vals_and_rows):
    outs = []
    for s in range(len(_slabs(next(v for v in vals_and_rows if not isinstance(v, tuple)), n_ctx))):
        outs.append(fn(*[v[s] if isinstance(v, tuple) else _slabs(v, n_ctx)[s] for v in vals_and_rows]))
    return outs[0] if len(outs) == 1 else jnp.concatenate(outs, axis=0)


def _split_sum(v, n_ctx, j, st_ref, row):
    parts = [colsum(p) for p in _slabs(v, n_ctx)]
    first_is_ctx = j == 0
    rest = parts[1] if len(parts) > 1 else jnp.zeros_like(parts[0])
    st_ref[row:row + 1, :] = rest + jnp.where(first_is_ctx, 0.0, parts[0])
    st_ref[row + 5:row + 6, :] = jnp.where(first_is_ctx, parts[0], 0.0)


def ffn_fwd(x1, mod, n3, n4, wg_t, wu_t, wd, n_ctx):
    bl, s, _ = x1.shape
    tf = _ffn_tile(s)
    nk = D_FF // FF_CHUNK
    tile = pl.BlockSpec((None, tf, D), lambda b, j, k: (b, j, 0))
    modspec = pl.BlockSpec((None, 2, 8, D), lambda b, j, k: (b, 0, 0, 0))
    vec = pl.BlockSpec((1, D), lambda b, j, k: (0, 0))
    wspec = pl.BlockSpec((FF_CHUNK, D), lambda b, j, k: (k, 0))
    ftile = pl.BlockSpec((None, tf, FF_CHUNK), lambda b, j, k: (b, j, k))

    def body(x_ref, mod_ref, n3_ref, n4_ref, wg_ref, wu_ref, wd_ref, x2_ref, f_ref, gate_ref, up_ref, h_s, acc_s):
        j, k = pl.program_id(1), pl.program_id(2)

        @pl.when(k == 0)
        def _():
            _, xn = rms_stats(x_ref[...])
            n3 = n3_ref[...]
            h_s[...] = _by_slab(lambda v, sh, sc: (v * n3 * (1.0 + sc) + sh).astype(BF16), n_ctx, xn,
                                _mod_rows(mod_ref, j, 3), _mod_rows(mod_ref, j, 4))
            acc_s[...] = jnp.zeros_like(acc_s)

        h = h_s[...]
        gate = dot_nt(h, wg_ref[...])
        up = dot_nt(h, wu_ref[...])
        gate_ref[...] = gate.astype(BF16)
        up_ref[...] = up.astype(BF16)
        act = (gate * jax.nn.sigmoid(gate)) * up
        acc_s[...] += dot_nn(act.astype(BF16), wd_ref[...])

        @pl.when(k == nk - 1)
        def _():
            f = acc_s[...]
            f_ref[...] = f
            _, fn = rms_stats(f)
            n4 = n4_ref[...]
            x2_ref[...] = _by_slab(lambda xv, fv, g2: xv + g2 * (fv * n4), n_ctx, x_ref[...], fn, _mod_rows(mod_ref, j, 5))

    return pl.pallas_call(
        body, name="ffn_fwd", grid=(bl, s // tf, nk),
        in_specs=[tile, modspec, vec, vec, wspec, wspec, wspec],
        out_specs=[tile, tile, ftile, ftile],
        out_shape=[jax.ShapeDtypeStruct((bl, s, D), F32), jax.ShapeDtypeStruct((bl, s, D), F32),
                   jax.ShapeDtypeStruct((bl, s, D_FF), BF16), jax.ShapeDtypeStruct((bl, s, D_FF), BF16)],
        scratch_shapes=[pltpu.VMEM((tf, D), BF16), pltpu.VMEM((tf, D), F32)],
        compiler_params=_cp(("arbitrary", "arbitrary", "arbitrary")),
    )(x1, mod, n3, n4, wg_t, wu_t, wd)


def ffn_bwd(dx2, x1, f, gate_b, up_b, mod, n3, n4, wg_t, wu_t, wd, n_ctx):
    bl, s, _ = x1.shape
    tf = _ffn_tile(s)
    nt = s // tf
    nk = D_FF // FF_CHUNK
    tile = pl.BlockSpec((None, tf, D), lambda b, j, k: (b, j, 0))
    ftile = pl.BlockSpec((None, tf, FF_CHUNK), lambda b, j, k: (b, j, jnp.minimum(k, nk - 1)))
    modspec = pl.BlockSpec((None, 2, 8, D), lambda b, j, k: (b, 0, 0, 0))
    vec = pl.BlockSpec((1, D), lambda b, j, k: (0, 0))
    wspec = pl.BlockSpec((FF_CHUNK, D), lambda b, j, k: (jnp.minimum(k, nk - 1), 0))
    wprev = pl.BlockSpec((FF_CHUNK, D), lambda b, j, k: (jnp.maximum(k - 1, 0), 0))
    stat = pl.BlockSpec((None, None, 8, D), lambda b, j, k: (b, j, 0, 0))

    def body(dx_ref, x_ref, f_ref, gate_ref, up_ref, mod_ref, n3_ref, n4_ref, wd_ref, wgp_ref, wup_ref,
             dx1_ref, h_ref, df_ref, act_ref, dgate_ref, dup_ref, st_ref, df_s, acc_s, dgate_s, dup_s):
        j, k = pl.program_id(1), pl.program_id(2)

        @pl.when(k == 0)
        def _():
            dx = dx_ref[...]
            n4 = n4_ref[...]
            rf, fn = rms_stats(f_ref[...])
            st_ref[...] = jnp.zeros_like(st_ref)
            _split_sum(dx * (fn * n4), n_ctx, j, st_ref, 2)
            dxg = _by_slab(lambda dv, g2: dv * g2, n_ctx, dx, _mod_rows(mod_ref, j, 5))
            st_ref[4:5, :] = colsum(dxg * fn)
            df = rms_bwd(rf, fn, dxg * n4).astype(BF16)
            df_s[...] = df
            df_ref[...] = df
            _, xn = rms_stats(x_ref[...])
            n3 = n3_ref[...]
            h_ref[...] = _by_slab(lambda v, sh, sc: (v * n3 * (1.0 + sc) + sh).astype(BF16), n_ctx, xn,
                                  _mod_rows(mod_ref, j, 3), _mod_rows(mod_ref, j, 4))
            acc_s[...] = jnp.zeros_like(acc_s)
            dgate_s[1] = jnp.zeros((tf, FF_CHUNK), BF16)
            dup_s[1] = jnp.zeros((tf, FF_CHUNK), BF16)

        prev = (k + 1) % 2
        acc_s[...] += dot_nn(dgate_s[prev], wgp_ref[...]) + dot_nn(dup_s[prev], wup_ref[...])
        gate = gate_ref[...].astype(F32)
        up = up_ref[...].astype(F32)
        sg = jax.nn.sigmoid(gate)
        silu = gate * sg
        dact = dot_nt(df_s[...], wd_ref[...])
        act_ref[...] = (silu * up).astype(BF16)
        dgate = (dact * up * (sg * (1.0 + gate * (1.0 - sg)))).astype(BF16)
        dup = (dact * silu).astype(BF16)
        dgate_ref[...] = dgate
        dup_ref[...] = dup
        dgate_s[k % 2] = dgate
        dup_s[k % 2] = dup

        @pl.when(k == nk)
        def _():
            dh = acc_s[...]
            r, xn = rms_stats(x_ref[...])
            n3 = n3_ref[...]
            xg = xn * n3
            dyv = _by_slab(lambda dv, sc: dv * (1.0 + sc), n_ctx, dh, _mod_rows(mod_ref, j, 4))
            _split_sum(dh, n_ctx, j, st_ref, 0)
            _split_sum(dh * xg, n_ctx, j, st_ref, 1)
            st_ref[3:4, :] = colsum(dyv * xn)
            dx1_ref[...] = dx_ref[...] + rms_bwd(r, xn, dyv * n3)

    return pl.pallas_call(
        body, name="ffn_bwd", grid=(bl, nt, nk + 1),
        in_specs=[tile, tile, tile, ftile, ftile, modspec, vec, vec, wspec, wprev, wprev],
        out_specs=[tile, tile, tile, ftile, ftile, ftile, stat],
        out_shape=[jax.ShapeDtypeStruct((bl, s, D), F32), jax.ShapeDtypeStruct((bl, s, D), BF16),
                   jax.ShapeDtypeStruct((bl, s, D), BF16), jax.ShapeDtypeStruct((bl, s, D_FF), BF16),
                   jax.ShapeDtypeStruct((bl, s, D_FF), BF16), jax.ShapeDtypeStruct((bl, s, D_FF), BF16),
                   jax.ShapeDtypeStruct((bl, nt, 8, D), F32)],
        scratch_shapes=[pltpu.VMEM((tf, D), BF16), pltpu.VMEM((tf, D), F32),
                        pltpu.VMEM((2, tf, FF_CHUNK), BF16), pltpu.VMEM((2, tf, FF_CHUNK), BF16)],
        compiler_params=_cp(("arbitrary", "arbitrary", "arbitrary")),
    )(dx2, x1, f, gate_b, up_b, mod, n3, n4, wd, wg_t, wu_t)


def loss_head(xs, target, nc):
    bl, s, _ = xs.shape
    nt = s // TT

    def body(x_ref, t_ref, dx_ref, l_ref):
        j = pl.program_id(1)

        @pl.when(j < nc)
        def _():
            dx_ref[...] = jnp.zeros_like(dx_ref)
            l_ref[...] = jnp.zeros_like(l_ref)

        @pl.when(j >= nc)
        def _():
            e = x_ref[...] - t_ref[...]
            dx_ref[...] = e * (1.0 / D)
            tok = jnp.mean(e * e, axis=-1, keepdims=True)
            l_ref[...] = jnp.zeros_like(l_ref) + 0.5 * jnp.sum(tok, axis=0, keepdims=True)

    return pl.pallas_call(
        body, name="loss_head", grid=(bl, nt),
        in_specs=[_tile_spec(D), pl.BlockSpec((None, TT, D), lambda b, j: (b, jnp.maximum(j - nc, 0), 0))],
        out_specs=[_tile_spec(D), pl.BlockSpec((None, None, 8, 128), lambda b, j: (b, j, 0, 0))],
        out_shape=[jax.ShapeDtypeStruct((bl, s, D), F32), jax.ShapeDtypeStruct((bl, nt, 8, 128), F32)],
        compiler_params=_cp(("arbitrary", "arbitrary")),
    )(xs, target)


def tn_matmul(a, b, name):
    t, ka = a.shape
    n = b.shape[1]
    tk = ka
    tt = next(x for x in (1024, 512, 256) if t % x == 0)
    nsteps = t // tt

    def body(a_ref, b_ref, o_ref, acc_s):
        @pl.when(pl.program_id(1) == 0)
        def _():
            acc_s[...] = jnp.zeros_like(acc_s)

        acc_s[...] += dot_tn(a_ref[...], b_ref[...])

        @pl.when(pl.program_id(1) == nsteps - 1)
        def _():
            o_ref[...] = acc_s[...].astype(BF16)

    return pl.pallas_call(
        body, name=name, grid=(ka // tk, nsteps),
        in_specs=[pl.BlockSpec((tt, tk), lambda i, s: (s, i)), pl.BlockSpec((tt, n), lambda i, s: (s, 0))],
        out_specs=pl.BlockSpec((tk, n), lambda i, s: (i, 0)),
        out_shape=jax.ShapeDtypeStruct((ka, n), BF16),
        scratch_shapes=[pltpu.VMEM((tk, n), F32)],
        compiler_params=_cp(("arbitrary", "arbitrary")),
    )(a, b)


def qmm(terms, name):
    r = terms[0][0].shape[0]
    rt = r // 2 if r % 16 == 0 and r >= 512 else r
    n = len(terms)

    def body(*refs):
        acc = None
        for k in range(n):
            y = dot_nn(refs[2 * k][...].astype(BF16), refs[2 * k + 1][...])
            acc = y if acc is None else acc + y
        refs[2 * n][...] = acc

    row = pl.BlockSpec((rt, QW), lambda q, i: (i, q))
    wspec = pl.BlockSpec((None, QW, QW), lambda q, i: (q, 0, 0))
    return pl.pallas_call(
        body, name=name, grid=(4, r // rt), in_specs=[row, wspec] * n, out_specs=row,
        out_shape=jax.ShapeDtypeStruct((r, ROW_W), F32),
        compiler_params=_cp(("arbitrary", "arbitrary")),
    )(*[x for term in terms for x in term])


def _same_group(rows, cols, row_group, col_group):
    ri = jnp.bitwise_and(lax.broadcasted_iota(jnp.int32, (rows, cols), 0) // row_group, GQ - 1)
    ci = jnp.bitwise_and(lax.broadcasted_iota(jnp.int32, (rows, cols), 1) // col_group, GQ - 1)
    return ri == ci


def _spread_matrix():
    m = np.zeros((2 * SSM_P, QW), np.float32)
    for reim in range(2):
        for g in range(GQ):
            for p in range(SSM_P):
                m[reim * SSM_P + p, reim * (QW // 2) + g * SSM_P + p] = 1.0
    return jnp.asarray(m, BF16)


def assemble_ts(v, name):
    def body(v_ref, f_ref, big_ref, bigt_ref):
        keep = _same_group(GQ * SSM_H, QW, SSM_H, SSM_P)
        for e in range(TC):
            hi, lo = split_bf16(v_ref[e])
            t = jnp.where(keep, dot_nn(hi, f_ref[...]) + dot_nn(lo, f_ref[...]), 0.0)
            big_ref[e * 128:(e + 1) * 128, :] = t.astype(BF16)
            bigt_ref[:, e * 128:(e + 1) * 128] = t.T.astype(BF16)

    return pl.pallas_call(
        body, name=name, grid=(4,),
        in_specs=[pl.BlockSpec((None, TC, 128, 128), lambda q: (q, 0, 0, 0)),
                  pl.BlockSpec((128, QW), lambda q: (0, 0))],
        out_specs=[pl.BlockSpec((None, QW, QW), lambda q: (q, 0, 0))] * 2,
        out_shape=[jax.ShapeDtypeStruct((4, QW, QW), BF16), jax.ShapeDtypeStruct((4, QW, QW), BF16)],
        compiler_params=_cp(("arbitrary",)),
    )(v, _spread_matrix())


def assemble_tt(lags, name):
    def body(l_ref, m_ref, mt_ref):
        blocks = [l_ref[n] for n in range(2 * TC - 1)]
        flipped = [b.T.astype(BF16) for b in blocks]
        blocks = [b.astype(BF16) for b in blocks]
        for s in range(TC):
            for t in range(TC):
                m_ref[s * 128:(s + 1) * 128, t * 128:(t + 1) * 128] = blocks[t - s + TC - 1]
                mt_ref[t * 128:(t + 1) * 128, s * 128:(s + 1) * 128] = flipped[t - s + TC - 1]

    return pl.pallas_call(
        body, name=name, grid=(4,),
        in_specs=[pl.BlockSpec((None, 2 * TC - 1, 128, 128), lambda q: (q, 0, 0, 0))],
        out_specs=[pl.BlockSpec((None, QW, QW), lambda q: (q, 0, 0))] * 2,
        out_shape=[jax.ShapeDtypeStruct((4, QW, QW), BF16)] * 2,
        compiler_params=_cp(("arbitrary",)),
    )(lags)


def _qtn_call(body, a, b, out_shape, out_block, extra, name):
    r = a.shape[0]
    col = pl.BlockSpec((r, QW), lambda q: (0, q))
    return pl.pallas_call(
        body, name=name, grid=(4,),
        in_specs=[col, col] + [pl.BlockSpec(x.shape, lambda q: (0, 0)) for x in extra],
        out_specs=pl.BlockSpec((None,) + out_block, lambda q: (q,) + (0,) * len(out_block)),
        out_shape=jax.ShapeDtypeStruct((4,) + out_block, F32),
        compiler_params=_cp(("arbitrary",)),
    )(a, b, *extra)


def qtn_ts(a, b, name):
    def body(a_ref, b_ref, f_ref, o_ref):
        full = dot_tn(a_ref[...].astype(BF16), b_ref[...].astype(BF16))
        keep = _same_group(GQ * SSM_H, QW, SSM_H, SSM_P)
        for e in range(TC):
            hi, lo = split_bf16(jnp.where(keep, full[e * 128:(e + 1) * 128, :], 0.0))
            o_ref[e] = dot_nt(hi, f_ref[...]) + dot_nt(lo, f_ref[...])

    return _qtn_call(body, a, b, None, (TC, 128, 128), [_spread_matrix()], name)


def qtn_tt(a, b, name):
    def body(a_ref, b_ref, o_ref):
        full = dot_tn(a_ref[...].astype(BF16), b_ref[...].astype(BF16))
        for lag in range(-(TC - 1), TC):
            acc = None
            for s in range(TC):
                t = s + lag
                if 0 <= t < TC:
                    blk = full[s * 128:(s + 1) * 128, t * 128:(t + 1) * 128]
                    acc = blk if acc is None else acc + blk
            o_ref[lag + TC - 1] = acc

    return _qtn_call(body, a, b, None, (2 * TC - 1, 128, 128), [], name)


def _scan_row(i, rb, ncr, reverse):
    if not reverse:
        return i
    return jnp.where(i < ncr, ncr - 1 - i, rb - 1 - (i - ncr))


def _swap_re_im(h):
    half = QW // 2
    return jnp.concatenate([h[:, q * QW + (1 - k) * half:q * QW + (2 - k) * half] for q in range(4) for k in range(2)],
                           axis=1)


def chunk_scan(xs, lam_ab, ncr, reverse, name):
    bl, rb, _ = xs.shape

    def body(x_ref, l_ref, hp_ref):
        la, lb = l_ref[0:1, :], l_ref[1:2, :]

        def step(i, h):
            row = _scan_row(i, rb, ncr, reverse)
            hp_ref[pl.ds(row, 1), :] = h
            return la * h + lb * _swap_re_im(h) + x_ref[pl.ds(row, 1), :]

        lax.fori_loop(0, rb, step, jnp.zeros((1, ROW_W), F32))

    blk = pl.BlockSpec((None, rb, ROW_W), lambda b: (b, 0, 0))
    return pl.pallas_call(
        body, name=name, grid=(bl,),
        in_specs=[blk, pl.BlockSpec((8, ROW_W), lambda b: (0, 0))], out_specs=blk,
        out_shape=jax.ShapeDtypeStruct((bl, rb, ROW_W), F32),
        compiler_params=_cp(("arbitrary",)),
    )(xs, lam_ab)


def chunk_scan_bwd(dhp, hp, lam_ab, ncr, reverse, name):
    bl, rb, _ = dhp.shape

    def body(d_ref, hp_ref, l_ref, g_ref, dl_ref):
        la, lb = l_ref[0:1, :], l_ref[1:2, :]

        dl_ref[...] = jnp.zeros_like(dl_ref)

        def step(n, g):
            row = _scan_row(rb - 1 - n, rb, ncr, reverse)
            g_ref[pl.ds(row, 1), :] = g
            pv = hp_ref[pl.ds(row, 1), :]
            dl_ref[0:1, :] += g * pv
            dl_ref[1:2, :] += g * _swap_re_im(pv)
            return d_ref[pl.ds(row, 1), :] + la * g + _swap_re_im(lb * g)

        lax.fori_loop(0, rb, step, jnp.zeros((1, ROW_W), F32))

    blk = pl.BlockSpec((None, rb, ROW_W), lambda b: (b, 0, 0))
    return pl.pallas_call(
        body, name=name, grid=(bl,),
        in_specs=[blk, blk, pl.BlockSpec((8, ROW_W), lambda b: (0, 0))],
        out_specs=[blk, pl.BlockSpec((None, 8, ROW_W), lambda b: (b, 0, 0))],
        out_shape=[jax.ShapeDtypeStruct((bl, rb, ROW_W), F32), jax.ShapeDtypeStruct((bl, 8, ROW_W), F32)],
        compiler_params=_cp(("arbitrary",)),
    )(dhp, hp, lam_ab)


def _quarter_rows(v):
    e = v.shape[0]
    return v.reshape(e, 4, 8, SSM_P, SSM_H).transpose(0, 1, 2, 4, 3).reshape(e, 4, 8 * SSM_H, SSM_P)


def _token_state_map(vr, vi):
    return jnp.concatenate([_quarter_rows(vr), _quarter_rows(vi)], axis=-1).transpose(1, 0, 2, 3)


def ssm_build(lam_re, lam_im, log_dt, b_re, b_im, c_re, c_im, d):
    dt = jnp.exp(log_dt)[..., None]
    mag = jnp.exp(lam_re * dt)
    ang = lam_im * dt
    lr, li = mag * jnp.cos(ang), mag * jnp.sin(ang)
    den = lam_re * lam_re + lam_im * lam_im
    nr = lr - 1.0
    fr = (nr * lam_re + li * lam_im) / den
    fi = (li * lam_re - nr * lam_im) / den
    bbr = fr[..., None] * b_re - fi[..., None] * b_im
    bbi = fr[..., None] * b_im + fi[..., None] * b_re
    pr, pi = [jnp.ones_like(lr)], [jnp.zeros_like(lr)]
    for _ in range(TC):
        pr, pi = pr + [pr[-1] * lr - pi[-1] * li], pi + [pr[-1] * li + pi[-1] * lr]
    pr, pi = jnp.stack(pr), jnp.stack(pi)
    clr = c_re[None] * pr[:, :, :, None, :] - c_im[None] * pi[:, :, :, None, :]
    cli = c_re[None] * pi[:, :, :, None, :] + c_im[None] * pr[:, :, :, None, :]
    same_group = jnp.asarray(np.kron(np.eye(8), np.ones((SSM_H, SSM_H))), F32)
    ein = functools.partial(jnp.einsum, precision=HI)

    out, lag_blocks = {}, {}
    for k, name in ((0, "f"), (1, "r")):
        ar, ai = _quarter_rows(bbr[k][None])[0], _quarter_rows(bbi[k][None])[0]
        cr = clr[:TC, k].reshape(TC, 4, 8 * SSM_H, SSM_P)
        ci = cli[:TC, k].reshape(TC, 4, 8 * SSM_H, SSM_P)
        lag_blocks[k] = (ein('qap,nqbp->nqab', ar, cr) - ein('qap,nqbp->nqab', ai, ci)) * same_group
        es = [TC - 1 - s for s in range(TC)] if k == 0 else list(range(TC))
        sr = jnp.stack([pr[e, k][:, :, None] * bbr[k] - pi[e, k][:, :, None] * bbi[k] for e in es])
        si = jnp.stack([pr[e, k][:, :, None] * bbi[k] + pi[e, k][:, :, None] * bbr[k] for e in es])
        out["bs_" + name] = _token_state_map(sr, si)
        et = [t + 1 for t in range(TC)] if k == 0 else [TC - t for t in range(TC)]
        crt = jnp.stack([jnp.swapaxes(clr[e, k], 1, 2) for e in et])
        cit = jnp.stack([-jnp.swapaxes(cli[e, k], 1, 2) for e in et])
        out["cst_" + name] = _token_state_map(crt, cit)
        l8r, l8i = pr[TC, k].reshape(4, 1, QW // 2), pi[TC, k].reshape(4, 1, QW // 2)
        la = jnp.concatenate([l8r, l8r], axis=1).reshape(1, ROW_W)
        lb = jnp.concatenate([-l8i, l8i], axis=1).reshape(1, ROW_W)
        out["lam_" + name] = jnp.concatenate([la, lb, jnp.zeros((6, ROW_W), F32)], axis=0)
    skip = jnp.eye(8 * SSM_H, dtype=F32)[None] * d.reshape(4, 1, 8 * SSM_H)
    center = lag_blocks[0][0] + lag_blocks[1][0] + skip
    lags = [lag_blocks[1][n] for n in range(TC - 1, 0, -1)] + [center] + [lag_blocks[0][n] for n in range(1, TC)]
    out["lags"] = jnp.stack(lags, axis=1)
    return out


def ssm_operators(mats, tag):
    ops = {}
    ops["m"], ops["mt"] = assemble_tt(mats["lags"], "ssm_map_intra" + tag)
    for dname in ("f", "r"):
        ops["bs_" + dname], ops["bst_" + dname] = assemble_ts(mats["bs_" + dname], f"ssm_map_state_in_{dname}{tag}")
        ops["cst_" + dname], ops["cs_" + dname] = assemble_ts(mats["cst_" + dname], f"ssm_map_readout_{dname}{tag}")
    return ops


def ssm_forward(u3, mats, ops, ncr):
    bl, rb, _ = u3.shape
    u = u3.reshape(bl * rb, ROW_W)
    hps, terms = {}, [(u, ops["m"])]
    for dname, rev in (("f", False), ("r", True)):
        xs = qmm([(u, ops["bs_" + dname])], "ssm_state_in_" + dname)
        hp = chunk_scan(xs.reshape(bl, rb, ROW_W), mats["lam_" + dname], ncr, rev, "ssm_scan_" + dname)
        hps[dname] = hp.reshape(bl * rb, ROW_W)
        terms.append((hps[dname], ops["cs_" + dname]))
    return qmm(terms, "ssm_output").reshape(bl, rb, ROW_W), hps


def ssm_backward(dy3, u3, hps, mats, ops, ncr):
    bl, rb, _ = u3.shape
    u = u3.reshape(bl * rb, ROW_W)
    dyr = dy3.reshape(bl * rb, ROW_W)
    cot = {"lags": qtn_tt(u, dyr, "ssm_d_intra")}
    terms = [(dyr, ops["mt"])]
    for dname, rev in (("f", False), ("r", True)):
        dhp = qmm([(dyr, ops["cst_" + dname])], "ssm_dstate_" + dname)
        g, dl = chunk_scan_bwd(dhp.reshape(bl, rb, ROW_W), hps[dname].reshape(bl, rb, ROW_W), mats["lam_" + dname],
                               ncr, rev, "ssm_scan_bwd_" + dname)
        g = g.reshape(bl * rb, ROW_W)
        cot["lam_" + dname] = jnp.sum(dl, axis=0)
        cot["bs_" + dname] = qtn_ts(u, g, "ssm_d_state_in_" + dname)
        cot["cst_" + dname] = qtn_ts(dyr, hps[dname], "ssm_d_readout_" + dname)
        terms.append((g, ops["bst_" + dname]))
    return qmm(terms, "ssm_input_grad").reshape(bl, rb, ROW_W), cot


def mod_forward(act, w_mod, b_cols):
    nl, _, wc = w_mod.shape
    r = act.shape[0]

    def body(a_ref, w_ref, b_ref, o_ref):
        o_ref[...] = dot_nn(a_ref[...].astype(BF16), w_ref[...].astype(BF16)) + b_ref[...]

    return pl.pallas_call(
        body, name="mod_forward", grid=(nl,),
        in_specs=[pl.BlockSpec((r, D), lambda l: (0, 0)), pl.BlockSpec((None, D, wc), lambda l: (l, 0, 0)),
                  pl.BlockSpec((None, 1, wc), lambda l: (l, 0, 0))],
        out_specs=pl.BlockSpec((None, r, wc), lambda l: (l, 0, 0)),
        out_shape=jax.ShapeDtypeStruct((nl, r, wc), F32),
        compiler_params=_cp(("arbitrary",)),
    )(act, w_mod, b_cols)


def mod_backward(act, dmod, dctx, w_mod):
    nl, _, wc = w_mod.shape
    r = act.shape[0]

    def body(a_ref, d_ref, c_ref, w_ref, gw_ref, gc_ref):
        gw_ref[...] = dot_tn(a_ref[...].astype(BF16), d_ref[...].astype(BF16))
        gc_ref[...] = dot_nt(c_ref[...].astype(BF16), w_ref[...].astype(BF16))

    return pl.pallas_call(
        body, name="mod_backward", grid=(nl,),
        in_specs=[pl.BlockSpec((r, D), lambda l: (0, 0)), pl.BlockSpec((None, r, wc), lambda l: (l, 0, 0)),
                  pl.BlockSpec((None, 8, wc), lambda l: (l, 0, 0)), pl.BlockSpec((None, D, wc), lambda l: (l, 0, 0))],
        out_specs=[pl.BlockSpec((None, D, wc), lambda l: (l, 0, 0)), pl.BlockSpec((None, 8, D), lambda l: (l, 0, 0))],
        out_shape=[jax.ShapeDtypeStruct((nl, D, wc), F32), jax.ShapeDtypeStruct((nl, 8, D), F32)],
        compiler_params=_cp(("arbitrary",)),
    )(act, dmod, dctx, w_mod)


def _place():
    return lax.axis_index("x"), lax.axis_index("y"), lax.axis_index("c")


def all_gather_rows(arrs, name):
    n = len(arrs)
    rs = [a.shape[1] for a in arrs]

    def body(*refs):
        x_refs, o_refs = refs[:n], refs[n:2 * n]
        send_sems, recv_sems, local_sems = refs[2 * n:]
        x, y, c = _place()
        me, sibling = (x, y, c), (x, y, 1 - c)
        chips = [(1 - x, y), (x, 1 - y), (1 - x, 1 - y)]

        def rows(a, px, py, pc):
            return o_refs[a].at[:, pl.ds((4 * px + 2 * py + pc) * rs[a], rs[a]), :]

        def copy(a, k, block, to, src=None):
            return pltpu.make_async_remote_copy(
                src_ref=rows(a, *block) if src is None else src, dst_ref=rows(a, *block),
                send_sem=send_sems.at[a, k], recv_sem=recv_sems.at[a, k], device_id=to, device_id_type=MESH)

        mine = [pltpu.make_async_copy(x_refs[a], rows(a, *me), local_sems.at[a]) for a in range(n)]
        for cp in mine:
            cp.start()
        first = []
        for a in range(n):
            first.append(copy(a, 0, me, sibling, src=x_refs[a]))
            first += [copy(a, 1 + j, me, (*chip, c), src=x_refs[a]) for j, chip in enumerate(chips)]
        for cp in first:
            cp.start()
        passed = []
        for j, chip in enumerate(chips):
            for a in range(n):
                copy(a, 1 + j, (*chip, c), me).wait_recv()
                fwd = copy(a, 4 + j, (*chip, c), sibling)
                fwd.start()
                passed.append(fwd)
        for a in range(n):
            copy(a, 0, sibling, me).wait_recv()
            for j, chip in enumerate(chips):
                copy(a, 4 + j, (*chip, 1 - c), me).wait_recv()
        for cp in first + passed:
            cp.wait_send()
        for cp in mine:
            cp.wait()

    any_spec = pl.BlockSpec(memory_space=pl.ANY)
    return pl.pallas_call(
        body, name=name,
        in_specs=[any_spec] * n, out_specs=[any_spec] * n,
        out_shape=[jax.ShapeDtypeStruct((a.shape[0], N_DEV * a.shape[1], a.shape[2]), a.dtype) for a in arrs],
        scratch_shapes=[pltpu.SemaphoreType.DMA((n, 7)), pltpu.SemaphoreType.DMA((n, 7)), pltpu.SemaphoreType.DMA((n,))],
    )(*arrs)


def all_to_all_rows(arrs, name):
    n = len(arrs)
    rs = [a.shape[1] // N_DEV for a in arrs]
    flips = [(fx, fy, fc) for fx in (0, 1) for fy in (0, 1) for fc in (0, 1)][1:]

    def body(*refs):
        x_refs, o_refs = refs[:n], refs[n:2 * n]
        send_sems, recv_sems, local_sems = refs[2 * n:]
        x, y, c = _place()
        my_idx = 4 * x + 2 * y + c

        def block(a, idx):
            return x_refs[a].at[:, pl.ds(idx * rs[a], rs[a]), :]

        mine = [pltpu.make_async_copy(block(a, my_idx), o_refs[a].at[my_idx], local_sems.at[a]) for a in range(n)]
        for cp in mine:
            cp.start()
        sends = []
        for k, (fx, fy, fc) in enumerate(flips):
            px = 1 - x if fx else x
            py = 1 - y if fy else y
            pc = 1 - c if fc else c
            p_idx = 4 * px + 2 * py + pc
            for a in range(n):
                sends.append(pltpu.make_async_remote_copy(
                    src_ref=block(a, p_idx), dst_ref=o_refs[a].at[my_idx], send_sem=send_sems.at[a, k],
                    recv_sem=recv_sems.at[a, k], device_id=(px, py, pc), device_id_type=MESH))
        for cp in sends:
            cp.start()
        for k, (fx, fy, fc) in enumerate(flips):
            px = 1 - x if fx else x
            py = 1 - y if fy else y
            pc = 1 - c if fc else c
            p_idx = 4 * px + 2 * py + pc
            for a in range(n):
                pltpu.make_async_remote_copy(
                    src_ref=block(a, p_idx), dst_ref=o_refs[a].at[p_idx], send_sem=send_sems.at[a, k],
                    recv_sem=recv_sems.at[a, k], device_id=(px, py, pc), device_id_type=MESH).wait_recv()
        for cp in sends:
            cp.wait_send()
        for cp in mine:
            cp.wait()

    any_spec = pl.BlockSpec(memory_space=pl.ANY)
    return pl.pallas_call(
        body, name=name,
        in_specs=[any_spec] * n, out_specs=[any_spec] * n,
        out_shape=[jax.ShapeDtypeStruct((N_DEV, a.shape[0], r, a.shape[2]), a.dtype) for a, r in zip(arrs, rs)],
        scratch_shapes=[pltpu.SemaphoreType.DMA((n, 7)), pltpu.SemaphoreType.DMA((n, 7)), pltpu.SemaphoreType.DMA((n,))],
    )(*arrs)


def _peers():
    x, y, c = _place()
    out = []
    for fx in (0, 1):
        for fy in (0, 1):
            for fc in (0, 1):
                if fx or fy or fc:
                    px, py, pc = (1 - x if fx else x), (1 - y if fy else y), (1 - c if fc else c)
                    out.append(((px, py, pc), 4 * px + 2 * py + pc))
    return out, 4 * x + 2 * y + c


def _split_call(body, name, ins, n_sem_out, thru, extra_out_shape, extra_out_specs, sem_ins=(), after=None):
    hbm = pl.BlockSpec(memory_space=pltpu.HBM)
    sem = pl.BlockSpec(memory_space=pltpu.SEMAPHORE)
    n_thru = len(thru)
    tail_in = [sem] * len(sem_ins) + ([pl.BlockSpec(memory_space=pl.ANY)] if after is not None else [])
    return pl.pallas_call(
        body, name=name,
        out_shape=tuple(n_sem_out) + tuple(pltpu.HBM(a.shape, a.dtype) for a in thru) + tuple(extra_out_shape),
        in_specs=[hbm] * n_thru + tail_in,
        out_specs=(sem,) * len(n_sem_out) + (hbm,) * n_thru + tuple(extra_out_specs),
        input_output_aliases={i: i + len(n_sem_out) for i in range(n_thru)},
        compiler_params=pltpu.CompilerParams(has_side_effects=pltpu.SideEffectType.DATAFLOW_SIDE_EFFECTING),
    )(*ins, *sem_ins, *([after] if after is not None else []))


def gather_start(shards, after, name):
    n = len(shards)
    rs = [a.shape[1] for a in shards]
    lands = [lax.empty((a.shape[0], N_DEV * a.shape[1], a.shape[2]), a.dtype) for a in shards]

    def body(*refs):
        x_refs, land_refs = refs[:n], refs[n:2 * n]
        send_sems, recv_sems = refs[2 * n + 1], refs[2 * n + 2]
        peers, my_idx = _peers()
        for k, (peer, _) in enumerate(peers):
            for a in range(n):
                pltpu.make_async_remote_copy(
                    src_ref=x_refs[a], dst_ref=land_refs[a].at[:, pl.ds(my_idx * rs[a], rs[a]), :],
                    send_sem=send_sems.at[a * 7 + k], recv_sem=recv_sems.at[a * 7 + k], device_id=peer,
                    device_id_type=MESH).start()
        refs[-1][...] = jnp.zeros_like(refs[-1])

    ins = [pltpu.with_memory_space_constraint(a, pltpu.HBM) for a in list(shards) + lands]
    outs = _split_call(body, name, ins, [pltpu.SemaphoreType.DMA((n * 7,))] * 2, ins,
                       [jax.ShapeDtypeStruct((8, 128), F32)], [pl.BlockSpec(memory_space=pltpu.VMEM)], after=after)
    return outs[0], outs[1], list(outs[2:2 + n]), list(outs[2 + n:2 + 2 * n]), outs[-1]


def gather_wait(send_sems, recv_sems, shards, lands, after, name):
    n = len(shards)
    rs = [a.shape[1] for a in shards]

    def body(*refs):
        x_refs, land_refs = refs[:n], refs[n:2 * n]
        s_sems, r_sems = refs[2 * n], refs[2 * n + 1]
        peers, _ = _peers()
        for k, (peer, p_idx) in enumerate(peers):
            for a in range(n):
                copy = pltpu.make_async_remote_copy(
                    src_ref=x_refs[a], dst_ref=land_refs[a].at[:, pl.ds(p_idx * rs[a], rs[a]), :],
                    send_sem=s_sems.at[a * 7 + k], recv_sem=r_sems.at[a * 7 + k], device_id=peer, device_id_type=MESH)
                copy.wait_send()
                copy.wait_recv()

    outs = _split_call(body, name, list(shards) + list(lands), [], list(shards) + list(lands), [], [],
                       sem_ins=(send_sems, recv_sems), after=after)
    my_idx = 4 * lax.axis_index("x") + 2 * lax.axis_index("y") + lax.axis_index("c")
    return [lax.dynamic_update_slice_in_dim(z, s, my_idx * r, axis=1) for z, s, r in zip(outs[n:], outs[:n], rs)]


def scatter_start(arrs, name):
    n = len(arrs)
    rs = [a.shape[1] // N_DEV for a in arrs]
    lands = [lax.empty((N_DEV, a.shape[0], r, a.shape[2]), a.dtype) for a, r in zip(arrs, rs)]

    def body(*refs):
        x_refs, land_refs = refs[:n], refs[n:2 * n]
        send_sems, recv_sems = refs[2 * n], refs[2 * n + 1]
        token = refs[-1]
        peers, my_idx = _peers()
        for k, (peer, p_idx) in enumerate(peers):
            for a in range(n):
                pltpu.make_async_remote_copy(
                    src_ref=x_refs[a].at[:, pl.ds(p_idx * rs[a], rs[a]), :], dst_ref=land_refs[a].at[my_idx],
                    send_sem=send_sems.at[a * 7 + k], recv_sem=recv_sems.at[a * 7 + k], device_id=peer,
                    device_id_type=MESH).start()
        token[...] = jnp.zeros_like(token)

    hbm = pl.BlockSpec(memory_space=pltpu.HBM)
    sem = pl.BlockSpec(memory_space=pltpu.SEMAPHORE)
    outs = pl.pallas_call(
        body, name=name,
        out_shape=(pltpu.SemaphoreType.DMA((n * 7,)), pltpu.SemaphoreType.DMA((n * 7,)))
        + tuple(pltpu.HBM(a.shape, a.dtype) for a in arrs) + tuple(pltpu.HBM(z.shape, z.dtype) for z in lands)
        + (jax.ShapeDtypeStruct((8, 128), F32),),
        in_specs=[hbm] * (2 * n),
        out_specs=(sem, sem) + (hbm,) * (2 * n) + (pl.BlockSpec(memory_space=pltpu.VMEM),),
        input_output_aliases={i: i + 2 for i in range(2 * n)},
        compiler_params=pltpu.CompilerParams(has_side_effects=pltpu.SideEffectType.DATAFLOW_SIDE_EFFECTING),
    )(*[pltpu.with_memory_space_constraint(a, pltpu.HBM) for a in arrs],
      *[pltpu.with_memory_space_constraint(z, pltpu.HBM) for z in lands])
    return outs[0], outs[1], list(outs[2:2 + n]), list(outs[2 + n:2 + 2 * n]), outs[-1]


def scatter_wait(send_sems, recv_sems, arrs, lands, after, name):
    n = len(arrs)
    rs = [a.shape[1] // N_DEV for a in arrs]

    def body(*refs):
        x_refs, land_refs = refs[:n], refs[n:2 * n]
        s_sems, r_sems = refs[2 * n], refs[2 * n + 1]
        peers, my_idx = _peers()
        for k, (peer, p_idx) in enumerate(peers):
            for a in range(n):
                copy = pltpu.make_async_remote_copy(
                    src_ref=x_refs[a].at[:, pl.ds(p_idx * rs[a], rs[a]), :], dst_ref=land_refs[a].at[p_idx],
                    send_sem=s_sems.at[a * 7 + k], recv_sem=r_sems.at[a * 7 + k], device_id=peer, device_id_type=MESH)
                copy.wait_send()
                copy.wait_recv()

    hbm = pl.BlockSpec(memory_space=pltpu.HBM)
    sem = pl.BlockSpec(memory_space=pltpu.SEMAPHORE)
    outs = pl.pallas_call(
        body, name=name,
        out_shape=tuple(pltpu.HBM(a.shape, a.dtype) for a in arrs) + tuple(pltpu.HBM(z.shape, z.dtype) for z in lands),
        in_specs=[hbm] * (2 * n) + [sem, sem, pl.BlockSpec(memory_space=pl.ANY)],
        out_specs=(hbm,) * (2 * n),
        input_output_aliases={i: i for i in range(2 * n)},
        compiler_params=pltpu.CompilerParams(has_side_effects=pltpu.SideEffectType.DATAFLOW_SIDE_EFFECTING),
    )(*arrs, *lands, send_sems, recv_sems, after)
    return list(outs[:n]), list(outs[n:])


def _row_tile(rows, cap):
    best = None
    for t in range(16, min(rows, cap) + 1, 16):
        if rows % t == 0:
            best = t
    return rows if best is None else best


def adamw(w, gparts, m, v, name):
    per_layer = isinstance(gparts, (list, tuple))
    glist = list(gparts) if per_layer else [gparts]
    n, _, ra, cb = glist[0].shape
    nl = w.shape[0]
    ng = len(glist)
    ta = _row_tile(ra, max(8, (1 << 19) // (cb * n)))

    def slot_sum(g_ref):
        g = g_ref[0].astype(F32)
        for p in range(1, n):
            g = g + g_ref[p].astype(F32)
        return g

    def body(*refs):
        w_ref, g_refs = refs[0], refs[1:1 + ng]
        m_ref, v_ref, go_ref, d_ref, mo_ref, vo_ref = refs[1 + ng:]
        g = slot_sum(g_refs[0])
        for layer in range(1, ng):
            g = jnp.where(pl.program_id(0) == layer, slot_sum(g_refs[layer]), g)
        mn = ADAM_B1 * m_ref[...] + (1.0 - ADAM_B1) * g
        vn = ADAM_B2 * v_ref[...] + (1.0 - ADAM_B2) * jnp.square(g)
        m_hat = mn / (1.0 - ADAM_B1 ** ADAM_STEP)
        v_hat = vn / (1.0 - ADAM_B2 ** ADAM_STEP)
        go_ref[...] = g
        d_ref[...] = -ADAM_LR * (m_hat / (jnp.sqrt(v_hat) + ADAM_EPS) + ADAM_WD * w_ref[...])
        mo_ref[...] = mn
        vo_ref[...] = vn

    blk = pl.BlockSpec((None, ta, cb), lambda l, i: (l, i, 0))
    if per_layer:
        gblk = pl.BlockSpec((n, None, ta, cb), lambda l, i: (0, 0, i, 0))
    else:
        gblk = pl.BlockSpec((n, None, ta, cb), lambda l, i: (0, l, i, 0))
    shp = jax.ShapeDtypeStruct((nl, ra, cb), F32)
    return pl.pallas_call(
        body, name=name, grid=(nl, ra // ta),
        in_specs=[blk] + [gblk] * ng + [blk, blk], out_specs=[blk] * 4, out_shape=[shp] * 4,
        compiler_params=_cp(("arbitrary", "arbitrary")),
    )(w, *glist, m, v)


def _sincos_2d(rows, cols, dim):
    quarter = dim // 4
    omega = 1.0 / (10000.0 ** (jnp.arange(quarter, dtype=F32) / quarter))
    r = jnp.arange(rows, dtype=F32)[:, None] * omega
    cc = jnp.arange(cols, dtype=F32)[:, None] * omega
    er = jnp.concatenate([jnp.sin(r), jnp.cos(r)], axis=-1)
    ec = jnp.concatenate([jnp.sin(cc), jnp.cos(cc)], axis=-1)
    pe = jnp.concatenate([jnp.broadcast_to(er[:, None, :], (rows, cols, dim // 2)),
                          jnp.broadcast_to(ec[None, :, :], (rows, cols, dim // 2))], axis=-1)
    return pe.reshape(rows * cols, dim)


def _pool_constants():
    nw = len(POOL_WINDOWS)
    band = np.zeros((2, nw, TT, TT), np.float32)
    icnt = np.zeros((2, TT, C_W), np.float32)
    for kind, n in ((0, TT), (1, GRID_W)):
        for i, w in enumerate(POOL_WINDOWS):
            for t in range(TT):
                base, tl = (t // n) * n, t % n
                lo = min(max(tl - w // 2, 0), n)
                hi = min(max(tl - w // 2 + w, 0), n)
                band[kind, i, t, base + lo:base + hi] = 1.0
                icnt[kind, t, i * (C_W // nw):(i + 1) * (C_W // nw)] = 1.0 / (hi - lo)
    return jnp.asarray(band, BF16), jnp.asarray(icnt, F32)


def _block_diag(blocks):
    n, a, _ = blocks.shape
    return jnp.einsum('gab,gh->gahb', blocks, jnp.eye(n, dtype=F32), precision=HI).reshape(n * a, n * a)


def _block_diag_parts(mat, n):
    a = mat.shape[0] // n
    m4 = mat.reshape(n, a, n, a)
    return jnp.stack([m4[g, :, g, :] for g in range(n)])


_SMALL = ("c_ctx", "b_mod", "norm_mix_pre", "norm_mix_post", "norm_ffn_pre", "norm_ffn_post", "sgu_w", "sgu_b",
          "ssm_lam_re", "ssm_lam_im", "ssm_log_dt", "ssm_b_re", "ssm_b_im", "ssm_c_re", "ssm_c_im", "ssm_d",
          "glu_b", "pool_w", "pool_scale")
_WEIGHTS = ("c_ctx", "w_mod", "b_mod", "norm_mix_pre", "norm_mix_post", "norm_ffn_pre", "norm_ffn_post", "w_in", "w_out",
            "sgu_w", "sgu_b", "ssm_lam_re", "ssm_lam_im", "ssm_log_dt", "ssm_b_re", "ssm_b_im", "ssm_c_re", "ssm_c_im",
            "ssm_d", "glu_w", "glu_b", "pool_w", "pool_scale", "ffn_w_gate", "ffn_w_up", "ffn_w_down")


def _pack_rows(a):
    flat = a.reshape(-1)
    rows = -(-flat.shape[0] // D)
    rows8 = -(-rows // 8) * 8
    return jnp.pad(flat, (0, rows8 * D - flat.shape[0])).reshape(rows8, D)


def _pack(tree):
    packed = jnp.concatenate([_pack_rows(tree[k]) for k in _SMALL], axis=0)
    return jnp.pad(packed, ((0, -packed.shape[0] % 64), (0, 0)))


def _unpack(packed, like):
    out, at = {}, 0
    for k in _SMALL:
        size = int(np.prod(like[k].shape))
        rows8 = -(-(-(-size // D)) // 8) * 8
        out[k] = packed[at:at + rows8].reshape(-1)[:size].reshape(like[k].shape)
        at += rows8
    return out


def kernel(x, c, ctx, c_ctx, w_mod, b_mod, norm_mix_pre, norm_mix_post, norm_ffn_pre, norm_ffn_post, w_in, w_out, sgu_w, sgu_b, ssm_lam_re, ssm_lam_im, ssm_log_dt, ssm_b_re, ssm_b_im, ssm_c_re, ssm_c_im, ssm_d, glu_w, glu_b, pool_w, pool_scale, ffn_w_gate, ffn_w_up, ffn_w_down, loss_target, m_c_ctx, m_w_mod, m_b_mod, m_norm_mix_pre, m_norm_mix_post, m_norm_ffn_pre, m_norm_ffn_post, m_w_in, m_w_out, m_sgu_w, m_sgu_b, m_ssm_lam_re, m_ssm_lam_im, m_ssm_log_dt, m_ssm_b_re, m_ssm_b_im, m_ssm_c_re, m_ssm_c_im, m_ssm_d, m_glu_w, m_glu_b, m_pool_w, m_pool_scale, m_ffn_w_gate, m_ffn_w_up, m_ffn_w_down, v_c_ctx, v_w_mod, v_b_mod, v_norm_mix_pre, v_norm_mix_post, v_norm_ffn_pre, v_norm_ffn_post, v_w_in, v_w_out, v_sgu_w, v_sgu_b, v_ssm_lam_re, v_ssm_lam_im, v_ssm_log_dt, v_ssm_b_re, v_ssm_b_im, v_ssm_c_re, v_ssm_c_im, v_ssm_d, v_glu_w, v_glu_b, v_pool_w, v_pool_scale, v_ffn_w_gate, v_ffn_w_up, v_ffn_w_down):
    wts = dict(c_ctx=c_ctx, w_mod=w_mod, b_mod=b_mod, norm_mix_pre=norm_mix_pre, norm_mix_post=norm_mix_post,
               norm_ffn_pre=norm_ffn_pre, norm_ffn_post=norm_ffn_post, w_in=w_in, w_out=w_out, sgu_w=sgu_w, sgu_b=sgu_b,
               ssm_lam_re=ssm_lam_re, ssm_lam_im=ssm_lam_im, ssm_log_dt=ssm_log_dt, ssm_b_re=ssm_b_re, ssm_b_im=ssm_b_im,
               ssm_c_re=ssm_c_re, ssm_c_im=ssm_c_im, ssm_d=ssm_d, glu_w=glu_w, glu_b=glu_b, pool_w=pool_w,
               pool_scale=pool_scale, ffn_w_gate=ffn_w_gate, ffn_w_up=ffn_w_up, ffn_w_down=ffn_w_down)
    mom_m = dict(c_ctx=m_c_ctx, w_mod=m_w_mod, b_mod=m_b_mod, norm_mix_pre=m_norm_mix_pre, norm_mix_post=m_norm_mix_post,
                 norm_ffn_pre=m_norm_ffn_pre, norm_ffn_post=m_norm_ffn_post, w_in=m_w_in, w_out=m_w_out, sgu_w=m_sgu_w,
                 sgu_b=m_sgu_b, ssm_lam_re=m_ssm_lam_re, ssm_lam_im=m_ssm_lam_im, ssm_log_dt=m_ssm_log_dt,
                 ssm_b_re=m_ssm_b_re, ssm_b_im=m_ssm_b_im, ssm_c_re=m_ssm_c_re, ssm_c_im=m_ssm_c_im, ssm_d=m_ssm_d,
                 glu_w=m_glu_w, glu_b=m_glu_b, pool_w=m_pool_w, pool_scale=m_pool_scale, ffn_w_gate=m_ffn_w_gate,
                 ffn_w_up=m_ffn_w_up, ffn_w_down=m_ffn_w_down)
    mom_v = dict(c_ctx=v_c_ctx, w_mod=v_w_mod, b_mod=v_b_mod, norm_mix_pre=v_norm_mix_pre, norm_mix_post=v_norm_mix_post,
                 norm_ffn_pre=v_norm_ffn_pre, norm_ffn_post=v_norm_ffn_post, w_in=v_w_in, w_out=v_w_out, sgu_w=v_sgu_w,
                 sgu_b=v_sgu_b, ssm_lam_re=v_ssm_lam_re, ssm_lam_im=v_ssm_lam_im, ssm_log_dt=v_ssm_log_dt,
                 ssm_b_re=v_ssm_b_re, ssm_b_im=v_ssm_b_im, ssm_c_re=v_ssm_c_re, ssm_c_im=v_ssm_c_im, ssm_d=v_ssm_d,
                 glu_w=v_glu_w, glu_b=v_glu_b, pool_w=v_pool_w, pool_scale=v_pool_scale, ffn_w_gate=v_ffn_w_gate,
                 ffn_w_up=v_ffn_w_up, ffn_w_down=v_ffn_w_down)

    bl, seq, _ = x.shape
    n_ctx = ctx.shape[1]
    assert n_ctx == TT and seq % TT == 0 and seq % GRID_W == 0
    depth = w_in.shape[0]
    nc = n_ctx // TT
    ncr = n_ctx // TC
    s_all = n_ctx + seq
    nt = s_all // TT
    t_all = bl * s_all
    n_batch = bl * N_DEV
    my_idx = 4 * lax.axis_index("x") + 2 * lax.axis_index("y") + lax.axis_index("c")
    wc = w_mod.shape[2]

    c_rows = jnp.pad(c, ((0, 8 - bl), (0, 0))) if bl < 8 else c
    rc = c_rows.shape[0]
    (c_all,) = all_gather_rows([c_rows[None]], "gather_c")
    c_all = c_all[0].reshape(N_DEV, rc, D)[:, :bl].reshape(n_batch, D)
    r_act = -(-(n_batch + 1) // 16) * 16
    pre_act = jnp.concatenate([c_all, c_ctx[None, :], jnp.zeros((r_act - n_batch - 1, D), F32)], axis=0)
    act = jax.nn.silu(pre_act)
    b_cols = lax.dynamic_slice_in_dim(b_mod, my_idx * wc, wc, axis=1)[:, None, :]
    mod_cols = mod_forward(act, w_mod, b_cols)
    (mod_all,) = all_gather_rows([mod_cols], "gather_mod")
    mod_all = mod_all.reshape(depth, N_DEV, r_act, wc).transpose(0, 2, 1, 3).reshape(depth, r_act, 6, D)
    mod_lat = lax.dynamic_slice_in_dim(mod_all, my_idx * bl, bl, axis=1)
    mod_ctx = jnp.broadcast_to(mod_all[:, n_batch:n_batch + 1], (depth, bl, 6, D))
    mods = jnp.pad(jnp.stack([mod_ctx, mod_lat], axis=2), ((0, 0), (0, 0), (0, 0), (0, 2), (0, 0)))

    tr = lambda a: jnp.swapaxes(a, 1, 2).astype(BF16)
    shards = dict(w_in=tr(w_in), w_out=w_out.astype(BF16), glu_w=glu_w.astype(BF16), gate=tr(ffn_w_gate),
                  up=tr(ffn_w_up), down=ffn_w_down.astype(BF16))
    mix_keys, ffn_keys = ("w_in", "w_out", "glu_w"), ("gate", "up", "down")
    layer = lambda k, i: shards[k][i:i + 1]
    full = [dict() for _ in range(depth)]
    for k, g in zip(mix_keys, all_gather_rows([layer(k, 0) for k in mix_keys], "gather_mix_weights_0")):
        full[0][k] = g[0]
    first_done = mods[0, 0, 0, 0:1, 0:128] + full[0]["w_in"][0:1, 0:128].astype(F32)
    weights_in_flight = {0: (ffn_keys, gather_start([layer(k, 0) for k in ffn_keys], first_done, "gather_start_ffn_0"))}
    for i in range(1, depth):
        prev_token = weights_in_flight[i - 1][1][4]
        weights_in_flight[i] = (mix_keys + ffn_keys, gather_start([layer(k, i) for k in mix_keys + ffn_keys], prev_token,
                                                                  f"gather_start_layer_{i}"))
    start_token = sum(fl[1][4][0:1, 0:1] for fl in weights_in_flight.values())

    def land_weights(i, after, name):
        keys, (send_sems, recv_sems, sent, lands, _) = weights_in_flight[i]
        for k, g in zip(keys, gather_wait(send_sems, recv_sems, sent, lands, after, name)):
            full[i][k] = g[0]

    band, icnt = _pool_constants()
    seg_p = jnp.asarray(np.kron(np.eye(A_HEADS), np.full((A_W // A_HEADS,) * 2, A_HEADS / A_W)), BF16)
    pe = _sincos_2d(seq // GRID_W, GRID_W, D)
    xs = embed_tokens(x, ctx, pe)

    saved = []
    for i in range(depth):
        mats, ssm_vjp = jax.vjp(ssm_build, ssm_lam_re[i], ssm_lam_im[i], ssm_log_dt[i], ssm_b_re[i], ssm_b_im[i],
                                ssm_c_re[i], ssm_c_im[i], ssm_d[i])
        if i > 0:
            land_weights(i, xs, f"gather_wait_layer_{i}")
        cst = dict(sw=sgu_w[i].astype(BF16),
                   sbias=jnp.repeat(sgu_b[i].T, A_W // A_HEADS, axis=1),
                   seg_p=seg_p, band=band, icnt=icnt, wbd=_block_diag(pool_w[i]).astype(BF16),
                   pscale=pool_scale[i][None, :], glu_w=full[i]["glu_w"], glu_b=glu_b[i][None, :], w_out=full[i]["w_out"],
                   n2=norm_mix_post[i][None, :])
        n1, n3, n4 = norm_mix_pre[i][None, :], norm_ffn_pre[i][None, :], norm_ffn_post[i][None, :]
        if i == 0:
            n1 = n1 + start_token
        za, zu, zp = pre_mix(xs, mods[i], n1, full[i]["w_in"], nc)
        ops = ssm_operators(mats, f"_{i}")
        ys, hps = ssm_forward(zu, mats, ops, ncr)
        x1, m_pre = post_mix(xs, za, zp, ys, mods[i], cst, nc)
        if i == 0:
            land_weights(0, x1, "gather_wait_ffn_0")
        x2, f_pre, gate_b, up_b = ffn_fwd(x1, mods[i], n3, n4, full[i]["gate"], full[i]["up"], full[i]["down"], n_ctx)
        saved.append(dict(xs=xs, za=za, zu=zu, zp=zp, ys=ys, hps=hps, x1=x1, m=m_pre, f=f_pre, gate=gate_b, up=up_b,
                          cst=cst, mats=mats,
                          ops=ops, ssm_vjp=ssm_vjp, n1=n1, n3=n3, n4=n4))
        xs = x2

    dx, loss_parts = loss_head(xs, loss_target, nc)
    loss = lax.psum(jnp.sum(loss_parts[:, :, 0, 0]), ("x", "y", "c"))

    grads = {k: [None] * depth for k in _WEIGHTS}
    big = {k: [None] * depth for k in ("w_in", "w_out", "glu_w", "ffn_w_gate", "ffn_w_up", "ffn_w_down")}
    dmods = [None] * depth
    scatter_groups = (("ffn_w_gate", "ffn_w_up", "ffn_w_down"), ("w_out", "glu_w"), ("w_in",))
    in_flight = []

    def send_grads(i, group):
        flight = scatter_start([big[k][i][None] for k in scatter_groups[group]], f"scatter_start_{i}_{group}")
        in_flight.append((i, group, flight))
        return flight[4][0:1, 0:1]

    flat = lambda a: a.reshape(t_all, a.shape[-1])
    for i in reversed(range(depth)):
        sv = saved[i]
        dx1, h2, df, act_b, dgate, dup, st_f = ffn_bwd(dx, sv["x1"], sv["f"], sv["gate"], sv["up"], mods[i], sv["n3"],
                                                       sv["n4"], full[i]["gate"], full[i]["up"], full[i]["down"], n_ctx)
        big["ffn_w_gate"][i] = tn_matmul(flat(dgate), flat(h2), f"grad_ffn_gate_{i}")
        big["ffn_w_up"][i] = tn_matmul(flat(dup), flat(h2), f"grad_ffn_up_{i}")
        big["ffn_w_down"][i] = tn_matmul(flat(act_b), flat(df), f"grad_ffn_down_{i}")
        cst_i = dict(sv["cst"], n2=sv["cst"]["n2"] + send_grads(i, 0))
        dza, dzp, dys, cat, dm, gg, dr, st_m, dsw, dsb, dwbd = post_mix_bwd(dx1, sv["m"], sv["za"], sv["zp"], sv["ys"],
                                                                            mods[i], cst_i, nc)
        big["w_out"][i] = tn_matmul(flat(cat), flat(dm), f"grad_w_out_{i}")
        big["glu_w"][i] = tn_matmul(flat(gg), flat(dr), f"grad_glu_w_{i}")
        mats_i = dict(sv["mats"], lam_f=sv["mats"]["lam_f"] + send_grads(i, 1))
        dzu, cot = ssm_backward(dys, sv["zu"], sv["hps"], mats_i, sv["ops"], ncr)
        (grads["ssm_lam_re"][i], grads["ssm_lam_im"][i], grads["ssm_log_dt"][i], grads["ssm_b_re"][i],
         grads["ssm_b_im"][i], grads["ssm_c_re"][i], grads["ssm_c_im"][i], grads["ssm_d"][i]) = sv["ssm_vjp"](cot)
        dx, h1, dz, st_p = pre_mix_bwd(dza, dzu, dzp, sv["xs"], dx1, mods[i], sv["n1"], full[i]["w_in"], nc)
        big["w_in"][i] = tn_matmul(flat(dz), flat(h1), f"grad_w_in_{i}")

        tiles = lambda st, row: st[:, :, row, :]
        allsum = lambda st, row: jnp.sum(tiles(st, row), axis=(0, 1))
        grads["norm_mix_pre"][i] = allsum(st_p, 2)
        grads["norm_mix_post"][i] = allsum(st_m, 1)
        grads["norm_ffn_pre"][i] = allsum(st_f, 3)
        grads["norm_ffn_post"][i] = allsum(st_f, 4)
        misc = allsum(st_m, 2)
        grads["glu_b"][i] = misc[:B_W]
        grads["pool_scale"][i] = misc[B_W:B_W + C_W]
        grads["sgu_w"][i] = dsw
        grads["sgu_b"][i] = jnp.sum(dsb.reshape(CHUNK, A_HEADS, A_W // A_HEADS), axis=2).T
        grads["pool_w"][i] = _block_diag_parts(dwbd, len(POOL_WINDOWS))
        mix = (tiles(st_p, 0), tiles(st_p, 1), tiles(st_m, 0))
        d_lat = jnp.stack([jnp.sum(t[:, nc:], axis=1) for t in mix]
                          + [jnp.sum(tiles(st_f, r), axis=1) for r in (0, 1, 2)], axis=1).reshape(bl, 6 * D)
        d_ctx = jnp.concatenate([jnp.sum(t[:, :nc], axis=(0, 1)) for t in mix]
                                + [allsum(st_f, r) for r in (5, 6, 7)]).reshape(1, 6 * D)
        dmods[i] = jnp.concatenate([d_lat, d_ctx, jnp.zeros((8 - (bl + 1) % 8 if (bl + 1) % 8 else 0, 6 * D), F32)],
                                   axis=0)
        token = send_grads(i, 2)
        if i > 0:
            saved[i - 1]["n3"] = saved[i - 1]["n3"] + token
        else:
            dmods[i] = dmods[i] + token
    grad_x = dx[:, n_ctx:, :]

    dmod_local = jnp.stack(dmods)
    rd = dmod_local.shape[1]
    (dmod_all,) = all_gather_rows([dmod_local], "gather_dmod")
    dmod_cols = lax.dynamic_slice_in_dim(dmod_all, my_idx * wc, wc, axis=2).reshape(depth, N_DEV, rd, wc)
    d_lat_all = dmod_cols[:, :, :bl].reshape(depth, n_batch, wc)
    d_ctx_all = dmod_cols[:, 0, bl]
    for p in range(1, N_DEV):
        d_ctx_all = d_ctx_all + dmod_cols[:, p, bl]
    dmod_rows = jnp.concatenate([d_lat_all, d_ctx_all[:, None, :], jnp.zeros((depth, r_act - n_batch - 1, wc), F32)],
                                axis=1)
    dctx_rows = jnp.pad(d_ctx_all[:, None, :], ((0, 0), (0, 7), (0, 0)))
    g_w_mod, dact_ctx = mod_backward(act, dmod_rows, dctx_rows, w_mod)
    sig_c = jax.nn.sigmoid(c_ctx)
    dsilu_c = sig_c * (1.0 + c_ctx * (1.0 - sig_c))
    small_g = {k: (jnp.stack(grads[k]) if grads[k][0] is not None else None) for k in _SMALL}
    small_g["c_ctx"] = jnp.sum(dact_ctx[:, 0, :], axis=0) * dsilu_c
    small_g["b_mod"] = jnp.stack([jnp.sum(dmods[i][:bl + 1], axis=0) for i in range(depth)])

    packed_g = _pack(small_g).astype(BF16)
    rows_s = packed_g.shape[0]
    (gathered,) = all_gather_rows([packed_g[None]], "gather_small_grads")
    res = {k: [None] * 4 for k in _WEIGHTS}

    landed = {}
    for i, group, (send_sems, recv_sems, arrs_thru, lands_thru, _) in in_flight:
        sent, lands = scatter_wait(send_sems, recv_sems, arrs_thru, lands_thru, gathered, f"scatter_wait_{i}_{group}")
        for k, a, z in zip(scatter_groups[group], sent, lands):
            r = a.shape[1] // N_DEV
            own = lax.dynamic_slice_in_dim(a, my_idx * r, r, axis=1)[None]
            landed[k, i] = lax.dynamic_update_slice_in_dim(z, own, my_idx, axis=0)
    for k in big:
        transposed = k in ("w_in", "ffn_w_gate", "ffn_w_up")
        view = (lambda a: jnp.swapaxes(a, 1, 2)) if transposed else (lambda a: a)
        o4 = adamw(view(wts[k]), [landed[k, i] for i in range(depth)], view(mom_m[k]), view(mom_v[k]), "adamw_" + k)
        res[k] = [view(o) for o in o4]
    res["w_mod"] = list(adamw(w_mod, g_w_mod[None], m_w_mod, v_w_mod, "adamw_w_mod"))

    small_w = {k: wts[k] for k in _SMALL}
    outs = adamw(_pack(small_w)[None], gathered.reshape(N_DEV, 1, rows_s, D), _pack({k: mom_m[k] for k in _SMALL})[None],
                 _pack({k: mom_v[k] for k in _SMALL})[None], "adamw_replicated")
    for slot, packed in enumerate(outs):
        un = _unpack(packed[0], small_w)
        for k in _SMALL:
            res[k][slot] = un[k]

    return (loss, grad_x, *[res[k][0] for k in _WEIGHTS], *[res[k][1] for k in _WEIGHTS],
            *[res[k][2] for k in _WEIGHTS], *[res[k][3] for k in _WEIGHTS])
```

```python
import functools
import math

import numpy as np
import jax
import jax.numpy as jnp
from jax import lax
from jax.experimental import pallas as pl
from jax.experimental.pallas import tpu as pltpu

F32 = jnp.float32
BF16 = jnp.bfloat16
HI = lax.Precision.HIGHEST
MESH = pl.DeviceIdType.MESH

D = 1024
D_IN = 1280
D_FF = 2816
A_W = 256
B_W = 512
C_W = 256
A_HEADS = 4
CHUNK = 128
SSM_G = 32
SSM_H = 16
SSM_P = 64
GRID_W = 64
POOL_WINDOWS = (2, 4, 8, 16)
EPS = 1e-6
N_DEV = 8

TT = 256
TC = 8
ROW_W = TC * B_W
QW = ROW_W // 4
GQ = 8
FF_CHUNK = 256
VMEM_LIMIT = 60 * 1024 * 1024

ADAM_LR = 0.001
ADAM_B1 = 0.9
ADAM_B2 = 0.999
ADAM_EPS = 1e-08
ADAM_WD = 0.01
ADAM_STEP = 10


def _cp(sem):
    return pltpu.CompilerParams(dimension_semantics=sem, vmem_limit_bytes=VMEM_LIMIT)


def dot_nn(a, b):
    return jnp.dot(a, b, preferred_element_type=F32)


def dot_nt(a, b):
    return lax.dot_general(a, b, (((1,), (1,)), ((), ())), preferred_element_type=F32)


def dot_tn(a, b):
    return lax.dot_general(a, b, (((0,), (0,)), ((), ())), preferred_element_type=F32)


def split_bf16(x):
    hi = x.astype(BF16)
    lo = (x - hi.astype(F32)).astype(BF16)
    return hi, lo


def gelu(x):
    return jax.nn.gelu(x)


def gelu_grad(x):
    c = math.sqrt(2.0 / math.pi)
    t = jnp.tanh(c * (x + 0.044715 * x * x * x))
    return 0.5 * (1.0 + t) + 0.5 * x * (1.0 - t * t) * c * (1.0 + 3.0 * 0.044715 * x * x)


def rms_stats(x):
    r = lax.rsqrt(jnp.mean(x * x, axis=-1, keepdims=True) + EPS)
    return r, x * r


def rms_bwd(r, xn, dxn):
    return r * (dxn - xn * jnp.mean(dxn * xn, axis=-1, keepdims=True))


def colsum(x):
    return jnp.sum(x, axis=0, keepdims=True)


def lane_group(width, group):
    return lax.broadcasted_iota(jnp.int32, (1, width), 1) // group


def _tile_spec(width):
    return pl.BlockSpec((None, TT, width), lambda b, j: (b, j, 0))


def _mod_spec(nc):
    return pl.BlockSpec((None, None, 8, D), lambda b, j: (b, jnp.where(j >= nc, 1, 0), 0, 0))


def _full_spec(shape):
    zeros = (0,) * len(shape)
    return pl.BlockSpec(shape, lambda b, j: zeros)


def _kind_spec(shape, nc):
    zeros = (0,) * len(shape)
    return pl.BlockSpec((None,) + shape, lambda b, j: (jnp.where(j >= nc, 1, 0),) + zeros)


def _stat_spec():
    return pl.BlockSpec((None, None, 8, D), lambda b, j: (b, j, 0, 0))


def _chunk_spec():
    return pl.BlockSpec((None, TT // TC, ROW_W), lambda b, j: (b, j, 0))


def _rows_to_chunks(val, scratch, out_ref):
    for cb in range(B_W // 128):
        scratch[cb] = val[:, cb * 128:(cb + 1) * 128]
    for s in range(TC):
        for cb in range(B_W // 128):
            lo = cb * QW + s * 128
            out_ref[:, lo:lo + 128] = scratch.at[cb][pl.ds(s, TT // TC, stride=TC), :]


def _chunks_to_rows(in_ref, scratch):
    for s in range(TC):
        for cb in range(B_W // 128):
            lo = cb * QW + s * 128
            scratch.at[cb][pl.ds(s, TT // TC, stride=TC), :] = in_ref[:, lo:lo + 128]
    return jnp.concatenate([scratch[cb] for cb in range(B_W // 128)], axis=1)


def _chunk_scratch():
    return pltpu.VMEM((B_W // 128, TT, 128), F32)


def embed_tokens(x, ctx, pe):
    bl, seq, _ = x.shape
    nc = ctx.shape[1] // TT
    nt = nc + seq // TT

    def body(ctx_ref, x_ref, pe_ref, o_ref):
        j = pl.program_id(1)

        @pl.when(j < nc)
        def _():
            o_ref[...] = ctx_ref[...]

        @pl.when(j >= nc)
        def _():
            o_ref[...] = x_ref[...] + pe_ref[...]

    return pl.pallas_call(
        body, name="embed_tokens", grid=(bl, nt),
        in_specs=[pl.BlockSpec((None, TT, D), lambda b, j: (b, jnp.minimum(j, nc - 1), 0)),
                  pl.BlockSpec((None, TT, D), lambda b, j: (b, jnp.maximum(j - nc, 0), 0)),
                  pl.BlockSpec((TT, D), lambda b, j: (jnp.maximum(j - nc, 0), 0))],
        out_specs=_tile_spec(D),
        out_shape=jax.ShapeDtypeStruct((bl, nt * TT, D), F32),
        compiler_params=_cp(("arbitrary", "arbitrary")),
    )(ctx, x, pe)


def pre_mix(xs, mod, n1, w_int, nc):
    bl, s, _ = xs.shape

    def body(x_ref, mod_ref, n_ref, w_ref, za_ref, zu_ref, zp_ref, u_s):
        r, xn = rms_stats(x_ref[...])
        h = xn * n_ref[...] * (1.0 + mod_ref[1:2, :]) + mod_ref[0:1, :]
        z = dot_nt(h.astype(BF16), w_ref[...])
        za_ref[...] = z[:, :2 * A_W]
        _rows_to_chunks(z[:, 2 * A_W:2 * A_W + B_W], u_s, zu_ref)
        zp_ref[...] = z[:, 2 * A_W + B_W:]

    return pl.pallas_call(
        body, name="pre_mix", grid=(bl, s // TT),
        in_specs=[_tile_spec(D), _mod_spec(nc), _full_spec((1, D)), _full_spec((D_IN, D))],
        out_specs=[_tile_spec(2 * A_W), _chunk_spec(), _tile_spec(C_W)],
        out_shape=[jax.ShapeDtypeStruct((bl, s, 2 * A_W), F32), jax.ShapeDtypeStruct((bl, s // TC, ROW_W), F32),
                   jax.ShapeDtypeStruct((bl, s, C_W), F32)],
        scratch_shapes=[_chunk_scratch()],
        compiler_params=_cp(("arbitrary", "arbitrary")),
    )(xs, mod, n1, w_int)


def _seg_mean(x, seg_p):
    hi, lo = split_bf16(x)
    return dot_nn(hi, seg_p) + dot_nn(lo, seg_p)


def _sgu_forward(za, sw_ref, sbias, seg_p):
    ge = gelu(za)
    u, v = ge[:, :A_W], ge[:, A_W:]
    dv = v - _seg_mean(v, seg_p)
    rs = lax.rsqrt(_seg_mean(dv * dv, seg_p) + EPS)
    vn = dv * rs
    head = lane_group(A_W, A_W // A_HEADS)
    parts = []
    for c2 in range(TT // CHUNK):
        vb = vn[c2 * CHUNK:(c2 + 1) * CHUNK].astype(BF16)
        sc = sbias
        for h in range(A_HEADS):
            sc = sc + jnp.where(head == h, dot_nn(sw_ref[h], vb), 0.0)
        parts.append(sc)
    sg = jnp.concatenate(parts, axis=0)
    return u * sg, (u, vn, rs, sg)


def _pool_forward(zp, band_ref, icnt, wbd, pscale):
    hi, lo = split_bf16(zp)
    grp = lane_group(C_W, C_W // len(POOL_WINDOWS))
    q = jnp.zeros_like(zp)
    for i in range(len(POOL_WINDOWS)):
        t = dot_nn(band_ref[i], hi) + dot_nn(band_ref[i], lo)
        q = jnp.where(grp == i, t, q)
    q = q * icnt - zp
    o = dot_nn(q.astype(BF16), wbd)
    return o * pscale, (q, o)


def _glu_forward(y, glu_w, glu_b):
    g = gelu(y)
    sg = jax.nn.sigmoid(dot_nn(g.astype(BF16), glu_w) + glu_b)
    return g * sg, (g, sg)


_MIX_CONST_SHAPES = dict(sw=(A_HEADS, CHUNK, CHUNK), sbias=(CHUNK, A_W), seg_p=(A_W, A_W), wbd=(C_W, C_W),
                         pscale=(1, C_W), glu_w=(B_W, B_W), glu_b=(1, B_W), w_out=(D, D), n2=(1, D))


def _mix_const_specs(nc):
    return ([_full_spec(_MIX_CONST_SHAPES[k]) for k in ("sw", "sbias", "seg_p")]
            + [_kind_spec((len(POOL_WINDOWS), TT, TT), nc), _kind_spec((TT, C_W), nc)]
            + [_full_spec(_MIX_CONST_SHAPES[k]) for k in ("wbd", "pscale", "glu_w", "glu_b", "w_out", "n2")])


def _mix_const_args(cst):
    return [cst[k] for k in ("sw", "sbias", "seg_p", "band", "icnt", "wbd", "pscale", "glu_w", "glu_b", "w_out", "n2")]


def post_mix(xs, za, zp, ys, mod, cst, nc):
    bl, s, _ = xs.shape

    def body(x_ref, za_ref, zp_ref, y_ref, mod_ref, sw_ref, sbias_ref, seg_ref, band_ref, icnt_ref, wbd_ref,
             ps_ref, gw_ref, gb_ref, wo_ref, n2_ref, x1_ref, m_ref, y_s):
        a, _ = _sgu_forward(za_ref[...], sw_ref, sbias_ref[...], seg_ref[...])
        p, _ = _pool_forward(zp_ref[...], band_ref, icnt_ref[...], wbd_ref[...], ps_ref[...])
        sm, _ = _glu_forward(_chunks_to_rows(y_ref, y_s), gw_ref[...], gb_ref[...])
        cat = jnp.concatenate([a, sm, p], axis=1).astype(BF16)
        m = dot_nn(cat, wo_ref[...])
        _, mn = rms_stats(m)
        m_ref[...] = m
        x1_ref[...] = x_ref[...] + mod_ref[2:3, :] * (mn * n2_ref[...])

    return pl.pallas_call(
        body, name="post_mix", grid=(bl, s // TT),
        in_specs=[_tile_spec(D), _tile_spec(2 * A_W), _tile_spec(C_W), _chunk_spec(), _mod_spec(nc)]
        + _mix_const_specs(nc),
        out_specs=[_tile_spec(D), _tile_spec(D)],
        out_shape=[jax.ShapeDtypeStruct((bl, s, D), F32), jax.ShapeDtypeStruct((bl, s, D), F32)],
        scratch_shapes=[_chunk_scratch()],
        compiler_params=_cp(("arbitrary", "arbitrary")),
    )(xs, za, zp, ys, mod, *_mix_const_args(cst))


def post_mix_bwd(dx1, m, za, zp, ys, mod, cst, nc):
    bl, s, _ = m.shape
    nt = s // TT

    def body(dx_ref, m_ref, za_ref, zp_ref, y_ref, mod_ref, sw_ref, sbias_ref, seg_ref, band_ref, icnt_ref,
             wbd_ref, ps_ref, gw_ref, gb_ref, wo_ref, n2_ref,
             dza_ref, dzp_ref, dy_ref, cat_ref, dm_ref, gg_ref, dr_ref, st_ref, dsw_ref, dsb_ref, dwbd_ref, y_s):
        first = jnp.logical_and(pl.program_id(0) == 0, pl.program_id(1) == 0)

        @pl.when(first)
        def _():
            dsw_ref[...] = jnp.zeros_like(dsw_ref)
            dsb_ref[...] = jnp.zeros_like(dsb_ref)
            dwbd_ref[...] = jnp.zeros_like(dwbd_ref)

        seg_p = seg_ref[...]
        za = za_ref[...]
        zp_v = zp_ref[...]
        yv = _chunks_to_rows(y_ref, y_s)
        a, (u, vn, rs, sg) = _sgu_forward(za, sw_ref, sbias_ref[...], seg_p)
        p, (q, o) = _pool_forward(zp_v, band_ref, icnt_ref[...], wbd_ref[...], ps_ref[...])
        sm, (g, sig) = _glu_forward(yv, gw_ref[...], gb_ref[...])
        cat_ref[...] = jnp.concatenate([a, sm, p], axis=1).astype(BF16)

        dx = dx_ref[...]
        g1 = mod_ref[2:3, :]
        n2 = n2_ref[...]
        mv = m_ref[...]
        rm, mn = rms_stats(mv)
        st_ref[...] = jnp.zeros_like(st_ref)
        st_ref[0:1, :] = colsum(dx * (mn * n2))
        st_ref[1:2, :] = colsum(dx * g1 * mn)
        dm = rms_bwd(rm, mn, dx * g1 * n2)
        dmb = dm.astype(BF16)
        dm_ref[...] = dmb
        dcat = dot_nt(dmb, wo_ref[...])
        da, dsm, dp = dcat[:, :A_W], dcat[:, A_W:A_W + B_W], dcat[:, A_W + B_W:]

        du = da * sg
        dsv = da * u
        head = lane_group(A_W, A_W // A_HEADS)
        dvn_parts = []
        dsb_acc = jnp.zeros((CHUNK, A_W), F32)
        for c2 in range(TT // CHUNK):
            dsc = dsv[c2 * CHUNK:(c2 + 1) * CHUNK]
            dsc_b = dsc.astype(BF16)
            vb = vn[c2 * CHUNK:(c2 + 1) * CHUNK].astype(BF16)
            dsb_acc = dsb_acc + dsc
            dvn_c = jnp.zeros((CHUNK, A_W), F32)
            for h in range(A_HEADS):
                dsw_ref[h] += dot_nt(jnp.where(head == h, dsc, 0.0).astype(BF16), vb)
                dvn_c = dvn_c + jnp.where(head == h, dot_tn(sw_ref[h], dsc_b), 0.0)
            dvn_parts.append(dvn_c)
        dsb_ref[...] += dsb_acc
        dvn = jnp.concatenate(dvn_parts, axis=0)
        dv = rs * (dvn - _seg_mean(dvn, seg_p) - vn * _seg_mean(dvn * vn, seg_p))
        dza_ref[...] = jnp.concatenate([du, dv], axis=1) * gelu_grad(za)

        ps = ps_ref[...]
        do = dp * ps
        dps = colsum(dp * o)
        dob = do.astype(BF16)
        dwbd_ref[...] += dot_tn(q.astype(BF16), dob)
        dq = dot_nt(dob, wbd_ref[...])
        hi, lo = split_bf16(dq * icnt_ref[...])
        grp = lane_group(C_W, C_W // len(POOL_WINDOWS))
        dzp = -dq
        for i in range(len(POOL_WINDOWS)):
            t = dot_tn(band_ref[i], hi) + dot_tn(band_ref[i], lo)
            dzp = dzp + jnp.where(grp == i, t, 0.0)
        dzp_ref[...] = dzp

        dr = dsm * g * sig * (1.0 - sig)
        drb = dr.astype(BF16)
        dr_ref[...] = drb
        gg_ref[...] = g.astype(BF16)
        dg = dsm * sig + dot_nt(drb, gw_ref[...])
        _rows_to_chunks(dg * gelu_grad(yv), y_s, dy_ref)
        st_ref[2:3, :] = jnp.concatenate([colsum(dr), dps, jnp.zeros((1, D - B_W - C_W), F32)], axis=1)

    acc = lambda shape: pl.BlockSpec(shape, lambda b, j: (0,) * len(shape))
    return pl.pallas_call(
        body, name="post_mix_bwd", grid=(bl, nt),
        in_specs=[_tile_spec(D), _tile_spec(D), _tile_spec(2 * A_W), _tile_spec(C_W), _chunk_spec(), _mod_spec(nc)]
        + _mix_const_specs(nc),
        out_specs=[_tile_spec(2 * A_W), _tile_spec(C_W), _chunk_spec(), _tile_spec(D), _tile_spec(D),
                   _tile_spec(B_W), _tile_spec(B_W), _stat_spec(),
                   acc((A_HEADS, CHUNK, CHUNK)), acc((CHUNK, A_W)), acc((C_W, C_W))],
        out_shape=[jax.ShapeDtypeStruct((bl, s, 2 * A_W), F32), jax.ShapeDtypeStruct((bl, s, C_W), F32),
                   jax.ShapeDtypeStruct((bl, s // TC, ROW_W), F32), jax.ShapeDtypeStruct((bl, s, D), BF16),
                   jax.ShapeDtypeStruct((bl, s, D), BF16), jax.ShapeDtypeStruct((bl, s, B_W), BF16),
                   jax.ShapeDtypeStruct((bl, s, B_W), BF16), jax.ShapeDtypeStruct((bl, nt, 8, D), F32),
                   jax.ShapeDtypeStruct((A_HEADS, CHUNK, CHUNK), F32), jax.ShapeDtypeStruct((CHUNK, A_W), F32),
                   jax.ShapeDtypeStruct((C_W, C_W), F32)],
        scratch_shapes=[_chunk_scratch()],
        compiler_params=_cp(("arbitrary", "arbitrary")),
    )(dx1, m, za, zp, ys, mod, *_mix_const_args(cst))


def pre_mix_bwd(dza, dzu, dzp, xs, dxres, mod, n1, w_int, nc):
    bl, s, _ = xs.shape
    nt = s // TT

    def body(dza_ref, dzu_ref, dzp_ref, x_ref, dres_ref, mod_ref, n_ref, w_ref, dx_ref, h_ref, dz_ref, st_ref, u_s):
        dz = jnp.concatenate([dza_ref[...], _chunks_to_rows(dzu_ref, u_s), dzp_ref[...]], axis=1).astype(BF16)
        dz_ref[...] = dz
        dh = dot_nn(dz, w_ref[...])
        r, xn = rms_stats(x_ref[...])
        n1v = n_ref[...]
        sc = mod_ref[1:2, :]
        xg = xn * n1v
        h_ref[...] = (xg * (1.0 + sc) + mod_ref[0:1, :]).astype(BF16)
        dyv = dh * (1.0 + sc)
        st_ref[...] = jnp.zeros_like(st_ref)
        st_ref[0:1, :] = colsum(dh)
        st_ref[1:2, :] = colsum(dh * xg)
        st_ref[2:3, :] = colsum(dyv * xn)
        dx_ref[...] = dres_ref[...] + rms_bwd(r, xn, dyv * n1v)

    return pl.pallas_call(
        body, name="pre_mix_bwd", grid=(bl, nt),
        in_specs=[_tile_spec(2 * A_W), _chunk_spec(), _tile_spec(C_W), _tile_spec(D), _tile_spec(D), _mod_spec(nc),
                  _full_spec((1, D)), _full_spec((D_IN, D))],
        out_specs=[_tile_spec(D), _tile_spec(D), _tile_spec(D_IN), _stat_spec()],
        out_shape=[jax.ShapeDtypeStruct((bl, s, D), F32), jax.ShapeDtypeStruct((bl, s, D), BF16),
                   jax.ShapeDtypeStruct((bl, s, D_IN), BF16), jax.ShapeDtypeStruct((bl, nt, 8, D), F32)],
        scratch_shapes=[_chunk_scratch()],
        compiler_params=_cp(("arbitrary", "arbitrary")),
    )(dza, dzu, dzp, xs, dxres, mod, n1, w_int)


def _ffn_tile(s):
    return 768 if s % 768 == 0 else TT


def _slabs(v, n_ctx):
    return (v,) if v.shape[0] == n_ctx else (v[:n_ctx], v[n_ctx:])


def _mod_rows(mod_ref, j, row):
    lat = mod_ref[1, row:row + 1, :]
    return jnp.where(j == 0, mod_ref[0, row:row + 1, :], lat), lat


def _by_slab(fn, n_ctx, *vals_and_rows):
    outs = []
    for s in range(len(_slabs(next(v for v in vals_and_rows if not isinstance(v, tuple)), n_ctx))):
        outs.append(fn(*[v[s] if isinstance(v, tuple) else _slabs(v, n_ctx)[s] for v in vals_and_rows]))
    return outs[0] if len(outs) == 1 else jnp.concatenate(outs, axis=0)


def _split_sum(v, n_ctx, j, st_ref, row):
    parts = [colsum(p) for p in _slabs(v, n_ctx)]
    first_is_ctx = j == 0
    rest = parts[1] if len(parts) > 1 else jnp.zeros_like(parts[0])
    st_ref[row:row + 1, :] = rest + jnp.where(first_is_ctx, 0.0, parts[0])
    st_ref[row + 5:row + 6, :] = jnp.where(first_is_ctx, parts[0], 0.0)


def ffn_fwd(x1, mod, n3, n4, wg_t, wu_t, wd, n_ctx):
    bl, s, _ = x1.shape
    tf = _ffn_tile(s)
    nk = D_FF // FF_CHUNK
    tile = pl.BlockSpec((None, tf, D), lambda b, j, k: (b, j, 0))
    modspec = pl.BlockSpec((None, 2, 8, D), lambda b, j, k: (b, 0, 0, 0))
    vec = pl.BlockSpec((1, D), lambda b, j, k: (0, 0))
    wspec = pl.BlockSpec((FF_CHUNK, D), lambda b, j, k: (k, 0))
    ftile = pl.BlockSpec((None, tf, FF_CHUNK), lambda b, j, k: (b, j, k))

    def body(x_ref, mod_ref, n3_ref, n4_ref, wg_ref, wu_ref, wd_ref, x2_ref, f_ref, gate_ref, up_ref, h_s, acc_s):
        j, k = pl.program_id(1), pl.program_id(2)

        @pl.when(k == 0)
        def _():
            _, xn = rms_stats(x_ref[...])
            n3 = n3_ref[...]
            h_s[...] = _by_slab(lambda v, sh, sc: (v * n3 * (1.0 + sc) + sh).astype(BF16), n_ctx, xn,
                                _mod_rows(mod_ref, j, 3), _mod_rows(mod_ref, j, 4))
            acc_s[...] = jnp.zeros_like(acc_s)

        h = h_s[...]
        gate = dot_nt(h, wg_ref[...])
        up = dot_nt(h, wu_ref[...])
        gate_ref[...] = gate.astype(BF16)
        up_ref[...] = up.astype(BF16)
        act = (gate * jax.nn.sigmoid(gate)) * up
        acc_s[...] += dot_nn(act.astype(BF16), wd_ref[...])

        @pl.when(k == nk - 1)
        def _():
            f = acc_s[...]
            f_ref[...] = f
            _, fn = rms_stats(f)
            n4 = n4_ref[...]
            x2_ref[...] = _by_slab(lambda xv, fv, g2: xv + g2 * (fv * n4), n_ctx, x_ref[...], fn, _mod_rows(mod_ref, j, 5))

    return pl.pallas_call(
        body, name="ffn_fwd", grid=(bl, s // tf, nk),
        in_specs=[tile, modspec, vec, vec, wspec, wspec, wspec],
        out_specs=[tile, tile, ftile, ftile],
        out_shape=[jax.ShapeDtypeStruct((bl, s, D), F32), jax.ShapeDtypeStruct((bl, s, D), F32),
                   jax.ShapeDtypeStruct((bl, s, D_FF), BF16), jax.ShapeDtypeStruct((bl, s, D_FF), BF16)],
        scratch_shapes=[pltpu.VMEM((tf, D), BF16), pltpu.VMEM((tf, D), F32)],
        compiler_params=_cp(("arbitrary", "arbitrary", "arbitrary")),
    )(x1, mod, n3, n4, wg_t, wu_t, wd)


def ffn_bwd(dx2, x1, f, gate_b, up_b, mod, n3, n4, wg_t, wu_t, wd, n_ctx):
    bl, s, _ = x1.shape
    tf = _ffn_tile(s)
    nt = s // tf
    nk = D_FF // FF_CHUNK
    tile = pl.BlockSpec((None, tf, D), lambda b, j, k: (b, j, 0))
    ftile = pl.BlockSpec((None, tf, FF_CHUNK), lambda b, j, k: (b, j, jnp.minimum(k, nk - 1)))
    modspec = pl.BlockSpec((None, 2, 8, D), lambda b, j, k: (b, 0, 0, 0))
    vec = pl.BlockSpec((1, D), lambda b, j, k: (0, 0))
    wspec = pl.BlockSpec((FF_CHUNK, D), lambda b, j, k: (jnp.minimum(k, nk - 1), 0))
    wprev = pl.BlockSpec((FF_CHUNK, D), lambda b, j, k: (jnp.maximum(k - 1, 0), 0))
    stat = pl.BlockSpec((None, None, 8, D), lambda b, j, k: (b, j, 0, 0))

    def body(dx_ref, x_ref, f_ref, gate_ref, up_ref, mod_ref, n3_ref, n4_ref, wd_ref, wgp_ref, wup_ref,
             dx1_ref, h_ref, df_ref, act_ref, dgate_ref, dup_ref, st_ref, df_s, acc_s, dgate_s, dup_s):
        j, k = pl.program_id(1), pl.program_id(2)

        @pl.when(k == 0)
        def _():
            dx = dx_ref[...]
            n4 = n4_ref[...]
            rf, fn = rms_stats(f_ref[...])
            st_ref[...] = jnp.zeros_like(st_ref)
            _split_sum(dx * (fn * n4), n_ctx, j, st_ref, 2)
            dxg = _by_slab(lambda dv, g2: dv * g2, n_ctx, dx, _mod_rows(mod_ref, j, 5))
            st_ref[4:5, :] = colsum(dxg * fn)
            df = rms_bwd(rf, fn, dxg * n4).astype(BF16)
            df_s[...] = df
            df_ref[...] = df
            _, xn = rms_stats(x_ref[...])
            n3 = n3_ref[...]
            h_ref[...] = _by_slab(lambda v, sh, sc: (v * n3 * (1.0 + sc) + sh).astype(BF16), n_ctx, xn,
                                  _mod_rows(mod_ref, j, 3), _mod_rows(mod_ref, j, 4))
            acc_s[...] = jnp.zeros_like(acc_s)
            dgate_s[1] = jnp.zeros((tf, FF_CHUNK), BF16)
            dup_s[1] = jnp.zeros((tf, FF_CHUNK), BF16)

        prev = (k + 1) % 2
        acc_s[...] += dot_nn(dgate_s[prev], wgp_ref[...]) + dot_nn(dup_s[prev], wup_ref[...])
        gate = gate_ref[...].astype(F32)
        up = up_ref[...].astype(F32)
        sg = jax.nn.sigmoid(gate)
        silu = gate * sg
        dact = dot_nt(df_s[...], wd_ref[...])
        act_ref[...] = (silu * up).astype(BF16)
        dgate = (dact * up * (sg * (1.0 + gate * (1.0 - sg)))).astype(BF16)
        dup = (dact * silu).astype(BF16)
        dgate_ref[...] = dgate
        dup_ref[...] = dup
        dgate_s[k % 2] = dgate
        dup_s[k % 2] = dup

        @pl.when(k == nk)
        def _():
            n3 = n3_ref[...]
            sc_first, sc_lat = _mod_rows(mod_ref, j, 4)
            sub = 128

            def rows_block(i, carry):
                rows = pl.ds(pl.multiple_of(i * sub, sub), sub)
                dh = acc_s[rows, :]
                r, xn = rms_stats(x_ref[rows, :])
                first = i < n_ctx // sub
                dyv = dh * (1.0 + jnp.where(first, sc_first, sc_lat))
                is_ctx = jnp.logical_and(first, j == 0)
                for row, part in ((0, colsum(dh)), (1, colsum(dh * (xn * n3)))):
                    st_ref[row:row + 1, :] += jnp.where(is_ctx, 0.0, part)
                    st_ref[row + 5:row + 6, :] += jnp.where(is_ctx, part, 0.0)
                st_ref[3:4, :] += colsum(dyv * xn)
                dx1_ref[rows, :] = dx_ref[rows, :] + rms_bwd(r, xn, dyv * n3)
                return carry

            lax.fori_loop(0, tf // sub, rows_block, 0)

    return pl.pallas_call(
        body, name="ffn_bwd", grid=(bl, nt, nk + 1),
        in_specs=[tile, tile, tile, ftile, ftile, modspec, vec, vec, wspec, wprev, wprev],
        out_specs=[tile, tile, tile, ftile, ftile, ftile, stat],
        out_shape=[jax.ShapeDtypeStruct((bl, s, D), F32), jax.ShapeDtypeStruct((bl, s, D), BF16),
                   jax.ShapeDtypeStruct((bl, s, D), BF16), jax.ShapeDtypeStruct((bl, s, D_FF), BF16),
                   jax.ShapeDtypeStruct((bl, s, D_FF), BF16), jax.ShapeDtypeStruct((bl, s, D_FF), BF16),
                   jax.ShapeDtypeStruct((bl, nt, 8, D), F32)],
        scratch_shapes=[pltpu.VMEM((tf, D), BF16), pltpu.VMEM((tf, D), F32),
                        pltpu.VMEM((2, tf, FF_CHUNK), BF16), pltpu.VMEM((2, tf, FF_CHUNK), BF16)],
        compiler_params=_cp(("arbitrary", "arbitrary", "arbitrary")),
    )(dx2, x1, f, gate_b, up_b, mod, n3, n4, wd, wg_t, wu_t)


def loss_head(xs, target, nc):
    bl, s, _ = xs.shape
    nt = s // TT

    def body(x_ref, t_ref, dx_ref, l_ref):
        j = pl.program_id(1)

        @pl.when(j < nc)
        def _():
            dx_ref[...] = jnp.zeros_like(dx_ref)
            l_ref[...] = jnp.zeros_like(l_ref)

        @pl.when(j >= nc)
        def _():
            e = x_ref[...] - t_ref[...]
            dx_ref[...] = e * (1.0 / D)
            tok = jnp.mean(e * e, axis=-1, keepdims=True)
            l_ref[...] = jnp.zeros_like(l_ref) + 0.5 * jnp.sum(tok, axis=0, keepdims=True)

    return pl.pallas_call(
        body, name="loss_head", grid=(bl, nt),
        in_specs=[_tile_spec(D), pl.BlockSpec((None, TT, D), lambda b, j: (b, jnp.maximum(j - nc, 0), 0))],
        out_specs=[_tile_spec(D), pl.BlockSpec((None, None, 8, 128), lambda b, j: (b, j, 0, 0))],
        out_shape=[jax.ShapeDtypeStruct((bl, s, D), F32), jax.ShapeDtypeStruct((bl, nt, 8, 128), F32)],
        compiler_params=_cp(("arbitrary", "arbitrary")),
    )(xs, target)


def tn_matmul(a, b, name):
    t, ka = a.shape
    n = b.shape[1]
    tk = ka
    tt = next(x for x in (1024, 512, 256) if t % x == 0)
    nsteps = t // tt

    def body(a_ref, b_ref, o_ref, acc_s):
        @pl.when(pl.program_id(1) == 0)
        def _():
            acc_s[...] = jnp.zeros_like(acc_s)

        acc_s[...] += dot_tn(a_ref[...], b_ref[...])

        @pl.when(pl.program_id(1) == nsteps - 1)
        def _():
            o_ref[...] = acc_s[...].astype(BF16)

    return pl.pallas_call(
        body, name=name, grid=(ka // tk, nsteps),
        in_specs=[pl.BlockSpec((tt, tk), lambda i, s: (s, i)), pl.BlockSpec((tt, n), lambda i, s: (s, 0))],
        out_specs=pl.BlockSpec((tk, n), lambda i, s: (i, 0)),
        out_shape=jax.ShapeDtypeStruct((ka, n), BF16),
        scratch_shapes=[pltpu.VMEM((tk, n), F32)],
        compiler_params=_cp(("arbitrary", "arbitrary")),
    )(a, b)


def qmm(terms, name):
    r = terms[0][0].shape[0]
    rt = r // 2 if r % 16 == 0 and r >= 512 else r
    n = len(terms)

    def body(*refs):
        acc = None
        for k in range(n):
            y = dot_nn(refs[2 * k][...].astype(BF16), refs[2 * k + 1][...])
            acc = y if acc is None else acc + y
        refs[2 * n][...] = acc

    row = pl.BlockSpec((rt, QW), lambda q, i: (i, q))
    wspec = pl.BlockSpec((None, QW, QW), lambda q, i: (q, 0, 0))
    return pl.pallas_call(
        body, name=name, grid=(4, r // rt), in_specs=[row, wspec] * n, out_specs=row,
        out_shape=jax.ShapeDtypeStruct((r, ROW_W), F32),
        compiler_params=_cp(("arbitrary", "arbitrary")),
    )(*[x for term in terms for x in term])


def _same_group(rows, cols, row_group, col_group):
    ri = jnp.bitwise_and(lax.broadcasted_iota(jnp.int32, (rows, cols), 0) // row_group, GQ - 1)
    ci = jnp.bitwise_and(lax.broadcasted_iota(jnp.int32, (rows, cols), 1) // col_group, GQ - 1)
    return ri == ci


def _spread_matrix():
    m = np.zeros((2 * SSM_P, QW), np.float32)
    for reim in range(2):
        for g in range(GQ):
            for p in range(SSM_P):
                m[reim * SSM_P + p, reim * (QW // 2) + g * SSM_P + p] = 1.0
    return jnp.asarray(m, BF16)


def assemble_ts(v, name):
    def body(v_ref, f_ref, big_ref, bigt_ref):
        keep = _same_group(GQ * SSM_H, QW, SSM_H, SSM_P)
        for e in range(TC):
            hi, lo = split_bf16(v_ref[e])
            t = jnp.where(keep, dot_nn(hi, f_ref[...]) + dot_nn(lo, f_ref[...]), 0.0)
            big_ref[e * 128:(e + 1) * 128, :] = t.astype(BF16)
            bigt_ref[:, e * 128:(e + 1) * 128] = t.T.astype(BF16)

    return pl.pallas_call(
        body, name=name, grid=(4,),
        in_specs=[pl.BlockSpec((None, TC, 128, 128), lambda q: (q, 0, 0, 0)),
                  pl.BlockSpec((128, QW), lambda q: (0, 0))],
        out_specs=[pl.BlockSpec((None, QW, QW), lambda q: (q, 0, 0))] * 2,
        out_shape=[jax.ShapeDtypeStruct((4, QW, QW), BF16), jax.ShapeDtypeStruct((4, QW, QW), BF16)],
        compiler_params=_cp(("arbitrary",)),
    )(v, _spread_matrix())


def assemble_tt(lags, name):
    def body(l_ref, m_ref, mt_ref):
        blocks = [l_ref[n] for n in range(2 * TC - 1)]
        flipped = [b.T.astype(BF16) for b in blocks]
        blocks = [b.astype(BF16) for b in blocks]
        for s in range(TC):
            for t in range(TC):
                m_ref[s * 128:(s + 1) * 128, t * 128:(t + 1) * 128] = blocks[t - s + TC - 1]
                mt_ref[t * 128:(t + 1) * 128, s * 128:(s + 1) * 128] = flipped[t - s + TC - 1]

    return pl.pallas_call(
        body, name=name, grid=(4,),
        in_specs=[pl.BlockSpec((None, 2 * TC - 1, 128, 128), lambda q: (q, 0, 0, 0))],
        out_specs=[pl.BlockSpec((None, QW, QW), lambda q: (q, 0, 0))] * 2,
        out_shape=[jax.ShapeDtypeStruct((4, QW, QW), BF16)] * 2,
        compiler_params=_cp(("arbitrary",)),
    )(lags)


def _qtn_call(body, a, b, out_shape, out_block, extra, name):
    r = a.shape[0]
    col = pl.BlockSpec((r, QW), lambda q: (0, q))
    return pl.pallas_call(
        body, name=name, grid=(4,),
        in_specs=[col, col] + [pl.BlockSpec(x.shape, lambda q: (0, 0)) for x in extra],
        out_specs=pl.BlockSpec((None,) + out_block, lambda q: (q,) + (0,) * len(out_block)),
        out_shape=jax.ShapeDtypeStruct((4,) + out_block, F32),
        compiler_params=_cp(("arbitrary",)),
    )(a, b, *extra)


def qtn_ts(a, b, name):
    def body(a_ref, b_ref, f_ref, o_ref):
        full = dot_tn(a_ref[...].astype(BF16), b_ref[...].astype(BF16))
        keep = _same_group(GQ * SSM_H, QW, SSM_H, SSM_P)
        for e in range(TC):
            hi, lo = split_bf16(jnp.where(keep, full[e * 128:(e + 1) * 128, :], 0.0))
            o_ref[e] = dot_nt(hi, f_ref[...]) + dot_nt(lo, f_ref[...])

    return _qtn_call(body, a, b, None, (TC, 128, 128), [_spread_matrix()], name)


def qtn_tt(a, b, name):
    def body(a_ref, b_ref, o_ref):
        full = dot_tn(a_ref[...].astype(BF16), b_ref[...].astype(BF16))
        for lag in range(-(TC - 1), TC):
            acc = None
            for s in range(TC):
                t = s + lag
                if 0 <= t < TC:
                    blk = full[s * 128:(s + 1) * 128, t * 128:(t + 1) * 128]
                    acc = blk if acc is None else acc + blk
            o_ref[lag + TC - 1] = acc

    return _qtn_call(body, a, b, None, (2 * TC - 1, 128, 128), [], name)


def _scan_row(i, rb, ncr, reverse):
    if not reverse:
        return i
    return jnp.where(i < ncr, ncr - 1 - i, rb - 1 - (i - ncr))


def _swap_re_im(h):
    half = QW // 2
    return jnp.concatenate([h[:, q * QW + (1 - k) * half:q * QW + (2 - k) * half] for q in range(4) for k in range(2)],
                           axis=1)


def chunk_scan(xs, lam_ab, ncr, reverse, name):
    bl, rb, _ = xs.shape

    def body(x_ref, l_ref, hp_ref):
        la, lb = l_ref[0:1, :], l_ref[1:2, :]

        def step(i, h):
            row = _scan_row(i, rb, ncr, reverse)
            hp_ref[pl.ds(row, 1), :] = h
            return la * h + lb * _swap_re_im(h) + x_ref[pl.ds(row, 1), :]

        lax.fori_loop(0, rb, step, jnp.zeros((1, ROW_W), F32))

    blk = pl.BlockSpec((None, rb, ROW_W), lambda b: (b, 0, 0))
    return pl.pallas_call(
        body, name=name, grid=(bl,),
        in_specs=[blk, pl.BlockSpec((8, ROW_W), lambda b: (0, 0))], out_specs=blk,
        out_shape=jax.ShapeDtypeStruct((bl, rb, ROW_W), F32),
        compiler_params=_cp(("arbitrary",)),
    )(xs, lam_ab)


def chunk_scan_bwd(dhp, hp, lam_ab, ncr, reverse, name):
    bl, rb, _ = dhp.shape

    def body(d_ref, hp_ref, l_ref, g_ref, dl_ref):
        la, lb = l_ref[0:1, :], l_ref[1:2, :]

        dl_ref[...] = jnp.zeros_like(dl_ref)

        def step(n, g):
            row = _scan_row(rb - 1 - n, rb, ncr, reverse)
            g_ref[pl.ds(row, 1), :] = g
            pv = hp_ref[pl.ds(row, 1), :]
            dl_ref[0:1, :] += g * pv
            dl_ref[1:2, :] += g * _swap_re_im(pv)
            return d_ref[pl.ds(row, 1), :] + la * g + _swap_re_im(lb * g)

        lax.fori_loop(0, rb, step, jnp.zeros((1, ROW_W), F32))

    blk = pl.BlockSpec((None, rb, ROW_W), lambda b: (b, 0, 0))
    return pl.pallas_call(
        body, name=name, grid=(bl,),
        in_specs=[blk, blk, pl.BlockSpec((8, ROW_W), lambda b: (0, 0))],
        out_specs=[blk, pl.BlockSpec((None, 8, ROW_W), lambda b: (b, 0, 0))],
        out_shape=[jax.ShapeDtypeStruct((bl, rb, ROW_W), F32), jax.ShapeDtypeStruct((bl, 8, ROW_W), F32)],
        compiler_params=_cp(("arbitrary",)),
    )(dhp, hp, lam_ab)


def _quarter_rows(v):
    e = v.shape[0]
    return v.reshape(e, 4, 8, SSM_P, SSM_H).transpose(0, 1, 2, 4, 3).reshape(e, 4, 8 * SSM_H, SSM_P)


def _token_state_map(vr, vi):
    return jnp.concatenate([_quarter_rows(vr), _quarter_rows(vi)], axis=-1).transpose(1, 0, 2, 3)


def ssm_build(lam_re, lam_im, log_dt, b_re, b_im, c_re, c_im, d):
    dt = jnp.exp(log_dt)[..., None]
    mag = jnp.exp(lam_re * dt)
    ang = lam_im * dt
    lr, li = mag * jnp.cos(ang), mag * jnp.sin(ang)
    den = lam_re * lam_re + lam_im * lam_im
    nr = lr - 1.0
    fr = (nr * lam_re + li * lam_im) / den
    fi = (li * lam_re - nr * lam_im) / den
    bbr = fr[..., None] * b_re - fi[..., None] * b_im
    bbi = fr[..., None] * b_im + fi[..., None] * b_re
    pr, pi = [jnp.ones_like(lr)], [jnp.zeros_like(lr)]
    for _ in range(TC):
        pr, pi = pr + [pr[-1] * lr - pi[-1] * li], pi + [pr[-1] * li + pi[-1] * lr]
    pr, pi = jnp.stack(pr), jnp.stack(pi)
    clr = c_re[None] * pr[:, :, :, None, :] - c_im[None] * pi[:, :, :, None, :]
    cli = c_re[None] * pi[:, :, :, None, :] + c_im[None] * pr[:, :, :, None, :]
    same_group = jnp.asarray(np.kron(np.eye(8), np.ones((SSM_H, SSM_H))), F32)
    ein = functools.partial(jnp.einsum, precision=HI)

    out, lag_blocks = {}, {}
    for k, name in ((0, "f"), (1, "r")):
        ar, ai = _quarter_rows(bbr[k][None])[0], _quarter_rows(bbi[k][None])[0]
        cr = clr[:TC, k].reshape(TC, 4, 8 * SSM_H, SSM_P)
        ci = cli[:TC, k].reshape(TC, 4, 8 * SSM_H, SSM_P)
        lag_blocks[k] = (ein('qap,nqbp->nqab', ar, cr) - ein('qap,nqbp->nqab', ai, ci)) * same_group
        es = [TC - 1 - s for s in range(TC)] if k == 0 else list(range(TC))
        sr = jnp.stack([pr[e, k][:, :, None] * bbr[k] - pi[e, k][:, :, None] * bbi[k] for e in es])
        si = jnp.stack([pr[e, k][:, :, None] * bbi[k] + pi[e, k][:, :, None] * bbr[k] for e in es])
        out["bs_" + name] = _token_state_map(sr, si)
        et = [t + 1 for t in range(TC)] if k == 0 else [TC - t for t in range(TC)]
        crt = jnp.stack([jnp.swapaxes(clr[e, k], 1, 2) for e in et])
        cit = jnp.stack([-jnp.swapaxes(cli[e, k], 1, 2) for e in et])
        out["cst_" + name] = _token_state_map(crt, cit)
        l8r, l8i = pr[TC, k].reshape(4, 1, QW // 2), pi[TC, k].reshape(4, 1, QW // 2)
        la = jnp.concatenate([l8r, l8r], axis=1).reshape(1, ROW_W)
        lb = jnp.concatenate([-l8i, l8i], axis=1).reshape(1, ROW_W)
        out["lam_" + name] = jnp.concatenate([la, lb, jnp.zeros((6, ROW_W), F32)], axis=0)
    skip = jnp.eye(8 * SSM_H, dtype=F32)[None] * d.reshape(4, 1, 8 * SSM_H)
    center = lag_blocks[0][0] + lag_blocks[1][0] + skip
    lags = [lag_blocks[1][n] for n in range(TC - 1, 0, -1)] + [center] + [lag_blocks[0][n] for n in range(1, TC)]
    out["lags"] = jnp.stack(lags, axis=1)
    return out


def ssm_operators(mats, tag):
    ops = {}
    ops["m"], ops["mt"] = assemble_tt(mats["lags"], "ssm_map_intra" + tag)
    for dname in ("f", "r"):
        ops["bs_" + dname], ops["bst_" + dname] = assemble_ts(mats["bs_" + dname], f"ssm_map_state_in_{dname}{tag}")
        ops["cst_" + dname], ops["cs_" + dname] = assemble_ts(mats["cst_" + dname], f"ssm_map_readout_{dname}{tag}")
    return ops


def ssm_forward(u3, mats, ops, ncr):
    bl, rb, _ = u3.shape
    u = u3.reshape(bl * rb, ROW_W)
    hps, terms = {}, [(u, ops["m"])]
    for dname, rev in (("f", False), ("r", True)):
        xs = qmm([(u, ops["bs_" + dname])], "ssm_state_in_" + dname)
        hp = chunk_scan(xs.reshape(bl, rb, ROW_W), mats["lam_" + dname], ncr, rev, "ssm_scan_" + dname)
        hps[dname] = hp.reshape(bl * rb, ROW_W)
        terms.append((hps[dname], ops["cs_" + dname]))
    return qmm(terms, "ssm_output").reshape(bl, rb, ROW_W), hps


def ssm_backward(dy3, u3, hps, mats, ops, ncr):
    bl, rb, _ = u3.shape
    u = u3.reshape(bl * rb, ROW_W)
    dyr = dy3.reshape(bl * rb, ROW_W)
    cot = {"lags": qtn_tt(u, dyr, "ssm_d_intra")}
    terms = [(dyr, ops["mt"])]
    for dname, rev in (("f", False), ("r", True)):
        dhp = qmm([(dyr, ops["cst_" + dname])], "ssm_dstate_" + dname)
        g, dl = chunk_scan_bwd(dhp.reshape(bl, rb, ROW_W), hps[dname].reshape(bl, rb, ROW_W), mats["lam_" + dname],
                               ncr, rev, "ssm_scan_bwd_" + dname)
        g = g.reshape(bl * rb, ROW_W)
        cot["lam_" + dname] = jnp.sum(dl, axis=0)
        cot["bs_" + dname] = qtn_ts(u, g, "ssm_d_state_in_" + dname)
        cot["cst_" + dname] = qtn_ts(dyr, hps[dname], "ssm_d_readout_" + dname)
        terms.append((g, ops["bst_" + dname]))
    return qmm(terms, "ssm_input_grad").reshape(bl, rb, ROW_W), cot


def mod_forward(act, w_mod, b_cols):
    nl, _, wc = w_mod.shape
    r = act.shape[0]

    def body(a_ref, w_ref, b_ref, o_ref):
        o_ref[...] = dot_nn(a_ref[...].astype(BF16), w_ref[...].astype(BF16)) + b_ref[...]

    return pl.pallas_call(
        body, name="mod_forward", grid=(nl,),
        in_specs=[pl.BlockSpec((r, D), lambda l: (0, 0)), pl.BlockSpec((None, D, wc), lambda l: (l, 0, 0)),
                  pl.BlockSpec((None, 1, wc), lambda l: (l, 0, 0))],
        out_specs=pl.BlockSpec((None, r, wc), lambda l: (l, 0, 0)),
        out_shape=jax.ShapeDtypeStruct((nl, r, wc), F32),
        compiler_params=_cp(("arbitrary",)),
    )(act, w_mod, b_cols)


def mod_backward(act, dmod, dctx, w_mod):
    nl, _, wc = w_mod.shape
    r = act.shape[0]

    def body(a_ref, d_ref, c_ref, w_ref, gw_ref, gc_ref):
        gw_ref[...] = dot_tn(a_ref[...].astype(BF16), d_ref[...].astype(BF16))
        gc_ref[...] = dot_nt(c_ref[...].astype(BF16), w_ref[...].astype(BF16))

    return pl.pallas_call(
        body, name="mod_backward", grid=(nl,),
        in_specs=[pl.BlockSpec((r, D), lambda l: (0, 0)), pl.BlockSpec((None, r, wc), lambda l: (l, 0, 0)),
                  pl.BlockSpec((None, 8, wc), lambda l: (l, 0, 0)), pl.BlockSpec((None, D, wc), lambda l: (l, 0, 0))],
        out_specs=[pl.BlockSpec((None, D, wc), lambda l: (l, 0, 0)), pl.BlockSpec((None, 8, D), lambda l: (l, 0, 0))],
        out_shape=[jax.ShapeDtypeStruct((nl, D, wc), F32), jax.ShapeDtypeStruct((nl, 8, D), F32)],
        compiler_params=_cp(("arbitrary",)),
    )(act, dmod, dctx, w_mod)


def _place():
    return lax.axis_index("x"), lax.axis_index("y"), lax.axis_index("c")


def all_gather_rows(arrs, name):
    n = len(arrs)
    rs = [a.shape[1] for a in arrs]

    def body(*refs):
        x_refs, o_refs = refs[:n], refs[n:2 * n]
        send_sems, recv_sems, local_sems = refs[2 * n:]
        x, y, c = _place()
        me, sibling = (x, y, c), (x, y, 1 - c)
        chips = [(1 - x, y), (x, 1 - y), (1 - x, 1 - y)]

        def rows(a, px, py, pc):
            return o_refs[a].at[:, pl.ds((4 * px + 2 * py + pc) * rs[a], rs[a]), :]

        def copy(a, k, block, to, src=None):
            return pltpu.make_async_remote_copy(
                src_ref=rows(a, *block) if src is None else src, dst_ref=rows(a, *block),
                send_sem=send_sems.at[a, k], recv_sem=recv_sems.at[a, k], device_id=to, device_id_type=MESH)

        mine = [pltpu.make_async_copy(x_refs[a], rows(a, *me), local_sems.at[a]) for a in range(n)]
        for cp in mine:
            cp.start()
        first = []
        for a in range(n):
            first.append(copy(a, 0, me, sibling, src=x_refs[a]))
            first += [copy(a, 1 + j, me, (*chip, c), src=x_refs[a]) for j, chip in enumerate(chips)]
        for cp in first:
            cp.start()
        passed = []
        for j, chip in enumerate(chips):
            for a in range(n):
                copy(a, 1 + j, (*chip, c), me).wait_recv()
                fwd = copy(a, 4 + j, (*chip, c), sibling)
                fwd.start()
                passed.append(fwd)
        for a in range(n):
            copy(a, 0, sibling, me).wait_recv()
            for j, chip in enumerate(chips):
                copy(a, 4 + j, (*chip, 1 - c), me).wait_recv()
        for cp in first + passed:
            cp.wait_send()
        for cp in mine:
            cp.wait()

    any_spec = pl.BlockSpec(memory_space=pl.ANY)
    return pl.pallas_call(
        body, name=name,
        in_specs=[any_spec] * n, out_specs=[any_spec] * n,
        out_shape=[jax.ShapeDtypeStruct((a.shape[0], N_DEV * a.shape[1], a.shape[2]), a.dtype) for a in arrs],
        scratch_shapes=[pltpu.SemaphoreType.DMA((n, 7)), pltpu.SemaphoreType.DMA((n, 7)), pltpu.SemaphoreType.DMA((n,))],
    )(*arrs)


def all_to_all_rows(arrs, name):
    n = len(arrs)
    rs = [a.shape[1] // N_DEV for a in arrs]
    flips = [(fx, fy, fc) for fx in (0, 1) for fy in (0, 1) for fc in (0, 1)][1:]

    def body(*refs):
        x_refs, o_refs = refs[:n], refs[n:2 * n]
        send_sems, recv_sems, local_sems = refs[2 * n:]
        x, y, c = _place()
        my_idx = 4 * x + 2 * y + c

        def block(a, idx):
            return x_refs[a].at[:, pl.ds(idx * rs[a], rs[a]), :]

        mine = [pltpu.make_async_copy(block(a, my_idx), o_refs[a].at[my_idx], local_sems.at[a]) for a in range(n)]
        for cp in mine:
            cp.start()
        sends = []
        for k, (fx, fy, fc) in enumerate(flips):
            px = 1 - x if fx else x
            py = 1 - y if fy else y
            pc = 1 - c if fc else c
            p_idx = 4 * px + 2 * py + pc
            for a in range(n):
                sends.append(pltpu.make_async_remote_copy(
                    src_ref=block(a, p_idx), dst_ref=o_refs[a].at[my_idx], send_sem=send_sems.at[a, k],
                    recv_sem=recv_sems.at[a, k], device_id=(px, py, pc), device_id_type=MESH))
        for cp in sends:
            cp.start()
        for k, (fx, fy, fc) in enumerate(flips):
            px = 1 - x if fx else x
            py = 1 - y if fy else y
            pc = 1 - c if fc else c
            p_idx = 4 * px + 2 * py + pc
            for a in range(n):
                pltpu.make_async_remote_copy(
                    src_ref=block(a, p_idx), dst_ref=o_refs[a].at[p_idx], send_sem=send_sems.at[a, k],
                    recv_sem=recv_sems.at[a, k], device_id=(px, py, pc), device_id_type=MESH).wait_recv()
        for cp in sends:
            cp.wait_send()
        for cp in mine:
            cp.wait()

    any_spec = pl.BlockSpec(memory_space=pl.ANY)
    return pl.pallas_call(
        body, name=name,
        in_specs=[any_spec] * n, out_specs=[any_spec] * n,
        out_shape=[jax.ShapeDtypeStruct((N_DEV, a.shape[0], r, a.shape[2]), a.dtype) for a, r in zip(arrs, rs)],
        scratch_shapes=[pltpu.SemaphoreType.DMA((n, 7)), pltpu.SemaphoreType.DMA((n, 7)), pltpu.SemaphoreType.DMA((n,))],
    )(*arrs)


def _peers():
    x, y, c = _place()
    out = []
    for fx in (0, 1):
        for fy in (0, 1):
            for fc in (0, 1):
                if fx or fy or fc:
                    px, py, pc = (1 - x if fx else x), (1 - y if fy else y), (1 - c if fc else c)
                    out.append(((px, py, pc), 4 * px + 2 * py + pc))
    return out, 4 * x + 2 * y + c


def _split_call(body, name, ins, n_sem_out, thru, extra_out_shape, extra_out_specs, sem_ins=(), after=None):
    hbm = pl.BlockSpec(memory_space=pltpu.HBM)
    sem = pl.BlockSpec(memory_space=pltpu.SEMAPHORE)
    n_thru = len(thru)
    tail_in = [sem] * len(sem_ins) + ([pl.BlockSpec(memory_space=pl.ANY)] if after is not None else [])
    return pl.pallas_call(
        body, name=name,
        out_shape=tuple(n_sem_out) + tuple(pltpu.HBM(a.shape, a.dtype) for a in thru) + tuple(extra_out_shape),
        in_specs=[hbm] * n_thru + tail_in,
        out_specs=(sem,) * len(n_sem_out) + (hbm,) * n_thru + tuple(extra_out_specs),
        input_output_aliases={i: i + len(n_sem_out) for i in range(n_thru)},
        compiler_params=pltpu.CompilerParams(has_side_effects=pltpu.SideEffectType.DATAFLOW_SIDE_EFFECTING),
    )(*ins, *sem_ins, *([after] if after is not None else []))


def gather_start(shards, after, name):
    n = len(shards)
    rs = [a.shape[1] for a in shards]
    lands = [lax.empty((a.shape[0], N_DEV * a.shape[1], a.shape[2]), a.dtype) for a in shards]

    def body(*refs):
        x_refs, land_refs = refs[:n], refs[n:2 * n]
        send_sems, recv_sems = refs[2 * n + 1], refs[2 * n + 2]
        peers, my_idx = _peers()
        for k, (peer, _) in enumerate(peers):
            for a in range(n):
                pltpu.make_async_remote_copy(
                    src_ref=x_refs[a], dst_ref=land_refs[a].at[:, pl.ds(my_idx * rs[a], rs[a]), :],
                    send_sem=send_sems.at[a * 7 + k], recv_sem=recv_sems.at[a * 7 + k], device_id=peer,
                    device_id_type=MESH).start()
        refs[-1][...] = jnp.zeros_like(refs[-1])

    ins = [pltpu.with_memory_space_constraint(a, pltpu.HBM) for a in list(shards) + lands]
    outs = _split_call(body, name, ins, [pltpu.SemaphoreType.DMA((n * 7,))] * 2, ins,
                       [jax.ShapeDtypeStruct((8, 128), F32)], [pl.BlockSpec(memory_space=pltpu.VMEM)], after=after)
    return outs[0], outs[1], list(outs[2:2 + n]), list(outs[2 + n:2 + 2 * n]), outs[-1]


def gather_wait(send_sems, recv_sems, shards, lands, after, name):
    n = len(shards)
    rs = [a.shape[1] for a in shards]

    def body(*refs):
        x_refs, land_refs = refs[:n], refs[n:2 * n]
        s_sems, r_sems = refs[2 * n], refs[2 * n + 1]
        peers, _ = _peers()
        for k, (peer, p_idx) in enumerate(peers):
            for a in range(n):
                copy = pltpu.make_async_remote_copy(
                    src_ref=x_refs[a], dst_ref=land_refs[a].at[:, pl.ds(p_idx * rs[a], rs[a]), :],
                    send_sem=s_sems.at[a * 7 + k], recv_sem=r_sems.at[a * 7 + k], device_id=peer, device_id_type=MESH)
                copy.wait_send()
                copy.wait_recv()

    outs = _split_call(body, name, list(shards) + list(lands), [], list(shards) + list(lands), [], [],
                       sem_ins=(send_sems, recv_sems), after=after)
    my_idx = 4 * lax.axis_index("x") + 2 * lax.axis_index("y") + lax.axis_index("c")
    return [lax.dynamic_update_slice_in_dim(z, s, my_idx * r, axis=1) for z, s, r in zip(outs[n:], outs[:n], rs)]


def scatter_start(arrs, name):
    n = len(arrs)
    rs = [a.shape[1] // N_DEV for a in arrs]
    lands = [lax.empty((N_DEV, a.shape[0], r, a.shape[2]), a.dtype) for a, r in zip(arrs, rs)]

    def body(*refs):
        x_refs, land_refs = refs[:n], refs[n:2 * n]
        send_sems, recv_sems = refs[2 * n], refs[2 * n + 1]
        token = refs[-1]
        peers, my_idx = _peers()
        for k, (peer, p_idx) in enumerate(peers):
            for a in range(n):
                pltpu.make_async_remote_copy(
                    src_ref=x_refs[a].at[:, pl.ds(p_idx * rs[a], rs[a]), :], dst_ref=land_refs[a].at[my_idx],
                    send_sem=send_sems.at[a * 7 + k], recv_sem=recv_sems.at[a * 7 + k], device_id=peer,
                    device_id_type=MESH).start()
        token[...] = jnp.zeros_like(token)

    hbm = pl.BlockSpec(memory_space=pltpu.HBM)
    sem = pl.BlockSpec(memory_space=pltpu.SEMAPHORE)
    outs = pl.pallas_call(
        body, name=name,
        out_shape=(pltpu.SemaphoreType.DMA((n * 7,)), pltpu.SemaphoreType.DMA((n * 7,)))
        + tuple(pltpu.HBM(a.shape, a.dtype) for a in arrs) + tuple(pltpu.HBM(z.shape, z.dtype) for z in lands)
        + (jax.ShapeDtypeStruct((8, 128), F32),),
        in_specs=[hbm] * (2 * n),
        out_specs=(sem, sem) + (hbm,) * (2 * n) + (pl.BlockSpec(memory_space=pltpu.VMEM),),
        input_output_aliases={i: i + 2 for i in range(2 * n)},
        compiler_params=pltpu.CompilerParams(has_side_effects=pltpu.SideEffectType.DATAFLOW_SIDE_EFFECTING),
    )(*[pltpu.with_memory_space_constraint(a, pltpu.HBM) for a in arrs],
      *[pltpu.with_memory_space_constraint(z, pltpu.HBM) for z in lands])
    return outs[0], outs[1], list(outs[2:2 + n]), list(outs[2 + n:2 + 2 * n]), outs[-1]


def scatter_wait(send_sems, recv_sems, arrs, lands, after, name):
    n = len(arrs)
    rs = [a.shape[1] // N_DEV for a in arrs]

    def body(*refs):
        x_refs, land_refs = refs[:n], refs[n:2 * n]
        s_sems, r_sems = refs[2 * n], refs[2 * n + 1]
        peers, my_idx = _peers()
        for k, (peer, p_idx) in enumerate(peers):
            for a in range(n):
                copy = pltpu.make_async_remote_copy(
                    src_ref=x_refs[a].at[:, pl.ds(p_idx * rs[a], rs[a]), :], dst_ref=land_refs[a].at[p_idx],
                    send_sem=s_sems.at[a * 7 + k], recv_sem=r_sems.at[a * 7 + k], device_id=peer, device_id_type=MESH)
                copy.wait_send()
                copy.wait_recv()

    hbm = pl.BlockSpec(memory_space=pltpu.HBM)
    sem = pl.BlockSpec(memory_space=pltpu.SEMAPHORE)
    outs = pl.pallas_call(
        body, name=name,
        out_shape=tuple(pltpu.HBM(a.shape, a.dtype) for a in arrs) + tuple(pltpu.HBM(z.shape, z.dtype) for z in lands),
        in_specs=[hbm] * (2 * n) + [sem, sem, pl.BlockSpec(memory_space=pl.ANY)],
        out_specs=(hbm,) * (2 * n),
        input_output_aliases={i: i for i in range(2 * n)},
        compiler_params=pltpu.CompilerParams(has_side_effects=pltpu.SideEffectType.DATAFLOW_SIDE_EFFECTING),
    )(*arrs, *lands, send_sems, recv_sems, after)
    return list(outs[:n]), list(outs[n:])


def _row_tile(rows, cap):
    best = None
    for t in range(16, min(rows, cap) + 1, 16):
        if rows % t == 0:
            best = t
    return rows if best is None else best


def adamw(w, gparts, m, v, name):
    per_layer = isinstance(gparts, (list, tuple))
    glist = list(gparts) if per_layer else [gparts]
    n, _, ra, cb = glist[0].shape
    nl = w.shape[0]
    ng = len(glist)
    ta = _row_tile(ra, max(8, (1 << 19) // (cb * n)))

    def slot_sum(g_ref):
        g = g_ref[0].astype(F32)
        for p in range(1, n):
            g = g + g_ref[p].astype(F32)
        return g

    def body(*refs):
        w_ref, g_refs = refs[0], refs[1:1 + ng]
        m_ref, v_ref, go_ref, d_ref, mo_ref, vo_ref = refs[1 + ng:]
        g = slot_sum(g_refs[0])
        for layer in range(1, ng):
            g = jnp.where(pl.program_id(0) == layer, slot_sum(g_refs[layer]), g)
        mn = ADAM_B1 * m_ref[...] + (1.0 - ADAM_B1) * g
        vn = ADAM_B2 * v_ref[...] + (1.0 - ADAM_B2) * jnp.square(g)
        m_hat = mn / (1.0 - ADAM_B1 ** ADAM_STEP)
        v_hat = vn / (1.0 - ADAM_B2 ** ADAM_STEP)
        go_ref[...] = g
        d_ref[...] = -ADAM_LR * (m_hat / (jnp.sqrt(v_hat) + ADAM_EPS) + ADAM_WD * w_ref[...])
        mo_ref[...] = mn
        vo_ref[...] = vn

    blk = pl.BlockSpec((None, ta, cb), lambda l, i: (l, i, 0))
    if per_layer:
        gblk = pl.BlockSpec((n, None, ta, cb), lambda l, i: (0, 0, i, 0))
    else:
        gblk = pl.BlockSpec((n, None, ta, cb), lambda l, i: (0, l, i, 0))
    shp = jax.ShapeDtypeStruct((nl, ra, cb), F32)
    return pl.pallas_call(
        body, name=name, grid=(nl, ra // ta),
        in_specs=[blk] + [gblk] * ng + [blk, blk], out_specs=[blk] * 4, out_shape=[shp] * 4,
        compiler_params=_cp(("arbitrary", "arbitrary")),
    )(w, *glist, m, v)


def _sincos_2d(rows, cols, dim):
    quarter = dim // 4
    omega = 1.0 / (10000.0 ** (jnp.arange(quarter, dtype=F32) / quarter))
    r = jnp.arange(rows, dtype=F32)[:, None] * omega
    cc = jnp.arange(cols, dtype=F32)[:, None] * omega
    er = jnp.concatenate([jnp.sin(r), jnp.cos(r)], axis=-1)
    ec = jnp.concatenate([jnp.sin(cc), jnp.cos(cc)], axis=-1)
    pe = jnp.concatenate([jnp.broadcast_to(er[:, None, :], (rows, cols, dim // 2)),
                          jnp.broadcast_to(ec[None, :, :], (rows, cols, dim // 2))], axis=-1)
    return pe.reshape(rows * cols, dim)


def _pool_constants():
    nw = len(POOL_WINDOWS)
    band = np.zeros((2, nw, TT, TT), np.float32)
    icnt = np.zeros((2, TT, C_W), np.float32)
    for kind, n in ((0, TT), (1, GRID_W)):
        for i, w in enumerate(POOL_WINDOWS):
            for t in range(TT):
                base, tl = (t // n) * n, t % n
                lo = min(max(tl - w // 2, 0), n)
                hi = min(max(tl - w // 2 + w, 0), n)
                band[kind, i, t, base + lo:base + hi] = 1.0
                icnt[kind, t, i * (C_W // nw):(i + 1) * (C_W // nw)] = 1.0 / (hi - lo)
    return jnp.asarray(band, BF16), jnp.asarray(icnt, F32)


def _block_diag(blocks):
    n, a, _ = blocks.shape
    return jnp.einsum('gab,gh->gahb', blocks, jnp.eye(n, dtype=F32), precision=HI).reshape(n * a, n * a)


def _block_diag_parts(mat, n):
    a = mat.shape[0] // n
    m4 = mat.reshape(n, a, n, a)
    return jnp.stack([m4[g, :, g, :] for g in range(n)])


_SMALL = ("c_ctx", "b_mod", "norm_mix_pre", "norm_mix_post", "norm_ffn_pre", "norm_ffn_post", "sgu_w", "sgu_b",
          "ssm_lam_re", "ssm_lam_im", "ssm_log_dt", "ssm_b_re", "ssm_b_im", "ssm_c_re", "ssm_c_im", "ssm_d",
          "glu_b", "pool_w", "pool_scale")
_WEIGHTS = ("c_ctx", "w_mod", "b_mod", "norm_mix_pre", "norm_mix_post", "norm_ffn_pre", "norm_ffn_post", "w_in", "w_out",
            "sgu_w", "sgu_b", "ssm_lam_re", "ssm_lam_im", "ssm_log_dt", "ssm_b_re", "ssm_b_im", "ssm_c_re", "ssm_c_im",
            "ssm_d", "glu_w", "glu_b", "pool_w", "pool_scale", "ffn_w_gate", "ffn_w_up", "ffn_w_down")


def _pack_rows(a):
    flat = a.reshape(-1)
    rows = -(-flat.shape[0] // D)
    rows8 = -(-rows // 8) * 8
    return jnp.pad(flat, (0, rows8 * D - flat.shape[0])).reshape(rows8, D)


def _pack(tree):
    packed = jnp.concatenate([_pack_rows(tree[k]) for k in _SMALL], axis=0)
    return jnp.pad(packed, ((0, -packed.shape[0] % 64), (0, 0)))


def _unpack(packed, like):
    out, at = {}, 0
    for k in _SMALL:
        size = int(np.prod(like[k].shape))
        rows8 = -(-(-(-size // D)) // 8) * 8
        out[k] = packed[at:at + rows8].reshape(-1)[:size].reshape(like[k].shape)
        at += rows8
    return out


def kernel(x, c, ctx, c_ctx, w_mod, b_mod, norm_mix_pre, norm_mix_post, norm_ffn_pre, norm_ffn_post, w_in, w_out, sgu_w, sgu_b, ssm_lam_re, ssm_lam_im, ssm_log_dt, ssm_b_re, ssm_b_im, ssm_c_re, ssm_c_im, ssm_d, glu_w, glu_b, pool_w, pool_scale, ffn_w_gate, ffn_w_up, ffn_w_down, loss_target, m_c_ctx, m_w_mod, m_b_mod, m_norm_mix_pre, m_norm_mix_post, m_norm_ffn_pre, m_norm_ffn_post, m_w_in, m_w_out, m_sgu_w, m_sgu_b, m_ssm_lam_re, m_ssm_lam_im, m_ssm_log_dt, m_ssm_b_re, m_ssm_b_im, m_ssm_c_re, m_ssm_c_im, m_ssm_d, m_glu_w, m_glu_b, m_pool_w, m_pool_scale, m_ffn_w_gate, m_ffn_w_up, m_ffn_w_down, v_c_ctx, v_w_mod, v_b_mod, v_norm_mix_pre, v_norm_mix_post, v_norm_ffn_pre, v_norm_ffn_post, v_w_in, v_w_out, v_sgu_w, v_sgu_b, v_ssm_lam_re, v_ssm_lam_im, v_ssm_log_dt, v_ssm_b_re, v_ssm_b_im, v_ssm_c_re, v_ssm_c_im, v_ssm_d, v_glu_w, v_glu_b, v_pool_w, v_pool_scale, v_ffn_w_gate, v_ffn_w_up, v_ffn_w_down):
    wts = dict(c_ctx=c_ctx, w_mod=w_mod, b_mod=b_mod, norm_mix_pre=norm_mix_pre, norm_mix_post=norm_mix_post,
               norm_ffn_pre=norm_ffn_pre, norm_ffn_post=norm_ffn_post, w_in=w_in, w_out=w_out, sgu_w=sgu_w, sgu_b=sgu_b,
               ssm_lam_re=ssm_lam_re, ssm_lam_im=ssm_lam_im, ssm_log_dt=ssm_log_dt, ssm_b_re=ssm_b_re, ssm_b_im=ssm_b_im,
               ssm_c_re=ssm_c_re, ssm_c_im=ssm_c_im, ssm_d=ssm_d, glu_w=glu_w, glu_b=glu_b, pool_w=pool_w,
               pool_scale=pool_scale, ffn_w_gate=ffn_w_gate, ffn_w_up=ffn_w_up, ffn_w_down=ffn_w_down)
    mom_m = dict(c_ctx=m_c_ctx, w_mod=m_w_mod, b_mod=m_b_mod, norm_mix_pre=m_norm_mix_pre, norm_mix_post=m_norm_mix_post,
                 norm_ffn_pre=m_norm_ffn_pre, norm_ffn_post=m_norm_ffn_post, w_in=m_w_in, w_out=m_w_out, sgu_w=m_sgu_w,
                 sgu_b=m_sgu_b, ssm_lam_re=m_ssm_lam_re, ssm_lam_im=m_ssm_lam_im, ssm_log_dt=m_ssm_log_dt,
                 ssm_b_re=m_ssm_b_re, ssm_b_im=m_ssm_b_im, ssm_c_re=m_ssm_c_re, ssm_c_im=m_ssm_c_im, ssm_d=m_ssm_d,
                 glu_w=m_glu_w, glu_b=m_glu_b, pool_w=m_pool_w, pool_scale=m_pool_scale, ffn_w_gate=m_ffn_w_gate,
                 ffn_w_up=m_ffn_w_up, ffn_w_down=m_ffn_w_down)
    mom_v = dict(c_ctx=v_c_ctx, w_mod=v_w_mod, b_mod=v_b_mod, norm_mix_pre=v_norm_mix_pre, norm_mix_post=v_norm_mix_post,
                 norm_ffn_pre=v_norm_ffn_pre, norm_ffn_post=v_norm_ffn_post, w_in=v_w_in, w_out=v_w_out, sgu_w=v_sgu_w,
                 sgu_b=v_sgu_b, ssm_lam_re=v_ssm_lam_re, ssm_lam_im=v_ssm_lam_im, ssm_log_dt=v_ssm_log_dt,
                 ssm_b_re=v_ssm_b_re, ssm_b_im=v_ssm_b_im, ssm_c_re=v_ssm_c_re, ssm_c_im=v_ssm_c_im, ssm_d=v_ssm_d,
                 glu_w=v_glu_w, glu_b=v_glu_b, pool_w=v_pool_w, pool_scale=v_pool_scale, ffn_w_gate=v_ffn_w_gate,
                 ffn_w_up=v_ffn_w_up, ffn_w_down=v_ffn_w_down)

    bl, seq, _ = x.shape
    n_ctx = ctx.shape[1]
    assert n_ctx == TT and seq % TT == 0 and seq % GRID_W == 0
    depth = w_in.shape[0]
    nc = n_ctx // TT
    ncr = n_ctx // TC
    s_all = n_ctx + seq
    nt = s_all // TT
    t_all = bl * s_all
    n_batch = bl * N_DEV
    my_idx = 4 * lax.axis_index("x") + 2 * lax.axis_index("y") + lax.axis_index("c")
    wc = w_mod.shape[2]

    c_rows = jnp.pad(c, ((0, 8 - bl), (0, 0))) if bl < 8 else c
    rc = c_rows.shape[0]
    (c_all,) = all_gather_rows([c_rows[None]], "gather_c")
    c_all = c_all[0].reshape(N_DEV, rc, D)[:, :bl].reshape(n_batch, D)
    r_act = -(-(n_batch + 1) // 16) * 16
    pre_act = jnp.concatenate([c_all, c_ctx[None, :], jnp.zeros((r_act - n_batch - 1, D), F32)], axis=0)
    act = jax.nn.silu(pre_act)
    b_cols = lax.dynamic_slice_in_dim(b_mod, my_idx * wc, wc, axis=1)[:, None, :]
    mod_cols = mod_forward(act, w_mod, b_cols)
    (mod_all,) = all_gather_rows([mod_cols], "gather_mod")
    mod_all = mod_all.reshape(depth, N_DEV, r_act, wc).transpose(0, 2, 1, 3).reshape(depth, r_act, 6, D)
    mod_lat = lax.dynamic_slice_in_dim(mod_all, my_idx * bl, bl, axis=1)
    mod_ctx = jnp.broadcast_to(mod_all[:, n_batch:n_batch + 1], (depth, bl, 6, D))
    mods = jnp.pad(jnp.stack([mod_ctx, mod_lat], axis=2), ((0, 0), (0, 0), (0, 0), (0, 2), (0, 0)))

    tr = lambda a: jnp.swapaxes(a, 1, 2).astype(BF16)
    shards = dict(w_in=tr(w_in), w_out=w_out.astype(BF16), glu_w=glu_w.astype(BF16), gate=tr(ffn_w_gate),
                  up=tr(ffn_w_up), down=ffn_w_down.astype(BF16))
    mix_keys, ffn_keys = ("w_in", "w_out", "glu_w"), ("gate", "up", "down")
    layer = lambda k, i: shards[k][i:i + 1]
    full = [dict() for _ in range(depth)]
    for k, g in zip(mix_keys, all_gather_rows([layer(k, 0) for k in mix_keys], "gather_mix_weights_0")):
        full[0][k] = g[0]
    first_done = mods[0, 0, 0, 0:1, 0:128] + full[0]["w_in"][0:1, 0:128].astype(F32)
    weights_in_flight = {0: (ffn_keys, gather_start([layer(k, 0) for k in ffn_keys], first_done, "gather_start_ffn_0"))}
    for i in range(1, depth):
        prev_token = weights_in_flight[i - 1][1][4]
        weights_in_flight[i] = (mix_keys + ffn_keys, gather_start([layer(k, i) for k in mix_keys + ffn_keys], prev_token,
                                                                  f"gather_start_layer_{i}"))
    start_token = sum(fl[1][4][0:1, 0:1] for fl in weights_in_flight.values())

    def land_weights(i, after, name):
        keys, (send_sems, recv_sems, sent, lands, _) = weights_in_flight[i]
        for k, g in zip(keys, gather_wait(send_sems, recv_sems, sent, lands, after, name)):
            full[i][k] = g[0]

    band, icnt = _pool_constants()
    seg_p = jnp.asarray(np.kron(np.eye(A_HEADS), np.full((A_W // A_HEADS,) * 2, A_HEADS / A_W)), BF16)
    pe = _sincos_2d(seq // GRID_W, GRID_W, D)
    xs = embed_tokens(x, ctx, pe)

    saved = []
    for i in range(depth):
        mats, ssm_vjp = jax.vjp(ssm_build, ssm_lam_re[i], ssm_lam_im[i], ssm_log_dt[i], ssm_b_re[i], ssm_b_im[i],
                                ssm_c_re[i], ssm_c_im[i], ssm_d[i])
        if i > 0:
            land_weights(i, xs, f"gather_wait_layer_{i}")
        cst = dict(sw=sgu_w[i].astype(BF16),
                   sbias=jnp.repeat(sgu_b[i].T, A_W // A_HEADS, axis=1),
                   seg_p=seg_p, band=band, icnt=icnt, wbd=_block_diag(pool_w[i]).astype(BF16),
                   pscale=pool_scale[i][None, :], glu_w=full[i]["glu_w"], glu_b=glu_b[i][None, :], w_out=full[i]["w_out"],
                   n2=norm_mix_post[i][None, :])
        n1, n3, n4 = norm_mix_pre[i][None, :], norm_ffn_pre[i][None, :], norm_ffn_post[i][None, :]
        if i == 0:
            n1 = n1 + start_token
        za, zu, zp = pre_mix(xs, mods[i], n1, full[i]["w_in"], nc)
        ops = ssm_operators(mats, f"_{i}")
        ys, hps = ssm_forward(zu, mats, ops, ncr)
        x1, m_pre = post_mix(xs, za, zp, ys, mods[i], cst, nc)
        if i == 0:
            land_weights(0, x1, "gather_wait_ffn_0")
        x2, f_pre, gate_b, up_b = ffn_fwd(x1, mods[i], n3, n4, full[i]["gate"], full[i]["up"], full[i]["down"], n_ctx)
        saved.append(dict(xs=xs, za=za, zu=zu, zp=zp, ys=ys, hps=hps, x1=x1, m=m_pre, f=f_pre, gate=gate_b, up=up_b,
                          cst=cst, mats=mats,
                          ops=ops, ssm_vjp=ssm_vjp, n1=n1, n3=n3, n4=n4))
        xs = x2

    dx, loss_parts = loss_head(xs, loss_target, nc)
    loss = lax.psum(jnp.sum(loss_parts[:, :, 0, 0]), ("x", "y", "c"))

    grads = {k: [None] * depth for k in _WEIGHTS}
    big = {k: [None] * depth for k in ("w_in", "w_out", "glu_w", "ffn_w_gate", "ffn_w_up", "ffn_w_down")}
    dmods = [None] * depth
    scatter_groups = (("ffn_w_gate", "ffn_w_up", "ffn_w_down"), ("w_out", "glu_w"), ("w_in",))
    in_flight = []

    def send_grads(i, group):
        flight = scatter_start([big[k][i][None] for k in scatter_groups[group]], f"scatter_start_{i}_{group}")
        in_flight.append((i, group, flight))
        return flight[4][0:1, 0:1]

    flat = lambda a: a.reshape(t_all, a.shape[-1])
    for i in reversed(range(depth)):
        sv = saved[i]
        dx1, h2, df, act_b, dgate, dup, st_f = ffn_bwd(dx, sv["x1"], sv["f"], sv["gate"], sv["up"], mods[i], sv["n3"],
                                                       sv["n4"], full[i]["gate"], full[i]["up"], full[i]["down"], n_ctx)
        big["ffn_w_gate"][i] = tn_matmul(flat(dgate), flat(h2), f"grad_ffn_gate_{i}")
        big["ffn_w_up"][i] = tn_matmul(flat(dup), flat(h2), f"grad_ffn_up_{i}")
        big["ffn_w_down"][i] = tn_matmul(flat(act_b), flat(df), f"grad_ffn_down_{i}")
        cst_i = dict(sv["cst"], n2=sv["cst"]["n2"] + send_grads(i, 0))
        dza, dzp, dys, cat, dm, gg, dr, st_m, dsw, dsb, dwbd = post_mix_bwd(dx1, sv["m"], sv["za"], sv["zp"], sv["ys"],
                                                                            mods[i], cst_i, nc)
        big["w_out"][i] = tn_matmul(flat(cat), flat(dm), f"grad_w_out_{i}")
        big["glu_w"][i] = tn_matmul(flat(gg), flat(dr), f"grad_glu_w_{i}")
        mats_i = dict(sv["mats"], lam_f=sv["mats"]["lam_f"] + send_grads(i, 1))
        dzu, cot = ssm_backward(dys, sv["zu"], sv["hps"], mats_i, sv["ops"], ncr)
        (grads["ssm_lam_re"][i], grads["ssm_lam_im"][i], grads["ssm_log_dt"][i], grads["ssm_b_re"][i],
         grads["ssm_b_im"][i], grads["ssm_c_re"][i], grads["ssm_c_im"][i], grads["ssm_d"][i]) = sv["ssm_vjp"](cot)
        dx, h1, dz, st_p = pre_mix_bwd(dza, dzu, dzp, sv["xs"], dx1, mods[i], sv["n1"], full[i]["w_in"], nc)
        big["w_in"][i] = tn_matmul(flat(dz), flat(h1), f"grad_w_in_{i}")

        tiles = lambda st, row: st[:, :, row, :]
        allsum = lambda st, row: jnp.sum(tiles(st, row), axis=(0, 1))
        grads["norm_mix_pre"][i] = allsum(st_p, 2)
        grads["norm_mix_post"][i] = allsum(st_m, 1)
        grads["norm_ffn_pre"][i] = allsum(st_f, 3)
        grads["norm_ffn_post"][i] = allsum(st_f, 4)
        misc = allsum(st_m, 2)
        grads["glu_b"][i] = misc[:B_W]
        grads["pool_scale"][i] = misc[B_W:B_W + C_W]
        grads["sgu_w"][i] = dsw
        grads["sgu_b"][i] = jnp.sum(dsb.reshape(CHUNK, A_HEADS, A_W // A_HEADS), axis=2).T
        grads["pool_w"][i] = _block_diag_parts(dwbd, len(POOL_WINDOWS))
        mix = (tiles(st_p, 0), tiles(st_p, 1), tiles(st_m, 0))
        d_lat = jnp.stack([jnp.sum(t[:, nc:], axis=1) for t in mix]
                          + [jnp.sum(tiles(st_f, r), axis=1) for r in (0, 1, 2)], axis=1).reshape(bl, 6 * D)
        d_ctx = jnp.concatenate([jnp.sum(t[:, :nc], axis=(0, 1)) for t in mix]
                                + [allsum(st_f, r) for r in (5, 6, 7)]).reshape(1, 6 * D)
        dmods[i] = jnp.concatenate([d_lat, d_ctx, jnp.zeros((8 - (bl + 1) % 8 if (bl + 1) % 8 else 0, 6 * D), F32)],
                                   axis=0)
        token = send_grads(i, 2)
        if i > 0:
            saved[i - 1]["n3"] = saved[i - 1]["n3"] + token
        else:
            dmods[i] = dmods[i] + token
    grad_x = dx[:, n_ctx:, :]

    dmod_local = jnp.stack(dmods)
    rd = dmod_local.shape[1]
    (dmod_all,) = all_gather_rows([dmod_local], "gather_dmod")
    dmod_cols = lax.dynamic_slice_in_dim(dmod_all, my_idx * wc, wc, axis=2).reshape(depth, N_DEV, rd, wc)
    d_lat_all = dmod_cols[:, :, :bl].reshape(depth, n_batch, wc)
    d_ctx_all = dmod_cols[:, 0, bl]
    for p in range(1, N_DEV):
        d_ctx_all = d_ctx_all + dmod_cols[:, p, bl]
    dmod_rows = jnp.concatenate([d_lat_all, d_ctx_all[:, None, :], jnp.zeros((depth, r_act - n_batch - 1, wc), F32)],
                                axis=1)
    dctx_rows = jnp.pad(d_ctx_all[:, None, :], ((0, 0), (0, 7), (0, 0)))
    g_w_mod, dact_ctx = mod_backward(act, dmod_rows, dctx_rows, w_mod)
    sig_c = jax.nn.sigmoid(c_ctx)
    dsilu_c = sig_c * (1.0 + c_ctx * (1.0 - sig_c))
    small_g = {k: (jnp.stack(grads[k]) if grads[k][0] is not None else None) for k in _SMALL}
    small_g["c_ctx"] = jnp.sum(dact_ctx[:, 0, :], axis=0) * dsilu_c
    small_g["b_mod"] = jnp.stack([jnp.sum(dmods[i][:bl + 1], axis=0) for i in range(depth)])

    packed_g = _pack(small_g).astype(BF16)
    rows_s = packed_g.shape[0]
    (gathered,) = all_gather_rows([packed_g[None]], "gather_small_grads")
    res = {k: [None] * 4 for k in _WEIGHTS}

    landed = {}
    for i, group, (send_sems, recv_sems, arrs_thru, lands_thru, _) in in_flight:
        sent, lands = scatter_wait(send_sems, recv_sems, arrs_thru, lands_thru, gathered, f"scatter_wait_{i}_{group}")
        for k, a, z in zip(scatter_groups[group], sent, lands):
            r = a.shape[1] // N_DEV
            own = lax.dynamic_slice_in_dim(a, my_idx * r, r, axis=1)[None]
            landed[k, i] = lax.dynamic_update_slice_in_dim(z, own, my_idx, axis=0)
    for k in big:
        transposed = k in ("w_in", "ffn_w_gate", "ffn_w_up")
        view = (lambda a: jnp.swapaxes(a, 1, 2)) if transposed else (lambda a: a)
        o4 = adamw(view(wts[k]), [landed[k, i] for i in range(depth)], view(mom_m[k]), view(mom_v[k]), "adamw_" + k)
        res[k] = [view(o) for o in o4]
    res["w_mod"] = list(adamw(w_mod, g_w_mod[None], m_w_mod, v_w_mod, "adamw_w_mod"))

    small_w = {k: wts[k] for k in _SMALL}
    outs = adamw(_pack(small_w)[None], gathered.reshape(N_DEV, 1, rows_s, D), _pack({k: mom_m[k] for k in _SMALL})[None],
                 _pack({k: mom_v[k] for k in _SMALL})[None], "adamw_replicated")
    for slot, packed in enumerate(outs):
        un = _unpack(packed[0], small_w)
        for k in _SMALL:
            res[k][slot] = un[k]

    return (loss, grad_x, *[res[k][0] for k in _WEIGHTS], *[res[k][1] for k in _WEIGHTS],
            *[res[k][2] for k in _WEIGHTS], *[res[k][3] for k in _WEIGHTS])
```
